```python
import jax, jax.numpy as jnp
from jax import lax
import numpy as np

D_MODEL = 1024
BATCH = 16
SEQ = 4096
DEPTH = 2

HEAD_DIM = 64
ATTN_WIDTH = D_MODEL // 2
N_Q_HEADS = ATTN_WIDTH // HEAD_DIM
N_KV_HEADS = max(N_Q_HEADS // 4, 1)
Q_PER_KV = N_Q_HEADS // N_KV_HEADS
KV_WIDTH = N_KV_HEADS * HEAD_DIM
WINDOW = 128
BLOCK = 128

CHUNK = 128
SGU_GROUP_DIM = 128
SGU_WIDTH = D_MODEL // 2
SGU_GROUPS = SGU_WIDTH // SGU_GROUP_DIM

ALPHA = (2.0 * DEPTH) ** 0.25
BETA = (8.0 * DEPTH) ** -0.25
LN_EPS = 1e-5

SPLITS = (ATTN_WIDTH, KV_WIDTH, KV_WIDTH, ATTN_WIDTH,
          SGU_WIDTH, SGU_WIDTH, SGU_WIDTH, D_MODEL, D_MODEL)
IN_COLS = sum(SPLITS)

kernel_name = "hybrid_swa_sink_sgu_gated_deepnorm"


def _layernorm(x, g, b):
    xf = x.astype(jnp.float32)
    mu = jnp.mean(xf, axis=-1, keepdims=True)
    var = jnp.mean(jnp.square(xf - mu), axis=-1, keepdims=True)
    y = (xf - mu) * lax.rsqrt(var + LN_EPS) * g.astype(jnp.float32) + b.astype(jnp.float32)
    return y.astype(x.dtype)


def _split_cols(h):
    parts, start = [], 0
    for w in SPLITS:
        parts.append(h[..., start:start + w])
        start += w
    return parts


def _swa_sinks(q, k, v, sinks):
    B, S = q.shape[0], q.shape[1]
    nb = S // BLOCK
    qb = q.reshape(B, nb, BLOCK, N_KV_HEADS, Q_PER_KV, HEAD_DIM)
    kb = k.reshape(B, nb, BLOCK, N_KV_HEADS, HEAD_DIM)
    vb = v.reshape(B, nb, BLOCK, N_KV_HEADS, HEAD_DIM)
    zpad = jnp.zeros_like(kb[:, :1])
    kk = jnp.concatenate([jnp.concatenate([zpad, kb[:, :-1]], axis=1), kb], axis=2)
    vv = jnp.concatenate([jnp.concatenate([zpad, vb[:, :-1]], axis=1), vb], axis=2)
    scale = HEAD_DIM ** -0.5
    scores = jnp.einsum('bnqhgd,bnkhd->bnhgqk', qb, kk).astype(jnp.float32) * scale
    qpos = jnp.arange(BLOCK)[:, None] + BLOCK
    kpos = jnp.arange(2 * BLOCK)[None, :]
    band = (kpos <= qpos) & (kpos > qpos - WINDOW)
    blk = jnp.arange(nb)[:, None, None]
    valid = band[None] & ((blk > 0) | (kpos[None] >= BLOCK))
    neg = jnp.finfo(jnp.float32).min
    scores = jnp.where(valid[None, :, None, None], scores, neg)
    sink = sinks.astype(jnp.float32).reshape(N_KV_HEADS, Q_PER_KV)[None, None, :, :, None, None]
    m = jnp.maximum(jnp.max(scores, axis=-1, keepdims=True), sink)
    p = jnp.exp(scores - m)
    denom = jnp.sum(p, axis=-1, keepdims=True) + jnp.exp(sink - m)
    probs = (p / denom).astype(vv.dtype)
    out = jnp.einsum('bnhgqk,bnkhd->bnqhgd', probs, vv)
    return out.reshape(B, S, ATTN_WIDTH)


def _chunked_sgu(u, v, vn_g, vn_b, w_s, b_s):
    B, S = v.shape[0], v.shape[1]
    nc = S // CHUNK
    v = _layernorm(v, vn_g, vn_b)
    vc = v.reshape(B, nc, CHUNK, SGU_GROUPS, SGU_GROUP_DIM)
    tril = jnp.tril(jnp.ones((CHUNK, CHUNK), dtype=w_s.dtype))
    w = w_s * tril[None]
    mixed = jnp.einsum('gts,bcsgd->bctgd', w, vc) + b_s.T[None, None, :, :, None]
    return u * mixed.reshape(B, S, SGU_WIDTH)


def _fwd_setup_inputs(seed: int = 0) -> dict:
    key = jax.random.key(seed)
    ks = jax.random.split(key, 16)
    L, D = DEPTH, D_MODEL
    x = jax.random.normal(ks[0], (BATCH, SEQ, D), jnp.float32)
    ln_in_g = 1.0 + 0.05 * jax.random.normal(ks[1], (D,), jnp.float32)
    ln_in_b = 0.02 * jax.random.normal(ks[2], (D,), jnp.float32)
    col_scale = jnp.concatenate([
        jnp.ones((ATTN_WIDTH + KV_WIDTH,), jnp.float32),
        jnp.full((KV_WIDTH,), BETA, jnp.float32),
        jnp.ones((ATTN_WIDTH,), jnp.float32),
        jnp.full((SGU_WIDTH,), BETA, jnp.float32),
        jnp.ones((2 * SGU_WIDTH + 2 * D,), jnp.float32)])
    w_in = jax.random.normal(ks[3], (L, D, IN_COLS), jnp.float32) * (D ** -0.5) * col_scale
    b_in = 0.02 * jax.random.normal(ks[4], (L, IN_COLS), jnp.float32)
    sinks = 0.5 * jax.random.normal(ks[5], (L, N_Q_HEADS), jnp.float32)
    vn_g = 1.0 + 0.05 * jax.random.normal(ks[6], (L, SGU_WIDTH), jnp.float32)
    vn_b = 0.02 * jax.random.normal(ks[7], (L, SGU_WIDTH), jnp.float32)
    w_s = jax.random.normal(ks[8], (L, SGU_GROUPS, CHUNK, CHUNK), jnp.float32) * (CHUNK ** -0.5)
    b_s = 1.0 + 0.1 * jax.random.normal(ks[9], (L, SGU_GROUPS, CHUNK), jnp.float32)
    p_a = jax.random.normal(ks[10], (L, ATTN_WIDTH, D), jnp.float32) * (ATTN_WIDTH ** -0.5) * BETA
    p_b = jax.random.normal(ks[11], (L, SGU_WIDTH, D), jnp.float32) * (SGU_WIDTH ** -0.5) * BETA
    w_out = jax.random.normal(ks[12], (L, D, D), jnp.float32) * (D ** -0.5) * BETA
    b_out = 0.02 * jax.random.normal(ks[13], (L, D), jnp.float32)
    ln_g = 1.0 + 0.05 * jax.random.normal(ks[14], (L, D), jnp.float32)
    ln_b = 0.02 * jax.random.normal(ks[15], (L, D), jnp.float32)
    return {"x": x, "ln_in_g": ln_in_g, "ln_in_b": ln_in_b, "w_in": w_in, "b_in": b_in,
            "sinks": sinks, "vn_g": vn_g, "vn_b": vn_b, "w_s": w_s, "b_s": b_s,
            "p_a": p_a, "p_b": p_b, "w_out": w_out, "b_out": b_out,
            "ln_g": ln_g, "ln_b": ln_b}


def _fwd_reference(x, ln_in_g, ln_in_b, w_in, b_in, sinks, vn_g, vn_b, w_s, b_s,
              p_a, p_b, w_out, b_out, ln_g, ln_b):
    x = _layernorm(x, ln_in_g, ln_in_b)
    for l in range(DEPTH):
        h = jnp.einsum('bsd,dc->bsc', x, w_in[l]) + b_in[l]
        q, k, v, g_a, u_b, v_b, g_b, r_a, r_b = _split_cols(h)
        y_a = _swa_sinks(q, k, v, sinks[l]) * jax.nn.silu(g_a)
        y_b = _chunked_sgu(jax.nn.gelu(u_b), jax.nn.gelu(v_b), vn_g[l], vn_b[l],
                           w_s[l], b_s[l]) * jax.nn.silu(g_b)
        merged = (jax.nn.sigmoid(r_a) * jnp.einsum('bsc,cd->bsd', y_a, p_a[l])
                  + jax.nn.sigmoid(r_b) * jnp.einsum('bsc,cd->bsd', y_b, p_b[l]))
        out = jnp.einsum('bsd,de->bse', merged, w_out[l]) + b_out[l]
        x = _layernorm(ALPHA * x + out, ln_g[l], ln_b[l])
    return x


import jax as _jax
import jax.numpy as _jnp

TWIN_FORMAT = 'train_step'
FWD_PARAMS = ['x', 'ln_in_g', 'ln_in_b', 'w_in', 'b_in', 'sinks', 'vn_g', 'vn_b', 'w_s', 'b_s', 'p_a', 'p_b', 'w_out', 'b_out', 'ln_g', 'ln_b']
TWIN_WEIGHTS = ['ln_in_g', 'ln_in_b', 'w_in', 'b_in', 'sinks', 'vn_g', 'vn_b', 'w_s', 'b_s', 'p_a', 'p_b', 'w_out', 'b_out', 'ln_g', 'ln_b']
TWIN_DIFF_INPUT = 'x'
TWIN_INPUTS = ['x', 'ln_in_g', 'ln_in_b', 'w_in', 'b_in', 'sinks', 'vn_g', 'vn_b', 'w_s', 'b_s', 'p_a', 'p_b', 'w_out', 'b_out', 'ln_g', 'ln_b', 'loss_target', 'm_ln_in_g', 'm_ln_in_b', 'm_w_in', 'm_b_in', 'm_sinks', 'm_vn_g', 'm_vn_b', 'm_w_s', 'm_b_s', 'm_p_a', 'm_p_b', 'm_w_out', 'm_b_out', 'm_ln_g', 'm_ln_b', 'v_ln_in_g', 'v_ln_in_b', 'v_w_in', 'v_b_in', 'v_sinks', 'v_vn_g', 'v_vn_b', 'v_w_s', 'v_b_s', 'v_p_a', 'v_p_b', 'v_w_out', 'v_b_out', 'v_ln_g', 'v_ln_b']
TWIN_OUTPUTS = ['loss', 'grad_x', 'grad_ln_in_g', 'grad_ln_in_b', 'grad_w_in', 'grad_b_in', 'grad_sinks', 'grad_vn_g', 'grad_vn_b', 'grad_w_s', 'grad_b_s', 'grad_p_a', 'grad_p_b', 'grad_w_out', 'grad_b_out', 'grad_ln_g', 'grad_ln_b', 'delta_ln_in_g', 'delta_ln_in_b', 'delta_w_in', 'delta_b_in', 'delta_sinks', 'delta_vn_g', 'delta_vn_b', 'delta_w_s', 'delta_b_s', 'delta_p_a', 'delta_p_b', 'delta_w_out', 'delta_b_out', 'delta_ln_g', 'delta_ln_b', 'new_m_ln_in_g', 'new_m_ln_in_b', 'new_m_w_in', 'new_m_b_in', 'new_m_sinks', 'new_m_vn_g', 'new_m_vn_b', 'new_m_w_s', 'new_m_b_s', 'new_m_p_a', 'new_m_p_b', 'new_m_w_out', 'new_m_b_out', 'new_m_ln_g', 'new_m_ln_b', 'new_v_ln_in_g', 'new_v_ln_in_b', 'new_v_w_in', 'new_v_b_in', 'new_v_sinks', 'new_v_vn_g', 'new_v_vn_b', 'new_v_w_s', 'new_v_b_s', 'new_v_p_a', 'new_v_p_b', 'new_v_w_out', 'new_v_b_out', 'new_v_ln_g', 'new_v_ln_b']
TWIN_LEAF_KINDS = {'loss': 'loss', 'grad_x': 'grad_x', 'grad_ln_in_g': 'grad_w', 'grad_ln_in_b': 'grad_w', 'grad_w_in': 'grad_w', 'grad_b_in': 'grad_w', 'grad_sinks': 'grad_w', 'grad_vn_g': 'grad_w', 'grad_vn_b': 'grad_w', 'grad_w_s': 'grad_w', 'grad_b_s': 'grad_w', 'grad_p_a': 'grad_w', 'grad_p_b': 'grad_w', 'grad_w_out': 'grad_w', 'grad_b_out': 'grad_w', 'grad_ln_g': 'grad_w', 'grad_ln_b': 'grad_w', 'delta_ln_in_g': 'delta_w', 'delta_ln_in_b': 'delta_w', 'delta_w_in': 'delta_w', 'delta_b_in': 'delta_w', 'delta_sinks': 'delta_w', 'delta_vn_g': 'delta_w', 'delta_vn_b': 'delta_w', 'delta_w_s': 'delta_w', 'delta_b_s': 'delta_w', 'delta_p_a': 'delta_w', 'delta_p_b': 'delta_w', 'delta_w_out': 'delta_w', 'delta_b_out': 'delta_w', 'delta_ln_g': 'delta_w', 'delta_ln_b': 'delta_w', 'new_m_ln_in_g': 'new_m', 'new_m_ln_in_b': 'new_m', 'new_m_w_in': 'new_m', 'new_m_b_in': 'new_m', 'new_m_sinks': 'new_m', 'new_m_vn_g': 'new_m', 'new_m_vn_b': 'new_m', 'new_m_w_s': 'new_m', 'new_m_b_s': 'new_m', 'new_m_p_a': 'new_m', 'new_m_p_b': 'new_m', 'new_m_w_out': 'new_m', 'new_m_b_out': 'new_m', 'new_m_ln_g': 'new_m', 'new_m_ln_b': 'new_m', 'new_v_ln_in_g': 'new_v', 'new_v_ln_in_b': 'new_v', 'new_v_w_in': 'new_v', 'new_v_b_in': 'new_v', 'new_v_sinks': 'new_v', 'new_v_vn_g': 'new_v', 'new_v_vn_b': 'new_v', 'new_v_w_s': 'new_v', 'new_v_b_s': 'new_v', 'new_v_p_a': 'new_v', 'new_v_p_b': 'new_v', 'new_v_w_out': 'new_v', 'new_v_b_out': 'new_v', 'new_v_ln_g': 'new_v', 'new_v_ln_b': 'new_v'}


def _forward(args):
    return _fwd_reference(*[args[k] for k in FWD_PARAMS])


def _output_shape():
    out = _jax.eval_shape(lambda: _forward(_fwd_setup_inputs(0)))
    return out.shape, out.dtype

N_MICROBATCH = 1
ADAM_LR = 0.001
ADAM_B1 = 0.9
ADAM_B2 = 0.999
ADAM_EPS = 1e-08
ADAM_WD = 0.01
ADAM_STEP = 10
PER_EXAMPLE_BATCH_AXIS = {'x': 0, 'loss_target': 0}
SHARED_INPUTS = []
_WEIGHT_DTYPES = {'ln_in_g': _jnp.float32, 'ln_in_b': _jnp.float32, 'w_in': _jnp.float32, 'b_in': _jnp.float32, 'sinks': _jnp.float32, 'vn_g': _jnp.float32, 'vn_b': _jnp.float32, 'w_s': _jnp.float32, 'b_s': _jnp.float32, 'p_a': _jnp.float32, 'p_b': _jnp.float32, 'w_out': _jnp.float32, 'b_out': _jnp.float32, 'ln_g': _jnp.float32, 'ln_b': _jnp.float32}
MOMENT_SCALE = {'ln_in_g': 6.132961e+00, 'ln_in_b': 7.901164e-01, 'w_in': 5.969611e-03, 'b_in': 1.022836e-02, 'sinks': 1.338524e-03, 'vn_g': 4.212450e-03, 'vn_b': 4.264948e-03, 'w_s': 4.198313e-03, 'b_s': 6.047195e-03, 'p_a': 3.011254e-03, 'p_b': 1.045257e-02, 'w_out': 1.070285e-02, 'b_out': 5.610093e-01, 'ln_g': 4.573311e+01, 'ln_b': 1.364259e+00}


def _to_microbatches(a, axis):
    t = _jnp.moveaxis(a, axis, 0)
    t = t.reshape((N_MICROBATCH, t.shape[0] // N_MICROBATCH) + t.shape[1:])
    return _jnp.moveaxis(t, 1, axis + 1)


def setup_inputs(seed: int = 0) -> dict:
    inp = _fwd_setup_inputs(seed)
    key = _jax.random.fold_in(_jax.random.key(seed), 7919)
    shape, _ = _output_shape()
    out = dict(inp)
    out["loss_target"] = _jax.random.normal(_jax.random.fold_in(key, 0), shape, _jnp.float32)
    for i, name in enumerate(TWIN_WEIGHTS):
        w = inp[name].astype(_jnp.float32)
        if MOMENT_SCALE is None:
            s = _jnp.sqrt(_jnp.mean(_jnp.square(w)) + 1e-30)
        else:
            s = MOMENT_SCALE[name]
        km, kv = _jax.random.split(_jax.random.fold_in(key, i + 1))
        out[name] = w
        out["m_" + name] = s * _jax.random.normal(km, w.shape, _jnp.float32)
        out["v_" + name] = (s * s) * _jax.random.uniform(kv, w.shape, _jnp.float32, 0.5, 1.5)
    if N_MICROBATCH > 1:
        for name, axis in PER_EXAMPLE_BATCH_AXIS.items():
            out[name] = _to_microbatches(out[name], axis)
    return {'x': out['x'], 'ln_in_g': out['ln_in_g'], 'ln_in_b': out['ln_in_b'], 'w_in': out['w_in'], 'b_in': out['b_in'], 'sinks': out['sinks'], 'vn_g': out['vn_g'], 'vn_b': out['vn_b'], 'w_s': out['w_s'], 'b_s': out['b_s'], 'p_a': out['p_a'], 'p_b': out['p_b'], 'w_out': out['w_out'], 'b_out': out['b_out'], 'ln_g': out['ln_g'], 'ln_b': out['ln_b'], 'loss_target': out['loss_target'], 'm_ln_in_g': out['m_ln_in_g'], 'm_ln_in_b': out['m_ln_in_b'], 'm_w_in': out['m_w_in'], 'm_b_in': out['m_b_in'], 'm_sinks': out['m_sinks'], 'm_vn_g': out['m_vn_g'], 'm_vn_b': out['m_vn_b'], 'm_w_s': out['m_w_s'], 'm_b_s': out['m_b_s'], 'm_p_a': out['m_p_a'], 'm_p_b': out['m_p_b'], 'm_w_out': out['m_w_out'], 'm_b_out': out['m_b_out'], 'm_ln_g': out['m_ln_g'], 'm_ln_b': out['m_ln_b'], 'v_ln_in_g': out['v_ln_in_g'], 'v_ln_in_b': out['v_ln_in_b'], 'v_w_in': out['v_w_in'], 'v_b_in': out['v_b_in'], 'v_sinks': out['v_sinks'], 'v_vn_g': out['v_vn_g'], 'v_vn_b': out['v_vn_b'], 'v_w_s': out['v_w_s'], 'v_b_s': out['v_b_s'], 'v_p_a': out['v_p_a'], 'v_p_b': out['v_p_b'], 'v_w_out': out['v_w_out'], 'v_b_out': out['v_b_out'], 'v_ln_g': out['v_ln_g'], 'v_ln_b': out['v_ln_b']}


def _loss(weights, diff, rest, loss_target):
    with _jax.named_scope("forward"):
        args = {**rest, TWIN_DIFF_INPUT: diff, **{k: w.astype(_WEIGHT_DTYPES[k]) for k, w in weights.items()}}
        y = _forward(args)
    with _jax.named_scope("loss_head"):
        err = _jnp.square(y.astype(_jnp.float32) - loss_target)
        return 0.5 * _jnp.sum(_jnp.mean(err, axis=-1)) if err.ndim else 0.5 * err


def _adamw(w, g, m, v):
    m = ADAM_B1 * m + (1.0 - ADAM_B1) * g
    v = ADAM_B2 * v + (1.0 - ADAM_B2) * _jnp.square(g)
    m_hat = m / (1.0 - ADAM_B1 ** ADAM_STEP)
    v_hat = v / (1.0 - ADAM_B2 ** ADAM_STEP)
    delta = -ADAM_LR * (m_hat / (_jnp.sqrt(v_hat) + ADAM_EPS) + ADAM_WD * w)
    return delta, m, v


def reference(x, ln_in_g, ln_in_b, w_in, b_in, sinks, vn_g, vn_b, w_s, b_s, p_a, p_b, w_out, b_out, ln_g, ln_b, loss_target, m_ln_in_g, m_ln_in_b, m_w_in, m_b_in, m_sinks, m_vn_g, m_vn_b, m_w_s, m_b_s, m_p_a, m_p_b, m_w_out, m_b_out, m_ln_g, m_ln_b, v_ln_in_g, v_ln_in_b, v_w_in, v_b_in, v_sinks, v_vn_g, v_vn_b, v_w_s, v_b_s, v_p_a, v_p_b, v_w_out, v_b_out, v_ln_g, v_ln_b):
    given = dict(x=x, ln_in_g=ln_in_g, ln_in_b=ln_in_b, w_in=w_in, b_in=b_in, sinks=sinks, vn_g=vn_g, vn_b=vn_b, w_s=w_s, b_s=b_s, p_a=p_a, p_b=p_b, w_out=w_out, b_out=b_out, ln_g=ln_g, ln_b=ln_b, loss_target=loss_target, m_ln_in_g=m_ln_in_g, m_ln_in_b=m_ln_in_b, m_w_in=m_w_in, m_b_in=m_b_in, m_sinks=m_sinks, m_vn_g=m_vn_g, m_vn_b=m_vn_b, m_w_s=m_w_s, m_b_s=m_b_s, m_p_a=m_p_a, m_p_b=m_p_b, m_w_out=m_w_out, m_b_out=m_b_out, m_ln_g=m_ln_g, m_ln_b=m_ln_b, v_ln_in_g=v_ln_in_g, v_ln_in_b=v_ln_in_b, v_w_in=v_w_in, v_b_in=v_b_in, v_sinks=v_sinks, v_vn_g=v_vn_g, v_vn_b=v_vn_b, v_w_s=v_w_s, v_b_s=v_b_s, v_p_a=v_p_a, v_p_b=v_p_b, v_w_out=v_w_out, v_b_out=v_b_out, v_ln_g=v_ln_g, v_ln_b=v_ln_b)
    weights = {n: given[n] for n in TWIN_WEIGHTS}
    shared = {n: given[n] for n in SHARED_INPUTS}
    per_example = {n: given[n] for n in ['x']}
    grad_fn = _jax.value_and_grad(_loss, argnums=(0, 1))

    def one_microbatch(ex, loss_target):
        ex = dict(ex)
        diff = ex.pop(TWIN_DIFF_INPUT)
        return grad_fn(weights, diff, {**shared, **ex}, loss_target)

    if N_MICROBATCH == 1:
        loss, (grad_w, grad_x) = one_microbatch(per_example, given["loss_target"])
    else:
        def body(carry, xs):
            loss_sum, grad_sum = carry
            l_k, (gw_k, gx_k) = one_microbatch(xs[0], xs[1])
            with _jax.named_scope("update"):
                return (loss_sum + l_k, _jax.tree.map(_jnp.add, grad_sum, gw_k)), gx_k

        init = (_jnp.zeros((), _jnp.float32), _jax.tree.map(_jnp.zeros_like, weights))
        (loss, grad_w), grad_x = _jax.lax.scan(body, init, (per_example, given["loss_target"]))
    with _jax.named_scope("update"):
        delta_w, new_m, new_v = {}, {}, {}
        for n in TWIN_WEIGHTS:
            delta_w[n], new_m[n], new_v[n] = _adamw(weights[n], grad_w[n], given["m_" + n], given["v_" + n])
    return (loss, grad_x, *[grad_w[n] for n in TWIN_WEIGHTS], *[delta_w[n] for n in TWIN_WEIGHTS],
            *[new_m[n] for n in TWIN_WEIGHTS], *[new_v[n] for n in TWIN_WEIGHTS])
```

```python
import jax
import jax.numpy as jnp
from jax import lax
from jax.experimental import pallas as pl
from jax.experimental.pallas import tpu as pltpu

F32 = jnp.float32
BF16 = jnp.bfloat16

D = 1024
BLK = 128
N_KV = 2
Q_W, KV_W, SGU_W = 512, 128, 512
C_Q, C_K, C_V, C_GA, C_UB, C_VB, C_GB = 0, 512, 640, 768, 1280, 1792, 2304
MAIN_W = 2816
R_W = 2048
IN_COLS = MAIN_W + R_W
N_DEV = 8
SHARD_COLS = IN_COLS // N_DEV

DEPTH = 2
ALPHA = (2.0 * DEPTH) ** 0.25
LN_EPS = 1e-5
ATTN_SCALE = 0.125
NEG = float(jnp.finfo(jnp.float32).min)

ADAM_LR, ADAM_B1, ADAM_B2, ADAM_EPS, ADAM_WD, ADAM_STEP = 0.001, 0.9, 0.999, 1e-08, 0.01, 10

TM = 256
NB = TM // BLK
MESH = pl.DeviceIdType.MESH
VMEM_LIMIT = 56 * 1024 * 1024

_ARB = pltpu.CompilerParams(dimension_semantics=("arbitrary",), vmem_limit_bytes=VMEM_LIMIT)


def _sigmoid(x):
    return 1.0 / (1.0 + jnp.exp(-x))


_GELU_C = 0.7978845608028654
_GELU_A = 0.044715


def _gelu(x):
    return 0.5 * x * (1.0 + jnp.tanh(_GELU_C * (x + _GELU_A * x * x * x)))


def _dgelu(x):
    t = jnp.tanh(_GELU_C * (x + _GELU_A * x * x * x))
    return 0.5 * (1.0 + t) + 0.5 * x * (1.0 - t * t) * (_GELU_C * (1.0 + 3.0 * _GELU_A * x * x))


def _ln_stats(x):
    mu = jnp.mean(x, axis=-1, keepdims=True)
    xc = x - mu
    var = jnp.mean(xc * xc, axis=-1, keepdims=True)
    rstd = lax.rsqrt(var + LN_EPS)
    return xc * rstd, rstd


def _ln_bwd(dy_g, xhat, rstd):
    m1 = jnp.mean(dy_g, axis=-1, keepdims=True)
    m2 = jnp.mean(dy_g * xhat, axis=-1, keepdims=True)
    return rstd * (dy_g - m1 - xhat * m2)


def _colsum(x):
    return jnp.sum(x, axis=0, keepdims=True)


def _dot(a, b):
    return jnp.dot(a, b, preferred_element_type=F32)


def _dot_nt(a, b):
    return lax.dot_general(a, b, (((1,), (1,)), ((), ())), preferred_element_type=F32)


def _dot_tn(a, b):
    return lax.dot_general(a, b, (((0,), (0,)), ((), ())), preferred_element_type=F32)


def _head_place(hk, g):
    j = 4 * hk + g
    return j, j // 2, j % 2


def _attn_group(q, kband, vband, hk, sinks_ref, lower):
    lane_k = lax.broadcasted_iota(jnp.int32, (2 * BLK, KV_W), 1)
    in_head = (lane_k >= 64 * hk) & (lane_k < 64 * hk + 64)
    kh = jnp.where(in_head, kband, 0.0).astype(BF16)
    vh = jnp.where(in_head, vband, 0.0).astype(BF16)
    parts = []
    for g in range(4):
        _, p, pos = _head_place(hk, g)
        qp = q[:, BLK * p:BLK * (p + 1)] * ATTN_SCALE
        if pos != hk:
            qp = pltpu.roll(qp, 64, 1)
        parts.append(qp.astype(BF16))
    q4 = jnp.concatenate(parts, axis=0)
    s = _dot_nt(q4, kh)
    row = lax.broadcasted_iota(jnp.int32, (4 * BLK, 2 * BLK), 0) & (BLK - 1)
    kpos = lax.broadcasted_iota(jnp.int32, (4 * BLK, 2 * BLK), 1)
    valid = (kpos > row) & (kpos <= row + BLK) & (kpos >= lower)
    s = jnp.where(valid, s, NEG)
    sink4 = jnp.concatenate(
        [jnp.full((BLK, 1), sinks_ref[4 * hk + g], F32) for g in range(4)], axis=0)
    m = jnp.maximum(jnp.max(s, axis=1, keepdims=True), sink4)
    p_un = jnp.exp(s - m)
    e_sink = jnp.exp(sink4 - m)
    inv = 1.0 / (jnp.sum(p_un, axis=1, keepdims=True) + e_sink)
    prob = p_un * inv
    o4 = _dot(prob.astype(BF16), vh)
    return q4, kh, vh, prob, e_sink * inv, o4


def _unstack_heads(x4, hk, pairs):
    for g in range(4):
        _, p, pos = _head_place(hk, g)
        xg = x4[BLK * g:BLK * (g + 1)]
        if pos != hk:
            xg = pltpu.roll(xg, 64, 1)
        pairs[p] = xg if pairs[p] is None else pairs[p] + xg
    return pairs


def _attn_fwd(q, kband, vband, sinks_ref, lower):
    pairs = [None] * 4
    for hk in range(N_KV):
        o4 = _attn_group(q, kband, vband, hk, sinks_ref, lower)[-1]
        pairs = _unstack_heads(o4, hk, pairs)
    return jnp.concatenate(pairs, axis=1)


def _tril_mask():
    r = lax.broadcasted_iota(jnp.int32, (BLK, BLK), 0)
    c = lax.broadcasted_iota(jnp.int32, (BLK, BLK), 1)
    return c <= r


def _sgu_fwd(u_b, v_b, vn_g, vn_b, wt, bsb_ref):
    u = _gelu(u_b)
    v = _gelu(v_b)
    vhat, rstd = _ln_stats(v)
    vn = vhat * vn_g + vn_b
    mixed = jnp.concatenate(
        [_dot(wt[g], vn[:, BLK * g:BLK * (g + 1)].astype(BF16)) + bsb_ref[g] for g in range(4)], axis=1)
    return u, vhat, rstd, vn, mixed


def _band(hm_ref, hprev_ref, s, col):
    r0 = s * BLK
    cur = hm_ref[r0:r0 + BLK, col:col + KV_W]
    if s == 0:
        off = 0 if col == C_K else KV_W
        prev = hprev_ref[:, off:off + KV_W]
    else:
        prev = hm_ref[r0 - BLK:r0, col:col + KV_W]
    return jnp.concatenate([prev, cur], axis=0)


def _mixer_in_specs(nt, rev):
    def tile(g):
        return nt - 1 - g if rev else g

    return [
        pl.BlockSpec(memory_space=pltpu.SMEM),
        pl.BlockSpec((TM, MAIN_W), lambda g: (tile(g), 0)),
        pl.BlockSpec((BLK, 2 * KV_W), lambda g: (jnp.maximum(tile(g) * NB - 1, 0), 2)),
    ]


_CONST2 = lambda g: (0, 0)
_CONST3 = lambda g: (0, 0, 0)


def _ln_fwd(x, g, b, name):
    t = x.shape[0]

    def body(x_ref, g_ref, b_ref, o_ref):
        xhat, _ = _ln_stats(x_ref[...])
        o_ref[...] = xhat * g_ref[...] + b_ref[...]

    return pl.pallas_call(
        body, name=name, grid=(t // TM,),
        in_specs=[pl.BlockSpec((TM, D), lambda i: (i, 0)), pl.BlockSpec((1, D), _CONST2), pl.BlockSpec((1, D), _CONST2)],
        out_specs=pl.BlockSpec((TM, D), lambda i: (i, 0)),
        out_shape=jax.ShapeDtypeStruct((t, D), F32), compiler_params=_ARB,
    )(x, g.reshape(1, D), b.reshape(1, D))


def _inproj(x, wm, wr, bm, br, name):
    t = x.shape[0]

    def body(x_ref, wm_ref, wr_ref, bm_ref, br_ref, hm_ref, hr_ref):
        xb = x_ref[...].astype(BF16)
        hm_ref[...] = _dot(xb, wm_ref[...]) + bm_ref[...]
        hr_ref[...] = _dot(xb, wr_ref[...]) + br_ref[...]

    return pl.pallas_call(
        body, name=name, grid=(t // TM,),
        in_specs=[pl.BlockSpec((TM, D), lambda i: (i, 0)),
                  pl.BlockSpec((D, MAIN_W), _CONST2), pl.BlockSpec((D, R_W), _CONST2),
                  pl.BlockSpec((1, MAIN_W), _CONST2), pl.BlockSpec((1, R_W), _CONST2)],
        out_specs=[pl.BlockSpec((TM, MAIN_W), lambda i: (i, 0)), pl.BlockSpec((TM, R_W), lambda i: (i, 0))],
        out_shape=[jax.ShapeDtypeStruct((t, MAIN_W), F32), jax.ShapeDtypeStruct((t, R_W), F32)],
        compiler_params=_ARB,
    )(x, wm, wr, bm, br)


def _mixer_fwd(hm, sinks, vn_g, vn_b, w_s, bsb, nblk_seq, name):
    t = hm.shape[0]
    nt = t // TM

    def body(sinks_ref, hm_ref, hprev_ref, vng_ref, vnb_ref, ws_ref, bsb_ref, ya_ref, yb_ref):
        i = pl.program_id(0)
        tril = _tril_mask()
        wt = [jnp.where(tril, ws_ref[g], 0.0).astype(BF16) for g in range(4)]
        for s in range(NB):
            r0 = s * BLK
            rows = slice(r0, r0 + BLK)
            lower = jnp.where((i * NB + s) % nblk_seq == 0, BLK, 0)
            attn = _attn_fwd(hm_ref[rows, C_Q:C_Q + Q_W], _band(hm_ref, hprev_ref, s, C_K),
                             _band(hm_ref, hprev_ref, s, C_V), sinks_ref, lower)
            g_a = hm_ref[rows, C_GA:C_GA + Q_W]
            ya_ref[rows, :] = (attn * (g_a * _sigmoid(g_a))).astype(BF16)
            u, _, _, _, mixed = _sgu_fwd(hm_ref[rows, C_UB:C_UB + SGU_W], hm_ref[rows, C_VB:C_VB + SGU_W],
                                         vng_ref[...], vnb_ref[...], wt, bsb_ref)
            g_b = hm_ref[rows, C_GB:C_GB + SGU_W]
            yb_ref[rows, :] = (u * mixed * (g_b * _sigmoid(g_b))).astype(BF16)

    return pl.pallas_call(
        body, name=name, grid=(nt,),
        in_specs=_mixer_in_specs(nt, False) + [
            pl.BlockSpec((1, SGU_W), _CONST2), pl.BlockSpec((1, SGU_W), _CONST2),
            pl.BlockSpec((4, BLK, BLK), _CONST3), pl.BlockSpec((4, BLK, BLK), _CONST3)],
        out_specs=[pl.BlockSpec((TM, Q_W), lambda i: (i, 0)), pl.BlockSpec((TM, SGU_W), lambda i: (i, 0))],
        out_shape=[jax.ShapeDtypeStruct((t, Q_W), BF16), jax.ShapeDtypeStruct((t, SGU_W), BF16)],
        compiler_params=_ARB,
    )(sinks, hm, hm, vn_g, vn_b, w_s, bsb)


def _tail_fwd(x, ya, yb, hr, pa_w, pb_w, wo, b_out, ln_g, ln_b, name):
    t = x.shape[0]

    def body(x_ref, ya_ref, yb_ref, hr_ref, paw_ref, pbw_ref, wo_ref, bo_ref, g_ref, b_ref,
             pa_ref, pb_ref, mg_ref, z_ref, xn_ref):
        pa = _dot(ya_ref[...], paw_ref[...])
        pb = _dot(yb_ref[...], pbw_ref[...])
        pa_ref[...] = pa
        pb_ref[...] = pb
        merged = _sigmoid(hr_ref[:, 0:D]) * pa + _sigmoid(hr_ref[:, D:2 * D]) * pb
        mb = merged.astype(BF16)
        mg_ref[...] = mb
        z = ALPHA * x_ref[...] + (_dot(mb, wo_ref[...]) + bo_ref[...])
        z_ref[...] = z
        zhat, _ = _ln_stats(z)
        xn_ref[...] = zhat * g_ref[...] + b_ref[...]

    row = lambda w: pl.BlockSpec((TM, w), lambda i: (i, 0))
    vec = pl.BlockSpec((1, D), _CONST2)
    return pl.pallas_call(
        body, name=name, grid=(t // TM,),
        in_specs=[row(D), row(Q_W), row(SGU_W), row(R_W),
                  pl.BlockSpec((Q_W, D), _CONST2), pl.BlockSpec((SGU_W, D), _CONST2), pl.BlockSpec((D, D), _CONST2),
                  vec, vec, vec],
        out_specs=[row(D), row(D), row(D), row(D), row(D)],
        out_shape=[jax.ShapeDtypeStruct((t, D), F32), jax.ShapeDtypeStruct((t, D), F32),
                   jax.ShapeDtypeStruct((t, D), BF16), jax.ShapeDtypeStruct((t, D), F32),
                   jax.ShapeDtypeStruct((t, D), F32)],
        compiler_params=_ARB,
    )(x, ya, yb, hr, pa_w, pb_w, wo, b_out, ln_g, ln_b)


def _loss_head(y, target, name):
    t = y.shape[0]

    def body(y_ref, t_ref, dy_ref, acc_ref):
        @pl.when(pl.program_id(0) == 0)
        def _():
            acc_ref[...] = jnp.zeros_like(acc_ref)

        err = y_ref[...] - t_ref[...]
        dy_ref[...] = err * (1.0 / D)
        acc_ref[...] += jnp.sum(jnp.sum(err * err, axis=1, keepdims=True), axis=0, keepdims=True)

    return pl.pallas_call(
        body, name=name, grid=(t // TM,),
        in_specs=[pl.BlockSpec((TM, D), lambda i: (i, 0)), pl.BlockSpec((TM, D), lambda i: (i, 0))],
        out_specs=[pl.BlockSpec((TM, D), lambda i: (i, 0)), pl.BlockSpec((8, 128), _CONST2)],
        out_shape=[jax.ShapeDtypeStruct((t, D), F32), jax.ShapeDtypeStruct((8, 128), F32)],
        compiler_params=_ARB,
    )(y, target)


def _tail_bwd(dxn, z, pa, pb, hr, wo_t, pa_t, pb_t, ln_g, name):
    t = dxn.shape[0]

    def body(dxn_ref, z_ref, pa_ref, pb_ref, hr_ref, wot_ref, pat_ref, pbt_ref, g_ref,
             dz_ref, dpa_ref, dpb_ref, dhr_ref, dya_ref, dyb_ref, acc_ref, gbr_ref):
        @pl.when(pl.program_id(0) == 0)
        def _():
            acc_ref[...] = jnp.zeros_like(acc_ref)
            gbr_ref[...] = jnp.zeros_like(gbr_ref)

        dxn_v = dxn_ref[...]
        zhat, rstd = _ln_stats(z_ref[...])
        dz = _ln_bwd(dxn_v * g_ref[...], zhat, rstd)
        dz_ref[...] = dz
        acc_ref[0:1, :] += _colsum(dxn_v * zhat)
        acc_ref[1:2, :] += _colsum(dxn_v)
        acc_ref[2:3, :] += _colsum(dz)
        dmerged = _dot(dz.astype(BF16), wot_ref[...])
        sa = _sigmoid(hr_ref[:, 0:D])
        sb = _sigmoid(hr_ref[:, D:2 * D])
        dpa = (dmerged * sa).astype(BF16)
        dpb = (dmerged * sb).astype(BF16)
        dpa_ref[...] = dpa
        dpb_ref[...] = dpb
        dra = dmerged * pa_ref[...] * (sa * (1.0 - sa))
        drb = dmerged * pb_ref[...] * (sb * (1.0 - sb))
        dhr_ref[:, 0:D] = dra.astype(BF16)
        dhr_ref[:, D:2 * D] = drb.astype(BF16)
        gbr_ref[0:1, 0:D] += _colsum(dra)
        gbr_ref[0:1, D:2 * D] += _colsum(drb)
        dya_ref[...] = _dot(dpa, pat_ref[...])
        dyb_ref[...] = _dot(dpb, pbt_ref[...])

    row = lambda w: pl.BlockSpec((TM, w), lambda i: (i, 0))
    return pl.pallas_call(
        body, name=name, grid=(t // TM,),
        in_specs=[row(D), row(D), row(D), row(D), row(R_W),
                  pl.BlockSpec((D, D), _CONST2), pl.BlockSpec((D, Q_W), _CONST2), pl.BlockSpec((D, SGU_W), _CONST2),
                  pl.BlockSpec((1, D), _CONST2)],
        out_specs=[row(D), row(D), row(D), row(R_W), row(Q_W), row(SGU_W),
                   pl.BlockSpec((8, D), _CONST2), pl.BlockSpec((8, R_W), _CONST2)],
        out_shape=[jax.ShapeDtypeStruct((t, D), F32), jax.ShapeDtypeStruct((t, D), BF16),
                   jax.ShapeDtypeStruct((t, D), BF16), jax.ShapeDtypeStruct((t, R_W), BF16),
                   jax.ShapeDtypeStruct((t, Q_W), F32), jax.ShapeDtypeStruct((t, SGU_W), F32),
                   jax.ShapeDtypeStruct((8, D), F32), jax.ShapeDtypeStruct((8, R_W), F32)],
        compiler_params=_ARB,
    )(dxn, z, pa, pb, hr, wo_t, pa_t, pb_t, ln_g)


def _mixer_bwd(hm, dya, dyb, sinks, vn_g, vn_b, w_s, bsb, nblk_seq, name):
    t = hm.shape[0]
    nt = t // TM

    def body(sinks_ref, hm_ref, hprev_ref, dya_ref, dyb_ref, vng_ref, vnb_ref, ws_ref, bsb_ref,
             dhm_ref, gbm_ref, gsk_ref, gvn_ref, gws_ref, gbs_ref, dk_carry, dv_carry):
        gi = pl.program_id(0)
        i = nt - 1 - gi

        @pl.when(gi == 0)
        def _():
            for r in (gbm_ref, gsk_ref, gvn_ref, gws_ref, gbs_ref, dk_carry, dv_carry):
                r[...] = jnp.zeros_like(r)

        tril = _tril_mask()
        wt = [jnp.where(tril, ws_ref[g], 0.0).astype(BF16) for g in range(4)]
        lane_q = lax.broadcasted_iota(jnp.int32, (4 * BLK, KV_W), 1)
        lane_k = lax.broadcasted_iota(jnp.int32, (2 * BLK, KV_W), 1)
        vng = vng_ref[...]

        def put(rows, col, val):
            dhm_ref[rows, col:col + val.shape[1]] = val.astype(BF16)
            gbm_ref[0:1, col:col + val.shape[1]] += _colsum(val)

        for s in reversed(range(NB)):
            r0 = s * BLK
            rows = slice(r0, r0 + BLK)
            lower = jnp.where((i * NB + s) % nblk_seq == 0, BLK, 0)
            q = hm_ref[rows, C_Q:C_Q + Q_W]
            kband = _band(hm_ref, hprev_ref, s, C_K)
            vband = _band(hm_ref, hprev_ref, s, C_V)
            g_a = hm_ref[rows, C_GA:C_GA + Q_W]
            sg = _sigmoid(g_a)
            dya_v = dya_ref[rows, :]
            d_o = dya_v * (g_a * sg)
            o_pairs, dq_pairs = [None] * 4, [None] * 4
            dkband = jnp.zeros((2 * BLK, KV_W), F32)
            dvband = jnp.zeros((2 * BLK, KV_W), F32)
            for hk in range(N_KV):
                q4, kh, vh, prob, p_sink, o4 = _attn_group(q, kband, vband, hk, sinks_ref, lower)
                o_pairs = _unstack_heads(o4, hk, o_pairs)
                parts = []
                for g in range(4):
                    _, p, pos = _head_place(hk, g)
                    dp = d_o[:, BLK * p:BLK * (p + 1)]
                    parts.append(pltpu.roll(dp, 64, 1) if pos != hk else dp)
                on_q = (lane_q >= 64 * hk) & (lane_q < 64 * hk + 64)
                do4 = jnp.where(on_q, jnp.concatenate(parts, axis=0), 0.0)
                do4b = do4.astype(BF16)
                delta = jnp.sum(do4 * o4, axis=1, keepdims=True)
                ds = prob * (_dot_nt(do4b, vh) - delta)
                dsb = ds.astype(BF16)
                dq_pairs = _unstack_heads(_dot(dsb, kh) * ATTN_SCALE, hk, dq_pairs)
                on_k = (lane_k >= 64 * hk) & (lane_k < 64 * hk + 64)
                dkband = dkband + jnp.where(on_k, _dot_tn(dsb, q4), 0.0)
                dvband = dvband + _dot_tn(prob.astype(BF16), do4b)
                dsk = p_sink * delta
                for g in range(4):
                    j = 4 * hk + g
                    tot = jnp.sum(dsk[BLK * g:BLK * (g + 1)], axis=0, keepdims=True)
                    gsk_ref[j:j + 1, :] += jnp.broadcast_to(-tot, (1, 128))
            attn = jnp.concatenate(o_pairs, axis=1)
            put(rows, C_Q, jnp.concatenate(dq_pairs, axis=1))
            put(rows, C_K, dkband[BLK:2 * BLK] + dk_carry[...])
            put(rows, C_V, dvband[BLK:2 * BLK] + dv_carry[...])
            dk_carry[...] = dkband[0:BLK]
            dv_carry[...] = dvband[0:BLK]
            put(rows, C_GA, dya_v * attn * (sg * (1.0 + g_a * (1.0 - sg))))
            u_b = hm_ref[rows, C_UB:C_UB + SGU_W]
            v_b = hm_ref[rows, C_VB:C_VB + SGU_W]
            g_b = hm_ref[rows, C_GB:C_GB + SGU_W]
            u, vhat, rstd, vn, mixed = _sgu_fwd(u_b, v_b, vng, vnb_ref[...], wt, bsb_ref)
            sgb = _sigmoid(g_b)
            silu_b = g_b * sgb
            dyb_v = dyb_ref[rows, :]
            du = dyb_v * mixed * silu_b
            dmixed = dyb_v * u * silu_b
            put(rows, C_GB, dyb_v * u * mixed * (sgb * (1.0 + g_b * (1.0 - sgb))))
            dvn_parts = []
            for g in range(4):
                cols = slice(BLK * g, BLK * (g + 1))
                dmg = dmixed[:, cols]
                dmgb = dmg.astype(BF16)
                dvn_parts.append(_dot_tn(wt[g], dmgb))
                gws_ref[g] += jnp.where(tril, _dot_nt(dmgb, vn[:, cols].astype(BF16)), 0.0)
                gbs_ref[g] += dmg
            dvn = jnp.concatenate(dvn_parts, axis=1)
            gvn_ref[0:1, :] += _colsum(dvn * vhat)
            gvn_ref[1:2, :] += _colsum(dvn)
            dv = _ln_bwd(dvn * vng, vhat, rstd)
            put(rows, C_UB, du * _dgelu(u_b))
            put(rows, C_VB, dv * _dgelu(v_b))

        @pl.when(gi == nt - 1)
        def _():
            for g in range(4):
                gbs_ref[g] = jnp.broadcast_to(jnp.sum(gbs_ref[g], axis=1, keepdims=True), (BLK, BLK))

    row = lambda w: pl.BlockSpec((TM, w), lambda g: (nt - 1 - g, 0))
    return pl.pallas_call(
        body, name=name, grid=(nt,),
        in_specs=_mixer_in_specs(nt, True) + [
            row(Q_W), row(SGU_W),
            pl.BlockSpec((1, SGU_W), _CONST2), pl.BlockSpec((1, SGU_W), _CONST2),
            pl.BlockSpec((4, BLK, BLK), _CONST3), pl.BlockSpec((4, BLK, BLK), _CONST3)],
        out_specs=[row(MAIN_W), pl.BlockSpec((8, MAIN_W), _CONST2), pl.BlockSpec((8, 128), _CONST2),
                   pl.BlockSpec((8, SGU_W), _CONST2), pl.BlockSpec((4, BLK, BLK), _CONST3),
                   pl.BlockSpec((4, BLK, BLK), _CONST3)],
        out_shape=[jax.ShapeDtypeStruct((t, MAIN_W), BF16), jax.ShapeDtypeStruct((8, MAIN_W), F32),
                   jax.ShapeDtypeStruct((8, 128), F32), jax.ShapeDtypeStruct((8, SGU_W), F32),
                   jax.ShapeDtypeStruct((4, BLK, BLK), F32), jax.ShapeDtypeStruct((4, BLK, BLK), F32)],
        scratch_shapes=[pltpu.VMEM((BLK, KV_W), F32), pltpu.VMEM((BLK, KV_W), F32)],
        compiler_params=_ARB,
    )(sinks, hm, hm, dya, dyb, vn_g, vn_b, w_s, bsb)


def _dx_inproj(dz, dhm, dhr, wm_t, wr_t, name):
    t = dz.shape[0]

    def body(dz_ref, dhm_ref, dhr_ref, wmt_ref, wrt_ref, dx_ref):
        dx_ref[...] = ALPHA * dz_ref[...] + _dot(dhm_ref[...], wmt_ref[...]) + _dot(dhr_ref[...], wrt_ref[...])

    row = lambda w: pl.BlockSpec((TM, w), lambda i: (i, 0))
    return pl.pallas_call(
        body, name=name, grid=(t // TM,),
        in_specs=[row(D), row(MAIN_W), row(R_W), pl.BlockSpec((MAIN_W, D), _CONST2), pl.BlockSpec((R_W, D), _CONST2)],
        out_specs=row(D), out_shape=jax.ShapeDtypeStruct((t, D), F32), compiler_params=_ARB,
    )(dz, dhm, dhr, wm_t, wr_t)


def _wgrad(a, b, tn, name, tk=512):
    t, m = a.shape
    n = b.shape[1]
    nk = t // tk

    def body(a_ref, b_ref, o_ref):
        @pl.when(pl.program_id(1) == 0)
        def _():
            o_ref[...] = jnp.zeros_like(o_ref)

        o_ref[...] += _dot_tn(a_ref[...].astype(BF16), b_ref[...].astype(BF16))

    return pl.pallas_call(
        body, name=name, grid=(n // tn, nk),
        in_specs=[pl.BlockSpec((tk, m), lambda j, k: (k, 0)), pl.BlockSpec((tk, tn), lambda j, k: (k, j))],
        out_specs=pl.BlockSpec((m, tn), lambda j, k: (0, j)),
        out_shape=jax.ShapeDtypeStruct((m, n), F32),
        compiler_params=pltpu.CompilerParams(dimension_semantics=("arbitrary", "arbitrary"), vmem_limit_bytes=VMEM_LIMIT),
    )(a, b)


def _ln_in_bwd(dx0, x, g, name):
    t = x.shape[0]

    def body(dx0_ref, x_ref, g_ref, gx_ref, acc_ref):
        @pl.when(pl.program_id(0) == 0)
        def _():
            acc_ref[...] = jnp.zeros_like(acc_ref)

        d = dx0_ref[...]
        xhat, rstd = _ln_stats(x_ref[...])
        gx_ref[...] = _ln_bwd(d * g_ref[...], xhat, rstd)
        acc_ref[0:1, :] += _colsum(d * xhat)
        acc_ref[1:2, :] += _colsum(d)

    row = pl.BlockSpec((TM, D), lambda i: (i, 0))
    return pl.pallas_call(
        body, name=name, grid=(t // TM,),
        in_specs=[row, row, pl.BlockSpec((1, D), _CONST2)],
        out_specs=[row, pl.BlockSpec((8, D), _CONST2)],
        out_shape=[jax.ShapeDtypeStruct((t, D), F32), jax.ShapeDtypeStruct((8, D), F32)],
        compiler_params=_ARB,
    )(dx0, x, g)


_ANY = pl.BlockSpec(memory_space=pl.ANY)


def _place():
    return lax.axis_index("x"), lax.axis_index("y"), lax.axis_index("c")


def _allgather8(xs, name):
    n = len(xs)

    def body(*refs):
        x_refs, o_refs = refs[:n], refs[n:2 * n]
        send_sems, recv_sems, local_sems = refs[2 * n:]
        x, y, c = _place()
        me, sib = (x, y, c), (x, y, 1 - c)
        chips = [(1 - x, y), (x, 1 - y), (1 - x, 1 - y)]

        def copy(a, k, block, to, src=None):
            dst = o_refs[a].at[4 * block[0] + 2 * block[1] + block[2]]
            return pltpu.make_async_remote_copy(
                src_ref=dst if src is None else src, dst_ref=dst, send_sem=send_sems.at[7 * a + k],
                recv_sem=recv_sems.at[7 * a + k], device_id=to, device_id_type=MESH)

        mine = [pltpu.make_async_copy(x_refs[a], o_refs[a].at[4 * x + 2 * y + c], local_sems.at[a]) for a in range(n)]
        for cp in mine:
            cp.start()
        sent = []
        for a in range(n):
            sent.append(copy(a, 0, me, sib, src=x_refs[a]))
            sent += [copy(a, 1 + j, me, (*chip, c), src=x_refs[a]) for j, chip in enumerate(chips)]
        for cp in sent:
            cp.start()
        for j, chip in enumerate(chips):
            for a in range(n):
                copy(a, 1 + j, (*chip, c), me).wait_recv()
                fwd = copy(a, 4 + j, (*chip, c), sib)
                fwd.start()
                sent.append(fwd)
        for a in range(n):
            copy(a, 0, sib, me).wait_recv()
            for j, chip in enumerate(chips):
                copy(a, 4 + j, (*chip, 1 - c), me).wait_recv()
        for cp in sent:
            cp.wait_send()
        for cp in mine:
            cp.wait()

    return pl.pallas_call(
        body, name=name, in_specs=[_ANY] * n, out_specs=[_ANY] * n,
        out_shape=[jax.ShapeDtypeStruct((N_DEV,) + v.shape, v.dtype) for v in xs],
        scratch_shapes=[pltpu.SemaphoreType.DMA((7 * n,)), pltpu.SemaphoreType.DMA((7 * n,)),
                        pltpu.SemaphoreType.DMA((n,))],
    )(*xs)


def _swap_sibling(gs, name):
    n = len(gs)

    def body(*refs):
        g_refs, r_refs = refs[:n], refs[n:2 * n]
        send_sems, recv_sems = refs[2 * n:]
        x, y, c = _place()
        cps = [pltpu.make_async_remote_copy(
            src_ref=g_refs[a].at[1 - c], dst_ref=r_refs[a], send_sem=send_sems.at[a], recv_sem=recv_sems.at[a],
            device_id=(x, y, 1 - c), device_id_type=MESH) for a in range(n)]
        for cp in cps:
            cp.start()
        for cp in cps:
            cp.wait()

    return pl.pallas_call(
        body, name=name, in_specs=[_ANY] * n, out_specs=[_ANY] * n,
        out_shape=[jax.ShapeDtypeStruct(v.shape[1:], v.dtype) for v in gs],
        scratch_shapes=[pltpu.SemaphoreType.DMA((n,)), pltpu.SemaphoreType.DMA((n,))],
    )(*gs)


def _pair_sum(g, r, name):
    _, _, rows, l = g.shape
    tr = min(rows, 512)

    def body(c_ref, g_ref, r_ref, o_ref):
        o_ref[...] = g_ref[0] + r_ref[...]

    return pl.pallas_call(
        body, name=name,
        grid_spec=pltpu.PrefetchScalarGridSpec(
            num_scalar_prefetch=1, grid=(4, rows // tr),
            in_specs=[pl.BlockSpec((1, 1, tr, l), lambda q, i, c_ref: (c_ref[0], q, i, 0)),
                      pl.BlockSpec((1, tr, l), lambda q, i, c_ref: (q, i, 0))],
            out_specs=pl.BlockSpec((1, tr, l), lambda q, i, c_ref: (q, i, 0))),
        out_shape=jax.ShapeDtypeStruct((4, rows, l), F32),
        compiler_params=pltpu.CompilerParams(dimension_semantics=("arbitrary", "arbitrary"), vmem_limit_bytes=VMEM_LIMIT),
    )(lax.axis_index("c").astype(jnp.int32).reshape(1), g, r)


def _exchange_chips(ps, name):
    n = len(ps)

    def body(*refs):
        p_refs, r_refs = refs[:n], refs[n:2 * n]
        send_sems, recv_sems, local_sems = refs[2 * n:]
        x, y, c = _place()
        mine = 2 * x + y
        chips = [(1 - x, y), (x, 1 - y), (1 - x, 1 - y)]
        local = [pltpu.make_async_copy(p_refs[a].at[mine], r_refs[a].at[mine], local_sems.at[a]) for a in range(n)]
        for cp in local:
            cp.start()
        cps = []
        for a in range(n):
            for j, (qx, qy) in enumerate(chips):
                cps.append(pltpu.make_async_remote_copy(
                    src_ref=p_refs[a].at[2 * qx + qy], dst_ref=r_refs[a].at[mine], send_sem=send_sems.at[3 * a + j],
                    recv_sem=recv_sems.at[3 * a + j], device_id=(qx, qy, c), device_id_type=MESH))
        for cp in cps:
            cp.start()
        for a in range(n):
            for j, (qx, qy) in enumerate(chips):
                pltpu.make_async_remote_copy(
                    src_ref=p_refs[a].at[mine], dst_ref=r_refs[a].at[2 * qx + qy], send_sem=send_sems.at[3 * a + j],
                    recv_sem=recv_sems.at[3 * a + j], device_id=(qx, qy, c), device_id_type=MESH).wait_recv()
        for cp in cps:
            cp.wait_send()
        for cp in local:
            cp.wait()

    return pl.pallas_call(
        body, name=name, in_specs=[_ANY] * n, out_specs=[_ANY] * n,
        out_shape=[jax.ShapeDtypeStruct(v.shape, v.dtype) for v in ps],
        scratch_shapes=[pltpu.SemaphoreType.DMA((3 * n,)), pltpu.SemaphoreType.DMA((3 * n,)),
                        pltpu.SemaphoreType.DMA((n,))],
    )(*ps)


def _adamw(parts, w, m, v, name):
    ns, rows, l = parts.shape
    tr = max(d for d in range(8, 257, 8) if rows % d == 0)
    c1 = 1.0 - ADAM_B1 ** ADAM_STEP
    c2 = 1.0 - ADAM_B2 ** ADAM_STEP

    def body(p_ref, w_ref, m_ref, v_ref, g_ref, d_ref, nm_ref, nv_ref):
        g = p_ref[0]
        for k in range(1, ns):
            g = g + p_ref[k]
        g_ref[...] = g
        nm = ADAM_B1 * m_ref[...] + (1.0 - ADAM_B1) * g
        nv = ADAM_B2 * v_ref[...] + (1.0 - ADAM_B2) * (g * g)
        nm_ref[...] = nm
        nv_ref[...] = nv
        d_ref[...] = -ADAM_LR * ((nm / c1) / (jnp.sqrt(nv / c2) + ADAM_EPS) + ADAM_WD * w_ref[...])

    row = pl.BlockSpec((tr, l), lambda i: (i, 0))
    return pl.pallas_call(
        body, name=name, grid=(rows // tr,),
        in_specs=[pl.BlockSpec((ns, tr, l), lambda i: (0, i, 0)), row, row, row],
        out_specs=[row] * 4, out_shape=[jax.ShapeDtypeStruct((rows, l), F32)] * 4, compiler_params=_ARB,
    )(parts, w, m, v)


_SMALL = ("ln_in_g", "ln_in_b", "b_in", "sinks", "vn_g", "vn_b", "w_s", "b_s", "b_out", "ln_g", "ln_b")


def _pack_small(d):
    out = []
    for n in _SMALL:
        f = d[n].reshape(-1)
        out.append(jnp.pad(f, (0, (-f.shape[0]) % 1024)))
    return jnp.concatenate(out).reshape(-1, 128)


def _unpack_small(p, like):
    flat, off, out = p.reshape(-1), 0, {}
    for n in _SMALL:
        size = like[n].size
        out[n] = flat[off:off + size].reshape(like[n].shape)
        off += size + (-size) % 1024
    return out


def _owner_major(g, axis):
    sh = g.shape
    g = g.reshape(sh[:axis] + (4, 2, sh[axis] // N_DEV) + sh[axis + 1:])
    g = jnp.moveaxis(g, (axis, axis + 1), (1, 0))
    return g


def kernel(x, ln_in_g, ln_in_b, w_in, b_in, sinks, vn_g, vn_b, w_s, b_s, p_a, p_b, w_out, b_out, ln_g, ln_b, loss_target, m_ln_in_g, m_ln_in_b, m_w_in, m_b_in, m_sinks, m_vn_g, m_vn_b, m_w_s, m_b_s, m_p_a, m_p_b, m_w_out, m_b_out, m_ln_g, m_ln_b, v_ln_in_g, v_ln_in_b, v_w_in, v_b_in, v_sinks, v_vn_g, v_vn_b, v_w_s, v_b_s, v_p_a, v_p_b, v_w_out, v_b_out, v_ln_g, v_ln_b):
    nseq, seq, _ = x.shape
    t = nseq * seq
    nblk_seq = seq // BLK
    x2 = x.reshape(t, D)
    tgt = loss_target.reshape(t, D)

    g_in, g_pa, g_pb, g_wo = _allgather8(
        [w_in.astype(BF16), p_a.astype(BF16), p_b.astype(BF16), w_out.astype(BF16)], "allgather_weights")
    w_full = jnp.moveaxis(g_in, 0, 2).reshape(DEPTH, D, IN_COLS)
    pa_full = jnp.moveaxis(g_pa, 0, 2).reshape(DEPTH, Q_W, D)
    pb_full = jnp.moveaxis(g_pb, 0, 2).reshape(DEPTH, SGU_W, D)
    wo_full = jnp.moveaxis(g_wo, 0, 1).reshape(DEPTH, D, D)
    w_main, w_r = w_full[:, :, :MAIN_W], w_full[:, :, MAIN_W:]
    w_main_t, w_r_t = jnp.swapaxes(w_main, 1, 2), jnp.swapaxes(w_r, 1, 2)
    pa_t, pb_t, wo_t = jnp.swapaxes(pa_full, 1, 2), jnp.swapaxes(pb_full, 1, 2), jnp.swapaxes(wo_full, 1, 2)
    bsb = jnp.broadcast_to(b_s[:, :, :, None], (DEPTH, 4, BLK, BLK))

    xs = [_ln_fwd(x2, ln_in_g, ln_in_b, "ln_in_fwd")]
    saved = []
    for l in range(DEPTH):
        hm, hr = _inproj(xs[l], w_main[l], w_r[l], b_in[l, :MAIN_W].reshape(1, -1), b_in[l, MAIN_W:].reshape(1, -1),
                         f"inproj{l}")
        ya, yb = _mixer_fwd(hm, sinks[l], vn_g[l].reshape(1, -1), vn_b[l].reshape(1, -1), w_s[l], bsb[l],
                            nblk_seq, f"mixer_fwd{l}")
        pa, pb, merged, z, xn = _tail_fwd(xs[l], ya, yb, hr, pa_full[l], pb_full[l], wo_full[l],
                                          b_out[l].reshape(1, D), ln_g[l].reshape(1, D), ln_b[l].reshape(1, D),
                                          f"tail_fwd{l}")
        saved.append((hm, hr, ya, yb, pa, pb, merged, z))
        xs.append(xn)

    dx, sq = _loss_head(xs[DEPTH], tgt, "loss_head")
    loss = lax.psum(sq[0, 0] * (0.5 / D), ("x", "y", "c"))

    big = {"w_in": [], "p_a": [], "p_b": [], "w_out": []}
    small = {n: [] for n in _SMALL}
    for l in reversed(range(DEPTH)):
        hm, hr, ya, yb, pa, pb, merged, z = saved[l]
        dz, dpa, dpb, dhr, dya, dyb, acc, gbr = _tail_bwd(
            dx, z, pa, pb, hr, wo_t[l], pa_t[l], pb_t[l], ln_g[l].reshape(1, D), f"tail_bwd{l}")
        dhm, gbm, gsk, gvn, gws, gbs = _mixer_bwd(
            hm, dya, dyb, sinks[l], vn_g[l].reshape(1, -1), vn_b[l].reshape(1, -1), w_s[l], bsb[l],
            nblk_seq, f"mixer_bwd{l}")
        dx = _dx_inproj(dz, dhm, dhr, w_main_t[l], w_r_t[l], f"dx_inproj{l}")
        gw = jnp.concatenate([_wgrad(xs[l], dhm, MAIN_W // 2, f"wgrad_in_main{l}"),
                              _wgrad(xs[l], dhr, R_W // 2, f"wgrad_in_route{l}")], axis=1)
        big["w_in"].append(gw)
        big["p_a"].append(_wgrad(ya, dpa, D, f"wgrad_pa{l}"))
        big["p_b"].append(_wgrad(yb, dpb, D, f"wgrad_pb{l}"))
        big["w_out"].append(_wgrad(merged, dz, D, f"wgrad_out{l}"))
        small["b_in"].append(jnp.concatenate([gbm[0], gbr[0]]))
        small["sinks"].append(gsk[:, 0])
        small["vn_g"].append(gvn[0])
        small["vn_b"].append(gvn[1])
        small["w_s"].append(gws)
        small["b_s"].append(gbs[:, :, 0])
        small["ln_g"].append(acc[0])
        small["ln_b"].append(acc[1])
        small["b_out"].append(acc[2])
    grad_x, acc_in = _ln_in_bwd(dx, x2, ln_in_g.reshape(1, D), "ln_in_bwd")
    part_small = {n: jnp.stack(v[::-1]) for n, v in small.items() if v}
    part_small["ln_in_g"], part_small["ln_in_b"] = acc_in[0], acc_in[1]

    gi = _owner_major(jnp.stack(big["w_in"][::-1]), 2)
    gpa = _owner_major(jnp.stack(big["p_a"][::-1]), 2)
    gpb = _owner_major(jnp.stack(big["p_b"][::-1]), 2)
    gwo = _owner_major(jnp.stack(big["w_out"][::-1]), 1)
    parts = [gi.reshape(2, 4, DEPTH * D, SHARD_COLS), gpa.reshape(2, 4, DEPTH * Q_W, 128),
             gpb.reshape(2, 4, DEPTH * SGU_W, 128), gwo.reshape(2, 4, DEPTH * 128, D)]
    from_sib = _swap_sibling(parts, "rs_sibling")
    pair = [_pair_sum(g, r, f"pair_sum{a}") for a, (g, r) in enumerate(zip(parts, from_sib))]
    by_chip = _exchange_chips(pair, "rs_chips")
    names = ("w_in", "p_a", "p_b", "w_out")
    given = {"w_in": (w_in, m_w_in, v_w_in), "p_a": (p_a, m_p_a, v_p_a), "p_b": (p_b, m_p_b, v_p_b),
             "w_out": (w_out, m_w_out, v_w_out)}
    res = {}
    for a, n in enumerate(names):
        rows, lanes = by_chip[a].shape[1:]
        outs = _adamw(by_chip[a], *[v.reshape(rows, lanes) for v in given[n]], f"adamw_{n}")
        res[n] = [o.reshape(given[n][0].shape) for o in outs]

    w_small = dict(ln_in_g=ln_in_g, ln_in_b=ln_in_b, b_in=b_in, sinks=sinks, vn_g=vn_g, vn_b=vn_b, w_s=w_s, b_s=b_s,
                   b_out=b_out, ln_g=ln_g, ln_b=ln_b)
    m_small = dict(ln_in_g=m_ln_in_g, ln_in_b=m_ln_in_b, b_in=m_b_in, sinks=m_sinks, vn_g=m_vn_g, vn_b=m_vn_b,
                   w_s=m_w_s, b_s=m_b_s, b_out=m_b_out, ln_g=m_ln_g, ln_b=m_ln_b)
    v_small = dict(ln_in_g=v_ln_in_g, ln_in_b=v_ln_in_b, b_in=v_b_in, sinks=v_sinks, vn_g=v_vn_g, vn_b=v_vn_b,
                   w_s=v_w_s, b_s=v_b_s, b_out=v_b_out, ln_g=v_ln_g, ln_b=v_ln_b)
    (all_small,) = _allgather8([_pack_small(part_small)], "allgather_small")
    outs = _adamw(all_small, _pack_small(w_small), _pack_small(m_small), _pack_small(v_small), "adamw_small")
    for k, o in enumerate(outs):
        u = _unpack_small(o, w_small)
        for n in _SMALL:
            res.setdefault(n, [None] * 4)[k] = u[n]

    order = ("ln_in_g", "ln_in_b", "w_in", "b_in", "sinks", "vn_g", "vn_b", "w_s", "b_s", "p_a", "p_b", "w_out",
             "b_out", "ln_g", "ln_b")
    return (loss, grad_x.reshape(x.shape), *[res[n][0] for n in order], *[res[n][1] for n in order],
            *[res[n][2] for n in order], *[res[n][3] for n in order])
```

```python
import jax
import jax.numpy as jnp
from jax import lax
from jax.experimental import pallas as pl
from jax.experimental.pallas import tpu as pltpu

F32 = jnp.float32
BF16 = jnp.bfloat16

D = 1024
BLK = 128
N_KV = 2
Q_W, KV_W, SGU_W = 512, 128, 512
C_Q, C_K, C_V, C_GA, C_UB, C_VB, C_GB = 0, 512, 640, 768, 1280, 1792, 2304
MAIN_W = 2816
R_W = 2048
IN_COLS = MAIN_W + R_W
N_DEV = 8
SHARD_COLS = IN_COLS // N_DEV

DEPTH = 2
ALPHA = (2.0 * DEPTH) ** 0.25
LN_EPS = 1e-5
ATTN_SCALE = 0.125
NEG = float(jnp.finfo(jnp.float32).min)

ADAM_LR, ADAM_B1, ADAM_B2, ADAM_EPS, ADAM_WD, ADAM_STEP = 0.001, 0.9, 0.999, 1e-08, 0.01, 10

TM = 256
NB = TM // BLK
MESH = pl.DeviceIdType.MESH
VMEM_LIMIT = 56 * 1024 * 1024

_ARB = pltpu.CompilerParams(dimension_semantics=("arbitrary",), vmem_limit_bytes=VMEM_LIMIT)


def _sigmoid(x):
    return 1.0 / (1.0 + jnp.exp(-x))


_GELU_C = 0.7978845608028654
_GELU_A = 0.044715


def _gelu(x):
    return 0.5 * x * (1.0 + jnp.tanh(_GELU_C * (x + _GELU_A * x * x * x)))


def _dgelu(x):
    t = jnp.tanh(_GELU_C * (x + _GELU_A * x * x * x))
    return 0.5 * (1.0 + t) + 0.5 * x * (1.0 - t * t) * (_GELU_C * (1.0 + 3.0 * _GELU_A * x * x))


def _ln_stats(x):
    mu = jnp.mean(x, axis=-1, keepdims=True)
    xc = x - mu
    var = jnp.mean(xc * xc, axis=-1, keepdims=True)
    rstd = lax.rsqrt(var + LN_EPS)
    return xc * rstd, rstd


def _ln_bwd(dy_g, xhat, rstd):
    m1 = jnp.mean(dy_g, axis=-1, keepdims=True)
    m2 = jnp.mean(dy_g * xhat, axis=-1, keepdims=True)
    return rstd * (dy_g - m1 - xhat * m2)


def _colsum(x):
    return jnp.sum(x, axis=0, keepdims=True)


def _dot(a, b):
    return jnp.dot(a, b, preferred_element_type=F32)


def _dot_nt(a, b):
    return lax.dot_general(a, b, (((1,), (1,)), ((), ())), preferred_element_type=F32)


def _dot_tn(a, b):
    return lax.dot_general(a, b, (((0,), (0,)), ((), ())), preferred_element_type=F32)


def _head_place(hk, g):
    j = 4 * hk + g
    return j, j // 2, j % 2


def _head_rows(x, hk):
    d = lax.broadcasted_iota(jnp.int32, x.shape, 0)
    return jnp.where((d >= 64 * hk) & (d < 64 * hk + 64), x, 0.0).astype(BF16)


def _head_lanes(x, hk):
    d = lax.broadcasted_iota(jnp.int32, x.shape, 1)
    return jnp.where((d >= 64 * hk) & (d < 64 * hk + 64), x, 0.0)


def _attn_group(q, kband, vband_t, hk, sinks_ref, lower):
    kh = _head_lanes(kband, hk).astype(BF16)
    parts = []
    for g in range(4):
        _, p, pos = _head_place(hk, g)
        qp = q[:, BLK * p:BLK * (p + 1)] * ATTN_SCALE
        if pos != hk:
            qp = pltpu.roll(qp, 64, 1)
        parts.append(qp.astype(BF16))
    q4 = jnp.concatenate(parts, axis=0)
    s_t = _dot_nt(kh, q4)
    kpos = lax.broadcasted_iota(jnp.int32, (2 * BLK, 4 * BLK), 0)
    row = lax.broadcasted_iota(jnp.int32, (2 * BLK, 4 * BLK), 1) & (BLK - 1)
    valid = (kpos > row) & (kpos <= row + BLK) & (kpos >= lower)
    s_t = jnp.where(valid, s_t, NEG)
    sink_row = jnp.concatenate(
        [jnp.full((1, BLK), sinks_ref[4 * hk + g], F32) for g in range(4)], axis=1)
    m = jnp.maximum(jnp.max(s_t, axis=0, keepdims=True), sink_row)
    p_un = jnp.exp(s_t - m)
    e_sink = jnp.exp(sink_row - m)
    inv = 1.0 / (jnp.sum(p_un, axis=0, keepdims=True) + e_sink)
    prob_t = p_un * inv
    o_t = _dot(_head_rows(vband_t, hk), prob_t.astype(BF16))
    return q4, kh, prob_t, e_sink * inv, o_t


def _unstack_heads(x4, hk, pairs):
    for g in range(4):
        _, p, pos = _head_place(hk, g)
        xg = x4[BLK * g:BLK * (g + 1)]
        if pos != hk:
            xg = pltpu.roll(xg, 64, 1)
        pairs[p] = xg if pairs[p] is None else pairs[p] + xg
    return pairs


def _attn_fwd(q, kband, vband, sinks_ref, lower):
    pairs = [None] * 4
    vband_t = vband.T
    for hk in range(N_KV):
        o_t = _attn_group(q, kband, vband_t, hk, sinks_ref, lower)[-1]
        pairs = _unstack_heads(o_t.T, hk, pairs)
    return jnp.concatenate(pairs, axis=1)


def _tril_mask():
    r = lax.broadcasted_iota(jnp.int32, (BLK, BLK), 0)
    c = lax.broadcasted_iota(jnp.int32, (BLK, BLK), 1)
    return c <= r


def _sgu_fwd(u_b, v_b, vn_g, vn_b, wt, bsb_ref):
    u = _gelu(u_b)
    v = _gelu(v_b)
    vhat, rstd = _ln_stats(v)
    vn = vhat * vn_g + vn_b
    mixed = jnp.concatenate(
        [_dot(wt[g], vn[:, BLK * g:BLK * (g + 1)].astype(BF16)) + bsb_ref[g] for g in range(4)], axis=1)
    return u, vhat, rstd, vn, mixed


def _cols(ref, rows, col, width):
    return ref[rows, col:col + width].astype(F32)


def _band(hm_ref, hprev_ref, s, col):
    r0 = s * BLK
    cur = hm_ref[r0:r0 + BLK, col:col + KV_W]
    if s == 0:
        off = 0 if col == C_K else KV_W
        prev = hprev_ref[:, off:off + KV_W]
    else:
        prev = hm_ref[r0 - BLK:r0, col:col + KV_W]
    return jnp.concatenate([prev, cur], axis=0).astype(F32)


def _mixer_in_specs(nt, rev):
    def tile(g):
        return nt - 1 - g if rev else g

    return [
        pl.BlockSpec(memory_space=pltpu.SMEM),
        pl.BlockSpec((TM, MAIN_W), lambda g: (tile(g), 0)),
        pl.BlockSpec((BLK, 2 * KV_W), lambda g: (jnp.maximum(tile(g) * NB - 1, 0), 2)),
    ]


_CONST2 = lambda g: (0, 0)
_CONST3 = lambda g: (0, 0, 0)


def _ln_fwd(x, g, b, name):
    t = x.shape[0]

    def body(x_ref, g_ref, b_ref, o_ref):
        xhat, _ = _ln_stats(x_ref[...])
        o_ref[...] = xhat * g_ref[...] + b_ref[...]

    return pl.pallas_call(
        body, name=name, grid=(t // TM,),
        in_specs=[pl.BlockSpec((TM, D), lambda i: (i, 0)), pl.BlockSpec((1, D), _CONST2), pl.BlockSpec((1, D), _CONST2)],
        out_specs=pl.BlockSpec((TM, D), lambda i: (i, 0)),
        out_shape=jax.ShapeDtypeStruct((t, D), F32), compiler_params=_ARB,
    )(x, g.reshape(1, D), b.reshape(1, D))


def _inproj(x, wm, wr, bm, br, name):
    t = x.shape[0]

    def body(x_ref, wm_ref, wr_ref, bm_ref, br_ref, hm_ref, hr_ref):
        xb = x_ref[...].astype(BF16)
        hm_ref[...] = (_dot(xb, wm_ref[...]) + bm_ref[...]).astype(BF16)
        hr_ref[...] = (_dot(xb, wr_ref[...]) + br_ref[...]).astype(BF16)

    return pl.pallas_call(
        body, name=name, grid=(t // TM,),
        in_specs=[pl.BlockSpec((TM, D), lambda i: (i, 0)),
                  pl.BlockSpec((D, MAIN_W), _CONST2), pl.BlockSpec((D, R_W), _CONST2),
                  pl.BlockSpec((1, MAIN_W), _CONST2), pl.BlockSpec((1, R_W), _CONST2)],
        out_specs=[pl.BlockSpec((TM, MAIN_W), lambda i: (i, 0)), pl.BlockSpec((TM, R_W), lambda i: (i, 0))],
        out_shape=[jax.ShapeDtypeStruct((t, MAIN_W), BF16), jax.ShapeDtypeStruct((t, R_W), BF16)],
        compiler_params=_ARB,
    )(x, wm, wr, bm, br)


def _mixer_fwd(hm, sinks, vn_g, vn_b, w_s, bsb, nblk_seq, name):
    t = hm.shape[0]
    nt = t // TM

    def body(sinks_ref, hm_ref, hprev_ref, vng_ref, vnb_ref, ws_ref, bsb_ref, ya_ref, yb_ref):
        i = pl.program_id(0)
        tril = _tril_mask()
        wt = [jnp.where(tril, ws_ref[g], 0.0).astype(BF16) for g in range(4)]
        for s in range(NB):
            r0 = s * BLK
            rows = slice(r0, r0 + BLK)
            lower = jnp.where((i * NB + s) % nblk_seq == 0, BLK, 0)
            attn = _attn_fwd(_cols(hm_ref, rows, C_Q, Q_W), _band(hm_ref, hprev_ref, s, C_K),
                             _band(hm_ref, hprev_ref, s, C_V), sinks_ref, lower)
            g_a = _cols(hm_ref, rows, C_GA, Q_W)
            ya_ref[rows, :] = (attn * (g_a * _sigmoid(g_a))).astype(BF16)
            u, _, _, _, mixed = _sgu_fwd(_cols(hm_ref, rows, C_UB, SGU_W), _cols(hm_ref, rows, C_VB, SGU_W),
                                         vng_ref[...], vnb_ref[...], wt, bsb_ref)
            g_b = _cols(hm_ref, rows, C_GB, SGU_W)
            yb_ref[rows, :] = (u * mixed * (g_b * _sigmoid(g_b))).astype(BF16)

    return pl.pallas_call(
        body, name=name, grid=(nt,),
        in_specs=_mixer_in_specs(nt, False) + [
            pl.BlockSpec((1, SGU_W), _CONST2), pl.BlockSpec((1, SGU_W), _CONST2),
            pl.BlockSpec((4, BLK, BLK), _CONST3), pl.BlockSpec((4, BLK, BLK), _CONST3)],
        out_specs=[pl.BlockSpec((TM, Q_W), lambda i: (i, 0)), pl.BlockSpec((TM, SGU_W), lambda i: (i, 0))],
        out_shape=[jax.ShapeDtypeStruct((t, Q_W), BF16), jax.ShapeDtypeStruct((t, SGU_W), BF16)],
        compiler_params=_ARB,
    )(sinks, hm, hm, vn_g, vn_b, w_s, bsb)


def _tail_fwd(x, ya, yb, hr, pa_w, pb_w, wo, b_out, ln_g, ln_b, name):
    t = x.shape[0]

    def body(x_ref, ya_ref, yb_ref, hr_ref, paw_ref, pbw_ref, wo_ref, bo_ref, g_ref, b_ref,
             pa_ref, pb_ref, mg_ref, z_ref, xn_ref):
        pa = _dot(ya_ref[...], paw_ref[...])
        pb = _dot(yb_ref[...], pbw_ref[...])
        pa_ref[...] = pa.astype(BF16)
        pb_ref[...] = pb.astype(BF16)
        everything = slice(None)
        merged = _sigmoid(_cols(hr_ref, everything, 0, D)) * pa + _sigmoid(_cols(hr_ref, everything, D, D)) * pb
        mb = merged.astype(BF16)
        mg_ref[...] = mb
        z = ALPHA * x_ref[...] + (_dot(mb, wo_ref[...]) + bo_ref[...])
        z_ref[...] = z
        zhat, _ = _ln_stats(z)
        xn_ref[...] = zhat * g_ref[...] + b_ref[...]

    row = lambda w: pl.BlockSpec((TM, w), lambda i: (i, 0))
    vec = pl.BlockSpec((1, D), _CONST2)
    return pl.pallas_call(
        body, name=name, grid=(t // TM,),
        in_specs=[row(D), row(Q_W), row(SGU_W), row(R_W),
                  pl.BlockSpec((Q_W, D), _CONST2), pl.BlockSpec((SGU_W, D), _CONST2), pl.BlockSpec((D, D), _CONST2),
                  vec, vec, vec],
        out_specs=[row(D), row(D), row(D), row(D), row(D)],
        out_shape=[jax.ShapeDtypeStruct((t, D), BF16), jax.ShapeDtypeStruct((t, D), BF16),
                   jax.ShapeDtypeStruct((t, D), BF16), jax.ShapeDtypeStruct((t, D), F32),
                   jax.ShapeDtypeStruct((t, D), F32)],
        compiler_params=_ARB,
    )(x, ya, yb, hr, pa_w, pb_w, wo, b_out, ln_g, ln_b)


def _loss_head(y, target, name):
    t = y.shape[0]

    def body(y_ref, t_ref, dy_ref, acc_ref):
        @pl.when(pl.program_id(0) == 0)
        def _():
            acc_ref[...] = jnp.zeros_like(acc_ref)

        err = y_ref[...] - t_ref[...]
        dy_ref[...] = err * (1.0 / D)
        acc_ref[...] += jnp.sum(jnp.sum(err * err, axis=1, keepdims=True), axis=0, keepdims=True)

    return pl.pallas_call(
        body, name=name, grid=(t // TM,),
        in_specs=[pl.BlockSpec((TM, D), lambda i: (i, 0)), pl.BlockSpec((TM, D), lambda i: (i, 0))],
        out_specs=[pl.BlockSpec((TM, D), lambda i: (i, 0)), pl.BlockSpec((8, 128), _CONST2)],
        out_shape=[jax.ShapeDtypeStruct((t, D), F32), jax.ShapeDtypeStruct((8, 128), F32)],
        compiler_params=_ARB,
    )(y, target)


def _tail_bwd(dxn, z, pa, pb, hr, wo_t, pa_t, pb_t, ln_g, name):
    t = dxn.shape[0]

    def body(dxn_ref, z_ref, pa_ref, pb_ref, hr_ref, wot_ref, pat_ref, pbt_ref, g_ref,
             dz_ref, dpa_ref, dpb_ref, dhr_ref, dya_ref, dyb_ref, acc_ref, gbr_ref):
        @pl.when(pl.program_id(0) == 0)
        def _():
            acc_ref[...] = jnp.zeros_like(acc_ref)
            gbr_ref[...] = jnp.zeros_like(gbr_ref)

        dxn_v = dxn_ref[...]
        zhat, rstd = _ln_stats(z_ref[...])
        dz = _ln_bwd(dxn_v * g_ref[...], zhat, rstd)
        dz_ref[...] = dz
        acc_ref[0:1, :] += _colsum(dxn_v * zhat)
        acc_ref[1:2, :] += _colsum(dxn_v)
        acc_ref[2:3, :] += _colsum(dz)
        dmerged = _dot(dz.astype(BF16), wot_ref[...])
        everything = slice(None)
        sa = _sigmoid(_cols(hr_ref, everything, 0, D))
        sb = _sigmoid(_cols(hr_ref, everything, D, D))
        dpa = (dmerged * sa).astype(BF16)
        dpb = (dmerged * sb).astype(BF16)
        dpa_ref[...] = dpa
        dpb_ref[...] = dpb
        dra = dmerged * pa_ref[...].astype(F32) * (sa * (1.0 - sa))
        drb = dmerged * pb_ref[...].astype(F32) * (sb * (1.0 - sb))
        dhr_ref[:, 0:D] = dra.astype(BF16)
        dhr_ref[:, D:2 * D] = drb.astype(BF16)
        gbr_ref[0:1, 0:D] += _colsum(dra)
        gbr_ref[0:1, D:2 * D] += _colsum(drb)
        dya_ref[...] = _dot(dpa, pat_ref[...]).astype(BF16)
        dyb_ref[...] = _dot(dpb, pbt_ref[...]).astype(BF16)

    row = lambda w: pl.BlockSpec((TM, w), lambda i: (i, 0))
    return pl.pallas_call(
        body, name=name, grid=(t // TM,),
        in_specs=[row(D), row(D), row(D), row(D), row(R_W),
                  pl.BlockSpec((D, D), _CONST2), pl.BlockSpec((D, Q_W), _CONST2), pl.BlockSpec((D, SGU_W), _CONST2),
                  pl.BlockSpec((1, D), _CONST2)],
        out_specs=[row(D), row(D), row(D), row(R_W), row(Q_W), row(SGU_W),
                   pl.BlockSpec((8, D), _CONST2), pl.BlockSpec((8, R_W), _CONST2)],
        out_shape=[jax.ShapeDtypeStruct((t, D), F32), jax.ShapeDtypeStruct((t, D), BF16),
                   jax.ShapeDtypeStruct((t, D), BF16), jax.ShapeDtypeStruct((t, R_W), BF16),
                   jax.ShapeDtypeStruct((t, Q_W), BF16), jax.ShapeDtypeStruct((t, SGU_W), BF16),
                   jax.ShapeDtypeStruct((8, D), F32), jax.ShapeDtypeStruct((8, R_W), F32)],
        compiler_params=_ARB,
    )(dxn, z, pa, pb, hr, wo_t, pa_t, pb_t, ln_g)


def _mixer_bwd(hm, dya, dyb, sinks, vn_g, vn_b, w_s, bsb, nblk_seq, name):
    t = hm.shape[0]
    nt = t // TM

    def body(sinks_ref, hm_ref, hprev_ref, dya_ref, dyb_ref, vng_ref, vnb_ref, ws_ref, bsb_ref,
             dhm_ref, gbm_ref, gsk_ref, gvn_ref, gws_ref, gbs_ref, dk_carry, dv_carry):
        gi = pl.program_id(0)
        i = nt - 1 - gi

        @pl.when(gi == 0)
        def _():
            for r in (gbm_ref, gsk_ref, gvn_ref, gws_ref, gbs_ref, dk_carry, dv_carry):
                r[...] = jnp.zeros_like(r)

        tril = _tril_mask()
        wt = [jnp.where(tril, ws_ref[g], 0.0).astype(BF16) for g in range(4)]
        vng = vng_ref[...]

        def put(rows, col, val):
            dhm_ref[rows, col:col + val.shape[1]] = val.astype(BF16)
            gbm_ref[0:1, col:col + val.shape[1]] += _colsum(val)

        for s in reversed(range(NB)):
            r0 = s * BLK
            rows = slice(r0, r0 + BLK)
            lower = jnp.where((i * NB + s) % nblk_seq == 0, BLK, 0)
            q = _cols(hm_ref, rows, C_Q, Q_W)
            kband = _band(hm_ref, hprev_ref, s, C_K)
            vband = _band(hm_ref, hprev_ref, s, C_V)
            g_a = _cols(hm_ref, rows, C_GA, Q_W)
            sg = _sigmoid(g_a)
            dya_v = _cols(dya_ref, rows, 0, Q_W)
            d_o = dya_v * (g_a * sg)
            o_pairs, dq_pairs = [None] * 4, [None] * 4
            dkband = jnp.zeros((2 * BLK, KV_W), F32)
            dvband = jnp.zeros((2 * BLK, KV_W), F32)
            kband_t, vband_t = kband.T, vband.T
            for hk in range(N_KV):
                q4, kh, prob_t, p_sink, o_t = _attn_group(q, kband, vband_t, hk, sinks_ref, lower)
                o_pairs = _unstack_heads(o_t.T, hk, o_pairs)
                parts = []
                for g in range(4):
                    _, p, pos = _head_place(hk, g)
                    dp = d_o[:, BLK * p:BLK * (p + 1)]
                    parts.append(pltpu.roll(dp, 64, 1) if pos != hk else dp)
                do4 = _head_lanes(jnp.concatenate(parts, axis=0), hk)
                do4b = do4.astype(BF16)
                delta = _colsum(do4.T * o_t)
                vh = _head_lanes(vband, hk).astype(BF16)
                ds_t = prob_t * (_dot_nt(vh, do4b) - delta)
                dsb = ds_t.astype(BF16)
                dq4_t = _dot(_head_rows(kband_t, hk), dsb)
                dq_pairs = _unstack_heads(dq4_t.T * ATTN_SCALE, hk, dq_pairs)
                dkband = dkband + _head_lanes(_dot(dsb, q4), hk)
                dvband = dvband + _dot(prob_t.astype(BF16), do4b)
                dsk = p_sink * delta
                for g in range(4):
                    j = 4 * hk + g
                    tot = jnp.sum(dsk[:, BLK * g:BLK * (g + 1)], axis=1, keepdims=True)
                    gsk_ref[j:j + 1, :] += jnp.broadcast_to(-tot, (1, 128))
            attn = jnp.concatenate(o_pairs, axis=1)
            put(rows, C_Q, jnp.concatenate(dq_pairs, axis=1))
            put(rows, C_K, dkband[BLK:2 * BLK] + dk_carry[...])
            put(rows, C_V, dvband[BLK:2 * BLK] + dv_carry[...])
            dk_carry[...] = dkband[0:BLK]
            dv_carry[...] = dvband[0:BLK]
            put(rows, C_GA, dya_v * attn * (sg * (1.0 + g_a * (1.0 - sg))))
            u_b = _cols(hm_ref, rows, C_UB, SGU_W)
            v_b = _cols(hm_ref, rows, C_VB, SGU_W)
            g_b = _cols(hm_ref, rows, C_GB, SGU_W)
            u, vhat, rstd, vn, mixed = _sgu_fwd(u_b, v_b, vng, vnb_ref[...], wt, bsb_ref)
            sgb = _sigmoid(g_b)
            silu_b = g_b * sgb
            dyb_v = _cols(dyb_ref, rows, 0, SGU_W)
            du = dyb_v * mixed * silu_b
            dmixed = dyb_v * u * silu_b
            put(rows, C_GB, dyb_v * u * mixed * (sgb * (1.0 + g_b * (1.0 - sgb))))
            dvn_parts = []
            for g in range(4):
                cols = slice(BLK * g, BLK * (g + 1))
                dmg = dmixed[:, cols]
                dmgb = dmg.astype(BF16)
                dvn_parts.append(_dot_tn(wt[g], dmgb))
                gws_ref[g] += jnp.where(tril, _dot_nt(dmgb, vn[:, cols].astype(BF16)), 0.0)
                gbs_ref[g] += dmg
            dvn = jnp.concatenate(dvn_parts, axis=1)
            gvn_ref[0:1, :] += _colsum(dvn * vhat)
            gvn_ref[1:2, :] += _colsum(dvn)
            dv = _ln_bwd(dvn * vng, vhat, rstd)
            put(rows, C_UB, du * _dgelu(u_b))
            put(rows, C_VB, dv * _dgelu(v_b))

        @pl.when(gi == nt - 1)
        def _():
            for g in range(4):
                gbs_ref[g] = jnp.broadcast_to(jnp.sum(gbs_ref[g], axis=1, keepdims=True), (BLK, BLK))

    row = lambda w: pl.BlockSpec((TM, w), lambda g: (nt - 1 - g, 0))
    return pl.pallas_call(
        body, name=name, grid=(nt,),
        in_specs=_mixer_in_specs(nt, True) + [
            row(Q_W), row(SGU_W),
            pl.BlockSpec((1, SGU_W), _CONST2), pl.BlockSpec((1, SGU_W), _CONST2),
            pl.BlockSpec((4, BLK, BLK), _CONST3), pl.BlockSpec((4, BLK, BLK), _CONST3)],
        out_specs=[row(MAIN_W), pl.BlockSpec((8, MAIN_W), _CONST2), pl.BlockSpec((8, 128), _CONST2),
                   pl.BlockSpec((8, SGU_W), _CONST2), pl.BlockSpec((4, BLK, BLK), _CONST3),
                   pl.BlockSpec((4, BLK, BLK), _CONST3)],
        out_shape=[jax.ShapeDtypeStruct((t, MAIN_W), BF16), jax.ShapeDtypeStruct((8, MAIN_W), F32),
                   jax.ShapeDtypeStruct((8, 128), F32), jax.ShapeDtypeStruct((8, SGU_W), F32),
                   jax.ShapeDtypeStruct((4, BLK, BLK), F32), jax.ShapeDtypeStruct((4, BLK, BLK), F32)],
        scratch_shapes=[pltpu.VMEM((BLK, KV_W), F32), pltpu.VMEM((BLK, KV_W), F32)],
        compiler_params=_ARB,
    )(sinks, hm, hm, dya, dyb, vn_g, vn_b, w_s, bsb)


def _dx_inproj(dz, dhm, dhr, wm_t, wr_t, name):
    t = dz.shape[0]

    def body(dz_ref, dhm_ref, dhr_ref, wmt_ref, wrt_ref, dx_ref):
        dx_ref[...] = ALPHA * dz_ref[...] + _dot(dhm_ref[...], wmt_ref[...]) + _dot(dhr_ref[...], wrt_ref[...])

    row = lambda w: pl.BlockSpec((TM, w), lambda i: (i, 0))
    return pl.pallas_call(
        body, name=name, grid=(t // TM,),
        in_specs=[row(D), row(MAIN_W), row(R_W), pl.BlockSpec((MAIN_W, D), _CONST2), pl.BlockSpec((R_W, D), _CONST2)],
        out_specs=row(D), out_shape=jax.ShapeDtypeStruct((t, D), F32), compiler_params=_ARB,
    )(dz, dhm, dhr, wm_t, wr_t)


def _wgrad(a, b, tn, name, tk=1024):
    t, m = a.shape
    n = b.shape[1]
    nk = t // tk

    def body(a_ref, b_ref, o_ref):
        @pl.when(pl.program_id(1) == 0)
        def _():
            o_ref[...] = jnp.zeros_like(o_ref)

        o_ref[...] += _dot_tn(a_ref[...].astype(BF16), b_ref[...].astype(BF16))

    return pl.pallas_call(
        body, name=name, grid=(n // tn, nk),
        in_specs=[pl.BlockSpec((tk, m), lambda j, k: (k, 0)), pl.BlockSpec((tk, tn), lambda j, k: (k, j))],
        out_specs=pl.BlockSpec((m, tn), lambda j, k: (0, j)),
        out_shape=jax.ShapeDtypeStruct((m, n), F32),
        compiler_params=pltpu.CompilerParams(dimension_semantics=("arbitrary", "arbitrary"), vmem_limit_bytes=VMEM_LIMIT),
    )(a, b)


def _ln_in_bwd(dx0, x, g, name):
    t = x.shape[0]

    def body(dx0_ref, x_ref, g_ref, gx_ref, acc_ref):
        @pl.when(pl.program_id(0) == 0)
        def _():
            acc_ref[...] = jnp.zeros_like(acc_ref)

        d = dx0_ref[...]
        xhat, rstd = _ln_stats(x_ref[...])
        gx_ref[...] = _ln_bwd(d * g_ref[...], xhat, rstd)
        acc_ref[0:1, :] += _colsum(d * xhat)
        acc_ref[1:2, :] += _colsum(d)

    row = pl.BlockSpec((TM, D), lambda i: (i, 0))
    return pl.pallas_call(
        body, name=name, grid=(t // TM,),
        in_specs=[row, row, pl.BlockSpec((1, D), _CONST2)],
        out_specs=[row, pl.BlockSpec((8, D), _CONST2)],
        out_shape=[jax.ShapeDtypeStruct((t, D), F32), jax.ShapeDtypeStruct((8, D), F32)],
        compiler_params=_ARB,
    )(dx0, x, g)


_ANY = pl.BlockSpec(memory_space=pl.ANY)


def _place():
    return lax.axis_index("x"), lax.axis_index("y"), lax.axis_index("c")


def _allgather8(xs, name):
    n = len(xs)

    def body(*refs):
        x_refs, o_refs = refs[:n], refs[n:2 * n]
        send_sems, recv_sems, local_sems = refs[2 * n:]
        x, y, c = _place()
        me, sib = (x, y, c), (x, y, 1 - c)
        chips = [(1 - x, y), (x, 1 - y), (1 - x, 1 - y)]

        def copy(a, k, block, to, src=None):
            dst = o_refs[a].at[4 * block[0] + 2 * block[1] + block[2]]
            return pltpu.make_async_remote_copy(
                src_ref=dst if src is None else src, dst_ref=dst, send_sem=send_sems.at[7 * a + k],
                recv_sem=recv_sems.at[7 * a + k], device_id=to, device_id_type=MESH)

        mine = [pltpu.make_async_copy(x_refs[a], o_refs[a].at[4 * x + 2 * y + c], local_sems.at[a]) for a in range(n)]
        for cp in mine:
            cp.start()
        sent = []
        for a in range(n):
            sent.append(copy(a, 0, me, sib, src=x_refs[a]))
            sent += [copy(a, 1 + j, me, (*chip, c), src=x_refs[a]) for j, chip in enumerate(chips)]
        for cp in sent:
            cp.start()
        for j, chip in enumerate(chips):
            for a in range(n):
                copy(a, 1 + j, (*chip, c), me).wait_recv()
                fwd = copy(a, 4 + j, (*chip, c), sib)
                fwd.start()
                sent.append(fwd)
        for a in range(n):
            copy(a, 0, sib, me).wait_recv()
            for j, chip in enumerate(chips):
                copy(a, 4 + j, (*chip, 1 - c), me).wait_recv()
        for cp in sent:
            cp.wait_send()
        for cp in mine:
            cp.wait()

    return pl.pallas_call(
        body, name=name, in_specs=[_ANY] * n, out_specs=[_ANY] * n,
        out_shape=[jax.ShapeDtypeStruct((N_DEV,) + v.shape, v.dtype) for v in xs],
        scratch_shapes=[pltpu.SemaphoreType.DMA((7 * n,)), pltpu.SemaphoreType.DMA((7 * n,)),
                        pltpu.SemaphoreType.DMA((n,))],
    )(*xs)


def _swap_sibling(gs, name):
    n = len(gs)

    def body(*refs):
        g_refs, r_refs = refs[:n], refs[n:2 * n]
        send_sems, recv_sems = refs[2 * n:]
        x, y, c = _place()
        cps = [pltpu.make_async_remote_copy(
            src_ref=g_refs[a].at[1 - c], dst_ref=r_refs[a], send_sem=send_sems.at[a], recv_sem=recv_sems.at[a],
            device_id=(x, y, 1 - c), device_id_type=MESH) for a in range(n)]
        for cp in cps:
            cp.start()
        for cp in cps:
            cp.wait()

    return pl.pallas_call(
        body, name=name, in_specs=[_ANY] * n, out_specs=[_ANY] * n,
        out_shape=[jax.ShapeDtypeStruct(v.shape[1:], v.dtype) for v in gs],
        scratch_shapes=[pltpu.SemaphoreType.DMA((n,)), pltpu.SemaphoreType.DMA((n,))],
    )(*gs)


def _pair_sum(g, r, name):
    _, _, rows, l = g.shape
    tr = min(rows, 512)

    def body(c_ref, g_ref, r_ref, o_ref):
        o_ref[...] = g_ref[0] + r_ref[...]

    return pl.pallas_call(
        body, name=name,
        grid_spec=pltpu.PrefetchScalarGridSpec(
            num_scalar_prefetch=1, grid=(4, rows // tr),
            in_specs=[pl.BlockSpec((1, 1, tr, l), lambda q, i, c_ref: (c_ref[0], q, i, 0)),
                      pl.BlockSpec((1, tr, l), lambda q, i, c_ref: (q, i, 0))],
            out_specs=pl.BlockSpec((1, tr, l), lambda q, i, c_ref: (q, i, 0))),
        out_shape=jax.ShapeDtypeStruct((4, rows, l), F32),
        compiler_params=pltpu.CompilerParams(dimension_semantics=("arbitrary", "arbitrary"), vmem_limit_bytes=VMEM_LIMIT),
    )(lax.axis_index("c").astype(jnp.int32).reshape(1), g, r)


def _exchange_chips(ps, name):
    n = len(ps)

    def body(*refs):
        p_refs, r_refs = refs[:n], refs[n:2 * n]
        send_sems, recv_sems, local_sems = refs[2 * n:]
        x, y, c = _place()
        mine = 2 * x + y
        chips = [(1 - x, y), (x, 1 - y), (1 - x, 1 - y)]
        local = [pltpu.make_async_copy(p_refs[a].at[mine], r_refs[a].at[mine], local_sems.at[a]) for a in range(n)]
        for cp in local:
            cp.start()
        cps = []
        for a in range(n):
            for j, (qx, qy) in enumerate(chips):
                cps.append(pltpu.make_async_remote_copy(
                    src_ref=p_refs[a].at[2 * qx + qy], dst_ref=r_refs[a].at[mine], send_sem=send_sems.at[3 * a + j],
                    recv_sem=recv_sems.at[3 * a + j], device_id=(qx, qy, c), device_id_type=MESH))
        for cp in cps:
            cp.start()
        for a in range(n):
            for j, (qx, qy) in enumerate(chips):
                pltpu.make_async_remote_copy(
                    src_ref=p_refs[a].at[mine], dst_ref=r_refs[a].at[2 * qx + qy], send_sem=send_sems.at[3 * a + j],
                    recv_sem=recv_sems.at[3 * a + j], device_id=(qx, qy, c), device_id_type=MESH).wait_recv()
        for cp in cps:
            cp.wait_send()
        for cp in local:
            cp.wait()

    return pl.pallas_call(
        body, name=name, in_specs=[_ANY] * n, out_specs=[_ANY] * n,
        out_shape=[jax.ShapeDtypeStruct(v.shape, v.dtype) for v in ps],
        scratch_shapes=[pltpu.SemaphoreType.DMA((3 * n,)), pltpu.SemaphoreType.DMA((3 * n,)),
                        pltpu.SemaphoreType.DMA((n,))],
    )(*ps)


def _adamw(parts, w, m, v, name):
    ns, rows, l = parts.shape
    tr = max(d for d in range(8, 257, 8) if rows % d == 0)
    c1 = 1.0 - ADAM_B1 ** ADAM_STEP
    c2 = 1.0 - ADAM_B2 ** ADAM_STEP

    def body(p_ref, w_ref, m_ref, v_ref, g_ref, d_ref, nm_ref, nv_ref):
        g = p_ref[0]
        for k in range(1, ns):
            g = g + p_ref[k]
        g_ref[...] = g
        nm = ADAM_B1 * m_ref[...] + (1.0 - ADAM_B1) * g
        nv = ADAM_B2 * v_ref[...] + (1.0 - ADAM_B2) * (g * g)
        nm_ref[...] = nm
        nv_ref[...] = nv
        d_ref[...] = -ADAM_LR * ((nm / c1) / (jnp.sqrt(nv / c2) + ADAM_EPS) + ADAM_WD * w_ref[...])

    row = pl.BlockSpec((tr, l), lambda i: (i, 0))
    return pl.pallas_call(
        body, name=name, grid=(rows // tr,),
        in_specs=[pl.BlockSpec((ns, tr, l), lambda i: (0, i, 0)), row, row, row],
        out_specs=[row] * 4, out_shape=[jax.ShapeDtypeStruct((rows, l), F32)] * 4, compiler_params=_ARB,
    )(parts, w, m, v)


_SMALL = ("ln_in_g", "ln_in_b", "b_in", "sinks", "vn_g", "vn_b", "w_s", "b_s", "b_out", "ln_g", "ln_b")


def _pack_small(d):
    out = []
    for n in _SMALL:
        f = d[n].reshape(-1)
        out.append(jnp.pad(f, (0, (-f.shape[0]) % 1024)))
    return jnp.concatenate(out).reshape(-1, 128)


def _unpack_small(p, like):
    flat, off, out = p.reshape(-1), 0, {}
    for n in _SMALL:
        size = like[n].size
        out[n] = flat[off:off + size].reshape(like[n].shape)
        off += size + (-size) % 1024
    return out


def _owner_major(g, axis):
    sh = g.shape
    g = g.reshape(sh[:axis] + (4, 2, sh[axis] // N_DEV) + sh[axis + 1:])
    g = jnp.moveaxis(g, (axis, axis + 1), (1, 0))
    return g


def kernel(x, ln_in_g, ln_in_b, w_in, b_in, sinks, vn_g, vn_b, w_s, b_s, p_a, p_b, w_out, b_out, ln_g, ln_b, loss_target, m_ln_in_g, m_ln_in_b, m_w_in, m_b_in, m_sinks, m_vn_g, m_vn_b, m_w_s, m_b_s, m_p_a, m_p_b, m_w_out, m_b_out, m_ln_g, m_ln_b, v_ln_in_g, v_ln_in_b, v_w_in, v_b_in, v_sinks, v_vn_g, v_vn_b, v_w_s, v_b_s, v_p_a, v_p_b, v_w_out, v_b_out, v_ln_g, v_ln_b):
    nseq, seq, _ = x.shape
    t = nseq * seq
    nblk_seq = seq // BLK
    x2 = x.reshape(t, D)
    tgt = loss_target.reshape(t, D)

    g_in, g_pa, g_pb, g_wo = _allgather8(
        [w_in.astype(BF16), p_a.astype(BF16), p_b.astype(BF16), w_out.astype(BF16)], "allgather_weights")
    w_full = jnp.moveaxis(g_in, 0, 2).reshape(DEPTH, D, IN_COLS)
    pa_full = jnp.moveaxis(g_pa, 0, 2).reshape(DEPTH, Q_W, D)
    pb_full = jnp.moveaxis(g_pb, 0, 2).reshape(DEPTH, SGU_W, D)
    wo_full = jnp.moveaxis(g_wo, 0, 1).reshape(DEPTH, D, D)
    w_main, w_r = w_full[:, :, :MAIN_W], w_full[:, :, MAIN_W:]
    w_main_t, w_r_t = jnp.swapaxes(w_main, 1, 2), jnp.swapaxes(w_r, 1, 2)
    pa_t, pb_t, wo_t = jnp.swapaxes(pa_full, 1, 2), jnp.swapaxes(pb_full, 1, 2), jnp.swapaxes(wo_full, 1, 2)
    bsb = jnp.broadcast_to(b_s[:, :, :, None], (DEPTH, 4, BLK, BLK))

    xs = [_ln_fwd(x2, ln_in_g, ln_in_b, "ln_in_fwd")]
    saved = []
    for l in range(DEPTH):
        hm, hr = _inproj(xs[l], w_main[l], w_r[l], b_in[l, :MAIN_W].reshape(1, -1), b_in[l, MAIN_W:].reshape(1, -1),
                         f"inproj{l}")
        ya, yb = _mixer_fwd(hm, sinks[l], vn_g[l].reshape(1, -1), vn_b[l].reshape(1, -1), w_s[l], bsb[l],
                            nblk_seq, f"mixer_fwd{l}")
        pa, pb, merged, z, xn = _tail_fwd(xs[l], ya, yb, hr, pa_full[l], pb_full[l], wo_full[l],
                                          b_out[l].reshape(1, D), ln_g[l].reshape(1, D), ln_b[l].reshape(1, D),
                                          f"tail_fwd{l}")
        saved.append((hm, hr, ya, yb, pa, pb, merged, z))
        xs.append(xn)

    dx, sq = _loss_head(xs[DEPTH], tgt, "loss_head")
    loss = lax.psum(sq[0, 0] * (0.5 / D), ("x", "y", "c"))

    big = {"w_in": [], "p_a": [], "p_b": [], "w_out": []}
    small = {n: [] for n in _SMALL}
    for l in reversed(range(DEPTH)):
        hm, hr, ya, yb, pa, pb, merged, z = saved[l]
        dz, dpa, dpb, dhr, dya, dyb, acc, gbr = _tail_bwd(
            dx, z, pa, pb, hr, wo_t[l], pa_t[l], pb_t[l], ln_g[l].reshape(1, D), f"tail_bwd{l}")
        dhm, gbm, gsk, gvn, gws, gbs = _mixer_bwd(
            hm, dya, dyb, sinks[l], vn_g[l].reshape(1, -1), vn_b[l].reshape(1, -1), w_s[l], bsb[l],
            nblk_seq, f"mixer_bwd{l}")
        dx = _dx_inproj(dz, dhm, dhr, w_main_t[l], w_r_t[l], f"dx_inproj{l}")
        gw = jnp.concatenate([_wgrad(xs[l], dhm, MAIN_W // 2, f"wgrad_in_main{l}"),
                              _wgrad(xs[l], dhr, R_W // 2, f"wgrad_in_route{l}")], axis=1)
        big["w_in"].append(gw)
        big["p_a"].append(_wgrad(ya, dpa, D, f"wgrad_pa{l}"))
        big["p_b"].append(_wgrad(yb, dpb, D, f"wgrad_pb{l}"))
        big["w_out"].append(_wgrad(merged, dz, D, f"wgrad_out{l}"))
        small["b_in"].append(jnp.concatenate([gbm[0], gbr[0]]))
        small["sinks"].append(gsk[:, 0])
        small["vn_g"].append(gvn[0])
        small["vn_b"].append(gvn[1])
        small["w_s"].append(gws)
        small["b_s"].append(gbs[:, :, 0])
        small["ln_g"].append(acc[0])
        small["ln_b"].append(acc[1])
        small["b_out"].append(acc[2])
    grad_x, acc_in = _ln_in_bwd(dx, x2, ln_in_g.reshape(1, D), "ln_in_bwd")
    part_small = {n: jnp.stack(v[::-1]) for n, v in small.items() if v}
    part_small["ln_in_g"], part_small["ln_in_b"] = acc_in[0], acc_in[1]

    gi = _owner_major(jnp.stack(big["w_in"][::-1]), 2)
    gpa = _owner_major(jnp.stack(big["p_a"][::-1]), 2)
    gpb = _owner_major(jnp.stack(big["p_b"][::-1]), 2)
    gwo = _owner_major(jnp.stack(big["w_out"][::-1]), 1)
    parts = [gi.reshape(2, 4, DEPTH * D, SHARD_COLS), gpa.reshape(2, 4, DEPTH * Q_W, 128),
             gpb.reshape(2, 4, DEPTH * SGU_W, 128), gwo.reshape(2, 4, DEPTH * 128, D)]
    from_sib = _swap_sibling(parts, "rs_sibling")
    pair = [_pair_sum(g, r, f"pair_sum{a}") for a, (g, r) in enumerate(zip(parts, from_sib))]
    by_chip = _exchange_chips(pair, "rs_chips")
    names = ("w_in", "p_a", "p_b", "w_out")
    given = {"w_in": (w_in, m_w_in, v_w_in), "p_a": (p_a, m_p_a, v_p_a), "p_b": (p_b, m_p_b, v_p_b),
             "w_out": (w_out, m_w_out, v_w_out)}
    res = {}
    for a, n in enumerate(names):
        rows, lanes = by_chip[a].shape[1:]
        outs = _adamw(by_chip[a], *[v.reshape(rows, lanes) for v in given[n]], f"adamw_{n}")
        res[n] = [o.reshape(given[n][0].shape) for o in outs]

    w_small = dict(ln_in_g=ln_in_g, ln_in_b=ln_in_b, b_in=b_in, sinks=sinks, vn_g=vn_g, vn_b=vn_b, w_s=w_s, b_s=b_s,
                   b_out=b_out, ln_g=ln_g, ln_b=ln_b)
    m_small = dict(ln_in_g=m_ln_in_g, ln_in_b=m_ln_in_b, b_in=m_b_in, sinks=m_sinks, vn_g=m_vn_g, vn_b=m_vn_b,
                   w_s=m_w_s, b_s=m_b_s, b_out=m_b_out, ln_g=m_ln_g, ln_b=m_ln_b)
    v_small = dict(ln_in_g=v_ln_in_g, ln_in_b=v_ln_in_b, b_in=v_b_in, sinks=v_sinks, vn_g=v_vn_g, vn_b=v_vn_b,
                   w_s=v_w_s, b_s=v_b_s, b_out=v_b_out, ln_g=v_ln_g, ln_b=v_ln_b)
    (all_small,) = _allgather8([_pack_small(part_small)], "allgather_small")
    outs = _adamw(all_small, _pack_small(w_small), _pack_small(m_small), _pack_small(v_small), "adamw_small")
    for k, o in enumerate(outs):
        u = _unpack_small(o, w_small)
        for n in _SMALL:
            res.setdefault(n, [None] * 4)[k] = u[n]

    order = ("ln_in_g", "ln_in_b", "w_in", "b_in", "sinks", "vn_g", "vn_b", "w_s", "b_s", "p_a", "p_b", "w_out",
             "b_out", "ln_g", "ln_b")
    return (loss, grad_x.reshape(x.shape), *[res[n][0] for n in order], *[res[n][1] for n in order],
            *[res[n][2] for n in order], *[res[n][3] for n in order])
```

```python
import jax
import jax.numpy as jnp
from jax import lax
from jax.experimental import pallas as pl
from jax.experimental.pallas import tpu as pltpu

F32 = jnp.float32
BF16 = jnp.bfloat16

D = 1024
BLK = 128
N_KV = 2
Q_W, KV_W, SGU_W = 512, 128, 512
C_Q, C_K, C_V, C_GA, C_UB, C_VB, C_GB = 0, 512, 640, 768, 1280, 1792, 2304
MAIN_W = 2816
R_W = 2048
IN_COLS = MAIN_W + R_W
N_DEV = 8
SHARD_COLS = IN_COLS // N_DEV

DEPTH = 2
ALPHA = (2.0 * DEPTH) ** 0.25
LN_EPS = 1e-5
ATTN_SCALE = 0.125
NEG = float(jnp.finfo(jnp.float32).min)

ADAM_LR, ADAM_B1, ADAM_B2, ADAM_EPS, ADAM_WD, ADAM_STEP = 0.001, 0.9, 0.999, 1e-08, 0.01, 10

TM = 256
NB = TM // BLK
MESH = pl.DeviceIdType.MESH
VMEM_LIMIT = 56 * 1024 * 1024

_ARB = pltpu.CompilerParams(dimension_semantics=("arbitrary",), vmem_limit_bytes=VMEM_LIMIT)


def _sigmoid(x):
    return 1.0 / (1.0 + jnp.exp(-x))


_GELU_C = 0.7978845608028654
_GELU_A = 0.044715


def _gelu(x):
    return 0.5 * x * (1.0 + jnp.tanh(_GELU_C * (x + _GELU_A * x * x * x)))


def _dgelu(x):
    t = jnp.tanh(_GELU_C * (x + _GELU_A * x * x * x))
    return 0.5 * (1.0 + t) + 0.5 * x * (1.0 - t * t) * (_GELU_C * (1.0 + 3.0 * _GELU_A * x * x))


def _ln_stats(x):
    mu = jnp.mean(x, axis=-1, keepdims=True)
    xc = x - mu
    var = jnp.mean(xc * xc, axis=-1, keepdims=True)
    rstd = lax.rsqrt(var + LN_EPS)
    return xc * rstd, rstd


def _ln_bwd(dy_g, xhat, rstd):
    m1 = jnp.mean(dy_g, axis=-1, keepdims=True)
    m2 = jnp.mean(dy_g * xhat, axis=-1, keepdims=True)
    return rstd * (dy_g - m1 - xhat * m2)


def _colsum(x):
    return jnp.sum(x, axis=0, keepdims=True)


def _dot(a, b):
    return jnp.dot(a, b, preferred_element_type=F32)


def _dot_nt(a, b):
    return lax.dot_general(a, b, (((1,), (1,)), ((), ())), preferred_element_type=F32)


def _dot_tn(a, b):
    return lax.dot_general(a, b, (((0,), (0,)), ((), ())), preferred_element_type=F32)


def _head_place(hk, g):
    j = 4 * hk + g
    return j, j // 2, j % 2


def _head_rows(x, hk):
    d = lax.broadcasted_iota(jnp.int32, x.shape, 0)
    return jnp.where((d >= 64 * hk) & (d < 64 * hk + 64), x, 0.0).astype(BF16)


def _head_lanes(x, hk):
    d = lax.broadcasted_iota(jnp.int32, x.shape, 1)
    return jnp.where((d >= 64 * hk) & (d < 64 * hk + 64), x, 0.0)


def _attn_group(q, kband, vband_t, hk, sinks_ref, lower):
    kh = _head_lanes(kband, hk).astype(BF16)
    parts = []
    for g in range(4):
        _, p, pos = _head_place(hk, g)
        qp = q[:, BLK * p:BLK * (p + 1)] * ATTN_SCALE
        if pos != hk:
            qp = pltpu.roll(qp, 64, 1)
        parts.append(qp.astype(BF16))
    q4 = jnp.concatenate(parts, axis=0)
    s_t = _dot_nt(kh, q4)
    kpos = lax.broadcasted_iota(jnp.int32, (2 * BLK, 4 * BLK), 0)
    row = lax.broadcasted_iota(jnp.int32, (2 * BLK, 4 * BLK), 1) & (BLK - 1)
    valid = (kpos > row) & (kpos <= row + BLK) & (kpos >= lower)
    s_t = jnp.where(valid, s_t, NEG)
    sink_row = jnp.concatenate(
        [jnp.full((1, BLK), sinks_ref[4 * hk + g], F32) for g in range(4)], axis=1)
    m = jnp.maximum(jnp.max(s_t, axis=0, keepdims=True), sink_row)
    p_un = jnp.exp(s_t - m)
    e_sink = jnp.exp(sink_row - m)
    inv = 1.0 / (jnp.sum(p_un, axis=0, keepdims=True) + e_sink)
    prob_t = p_un * inv
    o_t = _dot(_head_rows(vband_t, hk), prob_t.astype(BF16))
    return q4, kh, prob_t, e_sink * inv, o_t


def _unstack_heads(x4, hk, pairs):
    for g in range(4):
        _, p, pos = _head_place(hk, g)
        xg = x4[BLK * g:BLK * (g + 1)]
        if pos != hk:
            xg = pltpu.roll(xg, 64, 1)
        pairs[p] = xg if pairs[p] is None else pairs[p] + xg
    return pairs


def _attn_fwd(q, kband, vband, sinks_ref, lower):
    pairs = [None] * 4
    vband_t = vband.T
    for hk in range(N_KV):
        o_t = _attn_group(q, kband, vband_t, hk, sinks_ref, lower)[-1]
        pairs = _unstack_heads(o_t.T, hk, pairs)
    return jnp.concatenate(pairs, axis=1)


def _tril_mask():
    r = lax.broadcasted_iota(jnp.int32, (BLK, BLK), 0)
    c = lax.broadcasted_iota(jnp.int32, (BLK, BLK), 1)
    return c <= r


def _sgu_fwd(u_b, v_b, vn_g, vn_b, wt, bsb_ref):
    u = _gelu(u_b)
    v = _gelu(v_b)
    vhat, rstd = _ln_stats(v)
    vn = vhat * vn_g + vn_b
    mixed = jnp.concatenate(
        [_dot(wt[g], vn[:, BLK * g:BLK * (g + 1)].astype(BF16)) + bsb_ref[g] for g in range(4)], axis=1)
    return u, vhat, rstd, vn, mixed


def _cols(ref, rows, col, width):
    return ref[rows, col:col + width].astype(F32)


def _band(hm_ref, hprev_ref, s, col):
    r0 = s * BLK
    cur = hm_ref[r0:r0 + BLK, col:col + KV_W]
    if s == 0:
        off = 0 if col == C_K else KV_W
        prev = hprev_ref[:, off:off + KV_W]
    else:
        prev = hm_ref[r0 - BLK:r0, col:col + KV_W]
    return jnp.concatenate([prev, cur], axis=0).astype(F32)


def _mixer_in_specs(nt, rev):
    def tile(g):
        return nt - 1 - g if rev else g

    return [
        pl.BlockSpec(memory_space=pltpu.SMEM),
        pl.BlockSpec((TM, MAIN_W), lambda g: (tile(g), 0)),
        pl.BlockSpec((BLK, 2 * KV_W), lambda g: (jnp.maximum(tile(g) * NB - 1, 0), 2)),
    ]


_CONST2 = lambda g: (0, 0)
_CONST3 = lambda g: (0, 0, 0)


def _ln_fwd(x, g, b, name):
    t = x.shape[0]

    def body(x_ref, g_ref, b_ref, o_ref):
        xhat, _ = _ln_stats(x_ref[...])
        o_ref[...] = xhat * g_ref[...] + b_ref[...]

    return pl.pallas_call(
        body, name=name, grid=(t // TM,),
        in_specs=[pl.BlockSpec((TM, D), lambda i: (i, 0)), pl.BlockSpec((1, D), _CONST2), pl.BlockSpec((1, D), _CONST2)],
        out_specs=pl.BlockSpec((TM, D), lambda i: (i, 0)),
        out_shape=jax.ShapeDtypeStruct((t, D), F32), compiler_params=_ARB,
    )(x, g.reshape(1, D), b.reshape(1, D))


def _inproj(x, wm, wr, bm, br, name):
    t = x.shape[0]

    def body(x_ref, wm_ref, wr_ref, bm_ref, br_ref, hm_ref, hr_ref):
        xb = x_ref[...].astype(BF16)
        hm_ref[...] = (_dot(xb, wm_ref[...]) + bm_ref[...]).astype(BF16)
        hr_ref[...] = (_dot(xb, wr_ref[...]) + br_ref[...]).astype(BF16)

    return pl.pallas_call(
        body, name=name, grid=(t // TM,),
        in_specs=[pl.BlockSpec((TM, D), lambda i: (i, 0)),
                  pl.BlockSpec((D, MAIN_W), _CONST2), pl.BlockSpec((D, R_W), _CONST2),
                  pl.BlockSpec((1, MAIN_W), _CONST2), pl.BlockSpec((1, R_W), _CONST2)],
        out_specs=[pl.BlockSpec((TM, MAIN_W), lambda i: (i, 0)), pl.BlockSpec((TM, R_W), lambda i: (i, 0))],
        out_shape=[jax.ShapeDtypeStruct((t, MAIN_W), BF16), jax.ShapeDtypeStruct((t, R_W), BF16)],
        compiler_params=_ARB,
    )(x, wm, wr, bm, br)


def _mixer_fwd(hm, sinks, vn_g, vn_b, w_s, bsb, nblk_seq, name):
    t = hm.shape[0]
    nt = t // TM

    def body(sinks_ref, hm_ref, hprev_ref, vng_ref, vnb_ref, ws_ref, bsb_ref, ya_ref, yb_ref):
        i = pl.program_id(0)
        tril = _tril_mask()
        wt = [jnp.where(tril, ws_ref[g], 0.0).astype(BF16) for g in range(4)]
        for s in range(NB):
            r0 = s * BLK
            rows = slice(r0, r0 + BLK)
            lower = jnp.where((i * NB + s) % nblk_seq == 0, BLK, 0)
            attn = _attn_fwd(_cols(hm_ref, rows, C_Q, Q_W), _band(hm_ref, hprev_ref, s, C_K),
                             _band(hm_ref, hprev_ref, s, C_V), sinks_ref, lower)
            g_a = _cols(hm_ref, rows, C_GA, Q_W)
            ya_ref[rows, :] = (attn * (g_a * _sigmoid(g_a))).astype(BF16)
            u, _, _, _, mixed = _sgu_fwd(_cols(hm_ref, rows, C_UB, SGU_W), _cols(hm_ref, rows, C_VB, SGU_W),
                                         vng_ref[...], vnb_ref[...], wt, bsb_ref)
            g_b = _cols(hm_ref, rows, C_GB, SGU_W)
            yb_ref[rows, :] = (u * mixed * (g_b * _sigmoid(g_b))).astype(BF16)

    return pl.pallas_call(
        body, name=name, grid=(nt,),
        in_specs=_mixer_in_specs(nt, False) + [
            pl.BlockSpec((1, SGU_W), _CONST2), pl.BlockSpec((1, SGU_W), _CONST2),
            pl.BlockSpec((4, BLK, BLK), _CONST3), pl.BlockSpec((4, BLK, BLK), _CONST3)],
        out_specs=[pl.BlockSpec((TM, Q_W), lambda i: (i, 0)), pl.BlockSpec((TM, SGU_W), lambda i: (i, 0))],
        out_shape=[jax.ShapeDtypeStruct((t, Q_W), BF16), jax.ShapeDtypeStruct((t, SGU_W), BF16)],
        compiler_params=_ARB,
    )(sinks, hm, hm, vn_g, vn_b, w_s, bsb)


def _tail_fwd(x, ya, yb, hr, pa_w, pb_w, wo, b_out, ln_g, ln_b, name):
    t = x.shape[0]

    def body(x_ref, ya_ref, yb_ref, hr_ref, paw_ref, pbw_ref, wo_ref, bo_ref, g_ref, b_ref,
             pa_ref, pb_ref, mg_ref, z_ref, xn_ref):
        pa = _dot(ya_ref[...], paw_ref[...])
        pb = _dot(yb_ref[...], pbw_ref[...])
        pa_ref[...] = pa.astype(BF16)
        pb_ref[...] = pb.astype(BF16)
        everything = slice(None)
        merged = _sigmoid(_cols(hr_ref, everything, 0, D)) * pa + _sigmoid(_cols(hr_ref, everything, D, D)) * pb
        mb = merged.astype(BF16)
        mg_ref[...] = mb
        z = ALPHA * x_ref[...] + (_dot(mb, wo_ref[...]) + bo_ref[...])
        z_ref[...] = z
        zhat, _ = _ln_stats(z)
        xn_ref[...] = zhat * g_ref[...] + b_ref[...]

    row = lambda w: pl.BlockSpec((TM, w), lambda i: (i, 0))
    vec = pl.BlockSpec((1, D), _CONST2)
    return pl.pallas_call(
        body, name=name, grid=(t // TM,),
        in_specs=[row(D), row(Q_W), row(SGU_W), row(R_W),
                  pl.BlockSpec((Q_W, D), _CONST2), pl.BlockSpec((SGU_W, D), _CONST2), pl.BlockSpec((D, D), _CONST2),
                  vec, vec, vec],
        out_specs=[row(D), row(D), row(D), row(D), row(D)],
        out_shape=[jax.ShapeDtypeStruct((t, D), BF16), jax.ShapeDtypeStruct((t, D), BF16),
                   jax.ShapeDtypeStruct((t, D), BF16), jax.ShapeDtypeStruct((t, D), F32),
                   jax.ShapeDtypeStruct((t, D), F32)],
        compiler_params=_ARB,
    )(x, ya, yb, hr, pa_w, pb_w, wo, b_out, ln_g, ln_b)


def _loss_head(y, target, name):
    t = y.shape[0]

    def body(y_ref, t_ref, dy_ref, acc_ref):
        @pl.when(pl.program_id(0) == 0)
        def _():
            acc_ref[...] = jnp.zeros_like(acc_ref)

        err = y_ref[...] - t_ref[...]
        dy_ref[...] = err * (1.0 / D)
        acc_ref[...] += jnp.sum(jnp.sum(err * err, axis=1, keepdims=True), axis=0, keepdims=True)

    return pl.pallas_call(
        body, name=name, grid=(t // TM,),
        in_specs=[pl.BlockSpec((TM, D), lambda i: (i, 0)), pl.BlockSpec((TM, D), lambda i: (i, 0))],
        out_specs=[pl.BlockSpec((TM, D), lambda i: (i, 0)), pl.BlockSpec((8, 128), _CONST2)],
        out_shape=[jax.ShapeDtypeStruct((t, D), F32), jax.ShapeDtypeStruct((8, 128), F32)],
        compiler_params=_ARB,
    )(y, target)


def _tail_bwd(dxn, z, pa, pb, hr, wo_t, pa_t, pb_t, ln_g, name):
    t = dxn.shape[0]

    def body(dxn_ref, z_ref, pa_ref, pb_ref, hr_ref, wot_ref, pat_ref, pbt_ref, g_ref,
             dz_ref, dpa_ref, dpb_ref, dhr_ref, dya_ref, dyb_ref, acc_ref, gbr_ref):
        @pl.when(pl.program_id(0) == 0)
        def _():
            acc_ref[...] = jnp.zeros_like(acc_ref)
            gbr_ref[...] = jnp.zeros_like(gbr_ref)

        dxn_v = dxn_ref[...]
        zhat, rstd = _ln_stats(z_ref[...])
        dz = _ln_bwd(dxn_v * g_ref[...], zhat, rstd)
        dz_ref[...] = dz
        acc_ref[0:1, :] += _colsum(dxn_v * zhat)
        acc_ref[1:2, :] += _colsum(dxn_v)
        acc_ref[2:3, :] += _colsum(dz)
        dmerged = _dot(dz.astype(BF16), wot_ref[...])
        everything = slice(None)
        sa = _sigmoid(_cols(hr_ref, everything, 0, D))
        sb = _sigmoid(_cols(hr_ref, everything, D, D))
        dpa = (dmerged * sa).astype(BF16)
        dpb = (dmerged * sb).astype(BF16)
        dpa_ref[...] = dpa
        dpb_ref[...] = dpb
        dra = dmerged * pa_ref[...].astype(F32) * (sa * (1.0 - sa))
        drb = dmerged * pb_ref[...].astype(F32) * (sb * (1.0 - sb))
        dhr_ref[:, 0:D] = dra.astype(BF16)
        dhr_ref[:, D:2 * D] = drb.astype(BF16)
        gbr_ref[0:1, 0:D] += _colsum(dra)
        gbr_ref[0:1, D:2 * D] += _colsum(drb)
        dya_ref[...] = _dot(dpa, pat_ref[...]).astype(BF16)
        dyb_ref[...] = _dot(dpb, pbt_ref[...]).astype(BF16)

    row = lambda w: pl.BlockSpec((TM, w), lambda i: (i, 0))
    return pl.pallas_call(
        body, name=name, grid=(t // TM,),
        in_specs=[row(D), row(D), row(D), row(D), row(R_W),
                  pl.BlockSpec((D, D), _CONST2), pl.BlockSpec((D, Q_W), _CONST2), pl.BlockSpec((D, SGU_W), _CONST2),
                  pl.BlockSpec((1, D), _CONST2)],
        out_specs=[row(D), row(D), row(D), row(R_W), row(Q_W), row(SGU_W),
                   pl.BlockSpec((8, D), _CONST2), pl.BlockSpec((8, R_W), _CONST2)],
        out_shape=[jax.ShapeDtypeStruct((t, D), F32), jax.ShapeDtypeStruct((t, D), BF16),
                   jax.ShapeDtypeStruct((t, D), BF16), jax.ShapeDtypeStruct((t, R_W), BF16),
                   jax.ShapeDtypeStruct((t, Q_W), BF16), jax.ShapeDtypeStruct((t, SGU_W), BF16),
                   jax.ShapeDtypeStruct((8, D), F32), jax.ShapeDtypeStruct((8, R_W), F32)],
        compiler_params=_ARB,
    )(dxn, z, pa, pb, hr, wo_t, pa_t, pb_t, ln_g)


def _mixer_bwd(hm, dya, dyb, sinks, vn_g, vn_b, w_s, bsb, nblk_seq, name):
    t = hm.shape[0]
    nt = t // TM

    def body(sinks_ref, hm_ref, hprev_ref, dya_ref, dyb_ref, vng_ref, vnb_ref, ws_ref, bsb_ref,
             dhm_ref, gbm_ref, gsk_ref, gvn_ref, gws_ref, gbs_ref, dk_carry, dv_carry):
        gi = pl.program_id(0)
        i = nt - 1 - gi

        @pl.when(gi == 0)
        def _():
            for r in (gbm_ref, gsk_ref, gvn_ref, gws_ref, gbs_ref, dk_carry, dv_carry):
                r[...] = jnp.zeros_like(r)

        tril = _tril_mask()
        wt = [jnp.where(tril, ws_ref[g], 0.0).astype(BF16) for g in range(4)]
        vng = vng_ref[...]

        def put(rows, col, val):
            dhm_ref[rows, col:col + val.shape[1]] = val.astype(BF16)
            gbm_ref[0:1, col:col + val.shape[1]] += _colsum(val)

        for s in reversed(range(NB)):
            r0 = s * BLK
            rows = slice(r0, r0 + BLK)
            lower = jnp.where((i * NB + s) % nblk_seq == 0, BLK, 0)
            q = _cols(hm_ref, rows, C_Q, Q_W)
            kband = _band(hm_ref, hprev_ref, s, C_K)
            vband = _band(hm_ref, hprev_ref, s, C_V)
            g_a = _cols(hm_ref, rows, C_GA, Q_W)
            sg = _sigmoid(g_a)
            dya_v = _cols(dya_ref, rows, 0, Q_W)
            d_o = dya_v * (g_a * sg)
            o_pairs, dq_pairs = [None] * 4, [None] * 4
            dkband = jnp.zeros((2 * BLK, KV_W), F32)
            dvband = jnp.zeros((2 * BLK, KV_W), F32)
            kband_t, vband_t = kband.T, vband.T
            for hk in range(N_KV):
                q4, kh, prob_t, p_sink, o_t = _attn_group(q, kband, vband_t, hk, sinks_ref, lower)
                o_pairs = _unstack_heads(o_t.T, hk, o_pairs)
                parts = []
                for g in range(4):
                    _, p, pos = _head_place(hk, g)
                    dp = d_o[:, BLK * p:BLK * (p + 1)]
                    parts.append(pltpu.roll(dp, 64, 1) if pos != hk else dp)
                do4 = _head_lanes(jnp.concatenate(parts, axis=0), hk)
                do4b = do4.astype(BF16)
                delta = _colsum(do4.T * o_t)
                vh = _head_lanes(vband, hk).astype(BF16)
                ds_t = prob_t * (_dot_nt(vh, do4b) - delta)
                dsb = ds_t.astype(BF16)
                dq4_t = _dot(_head_rows(kband_t, hk), dsb)
                dq_pairs = _unstack_heads(dq4_t.T * ATTN_SCALE, hk, dq_pairs)
                dkband = dkband + _head_lanes(_dot(dsb, q4), hk)
                dvband = dvband + _dot(prob_t.astype(BF16), do4b)
                dsk = p_sink * delta
                for g in range(4):
                    j = 4 * hk + g
                    tot = jnp.sum(dsk[:, BLK * g:BLK * (g + 1)], axis=1, keepdims=True)
                    gsk_ref[j:j + 1, :] += jnp.broadcast_to(-tot, (1, 128))
            attn = jnp.concatenate(o_pairs, axis=1)
            put(rows, C_Q, jnp.concatenate(dq_pairs, axis=1))
            put(rows, C_K, dkband[BLK:2 * BLK] + dk_carry[...])
            put(rows, C_V, dvband[BLK:2 * BLK] + dv_carry[...])
            dk_carry[...] = dkband[0:BLK]
            dv_carry[...] = dvband[0:BLK]
            put(rows, C_GA, dya_v * attn * (sg * (1.0 + g_a * (1.0 - sg))))
            u_b = _cols(hm_ref, rows, C_UB, SGU_W)
            v_b = _cols(hm_ref, rows, C_VB, SGU_W)
            g_b = _cols(hm_ref, rows, C_GB, SGU_W)
            u, vhat, rstd, vn, mixed = _sgu_fwd(u_b, v_b, vng, vnb_ref[...], wt, bsb_ref)
            sgb = _sigmoid(g_b)
            silu_b = g_b * sgb
            dyb_v = _cols(dyb_ref, rows, 0, SGU_W)
            du = dyb_v * mixed * silu_b
            dmixed = dyb_v * u * silu_b
            put(rows, C_GB, dyb_v * u * mixed * (sgb * (1.0 + g_b * (1.0 - sgb))))
            dvn_parts = []
            for g in range(4):
                cols = slice(BLK * g, BLK * (g + 1))
                dmg = dmixed[:, cols]
                dmgb = dmg.astype(BF16)
                dvn_parts.append(_dot_tn(wt[g], dmgb))
                gws_ref[g] += jnp.where(tril, _dot_nt(dmgb, vn[:, cols].astype(BF16)), 0.0)
                gbs_ref[g] += dmg
            dvn = jnp.concatenate(dvn_parts, axis=1)
            gvn_ref[0:1, :] += _colsum(dvn * vhat)
            gvn_ref[1:2, :] += _colsum(dvn)
            dv = _ln_bwd(dvn * vng, vhat, rstd)
            put(rows, C_UB, du * _dgelu(u_b))
            put(rows, C_VB, dv * _dgelu(v_b))

        @pl.when(gi == nt - 1)
        def _():
            for g in range(4):
                gbs_ref[g] = jnp.broadcast_to(jnp.sum(gbs_ref[g], axis=1, keepdims=True), (BLK, BLK))

    row = lambda w: pl.BlockSpec((TM, w), lambda g: (nt - 1 - g, 0))
    return pl.pallas_call(
        body, name=name, grid=(nt,),
        in_specs=_mixer_in_specs(nt, True) + [
            row(Q_W), row(SGU_W),
            pl.BlockSpec((1, SGU_W), _CONST2), pl.BlockSpec((1, SGU_W), _CONST2),
            pl.BlockSpec((4, BLK, BLK), _CONST3), pl.BlockSpec((4, BLK, BLK), _CONST3)],
        out_specs=[row(MAIN_W), pl.BlockSpec((8, MAIN_W), _CONST2), pl.BlockSpec((8, 128), _CONST2),
                   pl.BlockSpec((8, SGU_W), _CONST2), pl.BlockSpec((4, BLK, BLK), _CONST3),
                   pl.BlockSpec((4, BLK, BLK), _CONST3)],
        out_shape=[jax.ShapeDtypeStruct((t, MAIN_W), BF16), jax.ShapeDtypeStruct((8, MAIN_W), F32),
                   jax.ShapeDtypeStruct((8, 128), F32), jax.ShapeDtypeStruct((8, SGU_W), F32),
                   jax.ShapeDtypeStruct((4, BLK, BLK), F32), jax.ShapeDtypeStruct((4, BLK, BLK), F32)],
        scratch_shapes=[pltpu.VMEM((BLK, KV_W), F32), pltpu.VMEM((BLK, KV_W), F32)],
        compiler_params=_ARB,
    )(sinks, hm, hm, dya, dyb, vn_g, vn_b, w_s, bsb)


def _dx_inproj(dz, dhm, dhr, wm_t, wr_t, name):
    t = dz.shape[0]

    def body(dz_ref, dhm_ref, dhr_ref, wmt_ref, wrt_ref, dx_ref):
        dx_ref[...] = ALPHA * dz_ref[...] + _dot(dhm_ref[...], wmt_ref[...]) + _dot(dhr_ref[...], wrt_ref[...])

    row = lambda w: pl.BlockSpec((TM, w), lambda i: (i, 0))
    return pl.pallas_call(
        body, name=name, grid=(t // TM,),
        in_specs=[row(D), row(MAIN_W), row(R_W), pl.BlockSpec((MAIN_W, D), _CONST2), pl.BlockSpec((R_W, D), _CONST2)],
        out_specs=row(D), out_shape=jax.ShapeDtypeStruct((t, D), F32), compiler_params=_ARB,
    )(dz, dhm, dhr, wm_t, wr_t)


def _wgrad(a, b, tn, name, tk=1024):
    t, m = a.shape
    n = b.shape[1]
    nk = t // tk

    def body(a_ref, b_ref, o_ref):
        @pl.when(pl.program_id(1) == 0)
        def _():
            o_ref[...] = jnp.zeros_like(o_ref)

        o_ref[...] += _dot_tn(a_ref[...].astype(BF16), b_ref[...].astype(BF16))

    return pl.pallas_call(
        body, name=name, grid=(n // tn, nk),
        in_specs=[pl.BlockSpec((tk, m), lambda j, k: (k, 0)), pl.BlockSpec((tk, tn), lambda j, k: (k, j))],
        out_specs=pl.BlockSpec((m, tn), lambda j, k: (0, j)),
        out_shape=jax.ShapeDtypeStruct((m, n), F32),
        compiler_params=pltpu.CompilerParams(dimension_semantics=("arbitrary", "arbitrary"), vmem_limit_bytes=VMEM_LIMIT),
    )(a, b)


def _ln_in_bwd(dx0, x, g, name):
    t = x.shape[0]

    def body(dx0_ref, x_ref, g_ref, gx_ref, acc_ref):
        @pl.when(pl.program_id(0) == 0)
        def _():
            acc_ref[...] = jnp.zeros_like(acc_ref)

        d = dx0_ref[...]
        xhat, rstd = _ln_stats(x_ref[...])
        gx_ref[...] = _ln_bwd(d * g_ref[...], xhat, rstd)
        acc_ref[0:1, :] += _colsum(d * xhat)
        acc_ref[1:2, :] += _colsum(d)

    row = pl.BlockSpec((TM, D), lambda i: (i, 0))
    return pl.pallas_call(
        body, name=name, grid=(t // TM,),
        in_specs=[row, row, pl.BlockSpec((1, D), _CONST2)],
        out_specs=[row, pl.BlockSpec((8, D), _CONST2)],
        out_shape=[jax.ShapeDtypeStruct((t, D), F32), jax.ShapeDtypeStruct((8, D), F32)],
        compiler_params=_ARB,
    )(dx0, x, g)


_ANY = pl.BlockSpec(memory_space=pl.ANY)


def _place():
    return lax.axis_index("x"), lax.axis_index("y"), lax.axis_index("c")


def _allgather8(xs, name):
    n = len(xs)

    def body(*refs):
        x_refs, o_refs = refs[:n], refs[n:2 * n]
        send_sems, recv_sems, local_sems = refs[2 * n:]
        x, y, c = _place()
        me, sib = (x, y, c), (x, y, 1 - c)
        chips = [(1 - x, y), (x, 1 - y), (1 - x, 1 - y)]

        def copy(a, k, block, to, src=None):
            dst = o_refs[a].at[4 * block[0] + 2 * block[1] + block[2]]
            return pltpu.make_async_remote_copy(
                src_ref=dst if src is None else src, dst_ref=dst, send_sem=send_sems.at[7 * a + k],
                recv_sem=recv_sems.at[7 * a + k], device_id=to, device_id_type=MESH)

        mine = [pltpu.make_async_copy(x_refs[a], o_refs[a].at[4 * x + 2 * y + c], local_sems.at[a]) for a in range(n)]
        for cp in mine:
            cp.start()
        sent = []
        for a in range(n):
            sent.append(copy(a, 0, me, sib, src=x_refs[a]))
            sent += [copy(a, 1 + j, me, (*chip, c), src=x_refs[a]) for j, chip in enumerate(chips)]
        for cp in sent:
            cp.start()
        for j, chip in enumerate(chips):
            for a in range(n):
                copy(a, 1 + j, (*chip, c), me).wait_recv()
                fwd = copy(a, 4 + j, (*chip, c), sib)
                fwd.start()
                sent.append(fwd)
        for a in range(n):
            copy(a, 0, sib, me).wait_recv()
            for j, chip in enumerate(chips):
                copy(a, 4 + j, (*chip, 1 - c), me).wait_recv()
        for cp in sent:
            cp.wait_send()
        for cp in mine:
            cp.wait()

    return pl.pallas_call(
        body, name=name, in_specs=[_ANY] * n, out_specs=[_ANY] * n,
        out_shape=[jax.ShapeDtypeStruct((N_DEV,) + v.shape, v.dtype) for v in xs],
        scratch_shapes=[pltpu.SemaphoreType.DMA((7 * n,)), pltpu.SemaphoreType.DMA((7 * n,)),
                        pltpu.SemaphoreType.DMA((n,))],
    )(*xs)


def _swap_sibling(gs, name):
    n = len(gs)

    def body(*refs):
        g_refs, r_refs = refs[:n], refs[n:2 * n]
        send_sems, recv_sems = refs[2 * n:]
        x, y, c = _place()
        cps = [pltpu.make_async_remote_copy(
            src_ref=g_refs[a].at[1 - c], dst_ref=r_refs[a], send_sem=send_sems.at[a], recv_sem=recv_sems.at[a],
            device_id=(x, y, 1 - c), device_id_type=MESH) for a in range(n)]
        for cp in cps:
            cp.start()
        for cp in cps:
            cp.wait()

    return pl.pallas_call(
        body, name=name, in_specs=[_ANY] * n, out_specs=[_ANY] * n,
        out_shape=[jax.ShapeDtypeStruct(v.shape[1:], v.dtype) for v in gs],
        scratch_shapes=[pltpu.SemaphoreType.DMA((n,)), pltpu.SemaphoreType.DMA((n,))],
    )(*gs)


def _row_tile(rows, cap):
    return max(d for d in range(16, cap + 1, 16) if rows % d == 0)


def _pair_sum(g, r, name):
    _, _, rows, l = g.shape
    tr = _row_tile(rows, 512)

    def body(c_ref, g_ref, r_ref, o_ref):
        o_ref[...] = (g_ref[0].astype(F32) + r_ref[...].astype(F32)).astype(o_ref.dtype)

    return pl.pallas_call(
        body, name=name,
        grid_spec=pltpu.PrefetchScalarGridSpec(
            num_scalar_prefetch=1, grid=(4, rows // tr),
            in_specs=[pl.BlockSpec((1, 1, tr, l), lambda q, i, c_ref: (c_ref[0], q, i, 0)),
                      pl.BlockSpec((1, tr, l), lambda q, i, c_ref: (q, i, 0))],
            out_specs=pl.BlockSpec((1, tr, l), lambda q, i, c_ref: (q, i, 0))),
        out_shape=jax.ShapeDtypeStruct((4, rows, l), g.dtype),
        compiler_params=pltpu.CompilerParams(dimension_semantics=("arbitrary", "arbitrary"), vmem_limit_bytes=VMEM_LIMIT),
    )(lax.axis_index("c").astype(jnp.int32).reshape(1), g, r)


def _exchange_chips(ps, name):
    n = len(ps)

    def body(*refs):
        p_refs, r_refs = refs[:n], refs[n:2 * n]
        send_sems, recv_sems, local_sems = refs[2 * n:]
        x, y, c = _place()
        mine = 2 * x + y
        chips = [(1 - x, y), (x, 1 - y), (1 - x, 1 - y)]
        local = [pltpu.make_async_copy(p_refs[a].at[mine], r_refs[a].at[mine], local_sems.at[a]) for a in range(n)]
        for cp in local:
            cp.start()
        cps = []
        for a in range(n):
            for j, (qx, qy) in enumerate(chips):
                cps.append(pltpu.make_async_remote_copy(
                    src_ref=p_refs[a].at[2 * qx + qy], dst_ref=r_refs[a].at[mine], send_sem=send_sems.at[3 * a + j],
                    recv_sem=recv_sems.at[3 * a + j], device_id=(qx, qy, c), device_id_type=MESH))
        for cp in cps:
            cp.start()
        for a in range(n):
            for j, (qx, qy) in enumerate(chips):
                pltpu.make_async_remote_copy(
                    src_ref=p_refs[a].at[mine], dst_ref=r_refs[a].at[2 * qx + qy], send_sem=send_sems.at[3 * a + j],
                    recv_sem=recv_sems.at[3 * a + j], device_id=(qx, qy, c), device_id_type=MESH).wait_recv()
        for cp in cps:
            cp.wait_send()
        for cp in local:
            cp.wait()

    return pl.pallas_call(
        body, name=name, in_specs=[_ANY] * n, out_specs=[_ANY] * n,
        out_shape=[jax.ShapeDtypeStruct(v.shape, v.dtype) for v in ps],
        scratch_shapes=[pltpu.SemaphoreType.DMA((3 * n,)), pltpu.SemaphoreType.DMA((3 * n,)),
                        pltpu.SemaphoreType.DMA((n,))],
    )(*ps)


def _adamw(parts, w, m, v, name, own=None):
    ns, rows, l = parts.shape
    tr = _row_tile(rows, 256)
    c1 = 1.0 - ADAM_B1 ** ADAM_STEP
    c2 = 1.0 - ADAM_B2 ** ADAM_STEP

    def body(q_ref, *refs):
        own_ref = refs[0] if own is not None else None
        p_ref, w_ref, m_ref, v_ref, g_ref, d_ref, nm_ref, nv_ref = refs[-8:]
        g = p_ref[0].astype(F32)
        if own_ref is not None:
            g = own_ref[0].astype(F32) + g
        for k in range(1, ns):
            g = g + p_ref[k].astype(F32)
        g_ref[...] = g
        nm = ADAM_B1 * m_ref[...] + (1.0 - ADAM_B1) * g
        nv = ADAM_B2 * v_ref[...] + (1.0 - ADAM_B2) * (g * g)
        nm_ref[...] = nm
        nv_ref[...] = nv
        d_ref[...] = -ADAM_LR * ((nm / c1) / (jnp.sqrt(nv / c2) + ADAM_EPS) + ADAM_WD * w_ref[...])

    row = pl.BlockSpec((tr, l), lambda i, q: (i, 0))
    own_specs = [] if own is None else [pl.BlockSpec((1, tr, l), lambda i, q: (q[0], i, 0))]
    chip = (2 * lax.axis_index("x") + lax.axis_index("y")).astype(jnp.int32).reshape(1)
    return pl.pallas_call(
        body, name=name,
        grid_spec=pltpu.PrefetchScalarGridSpec(
            num_scalar_prefetch=1, grid=(rows // tr,),
            in_specs=own_specs + [pl.BlockSpec((ns, tr, l), lambda i, q: (0, i, 0)), row, row, row],
            out_specs=[row] * 4),
        out_shape=[jax.ShapeDtypeStruct((rows, l), F32)] * 4, compiler_params=_ARB,
    )(chip, *([] if own is None else [own]), parts, w, m, v)


_HBM = pl.BlockSpec(memory_space=pltpu.HBM)
_SEM = pl.BlockSpec(memory_space=pltpu.SEMAPHORE)
_EFFECT = pltpu.SideEffectType.DATAFLOW_SIDE_EFFECTING


def _plan_all(x, y, c):
    me = 4 * x + 2 * y + c
    peers = [(x, y, 1 - c), (1 - x, y, c), (x, 1 - y, c), (1 - x, 1 - y, c),
             (1 - x, y, 1 - c), (x, 1 - y, 1 - c), (1 - x, 1 - y, 1 - c)]
    return [(None, me, p, 4 * p[0] + 2 * p[1] + p[2]) for p in peers]


def _plan_chips(x, y, c):
    me = 2 * x + y
    return [(2 * qx + qy, me, (qx, qy, c), 2 * qx + qy) for qx, qy in ((1 - x, y), (x, 1 - y), (1 - x, 1 - y))]


def _split_copies(plan, src_refs, land_refs, send_sems, recv_sems, arrival):
    n = len(src_refs)
    entries = plan(*_place())
    per = len(entries)
    cps = []
    for a in range(n):
        for k, (src_slot, dst_slot, peer, back_slot) in enumerate(entries):
            src = src_refs[a] if src_slot is None else src_refs[a].at[src_slot]
            cps.append(pltpu.make_async_remote_copy(
                src_ref=src, dst_ref=land_refs[a].at[back_slot if arrival else dst_slot],
                send_sem=send_sems.at[per * a + k], recv_sem=recv_sems.at[per * a + k],
                device_id=peer, device_id_type=MESH))
    return cps


def _split_start(srcs, lands, plan, per, name):
    n = len(srcs)

    def body(*refs):
        for cp in _split_copies(plan, refs[:n], refs[n:2 * n], refs[2 * n], refs[2 * n + 1], False):
            cp.start()
        refs[-1][...] = jnp.zeros_like(refs[-1])

    both = list(srcs) + list(lands)
    outs = pl.pallas_call(
        body, name=name,
        out_shape=(pltpu.SemaphoreType.DMA((per * n,)), pltpu.SemaphoreType.DMA((per * n,)),
                   *[pltpu.HBM(v.shape, v.dtype) for v in both], jax.ShapeDtypeStruct((8, 128), F32)),
        in_specs=[_HBM] * (2 * n),
        out_specs=(_SEM, _SEM, *[_HBM] * (2 * n), pl.BlockSpec(memory_space=pltpu.VMEM)),
        input_output_aliases={i: 2 + i for i in range(2 * n)},
        compiler_params=pltpu.CompilerParams(has_side_effects=_EFFECT),
    )(*[pltpu.with_memory_space_constraint(v, pltpu.HBM) for v in both])
    return outs[0], outs[1], list(outs[2:2 + 2 * n]), outs[-1]


def _split_wait(send_sems, recv_sems, thru, plan, after, name):
    n = len(thru) // 2

    def body(*refs):
        for cp in _split_copies(plan, refs[:n], refs[n:2 * n], refs[2 * n], refs[2 * n + 1], True):
            cp.wait_send()
            cp.wait_recv()

    outs = pl.pallas_call(
        body, name=name, out_shape=tuple(pltpu.HBM(v.shape, v.dtype) for v in thru),
        in_specs=[_HBM] * (2 * n) + [_SEM, _SEM, pl.BlockSpec(memory_space=pl.ANY)],
        out_specs=[_HBM] * (2 * n), input_output_aliases={i: i for i in range(2 * n)},
        compiler_params=pltpu.CompilerParams(has_side_effects=_EFFECT),
    )(*thru, send_sems, recv_sems, after)
    return list(outs[:n]), list(outs[n:])


_SMALL = ("ln_in_g", "ln_in_b", "b_in", "sinks", "vn_g", "vn_b", "w_s", "b_s", "b_out", "ln_g", "ln_b")


def _pack_small(d):
    out = []
    for n in _SMALL:
        f = d[n].reshape(-1)
        out.append(jnp.pad(f, (0, (-f.shape[0]) % 1024)))
    return jnp.concatenate(out).reshape(-1, 128)


def _unpack_small(p, like):
    flat, off, out = p.reshape(-1), 0, {}
    for n in _SMALL:
        size = like[n].size
        out[n] = flat[off:off + size].reshape(like[n].shape)
        off += size + (-size) % 1024
    return out


def _owner_major(g, axis):
    sh = g.shape
    g = g.reshape(sh[:axis] + (4, 2, sh[axis] // N_DEV) + sh[axis + 1:])
    g = jnp.moveaxis(g, (axis, axis + 1), (1, 0))
    return g


def kernel(x, ln_in_g, ln_in_b, w_in, b_in, sinks, vn_g, vn_b, w_s, b_s, p_a, p_b, w_out, b_out, ln_g, ln_b, loss_target, m_ln_in_g, m_ln_in_b, m_w_in, m_b_in, m_sinks, m_vn_g, m_vn_b, m_w_s, m_b_s, m_p_a, m_p_b, m_w_out, m_b_out, m_ln_g, m_ln_b, v_ln_in_g, v_ln_in_b, v_w_in, v_b_in, v_sinks, v_vn_g, v_vn_b, v_w_s, v_b_s, v_p_a, v_p_b, v_w_out, v_b_out, v_ln_g, v_ln_b):
    nseq, seq, _ = x.shape
    t = nseq * seq
    nblk_seq = seq // BLK
    x2 = x.reshape(t, D)
    tgt = loss_target.reshape(t, D)

    def blocks(l):
        return [w_in[l].astype(BF16), p_a[l].astype(BF16), p_b[l].astype(BF16), w_out[l].astype(BF16)]

    def full_weights(g):
        w_full = jnp.moveaxis(g[0], 0, 1).reshape(D, IN_COLS)
        pa_full = jnp.moveaxis(g[1], 0, 1).reshape(Q_W, D)
        pb_full = jnp.moveaxis(g[2], 0, 1).reshape(SGU_W, D)
        wo_full = g[3].reshape(D, D)
        w_main, w_r = w_full[:, :MAIN_W], w_full[:, MAIN_W:]
        return dict(w_main=w_main, w_r=w_r, pa=pa_full, pb=pb_full, wo=wo_full, w_main_t=w_main.T, w_r_t=w_r.T,
                    pa_t=pa_full.T, pb_t=pb_full.T, wo_t=wo_full.T)

    me = 4 * lax.axis_index("x") + 2 * lax.axis_index("y") + lax.axis_index("c")
    gathered0 = _allgather8(blocks(0), "allgather_weights0")
    blocks1, gathered0 = lax.optimization_barrier((blocks(1), gathered0))
    lands1 = [lax.empty((N_DEV,) + v.shape, v.dtype) for v in blocks1]
    ag_send, ag_recv, ag_thru, ag_token = _split_start(blocks1, lands1, _plan_all, 7, "allgather_weights1_start")
    weights = [full_weights(gathered0), None]
    bsb = jnp.broadcast_to(b_s[:, :, :, None], (DEPTH, 4, BLK, BLK))

    xs = [_ln_fwd(x2, ln_in_g + ag_token[0, 0], ln_in_b, "ln_in_fwd")]
    saved = []
    for l in range(DEPTH):
        if l == 1:
            sent, landed = _split_wait(ag_send, ag_recv, ag_thru, _plan_all, xs[1], "allgather_weights1_wait")
            weights[1] = full_weights(
                [lax.dynamic_update_index_in_dim(g, b, me, 0) for g, b in zip(landed, sent)])
        wl = weights[l]
        hm, hr = _inproj(xs[l], wl["w_main"], wl["w_r"], b_in[l, :MAIN_W].reshape(1, -1),
                         b_in[l, MAIN_W:].reshape(1, -1), f"inproj{l}")
        ya, yb = _mixer_fwd(hm, sinks[l], vn_g[l].reshape(1, -1), vn_b[l].reshape(1, -1), w_s[l], bsb[l],
                            nblk_seq, f"mixer_fwd{l}")
        pa, pb, merged, z, xn = _tail_fwd(xs[l], ya, yb, hr, wl["pa"], wl["pb"], wl["wo"],
                                          b_out[l].reshape(1, D), ln_g[l].reshape(1, D), ln_b[l].reshape(1, D),
                                          f"tail_fwd{l}")
        saved.append((hm, hr, ya, yb, pa, pb, merged, z))
        xs.append(xn)

    dx, sq = _loss_head(xs[DEPTH], tgt, "loss_head")
    loss = lax.psum(sq[0, 0] * (0.5 / D), ("x", "y", "c"))

    small = {n: [None] * DEPTH for n in _SMALL if n not in ("ln_in_g", "ln_in_b")}
    names = ("w_in", "p_a", "p_b", "w_out")
    owner_axis = {"w_in": 1, "p_a": 1, "p_b": 1, "w_out": 0}
    rs_token = jnp.zeros((8, 128), F32)
    pairs = [None] * DEPTH
    for l in reversed(range(DEPTH)):
        hm, hr, ya, yb, pa, pb, merged, z = saved[l]
        wl = weights[l]
        dz, dpa, dpb, dhr, dya, dyb, acc, gbr = _tail_bwd(
            dx, z, pa, pb, hr, wl["wo_t"], wl["pa_t"], wl["pb_t"], ln_g[l].reshape(1, D) + rs_token[0, 0],
            f"tail_bwd{l}")
        dhm, gbm, gsk, gvn, gws, gbs = _mixer_bwd(
            hm, dya, dyb, sinks[l], vn_g[l].reshape(1, -1), vn_b[l].reshape(1, -1), w_s[l], bsb[l],
            nblk_seq, f"mixer_bwd{l}")
        dx = _dx_inproj(dz, dhm, dhr, wl["w_main_t"], wl["w_r_t"], f"dx_inproj{l}")
        grads = {"w_in": jnp.concatenate([_wgrad(xs[l], dhm, MAIN_W // 2, f"wgrad_in_main{l}"),
                                          _wgrad(xs[l], dhr, R_W // 2, f"wgrad_in_route{l}")], axis=1),
                 "p_a": _wgrad(ya, dpa, D, f"wgrad_pa{l}"), "p_b": _wgrad(yb, dpb, D, f"wgrad_pb{l}"),
                 "w_out": _wgrad(merged, dz, D, f"wgrad_out{l}")}
        small["b_in"][l] = jnp.concatenate([gbm[0], gbr[0]])
        small["sinks"][l] = gsk[:, 0]
        small["vn_g"][l], small["vn_b"][l] = gvn[0], gvn[1]
        small["w_s"][l], small["b_s"][l] = gws, gbs[:, :, 0]
        small["ln_g"][l], small["ln_b"][l], small["b_out"][l] = acc[0], acc[1], acc[2]
        parts = [_owner_major(grads[n], owner_axis[n]).astype(BF16) for n in names]
        if l == 1:
            from_sib = _swap_sibling(parts, "rs_sibling1")
            pairs[1] = [_pair_sum(g, r, f"pair_sum1_{a}") for a, (g, r) in enumerate(zip(parts, from_sib))]
            rs_lands = [jnp.zeros(p.shape, p.dtype) for p in pairs[1]]
            rs_send, rs_recv, rs_thru, rs_token = _split_start(pairs[1], rs_lands, _plan_chips, 3, "rs_chips1_start")
    grad_x, acc_in = _ln_in_bwd(dx, x2, ln_in_g.reshape(1, D), "ln_in_bwd")
    part_small = {n: jnp.stack(v) for n, v in small.items()}
    part_small["ln_in_g"], part_small["ln_in_b"] = acc_in[0], acc_in[1]

    packed = _pack_small(part_small)
    parts.append(jnp.broadcast_to(packed[None, None], (2, 4) + packed.shape))
    from_sib = _swap_sibling(parts, "rs_sibling0")
    pairs[0] = [_pair_sum(g, r, f"pair_sum0_{a}") for a, (g, r) in enumerate(zip(parts, from_sib))]
    by_chip0 = _exchange_chips(pairs[0], "rs_chips0")
    pairs[1], by_chip1 = _split_wait(rs_send, rs_recv, rs_thru, _plan_chips, by_chip0[0], "rs_chips1_wait")

    given = {"w_in": (w_in, m_w_in, v_w_in), "p_a": (p_a, m_p_a, v_p_a), "p_b": (p_b, m_p_b, v_p_b),
             "w_out": (w_out, m_w_out, v_w_out)}
    res = {}
    for a, n in enumerate(names):
        rows, lanes = by_chip0[a].shape[1:]
        out0 = _adamw(by_chip0[a], *[v[0].reshape(rows, lanes) for v in given[n]], f"adamw0_{n}")
        out1 = _adamw(by_chip1[a], *[v[1].reshape(rows, lanes) for v in given[n]], f"adamw1_{n}", own=pairs[1][a])
        res[n] = [jnp.stack([o0, o1]).reshape(given[n][0].shape) for o0, o1 in zip(out0, out1)]

    w_small = dict(ln_in_g=ln_in_g, ln_in_b=ln_in_b, b_in=b_in, sinks=sinks, vn_g=vn_g, vn_b=vn_b, w_s=w_s, b_s=b_s,
                   b_out=b_out, ln_g=ln_g, ln_b=ln_b)
    m_small = dict(ln_in_g=m_ln_in_g, ln_in_b=m_ln_in_b, b_in=m_b_in, sinks=m_sinks, vn_g=m_vn_g, vn_b=m_vn_b,
                   w_s=m_w_s, b_s=m_b_s, b_out=m_b_out, ln_g=m_ln_g, ln_b=m_ln_b)
    v_small = dict(ln_in_g=v_ln_in_g, ln_in_b=v_ln_in_b, b_in=v_b_in, sinks=v_sinks, vn_g=v_vn_g, vn_b=v_vn_b,
                   w_s=v_w_s, b_s=v_b_s, b_out=v_b_out, ln_g=v_ln_g, ln_b=v_ln_b)
    outs = _adamw(by_chip0[4], _pack_small(w_small), _pack_small(m_small), _pack_small(v_small), "adamw_small")
    for k, o in enumerate(outs):
        u = _unpack_small(o, w_small)
        for n in _SMALL:
            res.setdefault(n, [None] * 4)[k] = u[n]

    order = ("ln_in_g", "ln_in_b", "w_in", "b_in", "sinks", "vn_g", "vn_b", "w_s", "b_s", "p_a", "p_b", "w_out",
             "b_out", "ln_g", "ln_b")
    return (loss, grad_x.reshape(x.shape), *[res[n][0] for n in order], *[res[n][1] for n in order],
            *[res[n][2] for n in order], *[res[n][3] for n in order])
```

```python
import jax
import jax.numpy as jnp
from jax import lax
from jax.experimental import pallas as pl
from jax.experimental.pallas import tpu as pltpu

F32 = jnp.float32
BF16 = jnp.bfloat16

D = 1024
BLK = 128
N_KV = 2
Q_W, KV_W, SGU_W = 512, 128, 512
C_Q, C_K, C_V, C_GA, C_UB, C_VB, C_GB = 0, 512, 640, 768, 1280, 1792, 2304
MAIN_W = 2816
R_W = 2048
IN_COLS = MAIN_W + R_W
N_DEV = 8
SHARD_COLS = IN_COLS // N_DEV

DEPTH = 2
ALPHA = (2.0 * DEPTH) ** 0.25
LN_EPS = 1e-5
ATTN_SCALE = 0.125
NEG = float(jnp.finfo(jnp.float32).min)

ADAM_LR, ADAM_B1, ADAM_B2, ADAM_EPS, ADAM_WD, ADAM_STEP = 0.001, 0.9, 0.999, 1e-08, 0.01, 10

TM = 256
TM_EW = 512
NB = TM // BLK
MESH = pl.DeviceIdType.MESH
VMEM_LIMIT = 56 * 1024 * 1024

_ARB = pltpu.CompilerParams(dimension_semantics=("arbitrary",), vmem_limit_bytes=VMEM_LIMIT)


def _sigmoid(x):
    return 1.0 / (1.0 + jnp.exp(-x))


_GELU_C = 0.7978845608028654
_GELU_A = 0.044715


def _gelu(x):
    return 0.5 * x * (1.0 + jnp.tanh(_GELU_C * (x + _GELU_A * x * x * x)))


def _dgelu(x):
    t = jnp.tanh(_GELU_C * (x + _GELU_A * x * x * x))
    return 0.5 * (1.0 + t) + 0.5 * x * (1.0 - t * t) * (_GELU_C * (1.0 + 3.0 * _GELU_A * x * x))


def _ln_stats(x):
    mu = jnp.mean(x, axis=-1, keepdims=True)
    xc = x - mu
    var = jnp.mean(xc * xc, axis=-1, keepdims=True)
    rstd = lax.rsqrt(var + LN_EPS)
    return xc * rstd, rstd


def _ln_bwd(dy_g, xhat, rstd):
    m1 = jnp.mean(dy_g, axis=-1, keepdims=True)
    m2 = jnp.mean(dy_g * xhat, axis=-1, keepdims=True)
    return rstd * (dy_g - m1 - xhat * m2)


def _colsum(x):
    return jnp.sum(x, axis=0, keepdims=True)


def _dot(a, b):
    return jnp.dot(a, b, preferred_element_type=F32)


def _dot_nt(a, b):
    return lax.dot_general(a, b, (((1,), (1,)), ((), ())), preferred_element_type=F32)


def _dot_tn(a, b):
    return lax.dot_general(a, b, (((0,), (0,)), ((), ())), preferred_element_type=F32)


def _head_place(hk, g):
    j = 4 * hk + g
    return j, j // 2, j % 2


def _head_rows(x, hk):
    d = lax.broadcasted_iota(jnp.int32, x.shape, 0)
    return jnp.where((d >= 64 * hk) & (d < 64 * hk + 64), x, 0.0).astype(BF16)


def _head_lanes(x, hk):
    d = lax.broadcasted_iota(jnp.int32, x.shape, 1)
    return jnp.where((d >= 64 * hk) & (d < 64 * hk + 64), x, 0.0)


def _band_bias():
    kpos = lax.broadcasted_iota(jnp.int32, (2 * BLK, 4 * BLK), 0)
    row = lax.broadcasted_iota(jnp.int32, (2 * BLK, 4 * BLK), 1) & (BLK - 1)
    band = (kpos > row) & (kpos <= row + BLK)
    return jnp.stack([jnp.where(band, 0.0, NEG), jnp.where(band & (kpos >= BLK), 0.0, NEG)]).astype(F32)


def _attn_group(q, kband, vband_t, hk, sinks_ref, bias):
    kh = _head_lanes(kband, hk).astype(BF16)
    parts = []
    for g in range(4):
        _, p, pos = _head_place(hk, g)
        qp = q[:, BLK * p:BLK * (p + 1)] * ATTN_SCALE
        if pos != hk:
            qp = pltpu.roll(qp, 64, 1)
        parts.append(qp.astype(BF16))
    q4 = jnp.concatenate(parts, axis=0)
    s_t = _dot_nt(kh, q4) + bias
    sink_row = jnp.concatenate(
        [jnp.full((1, BLK), sinks_ref[4 * hk + g], F32) for g in range(4)], axis=1)
    m = jnp.maximum(jnp.max(s_t, axis=0, keepdims=True), sink_row)
    p_un = jnp.exp(s_t - m)
    e_sink = jnp.exp(sink_row - m)
    inv = 1.0 / (jnp.sum(p_un, axis=0, keepdims=True) + e_sink)
    prob_t = p_un * inv
    o_t = _dot(_head_rows(vband_t, hk), prob_t.astype(BF16))
    return q4, kh, prob_t, e_sink * inv, o_t


def _unstack_heads(x4, hk, pairs):
    for g in range(4):
        _, p, pos = _head_place(hk, g)
        xg = x4[BLK * g:BLK * (g + 1)]
        if pos != hk:
            xg = pltpu.roll(xg, 64, 1)
        pairs[p] = xg if pairs[p] is None else pairs[p] + xg
    return pairs


def _attn_fwd(q, kband, vband, sinks_ref, bias):
    pairs = [None] * 4
    vband_t = vband.T
    for hk in range(N_KV):
        o_t = _attn_group(q, kband, vband_t, hk, sinks_ref, bias)[-1]
        pairs = _unstack_heads(o_t.T, hk, pairs)
    return jnp.concatenate(pairs, axis=1)


def _tril_mask():
    r = lax.broadcasted_iota(jnp.int32, (BLK, BLK), 0)
    c = lax.broadcasted_iota(jnp.int32, (BLK, BLK), 1)
    return c <= r


def _sgu_fwd(u_b, v_b, vn_g, vn_b, wt, bsb_ref):
    u = _gelu(u_b)
    v = _gelu(v_b)
    vhat, rstd = _ln_stats(v)
    vn = vhat * vn_g + vn_b
    mixed = jnp.concatenate(
        [_dot(wt[g], vn[:, BLK * g:BLK * (g + 1)].astype(BF16)) + bsb_ref[g] for g in range(4)], axis=1)
    return u, vhat, rstd, vn, mixed


def _cols(ref, rows, col, width):
    return ref[rows, col:col + width].astype(F32)


def _band(hm_ref, hprev_ref, s, col):
    r0 = s * BLK
    cur = hm_ref[r0:r0 + BLK, col:col + KV_W]
    if s == 0:
        off = 0 if col == C_K else KV_W
        prev = hprev_ref[:, off:off + KV_W]
    else:
        prev = hm_ref[r0 - BLK:r0, col:col + KV_W]
    return jnp.concatenate([prev, cur], axis=0).astype(F32)


def _mixer_in_specs(nt, rev):
    def tile(g):
        return nt - 1 - g if rev else g

    return [
        pl.BlockSpec(memory_space=pltpu.SMEM),
        pl.BlockSpec((TM, MAIN_W), lambda g: (tile(g), 0)),
        pl.BlockSpec((BLK, 2 * KV_W), lambda g: (jnp.maximum(tile(g) * NB - 1, 0), 2)),
        pl.BlockSpec((2, 2 * BLK, 4 * BLK), lambda g: (0, 0, 0)),
    ]


_CONST2 = lambda g: (0, 0)
_CONST3 = lambda g: (0, 0, 0)


def _ln_fwd(x, g, b, name):
    t = x.shape[0]

    def body(x_ref, g_ref, b_ref, o_ref):
        xhat, _ = _ln_stats(x_ref[...])
        o_ref[...] = xhat * g_ref[...] + b_ref[...]

    return pl.pallas_call(
        body, name=name, grid=(t // TM_EW,),
        in_specs=[pl.BlockSpec((TM_EW, D), lambda i: (i, 0)), pl.BlockSpec((1, D), _CONST2),
                  pl.BlockSpec((1, D), _CONST2)],
        out_specs=pl.BlockSpec((TM_EW, D), lambda i: (i, 0)),
        out_shape=jax.ShapeDtypeStruct((t, D), F32), compiler_params=_ARB,
    )(x, g.reshape(1, D), b.reshape(1, D))


def _inproj(x, w, b, name):
    t = x.shape[0]

    def body(x_ref, w_ref, b_ref, hm_ref, hr_ref):
        xb = x_ref[...].astype(BF16)
        hm_ref[...] = (_dot(xb, w_ref[:, 0:MAIN_W]) + b_ref[:, 0:MAIN_W]).astype(BF16)
        hr_ref[...] = (_dot(xb, w_ref[:, MAIN_W:IN_COLS]) + b_ref[:, MAIN_W:IN_COLS]).astype(BF16)

    return pl.pallas_call(
        body, name=name, grid=(t // TM,),
        in_specs=[pl.BlockSpec((TM, D), lambda i: (i, 0)),
                  pl.BlockSpec((D, IN_COLS), _CONST2), pl.BlockSpec((1, IN_COLS), _CONST2)],
        out_specs=[pl.BlockSpec((TM, MAIN_W), lambda i: (i, 0)), pl.BlockSpec((TM, R_W), lambda i: (i, 0))],
        out_shape=[jax.ShapeDtypeStruct((t, MAIN_W), BF16), jax.ShapeDtypeStruct((t, R_W), BF16)],
        compiler_params=_ARB,
    )(x, w, b)


def _mixer_fwd(hm, sinks, bias, vn_g, vn_b, w_s, bsb, nblk_seq, name):
    t = hm.shape[0]
    nt = t // TM

    def body(sinks_ref, hm_ref, hprev_ref, bias_ref, vng_ref, vnb_ref, ws_ref, bsb_ref, ya_ref, yb_ref):
        i = pl.program_id(0)
        tril = _tril_mask()
        wt = [jnp.where(tril, ws_ref[g], 0.0).astype(BF16) for g in range(4)]
        for s in range(NB):
            r0 = s * BLK
            rows = slice(r0, r0 + BLK)
            bias = bias_ref[jnp.where((i * NB + s) % nblk_seq == 0, 1, 0)]
            attn = _attn_fwd(_cols(hm_ref, rows, C_Q, Q_W), _band(hm_ref, hprev_ref, s, C_K),
                             _band(hm_ref, hprev_ref, s, C_V), sinks_ref, bias)
            g_a = _cols(hm_ref, rows, C_GA, Q_W)
            ya_ref[rows, :] = (attn * (g_a * _sigmoid(g_a))).astype(BF16)
            u, _, _, _, mixed = _sgu_fwd(_cols(hm_ref, rows, C_UB, SGU_W), _cols(hm_ref, rows, C_VB, SGU_W),
                                         vng_ref[...], vnb_ref[...], wt, bsb_ref)
            g_b = _cols(hm_ref, rows, C_GB, SGU_W)
            yb_ref[rows, :] = (u * mixed * (g_b * _sigmoid(g_b))).astype(BF16)

    return pl.pallas_call(
        body, name=name, grid=(nt,),
        in_specs=_mixer_in_specs(nt, False) + [
            pl.BlockSpec((1, SGU_W), _CONST2), pl.BlockSpec((1, SGU_W), _CONST2),
            pl.BlockSpec((4, BLK, BLK), _CONST3), pl.BlockSpec((4, BLK, BLK), _CONST3)],
        out_specs=[pl.BlockSpec((TM, Q_W), lambda i: (i, 0)), pl.BlockSpec((TM, SGU_W), lambda i: (i, 0))],
        out_shape=[jax.ShapeDtypeStruct((t, Q_W), BF16), jax.ShapeDtypeStruct((t, SGU_W), BF16)],
        compiler_params=_ARB,
    )(sinks, hm, hm, bias, vn_g, vn_b, w_s, bsb)


def _tail_fwd(x, ya, yb, hr, pa_w, pb_w, wo, b_out, ln_g, ln_b, name, last):
    t = x.shape[0]

    def body(x_ref, ya_ref, yb_ref, hr_ref, paw_ref, pbw_ref, wo_ref, bo_ref, g_ref, b_ref,
             pa_ref, pb_ref, mg_ref, z_ref, *xn_ref):
        pa = _dot(ya_ref[...], paw_ref[...])
        pb = _dot(yb_ref[...], pbw_ref[...])
        pa_ref[...] = pa.astype(BF16)
        pb_ref[...] = pb.astype(BF16)
        everything = slice(None)
        merged = _sigmoid(_cols(hr_ref, everything, 0, D)) * pa + _sigmoid(_cols(hr_ref, everything, D, D)) * pb
        mb = merged.astype(BF16)
        mg_ref[...] = mb
        z = ALPHA * x_ref[...] + (_dot(mb, wo_ref[...]) + bo_ref[...])
        z_ref[...] = z
        if not last:
            zhat, _ = _ln_stats(z)
            xn_ref[0][...] = zhat * g_ref[...] + b_ref[...]

    row = lambda w: pl.BlockSpec((TM, w), lambda i: (i, 0))
    vec = pl.BlockSpec((1, D), _CONST2)
    n_f32 = 1 if last else 2
    return pl.pallas_call(
        body, name=name, grid=(t // TM,),
        in_specs=[row(D), row(Q_W), row(SGU_W), row(R_W),
                  pl.BlockSpec((Q_W, D), _CONST2), pl.BlockSpec((SGU_W, D), _CONST2), pl.BlockSpec((D, D), _CONST2),
                  vec, vec, vec],
        out_specs=[row(D)] * (3 + n_f32),
        out_shape=[jax.ShapeDtypeStruct((t, D), BF16)] * 3 + [jax.ShapeDtypeStruct((t, D), F32)] * n_f32,
        compiler_params=_ARB,
    )(x, ya, yb, hr, pa_w, pb_w, wo, b_out, ln_g, ln_b)


def _tail_bwd(dxn, z, pa, pb, hr, wo_t, pa_t, pb_t, ln_g, ln_b, name, from_loss):
    t = dxn.shape[0]

    def body(dxn_ref, z_ref, pa_ref, pb_ref, hr_ref, wot_ref, pat_ref, pbt_ref, g_ref, b_ref,
             dz_ref, dpa_ref, dpb_ref, dhr_ref, dya_ref, dyb_ref, acc_ref, gbr_ref):
        @pl.when(pl.program_id(0) == 0)
        def _():
            acc_ref[...] = jnp.zeros_like(acc_ref)
            gbr_ref[...] = jnp.zeros_like(gbr_ref)

        zhat, rstd = _ln_stats(z_ref[...])
        if from_loss:
            err = zhat * g_ref[...] + b_ref[...] - dxn_ref[...]
            dxn_v = err * (1.0 / D)
            sq = jnp.sum(jnp.sum(err * err, axis=1, keepdims=True), axis=0, keepdims=True)
            acc_ref[3:4, :] += jnp.broadcast_to(sq, (1, D))
        else:
            dxn_v = dxn_ref[...]
        dz = _ln_bwd(dxn_v * g_ref[...], zhat, rstd)
        dz_ref[...] = dz
        acc_ref[0:1, :] += _colsum(dxn_v * zhat)
        acc_ref[1:2, :] += _colsum(dxn_v)
        acc_ref[2:3, :] += _colsum(dz)
        dmerged = _dot(dz.astype(BF16), wot_ref[...])
        everything = slice(None)
        sa = _sigmoid(_cols(hr_ref, everything, 0, D))
        sb = _sigmoid(_cols(hr_ref, everything, D, D))
        dpa = (dmerged * sa).astype(BF16)
        dpb = (dmerged * sb).astype(BF16)
        dpa_ref[...] = dpa
        dpb_ref[...] = dpb
        dra = dmerged * pa_ref[...].astype(F32) * (sa * (1.0 - sa))
        drb = dmerged * pb_ref[...].astype(F32) * (sb * (1.0 - sb))
        dhr_ref[:, 0:D] = dra.astype(BF16)
        dhr_ref[:, D:2 * D] = drb.astype(BF16)
        gbr_ref[0:1, 0:D] += _colsum(dra)
        gbr_ref[0:1, D:2 * D] += _colsum(drb)
        dya_ref[...] = _dot(dpa, pat_ref[...]).astype(BF16)
        dyb_ref[...] = _dot(dpb, pbt_ref[...]).astype(BF16)

    row = lambda w: pl.BlockSpec((TM, w), lambda i: (i, 0))
    vec = pl.BlockSpec((1, D), _CONST2)
    return pl.pallas_call(
        body, name=name, grid=(t // TM,),
        in_specs=[row(D), row(D), row(D), row(D), row(R_W),
                  pl.BlockSpec((D, D), _CONST2), pl.BlockSpec((D, Q_W), _CONST2), pl.BlockSpec((D, SGU_W), _CONST2),
                  vec, vec],
        out_specs=[row(D), row(D), row(D), row(R_W), row(Q_W), row(SGU_W), pl.BlockSpec((8, D), _CONST2),
                   pl.BlockSpec((8, R_W), _CONST2)],
        out_shape=[jax.ShapeDtypeStruct((t, D), F32), jax.ShapeDtypeStruct((t, D), BF16),
                   jax.ShapeDtypeStruct((t, D), BF16), jax.ShapeDtypeStruct((t, R_W), BF16),
                   jax.ShapeDtypeStruct((t, Q_W), BF16), jax.ShapeDtypeStruct((t, SGU_W), BF16),
                   jax.ShapeDtypeStruct((8, D), F32), jax.ShapeDtypeStruct((8, R_W), F32)],
        compiler_params=_ARB,
    )(dxn, z, pa, pb, hr, wo_t, pa_t, pb_t, ln_g, ln_b)


def _mixer_bwd(hm, dya, dyb, sinks, bias, vn_g, vn_b, w_s, bsb, nblk_seq, name):
    t = hm.shape[0]
    nt = t // TM

    def body(sinks_ref, hm_ref, hprev_ref, bias_ref, dya_ref, dyb_ref, vng_ref, vnb_ref, ws_ref, bsb_ref,
             dhm_ref, gbm_ref, gsk_ref, gvn_ref, gws_ref, gbs_ref, dk_carry, dv_carry):
        gi = pl.program_id(0)
        i = nt - 1 - gi

        @pl.when(gi == 0)
        def _():
            for r in (gbm_ref, gsk_ref, gvn_ref, gws_ref, gbs_ref, dk_carry, dv_carry):
                r[...] = jnp.zeros_like(r)

        tril = _tril_mask()
        wt = [jnp.where(tril, ws_ref[g], 0.0).astype(BF16) for g in range(4)]
        vng = vng_ref[...]

        def put(rows, col, val):
            dhm_ref[rows, col:col + val.shape[1]] = val.astype(BF16)
            gbm_ref[0:1, col:col + val.shape[1]] += _colsum(val)

        for s in reversed(range(NB)):
            r0 = s * BLK
            rows = slice(r0, r0 + BLK)
            bias = bias_ref[jnp.where((i * NB + s) % nblk_seq == 0, 1, 0)]
            q = _cols(hm_ref, rows, C_Q, Q_W)
            kband = _band(hm_ref, hprev_ref, s, C_K)
            vband = _band(hm_ref, hprev_ref, s, C_V)
            g_a = _cols(hm_ref, rows, C_GA, Q_W)
            sg = _sigmoid(g_a)
            dya_v = _cols(dya_ref, rows, 0, Q_W)
            d_o = dya_v * (g_a * sg)
            o_pairs, dq_pairs = [None] * 4, [None] * 4
            dkband = jnp.zeros((2 * BLK, KV_W), F32)
            dvband = jnp.zeros((2 * BLK, KV_W), F32)
            kband_t, vband_t = kband.T, vband.T
            for hk in range(N_KV):
                q4, kh, prob_t, p_sink, o_t = _attn_group(q, kband, vband_t, hk, sinks_ref, bias)
                o_pairs = _unstack_heads(o_t.T, hk, o_pairs)
                parts = []
                for g in range(4):
                    _, p, pos = _head_place(hk, g)
                    dp = d_o[:, BLK * p:BLK * (p + 1)]
                    parts.append(pltpu.roll(dp, 64, 1) if pos != hk else dp)
                do4 = _head_lanes(jnp.concatenate(parts, axis=0), hk)
                do4b = do4.astype(BF16)
                delta = _colsum(do4.T * o_t)
                vh = _head_lanes(vband, hk).astype(BF16)
                ds_t = prob_t * (_dot_nt(vh, do4b) - delta)
                dsb = ds_t.astype(BF16)
                dq4_t = _dot(_head_rows(kband_t, hk), dsb)
                dq_pairs = _unstack_heads(dq4_t.T * ATTN_SCALE, hk, dq_pairs)
                dkband = dkband + _head_lanes(_dot(dsb, q4), hk)
                dvband = dvband + _dot(prob_t.astype(BF16), do4b)
                dsk = p_sink * delta
                for g in range(4):
                    j = 4 * hk + g
                    tot = jnp.sum(dsk[:, BLK * g:BLK * (g + 1)], axis=1, keepdims=True)
                    gsk_ref[j:j + 1, :] += jnp.broadcast_to(-tot, (1, 128))
            attn = jnp.concatenate(o_pairs, axis=1)
            put(rows, C_Q, jnp.concatenate(dq_pairs, axis=1))
            put(rows, C_K, dkband[BLK:2 * BLK] + dk_carry[...])
            put(rows, C_V, dvband[BLK:2 * BLK] + dv_carry[...])
            dk_carry[...] = dkband[0:BLK]
            dv_carry[...] = dvband[0:BLK]
            put(rows, C_GA, dya_v * attn * (sg * (1.0 + g_a * (1.0 - sg))))
            u_b = _cols(hm_ref, rows, C_UB, SGU_W)
            v_b = _cols(hm_ref, rows, C_VB, SGU_W)
            g_b = _cols(hm_ref, rows, C_GB, SGU_W)
            u, vhat, rstd, vn, mixed = _sgu_fwd(u_b, v_b, vng, vnb_ref[...], wt, bsb_ref)
            sgb = _sigmoid(g_b)
            silu_b = g_b * sgb
            dyb_v = _cols(dyb_ref, rows, 0, SGU_W)
            du = dyb_v * mixed * silu_b
            dmixed = dyb_v * u * silu_b
            put(rows, C_GB, dyb_v * u * mixed * (sgb * (1.0 + g_b * (1.0 - sgb))))
            dvn_parts = []
            for g in range(4):
                cols = slice(BLK * g, BLK * (g + 1))
                dmg = dmixed[:, cols]
                dmgb = dmg.astype(BF16)
                dvn_parts.append(_dot_tn(wt[g], dmgb))
                gws_ref[g] += jnp.where(tril, _dot_nt(dmgb, vn[:, cols].astype(BF16)), 0.0)
                gbs_ref[g] += dmg
            dvn = jnp.concatenate(dvn_parts, axis=1)
            gvn_ref[0:1, :] += _colsum(dvn * vhat)
            gvn_ref[1:2, :] += _colsum(dvn)
            dv = _ln_bwd(dvn * vng, vhat, rstd)
            put(rows, C_UB, du * _dgelu(u_b))
            put(rows, C_VB, dv * _dgelu(v_b))

        @pl.when(gi == nt - 1)
        def _():
            for g in range(4):
                gbs_ref[g] = jnp.broadcast_to(jnp.sum(gbs_ref[g], axis=1, keepdims=True), (BLK, BLK))

    row = lambda w: pl.BlockSpec((TM, w), lambda g: (nt - 1 - g, 0))
    return pl.pallas_call(
        body, name=name, grid=(nt,),
        in_specs=_mixer_in_specs(nt, True) + [
            row(Q_W), row(SGU_W),
            pl.BlockSpec((1, SGU_W), _CONST2), pl.BlockSpec((1, SGU_W), _CONST2),
            pl.BlockSpec((4, BLK, BLK), _CONST3), pl.BlockSpec((4, BLK, BLK), _CONST3)],
        out_specs=[row(MAIN_W), pl.BlockSpec((8, MAIN_W), _CONST2), pl.BlockSpec((8, 128), _CONST2),
                   pl.BlockSpec((8, SGU_W), _CONST2), pl.BlockSpec((4, BLK, BLK), _CONST3),
                   pl.BlockSpec((4, BLK, BLK), _CONST3)],
        out_shape=[jax.ShapeDtypeStruct((t, MAIN_W), BF16), jax.ShapeDtypeStruct((8, MAIN_W), F32),
                   jax.ShapeDtypeStruct((8, 128), F32), jax.ShapeDtypeStruct((8, SGU_W), F32),
                   jax.ShapeDtypeStruct((4, BLK, BLK), F32), jax.ShapeDtypeStruct((4, BLK, BLK), F32)],
        scratch_shapes=[pltpu.VMEM((BLK, KV_W), F32), pltpu.VMEM((BLK, KV_W), F32)],
        compiler_params=_ARB,
    )(sinks, hm, hm, bias, dya, dyb, vn_g, vn_b, w_s, bsb)


def _dx_inproj(dz, dhm, dhr, w_t, name):
    t = dz.shape[0]

    def body(dz_ref, dhm_ref, dhr_ref, wt_ref, dx_ref):
        dx_ref[...] = (ALPHA * dz_ref[...] + _dot(dhm_ref[...], wt_ref[0:MAIN_W, :])
                       + _dot(dhr_ref[...], wt_ref[MAIN_W:IN_COLS, :]))

    row = lambda w: pl.BlockSpec((TM, w), lambda i: (i, 0))
    return pl.pallas_call(
        body, name=name, grid=(t // TM,),
        in_specs=[row(D), row(MAIN_W), row(R_W), pl.BlockSpec((IN_COLS, D), _CONST2)],
        out_specs=row(D), out_shape=jax.ShapeDtypeStruct((t, D), F32), compiler_params=_ARB,
    )(dz, dhm, dhr, w_t)


def _wgrad(a, b, tn, name, tk=1024):
    t, m = a.shape
    n = b.shape[1]
    nk = t // tk

    def body(a_ref, b_ref, o_ref):
        @pl.when(pl.program_id(1) == 0)
        def _():
            o_ref[...] = jnp.zeros_like(o_ref)

        o_ref[...] += _dot_tn(a_ref[...].astype(BF16), b_ref[...].astype(BF16))

    return pl.pallas_call(
        body, name=name, grid=(n // tn, nk),
        in_specs=[pl.BlockSpec((tk, m), lambda j, k: (k, 0)), pl.BlockSpec((tk, tn), lambda j, k: (k, j))],
        out_specs=pl.BlockSpec((m, tn), lambda j, k: (0, j)),
        out_shape=jax.ShapeDtypeStruct((m, n), F32),
        compiler_params=pltpu.CompilerParams(dimension_semantics=("arbitrary", "arbitrary"), vmem_limit_bytes=VMEM_LIMIT),
    )(a, b)


def _ln_in_bwd(dx0, x, g, name):
    t = x.shape[0]

    def body(dx0_ref, x_ref, g_ref, gx_ref, acc_ref):
        @pl.when(pl.program_id(0) == 0)
        def _():
            acc_ref[...] = jnp.zeros_like(acc_ref)

        d = dx0_ref[...]
        xhat, rstd = _ln_stats(x_ref[...])
        gx_ref[...] = _ln_bwd(d * g_ref[...], xhat, rstd)
        acc_ref[0:1, :] += _colsum(d * xhat)
        acc_ref[1:2, :] += _colsum(d)

    row = pl.BlockSpec((TM_EW, D), lambda i: (i, 0))
    return pl.pallas_call(
        body, name=name, grid=(t // TM_EW,),
        in_specs=[row, row, pl.BlockSpec((1, D), _CONST2)],
        out_specs=[row, pl.BlockSpec((8, D), _CONST2)],
        out_shape=[jax.ShapeDtypeStruct((t, D), F32), jax.ShapeDtypeStruct((8, D), F32)],
        compiler_params=_ARB,
    )(dx0, x, g)


_ANY = pl.BlockSpec(memory_space=pl.ANY)


def _place():
    return lax.axis_index("x"), lax.axis_index("y"), lax.axis_index("c")


def _allgather8(xs, name):
    n = len(xs)

    def body(*refs):
        x_refs, o_refs = refs[:n], refs[n:2 * n]
        send_sems, recv_sems, local_sems = refs[2 * n:]
        x, y, c = _place()
        me, sib = (x, y, c), (x, y, 1 - c)
        chips = [(1 - x, y), (x, 1 - y), (1 - x, 1 - y)]

        def copy(a, k, block, to, src=None):
            dst = o_refs[a].at[4 * block[0] + 2 * block[1] + block[2]]
            return pltpu.make_async_remote_copy(
                src_ref=dst if src is None else src, dst_ref=dst, send_sem=send_sems.at[7 * a + k],
                recv_sem=recv_sems.at[7 * a + k], device_id=to, device_id_type=MESH)

        mine = [pltpu.make_async_copy(x_refs[a], o_refs[a].at[4 * x + 2 * y + c], local_sems.at[a]) for a in range(n)]
        for cp in mine:
            cp.start()
        sent = []
        for a in range(n):
            sent.append(copy(a, 0, me, sib, src=x_refs[a]))
            sent += [copy(a, 1 + j, me, (*chip, c), src=x_refs[a]) for j, chip in enumerate(chips)]
        for cp in sent:
            cp.start()
        for j, chip in enumerate(chips):
            for a in range(n):
                copy(a, 1 + j, (*chip, c), me).wait_recv()
                fwd = copy(a, 4 + j, (*chip, c), sib)
                fwd.start()
                sent.append(fwd)
        for a in range(n):
            copy(a, 0, sib, me).wait_recv()
            for j, chip in enumerate(chips):
                copy(a, 4 + j, (*chip, 1 - c), me).wait_recv()
        for cp in sent:
            cp.wait_send()
        for cp in mine:
            cp.wait()

    return pl.pallas_call(
        body, name=name, in_specs=[_ANY] * n, out_specs=[_ANY] * n,
        out_shape=[jax.ShapeDtypeStruct((N_DEV,) + v.shape, v.dtype) for v in xs],
        scratch_shapes=[pltpu.SemaphoreType.DMA((7 * n,)), pltpu.SemaphoreType.DMA((7 * n,)),
                        pltpu.SemaphoreType.DMA((n,))],
    )(*xs)


def _swap_sibling(gs, name):
    n = len(gs)

    def body(*refs):
        g_refs, r_refs = refs[:n], refs[n:2 * n]
        send_sems, recv_sems = refs[2 * n:]
        x, y, c = _place()
        cps = [pltpu.make_async_remote_copy(
            src_ref=g_refs[a].at[1 - c], dst_ref=r_refs[a], send_sem=send_sems.at[a], recv_sem=recv_sems.at[a],
            device_id=(x, y, 1 - c), device_id_type=MESH) for a in range(n)]
        for cp in cps:
            cp.start()
        for cp in cps:
            cp.wait()

    return pl.pallas_call(
        body, name=name, in_specs=[_ANY] * n, out_specs=[_ANY] * n,
        out_shape=[jax.ShapeDtypeStruct(v.shape[1:], v.dtype) for v in gs],
        scratch_shapes=[pltpu.SemaphoreType.DMA((n,)), pltpu.SemaphoreType.DMA((n,))],
    )(*gs)


def _row_tile(rows, cap):
    return max(d for d in range(16, cap + 1, 16) if rows % d == 0)


def _pair_sum(g, r, name):
    _, _, rows, l = g.shape
    tr = _row_tile(rows, 512)

    def body(c_ref, g_ref, r_ref, o_ref):
        o_ref[...] = (g_ref[0].astype(F32) + r_ref[...].astype(F32)).astype(o_ref.dtype)

    return pl.pallas_call(
        body, name=name,
        grid_spec=pltpu.PrefetchScalarGridSpec(
            num_scalar_prefetch=1, grid=(4, rows // tr),
            in_specs=[pl.BlockSpec((1, 1, tr, l), lambda q, i, c_ref: (c_ref[0], q, i, 0)),
                      pl.BlockSpec((1, tr, l), lambda q, i, c_ref: (q, i, 0))],
            out_specs=pl.BlockSpec((1, tr, l), lambda q, i, c_ref: (q, i, 0))),
        out_shape=jax.ShapeDtypeStruct((4, rows, l), g.dtype),
        compiler_params=pltpu.CompilerParams(dimension_semantics=("arbitrary", "arbitrary"), vmem_limit_bytes=VMEM_LIMIT),
    )(lax.axis_index("c").astype(jnp.int32).reshape(1), g, r)


def _adamw(parts, w, m, v, name, own=None):
    ns, rows, l = parts.shape
    tr = rows if rows <= 256 else _row_tile(rows, 256)
    c1 = 1.0 - ADAM_B1 ** ADAM_STEP
    c2 = 1.0 - ADAM_B2 ** ADAM_STEP

    def body(q_ref, *refs):
        own_ref = refs[0] if own is not None else None
        p_ref, w_ref, m_ref, v_ref, g_ref, d_ref, nm_ref, nv_ref = refs[-8:]
        g = None
        for k in range(ns):
            term = p_ref[k].astype(F32)
            if own_ref is not None:
                term = jnp.where(q_ref[0] == k, own_ref[0].astype(F32), term)
            g = term if g is None else g + term
        g_ref[...] = g
        nm = ADAM_B1 * m_ref[...] + (1.0 - ADAM_B1) * g
        nv = ADAM_B2 * v_ref[...] + (1.0 - ADAM_B2) * (g * g)
        nm_ref[...] = nm
        nv_ref[...] = nv
        d_ref[...] = -ADAM_LR * ((nm / c1) / (jnp.sqrt(nv / c2) + ADAM_EPS) + ADAM_WD * w_ref[...])

    row = pl.BlockSpec((tr, l), lambda i, q: (i, 0))
    own_specs = [] if own is None else [pl.BlockSpec((1, tr, l), lambda i, q: (q[0], i, 0))]
    chip = (2 * lax.axis_index("x") + lax.axis_index("y")).astype(jnp.int32).reshape(1)
    return pl.pallas_call(
        body, name=name,
        grid_spec=pltpu.PrefetchScalarGridSpec(
            num_scalar_prefetch=1, grid=(rows // tr,),
            in_specs=own_specs + [pl.BlockSpec((ns, tr, l), lambda i, q: (0, i, 0)), row, row, row],
            out_specs=[row] * 4),
        out_shape=[jax.ShapeDtypeStruct((rows, l), F32)] * 4, compiler_params=_ARB,
    )(chip, *([] if own is None else [own]), parts, w, m, v)


_HBM = pl.BlockSpec(memory_space=pltpu.HBM)
_SEM = pl.BlockSpec(memory_space=pltpu.SEMAPHORE)
_EFFECT = pltpu.SideEffectType.DATAFLOW_SIDE_EFFECTING


def _plan_all(x, y, c):
    me = 4 * x + 2 * y + c
    peers = [(x, y, 1 - c), (1 - x, y, c), (x, 1 - y, c), (1 - x, 1 - y, c),
             (1 - x, y, 1 - c), (x, 1 - y, 1 - c), (1 - x, 1 - y, 1 - c)]
    return [(None, me, p, 4 * p[0] + 2 * p[1] + p[2]) for p in peers]


def _plan_chips(x, y, c):
    me = 2 * x + y
    return [(2 * qx + qy, me, (qx, qy, c), 2 * qx + qy) for qx, qy in ((1 - x, y), (x, 1 - y), (1 - x, 1 - y))]


def _split_copies(plan, src_refs, land_refs, send_sems, recv_sems, arrival):
    n = len(src_refs)
    entries = plan(*_place())
    per = len(entries)
    cps = []
    for a in range(n):
        for k, (src_slot, dst_slot, peer, back_slot) in enumerate(entries):
            src = src_refs[a] if src_slot is None else src_refs[a].at[src_slot]
            cps.append(pltpu.make_async_remote_copy(
                src_ref=src, dst_ref=land_refs[a].at[back_slot if arrival else dst_slot],
                send_sem=send_sems.at[per * a + k], recv_sem=recv_sems.at[per * a + k],
                device_id=peer, device_id_type=MESH))
    return cps


def _split_start(srcs, lands, plan, per, name):
    n = len(srcs)

    def body(*refs):
        for cp in _split_copies(plan, refs[:n], refs[n:2 * n], refs[2 * n], refs[2 * n + 1], False):
            cp.start()
        refs[-1][...] = jnp.zeros_like(refs[-1])

    both = list(srcs) + list(lands)
    outs = pl.pallas_call(
        body, name=name,
        out_shape=(pltpu.SemaphoreType.DMA((per * n,)), pltpu.SemaphoreType.DMA((per * n,)),
                   *[pltpu.HBM(v.shape, v.dtype) for v in both], jax.ShapeDtypeStruct((8, 128), F32)),
        in_specs=[_HBM] * (2 * n),
        out_specs=(_SEM, _SEM, *[_HBM] * (2 * n), pl.BlockSpec(memory_space=pltpu.VMEM)),
        input_output_aliases={i: 2 + i for i in range(2 * n)},
        compiler_params=pltpu.CompilerParams(has_side_effects=_EFFECT),
    )(*[pltpu.with_memory_space_constraint(v, pltpu.HBM) for v in both])
    return outs[0], outs[1], list(outs[2:2 + 2 * n]), outs[-1]


def _split_wait(send_sems, recv_sems, thru, plan, after, name):
    n = len(thru) // 2

    def body(*refs):
        for cp in _split_copies(plan, refs[:n], refs[n:2 * n], refs[2 * n], refs[2 * n + 1], True):
            cp.wait_send()
            cp.wait_recv()

    outs = pl.pallas_call(
        body, name=name, out_shape=tuple(pltpu.HBM(v.shape, v.dtype) for v in thru),
        in_specs=[_HBM] * (2 * n) + [_SEM, _SEM, pl.BlockSpec(memory_space=pl.ANY)],
        out_specs=[_HBM] * (2 * n), input_output_aliases={i: i for i in range(2 * n)},
        compiler_params=pltpu.CompilerParams(has_side_effects=_EFFECT),
    )(*thru, send_sems, recv_sems, after)
    return list(outs[:n]), list(outs[n:])


_SMALL_IN = ("ln_in_g", "ln_in_b")
_SMALL = ("b_in", "sinks", "vn_g", "vn_b", "w_s", "b_s", "b_out", "ln_g", "ln_b")


def _pack_small(d, names):
    out = []
    for n in names:
        f = d[n].reshape(-1)
        out.append(jnp.pad(f, (0, (-f.shape[0]) % 1024)))
    return jnp.concatenate(out).reshape(-1, 128)


def _unpack_small(p, like, names):
    flat, off, out = p.reshape(-1), 0, {}
    for n in names:
        size = like[n].size
        out[n] = flat[off:off + size].reshape(like[n].shape)
        off += size + (-size) % 1024
    return out


def _owner_major(g, axis):
    sh = g.shape
    g = g.reshape(sh[:axis] + (4, 2, sh[axis] // N_DEV) + sh[axis + 1:])
    g = jnp.moveaxis(g, (axis, axis + 1), (1, 0))
    return g


def kernel(x, ln_in_g, ln_in_b, w_in, b_in, sinks, vn_g, vn_b, w_s, b_s, p_a, p_b, w_out, b_out, ln_g, ln_b, loss_target, m_ln_in_g, m_ln_in_b, m_w_in, m_b_in, m_sinks, m_vn_g, m_vn_b, m_w_s, m_b_s, m_p_a, m_p_b, m_w_out, m_b_out, m_ln_g, m_ln_b, v_ln_in_g, v_ln_in_b, v_w_in, v_b_in, v_sinks, v_vn_g, v_vn_b, v_w_s, v_b_s, v_p_a, v_p_b, v_w_out, v_b_out, v_ln_g, v_ln_b):
    nseq, seq, _ = x.shape
    t = nseq * seq
    nblk_seq = seq // BLK
    x2 = x.reshape(t, D)
    tgt = loss_target.reshape(t, D)

    def blocks(l):
        return [w_in[l].astype(BF16), p_a[l].astype(BF16), p_b[l].astype(BF16), w_out[l].astype(BF16)]

    def full_weights(g):
        w_full = jnp.moveaxis(g[0], 0, 1).reshape(D, IN_COLS)
        pa_full = jnp.moveaxis(g[1], 0, 1).reshape(Q_W, D)
        pb_full = jnp.moveaxis(g[2], 0, 1).reshape(SGU_W, D)
        wo_full = g[3].reshape(D, D)
        return dict(w=w_full, pa=pa_full, pb=pb_full, wo=wo_full, w_t=w_full.T,
                    pa_t=pa_full.T, pb_t=pb_full.T, wo_t=wo_full.T)

    me = 4 * lax.axis_index("x") + 2 * lax.axis_index("y") + lax.axis_index("c")
    gathered0 = _allgather8(blocks(0), "allgather_weights0")
    blocks1, gathered0 = lax.optimization_barrier((blocks(1), gathered0))
    lands1 = [lax.empty((N_DEV,) + v.shape, v.dtype) for v in blocks1]
    ag_send, ag_recv, ag_thru, ag_token = _split_start(blocks1, lands1, _plan_all, 7, "allgather_weights1_start")
    weights = [full_weights(gathered0), None]
    bsb = jnp.broadcast_to(b_s[:, :, :, None], (DEPTH, 4, BLK, BLK))
    bias = _band_bias()

    xs = [_ln_fwd(x2, ln_in_g + ag_token[0, 0], ln_in_b, "ln_in_fwd")]
    saved = []
    for l in range(DEPTH):
        if l == 1:
            sent, landed = _split_wait(ag_send, ag_recv, ag_thru, _plan_all, xs[1], "allgather_weights1_wait")
            weights[1] = full_weights(
                [lax.dynamic_update_index_in_dim(g, b, me, 0) for g, b in zip(landed, sent)])
        wl = weights[l]
        last = l == DEPTH - 1
        hm, hr = _inproj(xs[l], wl["w"], b_in[l].reshape(1, -1), f"inproj{l}")
        ya, yb = _mixer_fwd(hm, sinks[l], bias, vn_g[l].reshape(1, -1), vn_b[l].reshape(1, -1), w_s[l], bsb[l],
                            nblk_seq, f"mixer_fwd{l}")
        outs = _tail_fwd(xs[l], ya, yb, hr, wl["pa"], wl["pb"], wl["wo"], b_out[l].reshape(1, D),
                         ln_g[l].reshape(1, D), ln_b[l].reshape(1, D), f"tail_fwd{l}", last)
        saved.append((hm, hr, ya, yb) + tuple(outs[:4]))
        if not last:
            xs.append(outs[4])

    small = {n: [None] * DEPTH for n in _SMALL}
    names = ("w_in", "p_a", "p_b", "w_out")
    owner_axis = {"w_in": 1, "p_a": 1, "p_b": 1, "w_out": 0}
    token = jnp.zeros((8, 128), F32)
    dx = tgt
    split = [None] * DEPTH
    for l in reversed(range(DEPTH)):
        hm, hr, ya, yb, pa, pb, merged, z = saved[l]
        wl = weights[l]
        dz, dpa, dpb, dhr, dya, dyb, acc, gbr = _tail_bwd(
            dx, z, pa, pb, hr, wl["wo_t"], wl["pa_t"], wl["pb_t"], ln_g[l].reshape(1, D) + token[0, 0],
            ln_b[l].reshape(1, D), f"tail_bwd{l}", l == DEPTH - 1)
        if l == DEPTH - 1:
            loss = lax.psum(acc[3, 0] * (0.5 / D), ("x", "y", "c"))
        dhm, gbm, gsk, gvn, gws, gbs = _mixer_bwd(
            hm, dya, dyb, sinks[l], bias, vn_g[l].reshape(1, -1), vn_b[l].reshape(1, -1), w_s[l], bsb[l],
            nblk_seq, f"mixer_bwd{l}")
        grads = {"w_in": jnp.concatenate([_wgrad(xs[l], dhm, MAIN_W // 2, f"wgrad_in_main{l}"),
                                          _wgrad(xs[l], dhr, R_W // 2, f"wgrad_in_route{l}")], axis=1),
                 "p_a": _wgrad(ya, dpa, D, f"wgrad_pa{l}"), "p_b": _wgrad(yb, dpb, D, f"wgrad_pb{l}"),
                 "w_out": _wgrad(merged, dz, D, f"wgrad_out{l}")}
        small["b_in"][l] = jnp.concatenate([gbm[0], gbr[0]])
        small["sinks"][l] = gsk[:, 0]
        small["vn_g"][l], small["vn_b"][l] = gvn[0], gvn[1]
        small["w_s"][l], small["b_s"][l] = gws, gbs[:, :, 0]
        small["ln_g"][l], small["ln_b"][l], small["b_out"][l] = acc[0], acc[1], acc[2]
        parts = [_owner_major(grads[n], owner_axis[n]).astype(BF16) for n in names]
        if l == 0:
            packed = _pack_small({n: jnp.stack(v) for n, v in small.items()}, _SMALL)
            parts.append(jnp.broadcast_to(packed[None, None], (2, 4) + packed.shape))
        from_sib = _swap_sibling(parts, f"rs_sibling{l}")
        pair = [_pair_sum(g, r, f"pair_sum{l}_{a}") for a, (g, r) in enumerate(zip(parts, from_sib))]
        lands = [jnp.zeros(p.shape, p.dtype) for p in pair]
        split[l] = _split_start(pair, lands, _plan_chips, 3, f"rs_chips{l}_start")
        token = split[l][3]
        dz, token = lax.optimization_barrier((dz, token))
        dx = _dx_inproj(dz, dhm, dhr, wl["w_t"], f"dx_inproj{l}")
    grad_x, acc_in = _ln_in_bwd(dx, x2, ln_in_g.reshape(1, D), "ln_in_bwd")
    (all_in,) = _allgather8([acc_in], "allgather_ln_in")

    given = {"w_in": (w_in, m_w_in, v_w_in), "p_a": (p_a, m_p_a, v_p_a), "p_b": (p_b, m_p_b, v_p_b),
             "w_out": (w_out, m_w_out, v_w_out)}
    res = {n: [] for n in names}
    small_parts = None
    for l in range(DEPTH):
        pair, by_chip = _split_wait(split[l][0], split[l][1], split[l][2], _plan_chips, all_in, f"rs_chips{l}_wait")
        for a, n in enumerate(names):
            rows, lanes = by_chip[a].shape[1:]
            res[n].append(_adamw(by_chip[a], *[v[l].reshape(rows, lanes) for v in given[n]], f"adamw{l}_{n}",
                                 own=pair[a]))
        if l == 0:
            small_parts = (by_chip[4], pair[4])
    res = {n: [jnp.stack(o).reshape(given[n][0].shape) for o in zip(*res[n])] for n in names}

    w_small = dict(ln_in_g=ln_in_g, ln_in_b=ln_in_b, b_in=b_in, sinks=sinks, vn_g=vn_g, vn_b=vn_b, w_s=w_s, b_s=b_s,
                   b_out=b_out, ln_g=ln_g, ln_b=ln_b)
    m_small = dict(ln_in_g=m_ln_in_g, ln_in_b=m_ln_in_b, b_in=m_b_in, sinks=m_sinks, vn_g=m_vn_g, vn_b=m_vn_b,
                   w_s=m_w_s, b_s=m_b_s, b_out=m_b_out, ln_g=m_ln_g, ln_b=m_ln_b)
    v_small = dict(ln_in_g=v_ln_in_g, ln_in_b=v_ln_in_b, b_in=v_b_in, sinks=v_sinks, vn_g=v_vn_g, vn_b=v_vn_b,
                   w_s=v_w_s, b_s=v_b_s, b_out=v_b_out, ln_g=v_ln_g, ln_b=v_ln_b)
    outs = _adamw(small_parts[0], *[_pack_small(d, _SMALL) for d in (w_small, m_small, v_small)], "adamw_small",
                  own=small_parts[1])
    outs_in = _adamw(all_in, *[jnp.pad(jnp.stack([d[n] for n in _SMALL_IN]), ((0, 6), (0, 0)))
                               for d in (w_small, m_small, v_small)], "adamw_ln_in")
    for k in range(4):
        u = _unpack_small(outs[k], w_small, _SMALL)
        u.update({n: outs_in[k][r] for r, n in enumerate(_SMALL_IN)})
        for n in u:
            res.setdefault(n, [None] * 4)[k] = u[n]

    order = ("ln_in_g", "ln_in_b", "w_in", "b_in", "sinks", "vn_g", "vn_b", "w_s", "b_s", "p_a", "p_b", "w_out",
             "b_out", "ln_g", "ln_b")
    return (loss, grad_x.reshape(x.shape), *[res[n][0] for n in order], *[res[n][1] for n in order],
            *[res[n][2] for n in order], *[res[n][3] for n in order])
```

```python
import jax
import jax.numpy as jnp
from jax import lax
from jax.experimental import pallas as pl
from jax.experimental.pallas import tpu as pltpu

F32 = jnp.float32
BF16 = jnp.bfloat16

D = 1024
BLK = 128
N_KV = 2
Q_W, KV_W, SGU_W = 512, 128, 512
C_Q, C_K, C_V, C_GA, C_UB, C_VB, C_GB = 0, 512, 640, 768, 1280, 1792, 2304
MAIN_W = 2816
R_W = 2048
IN_COLS = MAIN_W + R_W
N_DEV = 8
SHARD_COLS = IN_COLS // N_DEV

DEPTH = 2
ALPHA = (2.0 * DEPTH) ** 0.25
LN_EPS = 1e-5
ATTN_SCALE = 0.125
NEG = float(jnp.finfo(jnp.float32).min)

ADAM_LR, ADAM_B1, ADAM_B2, ADAM_EPS, ADAM_WD, ADAM_STEP = 0.001, 0.9, 0.999, 1e-08, 0.01, 10

TM = 256
TM_EW = 512
NB = TM // BLK
MESH = pl.DeviceIdType.MESH
VMEM_LIMIT = 56 * 1024 * 1024

_ARB = pltpu.CompilerParams(dimension_semantics=("arbitrary",), vmem_limit_bytes=VMEM_LIMIT)


def _sigmoid(x):
    return 1.0 / (1.0 + jnp.exp(-x))


_GELU_C = 0.7978845608028654
_GELU_A = 0.044715


def _gelu(x):
    return 0.5 * x * (1.0 + jnp.tanh(_GELU_C * (x + _GELU_A * x * x * x)))


def _dgelu(x):
    t = jnp.tanh(_GELU_C * (x + _GELU_A * x * x * x))
    return 0.5 * (1.0 + t) + 0.5 * x * (1.0 - t * t) * (_GELU_C * (1.0 + 3.0 * _GELU_A * x * x))


def _ln_stats(x):
    mu = jnp.mean(x, axis=-1, keepdims=True)
    xc = x - mu
    var = jnp.mean(xc * xc, axis=-1, keepdims=True)
    rstd = lax.rsqrt(var + LN_EPS)
    return xc * rstd, rstd


def _ln_bwd(dy_g, xhat, rstd):
    m1 = jnp.mean(dy_g, axis=-1, keepdims=True)
    m2 = jnp.mean(dy_g * xhat, axis=-1, keepdims=True)
    return rstd * (dy_g - m1 - xhat * m2)


def _colsum(x):
    return jnp.sum(x, axis=0, keepdims=True)


def _dot(a, b):
    return jnp.dot(a, b, preferred_element_type=F32)


def _dot_nt(a, b):
    return lax.dot_general(a, b, (((1,), (1,)), ((), ())), preferred_element_type=F32)


def _dot_tn(a, b):
    return lax.dot_general(a, b, (((0,), (0,)), ((), ())), preferred_element_type=F32)


def _head_place(hk, g):
    j = 4 * hk + g
    return j, j // 2, j % 2


def _head_rows(x, hk):
    d = lax.broadcasted_iota(jnp.int32, x.shape, 0)
    return jnp.where((d >= 64 * hk) & (d < 64 * hk + 64), x, 0.0).astype(BF16)


def _head_lanes(x, hk):
    d = lax.broadcasted_iota(jnp.int32, x.shape, 1)
    return jnp.where((d >= 64 * hk) & (d < 64 * hk + 64), x, 0.0)


def _band_bias():
    kpos = lax.broadcasted_iota(jnp.int32, (2 * BLK, 4 * BLK), 0)
    row = lax.broadcasted_iota(jnp.int32, (2 * BLK, 4 * BLK), 1) & (BLK - 1)
    band = (kpos > row) & (kpos <= row + BLK)
    return jnp.stack([jnp.where(band, 0.0, NEG), jnp.where(band & (kpos >= BLK), 0.0, NEG)]).astype(F32)


def _attn_group(q, kband, vband_t, hk, sinks_ref, bias):
    kh = _head_lanes(kband, hk).astype(BF16)
    parts = []
    for g in range(4):
        _, p, pos = _head_place(hk, g)
        qp = q[:, BLK * p:BLK * (p + 1)] * ATTN_SCALE
        if pos != hk:
            qp = pltpu.roll(qp, 64, 1)
        parts.append(qp.astype(BF16))
    q4 = jnp.concatenate(parts, axis=0)
    s_t = _dot_nt(kh, q4) + bias
    sink_row = jnp.concatenate(
        [jnp.full((1, BLK), sinks_ref[4 * hk + g], F32) for g in range(4)], axis=1)
    m = jnp.maximum(jnp.max(s_t, axis=0, keepdims=True), sink_row)
    p_un = jnp.exp(s_t - m)
    e_sink = jnp.exp(sink_row - m)
    inv = 1.0 / (jnp.sum(p_un, axis=0, keepdims=True) + e_sink)
    prob_t = p_un * inv
    o_t = _dot(_head_rows(vband_t, hk), prob_t.astype(BF16))
    return q4, kh, prob_t, e_sink * inv, o_t


def _unstack_heads(x4, hk, pairs):
    for g in range(4):
        _, p, pos = _head_place(hk, g)
        xg = x4[BLK * g:BLK * (g + 1)]
        if pos != hk:
            xg = pltpu.roll(xg, 64, 1)
        pairs[p] = xg if pairs[p] is None else pairs[p] + xg
    return pairs


def _attn_fwd(q, kband, vband, sinks_ref, bias):
    pairs = [None] * 4
    vband_t = vband.T
    for hk in range(N_KV):
        o_t = _attn_group(q, kband, vband_t, hk, sinks_ref, bias)[-1]
        pairs = _unstack_heads(o_t.T, hk, pairs)
    return jnp.concatenate(pairs, axis=1)


def _tril_mask():
    r = lax.broadcasted_iota(jnp.int32, (BLK, BLK), 0)
    c = lax.broadcasted_iota(jnp.int32, (BLK, BLK), 1)
    return c <= r


def _sgu_fwd(u_b, v_b, vn_g, vn_b, wt, bsb_ref):
    u = _gelu(u_b)
    v = _gelu(v_b)
    vhat, rstd = _ln_stats(v)
    vn = vhat * vn_g + vn_b
    mixed = jnp.concatenate(
        [_dot(wt[g], vn[:, BLK * g:BLK * (g + 1)].astype(BF16)) + bsb_ref[g] for g in range(4)], axis=1)
    return u, vhat, rstd, vn, mixed


def _cols(ref, rows, col, width):
    return ref[rows, col:col + width].astype(F32)


def _band(hm_ref, hprev_ref, s, col):
    r0 = s * BLK
    cur = hm_ref[r0:r0 + BLK, col:col + KV_W]
    if s == 0:
        off = 0 if col == C_K else KV_W
        prev = hprev_ref[:, off:off + KV_W]
    else:
        prev = hm_ref[r0 - BLK:r0, col:col + KV_W]
    return jnp.concatenate([prev, cur], axis=0).astype(F32)


def _mixer_in_specs(nt, rev):
    def tile(g):
        return nt - 1 - g if rev else g

    return [
        pl.BlockSpec(memory_space=pltpu.SMEM),
        pl.BlockSpec((TM, MAIN_W), lambda g: (tile(g), 0)),
        pl.BlockSpec((BLK, 2 * KV_W), lambda g: (jnp.maximum(tile(g) * NB - 1, 0), 2)),
        pl.BlockSpec((2, 2 * BLK, 4 * BLK), lambda g: (0, 0, 0)),
    ]


_CONST2 = lambda g: (0, 0)
_CONST3 = lambda g: (0, 0, 0)


def _ln_fwd(x, g, b, name):
    t = x.shape[0]

    def body(x_ref, g_ref, b_ref, o_ref):
        xhat, _ = _ln_stats(x_ref[...])
        o_ref[...] = xhat * g_ref[...] + b_ref[...]

    return pl.pallas_call(
        body, name=name, grid=(t // TM_EW,),
        in_specs=[pl.BlockSpec((TM_EW, D), lambda i: (i, 0)), pl.BlockSpec((1, D), _CONST2),
                  pl.BlockSpec((1, D), _CONST2)],
        out_specs=pl.BlockSpec((TM_EW, D), lambda i: (i, 0)),
        out_shape=jax.ShapeDtypeStruct((t, D), F32), compiler_params=_ARB,
    )(x, g.reshape(1, D), b.reshape(1, D))


def _inproj(x, w, b, name):
    t = x.shape[0]

    def body(x_ref, w_ref, b_ref, hm_ref, hr_ref):
        xb = x_ref[...].astype(BF16)
        hm_ref[...] = (_dot(xb, w_ref[:, 0:MAIN_W]) + b_ref[:, 0:MAIN_W]).astype(BF16)
        hr_ref[...] = (_dot(xb, w_ref[:, MAIN_W:IN_COLS]) + b_ref[:, MAIN_W:IN_COLS]).astype(BF16)

    return pl.pallas_call(
        body, name=name, grid=(t // TM,),
        in_specs=[pl.BlockSpec((TM, D), lambda i: (i, 0)),
                  pl.BlockSpec((D, IN_COLS), _CONST2), pl.BlockSpec((1, IN_COLS), _CONST2)],
        out_specs=[pl.BlockSpec((TM, MAIN_W), lambda i: (i, 0)), pl.BlockSpec((TM, R_W), lambda i: (i, 0))],
        out_shape=[jax.ShapeDtypeStruct((t, MAIN_W), BF16), jax.ShapeDtypeStruct((t, R_W), BF16)],
        compiler_params=_ARB,
    )(x, w, b)


def _mixer_fwd(hm, sinks, bias, vn_g, vn_b, w_s, bsb, nblk_seq, name):
    t = hm.shape[0]
    nt = t // TM

    def body(sinks_ref, hm_ref, hprev_ref, bias_ref, vng_ref, vnb_ref, ws_ref, bsb_ref, ya_ref, yb_ref):
        i = pl.program_id(0)
        tril = _tril_mask()
        wt = [jnp.where(tril, ws_ref[g], 0.0).astype(BF16) for g in range(4)]
        for s in range(NB):
            r0 = s * BLK
            rows = slice(r0, r0 + BLK)
            bias = bias_ref[jnp.where((i * NB + s) % nblk_seq == 0, 1, 0)]
            attn = _attn_fwd(_cols(hm_ref, rows, C_Q, Q_W), _band(hm_ref, hprev_ref, s, C_K),
                             _band(hm_ref, hprev_ref, s, C_V), sinks_ref, bias)
            g_a = _cols(hm_ref, rows, C_GA, Q_W)
            ya_ref[rows, :] = (attn * (g_a * _sigmoid(g_a))).astype(BF16)
            u, _, _, _, mixed = _sgu_fwd(_cols(hm_ref, rows, C_UB, SGU_W), _cols(hm_ref, rows, C_VB, SGU_W),
                                         vng_ref[...], vnb_ref[...], wt, bsb_ref)
            g_b = _cols(hm_ref, rows, C_GB, SGU_W)
            yb_ref[rows, :] = (u * mixed * (g_b * _sigmoid(g_b))).astype(BF16)

    return pl.pallas_call(
        body, name=name, grid=(nt,),
        in_specs=_mixer_in_specs(nt, False) + [
            pl.BlockSpec((1, SGU_W), _CONST2), pl.BlockSpec((1, SGU_W), _CONST2),
            pl.BlockSpec((4, BLK, BLK), _CONST3), pl.BlockSpec((4, BLK, BLK), _CONST3)],
        out_specs=[pl.BlockSpec((TM, Q_W), lambda i: (i, 0)), pl.BlockSpec((TM, SGU_W), lambda i: (i, 0))],
        out_shape=[jax.ShapeDtypeStruct((t, Q_W), BF16), jax.ShapeDtypeStruct((t, SGU_W), BF16)],
        compiler_params=_ARB,
    )(sinks, hm, hm, bias, vn_g, vn_b, w_s, bsb)


def _tail_fwd(x, ya, yb, hr, pa_w, pb_w, wo, b_out, ln_g, ln_b, name, last):
    t = x.shape[0]

    def body(x_ref, ya_ref, yb_ref, hr_ref, paw_ref, pbw_ref, wo_ref, bo_ref, g_ref, b_ref,
             pa_ref, pb_ref, mg_ref, z_ref, *xn_ref):
        pa = _dot(ya_ref[...], paw_ref[...])
        pb = _dot(yb_ref[...], pbw_ref[...])
        pa_ref[...] = pa.astype(BF16)
        pb_ref[...] = pb.astype(BF16)
        everything = slice(None)
        merged = _sigmoid(_cols(hr_ref, everything, 0, D)) * pa + _sigmoid(_cols(hr_ref, everything, D, D)) * pb
        mb = merged.astype(BF16)
        mg_ref[...] = mb
        z = ALPHA * x_ref[...] + (_dot(mb, wo_ref[...]) + bo_ref[...])
        z_ref[...] = z
        if not last:
            zhat, _ = _ln_stats(z)
            xn_ref[0][...] = zhat * g_ref[...] + b_ref[...]

    row = lambda w: pl.BlockSpec((TM, w), lambda i: (i, 0))
    vec = pl.BlockSpec((1, D), _CONST2)
    n_f32 = 1 if last else 2
    return pl.pallas_call(
        body, name=name, grid=(t // TM,),
        in_specs=[row(D), row(Q_W), row(SGU_W), row(R_W),
                  pl.BlockSpec((Q_W, D), _CONST2), pl.BlockSpec((SGU_W, D), _CONST2), pl.BlockSpec((D, D), _CONST2),
                  vec, vec, vec],
        out_specs=[row(D)] * (3 + n_f32),
        out_shape=[jax.ShapeDtypeStruct((t, D), BF16)] * 3 + [jax.ShapeDtypeStruct((t, D), F32)] * n_f32,
        compiler_params=_ARB,
    )(x, ya, yb, hr, pa_w, pb_w, wo, b_out, ln_g, ln_b)


def _tail_bwd(dxn, z, pa, pb, hr, wo, pa_w, pb_w, ln_g, ln_b, name, from_loss):
    t = dxn.shape[0]

    def body(dxn_ref, z_ref, pa_ref, pb_ref, hr_ref, wo_ref, paw_ref, pbw_ref, g_ref, b_ref,
             dz_ref, dpa_ref, dpb_ref, dhr_ref, dya_ref, dyb_ref, acc_ref, gbr_ref):
        @pl.when(pl.program_id(0) == 0)
        def _():
            acc_ref[...] = jnp.zeros_like(acc_ref)
            gbr_ref[...] = jnp.zeros_like(gbr_ref)

        zhat, rstd = _ln_stats(z_ref[...])
        if from_loss:
            err = zhat * g_ref[...] + b_ref[...] - dxn_ref[...]
            dxn_v = err * (1.0 / D)
            sq = jnp.sum(jnp.sum(err * err, axis=1, keepdims=True), axis=0, keepdims=True)
            acc_ref[3:4, :] += jnp.broadcast_to(sq, (1, D))
        else:
            dxn_v = dxn_ref[...]
        dz = _ln_bwd(dxn_v * g_ref[...], zhat, rstd)
        dz_ref[...] = dz
        acc_ref[0:1, :] += _colsum(dxn_v * zhat)
        acc_ref[1:2, :] += _colsum(dxn_v)
        acc_ref[2:3, :] += _colsum(dz)
        dmerged = _dot_nt(dz.astype(BF16), wo_ref[...])
        everything = slice(None)
        sa = _sigmoid(_cols(hr_ref, everything, 0, D))
        sb = _sigmoid(_cols(hr_ref, everything, D, D))
        dpa = (dmerged * sa).astype(BF16)
        dpb = (dmerged * sb).astype(BF16)
        dpa_ref[...] = dpa
        dpb_ref[...] = dpb
        dra = dmerged * pa_ref[...].astype(F32) * (sa * (1.0 - sa))
        drb = dmerged * pb_ref[...].astype(F32) * (sb * (1.0 - sb))
        dhr_ref[:, 0:D] = dra.astype(BF16)
        dhr_ref[:, D:2 * D] = drb.astype(BF16)
        gbr_ref[0:1, 0:D] += _colsum(dra)
        gbr_ref[0:1, D:2 * D] += _colsum(drb)
        dya_ref[...] = _dot_nt(dpa, paw_ref[...]).astype(BF16)
        dyb_ref[...] = _dot_nt(dpb, pbw_ref[...]).astype(BF16)

    row = lambda w: pl.BlockSpec((TM, w), lambda i: (i, 0))
    vec = pl.BlockSpec((1, D), _CONST2)
    return pl.pallas_call(
        body, name=name, grid=(t // TM,),
        in_specs=[row(D), row(D), row(D), row(D), row(R_W),
                  pl.BlockSpec((D, D), _CONST2), pl.BlockSpec((Q_W, D), _CONST2), pl.BlockSpec((SGU_W, D), _CONST2),
                  vec, vec],
        out_specs=[row(D), row(D), row(D), row(R_W), row(Q_W), row(SGU_W), pl.BlockSpec((8, D), _CONST2),
                   pl.BlockSpec((8, R_W), _CONST2)],
        out_shape=[jax.ShapeDtypeStruct((t, D), F32), jax.ShapeDtypeStruct((t, D), BF16),
                   jax.ShapeDtypeStruct((t, D), BF16), jax.ShapeDtypeStruct((t, R_W), BF16),
                   jax.ShapeDtypeStruct((t, Q_W), BF16), jax.ShapeDtypeStruct((t, SGU_W), BF16),
                   jax.ShapeDtypeStruct((8, D), F32), jax.ShapeDtypeStruct((8, R_W), F32)],
        compiler_params=_ARB,
    )(dxn, z, pa, pb, hr, wo, pa_w, pb_w, ln_g, ln_b)


def _mixer_bwd(hm, dya, dyb, sinks, bias, vn_g, vn_b, w_s, bsb, nblk_seq, name):
    t = hm.shape[0]
    nt = t // TM

    def body(sinks_ref, hm_ref, hprev_ref, bias_ref, dya_ref, dyb_ref, vng_ref, vnb_ref, ws_ref, bsb_ref,
             dhm_ref, gbm_ref, gsk_ref, gvn_ref, gws_ref, gbs_ref, dk_carry, dv_carry):
        gi = pl.program_id(0)
        i = nt - 1 - gi

        @pl.when(gi == 0)
        def _():
            for r in (gbm_ref, gsk_ref, gvn_ref, gws_ref, gbs_ref, dk_carry, dv_carry):
                r[...] = jnp.zeros_like(r)

        tril = _tril_mask()
        wt = [jnp.where(tril, ws_ref[g], 0.0).astype(BF16) for g in range(4)]
        vng = vng_ref[...]

        def put(rows, col, val):
            dhm_ref[rows, col:col + val.shape[1]] = val.astype(BF16)
            gbm_ref[0:1, col:col + val.shape[1]] += _colsum(val)

        for s in reversed(range(NB)):
            r0 = s * BLK
            rows = slice(r0, r0 + BLK)
            bias = bias_ref[jnp.where((i * NB + s) % nblk_seq == 0, 1, 0)]
            q = _cols(hm_ref, rows, C_Q, Q_W)
            kband = _band(hm_ref, hprev_ref, s, C_K)
            vband = _band(hm_ref, hprev_ref, s, C_V)
            g_a = _cols(hm_ref, rows, C_GA, Q_W)
            sg = _sigmoid(g_a)
            dya_v = _cols(dya_ref, rows, 0, Q_W)
            d_o = dya_v * (g_a * sg)
            o_pairs, dq_pairs = [None] * 4, [None] * 4
            dkband = jnp.zeros((2 * BLK, KV_W), F32)
            dvband = jnp.zeros((2 * BLK, KV_W), F32)
            kband_t, vband_t = kband.T, vband.T
            for hk in range(N_KV):
                q4, kh, prob_t, p_sink, o_t = _attn_group(q, kband, vband_t, hk, sinks_ref, bias)
                o_pairs = _unstack_heads(o_t.T, hk, o_pairs)
                parts = []
                for g in range(4):
                    _, p, pos = _head_place(hk, g)
                    dp = d_o[:, BLK * p:BLK * (p + 1)]
                    parts.append(pltpu.roll(dp, 64, 1) if pos != hk else dp)
                do4 = _head_lanes(jnp.concatenate(parts, axis=0), hk)
                do4b = do4.astype(BF16)
                delta = _colsum(do4.T * o_t)
                vh = _head_lanes(vband, hk).astype(BF16)
                ds_t = prob_t * (_dot_nt(vh, do4b) - delta)
                dsb = ds_t.astype(BF16)
                dq4_t = _dot(_head_rows(kband_t, hk), dsb)
                dq_pairs = _unstack_heads(dq4_t.T * ATTN_SCALE, hk, dq_pairs)
                dkband = dkband + _head_lanes(_dot(dsb, q4), hk)
                dvband = dvband + _dot(prob_t.astype(BF16), do4b)
                dsk = p_sink * delta
                for g in range(4):
                    j = 4 * hk + g
                    tot = jnp.sum(dsk[:, BLK * g:BLK * (g + 1)], axis=1, keepdims=True)
                    gsk_ref[j:j + 1, :] += jnp.broadcast_to(-tot, (1, 128))
            attn = jnp.concatenate(o_pairs, axis=1)
            put(rows, C_Q, jnp.concatenate(dq_pairs, axis=1))
            put(rows, C_K, dkband[BLK:2 * BLK] + dk_carry[...])
            put(rows, C_V, dvband[BLK:2 * BLK] + dv_carry[...])
            dk_carry[...] = dkband[0:BLK]
            dv_carry[...] = dvband[0:BLK]
            put(rows, C_GA, dya_v * attn * (sg * (1.0 + g_a * (1.0 - sg))))
            u_b = _cols(hm_ref, rows, C_UB, SGU_W)
            v_b = _cols(hm_ref, rows, C_VB, SGU_W)
            g_b = _cols(hm_ref, rows, C_GB, SGU_W)
            u, vhat, rstd, vn, mixed = _sgu_fwd(u_b, v_b, vng, vnb_ref[...], wt, bsb_ref)
            sgb = _sigmoid(g_b)
            silu_b = g_b * sgb
            dyb_v = _cols(dyb_ref, rows, 0, SGU_W)
            du = dyb_v * mixed * silu_b
            dmixed = dyb_v * u * silu_b
            put(rows, C_GB, dyb_v * u * mixed * (sgb * (1.0 + g_b * (1.0 - sgb))))
            dvn_parts = []
            for g in range(4):
                cols = slice(BLK * g, BLK * (g + 1))
                dmg = dmixed[:, cols]
                dmgb = dmg.astype(BF16)
                dvn_parts.append(_dot_tn(wt[g], dmgb))
                gws_ref[g] += jnp.where(tril, _dot_nt(dmgb, vn[:, cols].astype(BF16)), 0.0)
                gbs_ref[g] += dmg
            dvn = jnp.concatenate(dvn_parts, axis=1)
            gvn_ref[0:1, :] += _colsum(dvn * vhat)
            gvn_ref[1:2, :] += _colsum(dvn)
            dv = _ln_bwd(dvn * vng, vhat, rstd)
            put(rows, C_UB, du * _dgelu(u_b))
            put(rows, C_VB, dv * _dgelu(v_b))

        @pl.when(gi == nt - 1)
        def _():
            for g in range(4):
                gbs_ref[g] = jnp.broadcast_to(jnp.sum(gbs_ref[g], axis=1, keepdims=True), (BLK, BLK))

    row = lambda w: pl.BlockSpec((TM, w), lambda g: (nt - 1 - g, 0))
    return pl.pallas_call(
        body, name=name, grid=(nt,),
        in_specs=_mixer_in_specs(nt, True) + [
            row(Q_W), row(SGU_W),
            pl.BlockSpec((1, SGU_W), _CONST2), pl.BlockSpec((1, SGU_W), _CONST2),
            pl.BlockSpec((4, BLK, BLK), _CONST3), pl.BlockSpec((4, BLK, BLK), _CONST3)],
        out_specs=[row(MAIN_W), pl.BlockSpec((8, MAIN_W), _CONST2), pl.BlockSpec((8, 128), _CONST2),
                   pl.BlockSpec((8, SGU_W), _CONST2), pl.BlockSpec((4, BLK, BLK), _CONST3),
                   pl.BlockSpec((4, BLK, BLK), _CONST3)],
        out_shape=[jax.ShapeDtypeStruct((t, MAIN_W), BF16), jax.ShapeDtypeStruct((8, MAIN_W), F32),
                   jax.ShapeDtypeStruct((8, 128), F32), jax.ShapeDtypeStruct((8, SGU_W), F32),
                   jax.ShapeDtypeStruct((4, BLK, BLK), F32), jax.ShapeDtypeStruct((4, BLK, BLK), F32)],
        scratch_shapes=[pltpu.VMEM((BLK, KV_W), F32), pltpu.VMEM((BLK, KV_W), F32)],
        compiler_params=_ARB,
    )(sinks, hm, hm, bias, dya, dyb, vn_g, vn_b, w_s, bsb)


def _dx_inproj(dz, dhm, dhr, w, after, name):
    t = dz.shape[0]

    def body(dz_ref, dhm_ref, dhr_ref, w_ref, after_ref, dx_ref):
        dx_ref[...] = (ALPHA * dz_ref[...] + after_ref[0:1, 0:1] + _dot_nt(dhm_ref[...], w_ref[:, 0:MAIN_W])
                       + _dot_nt(dhr_ref[...], w_ref[:, MAIN_W:IN_COLS]))

    row = lambda w: pl.BlockSpec((TM, w), lambda i: (i, 0))
    return pl.pallas_call(
        body, name=name, grid=(t // TM,),
        in_specs=[row(D), row(MAIN_W), row(R_W), pl.BlockSpec((D, IN_COLS), _CONST2),
                  pl.BlockSpec((8, 128), _CONST2)],
        out_specs=row(D), out_shape=jax.ShapeDtypeStruct((t, D), F32), compiler_params=_ARB,
    )(dz, dhm, dhr, w, after)


def _wgrad(a, b, tn, name, tk=1024):
    t, m = a.shape
    n = b.shape[1]
    nk = t // tk

    def body(a_ref, b_ref, o_ref):
        @pl.when(pl.program_id(1) == 0)
        def _():
            o_ref[...] = jnp.zeros_like(o_ref)

        o_ref[...] += _dot_tn(a_ref[...].astype(BF16), b_ref[...].astype(BF16))

    return pl.pallas_call(
        body, name=name, grid=(n // tn, nk),
        in_specs=[pl.BlockSpec((tk, m), lambda j, k: (k, 0)), pl.BlockSpec((tk, tn), lambda j, k: (k, j))],
        out_specs=pl.BlockSpec((m, tn), lambda j, k: (0, j)),
        out_shape=jax.ShapeDtypeStruct((m, n), F32),
        compiler_params=pltpu.CompilerParams(dimension_semantics=("arbitrary", "arbitrary"), vmem_limit_bytes=VMEM_LIMIT),
    )(a, b)


def _ln_in_bwd(dx0, x, g, name):
    t = x.shape[0]

    def body(dx0_ref, x_ref, g_ref, gx_ref, acc_ref):
        @pl.when(pl.program_id(0) == 0)
        def _():
            acc_ref[...] = jnp.zeros_like(acc_ref)

        d = dx0_ref[...]
        xhat, rstd = _ln_stats(x_ref[...])
        gx_ref[...] = _ln_bwd(d * g_ref[...], xhat, rstd)
        acc_ref[0:1, :] += _colsum(d * xhat)
        acc_ref[1:2, :] += _colsum(d)

    row = pl.BlockSpec((TM_EW, D), lambda i: (i, 0))
    return pl.pallas_call(
        body, name=name, grid=(t // TM_EW,),
        in_specs=[row, row, pl.BlockSpec((1, D), _CONST2)],
        out_specs=[row, pl.BlockSpec((8, D), _CONST2)],
        out_shape=[jax.ShapeDtypeStruct((t, D), F32), jax.ShapeDtypeStruct((8, D), F32)],
        compiler_params=_ARB,
    )(dx0, x, g)


_ANY = pl.BlockSpec(memory_space=pl.ANY)


def _place():
    return lax.axis_index("x"), lax.axis_index("y"), lax.axis_index("c")


def _allgather8(xs, name):
    n = len(xs)

    def body(*refs):
        x_refs, o_refs = refs[:n], refs[n:2 * n]
        send_sems, recv_sems, local_sems = refs[2 * n:]
        x, y, c = _place()
        me, sib = (x, y, c), (x, y, 1 - c)
        chips = [(1 - x, y), (x, 1 - y), (1 - x, 1 - y)]

        def copy(a, k, block, to, src=None):
            dst = o_refs[a].at[4 * block[0] + 2 * block[1] + block[2]]
            return pltpu.make_async_remote_copy(
                src_ref=dst if src is None else src, dst_ref=dst, send_sem=send_sems.at[7 * a + k],
                recv_sem=recv_sems.at[7 * a + k], device_id=to, device_id_type=MESH)

        mine = [pltpu.make_async_copy(x_refs[a], o_refs[a].at[4 * x + 2 * y + c], local_sems.at[a]) for a in range(n)]
        for cp in mine:
            cp.start()
        sent = []
        for a in range(n):
            sent.append(copy(a, 0, me, sib, src=x_refs[a]))
            sent += [copy(a, 1 + j, me, (*chip, c), src=x_refs[a]) for j, chip in enumerate(chips)]
        for cp in sent:
            cp.start()
        for j, chip in enumerate(chips):
            for a in range(n):
                copy(a, 1 + j, (*chip, c), me).wait_recv()
                fwd = copy(a, 4 + j, (*chip, c), sib)
                fwd.start()
                sent.append(fwd)
        for a in range(n):
            copy(a, 0, sib, me).wait_recv()
            for j, chip in enumerate(chips):
                copy(a, 4 + j, (*chip, 1 - c), me).wait_recv()
        for cp in sent:
            cp.wait_send()
        for cp in mine:
            cp.wait()

    return pl.pallas_call(
        body, name=name, in_specs=[_ANY] * n, out_specs=[_ANY] * n,
        out_shape=[jax.ShapeDtypeStruct((N_DEV,) + v.shape, v.dtype) for v in xs],
        scratch_shapes=[pltpu.SemaphoreType.DMA((7 * n,)), pltpu.SemaphoreType.DMA((7 * n,)),
                        pltpu.SemaphoreType.DMA((n,))],
    )(*xs)


def _swap_sibling(gs, name):
    n = len(gs)

    def body(*refs):
        g_refs, r_refs = refs[:n], refs[n:2 * n]
        send_sems, recv_sems = refs[2 * n:]
        x, y, c = _place()
        cps = [pltpu.make_async_remote_copy(
            src_ref=g_refs[a].at[1 - c], dst_ref=r_refs[a], send_sem=send_sems.at[a], recv_sem=recv_sems.at[a],
            device_id=(x, y, 1 - c), device_id_type=MESH) for a in range(n)]
        for cp in cps:
            cp.start()
        for cp in cps:
            cp.wait()

    return pl.pallas_call(
        body, name=name, in_specs=[_ANY] * n, out_specs=[_ANY] * n,
        out_shape=[jax.ShapeDtypeStruct(v.shape[1:], v.dtype) for v in gs],
        scratch_shapes=[pltpu.SemaphoreType.DMA((n,)), pltpu.SemaphoreType.DMA((n,))],
    )(*gs)


def _row_tile(rows, cap):
    return max(d for d in range(16, cap + 1, 16) if rows % d == 0)


def _pair_sum(g, r, name):
    _, n, rows, l = g.shape
    tr = _row_tile(rows, 512)

    def body(c_ref, g_ref, r_ref, o_ref):
        o_ref[...] = (g_ref[0].astype(F32) + r_ref[...].astype(F32)).astype(o_ref.dtype)

    return pl.pallas_call(
        body, name=name,
        grid_spec=pltpu.PrefetchScalarGridSpec(
            num_scalar_prefetch=1, grid=(n, rows // tr),
            in_specs=[pl.BlockSpec((1, 1, tr, l), lambda q, i, c_ref: (c_ref[0], q, i, 0)),
                      pl.BlockSpec((1, tr, l), lambda q, i, c_ref: (q, i, 0))],
            out_specs=pl.BlockSpec((1, tr, l), lambda q, i, c_ref: (q, i, 0))),
        out_shape=jax.ShapeDtypeStruct((n, rows, l), g.dtype),
        compiler_params=pltpu.CompilerParams(dimension_semantics=("arbitrary", "arbitrary"), vmem_limit_bytes=VMEM_LIMIT),
    )(lax.axis_index("c").astype(jnp.int32).reshape(1), g, r)


def _adamw(parts, w, m, v, name, own=None):
    ns, rows, l = parts.shape
    tr = rows if rows <= 256 else _row_tile(rows, 256)
    c1 = 1.0 - ADAM_B1 ** ADAM_STEP
    c2 = 1.0 - ADAM_B2 ** ADAM_STEP

    def body(q_ref, *refs):
        own_ref = refs[0] if own is not None else None
        p_ref, w_ref, m_ref, v_ref, g_ref, d_ref, nm_ref, nv_ref = refs[-8:]
        g = None
        for k in range(ns):
            term = p_ref[k].astype(F32)
            if own_ref is not None:
                term = jnp.where(q_ref[0] == k, own_ref[0].astype(F32), term)
            g = term if g is None else g + term
        g_ref[...] = g
        nm = ADAM_B1 * m_ref[...] + (1.0 - ADAM_B1) * g
        nv = ADAM_B2 * v_ref[...] + (1.0 - ADAM_B2) * (g * g)
        nm_ref[...] = nm
        nv_ref[...] = nv
        d_ref[...] = -ADAM_LR * ((nm / c1) / (jnp.sqrt(nv / c2) + ADAM_EPS) + ADAM_WD * w_ref[...])

    row = pl.BlockSpec((tr, l), lambda i, q: (i, 0))
    own_specs = [] if own is None else [pl.BlockSpec((1, tr, l), lambda i, q: (q[0], i, 0))]
    chip = (2 * lax.axis_index("x") + lax.axis_index("y")).astype(jnp.int32).reshape(1)
    return pl.pallas_call(
        body, name=name,
        grid_spec=pltpu.PrefetchScalarGridSpec(
            num_scalar_prefetch=1, grid=(rows // tr,),
            in_specs=own_specs + [pl.BlockSpec((ns, tr, l), lambda i, q: (0, i, 0)), row, row, row],
            out_specs=[row] * 4),
        out_shape=[jax.ShapeDtypeStruct((rows, l), F32)] * 4, compiler_params=_ARB,
    )(chip, *([] if own is None else [own]), parts, w, m, v)


_HBM = pl.BlockSpec(memory_space=pltpu.HBM)
_SEM = pl.BlockSpec(memory_space=pltpu.SEMAPHORE)
_EFFECT = pltpu.SideEffectType.DATAFLOW_SIDE_EFFECTING


def _plan_all(x, y, c):
    me = 4 * x + 2 * y + c
    peers = [(x, y, 1 - c), (1 - x, y, c), (x, 1 - y, c), (1 - x, 1 - y, c),
             (1 - x, y, 1 - c), (x, 1 - y, 1 - c), (1 - x, 1 - y, 1 - c)]
    return [(None, me, p, 4 * p[0] + 2 * p[1] + p[2]) for p in peers]


def _plan_chips(x, y, c):
    me = 2 * x + y
    return [(2 * qx + qy, me, (qx, qy, c), 2 * qx + qy) for qx, qy in ((1 - x, y), (x, 1 - y), (1 - x, 1 - y))]


def _split_copies(plan, src_refs, land_refs, send_sems, recv_sems, arrival):
    n = len(src_refs)
    entries = plan(*_place())
    per = len(entries)
    cps = []
    for a in range(n):
        for k, (src_slot, dst_slot, peer, back_slot) in enumerate(entries):
            src = src_refs[a] if src_slot is None else src_refs[a].at[src_slot]
            cps.append(pltpu.make_async_remote_copy(
                src_ref=src, dst_ref=land_refs[a].at[back_slot if arrival else dst_slot],
                send_sem=send_sems.at[per * a + k], recv_sem=recv_sems.at[per * a + k],
                device_id=peer, device_id_type=MESH))
    return cps


def _split_start(srcs, lands, plan, per, name):
    n = len(srcs)

    def body(*refs):
        for cp in _split_copies(plan, refs[:n], refs[n:2 * n], refs[2 * n], refs[2 * n + 1], False):
            cp.start()
        refs[-1][...] = jnp.zeros_like(refs[-1])

    both = list(srcs) + list(lands)
    outs = pl.pallas_call(
        body, name=name,
        out_shape=(pltpu.SemaphoreType.DMA((per * n,)), pltpu.SemaphoreType.DMA((per * n,)),
                   *[pltpu.HBM(v.shape, v.dtype) for v in both], jax.ShapeDtypeStruct((8, 128), F32)),
        in_specs=[_HBM] * (2 * n),
        out_specs=(_SEM, _SEM, *[_HBM] * (2 * n), pl.BlockSpec(memory_space=pltpu.VMEM)),
        input_output_aliases={i: 2 + i for i in range(2 * n)},
        compiler_params=pltpu.CompilerParams(has_side_effects=_EFFECT),
    )(*[pltpu.with_memory_space_constraint(v, pltpu.HBM) for v in both])
    return outs[0], outs[1], list(outs[2:2 + 2 * n]), outs[-1]


def _split_wait(send_sems, recv_sems, thru, plan, after, name):
    n = len(thru) // 2

    def body(*refs):
        for cp in _split_copies(plan, refs[:n], refs[n:2 * n], refs[2 * n], refs[2 * n + 1], True):
            cp.wait_send()
            cp.wait_recv()

    outs = pl.pallas_call(
        body, name=name, out_shape=tuple(pltpu.HBM(v.shape, v.dtype) for v in thru),
        in_specs=[_HBM] * (2 * n) + [_SEM, _SEM, pl.BlockSpec(memory_space=pl.ANY)],
        out_specs=[_HBM] * (2 * n), input_output_aliases={i: i for i in range(2 * n)},
        compiler_params=pltpu.CompilerParams(has_side_effects=_EFFECT),
    )(*thru, send_sems, recv_sems, after)
    return list(outs[:n]), list(outs[n:])


_SMALL_IN = ("ln_in_g", "ln_in_b")
_SMALL = ("b_in", "sinks", "vn_g", "vn_b", "w_s", "b_s", "b_out", "ln_g", "ln_b")


def _pack_small(d, names):
    out = []
    for n in names:
        f = d[n].reshape(-1)
        out.append(jnp.pad(f, (0, (-f.shape[0]) % 1024)))
    return jnp.concatenate(out).reshape(-1, 128)


def _unpack_small(p, like, names):
    flat, off, out = p.reshape(-1), 0, {}
    for n in names:
        size = like[n].size
        out[n] = flat[off:off + size].reshape(like[n].shape)
        off += size + (-size) % 1024
    return out


def _owner_major(g, axis):
    sh = g.shape
    g = g.reshape(sh[:axis] + (4, 2, sh[axis] // N_DEV) + sh[axis + 1:])
    g = jnp.moveaxis(g, (axis, axis + 1), (1, 0))
    return g


def kernel(x, ln_in_g, ln_in_b, w_in, b_in, sinks, vn_g, vn_b, w_s, b_s, p_a, p_b, w_out, b_out, ln_g, ln_b, loss_target, m_ln_in_g, m_ln_in_b, m_w_in, m_b_in, m_sinks, m_vn_g, m_vn_b, m_w_s, m_b_s, m_p_a, m_p_b, m_w_out, m_b_out, m_ln_g, m_ln_b, v_ln_in_g, v_ln_in_b, v_w_in, v_b_in, v_sinks, v_vn_g, v_vn_b, v_w_s, v_b_s, v_p_a, v_p_b, v_w_out, v_b_out, v_ln_g, v_ln_b):
    nseq, seq, _ = x.shape
    t = nseq * seq
    nblk_seq = seq // BLK
    x2 = x.reshape(t, D)
    tgt = loss_target.reshape(t, D)

    def blocks(l):
        return [w_in[l].astype(BF16), p_a[l].astype(BF16), p_b[l].astype(BF16), w_out[l].astype(BF16)]

    def full_weights(g):
        w_full = jnp.moveaxis(g[0], 0, 1).reshape(D, IN_COLS)
        pa_full = jnp.moveaxis(g[1], 0, 1).reshape(Q_W, D)
        pb_full = jnp.moveaxis(g[2], 0, 1).reshape(SGU_W, D)
        wo_full = g[3].reshape(D, D)
        return dict(w=w_full, pa=pa_full, pb=pb_full, wo=wo_full)

    me = 4 * lax.axis_index("x") + 2 * lax.axis_index("y") + lax.axis_index("c")
    gathered0 = _allgather8(blocks(0), "allgather_weights0")
    blocks1, gathered0 = lax.optimization_barrier((blocks(1), gathered0))
    lands1 = [lax.empty((N_DEV,) + v.shape, v.dtype) for v in blocks1]
    ag_send, ag_recv, ag_thru, ag_token = _split_start(blocks1, lands1, _plan_all, 7, "allgather_weights1_start")
    weights = [full_weights(gathered0), None]
    bsb = jnp.broadcast_to(b_s[:, :, :, None], (DEPTH, 4, BLK, BLK))
    bias = _band_bias()

    xs = [_ln_fwd(x2, ln_in_g + ag_token[0, 0], ln_in_b, "ln_in_fwd")]
    saved = []
    for l in range(DEPTH):
        if l == 1:
            sent, landed = _split_wait(ag_send, ag_recv, ag_thru, _plan_all, xs[1], "allgather_weights1_wait")
            weights[1] = full_weights(
                [lax.dynamic_update_index_in_dim(g, b, me, 0) for g, b in zip(landed, sent)])
        wl = weights[l]
        last = l == DEPTH - 1
        hm, hr = _inproj(xs[l], wl["w"], b_in[l].reshape(1, -1), f"inproj{l}")
        ya, yb = _mixer_fwd(hm, sinks[l], bias, vn_g[l].reshape(1, -1), vn_b[l].reshape(1, -1), w_s[l], bsb[l],
                            nblk_seq, f"mixer_fwd{l}")
        outs = _tail_fwd(xs[l], ya, yb, hr, wl["pa"], wl["pb"], wl["wo"], b_out[l].reshape(1, D),
                         ln_g[l].reshape(1, D), ln_b[l].reshape(1, D), f"tail_fwd{l}", last)
        saved.append((hm, hr, ya, yb) + tuple(outs[:4]))
        if not last:
            xs.append(outs[4])

    small = {n: [None] * DEPTH for n in _SMALL}
    names = ("w_in", "p_a", "p_b", "w_out")
    owner_axis = {"w_in": 1, "p_a": 1, "p_b": 1, "w_out": 0}
    token = jnp.zeros((8, 128), F32)
    dx = tgt
    split = [None] * DEPTH
    for l in reversed(range(DEPTH)):
        hm, hr, ya, yb, pa, pb, merged, z = saved[l]
        wl = weights[l]
        dz, dpa, dpb, dhr, dya, dyb, acc, gbr = _tail_bwd(
            dx, z, pa, pb, hr, wl["wo"], wl["pa"], wl["pb"], ln_g[l].reshape(1, D) + token[0, 0],
            ln_b[l].reshape(1, D), f"tail_bwd{l}", l == DEPTH - 1)
        if l == DEPTH - 1:
            loss = lax.psum(acc[3, 0] * (0.5 / D), ("x", "y", "c"))
        dhm, gbm, gsk, gvn, gws, gbs = _mixer_bwd(
            hm, dya, dyb, sinks[l], bias, vn_g[l].reshape(1, -1), vn_b[l].reshape(1, -1), w_s[l], bsb[l],
            nblk_seq, f"mixer_bwd{l}")
        grads = {"w_in": jnp.concatenate([_wgrad(xs[l], dhm, MAIN_W // 2, f"wgrad_in_main{l}"),
                                          _wgrad(xs[l], dhr, R_W // 2, f"wgrad_in_route{l}")], axis=1),
                 "p_a": _wgrad(ya, dpa, D, f"wgrad_pa{l}"), "p_b": _wgrad(yb, dpb, D, f"wgrad_pb{l}"),
                 "w_out": _wgrad(merged, dz, D, f"wgrad_out{l}")}
        small["b_in"][l] = jnp.concatenate([gbm[0], gbr[0]])
        small["sinks"][l] = gsk[:, 0]
        small["vn_g"][l], small["vn_b"][l] = gvn[0], gvn[1]
        small["w_s"][l], small["b_s"][l] = gws, gbs[:, :, 0]
        small["ln_g"][l], small["ln_b"][l], small["b_out"][l] = acc[0], acc[1], acc[2]
        parts = [_owner_major(grads[n], owner_axis[n]).astype(BF16) for n in names]
        if l == 0:
            packed = _pack_small({n: jnp.stack(v) for n, v in small.items()}, _SMALL)
            parts.append(jnp.broadcast_to(packed[None, None], (2, 1) + packed.shape))
        from_sib = _swap_sibling(parts, f"rs_sibling{l}")
        pair = [_pair_sum(g, r, f"pair_sum{l}_{a}") for a, (g, r) in enumerate(zip(parts, from_sib))]
        if l == 0:
            pair[4] = jnp.broadcast_to(pair[4], (4,) + packed.shape)
        lands = [jnp.zeros(p.shape, p.dtype) for p in pair]
        split[l] = _split_start(pair, lands, _plan_chips, 3, f"rs_chips{l}_start")
        token = split[l][3]
        dx = _dx_inproj(dz, dhm, dhr, wl["w"], token, f"dx_inproj{l}")
    grad_x, acc_in = _ln_in_bwd(dx, x2, ln_in_g.reshape(1, D), "ln_in_bwd")
    (all_in,) = _allgather8([acc_in], "allgather_ln_in")

    given = {"w_in": (w_in, m_w_in, v_w_in), "p_a": (p_a, m_p_a, v_p_a), "p_b": (p_b, m_p_b, v_p_b),
             "w_out": (w_out, m_w_out, v_w_out)}
    res = {n: [] for n in names}
    small_parts = None
    for l in range(DEPTH):
        pair, by_chip = _split_wait(split[l][0], split[l][1], split[l][2], _plan_chips, all_in, f"rs_chips{l}_wait")
        for a, n in enumerate(names):
            rows, lanes = by_chip[a].shape[1:]
            res[n].append(_adamw(by_chip[a], *[v[l].reshape(rows, lanes) for v in given[n]], f"adamw{l}_{n}",
                                 own=pair[a]))
        if l == 0:
            small_parts = (by_chip[4], pair[4])
    res = {n: [jnp.stack(o).reshape(given[n][0].shape) for o in zip(*res[n])] for n in names}

    w_small = dict(ln_in_g=ln_in_g, ln_in_b=ln_in_b, b_in=b_in, sinks=sinks, vn_g=vn_g, vn_b=vn_b, w_s=w_s, b_s=b_s,
                   b_out=b_out, ln_g=ln_g, ln_b=ln_b)
    m_small = dict(ln_in_g=m_ln_in_g, ln_in_b=m_ln_in_b, b_in=m_b_in, sinks=m_sinks, vn_g=m_vn_g, vn_b=m_vn_b,
                   w_s=m_w_s, b_s=m_b_s, b_out=m_b_out, ln_g=m_ln_g, ln_b=m_ln_b)
    v_small = dict(ln_in_g=v_ln_in_g, ln_in_b=v_ln_in_b, b_in=v_b_in, sinks=v_sinks, vn_g=v_vn_g, vn_b=v_vn_b,
                   w_s=v_w_s, b_s=v_b_s, b_out=v_b_out, ln_g=v_ln_g, ln_b=v_ln_b)
    outs = _adamw(small_parts[0], *[_pack_small(d, _SMALL) for d in (w_small, m_small, v_small)], "adamw_small",
                  own=small_parts[1])
    outs_in = _adamw(all_in, *[jnp.pad(jnp.stack([d[n] for n in _SMALL_IN]), ((0, 6), (0, 0)))
                               for d in (w_small, m_small, v_small)], "adamw_ln_in")
    for k in range(4):
        u = _unpack_small(outs[k], w_small, _SMALL)
        u.update({n: outs_in[k][r] for r, n in enumerate(_SMALL_IN)})
        for n in u:
            res.setdefault(n, [None] * 4)[k] = u[n]

    order = ("ln_in_g", "ln_in_b", "w_in", "b_in", "sinks", "vn_g", "vn_b", "w_s", "b_s", "p_a", "p_b", "w_out",
             "b_out", "ln_g", "ln_b")
    return (loss, grad_x.reshape(x.shape), *[res[n][0] for n in order], *[res[n][1] for n in order],
            *[res[n][2] for n in order], *[res[n][3] for n in order])
```

```python
import jax
import jax.numpy as jnp
from jax import lax
from jax.experimental import pallas as pl
from jax.experimental.pallas import tpu as pltpu

F32 = jnp.float32
BF16 = jnp.bfloat16

D = 1024
BLK = 128
N_KV = 2
Q_W, KV_W, SGU_W = 512, 128, 512
C_Q, C_K, C_V, C_GA, C_UB, C_VB, C_GB = 0, 512, 640, 768, 1280, 1792, 2304
MAIN_W = 2816
R_W = 2048
IN_COLS = MAIN_W + R_W
N_DEV = 8
SHARD_COLS = IN_COLS // N_DEV

DEPTH = 2
ALPHA = (2.0 * DEPTH) ** 0.25
LN_EPS = 1e-5
ATTN_SCALE = 0.125
NEG = float(jnp.finfo(jnp.float32).min)

ADAM_LR, ADAM_B1, ADAM_B2, ADAM_EPS, ADAM_WD, ADAM_STEP = 0.001, 0.9, 0.999, 1e-08, 0.01, 10

TM = 256
TM_EW = 512
NB = TM // BLK
MESH = pl.DeviceIdType.MESH
VMEM_LIMIT = 56 * 1024 * 1024

_ARB = pltpu.CompilerParams(dimension_semantics=("arbitrary",), vmem_limit_bytes=VMEM_LIMIT)


def _sigmoid(x):
    return 1.0 / (1.0 + jnp.exp(-x))


_GELU_C = 0.7978845608028654
_GELU_A = 0.044715


def _gelu(x):
    return 0.5 * x * (1.0 + jnp.tanh(_GELU_C * (x + _GELU_A * x * x * x)))


def _dgelu(x):
    t = jnp.tanh(_GELU_C * (x + _GELU_A * x * x * x))
    return 0.5 * (1.0 + t) + 0.5 * x * (1.0 - t * t) * (_GELU_C * (1.0 + 3.0 * _GELU_A * x * x))


def _ln_stats(x):
    mu = jnp.mean(x, axis=-1, keepdims=True)
    xc = x - mu
    var = jnp.mean(xc * xc, axis=-1, keepdims=True)
    rstd = lax.rsqrt(var + LN_EPS)
    return xc * rstd, rstd


def _ln_bwd(dy_g, xhat, rstd):
    m1 = jnp.mean(dy_g, axis=-1, keepdims=True)
    m2 = jnp.mean(dy_g * xhat, axis=-1, keepdims=True)
    return rstd * (dy_g - m1 - xhat * m2)


def _colsum(x):
    return jnp.sum(x, axis=0, keepdims=True)


def _dot(a, b):
    return jnp.dot(a, b, preferred_element_type=F32)


def _dot_nt(a, b):
    return lax.dot_general(a, b, (((1,), (1,)), ((), ())), preferred_element_type=F32)


def _dot_tn(a, b):
    return lax.dot_general(a, b, (((0,), (0,)), ((), ())), preferred_element_type=F32)


def _head_place(hk, g):
    j = 4 * hk + g
    return j, j // 2, j % 2


def _head_rows(x, hk):
    d = lax.broadcasted_iota(jnp.int32, x.shape, 0)
    return jnp.where((d >= 64 * hk) & (d < 64 * hk + 64), x, 0.0).astype(BF16)


def _head_lanes(x, hk):
    d = lax.broadcasted_iota(jnp.int32, x.shape, 1)
    return jnp.where((d >= 64 * hk) & (d < 64 * hk + 64), x, 0.0)


def _band_bias():
    kpos = lax.broadcasted_iota(jnp.int32, (2 * BLK, 4 * BLK), 0)
    row = lax.broadcasted_iota(jnp.int32, (2 * BLK, 4 * BLK), 1) & (BLK - 1)
    band = (kpos > row) & (kpos <= row + BLK)
    return jnp.stack([jnp.where(band, 0.0, NEG), jnp.where(band & (kpos >= BLK), 0.0, NEG)]).astype(F32)


def _attn_group(q, kband, vband_t, hk, sinks_ref, bias):
    kh = _head_lanes(kband, hk).astype(BF16)
    parts = []
    for g in range(4):
        _, p, pos = _head_place(hk, g)
        qp = q[:, BLK * p:BLK * (p + 1)] * ATTN_SCALE
        if pos != hk:
            qp = pltpu.roll(qp, 64, 1)
        parts.append(qp.astype(BF16))
    q4 = jnp.concatenate(parts, axis=0)
    s_t = _dot_nt(kh, q4) + bias
    sink_row = jnp.concatenate(
        [jnp.full((1, BLK), sinks_ref[4 * hk + g], F32) for g in range(4)], axis=1)
    m = jnp.maximum(jnp.max(s_t, axis=0, keepdims=True), sink_row)
    p_un = jnp.exp(s_t - m)
    e_sink = jnp.exp(sink_row - m)
    inv = 1.0 / (jnp.sum(p_un, axis=0, keepdims=True) + e_sink)
    prob_t = p_un * inv
    o_t = _dot(_head_rows(vband_t, hk), prob_t.astype(BF16))
    return q4, kh, prob_t, e_sink * inv, o_t


def _unstack_heads(x4, hk, pairs):
    for g in range(4):
        _, p, pos = _head_place(hk, g)
        xg = x4[BLK * g:BLK * (g + 1)]
        if pos != hk:
            xg = pltpu.roll(xg, 64, 1)
        pairs[p] = xg if pairs[p] is None else pairs[p] + xg
    return pairs


def _attn_fwd(q, kband, vband, sinks_ref, bias):
    pairs = [None] * 4
    vband_t = vband.T
    for hk in range(N_KV):
        o_t = _attn_group(q, kband, vband_t, hk, sinks_ref, bias)[-1]
        pairs = _unstack_heads(o_t.T, hk, pairs)
    return jnp.concatenate(pairs, axis=1)


def _tril_mask():
    r = lax.broadcasted_iota(jnp.int32, (BLK, BLK), 0)
    c = lax.broadcasted_iota(jnp.int32, (BLK, BLK), 1)
    return c <= r


def _sgu_fwd(u_b, v_b, vn_g, vn_b, wt, bsb_ref):
    u = _gelu(u_b)
    v = _gelu(v_b)
    vhat, rstd = _ln_stats(v)
    vn = vhat * vn_g + vn_b
    mixed = jnp.concatenate(
        [_dot(wt[g], vn[:, BLK * g:BLK * (g + 1)].astype(BF16)) + bsb_ref[g] for g in range(4)], axis=1)
    return u, vhat, rstd, vn, mixed


def _cols(ref, rows, col, width):
    return ref[rows, col:col + width].astype(F32)


def _band(hm_ref, hprev_ref, s, col):
    r0 = s * BLK
    cur = hm_ref[r0:r0 + BLK, col:col + KV_W]
    if s == 0:
        off = 0 if col == C_K else KV_W
        prev = hprev_ref[:, off:off + KV_W]
    else:
        prev = hm_ref[r0 - BLK:r0, col:col + KV_W]
    return jnp.concatenate([prev, cur], axis=0).astype(F32)


def _mixer_in_specs(nt, rev):
    def tile(g):
        return nt - 1 - g if rev else g

    return [
        pl.BlockSpec(memory_space=pltpu.SMEM),
        pl.BlockSpec((TM, MAIN_W), lambda g: (tile(g), 0)),
        pl.BlockSpec((BLK, 2 * KV_W), lambda g: (jnp.maximum(tile(g) * NB - 1, 0), 2)),
        pl.BlockSpec((2, 2 * BLK, 4 * BLK), lambda g: (0, 0, 0)),
    ]


_CONST2 = lambda g: (0, 0)
_CONST3 = lambda g: (0, 0, 0)


def _ln_fwd(x, g, b, name):
    t = x.shape[0]

    def body(x_ref, g_ref, b_ref, o_ref):
        xhat, _ = _ln_stats(x_ref[...])
        o_ref[...] = xhat * g_ref[...] + b_ref[...]

    return pl.pallas_call(
        body, name=name, grid=(t // TM_EW,),
        in_specs=[pl.BlockSpec((TM_EW, D), lambda i: (i, 0)), pl.BlockSpec((1, D), _CONST2),
                  pl.BlockSpec((1, D), _CONST2)],
        out_specs=pl.BlockSpec((TM_EW, D), lambda i: (i, 0)),
        out_shape=jax.ShapeDtypeStruct((t, D), F32), compiler_params=_ARB,
    )(x, g.reshape(1, D), b.reshape(1, D))


def _inproj(x, w, b, name):
    t = x.shape[0]

    def body(x_ref, w_ref, b_ref, hm_ref, hr_ref):
        xb = x_ref[...].astype(BF16)
        hm_ref[...] = (_dot(xb, w_ref[:, 0:MAIN_W]) + b_ref[:, 0:MAIN_W]).astype(BF16)
        hr_ref[...] = (_dot(xb, w_ref[:, MAIN_W:IN_COLS]) + b_ref[:, MAIN_W:IN_COLS]).astype(BF16)

    return pl.pallas_call(
        body, name=name, grid=(t // TM,),
        in_specs=[pl.BlockSpec((TM, D), lambda i: (i, 0)),
                  pl.BlockSpec((D, IN_COLS), _CONST2), pl.BlockSpec((1, IN_COLS), _CONST2)],
        out_specs=[pl.BlockSpec((TM, MAIN_W), lambda i: (i, 0)), pl.BlockSpec((TM, R_W), lambda i: (i, 0))],
        out_shape=[jax.ShapeDtypeStruct((t, MAIN_W), BF16), jax.ShapeDtypeStruct((t, R_W), BF16)],
        compiler_params=_ARB,
    )(x, w, b)


def _mixer_fwd(hm, sinks, bias, vn_g, vn_b, w_s, bsb, nblk_seq, name):
    t = hm.shape[0]
    nt = t // TM

    def body(sinks_ref, hm_ref, hprev_ref, bias_ref, vng_ref, vnb_ref, ws_ref, bsb_ref, ya_ref, yb_ref):
        i = pl.program_id(0)
        tril = _tril_mask()
        wt = [jnp.where(tril, ws_ref[g], 0.0).astype(BF16) for g in range(4)]
        for s in range(NB):
            r0 = s * BLK
            rows = slice(r0, r0 + BLK)
            bias = bias_ref[jnp.where((i * NB + s) % nblk_seq == 0, 1, 0)]
            attn = _attn_fwd(_cols(hm_ref, rows, C_Q, Q_W), _band(hm_ref, hprev_ref, s, C_K),
                             _band(hm_ref, hprev_ref, s, C_V), sinks_ref, bias)
            g_a = _cols(hm_ref, rows, C_GA, Q_W)
            ya_ref[rows, :] = (attn * (g_a * _sigmoid(g_a))).astype(BF16)
            u, _, _, _, mixed = _sgu_fwd(_cols(hm_ref, rows, C_UB, SGU_W), _cols(hm_ref, rows, C_VB, SGU_W),
                                         vng_ref[...], vnb_ref[...], wt, bsb_ref)
            g_b = _cols(hm_ref, rows, C_GB, SGU_W)
            yb_ref[rows, :] = (u * mixed * (g_b * _sigmoid(g_b))).astype(BF16)

    return pl.pallas_call(
        body, name=name, grid=(nt,),
        in_specs=_mixer_in_specs(nt, False) + [
            pl.BlockSpec((1, SGU_W), _CONST2), pl.BlockSpec((1, SGU_W), _CONST2),
            pl.BlockSpec((4, BLK, BLK), _CONST3), pl.BlockSpec((4, BLK, BLK), _CONST3)],
        out_specs=[pl.BlockSpec((TM, Q_W), lambda i: (i, 0)), pl.BlockSpec((TM, SGU_W), lambda i: (i, 0))],
        out_shape=[jax.ShapeDtypeStruct((t, Q_W), BF16), jax.ShapeDtypeStruct((t, SGU_W), BF16)],
        compiler_params=_ARB,
    )(sinks, hm, hm, bias, vn_g, vn_b, w_s, bsb)


def _tail_fwd(x, ya, yb, hr, pa_w, pb_w, wo, b_out, ln_g, ln_b, name, last):
    t = x.shape[0]

    def body(x_ref, ya_ref, yb_ref, hr_ref, paw_ref, pbw_ref, wo_ref, bo_ref, g_ref, b_ref,
             pa_ref, pb_ref, mg_ref, z_ref, *xn_ref):
        pa = _dot(ya_ref[...], paw_ref[...])
        pb = _dot(yb_ref[...], pbw_ref[...])
        pa_ref[...] = pa.astype(BF16)
        pb_ref[...] = pb.astype(BF16)
        everything = slice(None)
        merged = _sigmoid(_cols(hr_ref, everything, 0, D)) * pa + _sigmoid(_cols(hr_ref, everything, D, D)) * pb
        mb = merged.astype(BF16)
        mg_ref[...] = mb
        z = ALPHA * x_ref[...] + (_dot(mb, wo_ref[...]) + bo_ref[...])
        z_ref[...] = z
        if not last:
            zhat, _ = _ln_stats(z)
            xn_ref[0][...] = zhat * g_ref[...] + b_ref[...]

    row = lambda w: pl.BlockSpec((TM, w), lambda i: (i, 0))
    vec = pl.BlockSpec((1, D), _CONST2)
    n_f32 = 1 if last else 2
    return pl.pallas_call(
        body, name=name, grid=(t // TM,),
        in_specs=[row(D), row(Q_W), row(SGU_W), row(R_W),
                  pl.BlockSpec((Q_W, D), _CONST2), pl.BlockSpec((SGU_W, D), _CONST2), pl.BlockSpec((D, D), _CONST2),
                  vec, vec, vec],
        out_specs=[row(D)] * (3 + n_f32),
        out_shape=[jax.ShapeDtypeStruct((t, D), BF16)] * 3 + [jax.ShapeDtypeStruct((t, D), F32)] * n_f32,
        compiler_params=_ARB,
    )(x, ya, yb, hr, pa_w, pb_w, wo, b_out, ln_g, ln_b)


def _tail_bwd(dxn, z, pa, pb, hr, wo, pa_w, pb_w, ln_g, ln_b, name, from_loss):
    t = dxn.shape[0]

    def body(dxn_ref, z_ref, pa_ref, pb_ref, hr_ref, wo_ref, paw_ref, pbw_ref, g_ref, b_ref,
             dz_ref, dpa_ref, dpb_ref, dhr_ref, dya_ref, dyb_ref, acc_ref, gbr_ref):
        @pl.when(pl.program_id(0) == 0)
        def _():
            acc_ref[...] = jnp.zeros_like(acc_ref)
            gbr_ref[...] = jnp.zeros_like(gbr_ref)

        zhat, rstd = _ln_stats(z_ref[...])
        if from_loss:
            err = zhat * g_ref[...] + b_ref[...] - dxn_ref[...]
            dxn_v = err * (1.0 / D)
            sq = jnp.sum(jnp.sum(err * err, axis=1, keepdims=True), axis=0, keepdims=True)
            acc_ref[3:4, :] += jnp.broadcast_to(sq, (1, D))
        else:
            dxn_v = dxn_ref[...]
        dz = _ln_bwd(dxn_v * g_ref[...], zhat, rstd)
        dz_ref[...] = dz
        acc_ref[0:1, :] += _colsum(dxn_v * zhat)
        acc_ref[1:2, :] += _colsum(dxn_v)
        acc_ref[2:3, :] += _colsum(dz)
        dmerged = _dot_nt(dz.astype(BF16), wo_ref[...])
        everything = slice(None)
        sa = _sigmoid(_cols(hr_ref, everything, 0, D))
        sb = _sigmoid(_cols(hr_ref, everything, D, D))
        dpa = (dmerged * sa).astype(BF16)
        dpb = (dmerged * sb).astype(BF16)
        dpa_ref[...] = dpa
        dpb_ref[...] = dpb
        dra = dmerged * pa_ref[...].astype(F32) * (sa * (1.0 - sa))
        drb = dmerged * pb_ref[...].astype(F32) * (sb * (1.0 - sb))
        dhr_ref[:, 0:D] = dra.astype(BF16)
        dhr_ref[:, D:2 * D] = drb.astype(BF16)
        gbr_ref[0:1, 0:D] += _colsum(dra)
        gbr_ref[0:1, D:2 * D] += _colsum(drb)
        dya_ref[...] = _dot_nt(dpa, paw_ref[...]).astype(BF16)
        dyb_ref[...] = _dot_nt(dpb, pbw_ref[...]).astype(BF16)

    row = lambda w: pl.BlockSpec((TM, w), lambda i: (i, 0))
    vec = pl.BlockSpec((1, D), _CONST2)
    return pl.pallas_call(
        body, name=name, grid=(t // TM,),
        in_specs=[row(D), row(D), row(D), row(D), row(R_W),
                  pl.BlockSpec((D, D), _CONST2), pl.BlockSpec((Q_W, D), _CONST2), pl.BlockSpec((SGU_W, D), _CONST2),
                  vec, vec],
        out_specs=[row(D), row(D), row(D), row(R_W), row(Q_W), row(SGU_W), pl.BlockSpec((8, D), _CONST2),
                   pl.BlockSpec((8, R_W), _CONST2)],
        out_shape=[jax.ShapeDtypeStruct((t, D), F32), jax.ShapeDtypeStruct((t, D), BF16),
                   jax.ShapeDtypeStruct((t, D), BF16), jax.ShapeDtypeStruct((t, R_W), BF16),
                   jax.ShapeDtypeStruct((t, Q_W), BF16), jax.ShapeDtypeStruct((t, SGU_W), BF16),
                   jax.ShapeDtypeStruct((8, D), F32), jax.ShapeDtypeStruct((8, R_W), F32)],
        compiler_params=_ARB,
    )(dxn, z, pa, pb, hr, wo, pa_w, pb_w, ln_g, ln_b)


def _mixer_bwd(hm, dya, dyb, sinks, bias, vn_g, vn_b, w_s, bsb, nblk_seq, name):
    t = hm.shape[0]
    nt = t // TM

    def body(sinks_ref, hm_ref, hprev_ref, bias_ref, dya_ref, dyb_ref, vng_ref, vnb_ref, ws_ref, bsb_ref,
             dhm_ref, gbm_ref, gsk_ref, gvn_ref, gws_ref, gbs_ref, dk_carry, dv_carry):
        gi = pl.program_id(0)
        i = nt - 1 - gi

        @pl.when(gi == 0)
        def _():
            for r in (gbm_ref, gsk_ref, gvn_ref, gws_ref, gbs_ref, dk_carry, dv_carry):
                r[...] = jnp.zeros_like(r)

        tril = _tril_mask()
        wt = [jnp.where(tril, ws_ref[g], 0.0).astype(BF16) for g in range(4)]
        vng = vng_ref[...]

        def put(rows, col, val):
            dhm_ref[rows, col:col + val.shape[1]] = val.astype(BF16)
            gbm_ref[0:1, col:col + val.shape[1]] += _colsum(val)

        for s in reversed(range(NB)):
            r0 = s * BLK
            rows = slice(r0, r0 + BLK)
            bias = bias_ref[jnp.where((i * NB + s) % nblk_seq == 0, 1, 0)]
            q = _cols(hm_ref, rows, C_Q, Q_W)
            kband = _band(hm_ref, hprev_ref, s, C_K)
            vband = _band(hm_ref, hprev_ref, s, C_V)
            g_a = _cols(hm_ref, rows, C_GA, Q_W)
            sg = _sigmoid(g_a)
            dya_v = _cols(dya_ref, rows, 0, Q_W)
            d_o = dya_v * (g_a * sg)
            o_pairs, dq_pairs = [None] * 4, [None] * 4
            dkband = jnp.zeros((2 * BLK, KV_W), F32)
            dvband = jnp.zeros((2 * BLK, KV_W), F32)
            kband_t, vband_t = kband.T, vband.T
            for hk in range(N_KV):
                q4, kh, prob_t, p_sink, o_t = _attn_group(q, kband, vband_t, hk, sinks_ref, bias)
                o_pairs = _unstack_heads(o_t.T, hk, o_pairs)
                parts = []
                for g in range(4):
                    _, p, pos = _head_place(hk, g)
                    dp = d_o[:, BLK * p:BLK * (p + 1)]
                    parts.append(pltpu.roll(dp, 64, 1) if pos != hk else dp)
                do4 = _head_lanes(jnp.concatenate(parts, axis=0), hk)
                do4b = do4.astype(BF16)
                delta = _colsum(do4.T * o_t)
                vh = _head_lanes(vband, hk).astype(BF16)
                ds_t = prob_t * (_dot_nt(vh, do4b) - delta)
                dsb = ds_t.astype(BF16)
                dq4_t = _dot(_head_rows(kband_t, hk), dsb)
                dq_pairs = _unstack_heads(dq4_t.T * ATTN_SCALE, hk, dq_pairs)
                dkband = dkband + _head_lanes(_dot(dsb, q4), hk)
                dvband = dvband + _dot(prob_t.astype(BF16), do4b)
                dsk = p_sink * delta
                for g in range(4):
                    j = 4 * hk + g
                    tot = jnp.sum(dsk[:, BLK * g:BLK * (g + 1)], axis=1, keepdims=True)
                    gsk_ref[j:j + 1, :] += jnp.broadcast_to(-tot, (1, 128))
            attn = jnp.concatenate(o_pairs, axis=1)
            put(rows, C_Q, jnp.concatenate(dq_pairs, axis=1))
            put(rows, C_K, dkband[BLK:2 * BLK] + dk_carry[...])
            put(rows, C_V, dvband[BLK:2 * BLK] + dv_carry[...])
            dk_carry[...] = dkband[0:BLK]
            dv_carry[...] = dvband[0:BLK]
            put(rows, C_GA, dya_v * attn * (sg * (1.0 + g_a * (1.0 - sg))))
            u_b = _cols(hm_ref, rows, C_UB, SGU_W)
            v_b = _cols(hm_ref, rows, C_VB, SGU_W)
            g_b = _cols(hm_ref, rows, C_GB, SGU_W)
            u, vhat, rstd, vn, mixed = _sgu_fwd(u_b, v_b, vng, vnb_ref[...], wt, bsb_ref)
            sgb = _sigmoid(g_b)
            silu_b = g_b * sgb
            dyb_v = _cols(dyb_ref, rows, 0, SGU_W)
            du = dyb_v * mixed * silu_b
            dmixed = dyb_v * u * silu_b
            put(rows, C_GB, dyb_v * u * mixed * (sgb * (1.0 + g_b * (1.0 - sgb))))
            dvn_parts = []
            for g in range(4):
                cols = slice(BLK * g, BLK * (g + 1))
                dmg = dmixed[:, cols]
                dmgb = dmg.astype(BF16)
                dvn_parts.append(_dot_tn(wt[g], dmgb))
                gws_ref[g] += jnp.where(tril, _dot_nt(dmgb, vn[:, cols].astype(BF16)), 0.0)
                gbs_ref[g] += dmg
            dvn = jnp.concatenate(dvn_parts, axis=1)
            gvn_ref[0:1, :] += _colsum(dvn * vhat)
            gvn_ref[1:2, :] += _colsum(dvn)
            dv = _ln_bwd(dvn * vng, vhat, rstd)
            put(rows, C_UB, du * _dgelu(u_b))
            put(rows, C_VB, dv * _dgelu(v_b))

        @pl.when(gi == nt - 1)
        def _():
            for g in range(4):
                gbs_ref[g] = jnp.broadcast_to(jnp.sum(gbs_ref[g], axis=1, keepdims=True), (BLK, BLK))

    row = lambda w: pl.BlockSpec((TM, w), lambda g: (nt - 1 - g, 0))
    return pl.pallas_call(
        body, name=name, grid=(nt,),
        in_specs=_mixer_in_specs(nt, True) + [
            row(Q_W), row(SGU_W),
            pl.BlockSpec((1, SGU_W), _CONST2), pl.BlockSpec((1, SGU_W), _CONST2),
            pl.BlockSpec((4, BLK, BLK), _CONST3), pl.BlockSpec((4, BLK, BLK), _CONST3)],
        out_specs=[row(MAIN_W), pl.BlockSpec((8, MAIN_W), _CONST2), pl.BlockSpec((8, 128), _CONST2),
                   pl.BlockSpec((8, SGU_W), _CONST2), pl.BlockSpec((4, BLK, BLK), _CONST3),
                   pl.BlockSpec((4, BLK, BLK), _CONST3)],
        out_shape=[jax.ShapeDtypeStruct((t, MAIN_W), BF16), jax.ShapeDtypeStruct((8, MAIN_W), F32),
                   jax.ShapeDtypeStruct((8, 128), F32), jax.ShapeDtypeStruct((8, SGU_W), F32),
                   jax.ShapeDtypeStruct((4, BLK, BLK), F32), jax.ShapeDtypeStruct((4, BLK, BLK), F32)],
        scratch_shapes=[pltpu.VMEM((BLK, KV_W), F32), pltpu.VMEM((BLK, KV_W), F32)],
        compiler_params=_ARB,
    )(sinks, hm, hm, bias, dya, dyb, vn_g, vn_b, w_s, bsb)


def _dx_inproj(dz, dhm, dhr, w, after, name):
    t = dz.shape[0]

    def body(dz_ref, dhm_ref, dhr_ref, w_ref, after_ref, dx_ref):
        dx_ref[...] = (ALPHA * dz_ref[...] + after_ref[0:1, 0:1] + _dot_nt(dhm_ref[...], w_ref[:, 0:MAIN_W])
                       + _dot_nt(dhr_ref[...], w_ref[:, MAIN_W:IN_COLS]))

    row = lambda w: pl.BlockSpec((TM, w), lambda i: (i, 0))
    return pl.pallas_call(
        body, name=name, grid=(t // TM,),
        in_specs=[row(D), row(MAIN_W), row(R_W), pl.BlockSpec((D, IN_COLS), _CONST2),
                  pl.BlockSpec((8, 128), _CONST2)],
        out_specs=row(D), out_shape=jax.ShapeDtypeStruct((t, D), F32), compiler_params=_ARB,
    )(dz, dhm, dhr, w, after)


def _wgrad(a, b, tn, name, tk=1024):
    t, m = a.shape
    n = b.shape[1]
    nk = t // tk

    def body(a_ref, b_ref, o_ref, acc_ref):
        k = pl.program_id(1)

        @pl.when(k == 0)
        def _():
            acc_ref[...] = jnp.zeros_like(acc_ref)

        acc_ref[...] += _dot_tn(a_ref[...].astype(BF16), b_ref[...].astype(BF16))

        @pl.when(k == nk - 1)
        def _():
            o_ref[...] = acc_ref[...].astype(BF16)

    return pl.pallas_call(
        body, name=name, grid=(n // tn, nk),
        in_specs=[pl.BlockSpec((tk, m), lambda j, k: (k, 0)), pl.BlockSpec((tk, tn), lambda j, k: (k, j))],
        out_specs=pl.BlockSpec((m, tn), lambda j, k: (0, j)),
        out_shape=jax.ShapeDtypeStruct((m, n), BF16),
        scratch_shapes=[pltpu.VMEM((m, tn), F32)],
        compiler_params=pltpu.CompilerParams(dimension_semantics=("arbitrary", "arbitrary"), vmem_limit_bytes=VMEM_LIMIT),
    )(a, b)


def _ln_in_bwd(dx0, x, g, name):
    t = x.shape[0]

    def body(dx0_ref, x_ref, g_ref, gx_ref, acc_ref):
        @pl.when(pl.program_id(0) == 0)
        def _():
            acc_ref[...] = jnp.zeros_like(acc_ref)

        d = dx0_ref[...]
        xhat, rstd = _ln_stats(x_ref[...])
        gx_ref[...] = _ln_bwd(d * g_ref[...], xhat, rstd)
        acc_ref[0:1, :] += _colsum(d * xhat)
        acc_ref[1:2, :] += _colsum(d)

    row = pl.BlockSpec((TM_EW, D), lambda i: (i, 0))
    return pl.pallas_call(
        body, name=name, grid=(t // TM_EW,),
        in_specs=[row, row, pl.BlockSpec((1, D), _CONST2)],
        out_specs=[row, pl.BlockSpec((8, D), _CONST2)],
        out_shape=[jax.ShapeDtypeStruct((t, D), F32), jax.ShapeDtypeStruct((8, D), F32)],
        compiler_params=_ARB,
    )(dx0, x, g)


_ANY = pl.BlockSpec(memory_space=pl.ANY)


def _place():
    return lax.axis_index("x"), lax.axis_index("y"), lax.axis_index("c")


def _allgather8(xs, name):
    n = len(xs)

    def body(*refs):
        x_refs, o_refs = refs[:n], refs[n:2 * n]
        send_sems, recv_sems, local_sems = refs[2 * n:]
        x, y, c = _place()
        me, sib = (x, y, c), (x, y, 1 - c)
        chips = [(1 - x, y), (x, 1 - y), (1 - x, 1 - y)]

        def copy(a, k, block, to, src=None):
            dst = o_refs[a].at[4 * block[0] + 2 * block[1] + block[2]]
            return pltpu.make_async_remote_copy(
                src_ref=dst if src is None else src, dst_ref=dst, send_sem=send_sems.at[7 * a + k],
                recv_sem=recv_sems.at[7 * a + k], device_id=to, device_id_type=MESH)

        mine = [pltpu.make_async_copy(x_refs[a], o_refs[a].at[4 * x + 2 * y + c], local_sems.at[a]) for a in range(n)]
        for cp in mine:
            cp.start()
        sent = []
        for a in range(n):
            sent.append(copy(a, 0, me, sib, src=x_refs[a]))
            sent += [copy(a, 1 + j, me, (*chip, c), src=x_refs[a]) for j, chip in enumerate(chips)]
        for cp in sent:
            cp.start()
        for j, chip in enumerate(chips):
            for a in range(n):
                copy(a, 1 + j, (*chip, c), me).wait_recv()
                fwd = copy(a, 4 + j, (*chip, c), sib)
                fwd.start()
                sent.append(fwd)
        for a in range(n):
            copy(a, 0, sib, me).wait_recv()
            for j, chip in enumerate(chips):
                copy(a, 4 + j, (*chip, 1 - c), me).wait_recv()
        for cp in sent:
            cp.wait_send()
        for cp in mine:
            cp.wait()

    return pl.pallas_call(
        body, name=name, in_specs=[_ANY] * n, out_specs=[_ANY] * n,
        out_shape=[jax.ShapeDtypeStruct((N_DEV,) + v.shape, v.dtype) for v in xs],
        scratch_shapes=[pltpu.SemaphoreType.DMA((7 * n,)), pltpu.SemaphoreType.DMA((7 * n,)),
                        pltpu.SemaphoreType.DMA((n,))],
    )(*xs)


def _forward_sibling(lands, name):
    n = len(lands)

    def body(*refs):
        l_refs = refs[n:2 * n]
        send_sems, recv_sems = refs[2 * n:]
        x, y, c = _place()
        chips = [(1 - x, y), (x, 1 - y), (1 - x, 1 - y)]

        def copy(a, j, core):
            rows = l_refs[a].at[4 * chips[j][0] + 2 * chips[j][1] + core]
            return pltpu.make_async_remote_copy(
                src_ref=rows, dst_ref=rows, send_sem=send_sems.at[3 * a + j], recv_sem=recv_sems.at[3 * a + j],
                device_id=(x, y, 1 - c), device_id_type=MESH)

        for a in range(n):
            for j in range(3):
                copy(a, j, c).start()
        for a in range(n):
            for j in range(3):
                copy(a, j, 1 - c).wait_recv()
                copy(a, j, c).wait_send()

    return pl.pallas_call(
        body, name=name, in_specs=[_ANY] * n, out_specs=[_ANY] * n,
        out_shape=[jax.ShapeDtypeStruct(v.shape, v.dtype) for v in lands],
        input_output_aliases={a: a for a in range(n)},
        scratch_shapes=[pltpu.SemaphoreType.DMA((3 * n,)), pltpu.SemaphoreType.DMA((3 * n,))],
    )(*lands)


def _swap_sibling(gs, name):
    n = len(gs)

    def body(*refs):
        g_refs, r_refs = refs[:n], refs[n:2 * n]
        send_sems, recv_sems = refs[2 * n:]
        x, y, c = _place()
        cps = [pltpu.make_async_remote_copy(
            src_ref=g_refs[a].at[1 - c], dst_ref=r_refs[a], send_sem=send_sems.at[a], recv_sem=recv_sems.at[a],
            device_id=(x, y, 1 - c), device_id_type=MESH) for a in range(n)]
        for cp in cps:
            cp.start()
        for cp in cps:
            cp.wait()

    return pl.pallas_call(
        body, name=name, in_specs=[_ANY] * n, out_specs=[_ANY] * n,
        out_shape=[jax.ShapeDtypeStruct(v.shape[1:], v.dtype) for v in gs],
        scratch_shapes=[pltpu.SemaphoreType.DMA((n,)), pltpu.SemaphoreType.DMA((n,))],
    )(*gs)


def _row_tile(rows, lanes, cap):
    if rows * lanes * 4 <= (1 << 20):
        return rows
    return max(d for d in range(8, cap + 1, 8) if rows % d == 0 and (d % 16 == 0 or rows % 16 != 0))


def _pair_sum(g, r, name):
    _, n, rows, l = g.shape
    tr = _row_tile(rows, l, 512)

    def body(c_ref, g_ref, r_ref, o_ref):
        o_ref[...] = (g_ref[0].astype(F32) + r_ref[...].astype(F32)).astype(o_ref.dtype)

    return pl.pallas_call(
        body, name=name,
        grid_spec=pltpu.PrefetchScalarGridSpec(
            num_scalar_prefetch=1, grid=(n, rows // tr),
            in_specs=[pl.BlockSpec((1, 1, tr, l), lambda q, i, c_ref: (c_ref[0], q, i, 0)),
                      pl.BlockSpec((1, tr, l), lambda q, i, c_ref: (q, i, 0))],
            out_specs=pl.BlockSpec((1, tr, l), lambda q, i, c_ref: (q, i, 0))),
        out_shape=jax.ShapeDtypeStruct((n, rows, l), g.dtype),
        compiler_params=pltpu.CompilerParams(dimension_semantics=("arbitrary", "arbitrary"), vmem_limit_bytes=VMEM_LIMIT),
    )(lax.axis_index("c").astype(jnp.int32).reshape(1), g, r)


def _adamw(parts, w, m, v, name, own=None):
    nl = len(parts)
    ns, rows, l = parts[0].shape
    tr = _row_tile(rows, l * ns, 256)
    nt = rows // tr
    c1 = 1.0 - ADAM_B1 ** ADAM_STEP
    c2 = 1.0 - ADAM_B2 ** ADAM_STEP

    def body(q_ref, *refs):
        own_refs = refs[:nl] if own is not None else None
        p_refs = refs[-7 - nl:-7]
        w_ref, m_ref, v_ref, g_ref, d_ref, nm_ref, nv_ref = refs[-7:]
        layer = pl.program_id(0)
        g = None
        for j in range(nl):
            gj = None
            for k in range(ns):
                term = p_refs[j][k].astype(F32)
                if own_refs is not None:
                    term = jnp.where(q_ref[0] == k, own_refs[j][0].astype(F32), term)
                gj = term if gj is None else gj + term
            g = gj if g is None else jnp.where(layer == j, gj, g)
        g_ref[...] = g
        nm = ADAM_B1 * m_ref[...] + (1.0 - ADAM_B1) * g
        nv = ADAM_B2 * v_ref[...] + (1.0 - ADAM_B2) * (g * g)
        nm_ref[...] = nm
        nv_ref[...] = nv
        d_ref[...] = -ADAM_LR * ((nm / c1) / (jnp.sqrt(nv / c2) + ADAM_EPS) + ADAM_WD * w_ref[...])

    def tile_of(j):
        return lambda la, i, q: jnp.where(la == j, i, jnp.where(la < j, 0, nt - 1))

    row = pl.BlockSpec((tr, l), lambda la, i, q: (la * nt + i, 0))
    own_specs = [] if own is None else [
        pl.BlockSpec((1, tr, l), lambda la, i, q, j=j: (q[0], tile_of(j)(la, i, q), 0)) for j in range(nl)]
    part_specs = [pl.BlockSpec((ns, tr, l), lambda la, i, q, j=j: (0, tile_of(j)(la, i, q), 0)) for j in range(nl)]
    chip = (2 * lax.axis_index("x") + lax.axis_index("y")).astype(jnp.int32).reshape(1)
    return pl.pallas_call(
        body, name=name,
        grid_spec=pltpu.PrefetchScalarGridSpec(
            num_scalar_prefetch=1, grid=(nl, nt),
            in_specs=own_specs + part_specs + [row, row, row], out_specs=[row] * 4),
        out_shape=[jax.ShapeDtypeStruct((nl * rows, l), F32)] * 4,
        compiler_params=pltpu.CompilerParams(dimension_semantics=("arbitrary", "arbitrary"), vmem_limit_bytes=VMEM_LIMIT),
    )(chip, *([] if own is None else own), *parts, w, m, v)


_HBM = pl.BlockSpec(memory_space=pltpu.HBM)
_SEM = pl.BlockSpec(memory_space=pltpu.SEMAPHORE)
_EFFECT = pltpu.SideEffectType.DATAFLOW_SIDE_EFFECTING


def _plan_all(x, y, c):
    me = 4 * x + 2 * y + c
    peers = [(x, y, 1 - c), (1 - x, y, c), (x, 1 - y, c), (1 - x, 1 - y, c),
             (1 - x, y, 1 - c), (x, 1 - y, 1 - c), (1 - x, 1 - y, 1 - c)]
    return [(None, me, p, 4 * p[0] + 2 * p[1] + p[2]) for p in peers]


def _plan_near(x, y, c):
    me = 4 * x + 2 * y + c
    peers = [(x, y, 1 - c), (1 - x, y, c), (x, 1 - y, c), (1 - x, 1 - y, c)]
    return [(None, me, p, 4 * p[0] + 2 * p[1] + p[2]) for p in peers]


def _plan_chips(x, y, c):
    me = 2 * x + y
    return [(2 * qx + qy, me, (qx, qy, c), 2 * qx + qy) for qx, qy in ((1 - x, y), (x, 1 - y), (1 - x, 1 - y))]


def _split_copies(plan, src_refs, land_refs, send_sems, recv_sems, arrival):
    n = len(src_refs)
    entries = plan(*_place())
    per = len(entries)
    cps = []
    for a in range(n):
        for k, (src_slot, dst_slot, peer, back_slot) in enumerate(entries):
            src = src_refs[a] if src_slot is None else src_refs[a].at[src_slot]
            cps.append(pltpu.make_async_remote_copy(
                src_ref=src, dst_ref=land_refs[a].at[back_slot if arrival else dst_slot],
                send_sem=send_sems.at[per * a + k], recv_sem=recv_sems.at[per * a + k],
                device_id=peer, device_id_type=MESH))
    return cps


def _split_start(srcs, lands, plan, per, name):
    n = len(srcs)

    def body(*refs):
        for cp in _split_copies(plan, refs[:n], refs[n:2 * n], refs[2 * n], refs[2 * n + 1], False):
            cp.start()
        refs[-1][...] = jnp.zeros_like(refs[-1])

    both = list(srcs) + list(lands)
    outs = pl.pallas_call(
        body, name=name,
        out_shape=(pltpu.SemaphoreType.DMA((per * n,)), pltpu.SemaphoreType.DMA((per * n,)),
                   *[pltpu.HBM(v.shape, v.dtype) for v in both], jax.ShapeDtypeStruct((8, 128), F32)),
        in_specs=[_HBM] * (2 * n),
        out_specs=(_SEM, _SEM, *[_HBM] * (2 * n), pl.BlockSpec(memory_space=pltpu.VMEM)),
        input_output_aliases={i: 2 + i for i in range(2 * n)},
        compiler_params=pltpu.CompilerParams(has_side_effects=_EFFECT),
    )(*[pltpu.with_memory_space_constraint(v, pltpu.HBM) for v in both])
    return outs[0], outs[1], list(outs[2:2 + 2 * n]), outs[-1]


def _split_wait(send_sems, recv_sems, thru, plan, after, name):
    n = len(thru) // 2

    def body(*refs):
        for cp in _split_copies(plan, refs[:n], refs[n:2 * n], refs[2 * n], refs[2 * n + 1], True):
            cp.wait_send()
            cp.wait_recv()

    outs = pl.pallas_call(
        body, name=name, out_shape=tuple(pltpu.HBM(v.shape, v.dtype) for v in thru),
        in_specs=[_HBM] * (2 * n) + [_SEM, _SEM, pl.BlockSpec(memory_space=pl.ANY)],
        out_specs=[_HBM] * (2 * n), input_output_aliases={i: i for i in range(2 * n)},
        compiler_params=pltpu.CompilerParams(has_side_effects=_EFFECT),
    )(*thru, send_sems, recv_sems, after)
    return list(outs[:n]), list(outs[n:])


_SMALL_IN = ("ln_in_g", "ln_in_b")
_SMALL = ("b_in", "sinks", "vn_g", "vn_b", "w_s", "b_s", "b_out", "ln_g", "ln_b")


def _pack_small(d, names):
    flat = jnp.concatenate([d[n].reshape(-1) for n in names])
    return jnp.pad(flat, (0, (-flat.shape[0]) % 1024)).reshape(-1, 128)


def _unpack_small(p, like, names):
    flat, off, out = p.reshape(-1), 0, {}
    for n in names:
        size = like[n].size
        out[n] = flat[off:off + size].reshape(like[n].shape)
        off += size
    return out


def _owner_major(g, axis):
    sh = g.shape
    g = g.reshape(sh[:axis] + (4, 2, sh[axis] // N_DEV) + sh[axis + 1:])
    g = jnp.moveaxis(g, (axis, axis + 1), (1, 0))
    return g


def kernel(x, ln_in_g, ln_in_b, w_in, b_in, sinks, vn_g, vn_b, w_s, b_s, p_a, p_b, w_out, b_out, ln_g, ln_b, loss_target, m_ln_in_g, m_ln_in_b, m_w_in, m_b_in, m_sinks, m_vn_g, m_vn_b, m_w_s, m_b_s, m_p_a, m_p_b, m_w_out, m_b_out, m_ln_g, m_ln_b, v_ln_in_g, v_ln_in_b, v_w_in, v_b_in, v_sinks, v_vn_g, v_vn_b, v_w_s, v_b_s, v_p_a, v_p_b, v_w_out, v_b_out, v_ln_g, v_ln_b):
    nseq, seq, _ = x.shape
    t = nseq * seq
    nblk_seq = seq // BLK
    x2 = x.reshape(t, D)
    tgt = loss_target.reshape(t, D)

    def blocks(l):
        return [w_in[l].astype(BF16), p_a[l].astype(BF16), p_b[l].astype(BF16), w_out[l].astype(BF16)]

    def full_weights(g):
        w_full = jnp.moveaxis(g[0], 0, 1).reshape(D, IN_COLS)
        pa_full = jnp.moveaxis(g[1], 0, 1).reshape(Q_W, D)
        pb_full = jnp.moveaxis(g[2], 0, 1).reshape(SGU_W, D)
        wo_full = g[3].reshape(D, D)
        return dict(w=w_full, pa=pa_full, pb=pb_full, wo=wo_full)

    def landing(bs):
        return [lax.empty((N_DEV,) + v.shape, v.dtype) for v in bs]

    def with_own(landed, sent):
        return [lax.dynamic_update_index_in_dim(g, b, me, 0) for g, b in zip(landed, sent)]

    me = 4 * lax.axis_index("x") + 2 * lax.axis_index("y") + lax.axis_index("c")
    blocks0 = blocks(0)
    ag_send, ag_recv, ag_thru, ag_token = _split_start(blocks0, landing(blocks0), _plan_near, 4,
                                                       "allgather_weights0_start")
    xs = [_ln_fwd(x2, ln_in_g + ag_token[0, 0], ln_in_b, "ln_in_fwd")]
    sent, landed = _split_wait(ag_send, ag_recv, ag_thru, _plan_near, xs[0], "allgather_weights0_wait")
    gathered0 = with_own(_forward_sibling(landed, "allgather_weights0_forward"), sent)
    blocks1, gathered0 = lax.optimization_barrier((blocks(1), gathered0))
    ag_send, ag_recv, ag_thru, ag_token = _split_start(blocks1, landing(blocks1), _plan_all, 7,
                                                       "allgather_weights1_start")
    weights = [full_weights(gathered0), None]
    bsb = jnp.broadcast_to(b_s[:, :, :, None], (DEPTH, 4, BLK, BLK))
    bias = _band_bias()

    saved = []
    for l in range(DEPTH):
        if l == 1:
            sent, landed = _split_wait(ag_send, ag_recv, ag_thru, _plan_all, xs[1], "allgather_weights1_wait")
            weights[1] = full_weights(with_own(landed, sent))
        wl = weights[l]
        last = l == DEPTH - 1
        b_l = b_in[l].reshape(1, -1) + (ag_token[0, 0] if l == 0 else 0.0)
        hm, hr = _inproj(xs[l], wl["w"], b_l, f"inproj{l}")
        ya, yb = _mixer_fwd(hm, sinks[l], bias, vn_g[l].reshape(1, -1), vn_b[l].reshape(1, -1), w_s[l], bsb[l],
                            nblk_seq, f"mixer_fwd{l}")
        outs = _tail_fwd(xs[l], ya, yb, hr, wl["pa"], wl["pb"], wl["wo"], b_out[l].reshape(1, D),
                         ln_g[l].reshape(1, D), ln_b[l].reshape(1, D), f"tail_fwd{l}", last)
        saved.append((hm, hr, ya, yb) + tuple(outs[:4]))
        if not last:
            xs.append(outs[4])

    small = {n: [None] * DEPTH for n in _SMALL}
    names = ("w_in", "p_a", "p_b", "w_out")
    owner_axis = {"w_in": 1, "p_a": 1, "p_b": 1, "w_out": 0}
    token = jnp.zeros((8, 128), F32)
    dx = tgt
    split = [None] * DEPTH
    for l in reversed(range(DEPTH)):
        hm, hr, ya, yb, pa, pb, merged, z = saved[l]
        wl = weights[l]
        dz, dpa, dpb, dhr, dya, dyb, acc, gbr = _tail_bwd(
            dx, z, pa, pb, hr, wl["wo"], wl["pa"], wl["pb"], ln_g[l].reshape(1, D) + token[0, 0],
            ln_b[l].reshape(1, D), f"tail_bwd{l}", l == DEPTH - 1)
        if l == DEPTH - 1:
            loss = lax.psum(acc[3, 0] * (0.5 / D), ("x", "y", "c"))
        dhm, gbm, gsk, gvn, gws, gbs = _mixer_bwd(
            hm, dya, dyb, sinks[l], bias, vn_g[l].reshape(1, -1), vn_b[l].reshape(1, -1), w_s[l], bsb[l],
            nblk_seq, f"mixer_bwd{l}")
        grads = {"w_in": jnp.concatenate([_wgrad(xs[l], dhm, MAIN_W // 2, f"wgrad_in_main{l}"),
                                          _wgrad(xs[l], dhr, R_W // 2, f"wgrad_in_route{l}")], axis=1),
                 "p_a": _wgrad(ya, dpa, D, f"wgrad_pa{l}"), "p_b": _wgrad(yb, dpb, D, f"wgrad_pb{l}"),
                 "w_out": _wgrad(merged, dz, D, f"wgrad_out{l}")}
        small["b_in"][l] = jnp.concatenate([gbm[0], gbr[0]])
        small["sinks"][l] = gsk[:, 0]
        small["vn_g"][l], small["vn_b"][l] = gvn[0], gvn[1]
        small["w_s"][l], small["b_s"][l] = gws, gbs[:, :, 0]
        small["ln_g"][l], small["ln_b"][l], small["b_out"][l] = acc[0], acc[1], acc[2]
        parts = [_owner_major(grads[n], owner_axis[n]).astype(BF16) for n in names]
        if l == 0:
            packed = _pack_small({n: jnp.stack(v) for n, v in small.items()}, _SMALL)
            parts.append(jnp.broadcast_to(packed[None, None], (2, 1) + packed.shape))
        from_sib = _swap_sibling(parts, f"rs_sibling{l}")
        pair = [_pair_sum(g, r, f"pair_sum{l}_{a}") for a, (g, r) in enumerate(zip(parts, from_sib))]
        if l == 0:
            pair[4] = jnp.broadcast_to(pair[4], (4,) + packed.shape)
        lands = [jnp.zeros(p.shape, p.dtype) for p in pair]
        split[l] = _split_start(pair, lands, _plan_chips, 3, f"rs_chips{l}_start")
        token = split[l][3]
        dx = _dx_inproj(dz, dhm, dhr, wl["w"], token, f"dx_inproj{l}")
    grad_x, acc_in = _ln_in_bwd(dx, x2, ln_in_g.reshape(1, D), "ln_in_bwd")
    (all_in,) = _allgather8([acc_in], "allgather_ln_in")

    given = {"w_in": (w_in, m_w_in, v_w_in), "p_a": (p_a, m_p_a, v_p_a), "p_b": (p_b, m_p_b, v_p_b),
             "w_out": (w_out, m_w_out, v_w_out)}
    waited = [_split_wait(split[l][0], split[l][1], split[l][2], _plan_chips, all_in, f"rs_chips{l}_wait")
              for l in range(DEPTH)]
    res = {}
    for a, n in enumerate(names):
        rows, lanes = waited[0][1][a].shape[1:]
        outs = _adamw([waited[l][1][a] for l in range(DEPTH)], *[v.reshape(DEPTH * rows, lanes) for v in given[n]],
                      f"adamw_{n}", own=[waited[l][0][a] for l in range(DEPTH)])
        res[n] = [o.reshape(given[n][0].shape) for o in outs]

    w_small = dict(ln_in_g=ln_in_g, ln_in_b=ln_in_b, b_in=b_in, sinks=sinks, vn_g=vn_g, vn_b=vn_b, w_s=w_s, b_s=b_s,
                   b_out=b_out, ln_g=ln_g, ln_b=ln_b)
    m_small = dict(ln_in_g=m_ln_in_g, ln_in_b=m_ln_in_b, b_in=m_b_in, sinks=m_sinks, vn_g=m_vn_g, vn_b=m_vn_b,
                   w_s=m_w_s, b_s=m_b_s, b_out=m_b_out, ln_g=m_ln_g, ln_b=m_ln_b)
    v_small = dict(ln_in_g=v_ln_in_g, ln_in_b=v_ln_in_b, b_in=v_b_in, sinks=v_sinks, vn_g=v_vn_g, vn_b=v_vn_b,
                   w_s=v_w_s, b_s=v_b_s, b_out=v_b_out, ln_g=v_ln_g, ln_b=v_ln_b)
    outs = _adamw([waited[0][1][4]], *[_pack_small(d, _SMALL) for d in (w_small, m_small, v_small)], "adamw_small",
                  own=[waited[0][0][4]])
    outs_in = _adamw([all_in], *[jnp.pad(jnp.stack([d[n] for n in _SMALL_IN]), ((0, 6), (0, 0)))
                                 for d in (w_small, m_small, v_small)], "adamw_ln_in")
    for k in range(4):
        u = _unpack_small(outs[k], w_small, _SMALL)
        u.update({n: outs_in[k][r] for r, n in enumerate(_SMALL_IN)})
        for n in u:
            res.setdefault(n, [None] * 4)[k] = u[n]

    order = ("ln_in_g", "ln_in_b", "w_in", "b_in", "sinks", "vn_g", "vn_b", "w_s", "b_s", "p_a", "p_b", "w_out",
             "b_out", "ln_g", "ln_b")
    return (loss, grad_x.reshape(x.shape), *[res[n][0] for n in order], *[res[n][1] for n in order],
            *[res[n][2] for n in order], *[res[n][3] for n in order])
```

```python
import jax
import jax.numpy as jnp
from jax import lax
from jax.experimental import pallas as pl
from jax.experimental.pallas import tpu as pltpu

F32 = jnp.float32
BF16 = jnp.bfloat16

D = 1024
BLK = 128
N_KV = 2
Q_W, KV_W, SGU_W = 512, 128, 512
C_Q, C_K, C_V, C_GA, C_UB, C_VB, C_GB = 0, 512, 640, 768, 1280, 1792, 2304
MAIN_W = 2816
R_W = 2048
IN_COLS = MAIN_W + R_W
N_DEV = 8
SHARD_COLS = IN_COLS // N_DEV

DEPTH = 2
ALPHA = (2.0 * DEPTH) ** 0.25
LN_EPS = 1e-5
ATTN_SCALE = 0.125
NEG = float(jnp.finfo(jnp.float32).min)

ADAM_LR, ADAM_B1, ADAM_B2, ADAM_EPS, ADAM_WD, ADAM_STEP = 0.001, 0.9, 0.999, 1e-08, 0.01, 10

TM = 256
TM_EW = 512
NB = TM // BLK
MESH = pl.DeviceIdType.MESH
VMEM_LIMIT = 56 * 1024 * 1024

_ARB = pltpu.CompilerParams(dimension_semantics=("arbitrary",), vmem_limit_bytes=VMEM_LIMIT)


def _sigmoid(x):
    return 1.0 / (1.0 + jnp.exp(-x))


_GELU_C = 0.7978845608028654
_GELU_A = 0.044715


def _gelu(x):
    return 0.5 * x * (1.0 + jnp.tanh(_GELU_C * (x + _GELU_A * x * x * x)))


def _dgelu(x):
    t = jnp.tanh(_GELU_C * (x + _GELU_A * x * x * x))
    return 0.5 * (1.0 + t) + 0.5 * x * (1.0 - t * t) * (_GELU_C * (1.0 + 3.0 * _GELU_A * x * x))


def _ln_stats(x):
    mu = jnp.mean(x, axis=-1, keepdims=True)
    xc = x - mu
    var = jnp.mean(xc * xc, axis=-1, keepdims=True)
    rstd = lax.rsqrt(var + LN_EPS)
    return xc * rstd, rstd


def _ln_bwd(dy_g, xhat, rstd):
    m1 = jnp.mean(dy_g, axis=-1, keepdims=True)
    m2 = jnp.mean(dy_g * xhat, axis=-1, keepdims=True)
    return rstd * (dy_g - m1 - xhat * m2)


def _colsum(x):
    return jnp.sum(x, axis=0, keepdims=True)


def _dot(a, b):
    return jnp.dot(a, b, preferred_element_type=F32)


def _dot_nt(a, b):
    return lax.dot_general(a, b, (((1,), (1,)), ((), ())), preferred_element_type=F32)


def _dot_tn(a, b):
    return lax.dot_general(a, b, (((0,), (0,)), ((), ())), preferred_element_type=F32)


def _head_place(hk, g):
    j = 4 * hk + g
    return j, j // 2, j % 2


def _head_rows(x, hk):
    d = lax.broadcasted_iota(jnp.int32, x.shape, 0)
    return jnp.where((d >= 64 * hk) & (d < 64 * hk + 64), x, 0.0).astype(BF16)


def _head_lanes(x, hk):
    d = lax.broadcasted_iota(jnp.int32, x.shape, 1)
    return jnp.where((d >= 64 * hk) & (d < 64 * hk + 64), x, 0.0)


def _band_bias():
    kpos = lax.broadcasted_iota(jnp.int32, (2 * BLK, 4 * BLK), 0)
    row = lax.broadcasted_iota(jnp.int32, (2 * BLK, 4 * BLK), 1) & (BLK - 1)
    band = (kpos > row) & (kpos <= row + BLK)
    return jnp.stack([jnp.where(band, 0.0, NEG), jnp.where(band & (kpos >= BLK), 0.0, NEG)]).astype(F32)


def _attn_group(q, kband, vband_t, hk, sinks_ref, bias):
    kh = _head_lanes(kband, hk).astype(BF16)
    parts = []
    for g in range(4):
        _, p, pos = _head_place(hk, g)
        qp = q[:, BLK * p:BLK * (p + 1)] * ATTN_SCALE
        if pos != hk:
            qp = pltpu.roll(qp, 64, 1)
        parts.append(qp.astype(BF16))
    q4 = jnp.concatenate(parts, axis=0)
    s_t = _dot_nt(kh, q4) + bias
    sink_row = jnp.concatenate(
        [jnp.full((1, BLK), sinks_ref[4 * hk + g], F32) for g in range(4)], axis=1)
    m = jnp.maximum(jnp.max(s_t, axis=0, keepdims=True), sink_row)
    p_un = jnp.exp(s_t - m)
    e_sink = jnp.exp(sink_row - m)
    inv = 1.0 / (jnp.sum(p_un, axis=0, keepdims=True) + e_sink)
    prob_t = p_un * inv
    o_t = _dot(_head_rows(vband_t, hk), prob_t.astype(BF16))
    return q4, kh, prob_t, e_sink * inv, o_t


def _unstack_heads(x4, hk, pairs):
    for g in range(4):
        _, p, pos = _head_place(hk, g)
        xg = x4[BLK * g:BLK * (g + 1)]
        if pos != hk:
            xg = pltpu.roll(xg, 64, 1)
        pairs[p] = xg if pairs[p] is None else pairs[p] + xg
    return pairs


def _attn_fwd(q, kband, vband, sinks_ref, bias):
    pairs = [None] * 4
    vband_t = vband.T
    for hk in range(N_KV):
        o_t = _attn_group(q, kband, vband_t, hk, sinks_ref, bias)[-1]
        pairs = _unstack_heads(o_t.T, hk, pairs)
    return jnp.concatenate(pairs, axis=1)


def _tril_mask():
    r = lax.broadcasted_iota(jnp.int32, (BLK, BLK), 0)
    c = lax.broadcasted_iota(jnp.int32, (BLK, BLK), 1)
    return c <= r


def _sgu_fwd(u_b, v_b, vn_g, vn_b, wt, bsb_ref):
    u = _gelu(u_b)
    v = _gelu(v_b)
    vhat, rstd = _ln_stats(v)
    vn = vhat * vn_g + vn_b
    mixed = jnp.concatenate(
        [_dot(wt[g], vn[:, BLK * g:BLK * (g + 1)].astype(BF16)) + bsb_ref[g] for g in range(4)], axis=1)
    return u, vhat, rstd, vn, mixed


def _cols(ref, rows, col, width):
    return ref[rows, col:col + width].astype(F32)


def _band(hm_ref, hprev_ref, s, col):
    r0 = s * BLK
    cur = hm_ref[r0:r0 + BLK, col:col + KV_W]
    if s == 0:
        off = 0 if col == C_K else KV_W
        prev = hprev_ref[:, off:off + KV_W]
    else:
        prev = hm_ref[r0 - BLK:r0, col:col + KV_W]
    return jnp.concatenate([prev, cur], axis=0).astype(F32)


def _mixer_in_specs(nt, rev):
    def tile(g):
        return nt - 1 - g if rev else g

    return [
        pl.BlockSpec(memory_space=pltpu.SMEM),
        pl.BlockSpec((TM, MAIN_W), lambda g: (tile(g), 0)),
        pl.BlockSpec((BLK, 2 * KV_W), lambda g: (jnp.maximum(tile(g) * NB - 1, 0), 2)),
        pl.BlockSpec((2, 2 * BLK, 4 * BLK), lambda g: (0, 0, 0)),
    ]


_CONST2 = lambda g: (0, 0)
_CONST3 = lambda g: (0, 0, 0)


def _ln_fwd(x, g, b, name):
    t = x.shape[0]

    def body(x_ref, g_ref, b_ref, o_ref):
        xhat, _ = _ln_stats(x_ref[...])
        o_ref[...] = xhat * g_ref[...] + b_ref[...]

    return pl.pallas_call(
        body, name=name, grid=(t // TM_EW,),
        in_specs=[pl.BlockSpec((TM_EW, D), lambda i: (i, 0)), pl.BlockSpec((1, D), _CONST2),
                  pl.BlockSpec((1, D), _CONST2)],
        out_specs=pl.BlockSpec((TM_EW, D), lambda i: (i, 0)),
        out_shape=jax.ShapeDtypeStruct((t, D), F32), compiler_params=_ARB,
    )(x, g.reshape(1, D), b.reshape(1, D))


def _inproj(x, w_t, b, name):
    t = x.shape[0]

    def body(x_ref, wt_ref, b_ref, hm_ref, hr_ref):
        xb = x_ref[...].astype(BF16)
        hm_ref[...] = (_dot_nt(xb, wt_ref[0:MAIN_W, :]) + b_ref[:, 0:MAIN_W]).astype(BF16)
        hr_ref[...] = (_dot_nt(xb, wt_ref[MAIN_W:IN_COLS, :]) + b_ref[:, MAIN_W:IN_COLS]).astype(BF16)

    return pl.pallas_call(
        body, name=name, grid=(t // TM,),
        in_specs=[pl.BlockSpec((TM, D), lambda i: (i, 0)),
                  pl.BlockSpec((IN_COLS, D), _CONST2), pl.BlockSpec((1, IN_COLS), _CONST2)],
        out_specs=[pl.BlockSpec((TM, MAIN_W), lambda i: (i, 0)), pl.BlockSpec((TM, R_W), lambda i: (i, 0))],
        out_shape=[jax.ShapeDtypeStruct((t, MAIN_W), BF16), jax.ShapeDtypeStruct((t, R_W), BF16)],
        compiler_params=_ARB,
    )(x, w_t, b)


def _mixer_fwd(hm, sinks, bias, vn_g, vn_b, w_s, bsb, nblk_seq, name):
    t = hm.shape[0]
    nt = t // TM

    def body(sinks_ref, hm_ref, hprev_ref, bias_ref, vng_ref, vnb_ref, ws_ref, bsb_ref, ya_ref, yb_ref):
        i = pl.program_id(0)
        tril = _tril_mask()
        wt = [jnp.where(tril, ws_ref[g], 0.0).astype(BF16) for g in range(4)]
        for s in range(NB):
            r0 = s * BLK
            rows = slice(r0, r0 + BLK)
            bias = bias_ref[jnp.where((i * NB + s) % nblk_seq == 0, 1, 0)]
            attn = _attn_fwd(_cols(hm_ref, rows, C_Q, Q_W), _band(hm_ref, hprev_ref, s, C_K),
                             _band(hm_ref, hprev_ref, s, C_V), sinks_ref, bias)
            g_a = _cols(hm_ref, rows, C_GA, Q_W)
            ya_ref[rows, :] = (attn * (g_a * _sigmoid(g_a))).astype(BF16)
            u, _, _, _, mixed = _sgu_fwd(_cols(hm_ref, rows, C_UB, SGU_W), _cols(hm_ref, rows, C_VB, SGU_W),
                                         vng_ref[...], vnb_ref[...], wt, bsb_ref)
            g_b = _cols(hm_ref, rows, C_GB, SGU_W)
            yb_ref[rows, :] = (u * mixed * (g_b * _sigmoid(g_b))).astype(BF16)

    return pl.pallas_call(
        body, name=name, grid=(nt,),
        in_specs=_mixer_in_specs(nt, False) + [
            pl.BlockSpec((1, SGU_W), _CONST2), pl.BlockSpec((1, SGU_W), _CONST2),
            pl.BlockSpec((4, BLK, BLK), _CONST3), pl.BlockSpec((4, BLK, BLK), _CONST3)],
        out_specs=[pl.BlockSpec((TM, Q_W), lambda i: (i, 0)), pl.BlockSpec((TM, SGU_W), lambda i: (i, 0))],
        out_shape=[jax.ShapeDtypeStruct((t, Q_W), BF16), jax.ShapeDtypeStruct((t, SGU_W), BF16)],
        compiler_params=_ARB,
    )(sinks, hm, hm, bias, vn_g, vn_b, w_s, bsb)


def _tail_fwd(x, ya, yb, hr, pa_w, pb_w, wo, b_out, ln_g, ln_b, name, last):
    t = x.shape[0]

    def body(x_ref, ya_ref, yb_ref, hr_ref, paw_ref, pbw_ref, wo_ref, bo_ref, g_ref, b_ref,
             pa_ref, pb_ref, mg_ref, z_ref, *xn_ref):
        pa = _dot(ya_ref[...], paw_ref[...])
        pb = _dot(yb_ref[...], pbw_ref[...])
        pa_ref[...] = pa.astype(BF16)
        pb_ref[...] = pb.astype(BF16)
        everything = slice(None)
        merged = _sigmoid(_cols(hr_ref, everything, 0, D)) * pa + _sigmoid(_cols(hr_ref, everything, D, D)) * pb
        mb = merged.astype(BF16)
        mg_ref[...] = mb
        z = ALPHA * x_ref[...] + (_dot(mb, wo_ref[...]) + bo_ref[...])
        z_ref[...] = z
        if not last:
            zhat, _ = _ln_stats(z)
            xn_ref[0][...] = zhat * g_ref[...] + b_ref[...]

    row = lambda w: pl.BlockSpec((TM, w), lambda i: (i, 0))
    vec = pl.BlockSpec((1, D), _CONST2)
    n_f32 = 1 if last else 2
    return pl.pallas_call(
        body, name=name, grid=(t // TM,),
        in_specs=[row(D), row(Q_W), row(SGU_W), row(R_W),
                  pl.BlockSpec((Q_W, D), _CONST2), pl.BlockSpec((SGU_W, D), _CONST2), pl.BlockSpec((D, D), _CONST2),
                  vec, vec, vec],
        out_specs=[row(D)] * (3 + n_f32),
        out_shape=[jax.ShapeDtypeStruct((t, D), BF16)] * 3 + [jax.ShapeDtypeStruct((t, D), F32)] * n_f32,
        compiler_params=_ARB,
    )(x, ya, yb, hr, pa_w, pb_w, wo, b_out, ln_g, ln_b)


def _tail_bwd(dxn, z, pa, pb, hr, wo, pa_w, pb_w, ln_g, ln_b, name, from_loss):
    t = dxn.shape[0]

    def body(dxn_ref, z_ref, pa_ref, pb_ref, hr_ref, wo_ref, paw_ref, pbw_ref, g_ref, b_ref,
             dz_ref, dpa_ref, dpb_ref, dhr_ref, dya_ref, dyb_ref, acc_ref, gbr_ref):
        @pl.when(pl.program_id(0) == 0)
        def _():
            acc_ref[...] = jnp.zeros_like(acc_ref)
            gbr_ref[...] = jnp.zeros_like(gbr_ref)

        zhat, rstd = _ln_stats(z_ref[...])
        if from_loss:
            err = zhat * g_ref[...] + b_ref[...] - dxn_ref[...]
            dxn_v = err * (1.0 / D)
            sq = jnp.sum(jnp.sum(err * err, axis=1, keepdims=True), axis=0, keepdims=True)
            acc_ref[3:4, :] += jnp.broadcast_to(sq, (1, D))
        else:
            dxn_v = dxn_ref[...]
        dz = _ln_bwd(dxn_v * g_ref[...], zhat, rstd)
        dz_ref[...] = dz
        acc_ref[0:1, :] += _colsum(dxn_v * zhat)
        acc_ref[1:2, :] += _colsum(dxn_v)
        acc_ref[2:3, :] += _colsum(dz)
        dmerged = _dot_nt(dz.astype(BF16), wo_ref[...])
        everything = slice(None)
        sa = _sigmoid(_cols(hr_ref, everything, 0, D))
        sb = _sigmoid(_cols(hr_ref, everything, D, D))
        dpa = (dmerged * sa).astype(BF16)
        dpb = (dmerged * sb).astype(BF16)
        dpa_ref[...] = dpa
        dpb_ref[...] = dpb
        dra = dmerged * pa_ref[...].astype(F32) * (sa * (1.0 - sa))
        drb = dmerged * pb_ref[...].astype(F32) * (sb * (1.0 - sb))
        dhr_ref[:, 0:D] = dra.astype(BF16)
        dhr_ref[:, D:2 * D] = drb.astype(BF16)
        gbr_ref[0:1, 0:D] += _colsum(dra)
        gbr_ref[0:1, D:2 * D] += _colsum(drb)
        dya_ref[...] = _dot_nt(dpa, paw_ref[...]).astype(BF16)
        dyb_ref[...] = _dot_nt(dpb, pbw_ref[...]).astype(BF16)

    row = lambda w: pl.BlockSpec((TM, w), lambda i: (i, 0))
    vec = pl.BlockSpec((1, D), _CONST2)
    return pl.pallas_call(
        body, name=name, grid=(t // TM,),
        in_specs=[row(D), row(D), row(D), row(D), row(R_W),
                  pl.BlockSpec((D, D), _CONST2), pl.BlockSpec((Q_W, D), _CONST2), pl.BlockSpec((SGU_W, D), _CONST2),
                  vec, vec],
        out_specs=[row(D), row(D), row(D), row(R_W), row(Q_W), row(SGU_W), pl.BlockSpec((8, D), _CONST2),
                   pl.BlockSpec((8, R_W), _CONST2)],
        out_shape=[jax.ShapeDtypeStruct((t, D), F32), jax.ShapeDtypeStruct((t, D), BF16),
                   jax.ShapeDtypeStruct((t, D), BF16), jax.ShapeDtypeStruct((t, R_W), BF16),
                   jax.ShapeDtypeStruct((t, Q_W), BF16), jax.ShapeDtypeStruct((t, SGU_W), BF16),
                   jax.ShapeDtypeStruct((8, D), F32), jax.ShapeDtypeStruct((8, R_W), F32)],
        compiler_params=_ARB,
    )(dxn, z, pa, pb, hr, wo, pa_w, pb_w, ln_g, ln_b)


def _mixer_bwd(hm, dya, dyb, sinks, bias, vn_g, vn_b, w_s, bsb, nblk_seq, name):
    t = hm.shape[0]
    nt = t // TM

    def body(sinks_ref, hm_ref, hprev_ref, bias_ref, dya_ref, dyb_ref, vng_ref, vnb_ref, ws_ref, bsb_ref,
             dhm_ref, gbm_ref, gsk_ref, gvn_ref, gws_ref, gbs_ref, dk_carry, dv_carry):
        gi = pl.program_id(0)
        i = nt - 1 - gi

        @pl.when(gi == 0)
        def _():
            for r in (gbm_ref, gsk_ref, gvn_ref, gws_ref, gbs_ref, dk_carry, dv_carry):
                r[...] = jnp.zeros_like(r)

        tril = _tril_mask()
        wt = [jnp.where(tril, ws_ref[g], 0.0).astype(BF16) for g in range(4)]
        vng = vng_ref[...]

        def put(rows, col, val):
            dhm_ref[rows, col:col + val.shape[1]] = val.astype(BF16)
            gbm_ref[0:1, col:col + val.shape[1]] += _colsum(val)

        for s in reversed(range(NB)):
            r0 = s * BLK
            rows = slice(r0, r0 + BLK)
            bias = bias_ref[jnp.where((i * NB + s) % nblk_seq == 0, 1, 0)]
            q = _cols(hm_ref, rows, C_Q, Q_W)
            kband = _band(hm_ref, hprev_ref, s, C_K)
            vband = _band(hm_ref, hprev_ref, s, C_V)
            g_a = _cols(hm_ref, rows, C_GA, Q_W)
            sg = _sigmoid(g_a)
            dya_v = _cols(dya_ref, rows, 0, Q_W)
            d_o = dya_v * (g_a * sg)
            o_pairs, dq_pairs = [None] * 4, [None] * 4
            dkband = jnp.zeros((2 * BLK, KV_W), F32)
            dvband = jnp.zeros((2 * BLK, KV_W), F32)
            kband_t, vband_t = kband.T, vband.T
            for hk in range(N_KV):
                q4, kh, prob_t, p_sink, o_t = _attn_group(q, kband, vband_t, hk, sinks_ref, bias)
                o_pairs = _unstack_heads(o_t.T, hk, o_pairs)
                parts = []
                for g in range(4):
                    _, p, pos = _head_place(hk, g)
                    dp = d_o[:, BLK * p:BLK * (p + 1)]
                    parts.append(pltpu.roll(dp, 64, 1) if pos != hk else dp)
                do4 = _head_lanes(jnp.concatenate(parts, axis=0), hk)
                do4b = do4.astype(BF16)
                delta = _colsum(do4.T * o_t)
                vh = _head_lanes(vband, hk).astype(BF16)
                ds_t = prob_t * (_dot_nt(vh, do4b) - delta)
                dsb = ds_t.astype(BF16)
                dq4_t = _dot(_head_rows(kband_t, hk), dsb)
                dq_pairs = _unstack_heads(dq4_t.T * ATTN_SCALE, hk, dq_pairs)
                dkband = dkband + _head_lanes(_dot(dsb, q4), hk)
                dvband = dvband + _dot(prob_t.astype(BF16), do4b)
                dsk = p_sink * delta
                for g in range(4):
                    j = 4 * hk + g
                    tot = jnp.sum(dsk[:, BLK * g:BLK * (g + 1)], axis=1, keepdims=True)
                    gsk_ref[j:j + 1, :] += jnp.broadcast_to(-tot, (1, 128))
            attn = jnp.concatenate(o_pairs, axis=1)
            put(rows, C_Q, jnp.concatenate(dq_pairs, axis=1))
            put(rows, C_K, dkband[BLK:2 * BLK] + dk_carry[...])
            put(rows, C_V, dvband[BLK:2 * BLK] + dv_carry[...])
            dk_carry[...] = dkband[0:BLK]
            dv_carry[...] = dvband[0:BLK]
            put(rows, C_GA, dya_v * attn * (sg * (1.0 + g_a * (1.0 - sg))))
            u_b = _cols(hm_ref, rows, C_UB, SGU_W)
            v_b = _cols(hm_ref, rows, C_VB, SGU_W)
            g_b = _cols(hm_ref, rows, C_GB, SGU_W)
            u, vhat, rstd, vn, mixed = _sgu_fwd(u_b, v_b, vng, vnb_ref[...], wt, bsb_ref)
            sgb = _sigmoid(g_b)
            silu_b = g_b * sgb
            dyb_v = _cols(dyb_ref, rows, 0, SGU_W)
            du = dyb_v * mixed * silu_b
            dmixed = dyb_v * u * silu_b
            put(rows, C_GB, dyb_v * u * mixed * (sgb * (1.0 + g_b * (1.0 - sgb))))
            dvn_parts = []
            for g in range(4):
                cols = slice(BLK * g, BLK * (g + 1))
                dmg = dmixed[:, cols]
                dmgb = dmg.astype(BF16)
                dvn_parts.append(_dot_tn(wt[g], dmgb))
                gws_ref[g] += jnp.where(tril, _dot_nt(dmgb, vn[:, cols].astype(BF16)), 0.0)
                gbs_ref[g] += dmg
            dvn = jnp.concatenate(dvn_parts, axis=1)
            gvn_ref[0:1, :] += _colsum(dvn * vhat)
            gvn_ref[1:2, :] += _colsum(dvn)
            dv = _ln_bwd(dvn * vng, vhat, rstd)
            put(rows, C_UB, du * _dgelu(u_b))
            put(rows, C_VB, dv * _dgelu(v_b))

        @pl.when(gi == nt - 1)
        def _():
            for g in range(4):
                gbs_ref[g] = jnp.broadcast_to(jnp.sum(gbs_ref[g], axis=1, keepdims=True), (BLK, BLK))

    row = lambda w: pl.BlockSpec((TM, w), lambda g: (nt - 1 - g, 0))
    return pl.pallas_call(
        body, name=name, grid=(nt,),
        in_specs=_mixer_in_specs(nt, True) + [
            row(Q_W), row(SGU_W),
            pl.BlockSpec((1, SGU_W), _CONST2), pl.BlockSpec((1, SGU_W), _CONST2),
            pl.BlockSpec((4, BLK, BLK), _CONST3), pl.BlockSpec((4, BLK, BLK), _CONST3)],
        out_specs=[row(MAIN_W), pl.BlockSpec((8, MAIN_W), _CONST2), pl.BlockSpec((8, 128), _CONST2),
                   pl.BlockSpec((8, SGU_W), _CONST2), pl.BlockSpec((4, BLK, BLK), _CONST3),
                   pl.BlockSpec((4, BLK, BLK), _CONST3)],
        out_shape=[jax.ShapeDtypeStruct((t, MAIN_W), BF16), jax.ShapeDtypeStruct((8, MAIN_W), F32),
                   jax.ShapeDtypeStruct((8, 128), F32), jax.ShapeDtypeStruct((8, SGU_W), F32),
                   jax.ShapeDtypeStruct((4, BLK, BLK), F32), jax.ShapeDtypeStruct((4, BLK, BLK), F32)],
        scratch_shapes=[pltpu.VMEM((BLK, KV_W), F32), pltpu.VMEM((BLK, KV_W), F32)],
        compiler_params=_ARB,
    )(sinks, hm, hm, bias, dya, dyb, vn_g, vn_b, w_s, bsb)


def _dx_inproj(dz, dhm, dhr, w_t, after, name):
    t = dz.shape[0]

    def body(dz_ref, dhm_ref, dhr_ref, wt_ref, after_ref, dx_ref):
        dx_ref[...] = (ALPHA * dz_ref[...] + after_ref[0:1, 0:1] + _dot(dhm_ref[...], wt_ref[0:MAIN_W, :])
                       + _dot(dhr_ref[...], wt_ref[MAIN_W:IN_COLS, :]))

    row = lambda w: pl.BlockSpec((TM, w), lambda i: (i, 0))
    return pl.pallas_call(
        body, name=name, grid=(t // TM,),
        in_specs=[row(D), row(MAIN_W), row(R_W), pl.BlockSpec((IN_COLS, D), _CONST2),
                  pl.BlockSpec((8, 128), _CONST2)],
        out_specs=row(D), out_shape=jax.ShapeDtypeStruct((t, D), F32), compiler_params=_ARB,
    )(dz, dhm, dhr, w_t, after)


def _wgrad(a, b, tm, name, tk=1024):
    t, m = a.shape
    n = b.shape[1]
    nk = t // tk

    def body(a_ref, b_ref, o_ref, acc_ref):
        k = pl.program_id(1)

        @pl.when(k == 0)
        def _():
            acc_ref[...] = jnp.zeros_like(acc_ref)

        acc_ref[...] += _dot_tn(a_ref[...].astype(BF16), b_ref[...].astype(BF16))

        @pl.when(k == nk - 1)
        def _():
            o_ref[...] = acc_ref[...].astype(BF16)

    return pl.pallas_call(
        body, name=name, grid=(m // tm, nk),
        in_specs=[pl.BlockSpec((tk, tm), lambda j, k: (k, j)), pl.BlockSpec((tk, n), lambda j, k: (k, 0))],
        out_specs=pl.BlockSpec((tm, n), lambda j, k: (j, 0)),
        out_shape=jax.ShapeDtypeStruct((m, n), BF16),
        scratch_shapes=[pltpu.VMEM((tm, n), F32)],
        compiler_params=pltpu.CompilerParams(dimension_semantics=("arbitrary", "arbitrary"), vmem_limit_bytes=VMEM_LIMIT),
    )(a, b)


def _ln_in_bwd(dx0, x, g, name):
    t = x.shape[0]

    def body(dx0_ref, x_ref, g_ref, gx_ref, acc_ref):
        @pl.when(pl.program_id(0) == 0)
        def _():
            acc_ref[...] = jnp.zeros_like(acc_ref)

        d = dx0_ref[...]
        xhat, rstd = _ln_stats(x_ref[...])
        gx_ref[...] = _ln_bwd(d * g_ref[...], xhat, rstd)
        acc_ref[0:1, :] += _colsum(d * xhat)
        acc_ref[1:2, :] += _colsum(d)

    row = pl.BlockSpec((TM_EW, D), lambda i: (i, 0))
    return pl.pallas_call(
        body, name=name, grid=(t // TM_EW,),
        in_specs=[row, row, pl.BlockSpec((1, D), _CONST2)],
        out_specs=[row, pl.BlockSpec((8, D), _CONST2)],
        out_shape=[jax.ShapeDtypeStruct((t, D), F32), jax.ShapeDtypeStruct((8, D), F32)],
        compiler_params=_ARB,
    )(dx0, x, g)


_ANY = pl.BlockSpec(memory_space=pl.ANY)


def _place():
    return lax.axis_index("x"), lax.axis_index("y"), lax.axis_index("c")


def _allgather8(xs, name):
    n = len(xs)

    def body(*refs):
        x_refs, o_refs = refs[:n], refs[n:2 * n]
        send_sems, recv_sems, local_sems = refs[2 * n:]
        x, y, c = _place()
        me, sib = (x, y, c), (x, y, 1 - c)
        chips = [(1 - x, y), (x, 1 - y), (1 - x, 1 - y)]

        def copy(a, k, block, to, src=None):
            dst = o_refs[a].at[4 * block[0] + 2 * block[1] + block[2]]
            return pltpu.make_async_remote_copy(
                src_ref=dst if src is None else src, dst_ref=dst, send_sem=send_sems.at[7 * a + k],
                recv_sem=recv_sems.at[7 * a + k], device_id=to, device_id_type=MESH)

        mine = [pltpu.make_async_copy(x_refs[a], o_refs[a].at[4 * x + 2 * y + c], local_sems.at[a]) for a in range(n)]
        for cp in mine:
            cp.start()
        sent = []
        for a in range(n):
            sent.append(copy(a, 0, me, sib, src=x_refs[a]))
            sent += [copy(a, 1 + j, me, (*chip, c), src=x_refs[a]) for j, chip in enumerate(chips)]
        for cp in sent:
            cp.start()
        for j, chip in enumerate(chips):
            for a in range(n):
                copy(a, 1 + j, (*chip, c), me).wait_recv()
                fwd = copy(a, 4 + j, (*chip, c), sib)
                fwd.start()
                sent.append(fwd)
        for a in range(n):
            copy(a, 0, sib, me).wait_recv()
            for j, chip in enumerate(chips):
                copy(a, 4 + j, (*chip, 1 - c), me).wait_recv()
        for cp in sent:
            cp.wait_send()
        for cp in mine:
            cp.wait()

    return pl.pallas_call(
        body, name=name, in_specs=[_ANY] * n, out_specs=[_ANY] * n,
        out_shape=[jax.ShapeDtypeStruct((N_DEV,) + v.shape, v.dtype) for v in xs],
        scratch_shapes=[pltpu.SemaphoreType.DMA((7 * n,)), pltpu.SemaphoreType.DMA((7 * n,)),
                        pltpu.SemaphoreType.DMA((n,))],
    )(*xs)


def _forward_sibling(lands, name):
    n = len(lands)

    def body(*refs):
        l_refs = refs[n:2 * n]
        send_sems, recv_sems = refs[2 * n:]
        x, y, c = _place()
        chips = [(1 - x, y), (x, 1 - y), (1 - x, 1 - y)]

        def copy(a, j, core):
            rows = l_refs[a].at[4 * chips[j][0] + 2 * chips[j][1] + core]
            return pltpu.make_async_remote_copy(
                src_ref=rows, dst_ref=rows, send_sem=send_sems.at[3 * a + j], recv_sem=recv_sems.at[3 * a + j],
                device_id=(x, y, 1 - c), device_id_type=MESH)

        for a in range(n):
            for j in range(3):
                copy(a, j, c).start()
        for a in range(n):
            for j in range(3):
                copy(a, j, 1 - c).wait_recv()
                copy(a, j, c).wait_send()

    return pl.pallas_call(
        body, name=name, in_specs=[_ANY] * n, out_specs=[_ANY] * n,
        out_shape=[jax.ShapeDtypeStruct(v.shape, v.dtype) for v in lands],
        input_output_aliases={a: a for a in range(n)},
        scratch_shapes=[pltpu.SemaphoreType.DMA((3 * n,)), pltpu.SemaphoreType.DMA((3 * n,))],
    )(*lands)


def _swap_sibling(gs, name):
    n = len(gs)
    first = [0]
    for v in gs:
        first.append(first[-1] + v.shape[0])

    def body(*refs):
        g_refs, r_refs = refs[:n], refs[n:2 * n]
        send_sems, recv_sems = refs[2 * n:]
        x, y, c = _place()
        cps = [pltpu.make_async_remote_copy(
            src_ref=g_refs[a].at[q, 1 - c], dst_ref=r_refs[a].at[q], send_sem=send_sems.at[first[a] + q],
            recv_sem=recv_sems.at[first[a] + q], device_id=(x, y, 1 - c), device_id_type=MESH)
            for a in range(n) for q in range(gs[a].shape[0])]
        for cp in cps:
            cp.start()
        for cp in cps:
            cp.wait()

    return pl.pallas_call(
        body, name=name, in_specs=[_ANY] * n, out_specs=[_ANY] * n,
        out_shape=[jax.ShapeDtypeStruct(v.shape[:1] + v.shape[2:], v.dtype) for v in gs],
        scratch_shapes=[pltpu.SemaphoreType.DMA((first[-1],)), pltpu.SemaphoreType.DMA((first[-1],))],
    )(*gs)


def _row_tile(rows, lanes, cap):
    if rows * lanes * 4 <= (1 << 20):
        return rows
    return max(d for d in range(8, cap + 1, 8) if rows % d == 0 and (d % 16 == 0 or rows % 16 != 0))


def _pair_sum(g, r, name):
    n, _, rows, l = g.shape
    tr = _row_tile(rows, l, 608)

    def body(c_ref, g_ref, r_ref, o_ref):
        o_ref[...] = (g_ref[0].astype(F32) + r_ref[...].astype(F32)).astype(o_ref.dtype)

    return pl.pallas_call(
        body, name=name,
        grid_spec=pltpu.PrefetchScalarGridSpec(
            num_scalar_prefetch=1, grid=(n, rows // tr),
            in_specs=[pl.BlockSpec((1, 1, tr, l), lambda q, i, c_ref: (q, c_ref[0], i, 0)),
                      pl.BlockSpec((1, tr, l), lambda q, i, c_ref: (q, i, 0))],
            out_specs=pl.BlockSpec((1, tr, l), lambda q, i, c_ref: (q, i, 0))),
        out_shape=jax.ShapeDtypeStruct((n, rows, l), g.dtype),
        compiler_params=pltpu.CompilerParams(dimension_semantics=("arbitrary", "arbitrary"), vmem_limit_bytes=VMEM_LIMIT),
    )(lax.axis_index("c").astype(jnp.int32).reshape(1), g, r)


def _adamw(parts, w, m, v, name, own=None):
    nl = len(parts)
    ns, rows, l = parts[0].shape
    tr = _row_tile(rows, l * ns, 304)
    nt = rows // tr
    c1 = 1.0 - ADAM_B1 ** ADAM_STEP
    c2 = 1.0 - ADAM_B2 ** ADAM_STEP

    def body(q_ref, *refs):
        own_refs = refs[:nl] if own is not None else None
        p_refs = refs[-7 - nl:-7]
        w_ref, m_ref, v_ref, g_ref, d_ref, nm_ref, nv_ref = refs[-7:]
        layer = pl.program_id(0)
        g = None
        for j in range(nl):
            gj = None
            for k in range(ns):
                term = p_refs[j][k].astype(F32)
                if own_refs is not None:
                    term = jnp.where(q_ref[0] == k, own_refs[j][0].astype(F32), term)
                gj = term if gj is None else gj + term
            g = gj if g is None else jnp.where(layer == j, gj, g)
        g_ref[...] = g
        nm = ADAM_B1 * m_ref[...] + (1.0 - ADAM_B1) * g
        nv = ADAM_B2 * v_ref[...] + (1.0 - ADAM_B2) * (g * g)
        nm_ref[...] = nm
        nv_ref[...] = nv
        d_ref[...] = -ADAM_LR * ((nm / c1) / (jnp.sqrt(nv / c2) + ADAM_EPS) + ADAM_WD * w_ref[...])

    def tile_of(j):
        return lambda la, i, q: jnp.where(la == j, i, jnp.where(la < j, 0, nt - 1))

    row = pl.BlockSpec((tr, l), lambda la, i, q: (la * nt + i, 0))
    own_specs = [] if own is None else [
        pl.BlockSpec((1, tr, l), lambda la, i, q, j=j: (q[0], tile_of(j)(la, i, q), 0)) for j in range(nl)]
    part_specs = [pl.BlockSpec((ns, tr, l), lambda la, i, q, j=j: (0, tile_of(j)(la, i, q), 0)) for j in range(nl)]
    chip = (2 * lax.axis_index("x") + lax.axis_index("y")).astype(jnp.int32).reshape(1)
    return pl.pallas_call(
        body, name=name,
        grid_spec=pltpu.PrefetchScalarGridSpec(
            num_scalar_prefetch=1, grid=(nl, nt),
            in_specs=own_specs + part_specs + [row, row, row], out_specs=[row] * 4),
        out_shape=[jax.ShapeDtypeStruct((nl * rows, l), F32)] * 4,
        compiler_params=pltpu.CompilerParams(dimension_semantics=("arbitrary", "arbitrary"), vmem_limit_bytes=VMEM_LIMIT),
    )(chip, *([] if own is None else own), *parts, w, m, v)


_HBM = pl.BlockSpec(memory_space=pltpu.HBM)
_SEM = pl.BlockSpec(memory_space=pltpu.SEMAPHORE)
_EFFECT = pltpu.SideEffectType.DATAFLOW_SIDE_EFFECTING


def _plan_all(x, y, c):
    me = 4 * x + 2 * y + c
    peers = [(x, y, 1 - c), (1 - x, y, c), (x, 1 - y, c), (1 - x, 1 - y, c),
             (1 - x, y, 1 - c), (x, 1 - y, 1 - c), (1 - x, 1 - y, 1 - c)]
    return [(None, me, p, 4 * p[0] + 2 * p[1] + p[2]) for p in peers]


def _plan_near(x, y, c):
    me = 4 * x + 2 * y + c
    peers = [(x, y, 1 - c), (1 - x, y, c), (x, 1 - y, c), (1 - x, 1 - y, c)]
    return [(None, me, p, 4 * p[0] + 2 * p[1] + p[2]) for p in peers]


def _plan_chips(x, y, c):
    me = 2 * x + y
    return [(2 * qx + qy, me, (qx, qy, c), 2 * qx + qy) for qx, qy in ((1 - x, y), (x, 1 - y), (1 - x, 1 - y))]


def _split_copies(plan, src_refs, land_refs, send_sems, recv_sems, arrival):
    n = len(src_refs)
    entries = plan(*_place())
    per = len(entries)
    cps = []
    for a in range(n):
        for k, (src_slot, dst_slot, peer, back_slot) in enumerate(entries):
            src = src_refs[a] if src_slot is None else src_refs[a].at[src_slot]
            cps.append(pltpu.make_async_remote_copy(
                src_ref=src, dst_ref=land_refs[a].at[back_slot if arrival else dst_slot],
                send_sem=send_sems.at[per * a + k], recv_sem=recv_sems.at[per * a + k],
                device_id=peer, device_id_type=MESH))
    return cps


def _split_start(srcs, lands, plan, per, name):
    n = len(srcs)

    def body(*refs):
        for cp in _split_copies(plan, refs[:n], refs[n:2 * n], refs[2 * n], refs[2 * n + 1], False):
            cp.start()
        refs[-1][...] = jnp.zeros_like(refs[-1])

    both = list(srcs) + list(lands)
    outs = pl.pallas_call(
        body, name=name,
        out_shape=(pltpu.SemaphoreType.DMA((per * n,)), pltpu.SemaphoreType.DMA((per * n,)),
                   *[pltpu.HBM(v.shape, v.dtype) for v in both], jax.ShapeDtypeStruct((8, 128), F32)),
        in_specs=[_HBM] * (2 * n),
        out_specs=(_SEM, _SEM, *[_HBM] * (2 * n), pl.BlockSpec(memory_space=pltpu.VMEM)),
        input_output_aliases={i: 2 + i for i in range(2 * n)},
        compiler_params=pltpu.CompilerParams(has_side_effects=_EFFECT),
    )(*[pltpu.with_memory_space_constraint(v, pltpu.HBM) for v in both])
    return outs[0], outs[1], list(outs[2:2 + 2 * n]), outs[-1]


def _split_wait(send_sems, recv_sems, thru, plan, after, name):
    n = len(thru) // 2

    def body(*refs):
        for cp in _split_copies(plan, refs[:n], refs[n:2 * n], refs[2 * n], refs[2 * n + 1], True):
            cp.wait_send()
            cp.wait_recv()

    outs = pl.pallas_call(
        body, name=name, out_shape=tuple(pltpu.HBM(v.shape, v.dtype) for v in thru),
        in_specs=[_HBM] * (2 * n) + [_SEM, _SEM, pl.BlockSpec(memory_space=pl.ANY)],
        out_specs=[_HBM] * (2 * n), input_output_aliases={i: i for i in range(2 * n)},
        compiler_params=pltpu.CompilerParams(has_side_effects=_EFFECT),
    )(*thru, send_sems, recv_sems, after)
    return list(outs[:n]), list(outs[n:])


_SMALL_IN = ("ln_in_g", "ln_in_b")
_SMALL = ("b_in", "sinks", "vn_g", "vn_b", "w_s", "b_s", "b_out", "ln_g", "ln_b")


def _rows128(a):
    flat = a.reshape(-1)
    return jnp.pad(flat, (0, (-flat.shape[0]) % 128)).reshape(-1, 128)


def _pack_small(d, names):
    rows = jnp.concatenate([_rows128(d[n]) for n in names])
    return jnp.pad(rows, ((0, (-rows.shape[0]) % 8), (0, 0)))


def _unpack_small(p, like, names):
    off, out = 0, {}
    for n in names:
        size = like[n].size
        rows = -(-size // 128)
        out[n] = p[off:off + rows].reshape(-1)[:size].reshape(like[n].shape)
        off += rows
    return out


def _owner_blocks(g, axis):
    sh = g.shape
    g = g.reshape(sh[:axis] + (4, 2, sh[axis] // N_DEV) + sh[axis + 1:])
    return jnp.moveaxis(g, (axis, axis + 1), (0, 1))


def kernel(x, ln_in_g, ln_in_b, w_in, b_in, sinks, vn_g, vn_b, w_s, b_s, p_a, p_b, w_out, b_out, ln_g, ln_b, loss_target, m_ln_in_g, m_ln_in_b, m_w_in, m_b_in, m_sinks, m_vn_g, m_vn_b, m_w_s, m_b_s, m_p_a, m_p_b, m_w_out, m_b_out, m_ln_g, m_ln_b, v_ln_in_g, v_ln_in_b, v_w_in, v_b_in, v_sinks, v_vn_g, v_vn_b, v_w_s, v_b_s, v_p_a, v_p_b, v_w_out, v_b_out, v_ln_g, v_ln_b):
    nseq, seq, _ = x.shape
    t = nseq * seq
    nblk_seq = seq // BLK
    x2 = x.reshape(t, D)
    tgt = loss_target.reshape(t, D)

    def turned(a):
        return jnp.swapaxes(a, 1, 2)

    w_in_t = turned(w_in)

    def blocks(l):
        return [w_in_t[l].astype(BF16), p_a[l].astype(BF16), p_b[l].astype(BF16), w_out[l].astype(BF16)]

    def full_weights(g):
        w_t_full = g[0].reshape(IN_COLS, D)
        pa_full = jnp.moveaxis(g[1], 0, 1).reshape(Q_W, D)
        pb_full = jnp.moveaxis(g[2], 0, 1).reshape(SGU_W, D)
        wo_full = g[3].reshape(D, D)
        return dict(w_t=w_t_full, pa=pa_full, pb=pb_full, wo=wo_full)

    def landing(bs):
        return [lax.empty((N_DEV,) + v.shape, v.dtype) for v in bs]

    def with_own(landed, sent):
        return [lax.dynamic_update_index_in_dim(g, b, me, 0) for g, b in zip(landed, sent)]

    me = 4 * lax.axis_index("x") + 2 * lax.axis_index("y") + lax.axis_index("c")
    blocks0 = blocks(0)
    ag_send, ag_recv, ag_thru, ag_token = _split_start(blocks0, landing(blocks0), _plan_near, 4,
                                                       "allgather_weights0_start")
    xs = [_ln_fwd(x2, ln_in_g + ag_token[0, 0], ln_in_b, "ln_in_fwd")]
    sent, landed = _split_wait(ag_send, ag_recv, ag_thru, _plan_near, xs[0], "allgather_weights0_wait")
    gathered0 = with_own(_forward_sibling(landed, "allgather_weights0_forward"), sent)
    blocks1, gathered0 = lax.optimization_barrier((blocks(1), gathered0))
    ag_send, ag_recv, ag_thru, ag_token = _split_start(blocks1, landing(blocks1), _plan_all, 7,
                                                       "allgather_weights1_start")
    weights = [full_weights(gathered0), None]
    bsb = jnp.broadcast_to(b_s[:, :, :, None], (DEPTH, 4, BLK, BLK))
    bias = _band_bias()

    saved = []
    for l in range(DEPTH):
        if l == 1:
            sent, landed = _split_wait(ag_send, ag_recv, ag_thru, _plan_all, xs[1], "allgather_weights1_wait")
            weights[1] = full_weights(with_own(landed, sent))
        wl = weights[l]
        last = l == DEPTH - 1
        b_l = b_in[l].reshape(1, -1) + (ag_token[0, 0] if l == 0 else 0.0)
        hm, hr = _inproj(xs[l], wl["w_t"], b_l, f"inproj{l}")
        ya, yb = _mixer_fwd(hm, sinks[l], bias, vn_g[l].reshape(1, -1), vn_b[l].reshape(1, -1), w_s[l], bsb[l],
                            nblk_seq, f"mixer_fwd{l}")
        outs = _tail_fwd(xs[l], ya, yb, hr, wl["pa"], wl["pb"], wl["wo"], b_out[l].reshape(1, D),
                         ln_g[l].reshape(1, D), ln_b[l].reshape(1, D), f"tail_fwd{l}", last)
        saved.append((hm, hr, ya, yb) + tuple(outs[:4]))
        if not last:
            xs.append(outs[4])

    small = {n: [None] * DEPTH for n in _SMALL}
    names = ("w_in", "p_a", "p_b", "w_out")
    owner_axis = {"w_in": 0, "p_a": 1, "p_b": 1, "w_out": 0}
    token = jnp.zeros((8, 128), F32)
    dx = tgt
    split = [None] * DEPTH
    for l in reversed(range(DEPTH)):
        hm, hr, ya, yb, pa, pb, merged, z = saved[l]
        wl = weights[l]
        dz, dpa, dpb, dhr, dya, dyb, acc, gbr = _tail_bwd(
            dx, z, pa, pb, hr, wl["wo"], wl["pa"], wl["pb"], ln_g[l].reshape(1, D) + token[0, 0],
            ln_b[l].reshape(1, D), f"tail_bwd{l}", l == DEPTH - 1)
        if l == DEPTH - 1:
            loss = lax.psum(acc[3, 0] * (0.5 / D), ("x", "y", "c"))
        dhm, gbm, gsk, gvn, gws, gbs = _mixer_bwd(
            hm, dya, dyb, sinks[l], bias, vn_g[l].reshape(1, -1), vn_b[l].reshape(1, -1), w_s[l], bsb[l],
            nblk_seq, f"mixer_bwd{l}")
        grads = {"w_in": jnp.concatenate([_wgrad(dhm, xs[l], MAIN_W // 2, f"wgrad_in_main{l}"),
                                          _wgrad(dhr, xs[l], R_W // 2, f"wgrad_in_route{l}")], axis=0),
                 "p_a": _wgrad(ya, dpa, Q_W, f"wgrad_pa{l}"), "p_b": _wgrad(yb, dpb, SGU_W, f"wgrad_pb{l}"),
                 "w_out": _wgrad(merged, dz, D, f"wgrad_out{l}")}
        small["b_in"][l] = jnp.concatenate([gbm[0], gbr[0]])
        small["sinks"][l] = gsk[:, 0]
        small["vn_g"][l], small["vn_b"][l] = gvn[0], gvn[1]
        small["w_s"][l], small["b_s"][l] = gws, gbs[:, :, 0]
        small["ln_g"][l], small["ln_b"][l], small["b_out"][l] = acc[0], acc[1], acc[2]
        parts = [_owner_blocks(grads[n], owner_axis[n]) for n in names]
        if l == 0:
            packed = _pack_small({n: jnp.stack(v) for n, v in small.items()}, _SMALL)
            parts.append(jnp.broadcast_to(packed[None, None], (1, 2) + packed.shape))
        from_sib = _swap_sibling(parts, f"rs_sibling{l}")
        pair = [_pair_sum(g, r, f"pair_sum{l}_{a}") for a, (g, r) in enumerate(zip(parts, from_sib))]
        if l == 0:
            pair[4] = jnp.broadcast_to(pair[4], (4,) + packed.shape)
        lands = [jnp.zeros(p.shape, p.dtype) for p in pair]
        split[l] = _split_start(pair, lands, _plan_chips, 3, f"rs_chips{l}_start")
        token = split[l][3]
        dx = _dx_inproj(dz, dhm, dhr, wl["w_t"], token, f"dx_inproj{l}")
    grad_x, acc_in = _ln_in_bwd(dx, x2, ln_in_g.reshape(1, D), "ln_in_bwd")
    (all_in,) = _allgather8([acc_in], "allgather_ln_in")

    given = {"w_in": (w_in_t, turned(m_w_in), turned(v_w_in)), "p_a": (p_a, m_p_a, v_p_a),
             "p_b": (p_b, m_p_b, v_p_b), "w_out": (w_out, m_w_out, v_w_out)}
    waited = [_split_wait(split[l][0], split[l][1], split[l][2], _plan_chips, all_in, f"rs_chips{l}_wait")
              for l in range(DEPTH)]
    res = {}
    for a, n in enumerate(names):
        rows, lanes = waited[0][1][a].shape[1:]
        outs = _adamw([waited[l][1][a] for l in range(DEPTH)], *[v.reshape(DEPTH * rows, lanes) for v in given[n]],
                      f"adamw_{n}", own=[waited[l][0][a] for l in range(DEPTH)])
        res[n] = [o.reshape(given[n][0].shape) for o in outs]
    res["w_in"] = [turned(o) for o in res["w_in"]]

    w_small = dict(ln_in_g=ln_in_g, ln_in_b=ln_in_b, b_in=b_in, sinks=sinks, vn_g=vn_g, vn_b=vn_b, w_s=w_s, b_s=b_s,
                   b_out=b_out, ln_g=ln_g, ln_b=ln_b)
    m_small = dict(ln_in_g=m_ln_in_g, ln_in_b=m_ln_in_b, b_in=m_b_in, sinks=m_sinks, vn_g=m_vn_g, vn_b=m_vn_b,
                   w_s=m_w_s, b_s=m_b_s, b_out=m_b_out, ln_g=m_ln_g, ln_b=m_ln_b)
    v_small = dict(ln_in_g=v_ln_in_g, ln_in_b=v_ln_in_b, b_in=v_b_in, sinks=v_sinks, vn_g=v_vn_g, vn_b=v_vn_b,
                   w_s=v_w_s, b_s=v_b_s, b_out=v_b_out, ln_g=v_ln_g, ln_b=v_ln_b)
    outs = _adamw([waited[0][1][4]], *[_pack_small(d, _SMALL) for d in (w_small, m_small, v_small)], "adamw_small",
                  own=[waited[0][0][4]])
    outs_in = _adamw([all_in], *[jnp.pad(jnp.stack([d[n] for n in _SMALL_IN]), ((0, 6), (0, 0)))
                                 for d in (w_small, m_small, v_small)], "adamw_ln_in")
    for k in range(4):
        u = _unpack_small(outs[k], w_small, _SMALL)
        u.update({n: outs_in[k][r] for r, n in enumerate(_SMALL_IN)})
        for n in u:
            res.setdefault(n, [None] * 4)[k] = u[n]

    order = ("ln_in_g", "ln_in_b", "w_in", "b_in", "sinks", "vn_g", "vn_b", "w_s", "b_s", "p_a", "p_b", "w_out",
             "b_out", "ln_g", "ln_b")
    return (loss, grad_x.reshape(x.shape), *[res[n][0] for n in order], *[res[n][1] for n in order],
            *[res[n][2] for n in order], *[res[n][3] for n in order])
```

```python
import jax
import jax.numpy as jnp
from jax import lax
from jax.experimental import pallas as pl
from jax.experimental.pallas import tpu as pltpu

F32 = jnp.float32
BF16 = jnp.bfloat16

D = 1024
BLK = 128
N_KV = 2
Q_W, KV_W, SGU_W = 512, 128, 512
C_Q, C_K, C_V, C_GA, C_UB, C_VB, C_GB = 0, 512, 640, 768, 1280, 1792, 2304
MAIN_W = 2816
R_W = 2048
IN_COLS = MAIN_W + R_W
N_DEV = 8
SHARD_COLS = IN_COLS // N_DEV

DEPTH = 2
ALPHA = (2.0 * DEPTH) ** 0.25
LN_EPS = 1e-5
ATTN_SCALE = 0.125
NEG = float(jnp.finfo(jnp.float32).min)

ADAM_LR, ADAM_B1, ADAM_B2, ADAM_EPS, ADAM_WD, ADAM_STEP = 0.001, 0.9, 0.999, 1e-08, 0.01, 10

TM = 256
TM_EW = 512
TM_MM = 512
NB = TM // BLK
MESH = pl.DeviceIdType.MESH
VMEM_LIMIT = 56 * 1024 * 1024

_ARB = pltpu.CompilerParams(dimension_semantics=("arbitrary",), vmem_limit_bytes=VMEM_LIMIT)


def _sigmoid(x):
    return 1.0 / (1.0 + jnp.exp(-x))


_GELU_C = 0.7978845608028654
_GELU_A = 0.044715


def _gelu(x):
    return 0.5 * x * (1.0 + jnp.tanh(_GELU_C * (x + _GELU_A * x * x * x)))


def _dgelu(x):
    t = jnp.tanh(_GELU_C * (x + _GELU_A * x * x * x))
    return 0.5 * (1.0 + t) + 0.5 * x * (1.0 - t * t) * (_GELU_C * (1.0 + 3.0 * _GELU_A * x * x))


def _ln_stats(x):
    mu = jnp.mean(x, axis=-1, keepdims=True)
    xc = x - mu
    var = jnp.mean(xc * xc, axis=-1, keepdims=True)
    rstd = lax.rsqrt(var + LN_EPS)
    return xc * rstd, rstd


def _ln_bwd(dy_g, xhat, rstd):
    m1 = jnp.mean(dy_g, axis=-1, keepdims=True)
    m2 = jnp.mean(dy_g * xhat, axis=-1, keepdims=True)
    return rstd * (dy_g - m1 - xhat * m2)


def _colsum(x):
    return jnp.sum(x, axis=0, keepdims=True)


def _dot(a, b):
    return jnp.dot(a, b, preferred_element_type=F32)


def _dot_nt(a, b):
    return lax.dot_general(a, b, (((1,), (1,)), ((), ())), preferred_element_type=F32)


def _dot_tn(a, b):
    return lax.dot_general(a, b, (((0,), (0,)), ((), ())), preferred_element_type=F32)


def _head_place(hk, g):
    j = 4 * hk + g
    return j, j // 2, j % 2


def _head_rows(x, hk):
    d = lax.broadcasted_iota(jnp.int32, x.shape, 0)
    return jnp.where((d >= 64 * hk) & (d < 64 * hk + 64), x, 0.0).astype(BF16)


def _head_lanes(x, hk):
    d = lax.broadcasted_iota(jnp.int32, x.shape, 1)
    return jnp.where((d >= 64 * hk) & (d < 64 * hk + 64), x, 0.0)


def _band_bias():
    kpos = lax.broadcasted_iota(jnp.int32, (2 * BLK, 4 * BLK), 0)
    row = lax.broadcasted_iota(jnp.int32, (2 * BLK, 4 * BLK), 1) & (BLK - 1)
    band = (kpos > row) & (kpos <= row + BLK)
    return jnp.stack([jnp.where(band, 0.0, NEG), jnp.where(band & (kpos >= BLK), 0.0, NEG)]).astype(F32)


def _attn_group(q, kband, vband_t, hk, sinks_ref, bias):
    kh = _head_lanes(kband, hk).astype(BF16)
    parts = []
    for g in range(4):
        _, p, pos = _head_place(hk, g)
        qp = q[:, BLK * p:BLK * (p + 1)] * ATTN_SCALE
        if pos != hk:
            qp = pltpu.roll(qp, 64, 1)
        parts.append(qp.astype(BF16))
    q4 = jnp.concatenate(parts, axis=0)
    s_t = _dot_nt(kh, q4) + bias
    sink_row = jnp.concatenate(
        [jnp.full((1, BLK), sinks_ref[4 * hk + g], F32) for g in range(4)], axis=1)
    m = jnp.maximum(jnp.max(s_t, axis=0, keepdims=True), sink_row)
    p_un = jnp.exp(s_t - m)
    e_sink = jnp.exp(sink_row - m)
    inv = 1.0 / (jnp.sum(p_un, axis=0, keepdims=True) + e_sink)
    prob_t = p_un * inv
    o_t = _dot(_head_rows(vband_t, hk), prob_t.astype(BF16))
    return q4, kh, prob_t, e_sink * inv, o_t


def _unstack_heads(x4, hk, pairs):
    for g in range(4):
        _, p, pos = _head_place(hk, g)
        xg = x4[BLK * g:BLK * (g + 1)]
        if pos != hk:
            xg = pltpu.roll(xg, 64, 1)
        pairs[p] = xg if pairs[p] is None else pairs[p] + xg
    return pairs


def _attn_fwd(q, kband, vband, sinks_ref, bias):
    pairs = [None] * 4
    vband_t = vband.T
    for hk in range(N_KV):
        o_t = _attn_group(q, kband, vband_t, hk, sinks_ref, bias)[-1]
        pairs = _unstack_heads(o_t.T, hk, pairs)
    return jnp.concatenate(pairs, axis=1)


def _tril_mask():
    r = lax.broadcasted_iota(jnp.int32, (BLK, BLK), 0)
    c = lax.broadcasted_iota(jnp.int32, (BLK, BLK), 1)
    return c <= r


def _sgu_fwd(u_b, v_b, vn_g, vn_b, wt, bsb_ref):
    u = _gelu(u_b)
    v = _gelu(v_b)
    vhat, rstd = _ln_stats(v)
    vn = vhat * vn_g + vn_b
    mixed = jnp.concatenate(
        [_dot(wt[g], vn[:, BLK * g:BLK * (g + 1)].astype(BF16)) + bsb_ref[g] for g in range(4)], axis=1)
    return u, vhat, rstd, vn, mixed


def _cols(ref, rows, col, width):
    return ref[rows, col:col + width].astype(F32)


def _band(hm_ref, hprev_ref, s, col):
    r0 = s * BLK
    cur = hm_ref[r0:r0 + BLK, col:col + KV_W]
    if s == 0:
        off = 0 if col == C_K else KV_W
        prev = hprev_ref[:, off:off + KV_W]
    else:
        prev = hm_ref[r0 - BLK:r0, col:col + KV_W]
    return jnp.concatenate([prev, cur], axis=0).astype(F32)


def _mixer_in_specs(nt, rev):
    def tile(g):
        return nt - 1 - g if rev else g

    return [
        pl.BlockSpec(memory_space=pltpu.SMEM),
        pl.BlockSpec((TM, MAIN_W), lambda g: (tile(g), 0)),
        pl.BlockSpec((BLK, 2 * KV_W), lambda g: (jnp.maximum(tile(g) * NB - 1, 0), 2)),
        pl.BlockSpec((2, 2 * BLK, 4 * BLK), lambda g: (0, 0, 0)),
    ]


_CONST2 = lambda g: (0, 0)
_CONST3 = lambda g: (0, 0, 0)


def _ln_fwd(x, g, b, name):
    t = x.shape[0]

    def body(x_ref, g_ref, b_ref, o_ref):
        xhat, _ = _ln_stats(x_ref[...])
        o_ref[...] = xhat * g_ref[...] + b_ref[...]

    return pl.pallas_call(
        body, name=name, grid=(t // TM_EW,),
        in_specs=[pl.BlockSpec((TM_EW, D), lambda i: (i, 0)), pl.BlockSpec((1, D), _CONST2),
                  pl.BlockSpec((1, D), _CONST2)],
        out_specs=pl.BlockSpec((TM_EW, D), lambda i: (i, 0)),
        out_shape=jax.ShapeDtypeStruct((t, D), F32), compiler_params=_ARB,
    )(x, g.reshape(1, D), b.reshape(1, D))


def _inproj(x, w_t, b, name):
    t = x.shape[0]

    def body(x_ref, wt_ref, b_ref, hm_ref, hr_ref):
        xb = x_ref[...].astype(BF16)
        hm_ref[...] = (_dot_nt(xb, wt_ref[0:MAIN_W, :]) + b_ref[:, 0:MAIN_W]).astype(BF16)
        hr_ref[...] = (_dot_nt(xb, wt_ref[MAIN_W:IN_COLS, :]) + b_ref[:, MAIN_W:IN_COLS]).astype(BF16)

    return pl.pallas_call(
        body, name=name, grid=(t // TM_MM,),
        in_specs=[pl.BlockSpec((TM_MM, D), lambda i: (i, 0)),
                  pl.BlockSpec((IN_COLS, D), _CONST2), pl.BlockSpec((1, IN_COLS), _CONST2)],
        out_specs=[pl.BlockSpec((TM_MM, MAIN_W), lambda i: (i, 0)), pl.BlockSpec((TM_MM, R_W), lambda i: (i, 0))],
        out_shape=[jax.ShapeDtypeStruct((t, MAIN_W), BF16), jax.ShapeDtypeStruct((t, R_W), BF16)],
        compiler_params=_ARB,
    )(x, w_t, b)


def _mixer_fwd(hm, sinks, bias, vn_g, vn_b, w_s, bsb, nblk_seq, name):
    t = hm.shape[0]
    nt = t // TM

    def body(sinks_ref, hm_ref, hprev_ref, bias_ref, vng_ref, vnb_ref, ws_ref, bsb_ref, ya_ref, yb_ref):
        i = pl.program_id(0)
        tril = _tril_mask()
        wt = [jnp.where(tril, ws_ref[g], 0.0).astype(BF16) for g in range(4)]
        for s in range(NB):
            r0 = s * BLK
            rows = slice(r0, r0 + BLK)
            bias = bias_ref[jnp.where((i * NB + s) % nblk_seq == 0, 1, 0)]
            attn = _attn_fwd(_cols(hm_ref, rows, C_Q, Q_W), _band(hm_ref, hprev_ref, s, C_K),
                             _band(hm_ref, hprev_ref, s, C_V), sinks_ref, bias)
            g_a = _cols(hm_ref, rows, C_GA, Q_W)
            ya_ref[rows, :] = (attn * (g_a * _sigmoid(g_a))).astype(BF16)
            u, _, _, _, mixed = _sgu_fwd(_cols(hm_ref, rows, C_UB, SGU_W), _cols(hm_ref, rows, C_VB, SGU_W),
                                         vng_ref[...], vnb_ref[...], wt, bsb_ref)
            g_b = _cols(hm_ref, rows, C_GB, SGU_W)
            yb_ref[rows, :] = (u * mixed * (g_b * _sigmoid(g_b))).astype(BF16)

    return pl.pallas_call(
        body, name=name, grid=(nt,),
        in_specs=_mixer_in_specs(nt, False) + [
            pl.BlockSpec((1, SGU_W), _CONST2), pl.BlockSpec((1, SGU_W), _CONST2),
            pl.BlockSpec((4, BLK, BLK), _CONST3), pl.BlockSpec((4, BLK, BLK), _CONST3)],
        out_specs=[pl.BlockSpec((TM, Q_W), lambda i: (i, 0)), pl.BlockSpec((TM, SGU_W), lambda i: (i, 0))],
        out_shape=[jax.ShapeDtypeStruct((t, Q_W), BF16), jax.ShapeDtypeStruct((t, SGU_W), BF16)],
        compiler_params=_ARB,
    )(sinks, hm, hm, bias, vn_g, vn_b, w_s, bsb)


def _tail_fwd(x, ya, yb, hr, pa_w, pb_w, wo, b_out, ln_g, ln_b, name, last):
    t = x.shape[0]

    def body(x_ref, ya_ref, yb_ref, hr_ref, paw_ref, pbw_ref, wo_ref, bo_ref, g_ref, b_ref,
             pa_ref, pb_ref, mg_ref, z_ref, *xn_ref):
        pa = _dot(ya_ref[...], paw_ref[...])
        pb = _dot(yb_ref[...], pbw_ref[...])
        pa_ref[...] = pa.astype(BF16)
        pb_ref[...] = pb.astype(BF16)
        everything = slice(None)
        merged = _sigmoid(_cols(hr_ref, everything, 0, D)) * pa + _sigmoid(_cols(hr_ref, everything, D, D)) * pb
        mb = merged.astype(BF16)
        mg_ref[...] = mb
        z = ALPHA * x_ref[...] + (_dot(mb, wo_ref[...]) + bo_ref[...])
        z_ref[...] = z
        if not last:
            zhat, _ = _ln_stats(z)
            xn_ref[0][...] = zhat * g_ref[...] + b_ref[...]

    row = lambda w: pl.BlockSpec((TM, w), lambda i: (i, 0))
    vec = pl.BlockSpec((1, D), _CONST2)
    n_f32 = 1 if last else 2
    return pl.pallas_call(
        body, name=name, grid=(t // TM,),
        in_specs=[row(D), row(Q_W), row(SGU_W), row(R_W),
                  pl.BlockSpec((Q_W, D), _CONST2), pl.BlockSpec((SGU_W, D), _CONST2), pl.BlockSpec((D, D), _CONST2),
                  vec, vec, vec],
        out_specs=[row(D)] * (3 + n_f32),
        out_shape=[jax.ShapeDtypeStruct((t, D), BF16)] * 3 + [jax.ShapeDtypeStruct((t, D), F32)] * n_f32,
        compiler_params=_ARB,
    )(x, ya, yb, hr, pa_w, pb_w, wo, b_out, ln_g, ln_b)


def _tail_bwd(dxn, z, pa, pb, hr, wo, pa_w, pb_w, ln_g, ln_b, name, from_loss):
    t = dxn.shape[0]

    def body(dxn_ref, z_ref, pa_ref, pb_ref, hr_ref, wo_ref, paw_ref, pbw_ref, g_ref, b_ref,
             dz_ref, dpa_ref, dpb_ref, dhr_ref, dya_ref, dyb_ref, acc_ref, gbr_ref):
        @pl.when(pl.program_id(0) == 0)
        def _():
            acc_ref[...] = jnp.zeros_like(acc_ref)
            gbr_ref[...] = jnp.zeros_like(gbr_ref)

        zhat, rstd = _ln_stats(z_ref[...])
        if from_loss:
            err = zhat * g_ref[...] + b_ref[...] - dxn_ref[...]
            dxn_v = err * (1.0 / D)
            sq = jnp.sum(jnp.sum(err * err, axis=1, keepdims=True), axis=0, keepdims=True)
            acc_ref[3:4, :] += jnp.broadcast_to(sq, (1, D))
        else:
            dxn_v = dxn_ref[...]
        dz = _ln_bwd(dxn_v * g_ref[...], zhat, rstd)
        dz_ref[...] = dz
        acc_ref[0:1, :] += _colsum(dxn_v * zhat)
        acc_ref[1:2, :] += _colsum(dxn_v)
        acc_ref[2:3, :] += _colsum(dz)
        dmerged = _dot_nt(dz.astype(BF16), wo_ref[...])
        everything = slice(None)
        sa = _sigmoid(_cols(hr_ref, everything, 0, D))
        sb = _sigmoid(_cols(hr_ref, everything, D, D))
        dpa = (dmerged * sa).astype(BF16)
        dpb = (dmerged * sb).astype(BF16)
        dpa_ref[...] = dpa
        dpb_ref[...] = dpb
        dra = dmerged * pa_ref[...].astype(F32) * (sa * (1.0 - sa))
        drb = dmerged * pb_ref[...].astype(F32) * (sb * (1.0 - sb))
        dhr_ref[:, 0:D] = dra.astype(BF16)
        dhr_ref[:, D:2 * D] = drb.astype(BF16)
        gbr_ref[0:1, 0:D] += _colsum(dra)
        gbr_ref[0:1, D:2 * D] += _colsum(drb)
        dya_ref[...] = _dot_nt(dpa, paw_ref[...]).astype(BF16)
        dyb_ref[...] = _dot_nt(dpb, pbw_ref[...]).astype(BF16)

    row = lambda w: pl.BlockSpec((TM, w), lambda i: (i, 0))
    vec = pl.BlockSpec((1, D), _CONST2)
    return pl.pallas_call(
        body, name=name, grid=(t // TM,),
        in_specs=[row(D), row(D), row(D), row(D), row(R_W),
                  pl.BlockSpec((D, D), _CONST2), pl.BlockSpec((Q_W, D), _CONST2), pl.BlockSpec((SGU_W, D), _CONST2),
                  vec, vec],
        out_specs=[row(D), row(D), row(D), row(R_W), row(Q_W), row(SGU_W), pl.BlockSpec((8, D), _CONST2),
                   pl.BlockSpec((8, R_W), _CONST2)],
        out_shape=[jax.ShapeDtypeStruct((t, D), F32), jax.ShapeDtypeStruct((t, D), BF16),
                   jax.ShapeDtypeStruct((t, D), BF16), jax.ShapeDtypeStruct((t, R_W), BF16),
                   jax.ShapeDtypeStruct((t, Q_W), BF16), jax.ShapeDtypeStruct((t, SGU_W), BF16),
                   jax.ShapeDtypeStruct((8, D), F32), jax.ShapeDtypeStruct((8, R_W), F32)],
        compiler_params=_ARB,
    )(dxn, z, pa, pb, hr, wo, pa_w, pb_w, ln_g, ln_b)


def _mixer_bwd(hm, dya, dyb, sinks, bias, vn_g, vn_b, w_s, bsb, nblk_seq, name):
    t = hm.shape[0]
    nt = t // TM

    def body(sinks_ref, hm_ref, hprev_ref, bias_ref, dya_ref, dyb_ref, vng_ref, vnb_ref, ws_ref, bsb_ref,
             dhm_ref, gbm_ref, gsk_ref, gvn_ref, gws_ref, gbs_ref, dk_carry, dv_carry):
        gi = pl.program_id(0)
        i = nt - 1 - gi

        @pl.when(gi == 0)
        def _():
            for r in (gbm_ref, gsk_ref, gvn_ref, gws_ref, gbs_ref, dk_carry, dv_carry):
                r[...] = jnp.zeros_like(r)

        tril = _tril_mask()
        wt = [jnp.where(tril, ws_ref[g], 0.0).astype(BF16) for g in range(4)]
        vng = vng_ref[...]

        def put(rows, col, val):
            dhm_ref[rows, col:col + val.shape[1]] = val.astype(BF16)
            gbm_ref[0:1, col:col + val.shape[1]] += _colsum(val)

        for s in reversed(range(NB)):
            r0 = s * BLK
            rows = slice(r0, r0 + BLK)
            bias = bias_ref[jnp.where((i * NB + s) % nblk_seq == 0, 1, 0)]
            q = _cols(hm_ref, rows, C_Q, Q_W)
            kband = _band(hm_ref, hprev_ref, s, C_K)
            vband = _band(hm_ref, hprev_ref, s, C_V)
            g_a = _cols(hm_ref, rows, C_GA, Q_W)
            sg = _sigmoid(g_a)
            dya_v = _cols(dya_ref, rows, 0, Q_W)
            d_o = dya_v * (g_a * sg)
            o_pairs, dq_pairs = [None] * 4, [None] * 4
            dkband = jnp.zeros((2 * BLK, KV_W), F32)
            dvband = jnp.zeros((2 * BLK, KV_W), F32)
            kband_t, vband_t = kband.T, vband.T
            for hk in range(N_KV):
                q4, kh, prob_t, p_sink, o_t = _attn_group(q, kband, vband_t, hk, sinks_ref, bias)
                o_pairs = _unstack_heads(o_t.T, hk, o_pairs)
                parts = []
                for g in range(4):
                    _, p, pos = _head_place(hk, g)
                    dp = d_o[:, BLK * p:BLK * (p + 1)]
                    parts.append(pltpu.roll(dp, 64, 1) if pos != hk else dp)
                do4 = _head_lanes(jnp.concatenate(parts, axis=0), hk)
                do4b = do4.astype(BF16)
                delta = _colsum(do4.T * o_t)
                vh = _head_lanes(vband, hk).astype(BF16)
                ds_t = prob_t * (_dot_nt(vh, do4b) - delta)
                dsb = ds_t.astype(BF16)
                dq4_t = _dot(_head_rows(kband_t, hk), dsb)
                dq_pairs = _unstack_heads(dq4_t.T * ATTN_SCALE, hk, dq_pairs)
                dkband = dkband + _head_lanes(_dot(dsb, q4), hk)
                dvband = dvband + _dot(prob_t.astype(BF16), do4b)
                dsk = p_sink * delta
                for g in range(4):
                    j = 4 * hk + g
                    tot = jnp.sum(dsk[:, BLK * g:BLK * (g + 1)], axis=1, keepdims=True)
                    gsk_ref[j:j + 1, :] += jnp.broadcast_to(-tot, (1, 128))
            attn = jnp.concatenate(o_pairs, axis=1)
            put(rows, C_Q, jnp.concatenate(dq_pairs, axis=1))
            put(rows, C_K, dkband[BLK:2 * BLK] + dk_carry[...])
            put(rows, C_V, dvband[BLK:2 * BLK] + dv_carry[...])
            dk_carry[...] = dkband[0:BLK]
            dv_carry[...] = dvband[0:BLK]
            put(rows, C_GA, dya_v * attn * (sg * (1.0 + g_a * (1.0 - sg))))
            u_b = _cols(hm_ref, rows, C_UB, SGU_W)
            v_b = _cols(hm_ref, rows, C_VB, SGU_W)
            g_b = _cols(hm_ref, rows, C_GB, SGU_W)
            u, vhat, rstd, vn, mixed = _sgu_fwd(u_b, v_b, vng, vnb_ref[...], wt, bsb_ref)
            sgb = _sigmoid(g_b)
            silu_b = g_b * sgb
            dyb_v = _cols(dyb_ref, rows, 0, SGU_W)
            du = dyb_v * mixed * silu_b
            dmixed = dyb_v * u * silu_b
            put(rows, C_GB, dyb_v * u * mixed * (sgb * (1.0 + g_b * (1.0 - sgb))))
            dvn_parts = []
            for g in range(4):
                cols = slice(BLK * g, BLK * (g + 1))
                dmg = dmixed[:, cols]
                dmgb = dmg.astype(BF16)
                dvn_parts.append(_dot_tn(wt[g], dmgb))
                gws_ref[g] += jnp.where(tril, _dot_nt(dmgb, vn[:, cols].astype(BF16)), 0.0)
                gbs_ref[g] += dmg
            dvn = jnp.concatenate(dvn_parts, axis=1)
            gvn_ref[0:1, :] += _colsum(dvn * vhat)
            gvn_ref[1:2, :] += _colsum(dvn)
            dv = _ln_bwd(dvn * vng, vhat, rstd)
            put(rows, C_UB, du * _dgelu(u_b))
            put(rows, C_VB, dv * _dgelu(v_b))

        @pl.when(gi == nt - 1)
        def _():
            for g in range(4):
                gbs_ref[g] = jnp.broadcast_to(jnp.sum(gbs_ref[g], axis=1, keepdims=True), (BLK, BLK))

    row = lambda w: pl.BlockSpec((TM, w), lambda g: (nt - 1 - g, 0))
    return pl.pallas_call(
        body, name=name, grid=(nt,),
        in_specs=_mixer_in_specs(nt, True) + [
            row(Q_W), row(SGU_W),
            pl.BlockSpec((1, SGU_W), _CONST2), pl.BlockSpec((1, SGU_W), _CONST2),
            pl.BlockSpec((4, BLK, BLK), _CONST3), pl.BlockSpec((4, BLK, BLK), _CONST3)],
        out_specs=[row(MAIN_W), pl.BlockSpec((8, MAIN_W), _CONST2), pl.BlockSpec((8, 128), _CONST2),
                   pl.BlockSpec((8, SGU_W), _CONST2), pl.BlockSpec((4, BLK, BLK), _CONST3),
                   pl.BlockSpec((4, BLK, BLK), _CONST3)],
        out_shape=[jax.ShapeDtypeStruct((t, MAIN_W), BF16), jax.ShapeDtypeStruct((8, MAIN_W), F32),
                   jax.ShapeDtypeStruct((8, 128), F32), jax.ShapeDtypeStruct((8, SGU_W), F32),
                   jax.ShapeDtypeStruct((4, BLK, BLK), F32), jax.ShapeDtypeStruct((4, BLK, BLK), F32)],
        scratch_shapes=[pltpu.VMEM((BLK, KV_W), F32), pltpu.VMEM((BLK, KV_W), F32)],
        compiler_params=_ARB,
    )(sinks, hm, hm, bias, dya, dyb, vn_g, vn_b, w_s, bsb)


def _dx_inproj(dz, dhm, dhr, w_t, after, name):
    t = dz.shape[0]

    def body(dz_ref, dhm_ref, dhr_ref, wt_ref, after_ref, dx_ref):
        dx_ref[...] = (ALPHA * dz_ref[...] + after_ref[0:1, 0:1] + _dot(dhm_ref[...], wt_ref[0:MAIN_W, :])
                       + _dot(dhr_ref[...], wt_ref[MAIN_W:IN_COLS, :]))

    row = lambda w: pl.BlockSpec((TM_MM, w), lambda i: (i, 0))
    return pl.pallas_call(
        body, name=name, grid=(t // TM_MM,),
        in_specs=[row(D), row(MAIN_W), row(R_W), pl.BlockSpec((IN_COLS, D), _CONST2),
                  pl.BlockSpec((8, 128), _CONST2)],
        out_specs=row(D), out_shape=jax.ShapeDtypeStruct((t, D), F32), compiler_params=_ARB,
    )(dz, dhm, dhr, w_t, after)


def _wgrad(a, b, tm, name, tk=1024):
    t, m = a.shape
    n = b.shape[1]
    nk = t // tk

    def body(a_ref, b_ref, o_ref, acc_ref):
        k = pl.program_id(1)

        @pl.when(k == 0)
        def _():
            acc_ref[...] = jnp.zeros_like(acc_ref)

        acc_ref[...] += _dot_tn(a_ref[...].astype(BF16), b_ref[...].astype(BF16))

        @pl.when(k == nk - 1)
        def _():
            o_ref[...] = acc_ref[...].astype(BF16)

    return pl.pallas_call(
        body, name=name, grid=(m // tm, nk),
        in_specs=[pl.BlockSpec((tk, tm), lambda j, k: (k, j)), pl.BlockSpec((tk, n), lambda j, k: (k, 0))],
        out_specs=pl.BlockSpec((tm, n), lambda j, k: (j, 0)),
        out_shape=jax.ShapeDtypeStruct((m, n), BF16),
        scratch_shapes=[pltpu.VMEM((tm, n), F32)],
        compiler_params=pltpu.CompilerParams(dimension_semantics=("arbitrary", "arbitrary"), vmem_limit_bytes=VMEM_LIMIT),
    )(a, b)


def _ln_in_bwd(dx0, x, g, name):
    t = x.shape[0]

    def body(dx0_ref, x_ref, g_ref, gx_ref, acc_ref):
        @pl.when(pl.program_id(0) == 0)
        def _():
            acc_ref[...] = jnp.zeros_like(acc_ref)

        d = dx0_ref[...]
        xhat, rstd = _ln_stats(x_ref[...])
        gx_ref[...] = _ln_bwd(d * g_ref[...], xhat, rstd)
        acc_ref[0:1, :] += _colsum(d * xhat)
        acc_ref[1:2, :] += _colsum(d)

    row = pl.BlockSpec((TM_EW, D), lambda i: (i, 0))
    return pl.pallas_call(
        body, name=name, grid=(t // TM_EW,),
        in_specs=[row, row, pl.BlockSpec((1, D), _CONST2)],
        out_specs=[row, pl.BlockSpec((8, D), _CONST2)],
        out_shape=[jax.ShapeDtypeStruct((t, D), F32), jax.ShapeDtypeStruct((8, D), F32)],
        compiler_params=_ARB,
    )(dx0, x, g)


_ANY = pl.BlockSpec(memory_space=pl.ANY)


def _place():
    return lax.axis_index("x"), lax.axis_index("y"), lax.axis_index("c")


def _allgather8(xs, name):
    n = len(xs)

    def body(*refs):
        x_refs, o_refs = refs[:n], refs[n:2 * n]
        send_sems, recv_sems, local_sems = refs[2 * n:]
        x, y, c = _place()
        me, sib = (x, y, c), (x, y, 1 - c)
        chips = [(1 - x, y), (x, 1 - y), (1 - x, 1 - y)]

        def copy(a, k, block, to, src=None):
            dst = o_refs[a].at[4 * block[0] + 2 * block[1] + block[2]]
            return pltpu.make_async_remote_copy(
                src_ref=dst if src is None else src, dst_ref=dst, send_sem=send_sems.at[7 * a + k],
                recv_sem=recv_sems.at[7 * a + k], device_id=to, device_id_type=MESH)

        mine = [pltpu.make_async_copy(x_refs[a], o_refs[a].at[4 * x + 2 * y + c], local_sems.at[a]) for a in range(n)]
        for cp in mine:
            cp.start()
        sent = []
        for a in range(n):
            sent.append(copy(a, 0, me, sib, src=x_refs[a]))
            sent += [copy(a, 1 + j, me, (*chip, c), src=x_refs[a]) for j, chip in enumerate(chips)]
        for cp in sent:
            cp.start()
        for j, chip in enumerate(chips):
            for a in range(n):
                copy(a, 1 + j, (*chip, c), me).wait_recv()
                fwd = copy(a, 4 + j, (*chip, c), sib)
                fwd.start()
                sent.append(fwd)
        for a in range(n):
            copy(a, 0, sib, me).wait_recv()
            for j, chip in enumerate(chips):
                copy(a, 4 + j, (*chip, 1 - c), me).wait_recv()
        for cp in sent:
            cp.wait_send()
        for cp in mine:
            cp.wait()

    return pl.pallas_call(
        body, name=name, in_specs=[_ANY] * n, out_specs=[_ANY] * n,
        out_shape=[jax.ShapeDtypeStruct((N_DEV,) + v.shape, v.dtype) for v in xs],
        scratch_shapes=[pltpu.SemaphoreType.DMA((7 * n,)), pltpu.SemaphoreType.DMA((7 * n,)),
                        pltpu.SemaphoreType.DMA((n,))],
    )(*xs)


def _forward_sibling(lands, name):
    n = len(lands)

    def body(*refs):
        l_refs = refs[n:2 * n]
        send_sems, recv_sems = refs[2 * n:]
        x, y, c = _place()
        chips = [(1 - x, y), (x, 1 - y), (1 - x, 1 - y)]

        def copy(a, j, core):
            rows = l_refs[a].at[4 * chips[j][0] + 2 * chips[j][1] + core]
            return pltpu.make_async_remote_copy(
                src_ref=rows, dst_ref=rows, send_sem=send_sems.at[3 * a + j], recv_sem=recv_sems.at[3 * a + j],
                device_id=(x, y, 1 - c), device_id_type=MESH)

        for a in range(n):
            for j in range(3):
                copy(a, j, c).start()
        for a in range(n):
            for j in range(3):
                copy(a, j, 1 - c).wait_recv()
                copy(a, j, c).wait_send()

    return pl.pallas_call(
        body, name=name, in_specs=[_ANY] * n, out_specs=[_ANY] * n,
        out_shape=[jax.ShapeDtypeStruct(v.shape, v.dtype) for v in lands],
        input_output_aliases={a: a for a in range(n)},
        scratch_shapes=[pltpu.SemaphoreType.DMA((3 * n,)), pltpu.SemaphoreType.DMA((3 * n,))],
    )(*lands)


def _swap_sibling(gs, name):
    n = len(gs)
    first = [0]
    for v in gs:
        first.append(first[-1] + v.shape[0])

    def body(*refs):
        g_refs, r_refs = refs[:n], refs[n:2 * n]
        send_sems, recv_sems = refs[2 * n:]
        x, y, c = _place()
        cps = [pltpu.make_async_remote_copy(
            src_ref=g_refs[a].at[q, 1 - c], dst_ref=r_refs[a].at[q], send_sem=send_sems.at[first[a] + q],
            recv_sem=recv_sems.at[first[a] + q], device_id=(x, y, 1 - c), device_id_type=MESH)
            for a in range(n) for q in range(gs[a].shape[0])]
        for cp in cps:
            cp.start()
        for cp in cps:
            cp.wait()

    return pl.pallas_call(
        body, name=name, in_specs=[_ANY] * n, out_specs=[_ANY] * n,
        out_shape=[jax.ShapeDtypeStruct(v.shape[:1] + v.shape[2:], v.dtype) for v in gs],
        scratch_shapes=[pltpu.SemaphoreType.DMA((first[-1],)), pltpu.SemaphoreType.DMA((first[-1],))],
    )(*gs)


def _row_tile(rows, lanes, cap):
    if rows * lanes * 4 <= (1 << 20):
        return rows
    return max(d for d in range(8, cap + 1, 8) if rows % d == 0 and (d % 16 == 0 or rows % 16 != 0))


def _pair_sum(g, r, name):
    n, _, rows, l = g.shape
    tr = _row_tile(rows, l, 608)

    def body(c_ref, g_ref, r_ref, o_ref):
        o_ref[...] = (g_ref[0].astype(F32) + r_ref[...].astype(F32)).astype(o_ref.dtype)

    return pl.pallas_call(
        body, name=name,
        grid_spec=pltpu.PrefetchScalarGridSpec(
            num_scalar_prefetch=1, grid=(n, rows // tr),
            in_specs=[pl.BlockSpec((1, 1, tr, l), lambda q, i, c_ref: (q, c_ref[0], i, 0)),
                      pl.BlockSpec((1, tr, l), lambda q, i, c_ref: (q, i, 0))],
            out_specs=pl.BlockSpec((1, tr, l), lambda q, i, c_ref: (q, i, 0))),
        out_shape=jax.ShapeDtypeStruct((n, rows, l), g.dtype),
        compiler_params=pltpu.CompilerParams(dimension_semantics=("arbitrary", "arbitrary"), vmem_limit_bytes=VMEM_LIMIT),
    )(lax.axis_index("c").astype(jnp.int32).reshape(1), g, r)


def _adamw(parts, w, m, v, name, own=None):
    nl = len(parts)
    ns, rows, l = parts[0].shape
    tr = _row_tile(rows, l * ns, 304)
    nt = rows // tr
    c1 = 1.0 - ADAM_B1 ** ADAM_STEP
    c2 = 1.0 - ADAM_B2 ** ADAM_STEP

    def body(q_ref, *refs):
        own_refs = refs[:nl] if own is not None else None
        p_refs = refs[-7 - nl:-7]
        w_ref, m_ref, v_ref, g_ref, d_ref, nm_ref, nv_ref = refs[-7:]
        layer = pl.program_id(0)
        g = None
        for j in range(nl):
            gj = None
            for k in range(ns):
                term = p_refs[j][k].astype(F32)
                if own_refs is not None:
                    term = jnp.where(q_ref[0] == k, own_refs[j][0].astype(F32), term)
                gj = term if gj is None else gj + term
            g = gj if g is None else jnp.where(layer == j, gj, g)
        g_ref[...] = g
        nm = ADAM_B1 * m_ref[...] + (1.0 - ADAM_B1) * g
        nv = ADAM_B2 * v_ref[...] + (1.0 - ADAM_B2) * (g * g)
        nm_ref[...] = nm
        nv_ref[...] = nv
        d_ref[...] = -ADAM_LR * ((nm / c1) / (jnp.sqrt(nv / c2) + ADAM_EPS) + ADAM_WD * w_ref[...])

    def tile_of(j):
        return lambda la, i, q: jnp.where(la == j, i, jnp.where(la < j, 0, nt - 1))

    row = pl.BlockSpec((tr, l), lambda la, i, q: (la * nt + i, 0))
    own_specs = [] if own is None else [
        pl.BlockSpec((1, tr, l), lambda la, i, q, j=j: (q[0], tile_of(j)(la, i, q), 0)) for j in range(nl)]
    part_specs = [pl.BlockSpec((ns, tr, l), lambda la, i, q, j=j: (0, tile_of(j)(la, i, q), 0)) for j in range(nl)]
    chip = (2 * lax.axis_index("x") + lax.axis_index("y")).astype(jnp.int32).reshape(1)
    return pl.pallas_call(
        body, name=name,
        grid_spec=pltpu.PrefetchScalarGridSpec(
            num_scalar_prefetch=1, grid=(nl, nt),
            in_specs=own_specs + part_specs + [row, row, row], out_specs=[row] * 4),
        out_shape=[jax.ShapeDtypeStruct((nl * rows, l), F32)] * 4,
        compiler_params=pltpu.CompilerParams(dimension_semantics=("arbitrary", "arbitrary"), vmem_limit_bytes=VMEM_LIMIT),
    )(chip, *([] if own is None else own), *parts, w, m, v)


_HBM = pl.BlockSpec(memory_space=pltpu.HBM)
_SEM = pl.BlockSpec(memory_space=pltpu.SEMAPHORE)
_EFFECT = pltpu.SideEffectType.DATAFLOW_SIDE_EFFECTING


def _plan_all(x, y, c):
    me = 4 * x + 2 * y + c
    peers = [(x, y, 1 - c), (1 - x, y, c), (x, 1 - y, c), (1 - x, 1 - y, c),
             (1 - x, y, 1 - c), (x, 1 - y, 1 - c), (1 - x, 1 - y, 1 - c)]
    return [(None, me, p, 4 * p[0] + 2 * p[1] + p[2]) for p in peers]


def _plan_near(x, y, c):
    me = 4 * x + 2 * y + c
    peers = [(x, y, 1 - c), (1 - x, y, c), (x, 1 - y, c), (1 - x, 1 - y, c)]
    return [(None, me, p, 4 * p[0] + 2 * p[1] + p[2]) for p in peers]


def _plan_chips(x, y, c):
    me = 2 * x + y
    return [(2 * qx + qy, me, (qx, qy, c), 2 * qx + qy) for qx, qy in ((1 - x, y), (x, 1 - y), (1 - x, 1 - y))]


def _split_copies(plan, src_refs, land_refs, send_sems, recv_sems, arrival):
    n = len(src_refs)
    entries = plan(*_place())
    per = len(entries)
    cps = []
    for a in range(n):
        for k, (src_slot, dst_slot, peer, back_slot) in enumerate(entries):
            src = src_refs[a] if src_slot is None else src_refs[a].at[src_slot]
            cps.append(pltpu.make_async_remote_copy(
                src_ref=src, dst_ref=land_refs[a].at[back_slot if arrival else dst_slot],
                send_sem=send_sems.at[per * a + k], recv_sem=recv_sems.at[per * a + k],
                device_id=peer, device_id_type=MESH))
    return cps


def _split_start(srcs, lands, plan, per, name):
    n = len(srcs)

    def body(*refs):
        for cp in _split_copies(plan, refs[:n], refs[n:2 * n], refs[2 * n], refs[2 * n + 1], False):
            cp.start()
        refs[-1][...] = jnp.zeros_like(refs[-1])

    both = list(srcs) + list(lands)
    outs = pl.pallas_call(
        body, name=name,
        out_shape=(pltpu.SemaphoreType.DMA((per * n,)), pltpu.SemaphoreType.DMA((per * n,)),
                   *[pltpu.HBM(v.shape, v.dtype) for v in both], jax.ShapeDtypeStruct((8, 128), F32)),
        in_specs=[_HBM] * (2 * n),
        out_specs=(_SEM, _SEM, *[_HBM] * (2 * n), pl.BlockSpec(memory_space=pltpu.VMEM)),
        input_output_aliases={i: 2 + i for i in range(2 * n)},
        compiler_params=pltpu.CompilerParams(has_side_effects=_EFFECT),
    )(*[pltpu.with_memory_space_constraint(v, pltpu.HBM) for v in both])
    return outs[0], outs[1], list(outs[2:2 + 2 * n]), outs[-1]


def _split_wait(send_sems, recv_sems, thru, plan, after, name):
    n = len(thru) // 2

    def body(*refs):
        for cp in _split_copies(plan, refs[:n], refs[n:2 * n], refs[2 * n], refs[2 * n + 1], True):
            cp.wait_send()
            cp.wait_recv()

    outs = pl.pallas_call(
        body, name=name, out_shape=tuple(pltpu.HBM(v.shape, v.dtype) for v in thru),
        in_specs=[_HBM] * (2 * n) + [_SEM, _SEM, pl.BlockSpec(memory_space=pl.ANY)],
        out_specs=[_HBM] * (2 * n), input_output_aliases={i: i for i in range(2 * n)},
        compiler_params=pltpu.CompilerParams(has_side_effects=_EFFECT),
    )(*thru, send_sems, recv_sems, after)
    return list(outs[:n]), list(outs[n:])


_SMALL_IN = ("ln_in_g", "ln_in_b")
_SMALL = ("b_in", "sinks", "vn_g", "vn_b", "w_s", "b_s", "b_out", "ln_g", "ln_b")


def _rows128(a):
    flat = a.reshape(-1)
    return jnp.pad(flat, (0, (-flat.shape[0]) % 128)).reshape(-1, 128)


def _pack_small(d, names):
    rows = jnp.concatenate([_rows128(d[n]) for n in names])
    return jnp.pad(rows, ((0, (-rows.shape[0]) % 8), (0, 0)))


def _unpack_small(p, like, names):
    off, out = 0, {}
    for n in names:
        size = like[n].size
        rows = -(-size // 128)
        out[n] = p[off:off + rows].reshape(-1)[:size].reshape(like[n].shape)
        off += rows
    return out


def _owner_blocks(g, axis):
    sh = g.shape
    g = g.reshape(sh[:axis] + (4, 2, sh[axis] // N_DEV) + sh[axis + 1:])
    return jnp.moveaxis(g, (axis, axis + 1), (0, 1))


def kernel(x, ln_in_g, ln_in_b, w_in, b_in, sinks, vn_g, vn_b, w_s, b_s, p_a, p_b, w_out, b_out, ln_g, ln_b, loss_target, m_ln_in_g, m_ln_in_b, m_w_in, m_b_in, m_sinks, m_vn_g, m_vn_b, m_w_s, m_b_s, m_p_a, m_p_b, m_w_out, m_b_out, m_ln_g, m_ln_b, v_ln_in_g, v_ln_in_b, v_w_in, v_b_in, v_sinks, v_vn_g, v_vn_b, v_w_s, v_b_s, v_p_a, v_p_b, v_w_out, v_b_out, v_ln_g, v_ln_b):
    nseq, seq, _ = x.shape
    t = nseq * seq
    nblk_seq = seq // BLK
    x2 = x.reshape(t, D)
    tgt = loss_target.reshape(t, D)

    def turned(a):
        return jnp.swapaxes(a, 1, 2)

    w_in_t = turned(w_in)

    def blocks(l):
        return [w_in_t[l].astype(BF16), p_a[l].astype(BF16), p_b[l].astype(BF16), w_out[l].astype(BF16)]

    def full_weights(g):
        w_t_full = g[0].reshape(IN_COLS, D)
        pa_full = jnp.moveaxis(g[1], 0, 1).reshape(Q_W, D)
        pb_full = jnp.moveaxis(g[2], 0, 1).reshape(SGU_W, D)
        wo_full = g[3].reshape(D, D)
        return dict(w_t=w_t_full, pa=pa_full, pb=pb_full, wo=wo_full)

    def landing(bs):
        return [lax.empty((N_DEV,) + v.shape, v.dtype) for v in bs]

    def with_own(landed, sent):
        return [lax.dynamic_update_index_in_dim(g, b, me, 0) for g, b in zip(landed, sent)]

    me = 4 * lax.axis_index("x") + 2 * lax.axis_index("y") + lax.axis_index("c")
    blocks0 = blocks(0)
    a_send, a_recv, a_thru, a_token = _split_start(blocks0[:1], landing(blocks0[:1]), _plan_near, 4,
                                                   "allgather_w_in0_start")
    rest0 = [b + a_token[0, 0].astype(BF16) for b in blocks0[1:]]
    b_send, b_recv, b_thru, b_token = _split_start(rest0, landing(rest0), _plan_near, 4, "allgather_rest0_start")
    xs = [_ln_fwd(x2, ln_in_g + b_token[0, 0], ln_in_b, "ln_in_fwd")]
    sent, landed = _split_wait(a_send, a_recv, a_thru, _plan_near, xs[0], "allgather_w_in0_wait")
    gathered0 = with_own(_forward_sibling(landed, "allgather_w_in0_forward"), sent)
    blocks1, gathered0 = lax.optimization_barrier((blocks(1), gathered0))
    ag_send, ag_recv, ag_thru, ag_token = _split_start(blocks1, landing(blocks1), _plan_all, 7,
                                                       "allgather_weights1_start")
    weights = [None, None]
    bsb = jnp.broadcast_to(b_s[:, :, :, None], (DEPTH, 4, BLK, BLK))
    bias = _band_bias()

    saved = []
    for l in range(DEPTH):
        if l == 1:
            sent, landed = _split_wait(ag_send, ag_recv, ag_thru, _plan_all, xs[1], "allgather_weights1_wait")
            weights[1] = full_weights(with_own(landed, sent))
        w_t = weights[l]["w_t"] if l else gathered0[0].reshape(IN_COLS, D)
        last = l == DEPTH - 1
        b_l = b_in[l].reshape(1, -1) + (ag_token[0, 0] if l == 0 else 0.0)
        hm, hr = _inproj(xs[l], w_t, b_l, f"inproj{l}")
        ya, yb = _mixer_fwd(hm, sinks[l], bias, vn_g[l].reshape(1, -1), vn_b[l].reshape(1, -1), w_s[l], bsb[l],
                            nblk_seq, f"mixer_fwd{l}")
        if l == 0:
            sent, landed = _split_wait(b_send, b_recv, b_thru, _plan_near, ya, "allgather_rest0_wait")
            weights[0] = full_weights(gathered0 + with_own(_forward_sibling(landed, "allgather_rest0_forward"), sent))
        wl = weights[l]
        outs = _tail_fwd(xs[l], ya, yb, hr, wl["pa"], wl["pb"], wl["wo"], b_out[l].reshape(1, D),
                         ln_g[l].reshape(1, D), ln_b[l].reshape(1, D), f"tail_fwd{l}", last)
        saved.append((hm, hr, ya, yb) + tuple(outs[:4]))
        if not last:
            xs.append(outs[4])

    small = {n: [None] * DEPTH for n in _SMALL}
    names = ("w_in", "p_a", "p_b", "w_out")
    owner_axis = {"w_in": 0, "p_a": 1, "p_b": 1, "w_out": 0}
    token = jnp.zeros((8, 128), F32)
    dx = tgt
    split = [None] * DEPTH
    for l in reversed(range(DEPTH)):
        hm, hr, ya, yb, pa, pb, merged, z = saved[l]
        wl = weights[l]
        dz, dpa, dpb, dhr, dya, dyb, acc, gbr = _tail_bwd(
            dx, z, pa, pb, hr, wl["wo"], wl["pa"], wl["pb"], ln_g[l].reshape(1, D) + token[0, 0],
            ln_b[l].reshape(1, D), f"tail_bwd{l}", l == DEPTH - 1)
        if l == DEPTH - 1:
            loss = lax.psum(acc[3, 0] * (0.5 / D), ("x", "y", "c"))
        dhm, gbm, gsk, gvn, gws, gbs = _mixer_bwd(
            hm, dya, dyb, sinks[l], bias, vn_g[l].reshape(1, -1), vn_b[l].reshape(1, -1), w_s[l], bsb[l],
            nblk_seq, f"mixer_bwd{l}")
        grads = {"w_in": jnp.concatenate([_wgrad(dhm, xs[l], MAIN_W // 2, f"wgrad_in_main{l}"),
                                          _wgrad(dhr, xs[l], R_W // 2, f"wgrad_in_route{l}")], axis=0),
                 "p_a": _wgrad(ya, dpa, Q_W, f"wgrad_pa{l}"), "p_b": _wgrad(yb, dpb, SGU_W, f"wgrad_pb{l}"),
                 "w_out": _wgrad(merged, dz, D, f"wgrad_out{l}")}
        small["b_in"][l] = jnp.concatenate([gbm[0], gbr[0]])
        small["sinks"][l] = gsk[:, 0]
        small["vn_g"][l], small["vn_b"][l] = gvn[0], gvn[1]
        small["w_s"][l], small["b_s"][l] = gws, gbs[:, :, 0]
        small["ln_g"][l], small["ln_b"][l], small["b_out"][l] = acc[0], acc[1], acc[2]
        parts = [_owner_blocks(grads[n], owner_axis[n]) for n in names]
        if l == 0:
            packed = _pack_small({n: jnp.stack(v) for n, v in small.items()}, _SMALL)
            parts.append(jnp.broadcast_to(packed[None, None], (1, 2) + packed.shape))
        from_sib = _swap_sibling(parts, f"rs_sibling{l}")
        pair = [_pair_sum(g, r, f"pair_sum{l}_{a}") for a, (g, r) in enumerate(zip(parts, from_sib))]
        if l == 0:
            pair[4] = jnp.broadcast_to(pair[4], (4,) + packed.shape)
        lands = [jnp.zeros(p.shape, p.dtype) for p in pair]
        split[l] = _split_start(pair, lands, _plan_chips, 3, f"rs_chips{l}_start")
        token = split[l][3]
        dx = _dx_inproj(dz, dhm, dhr, wl["w_t"], token, f"dx_inproj{l}")
    grad_x, acc_in = _ln_in_bwd(dx, x2, ln_in_g.reshape(1, D), "ln_in_bwd")
    (all_in,) = _allgather8([acc_in], "allgather_ln_in")

    given = {"w_in": (w_in_t, turned(m_w_in), turned(v_w_in)), "p_a": (p_a, m_p_a, v_p_a),
             "p_b": (p_b, m_p_b, v_p_b), "w_out": (w_out, m_w_out, v_w_out)}
    waited = [_split_wait(split[l][0], split[l][1], split[l][2], _plan_chips, all_in, f"rs_chips{l}_wait")
              for l in range(DEPTH)]
    res = {}
    for a, n in enumerate(names):
        rows, lanes = waited[0][1][a].shape[1:]
        outs = _adamw([waited[l][1][a] for l in range(DEPTH)], *[v.reshape(DEPTH * rows, lanes) for v in given[n]],
                      f"adamw_{n}", own=[waited[l][0][a] for l in range(DEPTH)])
        res[n] = [o.reshape(given[n][0].shape) for o in outs]
    res["w_in"] = [turned(o) for o in res["w_in"]]

    w_small = dict(ln_in_g=ln_in_g, ln_in_b=ln_in_b, b_in=b_in, sinks=sinks, vn_g=vn_g, vn_b=vn_b, w_s=w_s, b_s=b_s,
                   b_out=b_out, ln_g=ln_g, ln_b=ln_b)
    m_small = dict(ln_in_g=m_ln_in_g, ln_in_b=m_ln_in_b, b_in=m_b_in, sinks=m_sinks, vn_g=m_vn_g, vn_b=m_vn_b,
                   w_s=m_w_s, b_s=m_b_s, b_out=m_b_out, ln_g=m_ln_g, ln_b=m_ln_b)
    v_small = dict(ln_in_g=v_ln_in_g, ln_in_b=v_ln_in_b, b_in=v_b_in, sinks=v_sinks, vn_g=v_vn_g, vn_b=v_vn_b,
                   w_s=v_w_s, b_s=v_b_s, b_out=v_b_out, ln_g=v_ln_g, ln_b=v_ln_b)
    outs = _adamw([waited[0][1][4]], *[_pack_small(d, _SMALL) for d in (w_small, m_small, v_small)], "adamw_small",
                  own=[waited[0][0][4]])
    outs_in = _adamw([all_in], *[jnp.pad(jnp.stack([d[n] for n in _SMALL_IN]), ((0, 6), (0, 0)))
                                 for d in (w_small, m_small, v_small)], "adamw_ln_in")
    for k in range(4):
        u = _unpack_small(outs[k], w_small, _SMALL)
        u.update({n: outs_in[k][r] for r, n in enumerate(_SMALL_IN)})
        for n in u:
            res.setdefault(n, [None] * 4)[k] = u[n]

    order = ("ln_in_g", "ln_in_b", "w_in", "b_in", "sinks", "vn_g", "vn_b", "w_s", "b_s", "p_a", "p_b", "w_out",
             "b_out", "ln_g", "ln_b")
    return (loss, grad_x.reshape(x.shape), *[res[n][0] for n in order], *[res[n][1] for n in order],
            *[res[n][2] for n in order], *[res[n][3] for n in order])
```

```python
import jax
import jax.numpy as jnp
from jax import lax
from jax.experimental import pallas as pl
from jax.experimental.pallas import tpu as pltpu

F32 = jnp.float32
BF16 = jnp.bfloat16

D = 1024
BLK = 128
N_KV = 2
Q_W, KV_W, SGU_W = 512, 128, 512
C_Q, C_K, C_V, C_GA, C_UB, C_VB, C_GB = 0, 512, 640, 768, 1280, 1792, 2304
MAIN_W = 2816
R_W = 2048
IN_COLS = MAIN_W + R_W
N_DEV = 8
SHARD_COLS = IN_COLS // N_DEV

DEPTH = 2
ALPHA = (2.0 * DEPTH) ** 0.25
LN_EPS = 1e-5
ATTN_SCALE = 0.125
NEG = float(jnp.finfo(jnp.float32).min)

ADAM_LR, ADAM_B1, ADAM_B2, ADAM_EPS, ADAM_WD, ADAM_STEP = 0.001, 0.9, 0.999, 1e-08, 0.01, 10

TM = 256
TM_EW = 512
TM_MM = 512
NB = TM // BLK
MESH = pl.DeviceIdType.MESH
VMEM_LIMIT = 56 * 1024 * 1024

_ARB = pltpu.CompilerParams(dimension_semantics=("arbitrary",), vmem_limit_bytes=VMEM_LIMIT)


def _sigmoid(x):
    return 1.0 / (1.0 + jnp.exp(-x))


_GELU_C = 0.7978845608028654
_GELU_A = 0.044715


def _gelu_parts(x):
    x2 = x * x
    t = jnp.tanh(x * (_GELU_C + (_GELU_C * _GELU_A) * x2))
    hx = 0.5 * x
    return hx, t, x2


def _gelu(x):
    hx, t, _ = _gelu_parts(x)
    return hx + hx * t


def _gelu_and_grad(x):
    hx, t, x2 = _gelu_parts(x)
    grad = 0.5 + 0.5 * t + (hx - hx * (t * t)) * (_GELU_C + (3.0 * _GELU_C * _GELU_A) * x2)
    return hx + hx * t, grad


def _ln_stats(x):
    mu = jnp.mean(x, axis=-1, keepdims=True)
    xc = x - mu
    var = jnp.mean(xc * xc, axis=-1, keepdims=True)
    rstd = lax.rsqrt(var + LN_EPS)
    return xc * rstd, rstd


def _ln_bwd(dy_g, xhat, rstd):
    m1 = jnp.mean(dy_g, axis=-1, keepdims=True)
    m2 = jnp.mean(dy_g * xhat, axis=-1, keepdims=True)
    return rstd * (dy_g - m1 - xhat * m2)


def _colsum(x):
    return jnp.sum(x, axis=0, keepdims=True)


def _dot(a, b):
    return jnp.dot(a, b, preferred_element_type=F32)


def _dot_nt(a, b):
    return lax.dot_general(a, b, (((1,), (1,)), ((), ())), preferred_element_type=F32)


def _dot_tn(a, b):
    return lax.dot_general(a, b, (((0,), (0,)), ((), ())), preferred_element_type=F32)


def _head_place(hk, g):
    j = 4 * hk + g
    return j, j // 2, j % 2


def _head_rows(x, hk):
    d = lax.broadcasted_iota(jnp.int32, x.shape, 0)
    return jnp.where((d >= 64 * hk) & (d < 64 * hk + 64), x, 0.0).astype(BF16)


def _head_lanes(x, hk):
    d = lax.broadcasted_iota(jnp.int32, x.shape, 1)
    return jnp.where((d >= 64 * hk) & (d < 64 * hk + 64), x, 0.0)


def _band_bias():
    kpos = lax.broadcasted_iota(jnp.int32, (2 * BLK, 4 * BLK), 0)
    row = lax.broadcasted_iota(jnp.int32, (2 * BLK, 4 * BLK), 1) & (BLK - 1)
    band = (kpos > row) & (kpos <= row + BLK)
    return jnp.stack([jnp.where(band, 0.0, NEG), jnp.where(band & (kpos >= BLK), 0.0, NEG)]).astype(F32)


def _stack_q(q, hk):
    parts = []
    for g in range(4):
        _, p, pos = _head_place(hk, g)
        qp = q[:, BLK * p:BLK * (p + 1)] * ATTN_SCALE
        if pos != hk:
            qp = pltpu.roll(qp, 64, 1)
        parts.append(qp.astype(BF16))
    return jnp.concatenate(parts, axis=0)


def _attn_probs(q4, kh, hk, sinks_ref, bias):
    s_t = _dot_nt(kh, q4) + bias
    sink_row = jnp.concatenate(
        [jnp.full((1, BLK), sinks_ref[4 * hk + g], F32) for g in range(4)], axis=1)
    m = jnp.maximum(jnp.max(s_t, axis=0, keepdims=True), sink_row)
    p_un = jnp.exp(s_t - m)
    e_sink = jnp.exp(sink_row - m)
    inv = 1.0 / (jnp.sum(p_un, axis=0, keepdims=True) + e_sink)
    return (p_un * inv).astype(BF16), e_sink * inv


def _unstack_heads(x4, hk, pairs):
    for g in range(4):
        _, p, pos = _head_place(hk, g)
        xg = x4[BLK * g:BLK * (g + 1)]
        if pos != hk:
            xg = pltpu.roll(xg, 64, 1)
        pairs[p] = xg if pairs[p] is None else pairs[p] + xg
    return pairs


def _attn_fwd(q, kband, vband, sinks_ref, bias, save):
    pairs = [None] * 4
    vband_t = vband.T
    for hk in range(N_KV):
        prob_t, p_sink = _attn_probs(_stack_q(q, hk), _head_lanes(kband, hk).astype(BF16), hk, sinks_ref, bias)
        save(hk, prob_t, p_sink)
        o_t = _dot(_head_rows(vband_t, hk), prob_t)
        pairs = _unstack_heads(o_t.T, hk, pairs)
    return jnp.concatenate(pairs, axis=1)


def _tril_mask():
    r = lax.broadcasted_iota(jnp.int32, (BLK, BLK), 0)
    c = lax.broadcasted_iota(jnp.int32, (BLK, BLK), 1)
    return c <= r


def _sgu_fwd(u, v, vn_g, vn_b, wt, bsb_ref):
    vhat, rstd = _ln_stats(v)
    vn = vhat * vn_g + vn_b
    mixed = jnp.concatenate(
        [_dot(wt[g], vn[:, BLK * g:BLK * (g + 1)].astype(BF16)) + bsb_ref[g] for g in range(4)], axis=1)
    return vhat, rstd, vn, mixed


def _cols(ref, rows, col, width):
    return ref[rows, col:col + width].astype(F32)


def _band(hm_ref, hprev_ref, s, col):
    r0 = s * BLK
    cur = hm_ref[r0:r0 + BLK, col:col + KV_W]
    if s == 0:
        off = 0 if col == C_K else KV_W
        prev = hprev_ref[:, off:off + KV_W]
    else:
        prev = hm_ref[r0 - BLK:r0, col:col + KV_W]
    return jnp.concatenate([prev, cur], axis=0).astype(F32)


def _mixer_in_specs(nt, rev):
    def tile(g):
        return nt - 1 - g if rev else g

    return [
        pl.BlockSpec(memory_space=pltpu.SMEM),
        pl.BlockSpec((TM, MAIN_W), lambda g: (tile(g), 0)),
        pl.BlockSpec((BLK, 2 * KV_W), lambda g: (jnp.maximum(tile(g) * NB - 1, 0), 2)),
        pl.BlockSpec((2, 2 * BLK, 4 * BLK), lambda g: (0, 0, 0)),
    ]


_CONST2 = lambda g: (0, 0)
_CONST3 = lambda g: (0, 0, 0)


def _ln_fwd(x, g, b, name):
    t = x.shape[0]

    def body(x_ref, g_ref, b_ref, o_ref):
        xhat, _ = _ln_stats(x_ref[...])
        o_ref[...] = xhat * g_ref[...] + b_ref[...]

    return pl.pallas_call(
        body, name=name, grid=(t // TM_EW,),
        in_specs=[pl.BlockSpec((TM_EW, D), lambda i: (i, 0)), pl.BlockSpec((1, D), _CONST2),
                  pl.BlockSpec((1, D), _CONST2)],
        out_specs=pl.BlockSpec((TM_EW, D), lambda i: (i, 0)),
        out_shape=jax.ShapeDtypeStruct((t, D), F32), compiler_params=_ARB,
    )(x, g.reshape(1, D), b.reshape(1, D))


def _inproj(x, w_t, b, name):
    t = x.shape[0]

    def body(x_ref, wt_ref, b_ref, hm_ref, hr_ref):
        xb = x_ref[...].astype(BF16)
        hm_ref[...] = (_dot_nt(xb, wt_ref[0:MAIN_W, :]) + b_ref[:, 0:MAIN_W]).astype(BF16)
        hr_ref[...] = (_dot_nt(xb, wt_ref[MAIN_W:IN_COLS, :]) + b_ref[:, MAIN_W:IN_COLS]).astype(BF16)

    return pl.pallas_call(
        body, name=name, grid=(t // TM_MM,),
        in_specs=[pl.BlockSpec((TM_MM, D), lambda i: (i, 0)),
                  pl.BlockSpec((IN_COLS, D), _CONST2), pl.BlockSpec((1, IN_COLS), _CONST2)],
        out_specs=[pl.BlockSpec((TM_MM, MAIN_W), lambda i: (i, 0)), pl.BlockSpec((TM_MM, R_W), lambda i: (i, 0))],
        out_shape=[jax.ShapeDtypeStruct((t, MAIN_W), BF16), jax.ShapeDtypeStruct((t, R_W), BF16)],
        compiler_params=_ARB,
    )(x, w_t, b)


def _mixer_fwd(hm, sinks, bias, vn_g, vn_b, w_s, bsb, nblk_seq, name):
    t = hm.shape[0]
    nt = t // TM

    def body(sinks_ref, hm_ref, hprev_ref, bias_ref, vng_ref, vnb_ref, ws_ref, bsb_ref,
             ya_ref, yb_ref, prob_ref, psink_ref):
        i = pl.program_id(0)
        tril = _tril_mask()
        wt = [jnp.where(tril, ws_ref[g], 0.0).astype(BF16) for g in range(4)]
        for s in range(NB):
            r0 = s * BLK
            rows = slice(r0, r0 + BLK)
            bias = bias_ref[jnp.where((i * NB + s) % nblk_seq == 0, 1, 0)]

            def save(hk, prob_t, p_sink, s=s):
                prob_ref[N_KV * s + hk] = prob_t
                psink_ref[N_KV * s + hk] = jnp.broadcast_to(p_sink, (8, 4 * BLK))

            attn = _attn_fwd(_cols(hm_ref, rows, C_Q, Q_W), _band(hm_ref, hprev_ref, s, C_K),
                             _band(hm_ref, hprev_ref, s, C_V), sinks_ref, bias, save)
            g_a = _cols(hm_ref, rows, C_GA, Q_W)
            ya_ref[rows, :] = (attn * (g_a * _sigmoid(g_a))).astype(BF16)
            u = _gelu(_cols(hm_ref, rows, C_UB, SGU_W))
            mixed = _sgu_fwd(u, _gelu(_cols(hm_ref, rows, C_VB, SGU_W)), vng_ref[...], vnb_ref[...], wt, bsb_ref)[-1]
            g_b = _cols(hm_ref, rows, C_GB, SGU_W)
            yb_ref[rows, :] = (u * mixed * (g_b * _sigmoid(g_b))).astype(BF16)

    ngrp = N_KV * NB
    return pl.pallas_call(
        body, name=name, grid=(nt,),
        in_specs=_mixer_in_specs(nt, False) + [
            pl.BlockSpec((1, SGU_W), _CONST2), pl.BlockSpec((1, SGU_W), _CONST2),
            pl.BlockSpec((4, BLK, BLK), _CONST3), pl.BlockSpec((4, BLK, BLK), _CONST3)],
        out_specs=[pl.BlockSpec((TM, Q_W), lambda i: (i, 0)), pl.BlockSpec((TM, SGU_W), lambda i: (i, 0)),
                   pl.BlockSpec((ngrp, 2 * BLK, 4 * BLK), lambda i: (i, 0, 0)),
                   pl.BlockSpec((ngrp, 8, 4 * BLK), lambda i: (i, 0, 0))],
        out_shape=[jax.ShapeDtypeStruct((t, Q_W), BF16), jax.ShapeDtypeStruct((t, SGU_W), BF16),
                   jax.ShapeDtypeStruct((nt * ngrp, 2 * BLK, 4 * BLK), BF16),
                   jax.ShapeDtypeStruct((nt * ngrp, 8, 4 * BLK), F32)],
        compiler_params=_ARB,
    )(sinks, hm, hm, bias, vn_g, vn_b, w_s, bsb)


def _tail_fwd(x, ya, yb, hr, pa_w, pb_w, wo, b_out, ln_g, ln_b, name, last):
    t = x.shape[0]

    def body(x_ref, ya_ref, yb_ref, hr_ref, paw_ref, pbw_ref, wo_ref, bo_ref, g_ref, b_ref,
             pa_ref, pb_ref, mg_ref, z_ref, *xn_ref):
        pa = _dot(ya_ref[...], paw_ref[...])
        pb = _dot(yb_ref[...], pbw_ref[...])
        pa_ref[...] = pa.astype(BF16)
        pb_ref[...] = pb.astype(BF16)
        everything = slice(None)
        merged = _sigmoid(_cols(hr_ref, everything, 0, D)) * pa + _sigmoid(_cols(hr_ref, everything, D, D)) * pb
        mb = merged.astype(BF16)
        mg_ref[...] = mb
        z = ALPHA * x_ref[...] + (_dot(mb, wo_ref[...]) + bo_ref[...])
        z_ref[...] = z
        if not last:
            zhat, _ = _ln_stats(z)
            xn_ref[0][...] = zhat * g_ref[...] + b_ref[...]

    row = lambda w: pl.BlockSpec((TM, w), lambda i: (i, 0))
    vec = pl.BlockSpec((1, D), _CONST2)
    n_f32 = 1 if last else 2
    return pl.pallas_call(
        body, name=name, grid=(t // TM,),
        in_specs=[row(D), row(Q_W), row(SGU_W), row(R_W),
                  pl.BlockSpec((Q_W, D), _CONST2), pl.BlockSpec((SGU_W, D), _CONST2), pl.BlockSpec((D, D), _CONST2),
                  vec, vec, vec],
        out_specs=[row(D)] * (3 + n_f32),
        out_shape=[jax.ShapeDtypeStruct((t, D), BF16)] * 3 + [jax.ShapeDtypeStruct((t, D), F32)] * n_f32,
        compiler_params=_ARB,
    )(x, ya, yb, hr, pa_w, pb_w, wo, b_out, ln_g, ln_b)


def _tail_bwd(dxn, z, pa, pb, hr, wo, pa_w, pb_w, ln_g, ln_b, name, from_loss):
    t = dxn.shape[0]

    def body(dxn_ref, z_ref, pa_ref, pb_ref, hr_ref, wo_ref, paw_ref, pbw_ref, g_ref, b_ref,
             dz_ref, dpa_ref, dpb_ref, dhr_ref, dya_ref, dyb_ref, acc_ref, gbr_ref):
        @pl.when(pl.program_id(0) == 0)
        def _():
            acc_ref[...] = jnp.zeros_like(acc_ref)
            gbr_ref[...] = jnp.zeros_like(gbr_ref)

        zhat, rstd = _ln_stats(z_ref[...])
        if from_loss:
            err = zhat * g_ref[...] + b_ref[...] - dxn_ref[...]
            dxn_v = err * (1.0 / D)
            sq = jnp.sum(jnp.sum(err * err, axis=1, keepdims=True), axis=0, keepdims=True)
            acc_ref[3:4, :] += jnp.broadcast_to(sq, (1, D))
        else:
            dxn_v = dxn_ref[...]
        dz = _ln_bwd(dxn_v * g_ref[...], zhat, rstd)
        dz_ref[...] = dz
        acc_ref[0:1, :] += _colsum(dxn_v * zhat)
        acc_ref[1:2, :] += _colsum(dxn_v)
        acc_ref[2:3, :] += _colsum(dz)
        dmerged = _dot_nt(dz.astype(BF16), wo_ref[...])
        everything = slice(None)
        sa = _sigmoid(_cols(hr_ref, everything, 0, D))
        sb = _sigmoid(_cols(hr_ref, everything, D, D))
        dpa = (dmerged * sa).astype(BF16)
        dpb = (dmerged * sb).astype(BF16)
        dpa_ref[...] = dpa
        dpb_ref[...] = dpb
        dra = dmerged * pa_ref[...].astype(F32) * (sa * (1.0 - sa))
        drb = dmerged * pb_ref[...].astype(F32) * (sb * (1.0 - sb))
        dhr_ref[:, 0:D] = dra.astype(BF16)
        dhr_ref[:, D:2 * D] = drb.astype(BF16)
        gbr_ref[0:1, 0:D] += _colsum(dra)
        gbr_ref[0:1, D:2 * D] += _colsum(drb)
        dya_ref[...] = _dot_nt(dpa, paw_ref[...]).astype(BF16)
        dyb_ref[...] = _dot_nt(dpb, pbw_ref[...]).astype(BF16)

    row = lambda w: pl.BlockSpec((TM, w), lambda i: (i, 0))
    vec = pl.BlockSpec((1, D), _CONST2)
    return pl.pallas_call(
        body, name=name, grid=(t // TM,),
        in_specs=[row(D), row(D), row(D), row(D), row(R_W),
                  pl.BlockSpec((D, D), _CONST2), pl.BlockSpec((Q_W, D), _CONST2), pl.BlockSpec((SGU_W, D), _CONST2),
                  vec, vec],
        out_specs=[row(D), row(D), row(D), row(R_W), row(Q_W), row(SGU_W), pl.BlockSpec((8, D), _CONST2),
                   pl.BlockSpec((8, R_W), _CONST2)],
        out_shape=[jax.ShapeDtypeStruct((t, D), F32), jax.ShapeDtypeStruct((t, D), BF16),
                   jax.ShapeDtypeStruct((t, D), BF16), jax.ShapeDtypeStruct((t, R_W), BF16),
                   jax.ShapeDtypeStruct((t, Q_W), BF16), jax.ShapeDtypeStruct((t, SGU_W), BF16),
                   jax.ShapeDtypeStruct((8, D), F32), jax.ShapeDtypeStruct((8, R_W), F32)],
        compiler_params=_ARB,
    )(dxn, z, pa, pb, hr, wo, pa_w, pb_w, ln_g, ln_b)


def _mixer_bwd(hm, dya, dyb, prob, psink, vn_g, vn_b, w_s, bsb, name):
    t = hm.shape[0]
    nt = t // TM
    ngrp = N_KV * NB

    def body(hm_ref, hprev_ref, prob_ref, psink_ref, dya_ref, dyb_ref, vng_ref, vnb_ref, ws_ref, bsb_ref,
             dhm_ref, gbm_ref, gsk_ref, gvn_ref, gws_ref, gbs_ref, dk_carry, dv_carry):
        gi = pl.program_id(0)

        @pl.when(gi == 0)
        def _():
            for r in (gbm_ref, gsk_ref, gvn_ref, gws_ref, gbs_ref, dk_carry, dv_carry):
                r[...] = jnp.zeros_like(r)

        tril = _tril_mask()
        wt = [jnp.where(tril, ws_ref[g], 0.0).astype(BF16) for g in range(4)]
        vng = vng_ref[...]
        ones8 = jnp.ones((8, BLK), BF16)

        def put(rows, col, val):
            dhm_ref[rows, col:col + val.shape[1]] = val.astype(BF16)

        for s in reversed(range(NB)):
            r0 = s * BLK
            rows = slice(r0, r0 + BLK)
            q = _cols(hm_ref, rows, C_Q, Q_W)
            kband = _band(hm_ref, hprev_ref, s, C_K)
            vband = _band(hm_ref, hprev_ref, s, C_V)
            g_a = _cols(hm_ref, rows, C_GA, Q_W)
            sg = _sigmoid(g_a)
            dya_v = _cols(dya_ref, rows, 0, Q_W)
            d_o = dya_v * (g_a * sg)
            o_pairs, dq_pairs = [None] * 4, [None] * 4
            dkband = jnp.zeros((2 * BLK, KV_W), F32)
            dvband = jnp.zeros((2 * BLK, KV_W), F32)
            kband_t, vband_t = kband.T, vband.T
            for hk in range(N_KV):
                q4 = _stack_q(q, hk)
                prob_b = prob_ref[N_KV * s + hk]
                p_sink = psink_ref[N_KV * s + hk][0:1, :]
                o_t = _dot(_head_rows(vband_t, hk), prob_b)
                o_pairs = _unstack_heads(o_t.T, hk, o_pairs)
                parts = []
                for g in range(4):
                    _, p, pos = _head_place(hk, g)
                    dp = d_o[:, BLK * p:BLK * (p + 1)]
                    parts.append(pltpu.roll(dp, 64, 1) if pos != hk else dp)
                do4 = _head_lanes(jnp.concatenate(parts, axis=0), hk)
                do4b = do4.astype(BF16)
                delta = _colsum(do4.T * o_t)
                vh = _head_lanes(vband, hk).astype(BF16)
                ds_t = prob_b.astype(F32) * (_dot_nt(vh, do4b) - delta)
                dsb = ds_t.astype(BF16)
                dq4_t = _dot(_head_rows(kband_t, hk), dsb)
                dq_pairs = _unstack_heads(dq4_t.T * ATTN_SCALE, hk, dq_pairs)
                dkband = dkband + _head_lanes(_dot(dsb, q4), hk)
                dvband = dvband + _dot(prob_b, do4b)
                dsk = p_sink * delta
                for g in range(4):
                    j = 4 * hk + g
                    tot = jnp.sum(dsk[:, BLK * g:BLK * (g + 1)], axis=1, keepdims=True)
                    gsk_ref[j:j + 1, :] += jnp.broadcast_to(-tot, (1, 128))
            attn = jnp.concatenate(o_pairs, axis=1)
            put(rows, C_Q, jnp.concatenate(dq_pairs, axis=1))
            put(rows, C_K, dkband[BLK:2 * BLK] + dk_carry[...])
            put(rows, C_V, dvband[BLK:2 * BLK] + dv_carry[...])
            dk_carry[...] = dkband[0:BLK]
            dv_carry[...] = dvband[0:BLK]
            put(rows, C_GA, dya_v * attn * (sg * (1.0 + g_a * (1.0 - sg))))
            u, du_du_b = _gelu_and_grad(_cols(hm_ref, rows, C_UB, SGU_W))
            v, dv_dv_b = _gelu_and_grad(_cols(hm_ref, rows, C_VB, SGU_W))
            g_b = _cols(hm_ref, rows, C_GB, SGU_W)
            vhat, rstd, vn, mixed = _sgu_fwd(u, v, vng, vnb_ref[...], wt, bsb_ref)
            sgb = _sigmoid(g_b)
            silu_b = g_b * sgb
            dyb_v = _cols(dyb_ref, rows, 0, SGU_W)
            du = dyb_v * mixed * silu_b
            dmixed = dyb_v * u * silu_b
            put(rows, C_GB, dyb_v * u * mixed * (sgb * (1.0 + g_b * (1.0 - sgb))))
            dvn_parts = []
            for g in range(4):
                cols = slice(BLK * g, BLK * (g + 1))
                dmg = dmixed[:, cols]
                dmgb = dmg.astype(BF16)
                dvn_parts.append(_dot_tn(wt[g], dmgb))
                gws_ref[g] += jnp.where(tril, _dot_nt(dmgb, vn[:, cols].astype(BF16)), 0.0)
                gbs_ref[g] += dmg
            dvn = jnp.concatenate(dvn_parts, axis=1)
            gvn_ref[0:1, :] += _colsum(dvn * vhat)
            gvn_ref[1:2, :] += _colsum(dvn)
            dv = _ln_bwd(dvn * vng, vhat, rstd)
            put(rows, C_UB, du * du_du_b)
            put(rows, C_VB, dv * dv_dv_b)
            gbm_ref[...] += _dot(ones8, dhm_ref[rows, :])

        @pl.when(gi == nt - 1)
        def _():
            for g in range(4):
                gbs_ref[g] = jnp.broadcast_to(jnp.sum(gbs_ref[g], axis=1, keepdims=True), (BLK, BLK))

    row = lambda w: pl.BlockSpec((TM, w), lambda g: (nt - 1 - g, 0))
    return pl.pallas_call(
        body, name=name, grid=(nt,),
        in_specs=_mixer_in_specs(nt, True)[1:3] + [
            pl.BlockSpec((ngrp, 2 * BLK, 4 * BLK), lambda g: (nt - 1 - g, 0, 0)),
            pl.BlockSpec((ngrp, 8, 4 * BLK), lambda g: (nt - 1 - g, 0, 0)),
            row(Q_W), row(SGU_W),
            pl.BlockSpec((1, SGU_W), _CONST2), pl.BlockSpec((1, SGU_W), _CONST2),
            pl.BlockSpec((4, BLK, BLK), _CONST3), pl.BlockSpec((4, BLK, BLK), _CONST3)],
        out_specs=[row(MAIN_W), pl.BlockSpec((8, MAIN_W), _CONST2), pl.BlockSpec((8, 128), _CONST2),
                   pl.BlockSpec((8, SGU_W), _CONST2), pl.BlockSpec((4, BLK, BLK), _CONST3),
                   pl.BlockSpec((4, BLK, BLK), _CONST3)],
        out_shape=[jax.ShapeDtypeStruct((t, MAIN_W), BF16), jax.ShapeDtypeStruct((8, MAIN_W), F32),
                   jax.ShapeDtypeStruct((8, 128), F32), jax.ShapeDtypeStruct((8, SGU_W), F32),
                   jax.ShapeDtypeStruct((4, BLK, BLK), F32), jax.ShapeDtypeStruct((4, BLK, BLK), F32)],
        scratch_shapes=[pltpu.VMEM((BLK, KV_W), F32), pltpu.VMEM((BLK, KV_W), F32)],
        compiler_params=_ARB,
    )(hm, hm, prob, psink, dya, dyb, vn_g, vn_b, w_s, bsb)


def _dx_inproj(dz, dhm, dhr, w_t, after, name):
    t = dz.shape[0]

    def body(dz_ref, dhm_ref, dhr_ref, wt_ref, after_ref, dx_ref):
        dx_ref[...] = (ALPHA * dz_ref[...] + after_ref[0:1, 0:1] + _dot(dhm_ref[...], wt_ref[0:MAIN_W, :])
                       + _dot(dhr_ref[...], wt_ref[MAIN_W:IN_COLS, :]))

    row = lambda w: pl.BlockSpec((TM_MM, w), lambda i: (i, 0))
    return pl.pallas_call(
        body, name=name, grid=(t // TM_MM,),
        in_specs=[row(D), row(MAIN_W), row(R_W), pl.BlockSpec((IN_COLS, D), _CONST2),
                  pl.BlockSpec((8, 128), _CONST2)],
        out_specs=row(D), out_shape=jax.ShapeDtypeStruct((t, D), F32), compiler_params=_ARB,
    )(dz, dhm, dhr, w_t, after)


def _wgrad(a, b, tm, name, tk=1024):
    t, m = a.shape
    n = b.shape[1]
    nk = t // tk

    def body(a_ref, b_ref, o_ref, acc_ref):
        k = pl.program_id(1)

        @pl.when(k == 0)
        def _():
            acc_ref[...] = jnp.zeros_like(acc_ref)

        acc_ref[...] += _dot_tn(a_ref[...].astype(BF16), b_ref[...].astype(BF16))

        @pl.when(k == nk - 1)
        def _():
            o_ref[...] = acc_ref[...].astype(BF16)

    return pl.pallas_call(
        body, name=name, grid=(m // tm, nk),
        in_specs=[pl.BlockSpec((tk, tm), lambda j, k: (k, j)), pl.BlockSpec((tk, n), lambda j, k: (k, 0))],
        out_specs=pl.BlockSpec((tm, n), lambda j, k: (j, 0)),
        out_shape=jax.ShapeDtypeStruct((m, n), BF16),
        scratch_shapes=[pltpu.VMEM((tm, n), F32)],
        compiler_params=pltpu.CompilerParams(dimension_semantics=("arbitrary", "arbitrary"), vmem_limit_bytes=VMEM_LIMIT),
    )(a, b)


def _ln_in_bwd(dx0, x, g, name):
    t = x.shape[0]

    def body(dx0_ref, x_ref, g_ref, gx_ref, acc_ref):
        @pl.when(pl.program_id(0) == 0)
        def _():
            acc_ref[...] = jnp.zeros_like(acc_ref)

        d = dx0_ref[...]
        xhat, rstd = _ln_stats(x_ref[...])
        gx_ref[...] = _ln_bwd(d * g_ref[...], xhat, rstd)
        acc_ref[0:1, :] += _colsum(d * xhat)
        acc_ref[1:2, :] += _colsum(d)

    row = pl.BlockSpec((TM_EW, D), lambda i: (i, 0))
    return pl.pallas_call(
        body, name=name, grid=(t // TM_EW,),
        in_specs=[row, row, pl.BlockSpec((1, D), _CONST2)],
        out_specs=[row, pl.BlockSpec((8, D), _CONST2)],
        out_shape=[jax.ShapeDtypeStruct((t, D), F32), jax.ShapeDtypeStruct((8, D), F32)],
        compiler_params=_ARB,
    )(dx0, x, g)


_ANY = pl.BlockSpec(memory_space=pl.ANY)


def _place():
    return lax.axis_index("x"), lax.axis_index("y"), lax.axis_index("c")


def _allgather8(xs, name):
    n = len(xs)

    def body(*refs):
        x_refs, o_refs = refs[:n], refs[n:2 * n]
        send_sems, recv_sems, local_sems = refs[2 * n:]
        x, y, c = _place()
        me, sib = (x, y, c), (x, y, 1 - c)
        chips = [(1 - x, y), (x, 1 - y), (1 - x, 1 - y)]

        def copy(a, k, block, to, src=None):
            dst = o_refs[a].at[4 * block[0] + 2 * block[1] + block[2]]
            return pltpu.make_async_remote_copy(
                src_ref=dst if src is None else src, dst_ref=dst, send_sem=send_sems.at[7 * a + k],
                recv_sem=recv_sems.at[7 * a + k], device_id=to, device_id_type=MESH)

        mine = [pltpu.make_async_copy(x_refs[a], o_refs[a].at[4 * x + 2 * y + c], local_sems.at[a]) for a in range(n)]
        for cp in mine:
            cp.start()
        sent = []
        for a in range(n):
            sent.append(copy(a, 0, me, sib, src=x_refs[a]))
            sent += [copy(a, 1 + j, me, (*chip, c), src=x_refs[a]) for j, chip in enumerate(chips)]
        for cp in sent:
            cp.start()
        for j, chip in enumerate(chips):
            for a in range(n):
                copy(a, 1 + j, (*chip, c), me).wait_recv()
                fwd = copy(a, 4 + j, (*chip, c), sib)
                fwd.start()
                sent.append(fwd)
        for a in range(n):
            copy(a, 0, sib, me).wait_recv()
            for j, chip in enumerate(chips):
                copy(a, 4 + j, (*chip, 1 - c), me).wait_recv()
        for cp in sent:
            cp.wait_send()
        for cp in mine:
            cp.wait()

    return pl.pallas_call(
        body, name=name, in_specs=[_ANY] * n, out_specs=[_ANY] * n,
        out_shape=[jax.ShapeDtypeStruct((N_DEV,) + v.shape, v.dtype) for v in xs],
        scratch_shapes=[pltpu.SemaphoreType.DMA((7 * n,)), pltpu.SemaphoreType.DMA((7 * n,)),
                        pltpu.SemaphoreType.DMA((n,))],
    )(*xs)


def _forward_sibling(lands, name):
    n = len(lands)

    def body(*refs):
        l_refs = refs[n:2 * n]
        send_sems, recv_sems = refs[2 * n:]
        x, y, c = _place()
        chips = [(1 - x, y), (x, 1 - y), (1 - x, 1 - y)]

        def copy(a, j, core):
            rows = l_refs[a].at[4 * chips[j][0] + 2 * chips[j][1] + core]
            return pltpu.make_async_remote_copy(
                src_ref=rows, dst_ref=rows, send_sem=send_sems.at[3 * a + j], recv_sem=recv_sems.at[3 * a + j],
                device_id=(x, y, 1 - c), device_id_type=MESH)

        for a in range(n):
            for j in range(3):
                copy(a, j, c).start()
        for a in range(n):
            for j in range(3):
                copy(a, j, 1 - c).wait_recv()
                copy(a, j, c).wait_send()

    return pl.pallas_call(
        body, name=name, in_specs=[_ANY] * n, out_specs=[_ANY] * n,
        out_shape=[jax.ShapeDtypeStruct(v.shape, v.dtype) for v in lands],
        input_output_aliases={a: a for a in range(n)},
        scratch_shapes=[pltpu.SemaphoreType.DMA((3 * n,)), pltpu.SemaphoreType.DMA((3 * n,))],
    )(*lands)


def _swap_sibling(gs, name):
    n = len(gs)
    first = [0]
    for v in gs:
        first.append(first[-1] + v.shape[0])

    def body(*refs):
        g_refs, r_refs = refs[:n], refs[n:2 * n]
        send_sems, recv_sems = refs[2 * n:]
        x, y, c = _place()
        cps = [pltpu.make_async_remote_copy(
            src_ref=g_refs[a].at[q, 1 - c], dst_ref=r_refs[a].at[q], send_sem=send_sems.at[first[a] + q],
            recv_sem=recv_sems.at[first[a] + q], device_id=(x, y, 1 - c), device_id_type=MESH)
            for a in range(n) for q in range(gs[a].shape[0])]
        for cp in cps:
            cp.start()
        for cp in cps:
            cp.wait()

    return pl.pallas_call(
        body, name=name, in_specs=[_ANY] * n, out_specs=[_ANY] * n,
        out_shape=[jax.ShapeDtypeStruct(v.shape[:1] + v.shape[2:], v.dtype) for v in gs],
        scratch_shapes=[pltpu.SemaphoreType.DMA((first[-1],)), pltpu.SemaphoreType.DMA((first[-1],))],
    )(*gs)


def _row_tile(rows, lanes, cap):
    if rows * lanes * 4 <= (1 << 20):
        return rows
    return max(d for d in range(8, cap + 1, 8) if rows % d == 0 and (d % 16 == 0 or rows % 16 != 0))


def _pair_sum(g, r, name):
    n, _, rows, l = g.shape
    tr = _row_tile(rows, l, 608)

    def body(c_ref, g_ref, r_ref, o_ref):
        o_ref[...] = (g_ref[0].astype(F32) + r_ref[...].astype(F32)).astype(o_ref.dtype)

    return pl.pallas_call(
        body, name=name,
        grid_spec=pltpu.PrefetchScalarGridSpec(
            num_scalar_prefetch=1, grid=(n, rows // tr),
            in_specs=[pl.BlockSpec((1, 1, tr, l), lambda q, i, c_ref: (q, c_ref[0], i, 0)),
                      pl.BlockSpec((1, tr, l), lambda q, i, c_ref: (q, i, 0))],
            out_specs=pl.BlockSpec((1, tr, l), lambda q, i, c_ref: (q, i, 0))),
        out_shape=jax.ShapeDtypeStruct((n, rows, l), g.dtype),
        compiler_params=pltpu.CompilerParams(dimension_semantics=("arbitrary", "arbitrary"), vmem_limit_bytes=VMEM_LIMIT),
    )(lax.axis_index("c").astype(jnp.int32).reshape(1), g, r)


def _adamw(parts, w, m, v, name, own=None):
    nl = len(parts)
    ns, rows, l = parts[0].shape
    tr = _row_tile(rows, l * ns, 304)
    nt = rows // tr
    c1 = 1.0 - ADAM_B1 ** ADAM_STEP
    c2 = 1.0 - ADAM_B2 ** ADAM_STEP

    def body(q_ref, *refs):
        own_refs = refs[:nl] if own is not None else None
        p_refs = refs[-7 - nl:-7]
        w_ref, m_ref, v_ref, g_ref, d_ref, nm_ref, nv_ref = refs[-7:]
        layer = pl.program_id(0)
        g = None
        for j in range(nl):
            gj = None
            for k in range(ns):
                term = p_refs[j][k].astype(F32)
                if own_refs is not None:
                    term = jnp.where(q_ref[0] == k, own_refs[j][0].astype(F32), term)
                gj = term if gj is None else gj + term
            g = gj if g is None else jnp.where(layer == j, gj, g)
        g_ref[...] = g
        nm = ADAM_B1 * m_ref[...] + (1.0 - ADAM_B1) * g
        nv = ADAM_B2 * v_ref[...] + (1.0 - ADAM_B2) * (g * g)
        nm_ref[...] = nm
        nv_ref[...] = nv
        d_ref[...] = -ADAM_LR * ((nm / c1) / (jnp.sqrt(nv / c2) + ADAM_EPS) + ADAM_WD * w_ref[...])

    def tile_of(j):
        return lambda la, i, q: jnp.where(la == j, i, jnp.where(la < j, 0, nt - 1))

    row = pl.BlockSpec((tr, l), lambda la, i, q: (la * nt + i, 0))
    own_specs = [] if own is None else [
        pl.BlockSpec((1, tr, l), lambda la, i, q, j=j: (q[0], tile_of(j)(la, i, q), 0)) for j in range(nl)]
    part_specs = [pl.BlockSpec((ns, tr, l), lambda la, i, q, j=j: (0, tile_of(j)(la, i, q), 0)) for j in range(nl)]
    chip = (2 * lax.axis_index("x") + lax.axis_index("y")).astype(jnp.int32).reshape(1)
    return pl.pallas_call(
        body, name=name,
        grid_spec=pltpu.PrefetchScalarGridSpec(
            num_scalar_prefetch=1, grid=(nl, nt),
            in_specs=own_specs + part_specs + [row, row, row], out_specs=[row] * 4),
        out_shape=[jax.ShapeDtypeStruct((nl * rows, l), F32)] * 4,
        compiler_params=pltpu.CompilerParams(dimension_semantics=("arbitrary", "arbitrary"), vmem_limit_bytes=VMEM_LIMIT),
    )(chip, *([] if own is None else own), *parts, w, m, v)


_HBM = pl.BlockSpec(memory_space=pltpu.HBM)
_SEM = pl.BlockSpec(memory_space=pltpu.SEMAPHORE)
_EFFECT = pltpu.SideEffectType.DATAFLOW_SIDE_EFFECTING


def _plan_all(x, y, c):
    me = 4 * x + 2 * y + c
    peers = [(x, y, 1 - c), (1 - x, y, c), (x, 1 - y, c), (1 - x, 1 - y, c),
             (1 - x, y, 1 - c), (x, 1 - y, 1 - c), (1 - x, 1 - y, 1 - c)]
    return [(None, me, p, 4 * p[0] + 2 * p[1] + p[2]) for p in peers]


def _plan_near(x, y, c):
    me = 4 * x + 2 * y + c
    peers = [(x, y, 1 - c), (1 - x, y, c), (x, 1 - y, c), (1 - x, 1 - y, c)]
    return [(None, me, p, 4 * p[0] + 2 * p[1] + p[2]) for p in peers]


def _plan_chips(x, y, c):
    me = 2 * x + y
    return [(2 * qx + qy, me, (qx, qy, c), 2 * qx + qy) for qx, qy in ((1 - x, y), (x, 1 - y), (1 - x, 1 - y))]


def _split_copies(plan, src_refs, land_refs, send_sems, recv_sems, arrival):
    n = len(src_refs)
    entries = plan(*_place())
    per = len(entries)
    cps = []
    for a in range(n):
        for k, (src_slot, dst_slot, peer, back_slot) in enumerate(entries):
            src = src_refs[a] if src_slot is None else src_refs[a].at[src_slot]
            cps.append(pltpu.make_async_remote_copy(
                src_ref=src, dst_ref=land_refs[a].at[back_slot if arrival else dst_slot],
                send_sem=send_sems.at[per * a + k], recv_sem=recv_sems.at[per * a + k],
                device_id=peer, device_id_type=MESH))
    return cps


def _split_start(srcs, lands, plan, per, name):
    n = len(srcs)

    def body(*refs):
        for cp in _split_copies(plan, refs[:n], refs[n:2 * n], refs[2 * n], refs[2 * n + 1], False):
            cp.start()
        refs[-1][...] = jnp.zeros_like(refs[-1])

    both = list(srcs) + list(lands)
    outs = pl.pallas_call(
        body, name=name,
        out_shape=(pltpu.SemaphoreType.DMA((per * n,)), pltpu.SemaphoreType.DMA((per * n,)),
                   *[pltpu.HBM(v.shape, v.dtype) for v in both], jax.ShapeDtypeStruct((8, 128), F32)),
        in_specs=[_HBM] * (2 * n),
        out_specs=(_SEM, _SEM, *[_HBM] * (2 * n), pl.BlockSpec(memory_space=pltpu.VMEM)),
        input_output_aliases={i: 2 + i for i in range(2 * n)},
        compiler_params=pltpu.CompilerParams(has_side_effects=_EFFECT),
    )(*[pltpu.with_memory_space_constraint(v, pltpu.HBM) for v in both])
    return outs[0], outs[1], list(outs[2:2 + 2 * n]), outs[-1]


def _split_wait(send_sems, recv_sems, thru, plan, after, name):
    n = len(thru) // 2

    def body(*refs):
        for cp in _split_copies(plan, refs[:n], refs[n:2 * n], refs[2 * n], refs[2 * n + 1], True):
            cp.wait_send()
            cp.wait_recv()

    outs = pl.pallas_call(
        body, name=name, out_shape=tuple(pltpu.HBM(v.shape, v.dtype) for v in thru),
        in_specs=[_HBM] * (2 * n) + [_SEM, _SEM, pl.BlockSpec(memory_space=pl.ANY)],
        out_specs=[_HBM] * (2 * n), input_output_aliases={i: i for i in range(2 * n)},
        compiler_params=pltpu.CompilerParams(has_side_effects=_EFFECT),
    )(*thru, send_sems, recv_sems, after)
    return list(outs[:n]), list(outs[n:])


_SMALL_IN = ("ln_in_g", "ln_in_b")
_SMALL = ("b_in", "sinks", "vn_g", "vn_b", "w_s", "b_s", "b_out", "ln_g", "ln_b")


def _rows128(a):
    flat = a.reshape(-1)
    return jnp.pad(flat, (0, (-flat.shape[0]) % 128)).reshape(-1, 128)


def _pack_small(d, names):
    rows = jnp.concatenate([_rows128(d[n]) for n in names])
    return jnp.pad(rows, ((0, (-rows.shape[0]) % 8), (0, 0)))


def _unpack_small(p, like, names):
    off, out = 0, {}
    for n in names:
        size = like[n].size
        rows = -(-size // 128)
        out[n] = p[off:off + rows].reshape(-1)[:size].reshape(like[n].shape)
        off += rows
    return out


def _owner_blocks(g, axis):
    sh = g.shape
    g = g.reshape(sh[:axis] + (4, 2, sh[axis] // N_DEV) + sh[axis + 1:])
    return jnp.moveaxis(g, (axis, axis + 1), (0, 1))


def kernel(x, ln_in_g, ln_in_b, w_in, b_in, sinks, vn_g, vn_b, w_s, b_s, p_a, p_b, w_out, b_out, ln_g, ln_b, loss_target, m_ln_in_g, m_ln_in_b, m_w_in, m_b_in, m_sinks, m_vn_g, m_vn_b, m_w_s, m_b_s, m_p_a, m_p_b, m_w_out, m_b_out, m_ln_g, m_ln_b, v_ln_in_g, v_ln_in_b, v_w_in, v_b_in, v_sinks, v_vn_g, v_vn_b, v_w_s, v_b_s, v_p_a, v_p_b, v_w_out, v_b_out, v_ln_g, v_ln_b):
    nseq, seq, _ = x.shape
    t = nseq * seq
    nblk_seq = seq // BLK
    x2 = x.reshape(t, D)
    tgt = loss_target.reshape(t, D)

    def turned(a):
        return jnp.swapaxes(a, 1, 2)

    w_in_t = turned(w_in)

    def blocks(l):
        return [w_in_t[l].astype(BF16), p_a[l].astype(BF16), p_b[l].astype(BF16), w_out[l].astype(BF16)]

    def full_weights(g):
        w_t_full = g[0].reshape(IN_COLS, D)
        pa_full = jnp.moveaxis(g[1], 0, 1).reshape(Q_W, D)
        pb_full = jnp.moveaxis(g[2], 0, 1).reshape(SGU_W, D)
        wo_full = g[3].reshape(D, D)
        return dict(w_t=w_t_full, pa=pa_full, pb=pb_full, wo=wo_full)

    def landing(bs):
        return [lax.empty((N_DEV,) + v.shape, v.dtype) for v in bs]

    def with_own(landed, sent):
        return [lax.dynamic_update_index_in_dim(g, b, me, 0) for g, b in zip(landed, sent)]

    me = 4 * lax.axis_index("x") + 2 * lax.axis_index("y") + lax.axis_index("c")
    blocks0 = blocks(0)
    a_send, a_recv, a_thru, a_token = _split_start(blocks0[:1], landing(blocks0[:1]), _plan_near, 4,
                                                   "allgather_w_in0_start")
    rest0 = [b + a_token[0, 0].astype(BF16) for b in blocks0[1:]]
    b_send, b_recv, b_thru, b_token = _split_start(rest0, landing(rest0), _plan_near, 4, "allgather_rest0_start")
    xs = [_ln_fwd(x2, ln_in_g + b_token[0, 0], ln_in_b, "ln_in_fwd")]
    sent, landed = _split_wait(a_send, a_recv, a_thru, _plan_near, xs[0], "allgather_w_in0_wait")
    gathered0 = with_own(_forward_sibling(landed, "allgather_w_in0_forward"), sent)
    blocks1, gathered0 = lax.optimization_barrier((blocks(1), gathered0))
    ag_send, ag_recv, ag_thru, ag_token = _split_start(blocks1, landing(blocks1), _plan_all, 7,
                                                       "allgather_weights1_start")
    weights = [None, None]
    bsb = jnp.broadcast_to(b_s[:, :, :, None], (DEPTH, 4, BLK, BLK))
    bias = _band_bias()

    saved = []
    for l in range(DEPTH):
        if l == 1:
            sent, landed = _split_wait(ag_send, ag_recv, ag_thru, _plan_all, xs[1], "allgather_weights1_wait")
            weights[1] = full_weights(with_own(landed, sent))
        w_t = weights[l]["w_t"] if l else gathered0[0].reshape(IN_COLS, D)
        last = l == DEPTH - 1
        b_l = b_in[l].reshape(1, -1) + (ag_token[0, 0] if l == 0 else 0.0)
        hm, hr = _inproj(xs[l], w_t, b_l, f"inproj{l}")
        ya, yb, prob, psink = _mixer_fwd(hm, sinks[l], bias, vn_g[l].reshape(1, -1), vn_b[l].reshape(1, -1),
                                         w_s[l], bsb[l], nblk_seq, f"mixer_fwd{l}")
        if l == 0:
            sent, landed = _split_wait(b_send, b_recv, b_thru, _plan_near, ya, "allgather_rest0_wait")
            weights[0] = full_weights(gathered0 + with_own(_forward_sibling(landed, "allgather_rest0_forward"), sent))
        wl = weights[l]
        outs = _tail_fwd(xs[l], ya, yb, hr, wl["pa"], wl["pb"], wl["wo"], b_out[l].reshape(1, D),
                         ln_g[l].reshape(1, D), ln_b[l].reshape(1, D), f"tail_fwd{l}", last)
        saved.append((hm, hr, ya, yb, prob, psink) + tuple(outs[:4]))
        if not last:
            xs.append(outs[4])

    small = {n: [None] * DEPTH for n in _SMALL}
    names = ("w_in", "p_a", "p_b", "w_out")
    owner_axis = {"w_in": 0, "p_a": 1, "p_b": 1, "w_out": 0}
    token = jnp.zeros((8, 128), F32)
    dx = tgt
    split = [None] * DEPTH
    for l in reversed(range(DEPTH)):
        hm, hr, ya, yb, prob, psink, pa, pb, merged, z = saved[l]
        wl = weights[l]
        dz, dpa, dpb, dhr, dya, dyb, acc, gbr = _tail_bwd(
            dx, z, pa, pb, hr, wl["wo"], wl["pa"], wl["pb"], ln_g[l].reshape(1, D) + token[0, 0],
            ln_b[l].reshape(1, D), f"tail_bwd{l}", l == DEPTH - 1)
        if l == DEPTH - 1:
            loss = lax.psum(acc[3, 0] * (0.5 / D), ("x", "y", "c"))
        dhm, gbm, gsk, gvn, gws, gbs = _mixer_bwd(
            hm, dya, dyb, prob, psink, vn_g[l].reshape(1, -1), vn_b[l].reshape(1, -1), w_s[l], bsb[l],
            f"mixer_bwd{l}")
        grads = {"w_in": jnp.concatenate([_wgrad(dhm, xs[l], MAIN_W // 2, f"wgrad_in_main{l}"),
                                          _wgrad(dhr, xs[l], R_W // 2, f"wgrad_in_route{l}")], axis=0),
                 "p_a": _wgrad(ya, dpa, Q_W, f"wgrad_pa{l}"), "p_b": _wgrad(yb, dpb, SGU_W, f"wgrad_pb{l}"),
                 "w_out": _wgrad(merged, dz, D, f"wgrad_out{l}")}
        small["b_in"][l] = jnp.concatenate([gbm[0], gbr[0]])
        small["sinks"][l] = gsk[:, 0]
        small["vn_g"][l], small["vn_b"][l] = gvn[0], gvn[1]
        small["w_s"][l], small["b_s"][l] = gws, gbs[:, :, 0]
        small["ln_g"][l], small["ln_b"][l], small["b_out"][l] = acc[0], acc[1], acc[2]
        parts = [_owner_blocks(grads[n], owner_axis[n]) for n in names]
        if l == 0:
            packed = _pack_small({n: jnp.stack(v) for n, v in small.items()}, _SMALL)
            parts.append(jnp.broadcast_to(packed[None, None], (1, 2) + packed.shape))
        from_sib = _swap_sibling(parts, f"rs_sibling{l}")
        pair = [_pair_sum(g, r, f"pair_sum{l}_{a}") for a, (g, r) in enumerate(zip(parts, from_sib))]
        if l == 0:
            pair[4] = jnp.broadcast_to(pair[4], (4,) + packed.shape)
        lands = [jnp.zeros(p.shape, p.dtype) for p in pair]
        split[l] = _split_start(pair, lands, _plan_chips, 3, f"rs_chips{l}_start")
        token = split[l][3]
        dx = _dx_inproj(dz, dhm, dhr, wl["w_t"], token, f"dx_inproj{l}")
    grad_x, acc_in = _ln_in_bwd(dx, x2, ln_in_g.reshape(1, D), "ln_in_bwd")
    (all_in,) = _allgather8([acc_in], "allgather_ln_in")

    given = {"w_in": (w_in_t, turned(m_w_in), turned(v_w_in)), "p_a": (p_a, m_p_a, v_p_a),
             "p_b": (p_b, m_p_b, v_p_b), "w_out": (w_out, m_w_out, v_w_out)}
    waited = [_split_wait(split[l][0], split[l][1], split[l][2], _plan_chips, all_in, f"rs_chips{l}_wait")
              for l in range(DEPTH)]
    res = {}
    for a, n in enumerate(names):
        rows, lanes = waited[0][1][a].shape[1:]
        outs = _adamw([waited[l][1][a] for l in range(DEPTH)], *[v.reshape(DEPTH * rows, lanes) for v in given[n]],
                      f"adamw_{n}", own=[waited[l][0][a] for l in range(DEPTH)])
        res[n] = [o.reshape(given[n][0].shape) for o in outs]
    res["w_in"] = [turned(o) for o in res["w_in"]]

    w_small = dict(ln_in_g=ln_in_g, ln_in_b=ln_in_b, b_in=b_in, sinks=sinks, vn_g=vn_g, vn_b=vn_b, w_s=w_s, b_s=b_s,
                   b_out=b_out, ln_g=ln_g, ln_b=ln_b)
    m_small = dict(ln_in_g=m_ln_in_g, ln_in_b=m_ln_in_b, b_in=m_b_in, sinks=m_sinks, vn_g=m_vn_g, vn_b=m_vn_b,
                   w_s=m_w_s, b_s=m_b_s, b_out=m_b_out, ln_g=m_ln_g, ln_b=m_ln_b)
    v_small = dict(ln_in_g=v_ln_in_g, ln_in_b=v_ln_in_b, b_in=v_b_in, sinks=v_sinks, vn_g=v_vn_g, vn_b=v_vn_b,
                   w_s=v_w_s, b_s=v_b_s, b_out=v_b_out, ln_g=v_ln_g, ln_b=v_ln_b)
    outs = _adamw([waited[0][1][4]], *[_pack_small(d, _SMALL) for d in (w_small, m_small, v_small)], "adamw_small",
                  own=[waited[0][0][4]])
    outs_in = _adamw([all_in], *[jnp.pad(jnp.stack([d[n] for n in _SMALL_IN]), ((0, 6), (0, 0)))
                                 for d in (w_small, m_small, v_small)], "adamw_ln_in")
    for k in range(4):
        u = _unpack_small(outs[k], w_small, _SMALL)
        u.update({n: outs_in[k][r] for r, n in enumerate(_SMALL_IN)})
        for n in u:
            res.setdefault(n, [None] * 4)[k] = u[n]

    order = ("ln_in_g", "ln_in_b", "w_in", "b_in", "sinks", "vn_g", "vn_b", "w_s", "b_s", "p_a", "p_b", "w_out",
             "b_out", "ln_g", "ln_b")
    return (loss, grad_x.reshape(x.shape), *[res[n][0] for n in order], *[res[n][1] for n in order],
            *[res[n][2] for n in order], *[res[n][3] for n in order])
```

```python
import jax
import jax.numpy as jnp
from jax import lax
from jax.experimental import pallas as pl
from jax.experimental.pallas import tpu as pltpu

F32 = jnp.float32
BF16 = jnp.bfloat16

D = 1024
BLK = 128
N_KV = 2
Q_W, KV_W, SGU_W = 512, 128, 512
C_Q, C_K, C_V, C_GA, C_UB, C_VB, C_GB = 0, 512, 640, 768, 1280, 1792, 2304
MAIN_W = 2816
R_W = 2048
IN_COLS = MAIN_W + R_W
N_DEV = 8
SHARD_COLS = IN_COLS // N_DEV

DEPTH = 2
ALPHA = (2.0 * DEPTH) ** 0.25
LN_EPS = 1e-5
ATTN_SCALE = 0.125
NEG = float(jnp.finfo(jnp.float32).min)

ADAM_LR, ADAM_B1, ADAM_B2, ADAM_EPS, ADAM_WD, ADAM_STEP = 0.001, 0.9, 0.999, 1e-08, 0.01, 10

TM = 256
TM_EW = 512
TM_MM = 512
NB = TM // BLK
MESH = pl.DeviceIdType.MESH
VMEM_LIMIT = 56 * 1024 * 1024

_ARB = pltpu.CompilerParams(dimension_semantics=("arbitrary",), vmem_limit_bytes=VMEM_LIMIT)


def _sigmoid(x):
    return 1.0 / (1.0 + jnp.exp(-x))


_GELU_C = 0.7978845608028654
_GELU_A = 0.044715


def _gelu_parts(x):
    x2 = x * x
    t = jnp.tanh(x * (_GELU_C + (_GELU_C * _GELU_A) * x2))
    hx = 0.5 * x
    return hx, t, x2


def _gelu(x):
    hx, t, _ = _gelu_parts(x)
    return hx + hx * t


def _gelu_and_grad(x):
    hx, t, x2 = _gelu_parts(x)
    grad = 0.5 + 0.5 * t + (hx - hx * (t * t)) * (_GELU_C + (3.0 * _GELU_C * _GELU_A) * x2)
    return hx + hx * t, grad


def _ln_stats(x):
    mu = jnp.mean(x, axis=-1, keepdims=True)
    xc = x - mu
    var = jnp.mean(xc * xc, axis=-1, keepdims=True)
    rstd = lax.rsqrt(var + LN_EPS)
    return xc * rstd, rstd


def _ln_bwd(dy_g, xhat, rstd):
    m1 = jnp.mean(dy_g, axis=-1, keepdims=True)
    m2 = jnp.mean(dy_g * xhat, axis=-1, keepdims=True)
    return rstd * (dy_g - m1 - xhat * m2)


def _colsum(x):
    return jnp.sum(x, axis=0, keepdims=True)


def _dot(a, b):
    return jnp.dot(a, b, preferred_element_type=F32)


def _dot_nt(a, b):
    return lax.dot_general(a, b, (((1,), (1,)), ((), ())), preferred_element_type=F32)


def _dot_tn(a, b):
    return lax.dot_general(a, b, (((0,), (0,)), ((), ())), preferred_element_type=F32)


def _head_place(hk, g):
    j = 4 * hk + g
    return j, j // 2, j % 2


def _head_rows(x, hk):
    d = lax.broadcasted_iota(jnp.int32, x.shape, 0)
    return jnp.where((d >= 64 * hk) & (d < 64 * hk + 64), x, 0.0).astype(BF16)


def _head_lanes(x, hk):
    d = lax.broadcasted_iota(jnp.int32, x.shape, 1)
    return jnp.where((d >= 64 * hk) & (d < 64 * hk + 64), x, 0.0)


def _band_bias():
    kpos = lax.broadcasted_iota(jnp.int32, (2 * BLK, 4 * BLK), 0)
    row = lax.broadcasted_iota(jnp.int32, (2 * BLK, 4 * BLK), 1) & (BLK - 1)
    band = (kpos > row) & (kpos <= row + BLK)
    return jnp.stack([jnp.where(band, 0.0, NEG), jnp.where(band & (kpos >= BLK), 0.0, NEG)]).astype(F32)


def _stack_q(q, hk):
    parts = []
    for g in range(4):
        _, p, pos = _head_place(hk, g)
        qp = q[:, BLK * p:BLK * (p + 1)] * ATTN_SCALE
        if pos != hk:
            qp = pltpu.roll(qp, 64, 1)
        parts.append(qp.astype(BF16))
    return jnp.concatenate(parts, axis=0)


def _attn_probs(q4, kh, hk, sinks_ref, bias):
    s_t = _dot_nt(kh, q4) + bias
    sink_row = jnp.concatenate(
        [jnp.full((1, BLK), sinks_ref[4 * hk + g], F32) for g in range(4)], axis=1)
    m = jnp.maximum(jnp.max(s_t, axis=0, keepdims=True), sink_row)
    p_un = jnp.exp(s_t - m)
    e_sink = jnp.exp(sink_row - m)
    inv = 1.0 / (jnp.sum(p_un, axis=0, keepdims=True) + e_sink)
    return (p_un * inv).astype(BF16), e_sink * inv


def _unstack_heads(x4, hk, pairs):
    for g in range(4):
        _, p, pos = _head_place(hk, g)
        xg = x4[BLK * g:BLK * (g + 1)]
        if pos != hk:
            xg = pltpu.roll(xg, 64, 1)
        pairs[p] = xg if pairs[p] is None else pairs[p] + xg
    return pairs


def _attn_fwd(q, kband, vband, sinks_ref, bias, save):
    pairs = [None] * 4
    vband_t = vband.T
    for hk in range(N_KV):
        prob_t, p_sink = _attn_probs(_stack_q(q, hk), _head_lanes(kband, hk).astype(BF16), hk, sinks_ref, bias)
        save(hk, prob_t, p_sink)
        o_t = _dot(_head_rows(vband_t, hk), prob_t)
        pairs = _unstack_heads(o_t.T, hk, pairs)
    return jnp.concatenate(pairs, axis=1)


def _tril_mask():
    r = lax.broadcasted_iota(jnp.int32, (BLK, BLK), 0)
    c = lax.broadcasted_iota(jnp.int32, (BLK, BLK), 1)
    return c <= r


def _sgu_fwd(u, v, vn_g, vn_b, wt, bsb_ref):
    vhat, rstd = _ln_stats(v)
    vn = vhat * vn_g + vn_b
    mixed = jnp.concatenate(
        [_dot(wt[g], vn[:, BLK * g:BLK * (g + 1)].astype(BF16)) + bsb_ref[g] for g in range(4)], axis=1)
    return vhat, rstd, vn, mixed


def _cols(ref, rows, col, width):
    return ref[rows, col:col + width].astype(F32)


def _band(hm_ref, hprev_ref, s, col):
    r0 = s * BLK
    cur = hm_ref[r0:r0 + BLK, col:col + KV_W]
    if s == 0:
        off = 0 if col == C_K else KV_W
        prev = hprev_ref[:, off:off + KV_W]
    else:
        prev = hm_ref[r0 - BLK:r0, col:col + KV_W]
    return jnp.concatenate([prev, cur], axis=0).astype(F32)


def _mixer_in_specs(nt, rev):
    def tile(g):
        return nt - 1 - g if rev else g

    return [
        pl.BlockSpec(memory_space=pltpu.SMEM),
        pl.BlockSpec((TM, MAIN_W), lambda g: (tile(g), 0)),
        pl.BlockSpec((BLK, 2 * KV_W), lambda g: (jnp.maximum(tile(g) * NB - 1, 0), 2)),
        pl.BlockSpec((2, 2 * BLK, 4 * BLK), lambda g: (0, 0, 0)),
    ]


_CONST2 = lambda g: (0, 0)
_CONST3 = lambda g: (0, 0, 0)


def _ln_fwd(x, g, b, name):
    t = x.shape[0]

    def body(x_ref, g_ref, b_ref, o_ref):
        xhat, _ = _ln_stats(x_ref[...])
        o_ref[...] = xhat * g_ref[...] + b_ref[...]

    return pl.pallas_call(
        body, name=name, grid=(t // TM_EW,),
        in_specs=[pl.BlockSpec((TM_EW, D), lambda i: (i, 0)), pl.BlockSpec((1, D), _CONST2),
                  pl.BlockSpec((1, D), _CONST2)],
        out_specs=pl.BlockSpec((TM_EW, D), lambda i: (i, 0)),
        out_shape=jax.ShapeDtypeStruct((t, D), F32), compiler_params=_ARB,
    )(x, g.reshape(1, D), b.reshape(1, D))


def _inproj(x, w_t, b, name):
    t = x.shape[0]

    def body(x_ref, wt_ref, b_ref, hm_ref, hr_ref):
        xb = x_ref[...].astype(BF16)
        hm_ref[...] = (_dot_nt(xb, wt_ref[0:MAIN_W, :]) + b_ref[:, 0:MAIN_W]).astype(BF16)
        hr_ref[...] = (_dot_nt(xb, wt_ref[MAIN_W:IN_COLS, :]) + b_ref[:, MAIN_W:IN_COLS]).astype(BF16)

    return pl.pallas_call(
        body, name=name, grid=(t // TM_MM,),
        in_specs=[pl.BlockSpec((TM_MM, D), lambda i: (i, 0)),
                  pl.BlockSpec((IN_COLS, D), _CONST2), pl.BlockSpec((1, IN_COLS), _CONST2)],
        out_specs=[pl.BlockSpec((TM_MM, MAIN_W), lambda i: (i, 0)), pl.BlockSpec((TM_MM, R_W), lambda i: (i, 0))],
        out_shape=[jax.ShapeDtypeStruct((t, MAIN_W), BF16), jax.ShapeDtypeStruct((t, R_W), BF16)],
        compiler_params=_ARB,
    )(x, w_t, b)


def _mixer_fwd(hm, sinks, bias, vn_g, vn_b, w_s, bsb, nblk_seq, name):
    t = hm.shape[0]
    nt = t // TM

    def body(sinks_ref, hm_ref, hprev_ref, bias_ref, vng_ref, vnb_ref, ws_ref, bsb_ref,
             ya_ref, yb_ref, prob_ref, psink_ref):
        i = pl.program_id(0)
        tril = _tril_mask()
        wt = [jnp.where(tril, ws_ref[g], 0.0).astype(BF16) for g in range(4)]
        for s in range(NB):
            r0 = s * BLK
            rows = slice(r0, r0 + BLK)
            bias = bias_ref[jnp.where((i * NB + s) % nblk_seq == 0, 1, 0)]

            def save(hk, prob_t, p_sink, s=s):
                prob_ref[N_KV * s + hk] = prob_t
                psink_ref[N_KV * s + hk] = jnp.broadcast_to(p_sink, (8, 4 * BLK))

            attn = _attn_fwd(_cols(hm_ref, rows, C_Q, Q_W), _band(hm_ref, hprev_ref, s, C_K),
                             _band(hm_ref, hprev_ref, s, C_V), sinks_ref, bias, save)
            g_a = _cols(hm_ref, rows, C_GA, Q_W)
            ya_ref[rows, :] = (attn * (g_a * _sigmoid(g_a))).astype(BF16)
            u = _gelu(_cols(hm_ref, rows, C_UB, SGU_W))
            mixed = _sgu_fwd(u, _gelu(_cols(hm_ref, rows, C_VB, SGU_W)), vng_ref[...], vnb_ref[...], wt, bsb_ref)[-1]
            g_b = _cols(hm_ref, rows, C_GB, SGU_W)
            yb_ref[rows, :] = (u * mixed * (g_b * _sigmoid(g_b))).astype(BF16)

    ngrp = N_KV * NB
    return pl.pallas_call(
        body, name=name, grid=(nt,),
        in_specs=_mixer_in_specs(nt, False) + [
            pl.BlockSpec((1, SGU_W), _CONST2), pl.BlockSpec((1, SGU_W), _CONST2),
            pl.BlockSpec((4, BLK, BLK), _CONST3), pl.BlockSpec((4, BLK, BLK), _CONST3)],
        out_specs=[pl.BlockSpec((TM, Q_W), lambda i: (i, 0)), pl.BlockSpec((TM, SGU_W), lambda i: (i, 0)),
                   pl.BlockSpec((ngrp, 2 * BLK, 4 * BLK), lambda i: (i, 0, 0)),
                   pl.BlockSpec((ngrp, 8, 4 * BLK), lambda i: (i, 0, 0))],
        out_shape=[jax.ShapeDtypeStruct((t, Q_W), BF16), jax.ShapeDtypeStruct((t, SGU_W), BF16),
                   jax.ShapeDtypeStruct((nt * ngrp, 2 * BLK, 4 * BLK), BF16),
                   jax.ShapeDtypeStruct((nt * ngrp, 8, 4 * BLK), F32)],
        compiler_params=_ARB,
    )(sinks, hm, hm, bias, vn_g, vn_b, w_s, bsb)


def _tail_fwd(x, ya, yb, hr, pa_w, pb_w, wo, b_out, ln_g, ln_b, name, last):
    t = x.shape[0]

    def body(x_ref, ya_ref, yb_ref, hr_ref, paw_ref, pbw_ref, wo_ref, bo_ref, g_ref, b_ref,
             pa_ref, pb_ref, mg_ref, z_ref, *xn_ref):
        pa = _dot(ya_ref[...], paw_ref[...])
        pb = _dot(yb_ref[...], pbw_ref[...])
        pa_ref[...] = pa.astype(BF16)
        pb_ref[...] = pb.astype(BF16)
        everything = slice(None)
        merged = _sigmoid(_cols(hr_ref, everything, 0, D)) * pa + _sigmoid(_cols(hr_ref, everything, D, D)) * pb
        mb = merged.astype(BF16)
        mg_ref[...] = mb
        z = ALPHA * x_ref[...] + (_dot(mb, wo_ref[...]) + bo_ref[...])
        z_ref[...] = z
        if not last:
            zhat, _ = _ln_stats(z)
            xn_ref[0][...] = zhat * g_ref[...] + b_ref[...]

    row = lambda w: pl.BlockSpec((TM, w), lambda i: (i, 0))
    vec = pl.BlockSpec((1, D), _CONST2)
    n_f32 = 1 if last else 2
    return pl.pallas_call(
        body, name=name, grid=(t // TM,),
        in_specs=[row(D), row(Q_W), row(SGU_W), row(R_W),
                  pl.BlockSpec((Q_W, D), _CONST2), pl.BlockSpec((SGU_W, D), _CONST2), pl.BlockSpec((D, D), _CONST2),
                  vec, vec, vec],
        out_specs=[row(D)] * (3 + n_f32),
        out_shape=[jax.ShapeDtypeStruct((t, D), BF16)] * 3 + [jax.ShapeDtypeStruct((t, D), F32)] * n_f32,
        compiler_params=_ARB,
    )(x, ya, yb, hr, pa_w, pb_w, wo, b_out, ln_g, ln_b)


def _tail_bwd(dxn, z, pa, pb, hr, wo, pa_w, pb_w, ln_g, ln_b, name, from_loss):
    t = dxn.shape[0]

    def body(dxn_ref, z_ref, pa_ref, pb_ref, hr_ref, wo_ref, paw_ref, pbw_ref, g_ref, b_ref,
             dz_ref, dpa_ref, dpb_ref, dhr_ref, dya_ref, dyb_ref, acc_ref, gbr_ref):
        @pl.when(pl.program_id(0) == 0)
        def _():
            acc_ref[...] = jnp.zeros_like(acc_ref)
            gbr_ref[...] = jnp.zeros_like(gbr_ref)

        zhat, rstd = _ln_stats(z_ref[...])
        if from_loss:
            err = zhat * g_ref[...] + b_ref[...] - dxn_ref[...]
            dxn_v = err * (1.0 / D)
            sq = jnp.sum(jnp.sum(err * err, axis=1, keepdims=True), axis=0, keepdims=True)
            acc_ref[3:4, :] += jnp.broadcast_to(sq, (1, D))
        else:
            dxn_v = dxn_ref[...]
        dz = _ln_bwd(dxn_v * g_ref[...], zhat, rstd)
        dz_ref[...] = dz
        acc_ref[0:1, :] += _colsum(dxn_v * zhat)
        acc_ref[1:2, :] += _colsum(dxn_v)
        acc_ref[2:3, :] += _colsum(dz)
        dmerged = _dot_nt(dz.astype(BF16), wo_ref[...])
        everything = slice(None)
        sa = _sigmoid(_cols(hr_ref, everything, 0, D))
        sb = _sigmoid(_cols(hr_ref, everything, D, D))
        dpa = (dmerged * sa).astype(BF16)
        dpb = (dmerged * sb).astype(BF16)
        dpa_ref[...] = dpa
        dpb_ref[...] = dpb
        dra = dmerged * pa_ref[...].astype(F32) * (sa * (1.0 - sa))
        drb = dmerged * pb_ref[...].astype(F32) * (sb * (1.0 - sb))
        dhr_ref[:, 0:D] = dra.astype(BF16)
        dhr_ref[:, D:2 * D] = drb.astype(BF16)
        gbr_ref[0:1, 0:D] += _colsum(dra)
        gbr_ref[0:1, D:2 * D] += _colsum(drb)
        dya_ref[...] = _dot_nt(dpa, paw_ref[...]).astype(BF16)
        dyb_ref[...] = _dot_nt(dpb, pbw_ref[...]).astype(BF16)

    row = lambda w: pl.BlockSpec((TM, w), lambda i: (i, 0))
    vec = pl.BlockSpec((1, D), _CONST2)
    return pl.pallas_call(
        body, name=name, grid=(t // TM,),
        in_specs=[row(D), row(D), row(D), row(D), row(R_W),
                  pl.BlockSpec((D, D), _CONST2), pl.BlockSpec((Q_W, D), _CONST2), pl.BlockSpec((SGU_W, D), _CONST2),
                  vec, vec],
        out_specs=[row(D), row(D), row(D), row(R_W), row(Q_W), row(SGU_W), pl.BlockSpec((8, D), _CONST2),
                   pl.BlockSpec((8, R_W), _CONST2)],
        out_shape=[jax.ShapeDtypeStruct((t, D), F32), jax.ShapeDtypeStruct((t, D), BF16),
                   jax.ShapeDtypeStruct((t, D), BF16), jax.ShapeDtypeStruct((t, R_W), BF16),
                   jax.ShapeDtypeStruct((t, Q_W), BF16), jax.ShapeDtypeStruct((t, SGU_W), BF16),
                   jax.ShapeDtypeStruct((8, D), F32), jax.ShapeDtypeStruct((8, R_W), F32)],
        compiler_params=_ARB,
    )(dxn, z, pa, pb, hr, wo, pa_w, pb_w, ln_g, ln_b)


def _mixer_bwd(hm, dya, dyb, prob, psink, vn_g, vn_b, w_s, bsb, name):
    t = hm.shape[0]
    nt = t // TM
    ngrp = N_KV * NB

    def body(hm_ref, hprev_ref, prob_ref, psink_ref, dya_ref, dyb_ref, vng_ref, vnb_ref, ws_ref, bsb_ref,
             dhm_ref, gbm_ref, gsk_ref, gvn_ref, gws_ref, gbs_ref, dk_carry, dv_carry):
        gi = pl.program_id(0)

        @pl.when(gi == 0)
        def _():
            for r in (gbm_ref, gsk_ref, gvn_ref, gws_ref, gbs_ref, dk_carry, dv_carry):
                r[...] = jnp.zeros_like(r)

        tril = _tril_mask()
        wt = [jnp.where(tril, ws_ref[g], 0.0).astype(BF16) for g in range(4)]
        vng = vng_ref[...]
        ones8 = jnp.ones((8, BLK), BF16)

        def put(rows, col, val):
            dhm_ref[rows, col:col + val.shape[1]] = val.astype(BF16)

        for s in reversed(range(NB)):
            r0 = s * BLK
            rows = slice(r0, r0 + BLK)
            q = _cols(hm_ref, rows, C_Q, Q_W)
            kband = _band(hm_ref, hprev_ref, s, C_K)
            vband = _band(hm_ref, hprev_ref, s, C_V)
            g_a = _cols(hm_ref, rows, C_GA, Q_W)
            sg = _sigmoid(g_a)
            dya_v = _cols(dya_ref, rows, 0, Q_W)
            d_o = dya_v * (g_a * sg)
            o_pairs, dq_pairs = [None] * 4, [None] * 4
            dkband = jnp.zeros((2 * BLK, KV_W), F32)
            dvband = jnp.zeros((2 * BLK, KV_W), F32)
            kband_t, vband_t = kband.T, vband.T
            for hk in range(N_KV):
                q4 = _stack_q(q, hk)
                prob_b = prob_ref[N_KV * s + hk]
                p_sink = psink_ref[N_KV * s + hk][0:1, :]
                o_t = _dot(_head_rows(vband_t, hk), prob_b)
                o_pairs = _unstack_heads(o_t.T, hk, o_pairs)
                parts = []
                for g in range(4):
                    _, p, pos = _head_place(hk, g)
                    dp = d_o[:, BLK * p:BLK * (p + 1)]
                    parts.append(pltpu.roll(dp, 64, 1) if pos != hk else dp)
                do4 = _head_lanes(jnp.concatenate(parts, axis=0), hk)
                do4b = do4.astype(BF16)
                delta = _colsum(do4.T * o_t)
                vh = _head_lanes(vband, hk).astype(BF16)
                ds_t = prob_b.astype(F32) * (_dot_nt(vh, do4b) - delta)
                dsb = ds_t.astype(BF16)
                dq4_t = _dot(_head_rows(kband_t, hk), dsb)
                dq_pairs = _unstack_heads(dq4_t.T * ATTN_SCALE, hk, dq_pairs)
                dkband = dkband + _head_lanes(_dot(dsb, q4), hk)
                dvband = dvband + _dot(prob_b, do4b)
                dsk = p_sink * delta
                for g in range(4):
                    j = 4 * hk + g
                    tot = jnp.sum(dsk[:, BLK * g:BLK * (g + 1)], axis=1, keepdims=True)
                    gsk_ref[j:j + 1, :] += jnp.broadcast_to(-tot, (1, 128))
            attn = jnp.concatenate(o_pairs, axis=1)
            put(rows, C_Q, jnp.concatenate(dq_pairs, axis=1))
            put(rows, C_K, dkband[BLK:2 * BLK] + dk_carry[...])
            put(rows, C_V, dvband[BLK:2 * BLK] + dv_carry[...])
            dk_carry[...] = dkband[0:BLK]
            dv_carry[...] = dvband[0:BLK]
            put(rows, C_GA, dya_v * attn * (sg * (1.0 + g_a * (1.0 - sg))))
            u, du_du_b = _gelu_and_grad(_cols(hm_ref, rows, C_UB, SGU_W))
            v, dv_dv_b = _gelu_and_grad(_cols(hm_ref, rows, C_VB, SGU_W))
            g_b = _cols(hm_ref, rows, C_GB, SGU_W)
            vhat, rstd, vn, mixed = _sgu_fwd(u, v, vng, vnb_ref[...], wt, bsb_ref)
            sgb = _sigmoid(g_b)
            silu_b = g_b * sgb
            dyb_v = _cols(dyb_ref, rows, 0, SGU_W)
            du = dyb_v * mixed * silu_b
            dmixed = dyb_v * u * silu_b
            put(rows, C_GB, dyb_v * u * mixed * (sgb * (1.0 + g_b * (1.0 - sgb))))
            dvn_parts = []
            for g in range(4):
                cols = slice(BLK * g, BLK * (g + 1))
                dmg = dmixed[:, cols]
                dmgb = dmg.astype(BF16)
                dvn_parts.append(_dot_tn(wt[g], dmgb))
                gws_ref[g] += jnp.where(tril, _dot_nt(dmgb, vn[:, cols].astype(BF16)), 0.0)
                gbs_ref[g] += dmg
            dvn = jnp.concatenate(dvn_parts, axis=1)
            gvn_ref[0:1, :] += _colsum(dvn * vhat)
            gvn_ref[1:2, :] += _colsum(dvn)
            dv = _ln_bwd(dvn * vng, vhat, rstd)
            put(rows, C_UB, du * du_du_b)
            put(rows, C_VB, dv * dv_dv_b)
            gbm_ref[...] += _dot(ones8, dhm_ref[rows, :])

        @pl.when(gi == nt - 1)
        def _():
            for g in range(4):
                gbs_ref[g] = jnp.broadcast_to(jnp.sum(gbs_ref[g], axis=1, keepdims=True), (BLK, BLK))

    row = lambda w: pl.BlockSpec((TM, w), lambda g: (nt - 1 - g, 0))
    return pl.pallas_call(
        body, name=name, grid=(nt,),
        in_specs=_mixer_in_specs(nt, True)[1:3] + [
            pl.BlockSpec((ngrp, 2 * BLK, 4 * BLK), lambda g: (nt - 1 - g, 0, 0)),
            pl.BlockSpec((ngrp, 8, 4 * BLK), lambda g: (nt - 1 - g, 0, 0)),
            row(Q_W), row(SGU_W),
            pl.BlockSpec((1, SGU_W), _CONST2), pl.BlockSpec((1, SGU_W), _CONST2),
            pl.BlockSpec((4, BLK, BLK), _CONST3), pl.BlockSpec((4, BLK, BLK), _CONST3)],
        out_specs=[row(MAIN_W), pl.BlockSpec((8, MAIN_W), _CONST2), pl.BlockSpec((8, 128), _CONST2),
                   pl.BlockSpec((8, SGU_W), _CONST2), pl.BlockSpec((4, BLK, BLK), _CONST3),
                   pl.BlockSpec((4, BLK, BLK), _CONST3)],
        out_shape=[jax.ShapeDtypeStruct((t, MAIN_W), BF16), jax.ShapeDtypeStruct((8, MAIN_W), F32),
                   jax.ShapeDtypeStruct((8, 128), F32), jax.ShapeDtypeStruct((8, SGU_W), F32),
                   jax.ShapeDtypeStruct((4, BLK, BLK), F32), jax.ShapeDtypeStruct((4, BLK, BLK), F32)],
        scratch_shapes=[pltpu.VMEM((BLK, KV_W), F32), pltpu.VMEM((BLK, KV_W), F32)],
        compiler_params=_ARB,
    )(hm, hm, prob, psink, dya, dyb, vn_g, vn_b, w_s, bsb)


def _dx_inproj(dz, dhm, dhr, w_t, after, name, ln_in=None):
    t = dz.shape[0]

    def body(dz_ref, dhm_ref, dhr_ref, wt_ref, after_ref, *rest):
        dx = (ALPHA * dz_ref[...] + after_ref[0:1, 0:1] + _dot(dhm_ref[...], wt_ref[0:MAIN_W, :])
              + _dot(dhr_ref[...], wt_ref[MAIN_W:IN_COLS, :]))
        if ln_in is None:
            rest[0][...] = dx
            return
        x_ref, g_ref, gx_ref, acc_ref = rest

        @pl.when(pl.program_id(0) == 0)
        def _():
            acc_ref[...] = jnp.zeros_like(acc_ref)

        xhat, rstd = _ln_stats(x_ref[...])
        gx_ref[...] = _ln_bwd(dx * g_ref[...], xhat, rstd)
        acc_ref[0:1, :] += _colsum(dx * xhat)
        acc_ref[1:2, :] += _colsum(dx)

    row = lambda w: pl.BlockSpec((TM_MM, w), lambda i: (i, 0))
    in_specs = [row(D), row(MAIN_W), row(R_W), pl.BlockSpec((IN_COLS, D), _CONST2), pl.BlockSpec((8, 128), _CONST2)]
    if ln_in is None:
        return pl.pallas_call(
            body, name=name, grid=(t // TM_MM,), in_specs=in_specs,
            out_specs=row(D), out_shape=jax.ShapeDtypeStruct((t, D), F32), compiler_params=_ARB,
        )(dz, dhm, dhr, w_t, after)
    return pl.pallas_call(
        body, name=name, grid=(t // TM_MM,), in_specs=in_specs + [row(D), pl.BlockSpec((1, D), _CONST2)],
        out_specs=[row(D), pl.BlockSpec((8, D), _CONST2)],
        out_shape=[jax.ShapeDtypeStruct((t, D), F32), jax.ShapeDtypeStruct((8, D), F32)], compiler_params=_ARB,
    )(dz, dhm, dhr, w_t, after, *ln_in)


def _wgrad(a, b, tm, name):
    t, m = a.shape
    n = b.shape[1]
    tk = min(t, 2048)
    nk = t // tk

    def body(a_ref, b_ref, o_ref, acc_ref):
        k = pl.program_id(1)

        @pl.when(k == 0)
        def _():
            acc_ref[...] = jnp.zeros_like(acc_ref)

        acc_ref[...] += _dot_tn(a_ref[...].astype(BF16), b_ref[...].astype(BF16))

        @pl.when(k == nk - 1)
        def _():
            o_ref[...] = acc_ref[...].astype(BF16)

    return pl.pallas_call(
        body, name=name, grid=(m // tm, nk),
        in_specs=[pl.BlockSpec((tk, tm), lambda j, k: (k, j)), pl.BlockSpec((tk, n), lambda j, k: (k, 0))],
        out_specs=pl.BlockSpec((tm, n), lambda j, k: (j, 0)),
        out_shape=jax.ShapeDtypeStruct((m, n), BF16),
        scratch_shapes=[pltpu.VMEM((tm, n), F32)],
        compiler_params=pltpu.CompilerParams(dimension_semantics=("arbitrary", "arbitrary"), vmem_limit_bytes=VMEM_LIMIT),
    )(a, b)


_ANY = pl.BlockSpec(memory_space=pl.ANY)


def _place():
    return lax.axis_index("x"), lax.axis_index("y"), lax.axis_index("c")


def _allgather8(xs, name):
    n = len(xs)

    def body(*refs):
        x_refs, o_refs = refs[:n], refs[n:2 * n]
        send_sems, recv_sems, local_sems = refs[2 * n:]
        x, y, c = _place()
        me, sib = (x, y, c), (x, y, 1 - c)
        chips = [(1 - x, y), (x, 1 - y), (1 - x, 1 - y)]

        def copy(a, k, block, to, src=None):
            dst = o_refs[a].at[4 * block[0] + 2 * block[1] + block[2]]
            return pltpu.make_async_remote_copy(
                src_ref=dst if src is None else src, dst_ref=dst, send_sem=send_sems.at[7 * a + k],
                recv_sem=recv_sems.at[7 * a + k], device_id=to, device_id_type=MESH)

        mine = [pltpu.make_async_copy(x_refs[a], o_refs[a].at[4 * x + 2 * y + c], local_sems.at[a]) for a in range(n)]
        for cp in mine:
            cp.start()
        sent = []
        for a in range(n):
            sent.append(copy(a, 0, me, sib, src=x_refs[a]))
            sent += [copy(a, 1 + j, me, (*chip, c), src=x_refs[a]) for j, chip in enumerate(chips)]
        for cp in sent:
            cp.start()
        for j, chip in enumerate(chips):
            for a in range(n):
                copy(a, 1 + j, (*chip, c), me).wait_recv()
                fwd = copy(a, 4 + j, (*chip, c), sib)
                fwd.start()
                sent.append(fwd)
        for a in range(n):
            copy(a, 0, sib, me).wait_recv()
            for j, chip in enumerate(chips):
                copy(a, 4 + j, (*chip, 1 - c), me).wait_recv()
        for cp in sent:
            cp.wait_send()
        for cp in mine:
            cp.wait()

    return pl.pallas_call(
        body, name=name, in_specs=[_ANY] * n, out_specs=[_ANY] * n,
        out_shape=[jax.ShapeDtypeStruct((N_DEV,) + v.shape, v.dtype) for v in xs],
        scratch_shapes=[pltpu.SemaphoreType.DMA((7 * n,)), pltpu.SemaphoreType.DMA((7 * n,)),
                        pltpu.SemaphoreType.DMA((n,))],
    )(*xs)


def _forward_sibling(lands, name):
    n = len(lands)

    def body(*refs):
        l_refs = refs[n:2 * n]
        send_sems, recv_sems = refs[2 * n:]
        x, y, c = _place()
        chips = [(1 - x, y), (x, 1 - y), (1 - x, 1 - y)]

        def copy(a, j, core):
            rows = l_refs[a].at[4 * chips[j][0] + 2 * chips[j][1] + core]
            return pltpu.make_async_remote_copy(
                src_ref=rows, dst_ref=rows, send_sem=send_sems.at[3 * a + j], recv_sem=recv_sems.at[3 * a + j],
                device_id=(x, y, 1 - c), device_id_type=MESH)

        for a in range(n):
            for j in range(3):
                copy(a, j, c).start()
        for a in range(n):
            for j in range(3):
                copy(a, j, 1 - c).wait_recv()
                copy(a, j, c).wait_send()

    return pl.pallas_call(
        body, name=name, in_specs=[_ANY] * n, out_specs=[_ANY] * n,
        out_shape=[jax.ShapeDtypeStruct(v.shape, v.dtype) for v in lands],
        input_output_aliases={a: a for a in range(n)},
        scratch_shapes=[pltpu.SemaphoreType.DMA((3 * n,)), pltpu.SemaphoreType.DMA((3 * n,))],
    )(*lands)


def _swap_sibling(gs, name):
    n = len(gs)
    first = [0]
    for v in gs:
        first.append(first[-1] + v.shape[0])

    def body(*refs):
        g_refs, r_refs = refs[:n], refs[n:2 * n]
        send_sems, recv_sems = refs[2 * n:]
        x, y, c = _place()
        cps = [pltpu.make_async_remote_copy(
            src_ref=g_refs[a].at[q, 1 - c], dst_ref=r_refs[a].at[q], send_sem=send_sems.at[first[a] + q],
            recv_sem=recv_sems.at[first[a] + q], device_id=(x, y, 1 - c), device_id_type=MESH)
            for a in range(n) for q in range(gs[a].shape[0])]
        for cp in cps:
            cp.start()
        for cp in cps:
            cp.wait()

    return pl.pallas_call(
        body, name=name, in_specs=[_ANY] * n, out_specs=[_ANY] * n,
        out_shape=[jax.ShapeDtypeStruct(v.shape[:1] + v.shape[2:], v.dtype) for v in gs],
        scratch_shapes=[pltpu.SemaphoreType.DMA((first[-1],)), pltpu.SemaphoreType.DMA((first[-1],))],
    )(*gs)


def _row_tile(rows, lanes, cap):
    if rows * lanes * 4 <= (1 << 20):
        return rows
    return max(d for d in range(8, cap + 1, 8) if rows % d == 0 and (d % 16 == 0 or rows % 16 != 0))


def _pair_sum(g, r, name):
    n, _, rows, l = g.shape
    tr = _row_tile(rows, l, 608)

    def body(c_ref, g_ref, r_ref, o_ref):
        o_ref[...] = (g_ref[0].astype(F32) + r_ref[...].astype(F32)).astype(o_ref.dtype)

    return pl.pallas_call(
        body, name=name,
        grid_spec=pltpu.PrefetchScalarGridSpec(
            num_scalar_prefetch=1, grid=(n, rows // tr),
            in_specs=[pl.BlockSpec((1, 1, tr, l), lambda q, i, c_ref: (q, c_ref[0], i, 0)),
                      pl.BlockSpec((1, tr, l), lambda q, i, c_ref: (q, i, 0))],
            out_specs=pl.BlockSpec((1, tr, l), lambda q, i, c_ref: (q, i, 0))),
        out_shape=jax.ShapeDtypeStruct((n, rows, l), g.dtype),
        compiler_params=pltpu.CompilerParams(dimension_semantics=("arbitrary", "arbitrary"), vmem_limit_bytes=VMEM_LIMIT),
    )(lax.axis_index("c").astype(jnp.int32).reshape(1), g, r)


def _adamw(parts, w, m, v, name, own=None):
    nl = len(parts)
    ns, rows, l = parts[0].shape
    tr = _row_tile(rows, l * ns, 304)
    nt = rows // tr
    c1 = 1.0 - ADAM_B1 ** ADAM_STEP
    c2 = 1.0 - ADAM_B2 ** ADAM_STEP

    def body(q_ref, *refs):
        own_refs = refs[:nl] if own is not None else None
        p_refs = refs[-7 - nl:-7]
        w_ref, m_ref, v_ref, g_ref, d_ref, nm_ref, nv_ref = refs[-7:]
        layer = pl.program_id(0)
        g = None
        for j in range(nl):
            gj = None
            for k in range(ns):
                term = p_refs[j][k].astype(F32)
                if own_refs is not None:
                    term = jnp.where(q_ref[0] == k, own_refs[j][0].astype(F32), term)
                gj = term if gj is None else gj + term
            g = gj if g is None else jnp.where(layer == j, gj, g)
        g_ref[...] = g
        nm = ADAM_B1 * m_ref[...] + (1.0 - ADAM_B1) * g
        nv = ADAM_B2 * v_ref[...] + (1.0 - ADAM_B2) * (g * g)
        nm_ref[...] = nm
        nv_ref[...] = nv
        d_ref[...] = -ADAM_LR * ((nm / c1) / (jnp.sqrt(nv / c2) + ADAM_EPS) + ADAM_WD * w_ref[...])

    def tile_of(j):
        return lambda la, i, q: jnp.where(la == j, i, jnp.where(la < j, 0, nt - 1))

    row = pl.BlockSpec((tr, l), lambda la, i, q: (la * nt + i, 0))
    own_specs = [] if own is None else [
        pl.BlockSpec((1, tr, l), lambda la, i, q, j=j: (q[0], tile_of(j)(la, i, q), 0)) for j in range(nl)]
    part_specs = [pl.BlockSpec((ns, tr, l), lambda la, i, q, j=j: (0, tile_of(j)(la, i, q), 0)) for j in range(nl)]
    chip = (2 * lax.axis_index("x") + lax.axis_index("y")).astype(jnp.int32).reshape(1)
    return pl.pallas_call(
        body, name=name,
        grid_spec=pltpu.PrefetchScalarGridSpec(
            num_scalar_prefetch=1, grid=(nl, nt),
            in_specs=own_specs + part_specs + [row, row, row], out_specs=[row] * 4),
        out_shape=[jax.ShapeDtypeStruct((nl * rows, l), F32)] * 4,
        compiler_params=pltpu.CompilerParams(dimension_semantics=("arbitrary", "arbitrary"), vmem_limit_bytes=VMEM_LIMIT),
    )(chip, *([] if own is None else own), *parts, w, m, v)


_HBM = pl.BlockSpec(memory_space=pltpu.HBM)
_SEM = pl.BlockSpec(memory_space=pltpu.SEMAPHORE)
_EFFECT = pltpu.SideEffectType.DATAFLOW_SIDE_EFFECTING


def _plan_all(x, y, c):
    me = 4 * x + 2 * y + c
    peers = [(x, y, 1 - c), (1 - x, y, c), (x, 1 - y, c), (1 - x, 1 - y, c),
             (1 - x, y, 1 - c), (x, 1 - y, 1 - c), (1 - x, 1 - y, 1 - c)]
    return [(None, me, p, 4 * p[0] + 2 * p[1] + p[2]) for p in peers]


def _plan_near(x, y, c):
    me = 4 * x + 2 * y + c
    peers = [(x, y, 1 - c), (1 - x, y, c), (x, 1 - y, c), (1 - x, 1 - y, c)]
    return [(None, me, p, 4 * p[0] + 2 * p[1] + p[2]) for p in peers]


def _plan_chips(x, y, c):
    me = 2 * x + y
    return [(2 * qx + qy, me, (qx, qy, c), 2 * qx + qy) for qx, qy in ((1 - x, y), (x, 1 - y), (1 - x, 1 - y))]


def _split_copies(plan, src_refs, land_refs, send_sems, recv_sems, arrival):
    n = len(src_refs)
    entries = plan(*_place())
    per = len(entries)
    cps = []
    for a in range(n):
        for k, (src_slot, dst_slot, peer, back_slot) in enumerate(entries):
            src = src_refs[a] if src_slot is None else src_refs[a].at[src_slot]
            cps.append(pltpu.make_async_remote_copy(
                src_ref=src, dst_ref=land_refs[a].at[back_slot if arrival else dst_slot],
                send_sem=send_sems.at[per * a + k], recv_sem=recv_sems.at[per * a + k],
                device_id=peer, device_id_type=MESH))
    return cps


def _split_start(srcs, lands, plan, per, name):
    n = len(srcs)

    def body(*refs):
        for cp in _split_copies(plan, refs[:n], refs[n:2 * n], refs[2 * n], refs[2 * n + 1], False):
            cp.start()
        refs[-1][...] = jnp.zeros_like(refs[-1])

    both = list(srcs) + list(lands)
    outs = pl.pallas_call(
        body, name=name,
        out_shape=(pltpu.SemaphoreType.DMA((per * n,)), pltpu.SemaphoreType.DMA((per * n,)),
                   *[pltpu.HBM(v.shape, v.dtype) for v in both], jax.ShapeDtypeStruct((8, 128), F32)),
        in_specs=[_HBM] * (2 * n),
        out_specs=(_SEM, _SEM, *[_HBM] * (2 * n), pl.BlockSpec(memory_space=pltpu.VMEM)),
        input_output_aliases={i: 2 + i for i in range(2 * n)},
        compiler_params=pltpu.CompilerParams(has_side_effects=_EFFECT),
    )(*[pltpu.with_memory_space_constraint(v, pltpu.HBM) for v in both])
    return outs[0], outs[1], list(outs[2:2 + 2 * n]), outs[-1]


def _split_wait(send_sems, recv_sems, thru, plan, after, name):
    n = len(thru) // 2

    def body(*refs):
        for cp in _split_copies(plan, refs[:n], refs[n:2 * n], refs[2 * n], refs[2 * n + 1], True):
            cp.wait_send()
            cp.wait_recv()

    outs = pl.pallas_call(
        body, name=name, out_shape=tuple(pltpu.HBM(v.shape, v.dtype) for v in thru),
        in_specs=[_HBM] * (2 * n) + [_SEM, _SEM, pl.BlockSpec(memory_space=pl.ANY)],
        out_specs=[_HBM] * (2 * n), input_output_aliases={i: i for i in range(2 * n)},
        compiler_params=pltpu.CompilerParams(has_side_effects=_EFFECT),
    )(*thru, send_sems, recv_sems, after)
    return list(outs[:n]), list(outs[n:])


_SMALL_IN = ("ln_in_g", "ln_in_b")
_SMALL = ("b_in", "sinks", "vn_g", "vn_b", "w_s", "b_s", "b_out", "ln_g", "ln_b")


def _rows128(a):
    flat = a.reshape(-1)
    return jnp.pad(flat, (0, (-flat.shape[0]) % 128)).reshape(-1, 128)


def _pack_small(d, names):
    rows = jnp.concatenate([_rows128(d[n]) for n in names])
    return jnp.pad(rows, ((0, (-rows.shape[0]) % 8), (0, 0)))


def _unpack_small(p, like, names):
    off, out = 0, {}
    for n in names:
        size = like[n].size
        rows = -(-size // 128)
        out[n] = p[off:off + rows].reshape(-1)[:size].reshape(like[n].shape)
        off += rows
    return out


def _owner_blocks(g, axis):
    sh = g.shape
    g = g.reshape(sh[:axis] + (4, 2, sh[axis] // N_DEV) + sh[axis + 1:])
    return jnp.moveaxis(g, (axis, axis + 1), (0, 1))


def kernel(x, ln_in_g, ln_in_b, w_in, b_in, sinks, vn_g, vn_b, w_s, b_s, p_a, p_b, w_out, b_out, ln_g, ln_b, loss_target, m_ln_in_g, m_ln_in_b, m_w_in, m_b_in, m_sinks, m_vn_g, m_vn_b, m_w_s, m_b_s, m_p_a, m_p_b, m_w_out, m_b_out, m_ln_g, m_ln_b, v_ln_in_g, v_ln_in_b, v_w_in, v_b_in, v_sinks, v_vn_g, v_vn_b, v_w_s, v_b_s, v_p_a, v_p_b, v_w_out, v_b_out, v_ln_g, v_ln_b):
    nseq, seq, _ = x.shape
    t = nseq * seq
    nblk_seq = seq // BLK
    x2 = x.reshape(t, D)
    tgt = loss_target.reshape(t, D)

    def turned(a):
        return jnp.swapaxes(a, 1, 2)

    w_in_t = turned(w_in)

    def blocks(l):
        return [w_in_t[l].astype(BF16), p_a[l].astype(BF16), p_b[l].astype(BF16), w_out[l].astype(BF16)]

    def full_weights(g):
        w_t_full = g[0].reshape(IN_COLS, D)
        pa_full = jnp.moveaxis(g[1], 0, 1).reshape(Q_W, D)
        pb_full = jnp.moveaxis(g[2], 0, 1).reshape(SGU_W, D)
        wo_full = g[3].reshape(D, D)
        return dict(w_t=w_t_full, pa=pa_full, pb=pb_full, wo=wo_full)

    def landing(bs):
        return [lax.empty((N_DEV,) + v.shape, v.dtype) for v in bs]

    def with_own(landed, sent):
        return [lax.dynamic_update_index_in_dim(g, b, me, 0) for g, b in zip(landed, sent)]

    me = 4 * lax.axis_index("x") + 2 * lax.axis_index("y") + lax.axis_index("c")
    blocks0 = blocks(0)
    a_send, a_recv, a_thru, a_token = _split_start(blocks0[:1], landing(blocks0[:1]), _plan_near, 4,
                                                   "allgather_w_in0_start")
    rest0 = [b + a_token[0, 0].astype(BF16) for b in blocks0[1:]]
    b_send, b_recv, b_thru, b_token = _split_start(rest0, landing(rest0), _plan_near, 4, "allgather_rest0_start")
    xs = [_ln_fwd(x2, ln_in_g + b_token[0, 0], ln_in_b, "ln_in_fwd")]
    sent, landed = _split_wait(a_send, a_recv, a_thru, _plan_near, xs[0], "allgather_w_in0_wait")
    gathered0 = with_own(_forward_sibling(landed, "allgather_w_in0_forward"), sent)
    blocks1, gathered0 = lax.optimization_barrier((blocks(1), gathered0))
    ag_send, ag_recv, ag_thru, ag_token = _split_start(blocks1, landing(blocks1), _plan_all, 7,
                                                       "allgather_weights1_start")
    weights = [None, None]
    bsb = jnp.broadcast_to(b_s[:, :, :, None], (DEPTH, 4, BLK, BLK))
    bias = _band_bias()

    saved = []
    for l in range(DEPTH):
        if l == 1:
            sent, landed = _split_wait(ag_send, ag_recv, ag_thru, _plan_all, xs[1], "allgather_weights1_wait")
            weights[1] = full_weights(with_own(landed, sent))
        w_t = weights[l]["w_t"] if l else gathered0[0].reshape(IN_COLS, D)
        last = l == DEPTH - 1
        b_l = b_in[l].reshape(1, -1) + (ag_token[0, 0] if l == 0 else 0.0)
        hm, hr = _inproj(xs[l], w_t, b_l, f"inproj{l}")
        ya, yb, prob, psink = _mixer_fwd(hm, sinks[l], bias, vn_g[l].reshape(1, -1), vn_b[l].reshape(1, -1),
                                         w_s[l], bsb[l], nblk_seq, f"mixer_fwd{l}")
        if l == 0:
            sent, landed = _split_wait(b_send, b_recv, b_thru, _plan_near, ya, "allgather_rest0_wait")
            weights[0] = full_weights(gathered0 + with_own(_forward_sibling(landed, "allgather_rest0_forward"), sent))
        wl = weights[l]
        outs = _tail_fwd(xs[l], ya, yb, hr, wl["pa"], wl["pb"], wl["wo"], b_out[l].reshape(1, D),
                         ln_g[l].reshape(1, D), ln_b[l].reshape(1, D), f"tail_fwd{l}", last)
        saved.append((hm, hr, ya, yb, prob, psink) + tuple(outs[:4]))
        if not last:
            xs.append(outs[4])

    small = {n: [None] * DEPTH for n in _SMALL}
    names = ("w_in", "p_a", "p_b", "w_out")
    owner_axis = {"w_in": 0, "p_a": 1, "p_b": 1, "w_out": 0}
    token = jnp.zeros((8, 128), F32)
    dx = tgt
    split = [None] * DEPTH
    for l in reversed(range(DEPTH)):
        hm, hr, ya, yb, prob, psink, pa, pb, merged, z = saved[l]
        wl = weights[l]
        dz, dpa, dpb, dhr, dya, dyb, acc, gbr = _tail_bwd(
            dx, z, pa, pb, hr, wl["wo"], wl["pa"], wl["pb"], ln_g[l].reshape(1, D) + token[0, 0],
            ln_b[l].reshape(1, D), f"tail_bwd{l}", l == DEPTH - 1)
        if l == DEPTH - 1:
            loss = lax.psum(acc[3, 0] * (0.5 / D), ("x", "y", "c"))
        dhm, gbm, gsk, gvn, gws, gbs = _mixer_bwd(
            hm, dya, dyb, prob, psink, vn_g[l].reshape(1, -1), vn_b[l].reshape(1, -1), w_s[l], bsb[l],
            f"mixer_bwd{l}")
        grads = {"w_in": jnp.concatenate([_wgrad(dhm, xs[l], MAIN_W // 2, f"wgrad_in_main{l}"),
                                          _wgrad(dhr, xs[l], R_W // 2, f"wgrad_in_route{l}")], axis=0),
                 "p_a": _wgrad(ya, dpa, Q_W, f"wgrad_pa{l}"), "p_b": _wgrad(yb, dpb, SGU_W, f"wgrad_pb{l}"),
                 "w_out": _wgrad(merged, dz, D, f"wgrad_out{l}")}
        small["b_in"][l] = jnp.concatenate([gbm[0], gbr[0]])
        small["sinks"][l] = gsk[:, 0]
        small["vn_g"][l], small["vn_b"][l] = gvn[0], gvn[1]
        small["w_s"][l], small["b_s"][l] = gws, gbs[:, :, 0]
        small["ln_g"][l], small["ln_b"][l], small["b_out"][l] = acc[0], acc[1], acc[2]
        parts = [_owner_blocks(grads[n], owner_axis[n]) for n in names]
        if l == 0:
            packed = _pack_small({n: jnp.stack(v) for n, v in small.items()}, _SMALL)
            parts.append(jnp.broadcast_to(packed[None, None], (1, 2) + packed.shape))
        from_sib = _swap_sibling(parts, f"rs_sibling{l}")
        pair = [_pair_sum(g, r, f"pair_sum{l}_{a}") for a, (g, r) in enumerate(zip(parts, from_sib))]
        if l == 0:
            pair[4] = jnp.broadcast_to(pair[4], (4,) + packed.shape)
        lands = [jnp.zeros(p.shape, p.dtype) for p in pair]
        split[l] = _split_start(pair, lands, _plan_chips, 3, f"rs_chips{l}_start")
        token = split[l][3]
        dx = _dx_inproj(dz, dhm, dhr, wl["w_t"], token, f"dx_inproj{l}",
                        ln_in=(x2, ln_in_g.reshape(1, D)) if l == 0 else None)
    grad_x, acc_in = dx
    (all_in,) = _allgather8([acc_in], "allgather_ln_in")

    given = {"w_in": (w_in_t, turned(m_w_in), turned(v_w_in)), "p_a": (p_a, m_p_a, v_p_a),
             "p_b": (p_b, m_p_b, v_p_b), "w_out": (w_out, m_w_out, v_w_out)}
    waited = [_split_wait(split[l][0], split[l][1], split[l][2], _plan_chips, all_in, f"rs_chips{l}_wait")
              for l in range(DEPTH)]
    res = {}
    for a, n in enumerate(names):
        rows, lanes = waited[0][1][a].shape[1:]
        outs = _adamw([waited[l][1][a] for l in range(DEPTH)], *[v.reshape(DEPTH * rows, lanes) for v in given[n]],
                      f"adamw_{n}", own=[waited[l][0][a] for l in range(DEPTH)])
        res[n] = [o.reshape(given[n][0].shape) for o in outs]
    res["w_in"] = [turned(o) for o in res["w_in"]]

    w_small = dict(ln_in_g=ln_in_g, ln_in_b=ln_in_b, b_in=b_in, sinks=sinks, vn_g=vn_g, vn_b=vn_b, w_s=w_s, b_s=b_s,
                   b_out=b_out, ln_g=ln_g, ln_b=ln_b)
    m_small = dict(ln_in_g=m_ln_in_g, ln_in_b=m_ln_in_b, b_in=m_b_in, sinks=m_sinks, vn_g=m_vn_g, vn_b=m_vn_b,
                   w_s=m_w_s, b_s=m_b_s, b_out=m_b_out, ln_g=m_ln_g, ln_b=m_ln_b)
    v_small = dict(ln_in_g=v_ln_in_g, ln_in_b=v_ln_in_b, b_in=v_b_in, sinks=v_sinks, vn_g=v_vn_g, vn_b=v_vn_b,
                   w_s=v_w_s, b_s=v_b_s, b_out=v_b_out, ln_g=v_ln_g, ln_b=v_ln_b)
    outs = _adamw([waited[0][1][4]], *[_pack_small(d, _SMALL) for d in (w_small, m_small, v_small)], "adamw_small",
                  own=[waited[0][0][4]])
    outs_in = _adamw([all_in], *[jnp.pad(jnp.stack([d[n] for n in _SMALL_IN]), ((0, 6), (0, 0)))
                                 for d in (w_small, m_small, v_small)], "adamw_ln_in")
    for k in range(4):
        u = _unpack_small(outs[k], w_small, _SMALL)
        u.update({n: outs_in[k][r] for r, n in enumerate(_SMALL_IN)})
        for n in u:
            res.setdefault(n, [None] * 4)[k] = u[n]

    order = ("ln_in_g", "ln_in_b", "w_in", "b_in", "sinks", "vn_g", "vn_b", "w_s", "b_s", "p_a", "p_b", "w_out",
             "b_out", "ln_g", "ln_b")
    return (loss, grad_x.reshape(x.shape), *[res[n][0] for n in order], *[res[n][1] for n in order],
            *[res[n][2] for n in order], *[res[n][3] for n in order])
```

```python
import jax
import jax.numpy as jnp
from jax import lax
from jax.experimental import pallas as pl
from jax.experimental.pallas import tpu as pltpu

F32 = jnp.float32
BF16 = jnp.bfloat16

D = 1024
BLK = 128
N_KV = 2
Q_W, KV_W, SGU_W = 512, 128, 512
C_Q, C_K, C_V, C_GA, C_UB, C_VB, C_GB = 0, 512, 640, 768, 1280, 1792, 2304
MAIN_W = 2816
R_W = 2048
IN_COLS = MAIN_W + R_W
N_DEV = 8
SHARD_COLS = IN_COLS // N_DEV

DEPTH = 2
ALPHA = (2.0 * DEPTH) ** 0.25
LN_EPS = 1e-5
ATTN_SCALE = 0.125
NEG = float(jnp.finfo(jnp.float32).min)

ADAM_LR, ADAM_B1, ADAM_B2, ADAM_EPS, ADAM_WD, ADAM_STEP = 0.001, 0.9, 0.999, 1e-08, 0.01, 10

TM = 256
TM_EW = 512
TM_MM = 512
NB = TM // BLK
MESH = pl.DeviceIdType.MESH
VMEM_LIMIT = 56 * 1024 * 1024

_ARB = pltpu.CompilerParams(dimension_semantics=("arbitrary",), vmem_limit_bytes=VMEM_LIMIT)


def _sigmoid(x):
    return 1.0 / (1.0 + jnp.exp(-x))


_GELU_C = 0.7978845608028654
_GELU_A = 0.044715


def _gelu_parts(x):
    x2 = x * x
    t = jnp.tanh(x * (_GELU_C + (_GELU_C * _GELU_A) * x2))
    hx = 0.5 * x
    return hx, t, x2


def _gelu(x):
    hx, t, _ = _gelu_parts(x)
    return hx + hx * t


def _gelu_and_grad(x):
    hx, t, x2 = _gelu_parts(x)
    grad = 0.5 + 0.5 * t + (hx - hx * (t * t)) * (_GELU_C + (3.0 * _GELU_C * _GELU_A) * x2)
    return hx + hx * t, grad


def _ln_stats(x):
    mu = jnp.mean(x, axis=-1, keepdims=True)
    xc = x - mu
    var = jnp.mean(xc * xc, axis=-1, keepdims=True)
    rstd = lax.rsqrt(var + LN_EPS)
    return xc * rstd, rstd


def _ln_bwd(dy_g, xhat, rstd):
    m1 = jnp.mean(dy_g, axis=-1, keepdims=True)
    m2 = jnp.mean(dy_g * xhat, axis=-1, keepdims=True)
    return rstd * (dy_g - m1 - xhat * m2)


def _colsum(x):
    return jnp.sum(x, axis=0, keepdims=True)


def _dot(a, b):
    return jnp.dot(a, b, preferred_element_type=F32)


def _dot_nt(a, b):
    return lax.dot_general(a, b, (((1,), (1,)), ((), ())), preferred_element_type=F32)


def _dot_tn(a, b):
    return lax.dot_general(a, b, (((0,), (0,)), ((), ())), preferred_element_type=F32)


def _head_place(hk, g):
    j = 4 * hk + g
    return j, j // 2, j % 2


def _head_rows(x, hk):
    d = lax.broadcasted_iota(jnp.int32, x.shape, 0)
    return jnp.where((d >= 64 * hk) & (d < 64 * hk + 64), x, 0.0).astype(BF16)


def _head_lanes(x, hk):
    d = lax.broadcasted_iota(jnp.int32, x.shape, 1)
    return jnp.where((d >= 64 * hk) & (d < 64 * hk + 64), x, 0.0)


def _band_bias():
    kpos = lax.broadcasted_iota(jnp.int32, (2 * BLK, 4 * BLK), 0)
    row = lax.broadcasted_iota(jnp.int32, (2 * BLK, 4 * BLK), 1) & (BLK - 1)
    band = (kpos > row) & (kpos <= row + BLK)
    return jnp.stack([jnp.where(band, 0.0, NEG), jnp.where(band & (kpos >= BLK), 0.0, NEG)]).astype(F32)


def _stack_q(q, hk):
    parts = []
    for g in range(4):
        _, p, pos = _head_place(hk, g)
        qp = q[:, BLK * p:BLK * (p + 1)] * ATTN_SCALE
        if pos != hk:
            qp = pltpu.roll(qp, 64, 1)
        parts.append(qp.astype(BF16))
    return jnp.concatenate(parts, axis=0)


def _attn_probs(q4, kh, hk, sinks_ref, bias):
    s_t = _dot_nt(kh, q4) + bias
    sink_row = jnp.concatenate(
        [jnp.full((1, BLK), sinks_ref[4 * hk + g], F32) for g in range(4)], axis=1)
    m = jnp.maximum(jnp.max(s_t, axis=0, keepdims=True), sink_row)
    p_un = jnp.exp(s_t - m)
    e_sink = jnp.exp(sink_row - m)
    inv = 1.0 / (jnp.sum(p_un, axis=0, keepdims=True) + e_sink)
    return (p_un * inv).astype(BF16), e_sink * inv


def _unstack_heads(x4, hk, pairs):
    for g in range(4):
        _, p, pos = _head_place(hk, g)
        xg = x4[BLK * g:BLK * (g + 1)]
        if pos != hk:
            xg = pltpu.roll(xg, 64, 1)
        pairs[p] = xg if pairs[p] is None else pairs[p] + xg
    return pairs


def _attn_fwd(q, kband, vband, sinks_ref, bias, save):
    pairs = [None] * 4
    vband_t = vband.T
    for hk in range(N_KV):
        prob_t, p_sink = _attn_probs(_stack_q(q, hk), _head_lanes(kband, hk).astype(BF16), hk, sinks_ref, bias)
        save(hk, prob_t, p_sink)
        o_t = _dot(_head_rows(vband_t, hk), prob_t)
        pairs = _unstack_heads(o_t.T, hk, pairs)
    return jnp.concatenate(pairs, axis=1)


def _tril_mask():
    r = lax.broadcasted_iota(jnp.int32, (BLK, BLK), 0)
    c = lax.broadcasted_iota(jnp.int32, (BLK, BLK), 1)
    return c <= r


def _sgu_fwd(u, v, vn_g, vn_b, wt, bsb_ref):
    vhat, rstd = _ln_stats(v)
    vn = vhat * vn_g + vn_b
    mixed = jnp.concatenate(
        [_dot(wt[g], vn[:, BLK * g:BLK * (g + 1)].astype(BF16)) + bsb_ref[g] for g in range(4)], axis=1)
    return vhat, rstd, vn, mixed


def _cols(ref, rows, col, width):
    return ref[rows, col:col + width].astype(F32)


def _band(hm_ref, hprev_ref, s, col):
    r0 = s * BLK
    cur = hm_ref[r0:r0 + BLK, col:col + KV_W]
    if s == 0:
        off = 0 if col == C_K else KV_W
        prev = hprev_ref[:, off:off + KV_W]
    else:
        prev = hm_ref[r0 - BLK:r0, col:col + KV_W]
    return jnp.concatenate([prev, cur], axis=0).astype(F32)


def _mixer_in_specs(nt, rev):
    def tile(g):
        return nt - 1 - g if rev else g

    return [
        pl.BlockSpec(memory_space=pltpu.SMEM),
        pl.BlockSpec((TM, MAIN_W), lambda g: (tile(g), 0)),
        pl.BlockSpec((BLK, 2 * KV_W), lambda g: (jnp.maximum(tile(g) * NB - 1, 0), 2)),
        pl.BlockSpec((2, 2 * BLK, 4 * BLK), lambda g: (0, 0, 0)),
    ]


_CONST2 = lambda g: (0, 0)
_CONST3 = lambda g: (0, 0, 0)


def _ln_fwd(x, g, b, name):
    t = x.shape[0]

    def body(x_ref, g_ref, b_ref, o_ref):
        xhat, _ = _ln_stats(x_ref[...])
        o_ref[...] = xhat * g_ref[...] + b_ref[...]

    return pl.pallas_call(
        body, name=name, grid=(t // TM_EW,),
        in_specs=[pl.BlockSpec((TM_EW, D), lambda i: (i, 0)), pl.BlockSpec((1, D), _CONST2),
                  pl.BlockSpec((1, D), _CONST2)],
        out_specs=pl.BlockSpec((TM_EW, D), lambda i: (i, 0)),
        out_shape=jax.ShapeDtypeStruct((t, D), F32), compiler_params=_ARB,
    )(x, g.reshape(1, D), b.reshape(1, D))


def _inproj(x, w_t, b, name):
    t = x.shape[0]

    def body(x_ref, wt_ref, b_ref, hm_ref, hr_ref):
        xb = x_ref[...].astype(BF16)
        hm_ref[...] = (_dot_nt(xb, wt_ref[0:MAIN_W, :]) + b_ref[:, 0:MAIN_W]).astype(BF16)
        hr_ref[...] = (_dot_nt(xb, wt_ref[MAIN_W:IN_COLS, :]) + b_ref[:, MAIN_W:IN_COLS]).astype(BF16)

    return pl.pallas_call(
        body, name=name, grid=(t // TM_MM,),
        in_specs=[pl.BlockSpec((TM_MM, D), lambda i: (i, 0)),
                  pl.BlockSpec((IN_COLS, D), _CONST2), pl.BlockSpec((1, IN_COLS), _CONST2)],
        out_specs=[pl.BlockSpec((TM_MM, MAIN_W), lambda i: (i, 0)), pl.BlockSpec((TM_MM, R_W), lambda i: (i, 0))],
        out_shape=[jax.ShapeDtypeStruct((t, MAIN_W), BF16), jax.ShapeDtypeStruct((t, R_W), BF16)],
        compiler_params=_ARB,
    )(x, w_t, b)


def _mixer_fwd(hm, sinks, bias, vn_g, vn_b, w_s, bsb, nblk_seq, name):
    t = hm.shape[0]
    nt = t // TM

    def body(sinks_ref, hm_ref, hprev_ref, bias_ref, vng_ref, vnb_ref, ws_ref, bsb_ref,
             ya_ref, yb_ref, prob_ref, psink_ref):
        i = pl.program_id(0)
        tril = _tril_mask()
        wt = [jnp.where(tril, ws_ref[g], 0.0).astype(BF16) for g in range(4)]
        for s in range(NB):
            r0 = s * BLK
            rows = slice(r0, r0 + BLK)
            bias = bias_ref[jnp.where((i * NB + s) % nblk_seq == 0, 1, 0)]

            def save(hk, prob_t, p_sink, s=s):
                prob_ref[N_KV * s + hk] = prob_t
                psink_ref[N_KV * s + hk] = jnp.broadcast_to(p_sink, (8, 4 * BLK))

            attn = _attn_fwd(_cols(hm_ref, rows, C_Q, Q_W), _band(hm_ref, hprev_ref, s, C_K),
                             _band(hm_ref, hprev_ref, s, C_V), sinks_ref, bias, save)
            g_a = _cols(hm_ref, rows, C_GA, Q_W)
            ya_ref[rows, :] = (attn * (g_a * _sigmoid(g_a))).astype(BF16)
            u = _gelu(_cols(hm_ref, rows, C_UB, SGU_W))
            mixed = _sgu_fwd(u, _gelu(_cols(hm_ref, rows, C_VB, SGU_W)), vng_ref[...], vnb_ref[...], wt, bsb_ref)[-1]
            g_b = _cols(hm_ref, rows, C_GB, SGU_W)
            yb_ref[rows, :] = (u * mixed * (g_b * _sigmoid(g_b))).astype(BF16)

    ngrp = N_KV * NB
    return pl.pallas_call(
        body, name=name, grid=(nt,),
        in_specs=_mixer_in_specs(nt, False) + [
            pl.BlockSpec((1, SGU_W), _CONST2), pl.BlockSpec((1, SGU_W), _CONST2),
            pl.BlockSpec((4, BLK, BLK), _CONST3), pl.BlockSpec((4, BLK, BLK), _CONST3)],
        out_specs=[pl.BlockSpec((TM, Q_W), lambda i: (i, 0)), pl.BlockSpec((TM, SGU_W), lambda i: (i, 0)),
                   pl.BlockSpec((ngrp, 2 * BLK, 4 * BLK), lambda i: (i, 0, 0)),
                   pl.BlockSpec((ngrp, 8, 4 * BLK), lambda i: (i, 0, 0))],
        out_shape=[jax.ShapeDtypeStruct((t, Q_W), BF16), jax.ShapeDtypeStruct((t, SGU_W), BF16),
                   jax.ShapeDtypeStruct((nt * ngrp, 2 * BLK, 4 * BLK), BF16),
                   jax.ShapeDtypeStruct((nt * ngrp, 8, 4 * BLK), F32)],
        compiler_params=_ARB,
    )(sinks, hm, hm, bias, vn_g, vn_b, w_s, bsb)


def _tail_fwd(x, ya, yb, hr, pa_w, pb_w, wo, b_out, ln_g, ln_b, name, last):
    t = x.shape[0]

    def body(x_ref, ya_ref, yb_ref, hr_ref, paw_ref, pbw_ref, wo_ref, bo_ref, g_ref, b_ref,
             pa_ref, pb_ref, mg_ref, z_ref, *xn_ref):
        pa = _dot(ya_ref[...], paw_ref[...])
        pb = _dot(yb_ref[...], pbw_ref[...])
        pa_ref[...] = pa.astype(BF16)
        pb_ref[...] = pb.astype(BF16)
        everything = slice(None)
        merged = _sigmoid(_cols(hr_ref, everything, 0, D)) * pa + _sigmoid(_cols(hr_ref, everything, D, D)) * pb
        mb = merged.astype(BF16)
        mg_ref[...] = mb
        z = ALPHA * x_ref[...] + (_dot(mb, wo_ref[...]) + bo_ref[...])
        z_ref[...] = z
        if not last:
            zhat, _ = _ln_stats(z)
            xn_ref[0][...] = zhat * g_ref[...] + b_ref[...]

    row = lambda w: pl.BlockSpec((TM, w), lambda i: (i, 0))
    vec = pl.BlockSpec((1, D), _CONST2)
    n_f32 = 1 if last else 2
    return pl.pallas_call(
        body, name=name, grid=(t // TM,),
        in_specs=[row(D), row(Q_W), row(SGU_W), row(R_W),
                  pl.BlockSpec((Q_W, D), _CONST2), pl.BlockSpec((SGU_W, D), _CONST2), pl.BlockSpec((D, D), _CONST2),
                  vec, vec, vec],
        out_specs=[row(D)] * (3 + n_f32),
        out_shape=[jax.ShapeDtypeStruct((t, D), BF16)] * 3 + [jax.ShapeDtypeStruct((t, D), F32)] * n_f32,
        compiler_params=_ARB,
    )(x, ya, yb, hr, pa_w, pb_w, wo, b_out, ln_g, ln_b)


def _tail_bwd(dxn, z, pa, pb, hr, wo, pa_w, pb_w, ln_g, ln_b, name, from_loss):
    t = dxn.shape[0]

    def body(dxn_ref, z_ref, pa_ref, pb_ref, hr_ref, wo_ref, paw_ref, pbw_ref, g_ref, b_ref,
             dz_ref, dpa_ref, dpb_ref, dhr_ref, dya_ref, dyb_ref, acc_ref, gbr_ref):
        @pl.when(pl.program_id(0) == 0)
        def _():
            acc_ref[...] = jnp.zeros_like(acc_ref)
            gbr_ref[...] = jnp.zeros_like(gbr_ref)

        zhat, rstd = _ln_stats(z_ref[...])
        if from_loss:
            err = zhat * g_ref[...] + b_ref[...] - dxn_ref[...]
            dxn_v = err * (1.0 / D)
            sq = jnp.sum(jnp.sum(err * err, axis=1, keepdims=True), axis=0, keepdims=True)
            acc_ref[3:4, :] += jnp.broadcast_to(sq, (1, D))
        else:
            dxn_v = dxn_ref[...]
        dz = _ln_bwd(dxn_v * g_ref[...], zhat, rstd)
        dz_ref[...] = dz
        acc_ref[0:1, :] += _colsum(dxn_v * zhat)
        acc_ref[1:2, :] += _colsum(dxn_v)
        acc_ref[2:3, :] += _colsum(dz)
        dmerged = _dot_nt(dz.astype(BF16), wo_ref[...])
        everything = slice(None)
        sa = _sigmoid(_cols(hr_ref, everything, 0, D))
        sb = _sigmoid(_cols(hr_ref, everything, D, D))
        dpa = (dmerged * sa).astype(BF16)
        dpb = (dmerged * sb).astype(BF16)
        dpa_ref[...] = dpa
        dpb_ref[...] = dpb
        dra = dmerged * pa_ref[...].astype(F32) * (sa * (1.0 - sa))
        drb = dmerged * pb_ref[...].astype(F32) * (sb * (1.0 - sb))
        dhr_ref[:, 0:D] = dra.astype(BF16)
        dhr_ref[:, D:2 * D] = drb.astype(BF16)
        gbr_ref[0:1, 0:D] += _colsum(dra)
        gbr_ref[0:1, D:2 * D] += _colsum(drb)
        dya_ref[...] = _dot_nt(dpa, paw_ref[...]).astype(BF16)
        dyb_ref[...] = _dot_nt(dpb, pbw_ref[...]).astype(BF16)

    row = lambda w: pl.BlockSpec((TM, w), lambda i: (i, 0))
    vec = pl.BlockSpec((1, D), _CONST2)
    return pl.pallas_call(
        body, name=name, grid=(t // TM,),
        in_specs=[row(D), row(D), row(D), row(D), row(R_W),
                  pl.BlockSpec((D, D), _CONST2), pl.BlockSpec((Q_W, D), _CONST2), pl.BlockSpec((SGU_W, D), _CONST2),
                  vec, vec],
        out_specs=[row(D), row(D), row(D), row(R_W), row(Q_W), row(SGU_W), pl.BlockSpec((8, D), _CONST2),
                   pl.BlockSpec((8, R_W), _CONST2)],
        out_shape=[jax.ShapeDtypeStruct((t, D), F32), jax.ShapeDtypeStruct((t, D), BF16),
                   jax.ShapeDtypeStruct((t, D), BF16), jax.ShapeDtypeStruct((t, R_W), BF16),
                   jax.ShapeDtypeStruct((t, Q_W), BF16), jax.ShapeDtypeStruct((t, SGU_W), BF16),
                   jax.ShapeDtypeStruct((8, D), F32), jax.ShapeDtypeStruct((8, R_W), F32)],
        compiler_params=_ARB,
    )(dxn, z, pa, pb, hr, wo, pa_w, pb_w, ln_g, ln_b)


def _mixer_bwd(hm, dya, dyb, prob, psink, vn_g, vn_b, w_s, bsb, name):
    t = hm.shape[0]
    nt = t // TM
    ngrp = N_KV * NB

    def body(hm_ref, hprev_ref, prob_ref, psink_ref, dya_ref, dyb_ref, vng_ref, vnb_ref, ws_ref, bsb_ref,
             dhm_ref, gbm_ref, gsk_ref, gvn_ref, gws_ref, gbs_ref, dk_carry, dv_carry):
        gi = pl.program_id(0)

        @pl.when(gi == 0)
        def _():
            for r in (gbm_ref, gsk_ref, gvn_ref, gws_ref, gbs_ref, dk_carry, dv_carry):
                r[...] = jnp.zeros_like(r)

        tril = _tril_mask()
        wt = [jnp.where(tril, ws_ref[g], 0.0).astype(BF16) for g in range(4)]
        vng = vng_ref[...]
        ones8 = jnp.ones((8, BLK), BF16)

        def put(rows, col, val):
            dhm_ref[rows, col:col + val.shape[1]] = val.astype(BF16)

        for s in reversed(range(NB)):
            r0 = s * BLK
            rows = slice(r0, r0 + BLK)
            q = _cols(hm_ref, rows, C_Q, Q_W)
            kband = _band(hm_ref, hprev_ref, s, C_K)
            vband = _band(hm_ref, hprev_ref, s, C_V)
            g_a = _cols(hm_ref, rows, C_GA, Q_W)
            sg = _sigmoid(g_a)
            dya_v = _cols(dya_ref, rows, 0, Q_W)
            d_o = dya_v * (g_a * sg)
            o_pairs, dq_pairs = [None] * 4, [None] * 4
            dkband = jnp.zeros((2 * BLK, KV_W), F32)
            dvband = jnp.zeros((2 * BLK, KV_W), F32)
            kband_t, vband_t = kband.T, vband.T
            for hk in range(N_KV):
                q4 = _stack_q(q, hk)
                prob_b = prob_ref[N_KV * s + hk]
                p_sink = psink_ref[N_KV * s + hk][0:1, :]
                o_t = _dot(_head_rows(vband_t, hk), prob_b)
                o_pairs = _unstack_heads(o_t.T, hk, o_pairs)
                parts = []
                for g in range(4):
                    _, p, pos = _head_place(hk, g)
                    dp = d_o[:, BLK * p:BLK * (p + 1)]
                    parts.append(pltpu.roll(dp, 64, 1) if pos != hk else dp)
                do4 = _head_lanes(jnp.concatenate(parts, axis=0), hk)
                do4b = do4.astype(BF16)
                delta = _colsum(do4.T * o_t)
                vh = _head_lanes(vband, hk).astype(BF16)
                ds_t = prob_b.astype(F32) * (_dot_nt(vh, do4b) - delta)
                dsb = ds_t.astype(BF16)
                dq4_t = _dot(_head_rows(kband_t, hk), dsb)
                dq_pairs = _unstack_heads(dq4_t.T * ATTN_SCALE, hk, dq_pairs)
                dkband = dkband + _head_lanes(_dot(dsb, q4), hk)
                dvband = dvband + _dot(prob_b, do4b)
                dsk = p_sink * delta
                for g in range(4):
                    j = 4 * hk + g
                    tot = jnp.sum(dsk[:, BLK * g:BLK * (g + 1)], axis=1, keepdims=True)
                    gsk_ref[j:j + 1, :] += jnp.broadcast_to(-tot, (1, 128))
            attn = jnp.concatenate(o_pairs, axis=1)
            put(rows, C_Q, jnp.concatenate(dq_pairs, axis=1))
            put(rows, C_K, dkband[BLK:2 * BLK] + dk_carry[...])
            put(rows, C_V, dvband[BLK:2 * BLK] + dv_carry[...])
            dk_carry[...] = dkband[0:BLK]
            dv_carry[...] = dvband[0:BLK]
            put(rows, C_GA, dya_v * attn * (sg * (1.0 + g_a * (1.0 - sg))))
            u, du_du_b = _gelu_and_grad(_cols(hm_ref, rows, C_UB, SGU_W))
            v, dv_dv_b = _gelu_and_grad(_cols(hm_ref, rows, C_VB, SGU_W))
            g_b = _cols(hm_ref, rows, C_GB, SGU_W)
            vhat, rstd, vn, mixed = _sgu_fwd(u, v, vng, vnb_ref[...], wt, bsb_ref)
            sgb = _sigmoid(g_b)
            silu_b = g_b * sgb
            dyb_v = _cols(dyb_ref, rows, 0, SGU_W)
            du = dyb_v * mixed * silu_b
            dmixed = dyb_v * u * silu_b
            put(rows, C_GB, dyb_v * u * mixed * (sgb * (1.0 + g_b * (1.0 - sgb))))
            dvn_parts = []
            for g in range(4):
                cols = slice(BLK * g, BLK * (g + 1))
                dmg = dmixed[:, cols]
                dmgb = dmg.astype(BF16)
                dvn_parts.append(_dot_tn(wt[g], dmgb))
                gws_ref[g] += jnp.where(tril, _dot_nt(dmgb, vn[:, cols].astype(BF16)), 0.0)
                gbs_ref[g] += dmg
            dvn = jnp.concatenate(dvn_parts, axis=1)
            gvn_ref[0:1, :] += _colsum(dvn * vhat)
            gvn_ref[1:2, :] += _colsum(dvn)
            dv = _ln_bwd(dvn * vng, vhat, rstd)
            put(rows, C_UB, du * du_du_b)
            put(rows, C_VB, dv * dv_dv_b)
            gbm_ref[...] += _dot(ones8, dhm_ref[rows, :])

        @pl.when(gi == nt - 1)
        def _():
            for g in range(4):
                gbs_ref[g] = jnp.broadcast_to(jnp.sum(gbs_ref[g], axis=1, keepdims=True), (BLK, BLK))

    row = lambda w: pl.BlockSpec((TM, w), lambda g: (nt - 1 - g, 0))
    return pl.pallas_call(
        body, name=name, grid=(nt,),
        in_specs=_mixer_in_specs(nt, True)[1:3] + [
            pl.BlockSpec((ngrp, 2 * BLK, 4 * BLK), lambda g: (nt - 1 - g, 0, 0)),
            pl.BlockSpec((ngrp, 8, 4 * BLK), lambda g: (nt - 1 - g, 0, 0)),
            row(Q_W), row(SGU_W),
            pl.BlockSpec((1, SGU_W), _CONST2), pl.BlockSpec((1, SGU_W), _CONST2),
            pl.BlockSpec((4, BLK, BLK), _CONST3), pl.BlockSpec((4, BLK, BLK), _CONST3)],
        out_specs=[row(MAIN_W), pl.BlockSpec((8, MAIN_W), _CONST2), pl.BlockSpec((8, 128), _CONST2),
                   pl.BlockSpec((8, SGU_W), _CONST2), pl.BlockSpec((4, BLK, BLK), _CONST3),
                   pl.BlockSpec((4, BLK, BLK), _CONST3)],
        out_shape=[jax.ShapeDtypeStruct((t, MAIN_W), BF16), jax.ShapeDtypeStruct((8, MAIN_W), F32),
                   jax.ShapeDtypeStruct((8, 128), F32), jax.ShapeDtypeStruct((8, SGU_W), F32),
                   jax.ShapeDtypeStruct((4, BLK, BLK), F32), jax.ShapeDtypeStruct((4, BLK, BLK), F32)],
        scratch_shapes=[pltpu.VMEM((BLK, KV_W), F32), pltpu.VMEM((BLK, KV_W), F32)],
        compiler_params=_ARB,
    )(hm, hm, prob, psink, dya, dyb, vn_g, vn_b, w_s, bsb)


def _dx_inproj(dz, dhm, dhr, w_t, after, name, ln_in=None):
    t = dz.shape[0]

    def body(dz_ref, dhm_ref, dhr_ref, wt_ref, after_ref, *rest):
        dx = (ALPHA * dz_ref[...] + after_ref[0:1, 0:1] + _dot(dhm_ref[...], wt_ref[0:MAIN_W, :])
              + _dot(dhr_ref[...], wt_ref[MAIN_W:IN_COLS, :]))
        if ln_in is None:
            rest[0][...] = dx
            return
        x_ref, g_ref, gx_ref, acc_ref = rest

        @pl.when(pl.program_id(0) == 0)
        def _():
            acc_ref[...] = jnp.zeros_like(acc_ref)

        xhat, rstd = _ln_stats(x_ref[...])
        gx_ref[...] = _ln_bwd(dx * g_ref[...], xhat, rstd)
        acc_ref[0:1, :] += _colsum(dx * xhat)
        acc_ref[1:2, :] += _colsum(dx)

    row = lambda w: pl.BlockSpec((TM_MM, w), lambda i: (i, 0))
    in_specs = [row(D), row(MAIN_W), row(R_W), pl.BlockSpec((IN_COLS, D), _CONST2), pl.BlockSpec((8, 128), _CONST2)]
    if ln_in is None:
        return pl.pallas_call(
            body, name=name, grid=(t // TM_MM,), in_specs=in_specs,
            out_specs=row(D), out_shape=jax.ShapeDtypeStruct((t, D), F32), compiler_params=_ARB,
        )(dz, dhm, dhr, w_t, after)
    return pl.pallas_call(
        body, name=name, grid=(t // TM_MM,), in_specs=in_specs + [row(D), pl.BlockSpec((1, D), _CONST2)],
        out_specs=[row(D), pl.BlockSpec((8, D), _CONST2)],
        out_shape=[jax.ShapeDtypeStruct((t, D), F32), jax.ShapeDtypeStruct((8, D), F32)], compiler_params=_ARB,
    )(dz, dhm, dhr, w_t, after, *ln_in)


def _wgrad(a, b, tm, name, rows=None, under=None):
    t, m = a.shape
    n = b.shape[1]
    tk = min(t, 2048)
    nk = t // tk

    def body(a_ref, b_ref, *rest):
        o_ref, acc_ref = rest[-2:]
        k = pl.program_id(1)

        @pl.when(k == 0)
        def _():
            acc_ref[...] = jnp.zeros_like(acc_ref)

        acc_ref[...] += _dot_tn(a_ref[...].astype(BF16), b_ref[...].astype(BF16))

        @pl.when(k == nk - 1)
        def _():
            o_ref[...] = acc_ref[...].astype(BF16)

    in_specs = [pl.BlockSpec((tk, tm), lambda j, k: (k, j)), pl.BlockSpec((tk, n), lambda j, k: (k, 0))]
    if under is None:
        out_rows, out_spec, operands, aliases = rows or m, pl.BlockSpec((tm, n), lambda j, k: (j, 0)), (a, b), {}
    else:
        out_rows = under.shape[0]
        first = out_rows - m
        assert first % 128 == 0 and tm % 128 == 0
        out_spec = pl.BlockSpec((pl.Element(tm), pl.Element(n)),
                                lambda j, k: (pl.multiple_of(first + j * tm, 128), 0))
        in_specs, operands, aliases = in_specs + [_ANY], (a, b, under), {2: 0}
    return pl.pallas_call(
        body, name=name, grid=(m // tm, nk), in_specs=in_specs, out_specs=out_spec,
        out_shape=jax.ShapeDtypeStruct((out_rows, n), BF16), input_output_aliases=aliases,
        scratch_shapes=[pltpu.VMEM((tm, n), F32)],
        compiler_params=pltpu.CompilerParams(dimension_semantics=("arbitrary", "arbitrary"), vmem_limit_bytes=VMEM_LIMIT),
    )(*operands)


_ANY = pl.BlockSpec(memory_space=pl.ANY)


def _place():
    return lax.axis_index("x"), lax.axis_index("y"), lax.axis_index("c")


def _allgather8(xs, name):
    n = len(xs)

    def body(*refs):
        x_refs, o_refs = refs[:n], refs[n:2 * n]
        send_sems, recv_sems, local_sems = refs[2 * n:]
        x, y, c = _place()
        me, sib = (x, y, c), (x, y, 1 - c)
        chips = [(1 - x, y), (x, 1 - y), (1 - x, 1 - y)]

        def copy(a, k, block, to, src=None):
            dst = o_refs[a].at[4 * block[0] + 2 * block[1] + block[2]]
            return pltpu.make_async_remote_copy(
                src_ref=dst if src is None else src, dst_ref=dst, send_sem=send_sems.at[7 * a + k],
                recv_sem=recv_sems.at[7 * a + k], device_id=to, device_id_type=MESH)

        mine = [pltpu.make_async_copy(x_refs[a], o_refs[a].at[4 * x + 2 * y + c], local_sems.at[a]) for a in range(n)]
        for cp in mine:
            cp.start()
        sent = []
        for a in range(n):
            sent.append(copy(a, 0, me, sib, src=x_refs[a]))
            sent += [copy(a, 1 + j, me, (*chip, c), src=x_refs[a]) for j, chip in enumerate(chips)]
        for cp in sent:
            cp.start()
        for j, chip in enumerate(chips):
            for a in range(n):
                copy(a, 1 + j, (*chip, c), me).wait_recv()
                fwd = copy(a, 4 + j, (*chip, c), sib)
                fwd.start()
                sent.append(fwd)
        for a in range(n):
            copy(a, 0, sib, me).wait_recv()
            for j, chip in enumerate(chips):
                copy(a, 4 + j, (*chip, 1 - c), me).wait_recv()
        for cp in sent:
            cp.wait_send()
        for cp in mine:
            cp.wait()

    return pl.pallas_call(
        body, name=name, in_specs=[_ANY] * n, out_specs=[_ANY] * n,
        out_shape=[jax.ShapeDtypeStruct((N_DEV,) + v.shape, v.dtype) for v in xs],
        scratch_shapes=[pltpu.SemaphoreType.DMA((7 * n,)), pltpu.SemaphoreType.DMA((7 * n,)),
                        pltpu.SemaphoreType.DMA((n,))],
    )(*xs)


def _forward_sibling(lands, name):
    n = len(lands)

    def body(*refs):
        l_refs = refs[n:2 * n]
        send_sems, recv_sems = refs[2 * n:]
        x, y, c = _place()
        chips = [(1 - x, y), (x, 1 - y), (1 - x, 1 - y)]

        def copy(a, j, core):
            rows = l_refs[a].at[4 * chips[j][0] + 2 * chips[j][1] + core]
            return pltpu.make_async_remote_copy(
                src_ref=rows, dst_ref=rows, send_sem=send_sems.at[3 * a + j], recv_sem=recv_sems.at[3 * a + j],
                device_id=(x, y, 1 - c), device_id_type=MESH)

        for a in range(n):
            for j in range(3):
                copy(a, j, c).start()
        for a in range(n):
            for j in range(3):
                copy(a, j, 1 - c).wait_recv()
                copy(a, j, c).wait_send()

    return pl.pallas_call(
        body, name=name, in_specs=[_ANY] * n, out_specs=[_ANY] * n,
        out_shape=[jax.ShapeDtypeStruct(v.shape, v.dtype) for v in lands],
        input_output_aliases={a: a for a in range(n)},
        scratch_shapes=[pltpu.SemaphoreType.DMA((3 * n,)), pltpu.SemaphoreType.DMA((3 * n,))],
    )(*lands)


def _swap_sibling(gs, name):
    n = len(gs)
    first = [0]
    for v in gs:
        first.append(first[-1] + v.shape[0])

    def body(*refs):
        g_refs, r_refs = refs[:n], refs[n:2 * n]
        send_sems, recv_sems = refs[2 * n:]
        x, y, c = _place()
        cps = [pltpu.make_async_remote_copy(
            src_ref=g_refs[a].at[q, 1 - c], dst_ref=r_refs[a].at[q], send_sem=send_sems.at[first[a] + q],
            recv_sem=recv_sems.at[first[a] + q], device_id=(x, y, 1 - c), device_id_type=MESH)
            for a in range(n) for q in range(gs[a].shape[0])]
        for cp in cps:
            cp.start()
        for cp in cps:
            cp.wait()

    return pl.pallas_call(
        body, name=name, in_specs=[_ANY] * n, out_specs=[_ANY] * n,
        out_shape=[jax.ShapeDtypeStruct(v.shape[:1] + v.shape[2:], v.dtype) for v in gs],
        scratch_shapes=[pltpu.SemaphoreType.DMA((first[-1],)), pltpu.SemaphoreType.DMA((first[-1],))],
    )(*gs)


def _row_tile(rows, lanes, cap):
    if rows * lanes * 4 <= (1 << 20):
        return rows
    return max(d for d in range(8, cap + 1, 8) if rows % d == 0 and (d % 16 == 0 or rows % 16 != 0))


def _pair_sums(gs, rs, name):
    n = len(gs)

    def add(g, r, dtype):
        return (g.astype(F32) + r.astype(F32)).astype(dtype)

    def body(c_ref, *refs):
        g_refs, r_refs, o_refs = refs[:n], refs[n:2 * n], refs[2 * n:]
        o_refs[0][...] = add(g_refs[0][0], r_refs[0][...], o_refs[0].dtype)

        @pl.when(pl.program_id(0) == 0)
        def _():
            for a in range(1, n):
                o_refs[a][...] = add(g_refs[a][:, 0], r_refs[a][...], o_refs[a].dtype)

    def whole(shape, mine):
        if mine:
            return pl.BlockSpec(shape, lambda q, c_ref: (0, c_ref[0]) + (0,) * (len(shape) - 2))
        return pl.BlockSpec(shape, lambda q, c_ref: (0,) * len(shape))

    big = gs[0].shape
    return pl.pallas_call(
        body, name=name,
        grid_spec=pltpu.PrefetchScalarGridSpec(
            num_scalar_prefetch=1, grid=(big[0],),
            in_specs=[pl.BlockSpec((1, 1) + big[2:], lambda q, c_ref: (q, c_ref[0], 0, 0))]
            + [whole(g.shape[:1] + (1,) + g.shape[2:], True) for g in gs[1:]]
            + [pl.BlockSpec((1,) + big[2:], lambda q, c_ref: (q, 0, 0))]
            + [whole(r.shape, False) for r in rs[1:]],
            out_specs=[pl.BlockSpec((1,) + big[2:], lambda q, c_ref: (q, 0, 0))]
            + [whole(r.shape, False) for r in rs[1:]]),
        out_shape=[jax.ShapeDtypeStruct(r.shape, g.dtype) for g, r in zip(gs, rs)],
        compiler_params=_ARB,
    )(lax.axis_index("c").astype(jnp.int32).reshape(1), *gs, *rs)


def _adamw(parts, w, m, v, name, own=None):
    nl = len(parts)
    ns, rows, l = parts[0].shape
    tr = _row_tile(rows, l * ns, 304)
    nt = rows // tr
    c1 = 1.0 - ADAM_B1 ** ADAM_STEP
    c2 = 1.0 - ADAM_B2 ** ADAM_STEP

    def body(q_ref, *refs):
        own_refs = refs[:nl] if own is not None else None
        p_refs = refs[-7 - nl:-7]
        w_ref, m_ref, v_ref, g_ref, d_ref, nm_ref, nv_ref = refs[-7:]
        layer = pl.program_id(0)
        g = None
        for j in range(nl):
            gj = None
            for k in range(ns):
                term = p_refs[j][k].astype(F32)
                if own_refs is not None:
                    term = jnp.where(q_ref[0] == k, own_refs[j][0].astype(F32), term)
                gj = term if gj is None else gj + term
            g = gj if g is None else jnp.where(layer == j, gj, g)
        g_ref[...] = g
        nm = ADAM_B1 * m_ref[...] + (1.0 - ADAM_B1) * g
        nv = ADAM_B2 * v_ref[...] + (1.0 - ADAM_B2) * (g * g)
        nm_ref[...] = nm
        nv_ref[...] = nv
        d_ref[...] = -ADAM_LR * ((nm / c1) / (jnp.sqrt(nv / c2) + ADAM_EPS) + ADAM_WD * w_ref[...])

    def tile_of(j):
        return lambda la, i, q: jnp.where(la == j, i, jnp.where(la < j, 0, nt - 1))

    row = pl.BlockSpec((tr, l), lambda la, i, q: (la * nt + i, 0))
    own_specs = [] if own is None else [
        pl.BlockSpec((1, tr, l), lambda la, i, q, j=j: (q[0], tile_of(j)(la, i, q), 0)) for j in range(nl)]
    part_specs = [pl.BlockSpec((ns, tr, l), lambda la, i, q, j=j: (0, tile_of(j)(la, i, q), 0)) for j in range(nl)]
    chip = (2 * lax.axis_index("x") + lax.axis_index("y")).astype(jnp.int32).reshape(1)
    return pl.pallas_call(
        body, name=name,
        grid_spec=pltpu.PrefetchScalarGridSpec(
            num_scalar_prefetch=1, grid=(nl, nt),
            in_specs=own_specs + part_specs + [row, row, row], out_specs=[row] * 4),
        out_shape=[jax.ShapeDtypeStruct((nl * rows, l), F32)] * 4,
        compiler_params=pltpu.CompilerParams(dimension_semantics=("arbitrary", "arbitrary"), vmem_limit_bytes=VMEM_LIMIT),
    )(chip, *([] if own is None else own), *parts, w, m, v)


_HBM = pl.BlockSpec(memory_space=pltpu.HBM)
_SEM = pl.BlockSpec(memory_space=pltpu.SEMAPHORE)
_EFFECT = pltpu.SideEffectType.DATAFLOW_SIDE_EFFECTING


def _plan_all(x, y, c):
    me = 4 * x + 2 * y + c
    peers = [(x, y, 1 - c), (1 - x, y, c), (x, 1 - y, c), (1 - x, 1 - y, c),
             (1 - x, y, 1 - c), (x, 1 - y, 1 - c), (1 - x, 1 - y, 1 - c)]
    return [(None, me, p, 4 * p[0] + 2 * p[1] + p[2]) for p in peers]


def _plan_near(x, y, c):
    me = 4 * x + 2 * y + c
    peers = [(x, y, 1 - c), (1 - x, y, c), (x, 1 - y, c), (1 - x, 1 - y, c)]
    return [(None, me, p, 4 * p[0] + 2 * p[1] + p[2]) for p in peers]


def _plan_chips(x, y, c):
    me = 2 * x + y
    return [(2 * qx + qy, me, (qx, qy, c), 2 * qx + qy) for qx, qy in ((1 - x, y), (x, 1 - y), (1 - x, 1 - y))]


def _split_copies(plan, src_refs, land_refs, send_sems, recv_sems, arrival):
    n = len(src_refs)
    entries = plan(*_place())
    per = len(entries)
    cps = []
    for a in range(n):
        for k, (src_slot, dst_slot, peer, back_slot) in enumerate(entries):
            src = src_refs[a] if src_slot is None else src_refs[a].at[src_slot]
            cps.append(pltpu.make_async_remote_copy(
                src_ref=src, dst_ref=land_refs[a].at[back_slot if arrival else dst_slot],
                send_sem=send_sems.at[per * a + k], recv_sem=recv_sems.at[per * a + k],
                device_id=peer, device_id_type=MESH))
    return cps


def _split_start(srcs, lands, plan, per, name):
    n = len(srcs)

    def body(*refs):
        for cp in _split_copies(plan, refs[:n], refs[n:2 * n], refs[2 * n], refs[2 * n + 1], False):
            cp.start()
        refs[-1][...] = jnp.zeros_like(refs[-1])

    both = list(srcs) + list(lands)
    outs = pl.pallas_call(
        body, name=name,
        out_shape=(pltpu.SemaphoreType.DMA((per * n,)), pltpu.SemaphoreType.DMA((per * n,)),
                   *[pltpu.HBM(v.shape, v.dtype) for v in both], jax.ShapeDtypeStruct((8, 128), F32)),
        in_specs=[_HBM] * (2 * n),
        out_specs=(_SEM, _SEM, *[_HBM] * (2 * n), pl.BlockSpec(memory_space=pltpu.VMEM)),
        input_output_aliases={i: 2 + i for i in range(2 * n)},
        compiler_params=pltpu.CompilerParams(has_side_effects=_EFFECT),
    )(*[pltpu.with_memory_space_constraint(v, pltpu.HBM) for v in both])
    return outs[0], outs[1], list(outs[2:2 + 2 * n]), outs[-1]


def _split_wait(send_sems, recv_sems, thru, plan, after, name):
    n = len(thru) // 2

    def body(*refs):
        for cp in _split_copies(plan, refs[:n], refs[n:2 * n], refs[2 * n], refs[2 * n + 1], True):
            cp.wait_send()
            cp.wait_recv()

    outs = pl.pallas_call(
        body, name=name, out_shape=tuple(pltpu.HBM(v.shape, v.dtype) for v in thru),
        in_specs=[_HBM] * (2 * n) + [_SEM, _SEM, pl.BlockSpec(memory_space=pl.ANY)],
        out_specs=[_HBM] * (2 * n), input_output_aliases={i: i for i in range(2 * n)},
        compiler_params=pltpu.CompilerParams(has_side_effects=_EFFECT),
    )(*thru, send_sems, recv_sems, after)
    return list(outs[:n]), list(outs[n:])


_SMALL_IN = ("ln_in_g", "ln_in_b")
_SMALL = ("b_in", "sinks", "vn_g", "vn_b", "w_s", "b_s", "b_out", "ln_g", "ln_b")


def _rows128(a):
    flat = a.reshape(-1)
    return jnp.pad(flat, (0, (-flat.shape[0]) % 128)).reshape(-1, 128)


def _pack_small(d, names):
    rows = jnp.concatenate([_rows128(d[n]) for n in names])
    return jnp.pad(rows, ((0, (-rows.shape[0]) % 8), (0, 0)))


def _unpack_small(p, like, names):
    off, out = 0, {}
    for n in names:
        size = like[n].size
        rows = -(-size // 128)
        out[n] = p[off:off + rows].reshape(-1)[:size].reshape(like[n].shape)
        off += rows
    return out


def _owner_blocks(g, axis):
    sh = g.shape
    g = g.reshape(sh[:axis] + (4, 2, sh[axis] // N_DEV) + sh[axis + 1:])
    return jnp.moveaxis(g, (axis, axis + 1), (0, 1))


def kernel(x, ln_in_g, ln_in_b, w_in, b_in, sinks, vn_g, vn_b, w_s, b_s, p_a, p_b, w_out, b_out, ln_g, ln_b, loss_target, m_ln_in_g, m_ln_in_b, m_w_in, m_b_in, m_sinks, m_vn_g, m_vn_b, m_w_s, m_b_s, m_p_a, m_p_b, m_w_out, m_b_out, m_ln_g, m_ln_b, v_ln_in_g, v_ln_in_b, v_w_in, v_b_in, v_sinks, v_vn_g, v_vn_b, v_w_s, v_b_s, v_p_a, v_p_b, v_w_out, v_b_out, v_ln_g, v_ln_b):
    nseq, seq, _ = x.shape
    t = nseq * seq
    nblk_seq = seq // BLK
    x2 = x.reshape(t, D)
    tgt = loss_target.reshape(t, D)

    def turned(a):
        return jnp.swapaxes(a, 1, 2)

    w_in_t = turned(w_in)

    def blocks(l):
        return [w_in_t[l].astype(BF16), p_a[l].astype(BF16), p_b[l].astype(BF16), w_out[l].astype(BF16)]

    def full_weights(g):
        w_t_full = g[0].reshape(IN_COLS, D)
        pa_full = jnp.moveaxis(g[1], 0, 1).reshape(Q_W, D)
        pb_full = jnp.moveaxis(g[2], 0, 1).reshape(SGU_W, D)
        wo_full = g[3].reshape(D, D)
        return dict(w_t=w_t_full, pa=pa_full, pb=pb_full, wo=wo_full)

    def landing(bs):
        return [lax.empty((N_DEV,) + v.shape, v.dtype) for v in bs]

    def with_own(landed, sent):
        return [lax.dynamic_update_index_in_dim(g, b, me, 0) for g, b in zip(landed, sent)]

    me = 4 * lax.axis_index("x") + 2 * lax.axis_index("y") + lax.axis_index("c")
    blocks0 = blocks(0)
    a_send, a_recv, a_thru, a_token = _split_start(blocks0[:1], landing(blocks0[:1]), _plan_near, 4,
                                                   "allgather_w_in0_start")
    rest0 = [b + a_token[0, 0].astype(BF16) for b in blocks0[1:]]
    b_send, b_recv, b_thru, b_token = _split_start(rest0, landing(rest0), _plan_near, 4, "allgather_rest0_start")
    xs = [_ln_fwd(x2, ln_in_g + b_token[0, 0], ln_in_b, "ln_in_fwd")]
    sent, landed = _split_wait(a_send, a_recv, a_thru, _plan_near, xs[0], "allgather_w_in0_wait")
    gathered0 = with_own(_forward_sibling(landed, "allgather_w_in0_forward"), sent)
    blocks1, gathered0 = lax.optimization_barrier((blocks(1), gathered0))
    ag_send, ag_recv, ag_thru, ag_token = _split_start(blocks1, landing(blocks1), _plan_all, 7,
                                                       "allgather_weights1_start")
    weights = [None, None]
    bsb = jnp.broadcast_to(b_s[:, :, :, None], (DEPTH, 4, BLK, BLK))
    bias = _band_bias()

    saved = []
    for l in range(DEPTH):
        if l == 1:
            sent, landed = _split_wait(ag_send, ag_recv, ag_thru, _plan_all, xs[1], "allgather_weights1_wait")
            weights[1] = full_weights(with_own(landed, sent))
        w_t = weights[l]["w_t"] if l else gathered0[0].reshape(IN_COLS, D)
        last = l == DEPTH - 1
        b_l = b_in[l].reshape(1, -1) + (ag_token[0, 0] if l == 0 else 0.0)
        hm, hr = _inproj(xs[l], w_t, b_l, f"inproj{l}")
        ya, yb, prob, psink = _mixer_fwd(hm, sinks[l], bias, vn_g[l].reshape(1, -1), vn_b[l].reshape(1, -1),
                                         w_s[l], bsb[l], nblk_seq, f"mixer_fwd{l}")
        if l == 0:
            sent, landed = _split_wait(b_send, b_recv, b_thru, _plan_near, ya, "allgather_rest0_wait")
            weights[0] = full_weights(gathered0 + with_own(_forward_sibling(landed, "allgather_rest0_forward"), sent))
        wl = weights[l]
        outs = _tail_fwd(xs[l], ya, yb, hr, wl["pa"], wl["pb"], wl["wo"], b_out[l].reshape(1, D),
                         ln_g[l].reshape(1, D), ln_b[l].reshape(1, D), f"tail_fwd{l}", last)
        saved.append((hm, hr, ya, yb, prob, psink) + tuple(outs[:4]))
        if not last:
            xs.append(outs[4])

    small = {n: [None] * DEPTH for n in _SMALL}
    names = ("w_in", "p_a", "p_b", "w_out")
    owner_axis = {"w_in": 0, "p_a": 1, "p_b": 1, "w_out": 0}
    token = jnp.zeros((8, 128), F32)
    dx = tgt
    split = [None] * DEPTH
    for l in reversed(range(DEPTH)):
        hm, hr, ya, yb, prob, psink, pa, pb, merged, z = saved[l]
        wl = weights[l]
        dz, dpa, dpb, dhr, dya, dyb, acc, gbr = _tail_bwd(
            dx, z, pa, pb, hr, wl["wo"], wl["pa"], wl["pb"], ln_g[l].reshape(1, D) + token[0, 0],
            ln_b[l].reshape(1, D), f"tail_bwd{l}", l == DEPTH - 1)
        if l == DEPTH - 1:
            loss = lax.psum(acc[3, 0] * (0.5 / D), ("x", "y", "c"))
        dhm, gbm, gsk, gvn, gws, gbs = _mixer_bwd(
            hm, dya, dyb, prob, psink, vn_g[l].reshape(1, -1), vn_b[l].reshape(1, -1), w_s[l], bsb[l],
            f"mixer_bwd{l}")
        grads = {"w_in": _wgrad(dhr, xs[l], R_W // 2, f"wgrad_in_route{l}",
                                under=_wgrad(dhm, xs[l], MAIN_W // 2, f"wgrad_in_main{l}", rows=IN_COLS)),
                 "p_a": _wgrad(ya, dpa, Q_W, f"wgrad_pa{l}"), "p_b": _wgrad(yb, dpb, SGU_W, f"wgrad_pb{l}"),
                 "w_out": _wgrad(merged, dz, D, f"wgrad_out{l}")}
        small["b_in"][l] = jnp.concatenate([gbm[0], gbr[0]])
        small["sinks"][l] = gsk[:, 0]
        small["vn_g"][l], small["vn_b"][l] = gvn[0], gvn[1]
        small["w_s"][l], small["b_s"][l] = gws, gbs[:, :, 0]
        small["ln_g"][l], small["ln_b"][l], small["b_out"][l] = acc[0], acc[1], acc[2]
        parts = [_owner_blocks(grads[n], owner_axis[n]) for n in names]
        if l == 0:
            packed = _pack_small({n: jnp.stack(v) for n, v in small.items()}, _SMALL)
            parts.append(jnp.broadcast_to(packed[None, None], (1, 2) + packed.shape))
        from_sib = _swap_sibling(parts, f"rs_sibling{l}")
        pair = list(_pair_sums(parts, from_sib, f"pair_sums{l}"))
        if l == 0:
            pair[4] = jnp.broadcast_to(pair[4], (4,) + packed.shape)
        lands = [jnp.zeros(p.shape, p.dtype) for p in pair]
        split[l] = _split_start(pair, lands, _plan_chips, 3, f"rs_chips{l}_start")
        token = split[l][3]
        dx = _dx_inproj(dz, dhm, dhr, wl["w_t"], token, f"dx_inproj{l}",
                        ln_in=(x2, ln_in_g.reshape(1, D)) if l == 0 else None)
    grad_x, acc_in = dx
    (all_in,) = _allgather8([acc_in], "allgather_ln_in")

    given = {"w_in": (w_in_t, turned(m_w_in), turned(v_w_in)), "p_a": (p_a, m_p_a, v_p_a),
             "p_b": (p_b, m_p_b, v_p_b), "w_out": (w_out, m_w_out, v_w_out)}
    waited = [_split_wait(split[l][0], split[l][1], split[l][2], _plan_chips, all_in, f"rs_chips{l}_wait")
              for l in range(DEPTH)]
    res = {}
    for a, n in enumerate(names):
        rows, lanes = waited[0][1][a].shape[1:]
        outs = _adamw([waited[l][1][a] for l in range(DEPTH)], *[v.reshape(DEPTH * rows, lanes) for v in given[n]],
                      f"adamw_{n}", own=[waited[l][0][a] for l in range(DEPTH)])
        res[n] = [o.reshape(given[n][0].shape) for o in outs]
    res["w_in"] = [turned(o) for o in res["w_in"]]

    w_small = dict(ln_in_g=ln_in_g, ln_in_b=ln_in_b, b_in=b_in, sinks=sinks, vn_g=vn_g, vn_b=vn_b, w_s=w_s, b_s=b_s,
                   b_out=b_out, ln_g=ln_g, ln_b=ln_b)
    m_small = dict(ln_in_g=m_ln_in_g, ln_in_b=m_ln_in_b, b_in=m_b_in, sinks=m_sinks, vn_g=m_vn_g, vn_b=m_vn_b,
                   w_s=m_w_s, b_s=m_b_s, b_out=m_b_out, ln_g=m_ln_g, ln_b=m_ln_b)
    v_small = dict(ln_in_g=v_ln_in_g, ln_in_b=v_ln_in_b, b_in=v_b_in, sinks=v_sinks, vn_g=v_vn_g, vn_b=v_vn_b,
                   w_s=v_w_s, b_s=v_b_s, b_out=v_b_out, ln_g=v_ln_g, ln_b=v_ln_b)
    outs = _adamw([waited[0][1][4]], *[_pack_small(d, _SMALL) for d in (w_small, m_small, v_small)], "adamw_small",
                  own=[waited[0][0][4]])
    outs_in = _adamw([all_in], *[jnp.pad(jnp.stack([d[n] for n in _SMALL_IN]), ((0, 6), (0, 0)))
                                 for d in (w_small, m_small, v_small)], "adamw_ln_in")
    for k in range(4):
        u = _unpack_small(outs[k], w_small, _SMALL)
        u.update({n: outs_in[k][r] for r, n in enumerate(_SMALL_IN)})
        for n in u:
            res.setdefault(n, [None] * 4)[k] = u[n]

    order = ("ln_in_g", "ln_in_b", "w_in", "b_in", "sinks", "vn_g", "vn_b", "w_s", "b_s", "p_a", "p_b", "w_out",
             "b_out", "ln_g", "ln_b")
    return (loss, grad_x.reshape(x.shape), *[res[n][0] for n in order], *[res[n][1] for n in order],
            *[res[n][2] for n in order], *[res[n][3] for n in order])
```

```python
import jax
import jax.numpy as jnp
from jax import lax
from jax.experimental import pallas as pl
from jax.experimental.pallas import tpu as pltpu

F32 = jnp.float32
BF16 = jnp.bfloat16

D = 1024
BLK = 128
N_KV = 2
Q_W, KV_W, SGU_W = 512, 128, 512
C_Q, C_K, C_V, C_GA, C_UB, C_VB, C_GB = 0, 512, 640, 768, 1280, 1792, 2304
MAIN_W = 2816
R_W = 2048
IN_COLS = MAIN_W + R_W
N_DEV = 8
SHARD_COLS = IN_COLS // N_DEV

DEPTH = 2
ALPHA = (2.0 * DEPTH) ** 0.25
LN_EPS = 1e-5
ATTN_SCALE = 0.125
NEG = float(jnp.finfo(jnp.float32).min)

ADAM_LR, ADAM_B1, ADAM_B2, ADAM_EPS, ADAM_WD, ADAM_STEP = 0.001, 0.9, 0.999, 1e-08, 0.01, 10

TM = 256
TM_EW = 512
TM_MM = 512
NB = TM // BLK
MESH = pl.DeviceIdType.MESH
VMEM_LIMIT = 56 * 1024 * 1024

_ARB = pltpu.CompilerParams(dimension_semantics=("arbitrary",), vmem_limit_bytes=VMEM_LIMIT)


def _sigmoid(x):
    return 1.0 / (1.0 + jnp.exp(-x))


_GELU_C = 0.7978845608028654
_GELU_A = 0.044715


def _gelu_parts(x):
    x2 = x * x
    t = jnp.tanh(x * (_GELU_C + (_GELU_C * _GELU_A) * x2))
    hx = 0.5 * x
    return hx, t, x2


def _gelu(x):
    hx, t, _ = _gelu_parts(x)
    return hx + hx * t


def _gelu_and_grad(x):
    hx, t, x2 = _gelu_parts(x)
    grad = 0.5 + 0.5 * t + (hx - hx * (t * t)) * (_GELU_C + (3.0 * _GELU_C * _GELU_A) * x2)
    return hx + hx * t, grad


def _ln_stats(x):
    mu = jnp.mean(x, axis=-1, keepdims=True)
    xc = x - mu
    var = jnp.mean(xc * xc, axis=-1, keepdims=True)
    rstd = lax.rsqrt(var + LN_EPS)
    return xc * rstd, rstd


def _ln_bwd(dy_g, xhat, rstd):
    m1 = jnp.mean(dy_g, axis=-1, keepdims=True)
    m2 = jnp.mean(dy_g * xhat, axis=-1, keepdims=True)
    return rstd * (dy_g - m1 - xhat * m2)


def _colsum(x):
    return jnp.sum(x, axis=0, keepdims=True)


def _dot(a, b):
    return jnp.dot(a, b, preferred_element_type=F32)


def _dot_nt(a, b):
    return lax.dot_general(a, b, (((1,), (1,)), ((), ())), preferred_element_type=F32)


def _dot_tn(a, b):
    return lax.dot_general(a, b, (((0,), (0,)), ((), ())), preferred_element_type=F32)


def _head_place(hk, g):
    j = 4 * hk + g
    return j, j // 2, j % 2


def _head_rows(x, hk):
    d = lax.broadcasted_iota(jnp.int32, x.shape, 0)
    return jnp.where((d >= 64 * hk) & (d < 64 * hk + 64), x, 0.0).astype(BF16)


def _head_lanes(x, hk):
    d = lax.broadcasted_iota(jnp.int32, x.shape, 1)
    return jnp.where((d >= 64 * hk) & (d < 64 * hk + 64), x, 0.0)


def _band_bias():
    kpos = lax.broadcasted_iota(jnp.int32, (2 * BLK, 4 * BLK), 0)
    row = lax.broadcasted_iota(jnp.int32, (2 * BLK, 4 * BLK), 1) & (BLK - 1)
    band = (kpos > row) & (kpos <= row + BLK)
    return jnp.stack([jnp.where(band, 0.0, NEG), jnp.where(band & (kpos >= BLK), 0.0, NEG)]).astype(F32)


def _stack_q(q, hk):
    parts = []
    for g in range(4):
        _, p, pos = _head_place(hk, g)
        qp = q[:, BLK * p:BLK * (p + 1)] * ATTN_SCALE
        if pos != hk:
            qp = pltpu.roll(qp, 64, 1)
        parts.append(qp.astype(BF16))
    return jnp.concatenate(parts, axis=0)


def _attn_probs(q4, kh, hk, sinks_ref, bias):
    s_t = _dot_nt(kh, q4) + bias
    sink_row = jnp.concatenate(
        [jnp.full((1, BLK), sinks_ref[4 * hk + g], F32) for g in range(4)], axis=1)
    m = jnp.maximum(jnp.max(s_t, axis=0, keepdims=True), sink_row)
    p_un = jnp.exp(s_t - m)
    e_sink = jnp.exp(sink_row - m)
    inv = 1.0 / (jnp.sum(p_un, axis=0, keepdims=True) + e_sink)
    return (p_un * inv).astype(BF16), e_sink * inv


def _unstack_heads(x4, hk, pairs):
    for g in range(4):
        _, p, pos = _head_place(hk, g)
        xg = x4[BLK * g:BLK * (g + 1)]
        if pos != hk:
            xg = pltpu.roll(xg, 64, 1)
        pairs[p] = xg if pairs[p] is None else pairs[p] + xg
    return pairs


def _attn_fwd(q, kband, vband, sinks_ref, bias, save):
    pairs = [None] * 4
    vband_t = vband.T
    for hk in range(N_KV):
        prob_t, p_sink = _attn_probs(_stack_q(q, hk), _head_lanes(kband, hk).astype(BF16), hk, sinks_ref, bias)
        save(hk, prob_t, p_sink)
        o_t = _dot(_head_rows(vband_t, hk), prob_t)
        pairs = _unstack_heads(o_t.T, hk, pairs)
    return jnp.concatenate(pairs, axis=1)


def _tril_mask():
    r = lax.broadcasted_iota(jnp.int32, (BLK, BLK), 0)
    c = lax.broadcasted_iota(jnp.int32, (BLK, BLK), 1)
    return c <= r


def _sgu_fwd(u, v, vn_g, vn_b, wt, bsb_ref):
    vhat, rstd = _ln_stats(v)
    vn = vhat * vn_g + vn_b
    mixed = jnp.concatenate(
        [_dot(wt[g], vn[:, BLK * g:BLK * (g + 1)].astype(BF16)) + bsb_ref[g] for g in range(4)], axis=1)
    return vhat, rstd, vn, mixed


def _cols(ref, rows, col, width):
    return ref[rows, col:col + width].astype(F32)


def _band(hm_ref, hprev_ref, s, col):
    r0 = s * BLK
    cur = hm_ref[r0:r0 + BLK, col:col + KV_W]
    if s == 0:
        off = 0 if col == C_K else KV_W
        prev = hprev_ref[:, off:off + KV_W]
    else:
        prev = hm_ref[r0 - BLK:r0, col:col + KV_W]
    return jnp.concatenate([prev, cur], axis=0).astype(F32)


def _mixer_in_specs(nt, rev):
    def tile(g):
        return nt - 1 - g if rev else g

    return [
        pl.BlockSpec(memory_space=pltpu.SMEM),
        pl.BlockSpec((TM, MAIN_W), lambda g: (tile(g), 0)),
        pl.BlockSpec((BLK, 2 * KV_W), lambda g: (jnp.maximum(tile(g) * NB - 1, 0), 2)),
        pl.BlockSpec((2, 2 * BLK, 4 * BLK), lambda g: (0, 0, 0)),
    ]


_CONST2 = lambda g: (0, 0)
_CONST3 = lambda g: (0, 0, 0)


def _ln_fwd(x, g, b, name):
    t = x.shape[0]

    def body(x_ref, g_ref, b_ref, o_ref):
        xhat, _ = _ln_stats(x_ref[...])
        o_ref[...] = xhat * g_ref[...] + b_ref[...]

    return pl.pallas_call(
        body, name=name, grid=(t // TM_EW,),
        in_specs=[pl.BlockSpec((TM_EW, D), lambda i: (i, 0)), pl.BlockSpec((1, D), _CONST2),
                  pl.BlockSpec((1, D), _CONST2)],
        out_specs=pl.BlockSpec((TM_EW, D), lambda i: (i, 0)),
        out_shape=jax.ShapeDtypeStruct((t, D), F32), compiler_params=_ARB,
    )(x, g.reshape(1, D), b.reshape(1, D))


def _inproj(x, w_t, b, name):
    t = x.shape[0]

    def body(x_ref, wt_ref, b_ref, hm_ref, hr_ref):
        xb = x_ref[...].astype(BF16)
        hm_ref[...] = (_dot_nt(xb, wt_ref[0:MAIN_W, :]) + b_ref[:, 0:MAIN_W]).astype(BF16)
        hr_ref[...] = (_dot_nt(xb, wt_ref[MAIN_W:IN_COLS, :]) + b_ref[:, MAIN_W:IN_COLS]).astype(BF16)

    return pl.pallas_call(
        body, name=name, grid=(t // TM_MM,),
        in_specs=[pl.BlockSpec((TM_MM, D), lambda i: (i, 0)),
                  pl.BlockSpec((IN_COLS, D), _CONST2), pl.BlockSpec((1, IN_COLS), _CONST2)],
        out_specs=[pl.BlockSpec((TM_MM, MAIN_W), lambda i: (i, 0)), pl.BlockSpec((TM_MM, R_W), lambda i: (i, 0))],
        out_shape=[jax.ShapeDtypeStruct((t, MAIN_W), BF16), jax.ShapeDtypeStruct((t, R_W), BF16)],
        compiler_params=_ARB,
    )(x, w_t, b)


def _mixer_fwd(hm, sinks, bias, vn_g, vn_b, w_s, bsb, nblk_seq, name):
    t = hm.shape[0]
    nt = t // TM

    def body(sinks_ref, hm_ref, hprev_ref, bias_ref, vng_ref, vnb_ref, ws_ref, bsb_ref,
             ya_ref, yb_ref, prob_ref, psink_ref):
        i = pl.program_id(0)
        tril = _tril_mask()
        wt = [jnp.where(tril, ws_ref[g], 0.0).astype(BF16) for g in range(4)]
        for s in range(NB):
            r0 = s * BLK
            rows = slice(r0, r0 + BLK)
            bias = bias_ref[jnp.where((i * NB + s) % nblk_seq == 0, 1, 0)]

            def save(hk, prob_t, p_sink, s=s):
                prob_ref[N_KV * s + hk] = prob_t
                psink_ref[N_KV * s + hk] = jnp.broadcast_to(p_sink, (8, 4 * BLK))

            attn = _attn_fwd(_cols(hm_ref, rows, C_Q, Q_W), _band(hm_ref, hprev_ref, s, C_K),
                             _band(hm_ref, hprev_ref, s, C_V), sinks_ref, bias, save)
            g_a = _cols(hm_ref, rows, C_GA, Q_W)
            ya_ref[rows, :] = (attn * (g_a * _sigmoid(g_a))).astype(BF16)
            u = _gelu(_cols(hm_ref, rows, C_UB, SGU_W))
            mixed = _sgu_fwd(u, _gelu(_cols(hm_ref, rows, C_VB, SGU_W)), vng_ref[...], vnb_ref[...], wt, bsb_ref)[-1]
            g_b = _cols(hm_ref, rows, C_GB, SGU_W)
            yb_ref[rows, :] = (u * mixed * (g_b * _sigmoid(g_b))).astype(BF16)

    ngrp = N_KV * NB
    return pl.pallas_call(
        body, name=name, grid=(nt,),
        in_specs=_mixer_in_specs(nt, False) + [
            pl.BlockSpec((1, SGU_W), _CONST2), pl.BlockSpec((1, SGU_W), _CONST2),
            pl.BlockSpec((4, BLK, BLK), _CONST3), pl.BlockSpec((4, BLK, BLK), _CONST3)],
        out_specs=[pl.BlockSpec((TM, Q_W), lambda i: (i, 0)), pl.BlockSpec((TM, SGU_W), lambda i: (i, 0)),
                   pl.BlockSpec((ngrp, 2 * BLK, 4 * BLK), lambda i: (i, 0, 0)),
                   pl.BlockSpec((ngrp, 8, 4 * BLK), lambda i: (i, 0, 0))],
        out_shape=[jax.ShapeDtypeStruct((t, Q_W), BF16), jax.ShapeDtypeStruct((t, SGU_W), BF16),
                   jax.ShapeDtypeStruct((nt * ngrp, 2 * BLK, 4 * BLK), BF16),
                   jax.ShapeDtypeStruct((nt * ngrp, 8, 4 * BLK), F32)],
        compiler_params=_ARB,
    )(sinks, hm, hm, bias, vn_g, vn_b, w_s, bsb)


def _tail_fwd(x, ya, yb, hr, pa_w, pb_w, wo, b_out, ln_g, ln_b, name, last):
    t = x.shape[0]

    def body(x_ref, ya_ref, yb_ref, hr_ref, paw_ref, pbw_ref, wo_ref, bo_ref, g_ref, b_ref,
             pa_ref, pb_ref, mg_ref, z_ref, *xn_ref):
        pa = _dot(ya_ref[...], paw_ref[...])
        pb = _dot(yb_ref[...], pbw_ref[...])
        pa_ref[...] = pa.astype(BF16)
        pb_ref[...] = pb.astype(BF16)
        everything = slice(None)
        merged = _sigmoid(_cols(hr_ref, everything, 0, D)) * pa + _sigmoid(_cols(hr_ref, everything, D, D)) * pb
        mb = merged.astype(BF16)
        mg_ref[...] = mb
        z = ALPHA * x_ref[...] + (_dot(mb, wo_ref[...]) + bo_ref[...])
        z_ref[...] = z
        if not last:
            zhat, _ = _ln_stats(z)
            xn_ref[0][...] = zhat * g_ref[...] + b_ref[...]

    row = lambda w: pl.BlockSpec((TM, w), lambda i: (i, 0))
    vec = pl.BlockSpec((1, D), _CONST2)
    n_f32 = 1 if last else 2
    return pl.pallas_call(
        body, name=name, grid=(t // TM,),
        in_specs=[row(D), row(Q_W), row(SGU_W), row(R_W),
                  pl.BlockSpec((Q_W, D), _CONST2), pl.BlockSpec((SGU_W, D), _CONST2), pl.BlockSpec((D, D), _CONST2),
                  vec, vec, vec],
        out_specs=[row(D)] * (3 + n_f32),
        out_shape=[jax.ShapeDtypeStruct((t, D), BF16)] * 3 + [jax.ShapeDtypeStruct((t, D), F32)] * n_f32,
        compiler_params=_ARB,
    )(x, ya, yb, hr, pa_w, pb_w, wo, b_out, ln_g, ln_b)


def _tail_bwd(dxn, z, pa, pb, hr, wo, pa_w, pb_w, ln_g, ln_b, name, from_loss):
    t = dxn.shape[0]

    def body(dxn_ref, z_ref, pa_ref, pb_ref, hr_ref, wo_ref, paw_ref, pbw_ref, g_ref, b_ref,
             dz_ref, dpa_ref, dpb_ref, dhr_ref, dya_ref, dyb_ref, acc_ref, gbr_ref):
        @pl.when(pl.program_id(0) == 0)
        def _():
            acc_ref[...] = jnp.zeros_like(acc_ref)
            gbr_ref[...] = jnp.zeros_like(gbr_ref)

        zhat, rstd = _ln_stats(z_ref[...])
        if from_loss:
            err = zhat * g_ref[...] + b_ref[...] - dxn_ref[...]
            dxn_v = err * (1.0 / D)
            sq = jnp.sum(jnp.sum(err * err, axis=1, keepdims=True), axis=0, keepdims=True)
            acc_ref[3:4, :] += jnp.broadcast_to(sq, (1, D))
        else:
            dxn_v = dxn_ref[...]
        dz = _ln_bwd(dxn_v * g_ref[...], zhat, rstd)
        dz_ref[...] = dz
        acc_ref[0:1, :] += _colsum(dxn_v * zhat)
        acc_ref[1:2, :] += _colsum(dxn_v)
        acc_ref[2:3, :] += _colsum(dz)
        dmerged = _dot_nt(dz.astype(BF16), wo_ref[...])
        everything = slice(None)
        sa = _sigmoid(_cols(hr_ref, everything, 0, D))
        sb = _sigmoid(_cols(hr_ref, everything, D, D))
        dpa = (dmerged * sa).astype(BF16)
        dpb = (dmerged * sb).astype(BF16)
        dpa_ref[...] = dpa
        dpb_ref[...] = dpb
        dra = dmerged * pa_ref[...].astype(F32) * (sa * (1.0 - sa))
        drb = dmerged * pb_ref[...].astype(F32) * (sb * (1.0 - sb))
        dhr_ref[:, 0:D] = dra.astype(BF16)
        dhr_ref[:, D:2 * D] = drb.astype(BF16)
        gbr_ref[0:1, 0:D] += _colsum(dra)
        gbr_ref[0:1, D:2 * D] += _colsum(drb)
        dya_ref[...] = _dot_nt(dpa, paw_ref[...]).astype(BF16)
        dyb_ref[...] = _dot_nt(dpb, pbw_ref[...]).astype(BF16)

    row = lambda w: pl.BlockSpec((TM, w), lambda i: (i, 0))
    vec = pl.BlockSpec((1, D), _CONST2)
    return pl.pallas_call(
        body, name=name, grid=(t // TM,),
        in_specs=[row(D), row(D), row(D), row(D), row(R_W),
                  pl.BlockSpec((D, D), _CONST2), pl.BlockSpec((Q_W, D), _CONST2), pl.BlockSpec((SGU_W, D), _CONST2),
                  vec, vec],
        out_specs=[row(D), row(D), row(D), row(R_W), row(Q_W), row(SGU_W), pl.BlockSpec((8, D), _CONST2),
                   pl.BlockSpec((8, R_W), _CONST2)],
        out_shape=[jax.ShapeDtypeStruct((t, D), F32), jax.ShapeDtypeStruct((t, D), BF16),
                   jax.ShapeDtypeStruct((t, D), BF16), jax.ShapeDtypeStruct((t, R_W), BF16),
                   jax.ShapeDtypeStruct((t, Q_W), BF16), jax.ShapeDtypeStruct((t, SGU_W), BF16),
                   jax.ShapeDtypeStruct((8, D), F32), jax.ShapeDtypeStruct((8, R_W), F32)],
        compiler_params=_ARB,
    )(dxn, z, pa, pb, hr, wo, pa_w, pb_w, ln_g, ln_b)


def _mixer_bwd(hm, dya, dyb, prob, psink, vn_g, vn_b, w_s, bsb, name):
    t = hm.shape[0]
    nt = t // TM
    ngrp = N_KV * NB

    def body(hm_ref, hprev_ref, prob_ref, psink_ref, dya_ref, dyb_ref, vng_ref, vnb_ref, ws_ref, bsb_ref,
             dhm_ref, gbm_ref, gsk_ref, gvn_ref, gws_ref, gbs_ref, dk_carry, dv_carry):
        gi = pl.program_id(0)

        @pl.when(gi == 0)
        def _():
            for r in (gbm_ref, gsk_ref, gvn_ref, gws_ref, gbs_ref, dk_carry, dv_carry):
                r[...] = jnp.zeros_like(r)

        tril = _tril_mask()
        wt = [jnp.where(tril, ws_ref[g], 0.0).astype(BF16) for g in range(4)]
        vng = vng_ref[...]
        ones8 = jnp.ones((8, BLK), BF16)

        def put(rows, col, val):
            dhm_ref[rows, col:col + val.shape[1]] = val.astype(BF16)

        for s in reversed(range(NB)):
            r0 = s * BLK
            rows = slice(r0, r0 + BLK)
            q = _cols(hm_ref, rows, C_Q, Q_W)
            kband = _band(hm_ref, hprev_ref, s, C_K)
            vband = _band(hm_ref, hprev_ref, s, C_V)
            g_a = _cols(hm_ref, rows, C_GA, Q_W)
            sg = _sigmoid(g_a)
            dya_v = _cols(dya_ref, rows, 0, Q_W)
            d_o = dya_v * (g_a * sg)
            o_pairs, dq_pairs = [None] * 4, [None] * 4
            dkband = jnp.zeros((2 * BLK, KV_W), F32)
            dvband = jnp.zeros((2 * BLK, KV_W), F32)
            kband_t, vband_t = kband.T, vband.T
            for hk in range(N_KV):
                q4 = _stack_q(q, hk)
                prob_b = prob_ref[N_KV * s + hk]
                p_sink = psink_ref[N_KV * s + hk][0:1, :]
                o_t = _dot(_head_rows(vband_t, hk), prob_b)
                o_pairs = _unstack_heads(o_t.T, hk, o_pairs)
                parts = []
                for g in range(4):
                    _, p, pos = _head_place(hk, g)
                    dp = d_o[:, BLK * p:BLK * (p + 1)]
                    parts.append(pltpu.roll(dp, 64, 1) if pos != hk else dp)
                do4 = _head_lanes(jnp.concatenate(parts, axis=0), hk)
                do4b = do4.astype(BF16)
                delta = _colsum(do4.T * o_t)
                vh = _head_lanes(vband, hk).astype(BF16)
                ds_t = prob_b.astype(F32) * (_dot_nt(vh, do4b) - delta)
                dsb = ds_t.astype(BF16)
                dq4_t = _dot(_head_rows(kband_t, hk), dsb)
                dq_pairs = _unstack_heads(dq4_t.T * ATTN_SCALE, hk, dq_pairs)
                dkband = dkband + _head_lanes(_dot(dsb, q4), hk)
                dvband = dvband + _dot(prob_b, do4b)
                dsk = p_sink * delta
                for g in range(4):
                    j = 4 * hk + g
                    tot = jnp.sum(dsk[:, BLK * g:BLK * (g + 1)], axis=1, keepdims=True)
                    gsk_ref[j:j + 1, :] += jnp.broadcast_to(-tot, (1, 128))
            attn = jnp.concatenate(o_pairs, axis=1)
            put(rows, C_Q, jnp.concatenate(dq_pairs, axis=1))
            put(rows, C_K, dkband[BLK:2 * BLK] + dk_carry[...])
            put(rows, C_V, dvband[BLK:2 * BLK] + dv_carry[...])
            dk_carry[...] = dkband[0:BLK]
            dv_carry[...] = dvband[0:BLK]
            put(rows, C_GA, dya_v * attn * (sg * (1.0 + g_a * (1.0 - sg))))
            u, du_du_b = _gelu_and_grad(_cols(hm_ref, rows, C_UB, SGU_W))
            v, dv_dv_b = _gelu_and_grad(_cols(hm_ref, rows, C_VB, SGU_W))
            g_b = _cols(hm_ref, rows, C_GB, SGU_W)
            vhat, rstd, vn, mixed = _sgu_fwd(u, v, vng, vnb_ref[...], wt, bsb_ref)
            sgb = _sigmoid(g_b)
            silu_b = g_b * sgb
            dyb_v = _cols(dyb_ref, rows, 0, SGU_W)
            du = dyb_v * mixed * silu_b
            dmixed = dyb_v * u * silu_b
            put(rows, C_GB, dyb_v * u * mixed * (sgb * (1.0 + g_b * (1.0 - sgb))))
            dvn_parts = []
            for g in range(4):
                cols = slice(BLK * g, BLK * (g + 1))
                dmg = dmixed[:, cols]
                dmgb = dmg.astype(BF16)
                dvn_parts.append(_dot_tn(wt[g], dmgb))
                gws_ref[g] += jnp.where(tril, _dot_nt(dmgb, vn[:, cols].astype(BF16)), 0.0)
                gbs_ref[g] += dmg
            dvn = jnp.concatenate(dvn_parts, axis=1)
            gvn_ref[0:1, :] += _colsum(dvn * vhat)
            gvn_ref[1:2, :] += _colsum(dvn)
            dv = _ln_bwd(dvn * vng, vhat, rstd)
            put(rows, C_UB, du * du_du_b)
            put(rows, C_VB, dv * dv_dv_b)
            gbm_ref[...] += _dot(ones8, dhm_ref[rows, :])

        @pl.when(gi == nt - 1)
        def _():
            for g in range(4):
                gbs_ref[g] = jnp.broadcast_to(jnp.sum(gbs_ref[g], axis=1, keepdims=True), (BLK, BLK))

    row = lambda w: pl.BlockSpec((TM, w), lambda g: (nt - 1 - g, 0))
    return pl.pallas_call(
        body, name=name, grid=(nt,),
        in_specs=_mixer_in_specs(nt, True)[1:3] + [
            pl.BlockSpec((ngrp, 2 * BLK, 4 * BLK), lambda g: (nt - 1 - g, 0, 0)),
            pl.BlockSpec((ngrp, 8, 4 * BLK), lambda g: (nt - 1 - g, 0, 0)),
            row(Q_W), row(SGU_W),
            pl.BlockSpec((1, SGU_W), _CONST2), pl.BlockSpec((1, SGU_W), _CONST2),
            pl.BlockSpec((4, BLK, BLK), _CONST3), pl.BlockSpec((4, BLK, BLK), _CONST3)],
        out_specs=[row(MAIN_W), pl.BlockSpec((8, MAIN_W), _CONST2), pl.BlockSpec((8, 128), _CONST2),
                   pl.BlockSpec((8, SGU_W), _CONST2), pl.BlockSpec((4, BLK, BLK), _CONST3),
                   pl.BlockSpec((4, BLK, BLK), _CONST3)],
        out_shape=[jax.ShapeDtypeStruct((t, MAIN_W), BF16), jax.ShapeDtypeStruct((8, MAIN_W), F32),
                   jax.ShapeDtypeStruct((8, 128), F32), jax.ShapeDtypeStruct((8, SGU_W), F32),
                   jax.ShapeDtypeStruct((4, BLK, BLK), F32), jax.ShapeDtypeStruct((4, BLK, BLK), F32)],
        scratch_shapes=[pltpu.VMEM((BLK, KV_W), F32), pltpu.VMEM((BLK, KV_W), F32)],
        compiler_params=_ARB,
    )(hm, hm, prob, psink, dya, dyb, vn_g, vn_b, w_s, bsb)


def _dx_inproj(dz, dhm, dhr, w_t, after, name, ln_in=None):
    t = dz.shape[0]

    def body(dz_ref, dhm_ref, dhr_ref, wt_ref, after_ref, *rest):
        dx = (ALPHA * dz_ref[...] + after_ref[0:1, 0:1] + _dot(dhm_ref[...], wt_ref[0:MAIN_W, :])
              + _dot(dhr_ref[...], wt_ref[MAIN_W:IN_COLS, :]))
        if ln_in is None:
            rest[0][...] = dx
            return
        x_ref, g_ref, gx_ref, acc_ref = rest

        @pl.when(pl.program_id(0) == 0)
        def _():
            acc_ref[...] = jnp.zeros_like(acc_ref)

        xhat, rstd = _ln_stats(x_ref[...])
        gx_ref[...] = _ln_bwd(dx * g_ref[...], xhat, rstd)
        acc_ref[0:1, :] += _colsum(dx * xhat)
        acc_ref[1:2, :] += _colsum(dx)

    row = lambda w: pl.BlockSpec((TM_MM, w), lambda i: (i, 0))
    in_specs = [row(D), row(MAIN_W), row(R_W), pl.BlockSpec((IN_COLS, D), _CONST2), pl.BlockSpec((8, 128), _CONST2)]
    if ln_in is None:
        return pl.pallas_call(
            body, name=name, grid=(t // TM_MM,), in_specs=in_specs,
            out_specs=row(D), out_shape=jax.ShapeDtypeStruct((t, D), F32), compiler_params=_ARB,
        )(dz, dhm, dhr, w_t, after)
    return pl.pallas_call(
        body, name=name, grid=(t // TM_MM,), in_specs=in_specs + [row(D), pl.BlockSpec((1, D), _CONST2)],
        out_specs=[row(D), pl.BlockSpec((8, D), _CONST2)],
        out_shape=[jax.ShapeDtypeStruct((t, D), F32), jax.ShapeDtypeStruct((8, D), F32)], compiler_params=_ARB,
    )(dz, dhm, dhr, w_t, after, *ln_in)


def _wgrad(a, b, tm, name, rows=None, under=None):
    t, m = a.shape
    n = b.shape[1]
    tk = min(t, 2048)
    nk = t // tk

    def body(a_ref, b_ref, *rest):
        o_ref, acc_ref = rest[-2:]
        k = pl.program_id(1)

        @pl.when(k == 0)
        def _():
            acc_ref[...] = jnp.zeros_like(acc_ref)

        acc_ref[...] += _dot_tn(a_ref[...].astype(BF16), b_ref[...].astype(BF16))

        @pl.when(k == nk - 1)
        def _():
            o_ref[...] = acc_ref[...].astype(BF16)

    in_specs = [pl.BlockSpec((tk, tm), lambda j, k: (k, j)), pl.BlockSpec((tk, n), lambda j, k: (k, 0))]
    if under is None:
        out_rows, out_spec, operands, aliases = rows or m, pl.BlockSpec((tm, n), lambda j, k: (j, 0)), (a, b), {}
    else:
        out_rows = under.shape[0]
        first = out_rows - m
        assert first % 128 == 0 and tm % 128 == 0
        out_spec = pl.BlockSpec((pl.Element(tm), pl.Element(n)),
                                lambda j, k: (pl.multiple_of(first + j * tm, 128), 0))
        in_specs, operands, aliases = in_specs + [_ANY], (a, b, under), {2: 0}
    return pl.pallas_call(
        body, name=name, grid=(m // tm, nk), in_specs=in_specs, out_specs=out_spec,
        out_shape=jax.ShapeDtypeStruct((out_rows, n), BF16), input_output_aliases=aliases,
        scratch_shapes=[pltpu.VMEM((tm, n), F32)],
        compiler_params=pltpu.CompilerParams(dimension_semantics=("arbitrary", "arbitrary"), vmem_limit_bytes=VMEM_LIMIT),
    )(*operands)


_ANY = pl.BlockSpec(memory_space=pl.ANY)


def _place():
    return lax.axis_index("x"), lax.axis_index("y"), lax.axis_index("c")


def _allgather8(xs, name):
    n = len(xs)

    def body(*refs):
        x_refs, o_refs = refs[:n], refs[n:2 * n]
        send_sems, recv_sems, local_sems = refs[2 * n:]
        x, y, c = _place()
        me, sib = (x, y, c), (x, y, 1 - c)
        chips = [(1 - x, y), (x, 1 - y), (1 - x, 1 - y)]

        def copy(a, k, block, to, src=None):
            dst = o_refs[a].at[4 * block[0] + 2 * block[1] + block[2]]
            return pltpu.make_async_remote_copy(
                src_ref=dst if src is None else src, dst_ref=dst, send_sem=send_sems.at[7 * a + k],
                recv_sem=recv_sems.at[7 * a + k], device_id=to, device_id_type=MESH)

        mine = [pltpu.make_async_copy(x_refs[a], o_refs[a].at[4 * x + 2 * y + c], local_sems.at[a]) for a in range(n)]
        for cp in mine:
            cp.start()
        sent = []
        for a in range(n):
            sent.append(copy(a, 0, me, sib, src=x_refs[a]))
            sent += [copy(a, 1 + j, me, (*chip, c), src=x_refs[a]) for j, chip in enumerate(chips)]
        for cp in sent:
            cp.start()
        for j, chip in enumerate(chips):
            for a in range(n):
                copy(a, 1 + j, (*chip, c), me).wait_recv()
                fwd = copy(a, 4 + j, (*chip, c), sib)
                fwd.start()
                sent.append(fwd)
        for a in range(n):
            copy(a, 0, sib, me).wait_recv()
            for j, chip in enumerate(chips):
                copy(a, 4 + j, (*chip, 1 - c), me).wait_recv()
        for cp in sent:
            cp.wait_send()
        for cp in mine:
            cp.wait()

    return pl.pallas_call(
        body, name=name, in_specs=[_ANY] * n, out_specs=[_ANY] * n,
        out_shape=[jax.ShapeDtypeStruct((N_DEV,) + v.shape, v.dtype) for v in xs],
        scratch_shapes=[pltpu.SemaphoreType.DMA((7 * n,)), pltpu.SemaphoreType.DMA((7 * n,)),
                        pltpu.SemaphoreType.DMA((n,))],
    )(*xs)


def _forward_sibling(lands, name):
    n = len(lands)

    def body(*refs):
        l_refs = refs[n:2 * n]
        send_sems, recv_sems = refs[2 * n:]
        x, y, c = _place()
        chips = [(1 - x, y), (x, 1 - y), (1 - x, 1 - y)]

        def copy(a, j, core):
            rows = l_refs[a].at[4 * chips[j][0] + 2 * chips[j][1] + core]
            return pltpu.make_async_remote_copy(
                src_ref=rows, dst_ref=rows, send_sem=send_sems.at[3 * a + j], recv_sem=recv_sems.at[3 * a + j],
                device_id=(x, y, 1 - c), device_id_type=MESH)

        for a in range(n):
            for j in range(3):
                copy(a, j, c).start()
        for a in range(n):
            for j in range(3):
                copy(a, j, 1 - c).wait_recv()
                copy(a, j, c).wait_send()

    return pl.pallas_call(
        body, name=name, in_specs=[_ANY] * n, out_specs=[_ANY] * n,
        out_shape=[jax.ShapeDtypeStruct(v.shape, v.dtype) for v in lands],
        input_output_aliases={a: a for a in range(n)},
        scratch_shapes=[pltpu.SemaphoreType.DMA((3 * n,)), pltpu.SemaphoreType.DMA((3 * n,))],
    )(*lands)


def _swap_sibling(gs, name):
    n = len(gs)
    first = [0]
    for v in gs:
        first.append(first[-1] + v.shape[0])

    def body(*refs):
        g_refs, r_refs = refs[:n], refs[n:2 * n]
        send_sems, recv_sems = refs[2 * n:]
        x, y, c = _place()
        cps = [pltpu.make_async_remote_copy(
            src_ref=g_refs[a].at[q, 1 - c], dst_ref=r_refs[a].at[q], send_sem=send_sems.at[first[a] + q],
            recv_sem=recv_sems.at[first[a] + q], device_id=(x, y, 1 - c), device_id_type=MESH)
            for a in range(n) for q in range(gs[a].shape[0])]
        for cp in cps:
            cp.start()
        for cp in cps:
            cp.wait()

    return pl.pallas_call(
        body, name=name, in_specs=[_ANY] * n, out_specs=[_ANY] * n,
        out_shape=[jax.ShapeDtypeStruct(v.shape[:1] + v.shape[2:], v.dtype) for v in gs],
        scratch_shapes=[pltpu.SemaphoreType.DMA((first[-1],)), pltpu.SemaphoreType.DMA((first[-1],))],
    )(*gs)


def _row_tile(rows, lanes, cap):
    if rows * lanes * 4 <= (1 << 20):
        return rows
    return max(d for d in range(8, cap + 1, 8) if rows % d == 0 and (d % 16 == 0 or rows % 16 != 0))


def _pair_sums(gs, rs, name):
    n = len(gs)

    def add(g, r, dtype):
        return (g.astype(F32) + r.astype(F32)).astype(dtype)

    def body(c_ref, *refs):
        g_refs, r_refs, o_refs = refs[:n], refs[n:2 * n], refs[2 * n:]
        o_refs[0][...] = add(g_refs[0][0], r_refs[0][...], o_refs[0].dtype)

        @pl.when(pl.program_id(0) == 0)
        def _():
            for a in range(1, n):
                o_refs[a][...] = add(g_refs[a][:, 0], r_refs[a][...], o_refs[a].dtype)

    def whole(shape, mine):
        if mine:
            return pl.BlockSpec(shape, lambda q, c_ref: (0, c_ref[0]) + (0,) * (len(shape) - 2))
        return pl.BlockSpec(shape, lambda q, c_ref: (0,) * len(shape))

    big = gs[0].shape
    return pl.pallas_call(
        body, name=name,
        grid_spec=pltpu.PrefetchScalarGridSpec(
            num_scalar_prefetch=1, grid=(big[0],),
            in_specs=[pl.BlockSpec((1, 1) + big[2:], lambda q, c_ref: (q, c_ref[0], 0, 0))]
            + [whole(g.shape[:1] + (1,) + g.shape[2:], True) for g in gs[1:]]
            + [pl.BlockSpec((1,) + big[2:], lambda q, c_ref: (q, 0, 0))]
            + [whole(r.shape, False) for r in rs[1:]],
            out_specs=[pl.BlockSpec((1,) + big[2:], lambda q, c_ref: (q, 0, 0))]
            + [whole(r.shape, False) for r in rs[1:]]),
        out_shape=[jax.ShapeDtypeStruct(r.shape, g.dtype) for g, r in zip(gs, rs)],
        compiler_params=_ARB,
    )(lax.axis_index("c").astype(jnp.int32).reshape(1), *gs, *rs)


def _adamw(parts, w, m, v, name, own=None):
    nl = len(parts)
    ns, rows, l = parts[0].shape
    tr = _row_tile(rows, l * ns, 304)
    nt = rows // tr
    c1 = 1.0 - ADAM_B1 ** ADAM_STEP
    c2 = 1.0 - ADAM_B2 ** ADAM_STEP

    def body(q_ref, *refs):
        own_refs = refs[:nl] if own is not None else None
        p_refs = refs[-7 - nl:-7]
        w_ref, m_ref, v_ref, g_ref, d_ref, nm_ref, nv_ref = refs[-7:]
        layer = pl.program_id(0)
        g = None
        for j in range(nl):
            gj = None
            for k in range(ns):
                term = p_refs[j][k].astype(F32)
                if own_refs is not None:
                    term = jnp.where(q_ref[0] == k, own_refs[j][0].astype(F32), term)
                gj = term if gj is None else gj + term
            g = gj if g is None else jnp.where(layer == j, gj, g)
        g_ref[...] = g
        nm = ADAM_B1 * m_ref[...] + (1.0 - ADAM_B1) * g
        nv = ADAM_B2 * v_ref[...] + (1.0 - ADAM_B2) * (g * g)
        nm_ref[...] = nm
        nv_ref[...] = nv
        d_ref[...] = -ADAM_LR * ((nm / c1) / (jnp.sqrt(nv / c2) + ADAM_EPS) + ADAM_WD * w_ref[...])

    def tile_of(j):
        return lambda la, i, q: jnp.where(la == j, i, jnp.where(la < j, 0, nt - 1))

    row = pl.BlockSpec((tr, l), lambda la, i, q: (la * nt + i, 0))
    own_specs = [] if own is None else [
        pl.BlockSpec((1, tr, l), lambda la, i, q, j=j: (q[0], tile_of(j)(la, i, q), 0)) for j in range(nl)]
    part_specs = [pl.BlockSpec((ns, tr, l), lambda la, i, q, j=j: (0, tile_of(j)(la, i, q), 0)) for j in range(nl)]
    chip = (2 * lax.axis_index("x") + lax.axis_index("y")).astype(jnp.int32).reshape(1)
    return pl.pallas_call(
        body, name=name,
        grid_spec=pltpu.PrefetchScalarGridSpec(
            num_scalar_prefetch=1, grid=(nl, nt),
            in_specs=own_specs + part_specs + [row, row, row], out_specs=[row] * 4),
        out_shape=[jax.ShapeDtypeStruct((nl * rows, l), F32)] * 4,
        compiler_params=pltpu.CompilerParams(dimension_semantics=("arbitrary", "arbitrary"), vmem_limit_bytes=VMEM_LIMIT),
    )(chip, *([] if own is None else own), *parts, w, m, v)


_HBM = pl.BlockSpec(memory_space=pltpu.HBM)
_SEM = pl.BlockSpec(memory_space=pltpu.SEMAPHORE)
_EFFECT = pltpu.SideEffectType.DATAFLOW_SIDE_EFFECTING


def _plan_all(x, y, c):
    me = 4 * x + 2 * y + c
    peers = [(x, y, 1 - c), (1 - x, y, c), (x, 1 - y, c), (1 - x, 1 - y, c),
             (1 - x, y, 1 - c), (x, 1 - y, 1 - c), (1 - x, 1 - y, 1 - c)]
    return [(None, me, p, 4 * p[0] + 2 * p[1] + p[2]) for p in peers]


def _plan_near(x, y, c):
    me = 4 * x + 2 * y + c
    peers = [(x, y, 1 - c), (1 - x, y, c), (x, 1 - y, c), (1 - x, 1 - y, c)]
    return [(None, me, p, 4 * p[0] + 2 * p[1] + p[2]) for p in peers]


def _plan_sibling(x, y, c):
    return [(2 * q + 1 - c, q, (x, y, 1 - c), q) for q in range(4)]


def _plan_chips(x, y, c):
    me = 2 * x + y
    return [(2 * qx + qy, me, (qx, qy, c), 2 * qx + qy) for qx, qy in ((1 - x, y), (x, 1 - y), (1 - x, 1 - y))]


def _split_copies(plan, src_refs, land_refs, send_sems, recv_sems, arrival):
    n = len(src_refs)
    entries = plan(*_place())
    per = len(entries)
    cps = []
    for a in range(n):
        for k, (src_slot, dst_slot, peer, back_slot) in enumerate(entries):
            src = src_refs[a] if src_slot is None else src_refs[a].at[src_slot]
            cps.append(pltpu.make_async_remote_copy(
                src_ref=src, dst_ref=land_refs[a].at[back_slot if arrival else dst_slot],
                send_sem=send_sems.at[per * a + k], recv_sem=recv_sems.at[per * a + k],
                device_id=peer, device_id_type=MESH))
    return cps


def _split_start(srcs, lands, plan, per, name):
    n = len(srcs)

    def body(*refs):
        for cp in _split_copies(plan, refs[:n], refs[n:2 * n], refs[2 * n], refs[2 * n + 1], False):
            cp.start()
        refs[-1][...] = jnp.zeros_like(refs[-1])

    both = list(srcs) + list(lands)
    outs = pl.pallas_call(
        body, name=name,
        out_shape=(pltpu.SemaphoreType.DMA((per * n,)), pltpu.SemaphoreType.DMA((per * n,)),
                   *[pltpu.HBM(v.shape, v.dtype) for v in both], jax.ShapeDtypeStruct((8, 128), F32)),
        in_specs=[_HBM] * (2 * n),
        out_specs=(_SEM, _SEM, *[_HBM] * (2 * n), pl.BlockSpec(memory_space=pltpu.VMEM)),
        input_output_aliases={i: 2 + i for i in range(2 * n)},
        compiler_params=pltpu.CompilerParams(has_side_effects=_EFFECT),
    )(*[pltpu.with_memory_space_constraint(v, pltpu.HBM) for v in both])
    return outs[0], outs[1], list(outs[2:2 + 2 * n]), outs[-1]


def _split_wait(send_sems, recv_sems, thru, plan, after, name):
    n = len(thru) // 2

    def body(*refs):
        for cp in _split_copies(plan, refs[:n], refs[n:2 * n], refs[2 * n], refs[2 * n + 1], True):
            cp.wait_send()
            cp.wait_recv()

    outs = pl.pallas_call(
        body, name=name, out_shape=tuple(pltpu.HBM(v.shape, v.dtype) for v in thru),
        in_specs=[_HBM] * (2 * n) + [_SEM, _SEM, pl.BlockSpec(memory_space=pl.ANY)],
        out_specs=[_HBM] * (2 * n), input_output_aliases={i: i for i in range(2 * n)},
        compiler_params=pltpu.CompilerParams(has_side_effects=_EFFECT),
    )(*thru, send_sems, recv_sems, after)
    return list(outs[:n]), list(outs[n:])


_SMALL_IN = ("ln_in_g", "ln_in_b")
_SMALL = ("w_s", "b_in", "sinks", "vn_g", "vn_b", "b_s", "b_out", "ln_g", "ln_b")


def _rows128(a):
    flat = a.reshape(-1)
    return jnp.pad(flat, (0, (-flat.shape[0]) % 128)).reshape(-1, 128)


def _pack_small(d, names):
    rows = jnp.concatenate([_rows128(d[n]) for n in names])
    return jnp.pad(rows, ((0, (-rows.shape[0]) % 8), (0, 0)))


def _unpack_small(p, like, names):
    off, out = 0, {}
    for n in names:
        size = like[n].size
        rows = -(-size // 128)
        out[n] = p[off:off + rows].reshape(-1)[:size].reshape(like[n].shape)
        off += rows
    return out


def _owner_blocks(g, axis):
    sh = g.shape
    g = g.reshape(sh[:axis] + (4, 2, sh[axis] // N_DEV) + sh[axis + 1:])
    return jnp.moveaxis(g, (axis, axis + 1), (0, 1))


def kernel(x, ln_in_g, ln_in_b, w_in, b_in, sinks, vn_g, vn_b, w_s, b_s, p_a, p_b, w_out, b_out, ln_g, ln_b, loss_target, m_ln_in_g, m_ln_in_b, m_w_in, m_b_in, m_sinks, m_vn_g, m_vn_b, m_w_s, m_b_s, m_p_a, m_p_b, m_w_out, m_b_out, m_ln_g, m_ln_b, v_ln_in_g, v_ln_in_b, v_w_in, v_b_in, v_sinks, v_vn_g, v_vn_b, v_w_s, v_b_s, v_p_a, v_p_b, v_w_out, v_b_out, v_ln_g, v_ln_b):
    nseq, seq, _ = x.shape
    t = nseq * seq
    nblk_seq = seq // BLK
    x2 = x.reshape(t, D)
    tgt = loss_target.reshape(t, D)

    def turned(a):
        return jnp.swapaxes(a, 1, 2)

    w_in_t = turned(w_in)

    def blocks(l):
        return [w_in_t[l].astype(BF16), p_a[l].astype(BF16), p_b[l].astype(BF16), w_out[l].astype(BF16)]

    def full_weights(g):
        w_t_full = g[0].reshape(IN_COLS, D)
        pa_full = jnp.moveaxis(g[1], 0, 1).reshape(Q_W, D)
        pb_full = jnp.moveaxis(g[2], 0, 1).reshape(SGU_W, D)
        wo_full = g[3].reshape(D, D)
        return dict(w_t=w_t_full, pa=pa_full, pb=pb_full, wo=wo_full)

    def landing(bs):
        return [lax.empty((N_DEV,) + v.shape, v.dtype) for v in bs]

    def with_own(landed, sent):
        return [lax.dynamic_update_index_in_dim(g, b, me, 0) for g, b in zip(landed, sent)]

    me = 4 * lax.axis_index("x") + 2 * lax.axis_index("y") + lax.axis_index("c")
    blocks0 = blocks(0)
    a_send, a_recv, a_thru, a_token = _split_start(blocks0[:1], landing(blocks0[:1]), _plan_near, 4,
                                                   "allgather_w_in0_start")
    rest0 = [b + a_token[0, 0].astype(BF16) for b in blocks0[1:]]
    b_send, b_recv, b_thru, b_token = _split_start(rest0, landing(rest0), _plan_near, 4, "allgather_rest0_start")
    xs = [_ln_fwd(x2, ln_in_g + b_token[0, 0], ln_in_b, "ln_in_fwd")]
    sent, landed = _split_wait(a_send, a_recv, a_thru, _plan_near, xs[0], "allgather_w_in0_wait")
    gathered0 = with_own(_forward_sibling(landed, "allgather_w_in0_forward"), sent)
    blocks1, gathered0 = lax.optimization_barrier((blocks(1), gathered0))
    ag_send, ag_recv, ag_thru, ag_token = _split_start(blocks1, landing(blocks1), _plan_all, 7,
                                                       "allgather_weights1_start")
    weights = [None, None]
    bsb = jnp.broadcast_to(b_s[:, :, :, None], (DEPTH, 4, BLK, BLK))
    bias = _band_bias()

    saved = []
    for l in range(DEPTH):
        if l == 1:
            sent, landed = _split_wait(ag_send, ag_recv, ag_thru, _plan_all, xs[1], "allgather_weights1_wait")
            weights[1] = full_weights(with_own(landed, sent))
        w_t = weights[l]["w_t"] if l else gathered0[0].reshape(IN_COLS, D)
        last = l == DEPTH - 1
        b_l = b_in[l].reshape(1, -1) + (ag_token[0, 0] if l == 0 else 0.0)
        hm, hr = _inproj(xs[l], w_t, b_l, f"inproj{l}")
        ya, yb, prob, psink = _mixer_fwd(hm, sinks[l], bias, vn_g[l].reshape(1, -1), vn_b[l].reshape(1, -1),
                                         w_s[l], bsb[l], nblk_seq, f"mixer_fwd{l}")
        if l == 0:
            sent, landed = _split_wait(b_send, b_recv, b_thru, _plan_near, ya, "allgather_rest0_wait")
            weights[0] = full_weights(gathered0 + with_own(_forward_sibling(landed, "allgather_rest0_forward"), sent))
        wl = weights[l]
        outs = _tail_fwd(xs[l], ya, yb, hr, wl["pa"], wl["pb"], wl["wo"], b_out[l].reshape(1, D),
                         ln_g[l].reshape(1, D), ln_b[l].reshape(1, D), f"tail_fwd{l}", last)
        saved.append((hm, hr, ya, yb, prob, psink) + tuple(outs[:4]))
        if not last:
            xs.append(outs[4])

    small = {n: [None] * DEPTH for n in _SMALL}
    names = ("w_in", "p_a", "p_b", "w_out")
    owner_axis = {"w_in": 0, "p_a": 1, "p_b": 1, "w_out": 0}
    token = jnp.zeros((8, 128), F32)
    dx = tgt
    split = [None] * DEPTH
    for l in reversed(range(DEPTH)):
        hm, hr, ya, yb, prob, psink, pa, pb, merged, z = saved[l]
        wl = weights[l]
        dz, dpa, dpb, dhr, dya, dyb, acc, gbr = _tail_bwd(
            dx, z, pa, pb, hr, wl["wo"], wl["pa"], wl["pb"], ln_g[l].reshape(1, D) + token[0, 0],
            ln_b[l].reshape(1, D), f"tail_bwd{l}", l == DEPTH - 1)
        if l == DEPTH - 1:
            sq_err = acc[3:4, 0:128]
        dhm, gbm, gsk, gvn, gws, gbs = _mixer_bwd(
            hm, dya, dyb, prob, psink, vn_g[l].reshape(1, -1), vn_b[l].reshape(1, -1), w_s[l], bsb[l],
            f"mixer_bwd{l}")
        grads = {"w_in": _wgrad(dhr, xs[l], R_W // 2, f"wgrad_in_route{l}",
                                under=_wgrad(dhm, xs[l], MAIN_W // 2, f"wgrad_in_main{l}", rows=IN_COLS)),
                 "p_a": _wgrad(ya, dpa, Q_W, f"wgrad_pa{l}"), "p_b": _wgrad(yb, dpb, SGU_W, f"wgrad_pb{l}"),
                 "w_out": _wgrad(merged, dz, D, f"wgrad_out{l}")}
        small["b_in"][l] = jnp.concatenate([gbm[0], gbr[0]])
        small["sinks"][l] = gsk[:, 0]
        small["vn_g"][l], small["vn_b"][l] = gvn[0], gvn[1]
        small["w_s"][l], small["b_s"][l] = gws, gbs[:, :, 0]
        small["ln_g"][l], small["ln_b"][l], small["b_out"][l] = acc[0], acc[1], acc[2]
        parts = [_owner_blocks(grads[n], owner_axis[n]) for n in names]
        if l == 0:
            packed = _pack_small({n: jnp.stack(v) for n, v in small.items()}, _SMALL)
            parts.append(jnp.broadcast_to(packed[None, None], (1, 2) + packed.shape))
        if l == 0:
            from_sib = _swap_sibling(parts, f"rs_sibling{l}")
        else:
            halves = [p.reshape((N_DEV,) + p.shape[2:]) for p in parts]
            sib = _split_start(halves, [lax.empty((4,) + p.shape[2:], p.dtype) for p in parts], _plan_sibling, 4,
                               f"rs_sibling{l}_start")
            dx = _dx_inproj(dz, dhm, dhr, wl["w_t"], sib[3], f"dx_inproj{l}")
            halves, from_sib = _split_wait(sib[0], sib[1], sib[2], _plan_sibling, dx, f"rs_sibling{l}_wait")
            parts = [h.reshape(p.shape) for h, p in zip(halves, parts)]
        pair = list(_pair_sums(parts, from_sib, f"pair_sums{l}"))
        if l == 0:
            pair[4] = jnp.broadcast_to(pair[4], (4,) + packed.shape)
        lands = [jnp.zeros(p.shape, p.dtype) for p in pair]
        split[l] = _split_start(pair, lands, _plan_chips, 3, f"rs_chips{l}_start")
        token = split[l][3]
        if l == 0:
            grad_x, acc_in = _dx_inproj(dz, dhm, dhr, wl["w_t"], token, f"dx_inproj{l}",
                                        ln_in=(x2, ln_in_g.reshape(1, D)))
    all_in, all_sq = _allgather8([acc_in, jnp.broadcast_to(sq_err, (8, 128))], "allgather_ln_in")
    loss = jnp.sum(all_sq[:, 0, 0]) * (0.5 / D)

    given = {"w_in": (w_in_t, turned(m_w_in), turned(v_w_in)), "p_a": (p_a, m_p_a, v_p_a),
             "p_b": (p_b, m_p_b, v_p_b), "w_out": (w_out, m_w_out, v_w_out)}
    waited = [_split_wait(split[l][0], split[l][1], split[l][2], _plan_chips, all_in, f"rs_chips{l}_wait")
              for l in range(DEPTH)]
    res = {}
    for a, n in enumerate(names):
        rows, lanes = waited[0][1][a].shape[1:]
        outs = _adamw([waited[l][1][a] for l in range(DEPTH)], *[v.reshape(DEPTH * rows, lanes) for v in given[n]],
                      f"adamw_{n}", own=[waited[l][0][a] for l in range(DEPTH)])
        res[n] = [o.reshape(given[n][0].shape) for o in outs]
    res["w_in"] = [turned(o) for o in res["w_in"]]

    w_small = dict(ln_in_g=ln_in_g, ln_in_b=ln_in_b, b_in=b_in, sinks=sinks, vn_g=vn_g, vn_b=vn_b, w_s=w_s, b_s=b_s,
                   b_out=b_out, ln_g=ln_g, ln_b=ln_b)
    m_small = dict(ln_in_g=m_ln_in_g, ln_in_b=m_ln_in_b, b_in=m_b_in, sinks=m_sinks, vn_g=m_vn_g, vn_b=m_vn_b,
                   w_s=m_w_s, b_s=m_b_s, b_out=m_b_out, ln_g=m_ln_g, ln_b=m_ln_b)
    v_small = dict(ln_in_g=v_ln_in_g, ln_in_b=v_ln_in_b, b_in=v_b_in, sinks=v_sinks, vn_g=v_vn_g, vn_b=v_vn_b,
                   w_s=v_w_s, b_s=v_b_s, b_out=v_b_out, ln_g=v_ln_g, ln_b=v_ln_b)
    outs = _adamw([waited[0][1][4]], *[_pack_small(d, _SMALL) for d in (w_small, m_small, v_small)], "adamw_small",
                  own=[waited[0][0][4]])
    outs_in = _adamw([all_in], *[jnp.pad(jnp.stack([d[n] for n in _SMALL_IN]), ((0, 6), (0, 0)))
                                 for d in (w_small, m_small, v_small)], "adamw_ln_in")
    for k in range(4):
        u = _unpack_small(outs[k], w_small, _SMALL)
        u.update({n: outs_in[k][r] for r, n in enumerate(_SMALL_IN)})
        for n in u:
            res.setdefault(n, [None] * 4)[k] = u[n]

    order = ("ln_in_g", "ln_in_b", "w_in", "b_in", "sinks", "vn_g", "vn_b", "w_s", "b_s", "p_a", "p_b", "w_out",
             "b_out", "ln_g", "ln_b")
    return (loss, grad_x.reshape(x.shape), *[res[n][0] for n in order], *[res[n][1] for n in order],
            *[res[n][2] for n in order], *[res[n][3] for n in order])
```

```python
import jax
import jax.numpy as jnp
from jax import lax
from jax.experimental import pallas as pl
from jax.experimental.pallas import tpu as pltpu

F32 = jnp.float32
BF16 = jnp.bfloat16

D = 1024
BLK = 128
N_KV = 2
Q_W, KV_W, SGU_W = 512, 128, 512
C_Q, C_K, C_V, C_GA, C_UB, C_VB, C_GB = 0, 512, 640, 768, 1280, 1792, 2304
MAIN_W = 2816
R_W = 2048
IN_COLS = MAIN_W + R_W
N_DEV = 8

DEPTH = 2
ALPHA = (2.0 * DEPTH) ** 0.25
LN_EPS = 1e-5
ATTN_SCALE = 0.125
NEG = float(jnp.finfo(jnp.float32).min)

ADAM_LR, ADAM_B1, ADAM_B2, ADAM_EPS, ADAM_WD, ADAM_STEP = 0.001, 0.9, 0.999, 1e-08, 0.01, 10

TM = 512
TM_EW = 512
TM_MM = 512
TM_T = 512
NB = TM // BLK
MESH = pl.DeviceIdType.MESH
VMEM_LIMIT = 56 * 1024 * 1024

_ARB = pltpu.CompilerParams(dimension_semantics=("arbitrary",), vmem_limit_bytes=VMEM_LIMIT)


def _sigmoid(x):
    return 1.0 / (1.0 + jnp.exp(-x))


_GELU_C = 0.7978845608028654
_GELU_A = 0.044715


def _gelu_parts(x):
    x2 = x * x
    t = jnp.tanh(x * (_GELU_C + (_GELU_C * _GELU_A) * x2))
    hx = 0.5 * x
    return hx, t, x2


def _gelu(x):
    hx, t, _ = _gelu_parts(x)
    return hx + hx * t


def _gelu_and_grad(x):
    hx, t, x2 = _gelu_parts(x)
    grad = 0.5 + 0.5 * t + (hx - hx * (t * t)) * (_GELU_C + (3.0 * _GELU_C * _GELU_A) * x2)
    return hx + hx * t, grad


def _ln_stats(x):
    mu = jnp.mean(x, axis=-1, keepdims=True)
    xc = x - mu
    var = jnp.mean(xc * xc, axis=-1, keepdims=True)
    rstd = lax.rsqrt(var + LN_EPS)
    return xc * rstd, rstd


def _ln_bwd(dy_g, xhat, rstd):
    m1 = jnp.mean(dy_g, axis=-1, keepdims=True)
    m2 = jnp.mean(dy_g * xhat, axis=-1, keepdims=True)
    return rstd * (dy_g - m1 - xhat * m2)


def _colsum(x):
    return jnp.sum(x, axis=0, keepdims=True)


def _dot(a, b):
    return jnp.dot(a, b, preferred_element_type=F32)


def _dot_nt(a, b):
    return lax.dot_general(a, b, (((1,), (1,)), ((), ())), preferred_element_type=F32)


def _dot_tn(a, b):
    return lax.dot_general(a, b, (((0,), (0,)), ((), ())), preferred_element_type=F32)


def _head_place(hk, g):
    j = 4 * hk + g
    return j, j // 2, j % 2


def _head_rows(x, hk):
    d = lax.broadcasted_iota(jnp.int32, x.shape, 0)
    return jnp.where((d >= 64 * hk) & (d < 64 * hk + 64), x, 0.0).astype(BF16)


def _head_lanes(x, hk):
    d = lax.broadcasted_iota(jnp.int32, x.shape, 1)
    return jnp.where((d >= 64 * hk) & (d < 64 * hk + 64), x, 0.0)


def _band_bias():
    kpos = lax.broadcasted_iota(jnp.int32, (2 * BLK, 4 * BLK), 0)
    row = lax.broadcasted_iota(jnp.int32, (2 * BLK, 4 * BLK), 1) & (BLK - 1)
    band = (kpos > row) & (kpos <= row + BLK)
    return jnp.stack([jnp.where(band, 0.0, NEG), jnp.where(band & (kpos >= BLK), 0.0, NEG)]).astype(F32)


def _stack_q(q, hk):
    parts = []
    for g in range(4):
        _, p, pos = _head_place(hk, g)
        qp = q[:, BLK * p:BLK * (p + 1)] * ATTN_SCALE
        if pos != hk:
            qp = pltpu.roll(qp, 64, 1)
        parts.append(qp.astype(BF16))
    return jnp.concatenate(parts, axis=0)


def _attn_probs(q4, kh, hk, sinks_ref, bias):
    s_t = _dot_nt(kh, q4) + bias
    sink_row = jnp.concatenate(
        [jnp.full((1, BLK), sinks_ref[4 * hk + g], F32) for g in range(4)], axis=1)
    m = jnp.maximum(jnp.max(s_t, axis=0, keepdims=True), sink_row)
    p_un = jnp.exp(s_t - m)
    e_sink = jnp.exp(sink_row - m)
    inv = 1.0 / (jnp.sum(p_un, axis=0, keepdims=True) + e_sink)
    return (p_un * inv).astype(BF16), e_sink * inv


def _unstack_heads(x4, hk, pairs):
    for g in range(4):
        _, p, pos = _head_place(hk, g)
        xg = x4[BLK * g:BLK * (g + 1)]
        if pos != hk:
            xg = pltpu.roll(xg, 64, 1)
        pairs[p] = xg if pairs[p] is None else pairs[p] + xg
    return pairs


def _attn_fwd(q, kband, vband, sinks_ref, bias, save):
    pairs = [None] * 4
    vband_t = vband.T
    for hk in range(N_KV):
        prob_t, p_sink = _attn_probs(_stack_q(q, hk), _head_lanes(kband, hk).astype(BF16), hk, sinks_ref, bias)
        save(hk, prob_t, p_sink)
        o_t = _dot(_head_rows(vband_t, hk), prob_t)
        pairs = _unstack_heads(o_t.T, hk, pairs)
    return jnp.concatenate(pairs, axis=1)


def _tril_mask():
    r = lax.broadcasted_iota(jnp.int32, (BLK, BLK), 0)
    c = lax.broadcasted_iota(jnp.int32, (BLK, BLK), 1)
    return c <= r


def _sgu_fwd(u, v, vn_g, vn_b, wt, bsb_ref):
    vhat, rstd = _ln_stats(v)
    vn = vhat * vn_g + vn_b
    mixed = jnp.concatenate(
        [_dot(wt[g], vn[:, BLK * g:BLK * (g + 1)].astype(BF16)) + bsb_ref[g] for g in range(4)], axis=1)
    return vhat, rstd, vn, mixed


def _cols(ref, rows, col, width):
    return ref[rows, col:col + width].astype(F32)


def _band(hm_ref, hprev_ref, s, col):
    r0 = s * BLK
    cur = hm_ref[r0:r0 + BLK, col:col + KV_W]
    if s == 0:
        off = 0 if col == C_K else KV_W
        prev = hprev_ref[:, off:off + KV_W]
    else:
        prev = hm_ref[r0 - BLK:r0, col:col + KV_W]
    return jnp.concatenate([prev, cur], axis=0).astype(F32)


def _h_main_specs(nt, rev):
    def tile(g):
        return nt - 1 - g if rev else g

    return [pl.BlockSpec((TM, MAIN_W), lambda g: (tile(g), 0)),
            pl.BlockSpec((BLK, 2 * KV_W), lambda g: (jnp.maximum(tile(g) * NB - 1, 0), 2))]


_CONST2 = lambda g: (0, 0)
_CONST3 = lambda g: (0, 0, 0)


def _ln_fwd(x, g, b, name):
    t = x.shape[0]

    def body(x_ref, g_ref, b_ref, o_ref):
        xhat, _ = _ln_stats(x_ref[...])
        o_ref[...] = xhat * g_ref[...] + b_ref[...]

    return pl.pallas_call(
        body, name=name, grid=(t // TM_EW,),
        in_specs=[pl.BlockSpec((TM_EW, D), lambda i: (i, 0)), pl.BlockSpec((1, D), _CONST2),
                  pl.BlockSpec((1, D), _CONST2)],
        out_specs=pl.BlockSpec((TM_EW, D), lambda i: (i, 0)),
        out_shape=jax.ShapeDtypeStruct((t, D), F32), compiler_params=_ARB,
    )(x, g.reshape(1, D), b.reshape(1, D))


def _inproj(x, w_t, b, name):
    t = x.shape[0]

    def body(x_ref, wt_ref, b_ref, hm_ref, hr_ref):
        xb = x_ref[...].astype(BF16)
        hm_ref[...] = (_dot_nt(xb, wt_ref[0:MAIN_W, :]) + b_ref[:, 0:MAIN_W]).astype(BF16)
        hr_ref[...] = (_dot_nt(xb, wt_ref[MAIN_W:IN_COLS, :]) + b_ref[:, MAIN_W:IN_COLS]).astype(BF16)

    return pl.pallas_call(
        body, name=name, grid=(t // TM_MM,),
        in_specs=[pl.BlockSpec((TM_MM, D), lambda i: (i, 0)),
                  pl.BlockSpec((IN_COLS, D), _CONST2), pl.BlockSpec((1, IN_COLS), _CONST2)],
        out_specs=[pl.BlockSpec((TM_MM, MAIN_W), lambda i: (i, 0)), pl.BlockSpec((TM_MM, R_W), lambda i: (i, 0))],
        out_shape=[jax.ShapeDtypeStruct((t, MAIN_W), BF16), jax.ShapeDtypeStruct((t, R_W), BF16)],
        compiler_params=_ARB,
    )(x, w_t, b)


def _mixer_fwd(hm, sinks, bias, vn_g, vn_b, w_s, bsb, nblk_seq, name):
    t = hm.shape[0]
    nt = t // TM

    def body(sinks_ref, hm_ref, hprev_ref, bias_ref, vng_ref, vnb_ref, ws_ref, bsb_ref,
             ya_ref, yb_ref, prob_ref, psink_ref):
        i = pl.program_id(0)
        tril = _tril_mask()
        wt = [jnp.where(tril, ws_ref[g], 0.0).astype(BF16) for g in range(4)]
        for s in range(NB):
            r0 = s * BLK
            rows = slice(r0, r0 + BLK)
            bias = bias_ref[jnp.where((i * NB + s) % nblk_seq == 0, 1, 0)]

            def save(hk, prob_t, p_sink, s=s):
                prob_ref[N_KV * s + hk] = prob_t
                psink_ref[N_KV * s + hk] = jnp.broadcast_to(p_sink, (8, 4 * BLK))

            attn = _attn_fwd(_cols(hm_ref, rows, C_Q, Q_W), _band(hm_ref, hprev_ref, s, C_K),
                             _band(hm_ref, hprev_ref, s, C_V), sinks_ref, bias, save)
            g_a = _cols(hm_ref, rows, C_GA, Q_W)
            ya_ref[rows, :] = (attn * (g_a * _sigmoid(g_a))).astype(BF16)
            u = _gelu(_cols(hm_ref, rows, C_UB, SGU_W))
            mixed = _sgu_fwd(u, _gelu(_cols(hm_ref, rows, C_VB, SGU_W)), vng_ref[...], vnb_ref[...], wt, bsb_ref)[-1]
            g_b = _cols(hm_ref, rows, C_GB, SGU_W)
            yb_ref[rows, :] = (u * mixed * (g_b * _sigmoid(g_b))).astype(BF16)

    ngrp = N_KV * NB
    return pl.pallas_call(
        body, name=name, grid=(nt,),
        in_specs=[pl.BlockSpec(memory_space=pltpu.SMEM)] + _h_main_specs(nt, False) + [
            pl.BlockSpec((2, 2 * BLK, 4 * BLK), _CONST3),
            pl.BlockSpec((1, SGU_W), _CONST2), pl.BlockSpec((1, SGU_W), _CONST2),
            pl.BlockSpec((4, BLK, BLK), _CONST3), pl.BlockSpec((4, BLK, BLK), _CONST3)],
        out_specs=[pl.BlockSpec((TM, Q_W), lambda i: (i, 0)), pl.BlockSpec((TM, SGU_W), lambda i: (i, 0)),
                   pl.BlockSpec((ngrp, 2 * BLK, 4 * BLK), lambda i: (i, 0, 0)),
                   pl.BlockSpec((ngrp, 8, 4 * BLK), lambda i: (i, 0, 0))],
        out_shape=[jax.ShapeDtypeStruct((t, Q_W), BF16), jax.ShapeDtypeStruct((t, SGU_W), BF16),
                   jax.ShapeDtypeStruct((nt * ngrp, 2 * BLK, 4 * BLK), BF16),
                   jax.ShapeDtypeStruct((nt * ngrp, 8, 4 * BLK), F32)],
        compiler_params=_ARB,
    )(sinks, hm, hm, bias, vn_g, vn_b, w_s, bsb)


def _tail_fwd(x, ya, yb, hr, pa_w, pb_w, wo, b_out, ln_g, ln_b, name, last):
    t = x.shape[0]

    def body(x_ref, ya_ref, yb_ref, hr_ref, paw_ref, pbw_ref, wo_ref, bo_ref, g_ref, b_ref,
             pa_ref, pb_ref, mg_ref, z_ref, *xn_ref):
        pa = _dot(ya_ref[...], paw_ref[...])
        pb = _dot(yb_ref[...], pbw_ref[...])
        pa_ref[...] = pa.astype(BF16)
        pb_ref[...] = pb.astype(BF16)
        everything = slice(None)
        merged = _sigmoid(_cols(hr_ref, everything, 0, D)) * pa + _sigmoid(_cols(hr_ref, everything, D, D)) * pb
        mb = merged.astype(BF16)
        mg_ref[...] = mb
        z = ALPHA * x_ref[...] + (_dot(mb, wo_ref[...]) + bo_ref[...])
        z_ref[...] = z
        if not last:
            zhat, _ = _ln_stats(z)
            xn_ref[0][...] = zhat * g_ref[...] + b_ref[...]

    row = lambda w: pl.BlockSpec((TM_T, w), lambda i: (i, 0))
    vec = pl.BlockSpec((1, D), _CONST2)
    n_f32 = 1 if last else 2
    return pl.pallas_call(
        body, name=name, grid=(t // TM_T,),
        in_specs=[row(D), row(Q_W), row(SGU_W), row(R_W),
                  pl.BlockSpec((Q_W, D), _CONST2), pl.BlockSpec((SGU_W, D), _CONST2), pl.BlockSpec((D, D), _CONST2),
                  vec, vec, vec],
        out_specs=[row(D)] * (3 + n_f32),
        out_shape=[jax.ShapeDtypeStruct((t, D), BF16)] * 3 + [jax.ShapeDtypeStruct((t, D), F32)] * n_f32,
        compiler_params=_ARB,
    )(x, ya, yb, hr, pa_w, pb_w, wo, b_out, ln_g, ln_b)


def _tail_bwd(dxn, z, pa, pb, hr, wo, pa_w, pb_w, ln_g, ln_b, name, from_loss):
    t = dxn.shape[0]

    def body(dxn_ref, z_ref, pa_ref, pb_ref, hr_ref, wo_ref, paw_ref, pbw_ref, g_ref, b_ref,
             dz_ref, dpa_ref, dpb_ref, dhr_ref, dya_ref, dyb_ref, acc_ref, gbr_ref):
        @pl.when(pl.program_id(0) == 0)
        def _():
            acc_ref[...] = jnp.zeros_like(acc_ref)
            gbr_ref[...] = jnp.zeros_like(gbr_ref)

        zhat, rstd = _ln_stats(z_ref[...])
        if from_loss:
            err = zhat * g_ref[...] + b_ref[...] - dxn_ref[...]
            dxn_v = err * (1.0 / D)
            sq = jnp.sum(jnp.sum(err * err, axis=1, keepdims=True), axis=0, keepdims=True)
            acc_ref[3:4, :] += jnp.broadcast_to(sq, (1, D))
        else:
            dxn_v = dxn_ref[...]
        dz = _ln_bwd(dxn_v * g_ref[...], zhat, rstd)
        dz_ref[...] = dz
        acc_ref[0:1, :] += _colsum(dxn_v * zhat)
        acc_ref[1:2, :] += _colsum(dxn_v)
        acc_ref[2:3, :] += _colsum(dz)
        dmerged = _dot_nt(dz.astype(BF16), wo_ref[...])
        everything = slice(None)
        sa = _sigmoid(_cols(hr_ref, everything, 0, D))
        sb = _sigmoid(_cols(hr_ref, everything, D, D))
        dpa = (dmerged * sa).astype(BF16)
        dpb = (dmerged * sb).astype(BF16)
        dpa_ref[...] = dpa
        dpb_ref[...] = dpb
        dra = dmerged * pa_ref[...].astype(F32) * (sa * (1.0 - sa))
        drb = dmerged * pb_ref[...].astype(F32) * (sb * (1.0 - sb))
        dhr_ref[:, 0:D] = dra.astype(BF16)
        dhr_ref[:, D:2 * D] = drb.astype(BF16)
        gbr_ref[0:1, 0:D] += _colsum(dra)
        gbr_ref[0:1, D:2 * D] += _colsum(drb)
        dya_ref[...] = _dot_nt(dpa, paw_ref[...]).astype(BF16)
        dyb_ref[...] = _dot_nt(dpb, pbw_ref[...]).astype(BF16)

    row = lambda w: pl.BlockSpec((TM_T, w), lambda i: (i, 0))
    vec = pl.BlockSpec((1, D), _CONST2)
    return pl.pallas_call(
        body, name=name, grid=(t // TM_T,),
        in_specs=[row(D), row(D), row(D), row(D), row(R_W),
                  pl.BlockSpec((D, D), _CONST2), pl.BlockSpec((Q_W, D), _CONST2), pl.BlockSpec((SGU_W, D), _CONST2),
                  vec, vec],
        out_specs=[row(D), row(D), row(D), row(R_W), row(Q_W), row(SGU_W), pl.BlockSpec((8, D), _CONST2),
                   pl.BlockSpec((8, R_W), _CONST2)],
        out_shape=[jax.ShapeDtypeStruct((t, D), F32), jax.ShapeDtypeStruct((t, D), BF16),
                   jax.ShapeDtypeStruct((t, D), BF16), jax.ShapeDtypeStruct((t, R_W), BF16),
                   jax.ShapeDtypeStruct((t, Q_W), BF16), jax.ShapeDtypeStruct((t, SGU_W), BF16),
                   jax.ShapeDtypeStruct((8, D), F32), jax.ShapeDtypeStruct((8, R_W), F32)],
        compiler_params=_ARB,
    )(dxn, z, pa, pb, hr, wo, pa_w, pb_w, ln_g, ln_b)


def _mixer_bwd(hm, dya, dyb, prob, psink, vn_g, vn_b, w_s, bsb, name):
    t = hm.shape[0]
    nt = t // TM
    ngrp = N_KV * NB

    def body(hm_ref, hprev_ref, prob_ref, psink_ref, dya_ref, dyb_ref, vng_ref, vnb_ref, ws_ref, bsb_ref,
             dhm_ref, gbm_ref, gsk_ref, gvn_ref, gws_ref, gbs_ref, dk_carry, dv_carry):
        gi = pl.program_id(0)

        @pl.when(gi == 0)
        def _():
            for r in (gbm_ref, gsk_ref, gvn_ref, gws_ref, gbs_ref, dk_carry, dv_carry):
                r[...] = jnp.zeros_like(r)

        tril = _tril_mask()
        wt = [jnp.where(tril, ws_ref[g], 0.0).astype(BF16) for g in range(4)]
        vng = vng_ref[...]
        ones8 = jnp.ones((8, BLK), BF16)

        def put(rows, col, val):
            dhm_ref[rows, col:col + val.shape[1]] = val.astype(BF16)

        for s in reversed(range(NB)):
            r0 = s * BLK
            rows = slice(r0, r0 + BLK)
            q = _cols(hm_ref, rows, C_Q, Q_W)
            kband = _band(hm_ref, hprev_ref, s, C_K)
            vband = _band(hm_ref, hprev_ref, s, C_V)
            g_a = _cols(hm_ref, rows, C_GA, Q_W)
            sg = _sigmoid(g_a)
            dya_v = _cols(dya_ref, rows, 0, Q_W)
            d_o = dya_v * (g_a * sg)
            o_pairs, dq_pairs = [None] * 4, [None] * 4
            dkband = jnp.zeros((2 * BLK, KV_W), F32)
            dvband = jnp.zeros((2 * BLK, KV_W), F32)
            kband_t, vband_t = kband.T, vband.T
            for hk in range(N_KV):
                q4 = _stack_q(q, hk)
                prob_b = prob_ref[N_KV * s + hk]
                p_sink = psink_ref[N_KV * s + hk][0:1, :]
                o_t = _dot(_head_rows(vband_t, hk), prob_b)
                o_pairs = _unstack_heads(o_t.T, hk, o_pairs)
                parts = []
                for g in range(4):
                    _, p, pos = _head_place(hk, g)
                    dp = d_o[:, BLK * p:BLK * (p + 1)]
                    parts.append(pltpu.roll(dp, 64, 1) if pos != hk else dp)
                do4 = _head_lanes(jnp.concatenate(parts, axis=0), hk)
                do4b = do4.astype(BF16)
                delta = _colsum(do4.T * o_t)
                vh = _head_lanes(vband, hk).astype(BF16)
                ds_t = prob_b.astype(F32) * (_dot_nt(vh, do4b) - delta)
                dsb = ds_t.astype(BF16)
                dq4_t = _dot(_head_rows(kband_t, hk), dsb)
                dq_pairs = _unstack_heads(dq4_t.T * ATTN_SCALE, hk, dq_pairs)
                dkband = dkband + _head_lanes(_dot(dsb, q4), hk)
                dvband = dvband + _dot(prob_b, do4b)
                dsk = p_sink * delta
                for g in range(4):
                    j = 4 * hk + g
                    tot = jnp.sum(dsk[:, BLK * g:BLK * (g + 1)], axis=1, keepdims=True)
                    gsk_ref[j:j + 1, :] += jnp.broadcast_to(-tot, (1, 128))
            attn = jnp.concatenate(o_pairs, axis=1)
            put(rows, C_Q, jnp.concatenate(dq_pairs, axis=1))
            put(rows, C_K, dkband[BLK:2 * BLK] + dk_carry[...])
            put(rows, C_V, dvband[BLK:2 * BLK] + dv_carry[...])
            dk_carry[...] = dkband[0:BLK]
            dv_carry[...] = dvband[0:BLK]
            put(rows, C_GA, dya_v * attn * (sg * (1.0 + g_a * (1.0 - sg))))
            u, du_du_b = _gelu_and_grad(_cols(hm_ref, rows, C_UB, SGU_W))
            v, dv_dv_b = _gelu_and_grad(_cols(hm_ref, rows, C_VB, SGU_W))
            g_b = _cols(hm_ref, rows, C_GB, SGU_W)
            vhat, rstd, vn, mixed = _sgu_fwd(u, v, vng, vnb_ref[...], wt, bsb_ref)
            sgb = _sigmoid(g_b)
            silu_b = g_b * sgb
            dyb_v = _cols(dyb_ref, rows, 0, SGU_W)
            du = dyb_v * mixed * silu_b
            dmixed = dyb_v * u * silu_b
            put(rows, C_GB, dyb_v * u * mixed * (sgb * (1.0 + g_b * (1.0 - sgb))))
            dvn_parts = []
            for g in range(4):
                cols = slice(BLK * g, BLK * (g + 1))
                dmg = dmixed[:, cols]
                dmgb = dmg.astype(BF16)
                dvn_parts.append(_dot_tn(wt[g], dmgb))
                gws_ref[g] += jnp.where(tril, _dot_nt(dmgb, vn[:, cols].astype(BF16)), 0.0)
                gbs_ref[g] += dmg
            dvn = jnp.concatenate(dvn_parts, axis=1)
            gvn_ref[0:1, :] += _colsum(dvn * vhat)
            gvn_ref[1:2, :] += _colsum(dvn)
            dv = _ln_bwd(dvn * vng, vhat, rstd)
            put(rows, C_UB, du * du_du_b)
            put(rows, C_VB, dv * dv_dv_b)
            gbm_ref[...] += _dot(ones8, dhm_ref[rows, :])

        @pl.when(gi == nt - 1)
        def _():
            for g in range(4):
                gbs_ref[g] = jnp.broadcast_to(jnp.sum(gbs_ref[g], axis=1, keepdims=True), (BLK, BLK))

    row = lambda w: pl.BlockSpec((TM, w), lambda g: (nt - 1 - g, 0))
    return pl.pallas_call(
        body, name=name, grid=(nt,),
        in_specs=_h_main_specs(nt, True) + [
            pl.BlockSpec((ngrp, 2 * BLK, 4 * BLK), lambda g: (nt - 1 - g, 0, 0)),
            pl.BlockSpec((ngrp, 8, 4 * BLK), lambda g: (nt - 1 - g, 0, 0)),
            row(Q_W), row(SGU_W),
            pl.BlockSpec((1, SGU_W), _CONST2), pl.BlockSpec((1, SGU_W), _CONST2),
            pl.BlockSpec((4, BLK, BLK), _CONST3), pl.BlockSpec((4, BLK, BLK), _CONST3)],
        out_specs=[row(MAIN_W), pl.BlockSpec((8, MAIN_W), _CONST2), pl.BlockSpec((8, 128), _CONST2),
                   pl.BlockSpec((8, SGU_W), _CONST2), pl.BlockSpec((4, BLK, BLK), _CONST3),
                   pl.BlockSpec((4, BLK, BLK), _CONST3)],
        out_shape=[jax.ShapeDtypeStruct((t, MAIN_W), BF16), jax.ShapeDtypeStruct((8, MAIN_W), F32),
                   jax.ShapeDtypeStruct((8, 128), F32), jax.ShapeDtypeStruct((8, SGU_W), F32),
                   jax.ShapeDtypeStruct((4, BLK, BLK), F32), jax.ShapeDtypeStruct((4, BLK, BLK), F32)],
        scratch_shapes=[pltpu.VMEM((BLK, KV_W), F32), pltpu.VMEM((BLK, KV_W), F32)],
        compiler_params=_ARB,
    )(hm, hm, prob, psink, dya, dyb, vn_g, vn_b, w_s, bsb)


def _dx_inproj(dz, dhm, dhr, w_t, after, name, ln_in=None):
    t = dz.shape[0]

    def body(dz_ref, dhm_ref, dhr_ref, wt_ref, after_ref, *rest):
        dx = (ALPHA * dz_ref[...] + after_ref[0:1, 0:1] + _dot(dhm_ref[...], wt_ref[0:MAIN_W, :])
              + _dot(dhr_ref[...], wt_ref[MAIN_W:IN_COLS, :]))
        if ln_in is None:
            rest[0][...] = dx
            return
        x_ref, g_ref, gx_ref, acc_ref = rest

        @pl.when(pl.program_id(0) == 0)
        def _():
            acc_ref[...] = jnp.zeros_like(acc_ref)

        xhat, rstd = _ln_stats(x_ref[...])
        gx_ref[...] = _ln_bwd(dx * g_ref[...], xhat, rstd)
        acc_ref[0:1, :] += _colsum(dx * xhat)
        acc_ref[1:2, :] += _colsum(dx)

    row = lambda w: pl.BlockSpec((TM_MM, w), lambda i: (i, 0))
    in_specs = [row(D), row(MAIN_W), row(R_W), pl.BlockSpec((IN_COLS, D), _CONST2), pl.BlockSpec((8, 128), _CONST2)]
    if ln_in is None:
        return pl.pallas_call(
            body, name=name, grid=(t // TM_MM,), in_specs=in_specs,
            out_specs=row(D), out_shape=jax.ShapeDtypeStruct((t, D), F32), compiler_params=_ARB,
        )(dz, dhm, dhr, w_t, after)
    return pl.pallas_call(
        body, name=name, grid=(t // TM_MM,), in_specs=in_specs + [row(D), pl.BlockSpec((1, D), _CONST2)],
        out_specs=[row(D), pl.BlockSpec((8, D), _CONST2)],
        out_shape=[jax.ShapeDtypeStruct((t, D), F32), jax.ShapeDtypeStruct((8, D), F32)], compiler_params=_ARB,
    )(dz, dhm, dhr, w_t, after, *ln_in)


def _wgrad(a, b, tm, name, rows=None, under=None):
    t, m = a.shape
    n = b.shape[1]
    tk = min(t, 2048)
    nk = t // tk

    def body(a_ref, b_ref, *rest):
        o_ref, acc_ref = rest[-2:]
        k = pl.program_id(1)

        @pl.when(k == 0)
        def _():
            acc_ref[...] = jnp.zeros_like(acc_ref)

        acc_ref[...] += _dot_tn(a_ref[...].astype(BF16), b_ref[...].astype(BF16))

        @pl.when(k == nk - 1)
        def _():
            o_ref[...] = acc_ref[...].astype(BF16)

    in_specs = [pl.BlockSpec((tk, tm), lambda j, k: (k, j)), pl.BlockSpec((tk, n), lambda j, k: (k, 0))]
    if under is None:
        out_rows, out_spec, operands, aliases = rows or m, pl.BlockSpec((tm, n), lambda j, k: (j, 0)), (a, b), {}
    else:
        out_rows = under.shape[0]
        first = out_rows - m
        assert first % 128 == 0 and tm % 128 == 0
        out_spec = pl.BlockSpec((pl.Element(tm), pl.Element(n)),
                                lambda j, k: (pl.multiple_of(first + j * tm, 128), 0))
        in_specs, operands, aliases = in_specs + [_ANY], (a, b, under), {2: 0}
    return pl.pallas_call(
        body, name=name, grid=(m // tm, nk), in_specs=in_specs, out_specs=out_spec,
        out_shape=jax.ShapeDtypeStruct((out_rows, n), BF16), input_output_aliases=aliases,
        scratch_shapes=[pltpu.VMEM((tm, n), F32)],
        compiler_params=pltpu.CompilerParams(dimension_semantics=("arbitrary", "arbitrary"), vmem_limit_bytes=VMEM_LIMIT),
    )(*operands)


_ANY = pl.BlockSpec(memory_space=pl.ANY)


def _place():
    return lax.axis_index("x"), lax.axis_index("y"), lax.axis_index("c")


def _allgather8(xs, name):
    n = len(xs)

    def body(*refs):
        x_refs, o_refs = refs[:n], refs[n:2 * n]
        send_sems, recv_sems, local_sems = refs[2 * n:]
        x, y, c = _place()
        me, sib = (x, y, c), (x, y, 1 - c)
        chips = [(1 - x, y), (x, 1 - y), (1 - x, 1 - y)]

        def copy(a, k, block, to, src=None):
            dst = o_refs[a].at[4 * block[0] + 2 * block[1] + block[2]]
            return pltpu.make_async_remote_copy(
                src_ref=dst if src is None else src, dst_ref=dst, send_sem=send_sems.at[7 * a + k],
                recv_sem=recv_sems.at[7 * a + k], device_id=to, device_id_type=MESH)

        mine = [pltpu.make_async_copy(x_refs[a], o_refs[a].at[4 * x + 2 * y + c], local_sems.at[a]) for a in range(n)]
        for cp in mine:
            cp.start()
        sent = []
        for a in range(n):
            sent.append(copy(a, 0, me, sib, src=x_refs[a]))
            sent += [copy(a, 1 + j, me, (*chip, c), src=x_refs[a]) for j, chip in enumerate(chips)]
        for cp in sent:
            cp.start()
        for j, chip in enumerate(chips):
            for a in range(n):
                copy(a, 1 + j, (*chip, c), me).wait_recv()
                fwd = copy(a, 4 + j, (*chip, c), sib)
                fwd.start()
                sent.append(fwd)
        for a in range(n):
            copy(a, 0, sib, me).wait_recv()
            for j, chip in enumerate(chips):
                copy(a, 4 + j, (*chip, 1 - c), me).wait_recv()
        for cp in sent:
            cp.wait_send()
        for cp in mine:
            cp.wait()

    return pl.pallas_call(
        body, name=name, in_specs=[_ANY] * n, out_specs=[_ANY] * n,
        out_shape=[jax.ShapeDtypeStruct((N_DEV,) + v.shape, v.dtype) for v in xs],
        scratch_shapes=[pltpu.SemaphoreType.DMA((7 * n,)), pltpu.SemaphoreType.DMA((7 * n,)),
                        pltpu.SemaphoreType.DMA((n,))],
    )(*xs)


def _forward_sibling(lands, name):
    n = len(lands)

    def body(*refs):
        l_refs = refs[n:2 * n]
        send_sems, recv_sems = refs[2 * n:]
        x, y, c = _place()
        chips = [(1 - x, y), (x, 1 - y), (1 - x, 1 - y)]

        def copy(a, j, core):
            rows = l_refs[a].at[4 * chips[j][0] + 2 * chips[j][1] + core]
            return pltpu.make_async_remote_copy(
                src_ref=rows, dst_ref=rows, send_sem=send_sems.at[3 * a + j], recv_sem=recv_sems.at[3 * a + j],
                device_id=(x, y, 1 - c), device_id_type=MESH)

        for a in range(n):
            for j in range(3):
                copy(a, j, c).start()
        for a in range(n):
            for j in range(3):
                copy(a, j, 1 - c).wait_recv()
                copy(a, j, c).wait_send()

    return pl.pallas_call(
        body, name=name, in_specs=[_ANY] * n, out_specs=[_ANY] * n,
        out_shape=[jax.ShapeDtypeStruct(v.shape, v.dtype) for v in lands],
        input_output_aliases={a: a for a in range(n)},
        scratch_shapes=[pltpu.SemaphoreType.DMA((3 * n,)), pltpu.SemaphoreType.DMA((3 * n,))],
    )(*lands)


def _swap_sibling(gs, name):
    n = len(gs)
    first = [0]
    for v in gs:
        first.append(first[-1] + v.shape[0])

    def body(*refs):
        g_refs, r_refs = refs[:n], refs[n:2 * n]
        send_sems, recv_sems = refs[2 * n:]
        x, y, c = _place()
        cps = [pltpu.make_async_remote_copy(
            src_ref=g_refs[a].at[q, 1 - c], dst_ref=r_refs[a].at[q], send_sem=send_sems.at[first[a] + q],
            recv_sem=recv_sems.at[first[a] + q], device_id=(x, y, 1 - c), device_id_type=MESH)
            for a in range(n) for q in range(gs[a].shape[0])]
        for cp in cps:
            cp.start()
        for cp in cps:
            cp.wait()

    return pl.pallas_call(
        body, name=name, in_specs=[_ANY] * n, out_specs=[_ANY] * n,
        out_shape=[jax.ShapeDtypeStruct(v.shape[:1] + v.shape[2:], v.dtype) for v in gs],
        scratch_shapes=[pltpu.SemaphoreType.DMA((first[-1],)), pltpu.SemaphoreType.DMA((first[-1],))],
    )(*gs)


def _row_tile(rows, lanes, cap):
    if rows * lanes * 4 <= (1 << 20):
        return rows
    return max(d for d in range(8, cap + 1, 8) if rows % d == 0 and (d % 16 == 0 or rows % 16 != 0))


def _pair_sums(gs, rs, name):
    n = len(gs)

    def add(g, r, dtype):
        return (g.astype(F32) + r.astype(F32)).astype(dtype)

    def body(c_ref, *refs):
        g_refs, r_refs, o_refs = refs[:n], refs[n:2 * n], refs[2 * n:]
        o_refs[0][...] = add(g_refs[0][0], r_refs[0][...], o_refs[0].dtype)

        @pl.when(pl.program_id(0) == 0)
        def _():
            for a in range(1, n):
                o_refs[a][...] = add(g_refs[a][:, 0], r_refs[a][...], o_refs[a].dtype)

    def whole(shape, mine):
        if mine:
            return pl.BlockSpec(shape, lambda q, c_ref: (0, c_ref[0]) + (0,) * (len(shape) - 2))
        return pl.BlockSpec(shape, lambda q, c_ref: (0,) * len(shape))

    big = gs[0].shape
    return pl.pallas_call(
        body, name=name,
        grid_spec=pltpu.PrefetchScalarGridSpec(
            num_scalar_prefetch=1, grid=(big[0],),
            in_specs=[pl.BlockSpec((1, 1) + big[2:], lambda q, c_ref: (q, c_ref[0], 0, 0))]
            + [whole(g.shape[:1] + (1,) + g.shape[2:], True) for g in gs[1:]]
            + [pl.BlockSpec((1,) + big[2:], lambda q, c_ref: (q, 0, 0))]
            + [whole(r.shape, False) for r in rs[1:]],
            out_specs=[pl.BlockSpec((1,) + big[2:], lambda q, c_ref: (q, 0, 0))]
            + [whole(r.shape, False) for r in rs[1:]]),
        out_shape=[jax.ShapeDtypeStruct(r.shape, g.dtype) for g, r in zip(gs, rs)],
        compiler_params=_ARB,
    )(lax.axis_index("c").astype(jnp.int32).reshape(1), *gs, *rs)


def _adamw(parts, w, m, v, name, own=None):
    nl = len(parts)
    ns, rows, l = parts[0].shape
    tr = _row_tile(rows, l * ns, 304)
    nt = rows // tr
    c1 = 1.0 - ADAM_B1 ** ADAM_STEP
    c2 = 1.0 - ADAM_B2 ** ADAM_STEP

    def body(q_ref, *refs):
        own_refs = refs[:nl] if own is not None else None
        p_refs = refs[-7 - nl:-7]
        w_ref, m_ref, v_ref, g_ref, d_ref, nm_ref, nv_ref = refs[-7:]
        layer = pl.program_id(0)
        g = None
        for j in range(nl):
            gj = None
            for k in range(ns):
                term = p_refs[j][k].astype(F32)
                if own_refs is not None:
                    term = jnp.where(q_ref[0] == k, own_refs[j][0].astype(F32), term)
                gj = term if gj is None else gj + term
            g = gj if g is None else jnp.where(layer == j, gj, g)
        g_ref[...] = g
        nm = ADAM_B1 * m_ref[...] + (1.0 - ADAM_B1) * g
        nv = ADAM_B2 * v_ref[...] + (1.0 - ADAM_B2) * (g * g)
        nm_ref[...] = nm
        nv_ref[...] = nv
        d_ref[...] = -ADAM_LR * ((nm / c1) / (jnp.sqrt(nv / c2) + ADAM_EPS) + ADAM_WD * w_ref[...])

    def tile_of(j):
        return lambda la, i, q: jnp.where(la == j, i, jnp.where(la < j, 0, nt - 1))

    row = pl.BlockSpec((tr, l), lambda la, i, q: (la * nt + i, 0))
    own_specs = [] if own is None else [
        pl.BlockSpec((1, tr, l), lambda la, i, q, j=j: (q[0], tile_of(j)(la, i, q), 0)) for j in range(nl)]
    part_specs = [pl.BlockSpec((ns, tr, l), lambda la, i, q, j=j: (0, tile_of(j)(la, i, q), 0)) for j in range(nl)]
    chip = (2 * lax.axis_index("x") + lax.axis_index("y")).astype(jnp.int32).reshape(1)
    return pl.pallas_call(
        body, name=name,
        grid_spec=pltpu.PrefetchScalarGridSpec(
            num_scalar_prefetch=1, grid=(nl, nt),
            in_specs=own_specs + part_specs + [row, row, row], out_specs=[row] * 4),
        out_shape=[jax.ShapeDtypeStruct((nl * rows, l), F32)] * 4,
        compiler_params=pltpu.CompilerParams(dimension_semantics=("arbitrary", "arbitrary"), vmem_limit_bytes=VMEM_LIMIT),
    )(chip, *([] if own is None else own), *parts, w, m, v)


_HBM = pl.BlockSpec(memory_space=pltpu.HBM)
_SEM = pl.BlockSpec(memory_space=pltpu.SEMAPHORE)
_EFFECT = pltpu.SideEffectType.DATAFLOW_SIDE_EFFECTING


def _plan_all(x, y, c):
    me = 4 * x + 2 * y + c
    peers = [(x, y, 1 - c), (1 - x, y, c), (x, 1 - y, c), (1 - x, 1 - y, c),
             (1 - x, y, 1 - c), (x, 1 - y, 1 - c), (1 - x, 1 - y, 1 - c)]
    return [(None, me, p, 4 * p[0] + 2 * p[1] + p[2]) for p in peers]


def _plan_near(x, y, c):
    me = 4 * x + 2 * y + c
    peers = [(x, y, 1 - c), (1 - x, y, c), (x, 1 - y, c), (1 - x, 1 - y, c)]
    return [(None, me, p, 4 * p[0] + 2 * p[1] + p[2]) for p in peers]


def _plan_sibling(x, y, c):
    return [(2 * q + 1 - c, q, (x, y, 1 - c), q) for q in range(4)]


def _plan_chips(x, y, c):
    me = 2 * x + y
    return [(2 * qx + qy, me, (qx, qy, c), 2 * qx + qy) for qx, qy in ((1 - x, y), (x, 1 - y), (1 - x, 1 - y))]


def _split_copies(plan, src_refs, land_refs, send_sems, recv_sems, arrival):
    n = len(src_refs)
    entries = plan(*_place())
    per = len(entries)
    cps = []
    for a in range(n):
        for k, (src_slot, dst_slot, peer, back_slot) in enumerate(entries):
            src = src_refs[a] if src_slot is None else src_refs[a].at[src_slot]
            cps.append(pltpu.make_async_remote_copy(
                src_ref=src, dst_ref=land_refs[a].at[back_slot if arrival else dst_slot],
                send_sem=send_sems.at[per * a + k], recv_sem=recv_sems.at[per * a + k],
                device_id=peer, device_id_type=MESH))
    return cps


def _split_start(srcs, lands, plan, per, name):
    n = len(srcs)

    def body(*refs):
        for cp in _split_copies(plan, refs[:n], refs[n:2 * n], refs[2 * n], refs[2 * n + 1], False):
            cp.start()
        refs[-1][...] = jnp.zeros_like(refs[-1])

    both = list(srcs) + list(lands)
    outs = pl.pallas_call(
        body, name=name,
        out_shape=(pltpu.SemaphoreType.DMA((per * n,)), pltpu.SemaphoreType.DMA((per * n,)),
                   *[pltpu.HBM(v.shape, v.dtype) for v in both], jax.ShapeDtypeStruct((8, 128), F32)),
        in_specs=[_HBM] * (2 * n),
        out_specs=(_SEM, _SEM, *[_HBM] * (2 * n), pl.BlockSpec(memory_space=pltpu.VMEM)),
        input_output_aliases={i: 2 + i for i in range(2 * n)},
        compiler_params=pltpu.CompilerParams(has_side_effects=_EFFECT),
    )(*[pltpu.with_memory_space_constraint(v, pltpu.HBM) for v in both])
    return outs[0], outs[1], list(outs[2:2 + 2 * n]), outs[-1]


def _split_wait(send_sems, recv_sems, thru, plan, after, name):
    n = len(thru) // 2

    def body(*refs):
        for cp in _split_copies(plan, refs[:n], refs[n:2 * n], refs[2 * n], refs[2 * n + 1], True):
            cp.wait_send()
            cp.wait_recv()

    outs = pl.pallas_call(
        body, name=name, out_shape=tuple(pltpu.HBM(v.shape, v.dtype) for v in thru),
        in_specs=[_HBM] * (2 * n) + [_SEM, _SEM, pl.BlockSpec(memory_space=pl.ANY)],
        out_specs=[_HBM] * (2 * n), input_output_aliases={i: i for i in range(2 * n)},
        compiler_params=pltpu.CompilerParams(has_side_effects=_EFFECT),
    )(*thru, send_sems, recv_sems, after)
    return list(outs[:n]), list(outs[n:])


_SMALL_IN = ("ln_in_g", "ln_in_b")
_SMALL = ("w_s", "b_in", "sinks", "vn_g", "vn_b", "b_s", "b_out", "ln_g", "ln_b")


def _rows128(a):
    flat = a.reshape(-1)
    return jnp.pad(flat, (0, (-flat.shape[0]) % 128)).reshape(-1, 128)


def _pack_small(d, names):
    rows = jnp.concatenate([_rows128(d[n]) for n in names])
    return jnp.pad(rows, ((0, (-rows.shape[0]) % 8), (0, 0)))


def _unpack_small(p, like, names):
    off, out = 0, {}
    for n in names:
        size = like[n].size
        rows = -(-size // 128)
        out[n] = p[off:off + rows].reshape(-1)[:size].reshape(like[n].shape)
        off += rows
    return out


def _owner_blocks(g, axis):
    sh = g.shape
    g = g.reshape(sh[:axis] + (4, 2, sh[axis] // N_DEV) + sh[axis + 1:])
    return jnp.moveaxis(g, (axis, axis + 1), (0, 1))


def kernel(x, ln_in_g, ln_in_b, w_in, b_in, sinks, vn_g, vn_b, w_s, b_s, p_a, p_b, w_out, b_out, ln_g, ln_b, loss_target, m_ln_in_g, m_ln_in_b, m_w_in, m_b_in, m_sinks, m_vn_g, m_vn_b, m_w_s, m_b_s, m_p_a, m_p_b, m_w_out, m_b_out, m_ln_g, m_ln_b, v_ln_in_g, v_ln_in_b, v_w_in, v_b_in, v_sinks, v_vn_g, v_vn_b, v_w_s, v_b_s, v_p_a, v_p_b, v_w_out, v_b_out, v_ln_g, v_ln_b):
    nseq, seq, _ = x.shape
    t = nseq * seq
    nblk_seq = seq // BLK
    x2 = x.reshape(t, D)
    tgt = loss_target.reshape(t, D)

    def turned(a):
        return jnp.swapaxes(a, 1, 2)

    w_in_t = turned(w_in)

    def blocks(l):
        return [w_in_t[l].astype(BF16), p_a[l].astype(BF16), p_b[l].astype(BF16), w_out[l].astype(BF16)]

    def full_weights(g):
        w_t_full = g[0].reshape(IN_COLS, D)
        pa_full = jnp.moveaxis(g[1], 0, 1).reshape(Q_W, D)
        pb_full = jnp.moveaxis(g[2], 0, 1).reshape(SGU_W, D)
        wo_full = g[3].reshape(D, D)
        return dict(w_t=w_t_full, pa=pa_full, pb=pb_full, wo=wo_full)

    def landing(bs):
        return [lax.empty((N_DEV,) + v.shape, v.dtype) for v in bs]

    def with_own(landed, sent):
        return [lax.dynamic_update_index_in_dim(g, b, me, 0) for g, b in zip(landed, sent)]

    me = 4 * lax.axis_index("x") + 2 * lax.axis_index("y") + lax.axis_index("c")
    blocks0 = blocks(0)
    a_send, a_recv, a_thru, a_token = _split_start(blocks0[:1], landing(blocks0[:1]), _plan_near, 4,
                                                   "allgather_w_in0_start")
    rest0 = [b + a_token[0, 0].astype(BF16) for b in blocks0[1:]]
    b_send, b_recv, b_thru, b_token = _split_start(rest0, landing(rest0), _plan_near, 4, "allgather_rest0_start")
    xs = [_ln_fwd(x2, ln_in_g + b_token[0, 0], ln_in_b, "ln_in_fwd")]
    sent, landed = _split_wait(a_send, a_recv, a_thru, _plan_near, xs[0], "allgather_w_in0_wait")
    gathered0 = with_own(_forward_sibling(landed, "allgather_w_in0_forward"), sent)
    blocks1, gathered0 = lax.optimization_barrier((blocks(1), gathered0))
    ag_send, ag_recv, ag_thru, ag_token = _split_start(blocks1, landing(blocks1), _plan_all, 7,
                                                       "allgather_weights1_start")
    weights = [None, None]
    bsb = jnp.broadcast_to(b_s[:, :, :, None], (DEPTH, 4, BLK, BLK))
    bias = _band_bias()

    saved = []
    for l in range(DEPTH):
        if l == 1:
            sent, landed = _split_wait(ag_send, ag_recv, ag_thru, _plan_all, xs[1], "allgather_weights1_wait")
            weights[1] = full_weights(with_own(landed, sent))
        w_t = weights[l]["w_t"] if l else gathered0[0].reshape(IN_COLS, D)
        last = l == DEPTH - 1
        b_l = b_in[l].reshape(1, -1) + (ag_token[0, 0] if l == 0 else 0.0)
        hm, hr = _inproj(xs[l], w_t, b_l, f"inproj{l}")
        ya, yb, prob, psink = _mixer_fwd(hm, sinks[l], bias, vn_g[l].reshape(1, -1), vn_b[l].reshape(1, -1),
                                         w_s[l], bsb[l], nblk_seq, f"mixer_fwd{l}")
        if l == 0:
            sent, landed = _split_wait(b_send, b_recv, b_thru, _plan_near, ya, "allgather_rest0_wait")
            weights[0] = full_weights(gathered0 + with_own(_forward_sibling(landed, "allgather_rest0_forward"), sent))
        wl = weights[l]
        outs = _tail_fwd(xs[l], ya, yb, hr, wl["pa"], wl["pb"], wl["wo"], b_out[l].reshape(1, D),
                         ln_g[l].reshape(1, D), ln_b[l].reshape(1, D), f"tail_fwd{l}", last)
        saved.append((hm, hr, ya, yb, prob, psink) + tuple(outs[:4]))
        if not last:
            xs.append(outs[4])

    small = {n: [None] * DEPTH for n in _SMALL}
    names = ("w_in", "p_a", "p_b", "w_out")
    owner_axis = {"w_in": 0, "p_a": 1, "p_b": 1, "w_out": 0}
    token = jnp.zeros((8, 128), F32)
    dx = tgt
    split = [None] * DEPTH
    for l in reversed(range(DEPTH)):
        hm, hr, ya, yb, prob, psink, pa, pb, merged, z = saved[l]
        wl = weights[l]
        dz, dpa, dpb, dhr, dya, dyb, acc, gbr = _tail_bwd(
            dx, z, pa, pb, hr, wl["wo"], wl["pa"], wl["pb"], ln_g[l].reshape(1, D) + token[0, 0],
            ln_b[l].reshape(1, D), f"tail_bwd{l}", l == DEPTH - 1)
        if l == DEPTH - 1:
            sq_err = acc[3:4, 0:128]
        dhm, gbm, gsk, gvn, gws, gbs = _mixer_bwd(
            hm, dya, dyb, prob, psink, vn_g[l].reshape(1, -1), vn_b[l].reshape(1, -1), w_s[l], bsb[l],
            f"mixer_bwd{l}")
        grads = {"w_in": _wgrad(dhr, xs[l], R_W // 2, f"wgrad_in_route{l}",
                                under=_wgrad(dhm, xs[l], MAIN_W // 2, f"wgrad_in_main{l}", rows=IN_COLS)),
                 "p_a": _wgrad(ya, dpa, Q_W, f"wgrad_pa{l}"), "p_b": _wgrad(yb, dpb, SGU_W, f"wgrad_pb{l}"),
                 "w_out": _wgrad(merged, dz, D, f"wgrad_out{l}")}
        small["b_in"][l] = jnp.concatenate([gbm[0], gbr[0]])
        small["sinks"][l] = gsk[:, 0]
        small["vn_g"][l], small["vn_b"][l] = gvn[0], gvn[1]
        small["w_s"][l], small["b_s"][l] = gws, gbs[:, :, 0]
        small["ln_g"][l], small["ln_b"][l], small["b_out"][l] = acc[0], acc[1], acc[2]
        parts = [_owner_blocks(grads[n], owner_axis[n]) for n in names]
        if l == 0:
            packed = _pack_small({n: jnp.stack(v) for n, v in small.items()}, _SMALL)
            parts.append(jnp.broadcast_to(packed[None, None], (1, 2) + packed.shape))
        if l == 0:
            from_sib = _swap_sibling(parts, f"rs_sibling{l}")
        else:
            halves = [p.reshape((N_DEV,) + p.shape[2:]) for p in parts]
            sib = _split_start(halves, [lax.empty((4,) + p.shape[2:], p.dtype) for p in parts], _plan_sibling, 4,
                               f"rs_sibling{l}_start")
            dx = _dx_inproj(dz, dhm, dhr, wl["w_t"], sib[3], f"dx_inproj{l}")
            halves, from_sib = _split_wait(sib[0], sib[1], sib[2], _plan_sibling, dx, f"rs_sibling{l}_wait")
            parts = [h.reshape(p.shape) for h, p in zip(halves, parts)]
        pair = list(_pair_sums(parts, from_sib, f"pair_sums{l}"))
        if l == 0:
            pair[4] = jnp.broadcast_to(pair[4], (4,) + packed.shape)
        lands = [jnp.zeros(p.shape, p.dtype) for p in pair]
        split[l] = _split_start(pair, lands, _plan_chips, 3, f"rs_chips{l}_start")
        token = split[l][3]
        if l == 0:
            grad_x, acc_in = _dx_inproj(dz, dhm, dhr, wl["w_t"], token, f"dx_inproj{l}",
                                        ln_in=(x2, ln_in_g.reshape(1, D)))
    all_in, all_sq = _allgather8([acc_in, jnp.broadcast_to(sq_err, (8, 128))], "allgather_ln_in")
    loss = jnp.sum(all_sq[:, 0, 0]) * (0.5 / D)

    given = {"w_in": (w_in_t, turned(m_w_in), turned(v_w_in)), "p_a": (p_a, m_p_a, v_p_a),
             "p_b": (p_b, m_p_b, v_p_b), "w_out": (w_out, m_w_out, v_w_out)}
    waited = [_split_wait(split[l][0], split[l][1], split[l][2], _plan_chips, all_in, f"rs_chips{l}_wait")
              for l in range(DEPTH)]
    res = {}
    for a, n in enumerate(names):
        rows, lanes = waited[0][1][a].shape[1:]
        outs = _adamw([waited[l][1][a] for l in range(DEPTH)], *[v.reshape(DEPTH * rows, lanes) for v in given[n]],
                      f"adamw_{n}", own=[waited[l][0][a] for l in range(DEPTH)])
        res[n] = [o.reshape(given[n][0].shape) for o in outs]
    res["w_in"] = [turned(o) for o in res["w_in"]]

    w_small = dict(ln_in_g=ln_in_g, ln_in_b=ln_in_b, b_in=b_in, sinks=sinks, vn_g=vn_g, vn_b=vn_b, w_s=w_s, b_s=b_s,
                   b_out=b_out, ln_g=ln_g, ln_b=ln_b)
    m_small = dict(ln_in_g=m_ln_in_g, ln_in_b=m_ln_in_b, b_in=m_b_in, sinks=m_sinks, vn_g=m_vn_g, vn_b=m_vn_b,
                   w_s=m_w_s, b_s=m_b_s, b_out=m_b_out, ln_g=m_ln_g, ln_b=m_ln_b)
    v_small = dict(ln_in_g=v_ln_in_g, ln_in_b=v_ln_in_b, b_in=v_b_in, sinks=v_sinks, vn_g=v_vn_g, vn_b=v_vn_b,
                   w_s=v_w_s, b_s=v_b_s, b_out=v_b_out, ln_g=v_ln_g, ln_b=v_ln_b)
    outs = _adamw([waited[0][1][4]], *[_pack_small(d, _SMALL) for d in (w_small, m_small, v_small)], "adamw_small",
                  own=[waited[0][0][4]])
    outs_in = _adamw([all_in], *[jnp.pad(jnp.stack([d[n] for n in _SMALL_IN]), ((0, 6), (0, 0)))
                                 for d in (w_small, m_small, v_small)], "adamw_ln_in")
    for k in range(4):
        u = _unpack_small(outs[k], w_small, _SMALL)
        u.update({n: outs_in[k][r] for r, n in enumerate(_SMALL_IN)})
        for n in u:
            res.setdefault(n, [None] * 4)[k] = u[n]

    order = ("ln_in_g", "ln_in_b", "w_in", "b_in", "sinks", "vn_g", "vn_b", "w_s", "b_s", "p_a", "p_b", "w_out",
             "b_out", "ln_g", "ln_b")
    return (loss, grad_x.reshape(x.shape), *[res[n][0] for n in order], *[res[n][1] for n in order],
            *[res[n][2] for n in order], *[res[n][3] for n in order])
```

```python
import jax
import jax.numpy as jnp
from jax import lax
from jax.experimental import pallas as pl
from jax.experimental.pallas import tpu as pltpu

F32 = jnp.float32
BF16 = jnp.bfloat16

D = 1024
BLK = 128
N_KV = 2
Q_W, KV_W, SGU_W = 512, 128, 512
C_Q, C_K, C_V, C_GA, C_UB, C_VB, C_GB = 0, 512, 640, 768, 1280, 1792, 2304
MAIN_W = 2816
R_W = 2048
IN_COLS = MAIN_W + R_W
N_DEV = 8

DEPTH = 2
ALPHA = (2.0 * DEPTH) ** 0.25
LN_EPS = 1e-5
ATTN_SCALE = 0.125
NEG = float(jnp.finfo(jnp.float32).min)

ADAM_LR, ADAM_B1, ADAM_B2, ADAM_EPS, ADAM_WD, ADAM_STEP = 0.001, 0.9, 0.999, 1e-08, 0.01, 10

TM = 512
TM_EW = 1024
TM_MM = 512
TM_T = 512
NB = TM // BLK
MESH = pl.DeviceIdType.MESH
VMEM_LIMIT = 56 * 1024 * 1024

_ARB = pltpu.CompilerParams(dimension_semantics=("arbitrary",), vmem_limit_bytes=VMEM_LIMIT)


def _sigmoid(x):
    return 1.0 / (1.0 + jnp.exp(-x))


_GELU_C = 0.7978845608028654
_GELU_A = 0.044715


def _gelu_parts(x):
    x2 = x * x
    t = jnp.tanh(x * (_GELU_C + (_GELU_C * _GELU_A) * x2))
    hx = 0.5 * x
    return hx, t, x2


def _gelu(x):
    hx, t, _ = _gelu_parts(x)
    return hx + hx * t


def _gelu_and_grad(x):
    hx, t, x2 = _gelu_parts(x)
    grad = 0.5 + 0.5 * t + (hx - hx * (t * t)) * (_GELU_C + (3.0 * _GELU_C * _GELU_A) * x2)
    return hx + hx * t, grad


def _ln_stats(x):
    mu = jnp.mean(x, axis=-1, keepdims=True)
    xc = x - mu
    var = jnp.mean(xc * xc, axis=-1, keepdims=True)
    rstd = lax.rsqrt(var + LN_EPS)
    return xc * rstd, rstd


def _ln_bwd(dy_g, xhat, rstd):
    m1 = jnp.mean(dy_g, axis=-1, keepdims=True)
    m2 = jnp.mean(dy_g * xhat, axis=-1, keepdims=True)
    return rstd * (dy_g - m1 - xhat * m2)


def _colsum(x):
    return jnp.sum(x, axis=0, keepdims=True)


def _dot(a, b):
    return jnp.dot(a, b, preferred_element_type=F32)


def _dot_nt(a, b):
    return lax.dot_general(a, b, (((1,), (1,)), ((), ())), preferred_element_type=F32)


def _dot_tn(a, b):
    return lax.dot_general(a, b, (((0,), (0,)), ((), ())), preferred_element_type=F32)


def _head_place(hk, g):
    j = 4 * hk + g
    return j, j // 2, j % 2


def _head_rows(x, hk):
    d = lax.broadcasted_iota(jnp.int32, x.shape, 0)
    return jnp.where((d >= 64 * hk) & (d < 64 * hk + 64), x, 0.0).astype(BF16)


def _head_lanes(x, hk):
    d = lax.broadcasted_iota(jnp.int32, x.shape, 1)
    return jnp.where((d >= 64 * hk) & (d < 64 * hk + 64), x, 0.0)


def _band_bias():
    kpos = lax.broadcasted_iota(jnp.int32, (2 * BLK, 4 * BLK), 0)
    row = lax.broadcasted_iota(jnp.int32, (2 * BLK, 4 * BLK), 1) & (BLK - 1)
    band = (kpos > row) & (kpos <= row + BLK)
    return jnp.stack([jnp.where(band, 0.0, NEG), jnp.where(band & (kpos >= BLK), 0.0, NEG)]).astype(F32)


def _stack_q(q, hk):
    parts = []
    for g in range(4):
        _, p, pos = _head_place(hk, g)
        qp = q[:, BLK * p:BLK * (p + 1)] * ATTN_SCALE
        if pos != hk:
            qp = pltpu.roll(qp, 64, 1)
        parts.append(qp.astype(BF16))
    return jnp.concatenate(parts, axis=0)


def _attn_probs(q4, kh, hk, sinks_ref, bias):
    s_t = _dot_nt(kh, q4) + bias
    sink_row = jnp.concatenate(
        [jnp.full((1, BLK), sinks_ref[4 * hk + g], F32) for g in range(4)], axis=1)
    m = jnp.maximum(jnp.max(s_t, axis=0, keepdims=True), sink_row)
    p_un = jnp.exp(s_t - m)
    e_sink = jnp.exp(sink_row - m)
    inv = 1.0 / (jnp.sum(p_un, axis=0, keepdims=True) + e_sink)
    return (p_un * inv).astype(BF16), e_sink * inv


def _unstack_heads(x4, hk, pairs):
    for g in range(4):
        _, p, pos = _head_place(hk, g)
        xg = x4[BLK * g:BLK * (g + 1)]
        if pos != hk:
            xg = pltpu.roll(xg, 64, 1)
        pairs[p] = xg if pairs[p] is None else pairs[p] + xg
    return pairs


def _attn_fwd(q, kband, vband, sinks_ref, bias, save):
    pairs = [None] * 4
    vband_t = vband.T
    for hk in range(N_KV):
        prob_t, p_sink = _attn_probs(_stack_q(q, hk), _head_lanes(kband, hk).astype(BF16), hk, sinks_ref, bias)
        save(hk, prob_t, p_sink)
        o_t = _dot(_head_rows(vband_t, hk), prob_t)
        pairs = _unstack_heads(o_t.T, hk, pairs)
    return jnp.concatenate(pairs, axis=1)


def _tril_mask():
    r = lax.broadcasted_iota(jnp.int32, (BLK, BLK), 0)
    c = lax.broadcasted_iota(jnp.int32, (BLK, BLK), 1)
    return c <= r


def _sgu_fwd(u, v, vn_g, vn_b, wt, bsb_ref):
    vhat, rstd = _ln_stats(v)
    vn = vhat * vn_g + vn_b
    mixed = jnp.concatenate(
        [_dot(wt[g], vn[:, BLK * g:BLK * (g + 1)].astype(BF16)) + bsb_ref[g] for g in range(4)], axis=1)
    return vhat, rstd, vn, mixed


def _cols(ref, rows, col, width):
    return ref[rows, col:col + width].astype(F32)


def _band(hm_ref, hprev_ref, s, col):
    r0 = s * BLK
    cur = hm_ref[r0:r0 + BLK, col:col + KV_W]
    if s == 0:
        off = 0 if col == C_K else KV_W
        prev = hprev_ref[:, off:off + KV_W]
    else:
        prev = hm_ref[r0 - BLK:r0, col:col + KV_W]
    return jnp.concatenate([prev, cur], axis=0).astype(F32)


def _h_main_specs(nt, rev):
    def tile(g):
        return nt - 1 - g if rev else g

    return [pl.BlockSpec((TM, MAIN_W), lambda g: (tile(g), 0)),
            pl.BlockSpec((BLK, 2 * KV_W), lambda g: (jnp.maximum(tile(g) * NB - 1, 0), 2))]


_CONST2 = lambda g: (0, 0)
_CONST3 = lambda g: (0, 0, 0)


def _ln_fwd(x, g, b, name):
    t = x.shape[0]

    def body(x_ref, g_ref, b_ref, o_ref):
        xhat, _ = _ln_stats(x_ref[...])
        o_ref[...] = xhat * g_ref[...] + b_ref[...]

    return pl.pallas_call(
        body, name=name, grid=(t // TM_EW,),
        in_specs=[pl.BlockSpec((TM_EW, D), lambda i: (i, 0)), pl.BlockSpec((1, D), _CONST2),
                  pl.BlockSpec((1, D), _CONST2)],
        out_specs=pl.BlockSpec((TM_EW, D), lambda i: (i, 0)),
        out_shape=jax.ShapeDtypeStruct((t, D), F32), compiler_params=_ARB,
    )(x, g.reshape(1, D), b.reshape(1, D))


def _inproj(x, w_t, b, name):
    t = x.shape[0]

    def body(x_ref, wt_ref, b_ref, hm_ref, hr_ref):
        xb = x_ref[...].astype(BF16)
        hm_ref[...] = (_dot_nt(xb, wt_ref[0:MAIN_W, :]) + b_ref[:, 0:MAIN_W]).astype(BF16)
        hr_ref[...] = (_dot_nt(xb, wt_ref[MAIN_W:IN_COLS, :]) + b_ref[:, MAIN_W:IN_COLS]).astype(BF16)

    return pl.pallas_call(
        body, name=name, grid=(t // TM_MM,),
        in_specs=[pl.BlockSpec((TM_MM, D), lambda i: (i, 0)),
                  pl.BlockSpec((IN_COLS, D), _CONST2), pl.BlockSpec((1, IN_COLS), _CONST2)],
        out_specs=[pl.BlockSpec((TM_MM, MAIN_W), lambda i: (i, 0)), pl.BlockSpec((TM_MM, R_W), lambda i: (i, 0))],
        out_shape=[jax.ShapeDtypeStruct((t, MAIN_W), BF16), jax.ShapeDtypeStruct((t, R_W), BF16)],
        compiler_params=_ARB,
    )(x, w_t, b)


def _mixer_fwd(hm, sinks, bias, vn_g, vn_b, w_s, bsb, nblk_seq, name):
    t = hm.shape[0]
    nt = t // TM

    def body(sinks_ref, hm_ref, hprev_ref, bias_ref, vng_ref, vnb_ref, ws_ref, bsb_ref,
             ya_ref, yb_ref, prob_ref, psink_ref):
        i = pl.program_id(0)
        tril = _tril_mask()
        wt = [jnp.where(tril, ws_ref[g], 0.0).astype(BF16) for g in range(4)]
        for s in range(NB):
            r0 = s * BLK
            rows = slice(r0, r0 + BLK)
            bias = bias_ref[jnp.where((i * NB + s) % nblk_seq == 0, 1, 0)]

            def save(hk, prob_t, p_sink, s=s):
                prob_ref[N_KV * s + hk] = prob_t
                psink_ref[N_KV * s + hk] = jnp.broadcast_to(p_sink, (8, 4 * BLK))

            attn = _attn_fwd(_cols(hm_ref, rows, C_Q, Q_W), _band(hm_ref, hprev_ref, s, C_K),
                             _band(hm_ref, hprev_ref, s, C_V), sinks_ref, bias, save)
            g_a = _cols(hm_ref, rows, C_GA, Q_W)
            ya_ref[rows, :] = (attn * (g_a * _sigmoid(g_a))).astype(BF16)
            u = _gelu(_cols(hm_ref, rows, C_UB, SGU_W))
            mixed = _sgu_fwd(u, _gelu(_cols(hm_ref, rows, C_VB, SGU_W)), vng_ref[...], vnb_ref[...], wt, bsb_ref)[-1]
            g_b = _cols(hm_ref, rows, C_GB, SGU_W)
            yb_ref[rows, :] = (u * mixed * (g_b * _sigmoid(g_b))).astype(BF16)

    ngrp = N_KV * NB
    return pl.pallas_call(
        body, name=name, grid=(nt,),
        in_specs=[pl.BlockSpec(memory_space=pltpu.SMEM)] + _h_main_specs(nt, False) + [
            pl.BlockSpec((2, 2 * BLK, 4 * BLK), _CONST3),
            pl.BlockSpec((1, SGU_W), _CONST2), pl.BlockSpec((1, SGU_W), _CONST2),
            pl.BlockSpec((4, BLK, BLK), _CONST3), pl.BlockSpec((4, BLK, BLK), _CONST3)],
        out_specs=[pl.BlockSpec((TM, Q_W), lambda i: (i, 0)), pl.BlockSpec((TM, SGU_W), lambda i: (i, 0)),
                   pl.BlockSpec((ngrp, 2 * BLK, 4 * BLK), lambda i: (i, 0, 0)),
                   pl.BlockSpec((ngrp, 8, 4 * BLK), lambda i: (i, 0, 0))],
        out_shape=[jax.ShapeDtypeStruct((t, Q_W), BF16), jax.ShapeDtypeStruct((t, SGU_W), BF16),
                   jax.ShapeDtypeStruct((nt * ngrp, 2 * BLK, 4 * BLK), BF16),
                   jax.ShapeDtypeStruct((nt * ngrp, 8, 4 * BLK), F32)],
        compiler_params=_ARB,
    )(sinks, hm, hm, bias, vn_g, vn_b, w_s, bsb)


def _tail_fwd(x, ya, yb, hr, pa_w, pb_w, wo, b_out, ln_g, ln_b, name, last):
    t = x.shape[0]

    def body(x_ref, ya_ref, yb_ref, hr_ref, paw_ref, pbw_ref, wo_ref, bo_ref, g_ref, b_ref,
             pa_ref, pb_ref, mg_ref, z_ref, *xn_ref):
        pa = _dot(ya_ref[...], paw_ref[...])
        pb = _dot(yb_ref[...], pbw_ref[...])
        pa_ref[...] = pa.astype(BF16)
        pb_ref[...] = pb.astype(BF16)
        everything = slice(None)
        merged = _sigmoid(_cols(hr_ref, everything, 0, D)) * pa + _sigmoid(_cols(hr_ref, everything, D, D)) * pb
        mb = merged.astype(BF16)
        mg_ref[...] = mb
        z = ALPHA * x_ref[...] + (_dot(mb, wo_ref[...]) + bo_ref[...])
        z_ref[...] = z
        if not last:
            zhat, _ = _ln_stats(z)
            xn_ref[0][...] = zhat * g_ref[...] + b_ref[...]

    row = lambda w: pl.BlockSpec((TM_T, w), lambda i: (i, 0))
    vec = pl.BlockSpec((1, D), _CONST2)
    n_f32 = 1 if last else 2
    return pl.pallas_call(
        body, name=name, grid=(t // TM_T,),
        in_specs=[row(D), row(Q_W), row(SGU_W), row(R_W),
                  pl.BlockSpec((Q_W, D), _CONST2), pl.BlockSpec((SGU_W, D), _CONST2), pl.BlockSpec((D, D), _CONST2),
                  vec, vec, vec],
        out_specs=[row(D)] * (3 + n_f32),
        out_shape=[jax.ShapeDtypeStruct((t, D), BF16)] * 3 + [jax.ShapeDtypeStruct((t, D), F32)] * n_f32,
        compiler_params=_ARB,
    )(x, ya, yb, hr, pa_w, pb_w, wo, b_out, ln_g, ln_b)


def _tail_bwd(dxn, z, pa, pb, hr, wo, pa_w, pb_w, ln_g, ln_b, name, from_loss):
    t = dxn.shape[0]

    def body(dxn_ref, z_ref, pa_ref, pb_ref, hr_ref, wo_ref, paw_ref, pbw_ref, g_ref, b_ref,
             dz_ref, dpa_ref, dpb_ref, dhr_ref, dya_ref, dyb_ref, acc_ref, gbr_ref):
        @pl.when(pl.program_id(0) == 0)
        def _():
            acc_ref[...] = jnp.zeros_like(acc_ref)
            gbr_ref[...] = jnp.zeros_like(gbr_ref)

        zhat, rstd = _ln_stats(z_ref[...])
        if from_loss:
            err = zhat * g_ref[...] + b_ref[...] - dxn_ref[...]
            dxn_v = err * (1.0 / D)
            sq = jnp.sum(jnp.sum(err * err, axis=1, keepdims=True), axis=0, keepdims=True)
            acc_ref[3:4, :] += jnp.broadcast_to(sq, (1, D))
        else:
            dxn_v = dxn_ref[...]
        dz = _ln_bwd(dxn_v * g_ref[...], zhat, rstd)
        dz_ref[...] = dz
        acc_ref[0:1, :] += _colsum(dxn_v * zhat)
        acc_ref[1:2, :] += _colsum(dxn_v)
        acc_ref[2:3, :] += _colsum(dz)
        dmerged = _dot_nt(dz.astype(BF16), wo_ref[...])
        everything = slice(None)
        sa = _sigmoid(_cols(hr_ref, everything, 0, D))
        sb = _sigmoid(_cols(hr_ref, everything, D, D))
        dpa = (dmerged * sa).astype(BF16)
        dpb = (dmerged * sb).astype(BF16)
        dpa_ref[...] = dpa
        dpb_ref[...] = dpb
        dra = dmerged * pa_ref[...].astype(F32) * (sa * (1.0 - sa))
        drb = dmerged * pb_ref[...].astype(F32) * (sb * (1.0 - sb))
        dhr_ref[:, 0:D] = dra.astype(BF16)
        dhr_ref[:, D:2 * D] = drb.astype(BF16)
        gbr_ref[0:1, 0:D] += _colsum(dra)
        gbr_ref[0:1, D:2 * D] += _colsum(drb)
        dya_ref[...] = _dot_nt(dpa, paw_ref[...]).astype(BF16)
        dyb_ref[...] = _dot_nt(dpb, pbw_ref[...]).astype(BF16)

    row = lambda w: pl.BlockSpec((TM_T, w), lambda i: (i, 0))
    vec = pl.BlockSpec((1, D), _CONST2)
    return pl.pallas_call(
        body, name=name, grid=(t // TM_T,),
        in_specs=[row(D), row(D), row(D), row(D), row(R_W),
                  pl.BlockSpec((D, D), _CONST2), pl.BlockSpec((Q_W, D), _CONST2), pl.BlockSpec((SGU_W, D), _CONST2),
                  vec, vec],
        out_specs=[row(D), row(D), row(D), row(R_W), row(Q_W), row(SGU_W), pl.BlockSpec((8, D), _CONST2),
                   pl.BlockSpec((8, R_W), _CONST2)],
        out_shape=[jax.ShapeDtypeStruct((t, D), F32), jax.ShapeDtypeStruct((t, D), BF16),
                   jax.ShapeDtypeStruct((t, D), BF16), jax.ShapeDtypeStruct((t, R_W), BF16),
                   jax.ShapeDtypeStruct((t, Q_W), BF16), jax.ShapeDtypeStruct((t, SGU_W), BF16),
                   jax.ShapeDtypeStruct((8, D), F32), jax.ShapeDtypeStruct((8, R_W), F32)],
        compiler_params=_ARB,
    )(dxn, z, pa, pb, hr, wo, pa_w, pb_w, ln_g, ln_b)


def _mixer_bwd(hm, dya, dyb, prob, psink, vn_g, vn_b, w_s, bsb, name):
    t = hm.shape[0]
    nt = t // TM
    ngrp = N_KV * NB

    def body(hm_ref, hprev_ref, prob_ref, psink_ref, dya_ref, dyb_ref, vng_ref, vnb_ref, ws_ref, bsb_ref,
             dhm_ref, gbm_ref, gsk_ref, gvn_ref, gws_ref, gbs_ref, dk_carry, dv_carry):
        gi = pl.program_id(0)

        @pl.when(gi == 0)
        def _():
            for r in (gbm_ref, gsk_ref, gvn_ref, gws_ref, gbs_ref, dk_carry, dv_carry):
                r[...] = jnp.zeros_like(r)

        tril = _tril_mask()
        wt = [jnp.where(tril, ws_ref[g], 0.0).astype(BF16) for g in range(4)]
        vng = vng_ref[...]
        ones8 = jnp.ones((8, BLK), BF16)

        def put(rows, col, val):
            dhm_ref[rows, col:col + val.shape[1]] = val.astype(BF16)

        for s in reversed(range(NB)):
            r0 = s * BLK
            rows = slice(r0, r0 + BLK)
            q = _cols(hm_ref, rows, C_Q, Q_W)
            kband = _band(hm_ref, hprev_ref, s, C_K)
            vband = _band(hm_ref, hprev_ref, s, C_V)
            g_a = _cols(hm_ref, rows, C_GA, Q_W)
            sg = _sigmoid(g_a)
            dya_v = _cols(dya_ref, rows, 0, Q_W)
            d_o = dya_v * (g_a * sg)
            o_pairs, dq_pairs = [None] * 4, [None] * 4
            dkband = jnp.zeros((2 * BLK, KV_W), F32)
            dvband = jnp.zeros((2 * BLK, KV_W), F32)
            kband_t, vband_t = kband.T, vband.T
            for hk in range(N_KV):
                q4 = _stack_q(q, hk)
                prob_b = prob_ref[N_KV * s + hk]
                p_sink = psink_ref[N_KV * s + hk][0:1, :]
                o_t = _dot(_head_rows(vband_t, hk), prob_b)
                o_pairs = _unstack_heads(o_t.T, hk, o_pairs)
                parts = []
                for g in range(4):
                    _, p, pos = _head_place(hk, g)
                    dp = d_o[:, BLK * p:BLK * (p + 1)]
                    parts.append(pltpu.roll(dp, 64, 1) if pos != hk else dp)
                do4 = _head_lanes(jnp.concatenate(parts, axis=0), hk)
                do4b = do4.astype(BF16)
                delta = _colsum(do4.T * o_t)
                vh = _head_lanes(vband, hk).astype(BF16)
                ds_t = prob_b.astype(F32) * (_dot_nt(vh, do4b) - delta)
                dsb = ds_t.astype(BF16)
                dq4_t = _dot(_head_rows(kband_t, hk), dsb)
                dq_pairs = _unstack_heads(dq4_t.T * ATTN_SCALE, hk, dq_pairs)
                dkband = dkband + _head_lanes(_dot(dsb, q4), hk)
                dvband = dvband + _dot(prob_b, do4b)
                dsk = p_sink * delta
                for g in range(4):
                    j = 4 * hk + g
                    tot = jnp.sum(dsk[:, BLK * g:BLK * (g + 1)], axis=1, keepdims=True)
                    gsk_ref[j:j + 1, :] += jnp.broadcast_to(-tot, (1, 128))
            attn = jnp.concatenate(o_pairs, axis=1)
            put(rows, C_Q, jnp.concatenate(dq_pairs, axis=1))
            put(rows, C_K, dkband[BLK:2 * BLK] + dk_carry[...])
            put(rows, C_V, dvband[BLK:2 * BLK] + dv_carry[...])
            dk_carry[...] = dkband[0:BLK]
            dv_carry[...] = dvband[0:BLK]
            put(rows, C_GA, dya_v * attn * (sg * (1.0 + g_a * (1.0 - sg))))
            u, du_du_b = _gelu_and_grad(_cols(hm_ref, rows, C_UB, SGU_W))
            v, dv_dv_b = _gelu_and_grad(_cols(hm_ref, rows, C_VB, SGU_W))
            g_b = _cols(hm_ref, rows, C_GB, SGU_W)
            vhat, rstd, vn, mixed = _sgu_fwd(u, v, vng, vnb_ref[...], wt, bsb_ref)
            sgb = _sigmoid(g_b)
            silu_b = g_b * sgb
            dyb_v = _cols(dyb_ref, rows, 0, SGU_W)
            du = dyb_v * mixed * silu_b
            dmixed = dyb_v * u * silu_b
            put(rows, C_GB, dyb_v * u * mixed * (sgb * (1.0 + g_b * (1.0 - sgb))))
            dvn_parts = []
            for g in range(4):
                cols = slice(BLK * g, BLK * (g + 1))
                dmg = dmixed[:, cols]
                dmgb = dmg.astype(BF16)
                dvn_parts.append(_dot_tn(wt[g], dmgb))
                gws_ref[g] += jnp.where(tril, _dot_nt(dmgb, vn[:, cols].astype(BF16)), 0.0)
                gbs_ref[g] += dmg
            dvn = jnp.concatenate(dvn_parts, axis=1)
            gvn_ref[0:1, :] += _colsum(dvn * vhat)
            gvn_ref[1:2, :] += _colsum(dvn)
            dv = _ln_bwd(dvn * vng, vhat, rstd)
            put(rows, C_UB, du * du_du_b)
            put(rows, C_VB, dv * dv_dv_b)
            gbm_ref[...] += _dot(ones8, dhm_ref[rows, :])

        @pl.when(gi == nt - 1)
        def _():
            for g in range(4):
                gbs_ref[g] = jnp.broadcast_to(jnp.sum(gbs_ref[g], axis=1, keepdims=True), (BLK, BLK))

    row = lambda w: pl.BlockSpec((TM, w), lambda g: (nt - 1 - g, 0))
    return pl.pallas_call(
        body, name=name, grid=(nt,),
        in_specs=_h_main_specs(nt, True) + [
            pl.BlockSpec((ngrp, 2 * BLK, 4 * BLK), lambda g: (nt - 1 - g, 0, 0)),
            pl.BlockSpec((ngrp, 8, 4 * BLK), lambda g: (nt - 1 - g, 0, 0)),
            row(Q_W), row(SGU_W),
            pl.BlockSpec((1, SGU_W), _CONST2), pl.BlockSpec((1, SGU_W), _CONST2),
            pl.BlockSpec((4, BLK, BLK), _CONST3), pl.BlockSpec((4, BLK, BLK), _CONST3)],
        out_specs=[row(MAIN_W), pl.BlockSpec((8, MAIN_W), _CONST2), pl.BlockSpec((8, 128), _CONST2),
                   pl.BlockSpec((8, SGU_W), _CONST2), pl.BlockSpec((4, BLK, BLK), _CONST3),
                   pl.BlockSpec((4, BLK, BLK), _CONST3)],
        out_shape=[jax.ShapeDtypeStruct((t, MAIN_W), BF16), jax.ShapeDtypeStruct((8, MAIN_W), F32),
                   jax.ShapeDtypeStruct((8, 128), F32), jax.ShapeDtypeStruct((8, SGU_W), F32),
                   jax.ShapeDtypeStruct((4, BLK, BLK), F32), jax.ShapeDtypeStruct((4, BLK, BLK), F32)],
        scratch_shapes=[pltpu.VMEM((BLK, KV_W), F32), pltpu.VMEM((BLK, KV_W), F32)],
        compiler_params=_ARB,
    )(hm, hm, prob, psink, dya, dyb, vn_g, vn_b, w_s, bsb)


def _dx_inproj(dz, dhm, dhr, w_t, after, name, ln_in=None):
    t = dz.shape[0]

    def body(dz_ref, dhm_ref, dhr_ref, wt_ref, after_ref, *rest):
        dx = (ALPHA * dz_ref[...] + after_ref[0:1, 0:1] + _dot(dhm_ref[...], wt_ref[0:MAIN_W, :])
              + _dot(dhr_ref[...], wt_ref[MAIN_W:IN_COLS, :]))
        if ln_in is None:
            rest[0][...] = dx
            return
        x_ref, g_ref, gx_ref, acc_ref = rest

        @pl.when(pl.program_id(0) == 0)
        def _():
            acc_ref[...] = jnp.zeros_like(acc_ref)

        xhat, rstd = _ln_stats(x_ref[...])
        gx_ref[...] = _ln_bwd(dx * g_ref[...], xhat, rstd)
        acc_ref[0:1, :] += _colsum(dx * xhat)
        acc_ref[1:2, :] += _colsum(dx)

    row = lambda w: pl.BlockSpec((TM_MM, w), lambda i: (i, 0))
    in_specs = [row(D), row(MAIN_W), row(R_W), pl.BlockSpec((IN_COLS, D), _CONST2), pl.BlockSpec((8, 128), _CONST2)]
    if ln_in is None:
        return pl.pallas_call(
            body, name=name, grid=(t // TM_MM,), in_specs=in_specs,
            out_specs=row(D), out_shape=jax.ShapeDtypeStruct((t, D), F32), compiler_params=_ARB,
        )(dz, dhm, dhr, w_t, after)
    return pl.pallas_call(
        body, name=name, grid=(t // TM_MM,), in_specs=in_specs + [row(D), pl.BlockSpec((1, D), _CONST2)],
        out_specs=[row(D), pl.BlockSpec((8, D), _CONST2)],
        out_shape=[jax.ShapeDtypeStruct((t, D), F32), jax.ShapeDtypeStruct((8, D), F32)], compiler_params=_ARB,
    )(dz, dhm, dhr, w_t, after, *ln_in)


def _wgrad(a, b, tm, name, rows=None, under=None):
    t, m = a.shape
    n = b.shape[1]
    tk = min(t, 2048)
    nk = t // tk

    def body(a_ref, b_ref, *rest):
        o_ref, acc_ref = rest[-2:]
        k = pl.program_id(1)

        @pl.when(k == 0)
        def _():
            acc_ref[...] = jnp.zeros_like(acc_ref)

        acc_ref[...] += _dot_tn(a_ref[...].astype(BF16), b_ref[...].astype(BF16))

        @pl.when(k == nk - 1)
        def _():
            o_ref[...] = acc_ref[...].astype(BF16)

    in_specs = [pl.BlockSpec((tk, tm), lambda j, k: (k, j)), pl.BlockSpec((tk, n), lambda j, k: (k, 0))]
    if under is None:
        out_rows, out_spec, operands, aliases = rows or m, pl.BlockSpec((tm, n), lambda j, k: (j, 0)), (a, b), {}
    else:
        out_rows = under.shape[0]
        first = out_rows - m
        assert first % 128 == 0 and tm % 128 == 0
        out_spec = pl.BlockSpec((pl.Element(tm), pl.Element(n)),
                                lambda j, k: (pl.multiple_of(first + j * tm, 128), 0))
        in_specs, operands, aliases = in_specs + [_ANY], (a, b, under), {2: 0}
    return pl.pallas_call(
        body, name=name, grid=(m // tm, nk), in_specs=in_specs, out_specs=out_spec,
        out_shape=jax.ShapeDtypeStruct((out_rows, n), BF16), input_output_aliases=aliases,
        scratch_shapes=[pltpu.VMEM((tm, n), F32)],
        compiler_params=pltpu.CompilerParams(dimension_semantics=("arbitrary", "arbitrary"), vmem_limit_bytes=VMEM_LIMIT),
    )(*operands)


_ANY = pl.BlockSpec(memory_space=pl.ANY)


def _place():
    return lax.axis_index("x"), lax.axis_index("y"), lax.axis_index("c")


def _forward_sibling(lands, name):
    n = len(lands)

    def body(*refs):
        l_refs = refs[n:2 * n]
        send_sems, recv_sems = refs[2 * n:]
        x, y, c = _place()
        chips = [(1 - x, y), (x, 1 - y), (1 - x, 1 - y)]

        def copy(a, j, core):
            rows = l_refs[a].at[4 * chips[j][0] + 2 * chips[j][1] + core]
            return pltpu.make_async_remote_copy(
                src_ref=rows, dst_ref=rows, send_sem=send_sems.at[3 * a + j], recv_sem=recv_sems.at[3 * a + j],
                device_id=(x, y, 1 - c), device_id_type=MESH)

        for a in range(n):
            for j in range(3):
                copy(a, j, c).start()
        for a in range(n):
            for j in range(3):
                copy(a, j, 1 - c).wait_recv()
                copy(a, j, c).wait_send()

    return pl.pallas_call(
        body, name=name, in_specs=[_ANY] * n, out_specs=[_ANY] * n,
        out_shape=[jax.ShapeDtypeStruct(v.shape, v.dtype) for v in lands],
        input_output_aliases={a: a for a in range(n)},
        scratch_shapes=[pltpu.SemaphoreType.DMA((3 * n,)), pltpu.SemaphoreType.DMA((3 * n,))],
    )(*lands)


def _swap_sibling(gs, name):
    n = len(gs)
    first = [0]
    for v in gs:
        first.append(first[-1] + v.shape[0])

    def body(*refs):
        g_refs, r_refs = refs[:n], refs[n:2 * n]
        send_sems, recv_sems = refs[2 * n:]
        x, y, c = _place()
        cps = [pltpu.make_async_remote_copy(
            src_ref=g_refs[a].at[q, 1 - c], dst_ref=r_refs[a].at[q], send_sem=send_sems.at[first[a] + q],
            recv_sem=recv_sems.at[first[a] + q], device_id=(x, y, 1 - c), device_id_type=MESH)
            for a in range(n) for q in range(gs[a].shape[0])]
        for cp in cps:
            cp.start()
        for cp in cps:
            cp.wait()

    return pl.pallas_call(
        body, name=name, in_specs=[_ANY] * n, out_specs=[_ANY] * n,
        out_shape=[jax.ShapeDtypeStruct(v.shape[:1] + v.shape[2:], v.dtype) for v in gs],
        scratch_shapes=[pltpu.SemaphoreType.DMA((first[-1],)), pltpu.SemaphoreType.DMA((first[-1],))],
    )(*gs)


def _row_tile(rows, lanes, cap):
    if rows * lanes * 4 <= (1 << 20):
        return rows
    return max(d for d in range(8, cap + 1, 8) if rows % d == 0 and (d % 16 == 0 or rows % 16 != 0))


def _pair_sums(gs, rs, name):
    n = len(gs)

    def add(g, r, dtype):
        return (g.astype(F32) + r.astype(F32)).astype(dtype)

    def body(c_ref, *refs):
        g_refs, r_refs, o_refs = refs[:n], refs[n:2 * n], refs[2 * n:]
        o_refs[0][...] = add(g_refs[0][0], r_refs[0][...], o_refs[0].dtype)

        @pl.when(pl.program_id(0) == 0)
        def _():
            for a in range(1, n):
                o_refs[a][...] = add(g_refs[a][:, 0], r_refs[a][...], o_refs[a].dtype)

    def whole(shape, mine):
        if mine:
            return pl.BlockSpec(shape, lambda q, c_ref: (0, c_ref[0]) + (0,) * (len(shape) - 2))
        return pl.BlockSpec(shape, lambda q, c_ref: (0,) * len(shape))

    big = gs[0].shape
    return pl.pallas_call(
        body, name=name,
        grid_spec=pltpu.PrefetchScalarGridSpec(
            num_scalar_prefetch=1, grid=(big[0],),
            in_specs=[pl.BlockSpec((1, 1) + big[2:], lambda q, c_ref: (q, c_ref[0], 0, 0))]
            + [whole(g.shape[:1] + (1,) + g.shape[2:], True) for g in gs[1:]]
            + [pl.BlockSpec((1,) + big[2:], lambda q, c_ref: (q, 0, 0))]
            + [whole(r.shape, False) for r in rs[1:]],
            out_specs=[pl.BlockSpec((1,) + big[2:], lambda q, c_ref: (q, 0, 0))]
            + [whole(r.shape, False) for r in rs[1:]]),
        out_shape=[jax.ShapeDtypeStruct(r.shape, g.dtype) for g, r in zip(gs, rs)],
        compiler_params=_ARB,
    )(lax.axis_index("c").astype(jnp.int32).reshape(1), *gs, *rs)


def _adamw(parts, w, m, v, name, own=None):
    nl = len(parts)
    ns, rows, l = parts[0].shape
    tr = _row_tile(rows, l * ns, 304)
    nt = rows // tr
    c1 = 1.0 - ADAM_B1 ** ADAM_STEP
    c2 = 1.0 - ADAM_B2 ** ADAM_STEP

    def body(q_ref, *refs):
        own_refs = refs[:nl] if own is not None else None
        p_refs = refs[-7 - nl:-7]
        w_ref, m_ref, v_ref, g_ref, d_ref, nm_ref, nv_ref = refs[-7:]
        layer = pl.program_id(0)
        g = None
        for j in range(nl):
            gj = None
            for k in range(ns):
                term = p_refs[j][k].astype(F32)
                if own_refs is not None:
                    term = jnp.where(q_ref[0] == k, own_refs[j][0].astype(F32), term)
                gj = term if gj is None else gj + term
            g = gj if g is None else jnp.where(layer == j, gj, g)
        g_ref[...] = g
        nm = ADAM_B1 * m_ref[...] + (1.0 - ADAM_B1) * g
        nv = ADAM_B2 * v_ref[...] + (1.0 - ADAM_B2) * (g * g)
        nm_ref[...] = nm
        nv_ref[...] = nv
        d_ref[...] = -ADAM_LR * ((nm / c1) / (jnp.sqrt(nv / c2) + ADAM_EPS) + ADAM_WD * w_ref[...])

    def tile_of(j):
        return lambda la, i, q: jnp.where(la == j, i, jnp.where(la < j, 0, nt - 1))

    row = pl.BlockSpec((tr, l), lambda la, i, q: (la * nt + i, 0))
    own_specs = [] if own is None else [
        pl.BlockSpec((1, tr, l), lambda la, i, q, j=j: (q[0], tile_of(j)(la, i, q), 0)) for j in range(nl)]
    part_specs = [pl.BlockSpec((ns, tr, l), lambda la, i, q, j=j: (0, tile_of(j)(la, i, q), 0)) for j in range(nl)]
    chip = (2 * lax.axis_index("x") + lax.axis_index("y")).astype(jnp.int32).reshape(1)
    return pl.pallas_call(
        body, name=name,
        grid_spec=pltpu.PrefetchScalarGridSpec(
            num_scalar_prefetch=1, grid=(nl, nt),
            in_specs=own_specs + part_specs + [row, row, row], out_specs=[row] * 4),
        out_shape=[jax.ShapeDtypeStruct((nl * rows, l), F32)] * 4,
        compiler_params=pltpu.CompilerParams(dimension_semantics=("arbitrary", "arbitrary"), vmem_limit_bytes=VMEM_LIMIT),
    )(chip, *([] if own is None else own), *parts, w, m, v)


_HBM = pl.BlockSpec(memory_space=pltpu.HBM)
_SEM = pl.BlockSpec(memory_space=pltpu.SEMAPHORE)
_EFFECT = pltpu.SideEffectType.DATAFLOW_SIDE_EFFECTING


def _plan_all(x, y, c):
    me = 4 * x + 2 * y + c
    peers = [(x, y, 1 - c), (1 - x, y, c), (x, 1 - y, c), (1 - x, 1 - y, c),
             (1 - x, y, 1 - c), (x, 1 - y, 1 - c), (1 - x, 1 - y, 1 - c)]
    return [(None, me, p, 4 * p[0] + 2 * p[1] + p[2]) for p in peers]


def _plan_near(x, y, c):
    me = 4 * x + 2 * y + c
    peers = [(x, y, 1 - c), (1 - x, y, c), (x, 1 - y, c), (1 - x, 1 - y, c)]
    return [(None, me, p, 4 * p[0] + 2 * p[1] + p[2]) for p in peers]


def _plan_sibling(x, y, c):
    return [(2 * q + 1 - c, q, (x, y, 1 - c), q) for q in range(4)]


def _plan_chips(x, y, c):
    me = 2 * x + y
    return [(2 * qx + qy, me, (qx, qy, c), 2 * qx + qy) for qx, qy in ((1 - x, y), (x, 1 - y), (1 - x, 1 - y))]


def _split_copies(plan, src_refs, land_refs, send_sems, recv_sems, arrival):
    n = len(src_refs)
    entries = plan(*_place())
    per = len(entries)
    cps = []
    for a in range(n):
        for k, (src_slot, dst_slot, peer, back_slot) in enumerate(entries):
            src = src_refs[a] if src_slot is None else src_refs[a].at[src_slot]
            cps.append(pltpu.make_async_remote_copy(
                src_ref=src, dst_ref=land_refs[a].at[back_slot if arrival else dst_slot],
                send_sem=send_sems.at[per * a + k], recv_sem=recv_sems.at[per * a + k],
                device_id=peer, device_id_type=MESH))
    return cps


def _split_start(srcs, lands, plan, per, name):
    n = len(srcs)

    def body(*refs):
        for cp in _split_copies(plan, refs[:n], refs[n:2 * n], refs[2 * n], refs[2 * n + 1], False):
            cp.start()
        refs[-1][...] = jnp.zeros_like(refs[-1])

    both = list(srcs) + list(lands)
    outs = pl.pallas_call(
        body, name=name,
        out_shape=(pltpu.SemaphoreType.DMA((per * n,)), pltpu.SemaphoreType.DMA((per * n,)),
                   *[pltpu.HBM(v.shape, v.dtype) for v in both], jax.ShapeDtypeStruct((8, 128), F32)),
        in_specs=[_HBM] * (2 * n),
        out_specs=(_SEM, _SEM, *[_HBM] * (2 * n), pl.BlockSpec(memory_space=pltpu.VMEM)),
        input_output_aliases={i: 2 + i for i in range(2 * n)},
        compiler_params=pltpu.CompilerParams(has_side_effects=_EFFECT),
    )(*[pltpu.with_memory_space_constraint(v, pltpu.HBM) for v in both])
    return outs[0], outs[1], list(outs[2:2 + 2 * n]), outs[-1]


def _split_wait(send_sems, recv_sems, thru, plan, after, name):
    n = len(thru) // 2

    def body(*refs):
        for cp in _split_copies(plan, refs[:n], refs[n:2 * n], refs[2 * n], refs[2 * n + 1], True):
            cp.wait_send()
            cp.wait_recv()

    outs = pl.pallas_call(
        body, name=name, out_shape=tuple(pltpu.HBM(v.shape, v.dtype) for v in thru),
        in_specs=[_HBM] * (2 * n) + [_SEM, _SEM, pl.BlockSpec(memory_space=pl.ANY)],
        out_specs=[_HBM] * (2 * n), input_output_aliases={i: i for i in range(2 * n)},
        compiler_params=pltpu.CompilerParams(has_side_effects=_EFFECT),
    )(*thru, send_sems, recv_sems, after)
    return list(outs[:n]), list(outs[n:])


_SMALL_IN = ("ln_in_g", "ln_in_b")
_SMALL = ("w_s", "b_in", "sinks", "vn_g", "vn_b", "b_s", "b_out", "ln_g", "ln_b")


def _rows128(a):
    flat = a.reshape(-1)
    return jnp.pad(flat, (0, (-flat.shape[0]) % 128)).reshape(-1, 128)


def _pack_small(d, names):
    rows = jnp.concatenate([_rows128(d[n]) for n in names])
    return jnp.pad(rows, ((0, (-rows.shape[0]) % 8), (0, 0)))


def _unpack_small(p, like, names):
    off, out = 0, {}
    for n in names:
        size = like[n].size
        rows = -(-size // 128)
        out[n] = p[off:off + rows].reshape(-1)[:size].reshape(like[n].shape)
        off += rows
    return out


def _owner_blocks(g, axis):
    sh = g.shape
    g = g.reshape(sh[:axis] + (4, 2, sh[axis] // N_DEV) + sh[axis + 1:])
    return jnp.moveaxis(g, (axis, axis + 1), (0, 1))


def kernel(x, ln_in_g, ln_in_b, w_in, b_in, sinks, vn_g, vn_b, w_s, b_s, p_a, p_b, w_out, b_out, ln_g, ln_b, loss_target, m_ln_in_g, m_ln_in_b, m_w_in, m_b_in, m_sinks, m_vn_g, m_vn_b, m_w_s, m_b_s, m_p_a, m_p_b, m_w_out, m_b_out, m_ln_g, m_ln_b, v_ln_in_g, v_ln_in_b, v_w_in, v_b_in, v_sinks, v_vn_g, v_vn_b, v_w_s, v_b_s, v_p_a, v_p_b, v_w_out, v_b_out, v_ln_g, v_ln_b):
    nseq, seq, _ = x.shape
    t = nseq * seq
    nblk_seq = seq // BLK
    x2 = x.reshape(t, D)
    tgt = loss_target.reshape(t, D)

    def turned(a):
        return jnp.swapaxes(a, 1, 2)

    w_in_t = turned(w_in)

    def blocks(l):
        return [w_in_t[l].astype(BF16), p_a[l].astype(BF16), p_b[l].astype(BF16), w_out[l].astype(BF16)]

    def full_weights(g):
        w_t_full = g[0].reshape(IN_COLS, D)
        pa_full = jnp.moveaxis(g[1], 0, 1).reshape(Q_W, D)
        pb_full = jnp.moveaxis(g[2], 0, 1).reshape(SGU_W, D)
        wo_full = g[3].reshape(D, D)
        return dict(w_t=w_t_full, pa=pa_full, pb=pb_full, wo=wo_full)

    def landing(bs):
        return [lax.empty((N_DEV,) + v.shape, v.dtype) for v in bs]

    def with_own(landed, sent):
        return [lax.dynamic_update_index_in_dim(g, b, me, 0) for g, b in zip(landed, sent)]

    me = 4 * lax.axis_index("x") + 2 * lax.axis_index("y") + lax.axis_index("c")
    blocks0 = blocks(0)
    a_send, a_recv, a_thru, a_token = _split_start(blocks0[:1], landing(blocks0[:1]), _plan_near, 4,
                                                   "allgather_w_in0_start")
    rest0 = [b + a_token[0, 0].astype(BF16) for b in blocks0[1:]]
    b_send, b_recv, b_thru, b_token = _split_start(rest0, landing(rest0), _plan_all, 7, "allgather_rest0_start")
    xs = [_ln_fwd(x2, ln_in_g + b_token[0, 0], ln_in_b, "ln_in_fwd")]
    sent, landed = _split_wait(a_send, a_recv, a_thru, _plan_near, xs[0], "allgather_w_in0_wait")
    gathered0 = with_own(_forward_sibling(landed, "allgather_w_in0_forward"), sent)
    blocks1, gathered0 = lax.optimization_barrier((blocks(1), gathered0))
    ag_send, ag_recv, ag_thru, ag_token = _split_start(blocks1, landing(blocks1), _plan_all, 7,
                                                       "allgather_weights1_start")
    weights = [None, None]
    bsb = jnp.broadcast_to(b_s[:, :, :, None], (DEPTH, 4, BLK, BLK))
    bias = _band_bias()

    saved = []
    for l in range(DEPTH):
        if l == 1:
            sent, landed = _split_wait(ag_send, ag_recv, ag_thru, _plan_all, xs[1], "allgather_weights1_wait")
            weights[1] = full_weights(with_own(landed, sent))
        w_t = weights[l]["w_t"] if l else gathered0[0].reshape(IN_COLS, D)
        last = l == DEPTH - 1
        b_l = b_in[l].reshape(1, -1) + (ag_token[0, 0] if l == 0 else 0.0)
        hm, hr = _inproj(xs[l], w_t, b_l, f"inproj{l}")
        ya, yb, prob, psink = _mixer_fwd(hm, sinks[l], bias, vn_g[l].reshape(1, -1), vn_b[l].reshape(1, -1),
                                         w_s[l], bsb[l], nblk_seq, f"mixer_fwd{l}")
        if l == 0:
            sent, landed = _split_wait(b_send, b_recv, b_thru, _plan_all, ya, "allgather_rest0_wait")
            weights[0] = full_weights(gathered0 + with_own(landed, sent))
        wl = weights[l]
        outs = _tail_fwd(xs[l], ya, yb, hr, wl["pa"], wl["pb"], wl["wo"], b_out[l].reshape(1, D),
                         ln_g[l].reshape(1, D), ln_b[l].reshape(1, D), f"tail_fwd{l}", last)
        saved.append((hm, hr, ya, yb, prob, psink) + tuple(outs[:4]))
        if not last:
            xs.append(outs[4])

    small = {n: [None] * DEPTH for n in _SMALL}
    names = ("w_in", "p_a", "p_b", "w_out")
    owner_axis = {"w_in": 0, "p_a": 1, "p_b": 1, "w_out": 0}
    token = jnp.zeros((8, 128), F32)
    dx = tgt
    split = [None] * DEPTH
    for l in reversed(range(DEPTH)):
        hm, hr, ya, yb, prob, psink, pa, pb, merged, z = saved[l]
        wl = weights[l]
        dz, dpa, dpb, dhr, dya, dyb, acc, gbr = _tail_bwd(
            dx, z, pa, pb, hr, wl["wo"], wl["pa"], wl["pb"], ln_g[l].reshape(1, D) + token[0, 0],
            ln_b[l].reshape(1, D), f"tail_bwd{l}", l == DEPTH - 1)
        if l == DEPTH - 1:
            sq_err = acc[3:4, 0:128]
        dhm, gbm, gsk, gvn, gws, gbs = _mixer_bwd(
            hm, dya, dyb, prob, psink, vn_g[l].reshape(1, -1), vn_b[l].reshape(1, -1), w_s[l], bsb[l],
            f"mixer_bwd{l}")
        grads = {"w_in": _wgrad(dhr, xs[l], R_W // 2, f"wgrad_in_route{l}",
                                under=_wgrad(dhm, xs[l], MAIN_W // 2, f"wgrad_in_main{l}", rows=IN_COLS)),
                 "p_a": _wgrad(ya, dpa, Q_W, f"wgrad_pa{l}"), "p_b": _wgrad(yb, dpb, SGU_W, f"wgrad_pb{l}"),
                 "w_out": _wgrad(merged, dz, D, f"wgrad_out{l}")}
        small["b_in"][l] = jnp.concatenate([gbm[0], gbr[0]])
        small["sinks"][l] = gsk[:, 0]
        small["vn_g"][l], small["vn_b"][l] = gvn[0], gvn[1]
        small["w_s"][l], small["b_s"][l] = gws, gbs[:, :, 0]
        small["ln_g"][l], small["ln_b"][l], small["b_out"][l] = acc[0], acc[1], acc[2]
        parts = [_owner_blocks(grads[n], owner_axis[n]) for n in names]
        if l == 0:
            packed = _pack_small({n: jnp.stack(v) for n, v in small.items()}, _SMALL)
            parts.append(jnp.broadcast_to(packed[None, None], (1, 2) + packed.shape))
        if l == 0:
            from_sib = _swap_sibling(parts, f"rs_sibling{l}")
        else:
            halves = [p.reshape((N_DEV,) + p.shape[2:]) for p in parts]
            sib = _split_start(halves, [lax.empty((4,) + p.shape[2:], p.dtype) for p in parts], _plan_sibling, 4,
                               f"rs_sibling{l}_start")
            dx = _dx_inproj(dz, dhm, dhr, wl["w_t"], sib[3], f"dx_inproj{l}")
            halves, from_sib = _split_wait(sib[0], sib[1], sib[2], _plan_sibling, dx, f"rs_sibling{l}_wait")
            parts = [h.reshape(p.shape) for h, p in zip(halves, parts)]
        pair = list(_pair_sums(parts, from_sib, f"pair_sums{l}"))
        if l == 0:
            pair[4] = jnp.broadcast_to(pair[4], (4,) + packed.shape)
        lands = [jnp.zeros(p.shape, p.dtype) for p in pair]
        split[l] = _split_start(pair, lands, _plan_chips, 3, f"rs_chips{l}_start")
        token = split[l][3]
        if l == 0:
            grad_x, acc_in = _dx_inproj(dz, dhm, dhr, wl["w_t"], token, f"dx_inproj{l}",
                                        ln_in=(x2, ln_in_g.reshape(1, D)))
    last = [acc_in, jnp.broadcast_to(sq_err, (8, 128))]
    ln_send, ln_recv, ln_thru, ln_token = _split_start(last, landing(last), _plan_all, 7, "allgather_ln_in_start")

    given = {"w_in": (w_in_t, turned(m_w_in), turned(v_w_in)), "p_a": (p_a, m_p_a, v_p_a),
             "p_b": (p_b, m_p_b, v_p_b), "w_out": (w_out, m_w_out, v_w_out)}
    waited = [_split_wait(split[l][0], split[l][1], split[l][2], _plan_chips, ln_token, f"rs_chips{l}_wait")
              for l in range(DEPTH)]
    res = {}
    for a, n in enumerate(names):
        rows, lanes = waited[0][1][a].shape[1:]
        outs = _adamw([waited[l][1][a] for l in range(DEPTH)], *[v.reshape(DEPTH * rows, lanes) for v in given[n]],
                      f"adamw_{n}", own=[waited[l][0][a] for l in range(DEPTH)])
        res[n] = [o.reshape(given[n][0].shape) for o in outs]
    res["w_in"] = [turned(o) for o in res["w_in"]]

    w_small = dict(ln_in_g=ln_in_g, ln_in_b=ln_in_b, b_in=b_in, sinks=sinks, vn_g=vn_g, vn_b=vn_b, w_s=w_s, b_s=b_s,
                   b_out=b_out, ln_g=ln_g, ln_b=ln_b)
    m_small = dict(ln_in_g=m_ln_in_g, ln_in_b=m_ln_in_b, b_in=m_b_in, sinks=m_sinks, vn_g=m_vn_g, vn_b=m_vn_b,
                   w_s=m_w_s, b_s=m_b_s, b_out=m_b_out, ln_g=m_ln_g, ln_b=m_ln_b)
    v_small = dict(ln_in_g=v_ln_in_g, ln_in_b=v_ln_in_b, b_in=v_b_in, sinks=v_sinks, vn_g=v_vn_g, vn_b=v_vn_b,
                   w_s=v_w_s, b_s=v_b_s, b_out=v_b_out, ln_g=v_ln_g, ln_b=v_ln_b)
    outs = _adamw([waited[0][1][4]], *[_pack_small(d, _SMALL) for d in (w_small, m_small, v_small)], "adamw_small",
                  own=[waited[0][0][4]])
    sent, landed = _split_wait(ln_send, ln_recv, ln_thru, _plan_all, outs[0], "allgather_ln_in_wait")
    all_in, all_sq = with_own(landed, sent)
    loss = jnp.sum(all_sq[:, 0, 0]) * (0.5 / D)
    outs_in = _adamw([all_in], *[jnp.pad(jnp.stack([d[n] for n in _SMALL_IN]), ((0, 6), (0, 0)))
                                 for d in (w_small, m_small, v_small)], "adamw_ln_in")
    for k in range(4):
        u = _unpack_small(outs[k], w_small, _SMALL)
        u.update({n: outs_in[k][r] for r, n in enumerate(_SMALL_IN)})
        for n in u:
            res.setdefault(n, [None] * 4)[k] = u[n]

    order = ("ln_in_g", "ln_in_b", "w_in", "b_in", "sinks", "vn_g", "vn_b", "w_s", "b_s", "p_a", "p_b", "w_out",
             "b_out", "ln_g", "ln_b")
    return (loss, grad_x.reshape(x.shape), *[res[n][0] for n in order], *[res[n][1] for n in order],
            *[res[n][2] for n in order], *[res[n][3] for n in order])
```

```python
import jax
import jax.numpy as jnp
from jax import lax
from jax.experimental import pallas as pl
from jax.experimental.pallas import tpu as pltpu

F32 = jnp.float32
BF16 = jnp.bfloat16

D = 1024
BLK = 128
N_KV = 2
Q_W, KV_W, SGU_W = 512, 128, 512
C_Q, C_K, C_V, C_GA, C_UB, C_VB, C_GB = 0, 512, 640, 768, 1280, 1792, 2304
MAIN_W = 2816
R_W = 2048
IN_COLS = MAIN_W + R_W
N_DEV = 8

DEPTH = 2
ALPHA = (2.0 * DEPTH) ** 0.25
LN_EPS = 1e-5
ATTN_SCALE = 0.125
NEG = float(jnp.finfo(jnp.float32).min)

ADAM_LR, ADAM_B1, ADAM_B2, ADAM_EPS, ADAM_WD, ADAM_STEP = 0.001, 0.9, 0.999, 1e-08, 0.01, 10

TM = 512
TM_EW = 1024
TM_MM = 512
TM_T = 512
NB = TM // BLK
MESH = pl.DeviceIdType.MESH
VMEM_LIMIT = 56 * 1024 * 1024

_ARB = pltpu.CompilerParams(dimension_semantics=("arbitrary",), vmem_limit_bytes=VMEM_LIMIT)


def _sigmoid(x):
    return 1.0 / (1.0 + jnp.exp(-x))


_GELU_C = 0.7978845608028654
_GELU_A = 0.044715


def _gelu_parts(x):
    x2 = x * x
    t = jnp.tanh(x * (_GELU_C + (_GELU_C * _GELU_A) * x2))
    hx = 0.5 * x
    return hx, t, x2


def _gelu(x):
    hx, t, _ = _gelu_parts(x)
    return hx + hx * t


def _gelu_and_grad(x):
    hx, t, x2 = _gelu_parts(x)
    grad = 0.5 + 0.5 * t + (hx - hx * (t * t)) * (_GELU_C + (3.0 * _GELU_C * _GELU_A) * x2)
    return hx + hx * t, grad


def _ln_stats(x):
    mu = jnp.mean(x, axis=-1, keepdims=True)
    xc = x - mu
    var = jnp.mean(xc * xc, axis=-1, keepdims=True)
    rstd = lax.rsqrt(var + LN_EPS)
    return xc * rstd, rstd


def _ln_bwd(dy_g, xhat, rstd):
    m1 = jnp.mean(dy_g, axis=-1, keepdims=True)
    m2 = jnp.mean(dy_g * xhat, axis=-1, keepdims=True)
    return rstd * (dy_g - m1 - xhat * m2)


def _colsum(x):
    return jnp.sum(x, axis=0, keepdims=True)


def _dot(a, b):
    return jnp.dot(a, b, preferred_element_type=F32)


def _dot_nt(a, b):
    return lax.dot_general(a, b, (((1,), (1,)), ((), ())), preferred_element_type=F32)


def _dot_tn(a, b):
    return lax.dot_general(a, b, (((0,), (0,)), ((), ())), preferred_element_type=F32)


def _head_place(hk, g):
    j = 4 * hk + g
    return j, j // 2, j % 2


def _head_rows(x, hk):
    d = lax.broadcasted_iota(jnp.int32, x.shape, 0)
    return jnp.where((d >= 64 * hk) & (d < 64 * hk + 64), x, 0.0).astype(BF16)


def _head_lanes(x, hk):
    d = lax.broadcasted_iota(jnp.int32, x.shape, 1)
    return jnp.where((d >= 64 * hk) & (d < 64 * hk + 64), x, 0.0)


def _band_bias():
    kpos = lax.broadcasted_iota(jnp.int32, (2 * BLK, 4 * BLK), 0)
    row = lax.broadcasted_iota(jnp.int32, (2 * BLK, 4 * BLK), 1) & (BLK - 1)
    band = (kpos > row) & (kpos <= row + BLK)
    return jnp.stack([jnp.where(band, 0.0, NEG), jnp.where(band & (kpos >= BLK), 0.0, NEG)]).astype(F32)


def _stack_q(q, hk):
    parts = []
    for g in range(4):
        _, p, pos = _head_place(hk, g)
        qp = q[:, BLK * p:BLK * (p + 1)] * ATTN_SCALE
        if pos != hk:
            qp = pltpu.roll(qp, 64, 1)
        parts.append(qp.astype(BF16))
    return jnp.concatenate(parts, axis=0)


def _attn_probs(q4, kh, hk, sinks_ref, bias):
    s_t = _dot_nt(kh, q4) + bias
    sink_row = jnp.concatenate(
        [jnp.full((1, BLK), sinks_ref[4 * hk + g], F32) for g in range(4)], axis=1)
    m = jnp.maximum(jnp.max(s_t, axis=0, keepdims=True), sink_row)
    p_un = jnp.exp(s_t - m)
    e_sink = jnp.exp(sink_row - m)
    inv = 1.0 / (jnp.sum(p_un, axis=0, keepdims=True) + e_sink)
    return (p_un * inv).astype(BF16), e_sink * inv


def _unstack_heads(x4, hk, pairs):
    for g in range(4):
        _, p, pos = _head_place(hk, g)
        xg = x4[BLK * g:BLK * (g + 1)]
        if pos != hk:
            xg = pltpu.roll(xg, 64, 1)
        pairs[p] = xg if pairs[p] is None else pairs[p] + xg
    return pairs


def _attn_fwd(q, kband, vband, sinks_ref, bias, save):
    pairs = [None] * 4
    vband_t = vband.T
    for hk in range(N_KV):
        prob_t, p_sink = _attn_probs(_stack_q(q, hk), _head_lanes(kband, hk).astype(BF16), hk, sinks_ref, bias)
        save(hk, prob_t, p_sink)
        o_t = _dot(_head_rows(vband_t, hk), prob_t)
        pairs = _unstack_heads(o_t.T, hk, pairs)
    return jnp.concatenate(pairs, axis=1)


def _tril_mask():
    r = lax.broadcasted_iota(jnp.int32, (BLK, BLK), 0)
    c = lax.broadcasted_iota(jnp.int32, (BLK, BLK), 1)
    return c <= r


def _sgu_fwd(u, v, vn_g, vn_b, wt, bsb_ref):
    vhat, rstd = _ln_stats(v)
    vn = vhat * vn_g + vn_b
    mixed = jnp.concatenate(
        [_dot(wt[g], vn[:, BLK * g:BLK * (g + 1)].astype(BF16)) + bsb_ref[g] for g in range(4)], axis=1)
    return vhat, rstd, vn, mixed


def _cols(ref, rows, col, width):
    return ref[rows, col:col + width].astype(F32)


def _band(hm_ref, hprev_ref, s, col):
    r0 = s * BLK
    cur = hm_ref[r0:r0 + BLK, col:col + KV_W]
    if s == 0:
        off = 0 if col == C_K else KV_W
        prev = hprev_ref[:, off:off + KV_W]
    else:
        prev = hm_ref[r0 - BLK:r0, col:col + KV_W]
    return jnp.concatenate([prev, cur], axis=0).astype(F32)


def _h_main_specs(nt, rev):
    def tile(g):
        return nt - 1 - g if rev else g

    return [pl.BlockSpec((TM, MAIN_W), lambda g: (tile(g), 0)),
            pl.BlockSpec((BLK, 2 * KV_W), lambda g: (jnp.maximum(tile(g) * NB - 1, 0), 2))]


_CONST2 = lambda g: (0, 0)
_CONST3 = lambda g: (0, 0, 0)


def _ln_fwd(x, g, b, name):
    t = x.shape[0]

    def body(x_ref, g_ref, b_ref, o_ref):
        xhat, _ = _ln_stats(x_ref[...])
        o_ref[...] = xhat * g_ref[...] + b_ref[...]

    return pl.pallas_call(
        body, name=name, grid=(t // TM_EW,),
        in_specs=[pl.BlockSpec((TM_EW, D), lambda i: (i, 0)), pl.BlockSpec((1, D), _CONST2),
                  pl.BlockSpec((1, D), _CONST2)],
        out_specs=pl.BlockSpec((TM_EW, D), lambda i: (i, 0)),
        out_shape=jax.ShapeDtypeStruct((t, D), F32), compiler_params=_ARB,
    )(x, g.reshape(1, D), b.reshape(1, D))


def _inproj(x, w_t, b, name):
    t = x.shape[0]

    def body(x_ref, wt_ref, b_ref, hm_ref, hr_ref):
        xb = x_ref[...].astype(BF16)
        hm_ref[...] = (_dot_nt(xb, wt_ref[0:MAIN_W, :]) + b_ref[:, 0:MAIN_W]).astype(BF16)
        hr_ref[...] = (_dot_nt(xb, wt_ref[MAIN_W:IN_COLS, :]) + b_ref[:, MAIN_W:IN_COLS]).astype(BF16)

    return pl.pallas_call(
        body, name=name, grid=(t // TM_MM,),
        in_specs=[pl.BlockSpec((TM_MM, D), lambda i: (i, 0)),
                  pl.BlockSpec((IN_COLS, D), _CONST2), pl.BlockSpec((1, IN_COLS), _CONST2)],
        out_specs=[pl.BlockSpec((TM_MM, MAIN_W), lambda i: (i, 0)), pl.BlockSpec((TM_MM, R_W), lambda i: (i, 0))],
        out_shape=[jax.ShapeDtypeStruct((t, MAIN_W), BF16), jax.ShapeDtypeStruct((t, R_W), BF16)],
        compiler_params=_ARB,
    )(x, w_t, b)


def _mixer_fwd(hm, sinks, bias, vn_g, vn_b, w_s, bsb, nblk_seq, name):
    t = hm.shape[0]
    nt = t // TM

    def body(sinks_ref, hm_ref, hprev_ref, bias_ref, vng_ref, vnb_ref, ws_ref, bsb_ref,
             ya_ref, yb_ref, prob_ref, psink_ref):
        i = pl.program_id(0)
        tril = _tril_mask()
        wt = [jnp.where(tril, ws_ref[g], 0.0).astype(BF16) for g in range(4)]
        for s in range(NB):
            r0 = s * BLK
            rows = slice(r0, r0 + BLK)
            bias = bias_ref[jnp.where((i * NB + s) % nblk_seq == 0, 1, 0)]

            def save(hk, prob_t, p_sink, s=s):
                prob_ref[N_KV * s + hk] = prob_t
                psink_ref[N_KV * s + hk] = jnp.broadcast_to(p_sink, (8, 4 * BLK))

            attn = _attn_fwd(_cols(hm_ref, rows, C_Q, Q_W), _band(hm_ref, hprev_ref, s, C_K),
                             _band(hm_ref, hprev_ref, s, C_V), sinks_ref, bias, save)
            g_a = _cols(hm_ref, rows, C_GA, Q_W)
            ya_ref[rows, :] = (attn * (g_a * _sigmoid(g_a))).astype(BF16)
            u = _gelu(_cols(hm_ref, rows, C_UB, SGU_W))
            mixed = _sgu_fwd(u, _gelu(_cols(hm_ref, rows, C_VB, SGU_W)), vng_ref[...], vnb_ref[...], wt, bsb_ref)[-1]
            g_b = _cols(hm_ref, rows, C_GB, SGU_W)
            yb_ref[rows, :] = (u * mixed * (g_b * _sigmoid(g_b))).astype(BF16)

    ngrp = N_KV * NB
    return pl.pallas_call(
        body, name=name, grid=(nt,),
        in_specs=[pl.BlockSpec(memory_space=pltpu.SMEM)] + _h_main_specs(nt, False) + [
            pl.BlockSpec((2, 2 * BLK, 4 * BLK), _CONST3),
            pl.BlockSpec((1, SGU_W), _CONST2), pl.BlockSpec((1, SGU_W), _CONST2),
            pl.BlockSpec((4, BLK, BLK), _CONST3), pl.BlockSpec((4, BLK, BLK), _CONST3)],
        out_specs=[pl.BlockSpec((TM, Q_W), lambda i: (i, 0)), pl.BlockSpec((TM, SGU_W), lambda i: (i, 0)),
                   pl.BlockSpec((ngrp, 2 * BLK, 4 * BLK), lambda i: (i, 0, 0)),
                   pl.BlockSpec((ngrp, 8, 4 * BLK), lambda i: (i, 0, 0))],
        out_shape=[jax.ShapeDtypeStruct((t, Q_W), BF16), jax.ShapeDtypeStruct((t, SGU_W), BF16),
                   jax.ShapeDtypeStruct((nt * ngrp, 2 * BLK, 4 * BLK), BF16),
                   jax.ShapeDtypeStruct((nt * ngrp, 8, 4 * BLK), F32)],
        compiler_params=_ARB,
    )(sinks, hm, hm, bias, vn_g, vn_b, w_s, bsb)


def _tail_fwd(x, ya, yb, hr, pa_w, pb_w, wo, b_out, ln_g, ln_b, name, last):
    t = x.shape[0]

    def body(x_ref, ya_ref, yb_ref, hr_ref, paw_ref, pbw_ref, wo_ref, bo_ref, g_ref, b_ref,
             pa_ref, pb_ref, mg_ref, z_ref, *xn_ref):
        pa = _dot(ya_ref[...], paw_ref[...])
        pb = _dot(yb_ref[...], pbw_ref[...])
        pa_ref[...] = pa.astype(BF16)
        pb_ref[...] = pb.astype(BF16)
        everything = slice(None)
        merged = _sigmoid(_cols(hr_ref, everything, 0, D)) * pa + _sigmoid(_cols(hr_ref, everything, D, D)) * pb
        mb = merged.astype(BF16)
        mg_ref[...] = mb
        z = ALPHA * x_ref[...] + (_dot(mb, wo_ref[...]) + bo_ref[...])
        z_ref[...] = z
        if not last:
            zhat, _ = _ln_stats(z)
            xn_ref[0][...] = zhat * g_ref[...] + b_ref[...]

    row = lambda w: pl.BlockSpec((TM_T, w), lambda i: (i, 0))
    vec = pl.BlockSpec((1, D), _CONST2)
    n_f32 = 1 if last else 2
    return pl.pallas_call(
        body, name=name, grid=(t // TM_T,),
        in_specs=[row(D), row(Q_W), row(SGU_W), row(R_W),
                  pl.BlockSpec((Q_W, D), _CONST2), pl.BlockSpec((SGU_W, D), _CONST2), pl.BlockSpec((D, D), _CONST2),
                  vec, vec, vec],
        out_specs=[row(D)] * (3 + n_f32),
        out_shape=[jax.ShapeDtypeStruct((t, D), BF16)] * 3 + [jax.ShapeDtypeStruct((t, D), F32)] * n_f32,
        compiler_params=_ARB,
    )(x, ya, yb, hr, pa_w, pb_w, wo, b_out, ln_g, ln_b)


def _tail_bwd(dxn, z, pa, pb, hr, wo, pa_w, pb_w, ln_g, ln_b, name, from_loss):
    t = dxn.shape[0]

    def body(dxn_ref, z_ref, pa_ref, pb_ref, hr_ref, wo_ref, paw_ref, pbw_ref, g_ref, b_ref,
             dz_ref, dpa_ref, dpb_ref, dhr_ref, dya_ref, dyb_ref, acc_ref, gbr_ref):
        @pl.when(pl.program_id(0) == 0)
        def _():
            acc_ref[...] = jnp.zeros_like(acc_ref)
            gbr_ref[...] = jnp.zeros_like(gbr_ref)

        zhat, rstd = _ln_stats(z_ref[...])
        if from_loss:
            err = zhat * g_ref[...] + b_ref[...] - dxn_ref[...]
            dxn_v = err * (1.0 / D)
            sq = jnp.sum(jnp.sum(err * err, axis=1, keepdims=True), axis=0, keepdims=True)
            acc_ref[3:4, :] += jnp.broadcast_to(sq, (1, D))
        else:
            dxn_v = dxn_ref[...]
        dz = _ln_bwd(dxn_v * g_ref[...], zhat, rstd)
        dz_ref[...] = dz
        acc_ref[0:1, :] += _colsum(dxn_v * zhat)
        acc_ref[1:2, :] += _colsum(dxn_v)
        acc_ref[2:3, :] += _colsum(dz)
        dmerged = _dot_nt(dz.astype(BF16), wo_ref[...])
        everything = slice(None)
        sa = _sigmoid(_cols(hr_ref, everything, 0, D))
        sb = _sigmoid(_cols(hr_ref, everything, D, D))
        dpa = (dmerged * sa).astype(BF16)
        dpb = (dmerged * sb).astype(BF16)
        dpa_ref[...] = dpa
        dpb_ref[...] = dpb
        dra = dmerged * pa_ref[...].astype(F32) * (sa * (1.0 - sa))
        drb = dmerged * pb_ref[...].astype(F32) * (sb * (1.0 - sb))
        dhr_ref[:, 0:D] = dra.astype(BF16)
        dhr_ref[:, D:2 * D] = drb.astype(BF16)
        gbr_ref[0:1, 0:D] += _colsum(dra)
        gbr_ref[0:1, D:2 * D] += _colsum(drb)
        dya_ref[...] = _dot_nt(dpa, paw_ref[...]).astype(BF16)
        dyb_ref[...] = _dot_nt(dpb, pbw_ref[...]).astype(BF16)

    row = lambda w: pl.BlockSpec((TM_T, w), lambda i: (i, 0))
    vec = pl.BlockSpec((1, D), _CONST2)
    return pl.pallas_call(
        body, name=name, grid=(t // TM_T,),
        in_specs=[row(D), row(D), row(D), row(D), row(R_W),
                  pl.BlockSpec((D, D), _CONST2), pl.BlockSpec((Q_W, D), _CONST2), pl.BlockSpec((SGU_W, D), _CONST2),
                  vec, vec],
        out_specs=[row(D), row(D), row(D), row(R_W), row(Q_W), row(SGU_W), pl.BlockSpec((8, D), _CONST2),
                   pl.BlockSpec((8, R_W), _CONST2)],
        out_shape=[jax.ShapeDtypeStruct((t, D), F32), jax.ShapeDtypeStruct((t, D), BF16),
                   jax.ShapeDtypeStruct((t, D), BF16), jax.ShapeDtypeStruct((t, R_W), BF16),
                   jax.ShapeDtypeStruct((t, Q_W), BF16), jax.ShapeDtypeStruct((t, SGU_W), BF16),
                   jax.ShapeDtypeStruct((8, D), F32), jax.ShapeDtypeStruct((8, R_W), F32)],
        compiler_params=_ARB,
    )(dxn, z, pa, pb, hr, wo, pa_w, pb_w, ln_g, ln_b)


def _mixer_bwd(hm, dya, dyb, prob, psink, vn_g, vn_b, w_s, bsb, name):
    t = hm.shape[0]
    nt = t // TM
    ngrp = N_KV * NB

    def body(hm_ref, hprev_ref, prob_ref, psink_ref, dya_ref, dyb_ref, vng_ref, vnb_ref, ws_ref, bsb_ref,
             dhm_ref, gbm_ref, gsk_ref, gvn_ref, gws_ref, gbs_ref, dk_carry, dv_carry):
        gi = pl.program_id(0)

        @pl.when(gi == 0)
        def _():
            for r in (gbm_ref, gsk_ref, gvn_ref, gws_ref, gbs_ref, dk_carry, dv_carry):
                r[...] = jnp.zeros_like(r)

        tril = _tril_mask()
        wt = [jnp.where(tril, ws_ref[g], 0.0).astype(BF16) for g in range(4)]
        vng = vng_ref[...]
        ones8 = jnp.ones((8, BLK), BF16)

        def put(rows, col, val):
            dhm_ref[rows, col:col + val.shape[1]] = val.astype(BF16)

        for s in reversed(range(NB)):
            r0 = s * BLK
            rows = slice(r0, r0 + BLK)
            q = _cols(hm_ref, rows, C_Q, Q_W)
            kband = _band(hm_ref, hprev_ref, s, C_K)
            vband = _band(hm_ref, hprev_ref, s, C_V)
            g_a = _cols(hm_ref, rows, C_GA, Q_W)
            sg = _sigmoid(g_a)
            dya_v = _cols(dya_ref, rows, 0, Q_W)
            d_o = dya_v * (g_a * sg)
            o_pairs, dq_pairs = [None] * 4, [None] * 4
            dkband = jnp.zeros((2 * BLK, KV_W), F32)
            dvband = jnp.zeros((2 * BLK, KV_W), F32)
            kband_t, vband_t = kband.T, vband.T
            for hk in range(N_KV):
                q4 = _stack_q(q, hk)
                prob_b = prob_ref[N_KV * s + hk]
                p_sink = psink_ref[N_KV * s + hk][0:1, :]
                o_t = _dot(_head_rows(vband_t, hk), prob_b)
                o_pairs = _unstack_heads(o_t.T, hk, o_pairs)
                parts = []
                for g in range(4):
                    _, p, pos = _head_place(hk, g)
                    dp = d_o[:, BLK * p:BLK * (p + 1)]
                    parts.append(pltpu.roll(dp, 64, 1) if pos != hk else dp)
                do4 = _head_lanes(jnp.concatenate(parts, axis=0), hk)
                do4b = do4.astype(BF16)
                delta = _colsum(do4.T * o_t)
                vh = _head_lanes(vband, hk).astype(BF16)
                ds_t = prob_b.astype(F32) * (_dot_nt(vh, do4b) - delta)
                dsb = ds_t.astype(BF16)
                dq4_t = _dot(_head_rows(kband_t, hk), dsb)
                dq_pairs = _unstack_heads(dq4_t.T * ATTN_SCALE, hk, dq_pairs)
                dkband = dkband + _head_lanes(_dot(dsb, q4), hk)
                dvband = dvband + _dot(prob_b, do4b)
                dsk = p_sink * delta
                for g in range(4):
                    j = 4 * hk + g
                    tot = jnp.sum(dsk[:, BLK * g:BLK * (g + 1)], axis=1, keepdims=True)
                    gsk_ref[j:j + 1, :] += jnp.broadcast_to(-tot, (1, 128))
            attn = jnp.concatenate(o_pairs, axis=1)
            put(rows, C_Q, jnp.concatenate(dq_pairs, axis=1))
            put(rows, C_K, dkband[BLK:2 * BLK] + dk_carry[...])
            put(rows, C_V, dvband[BLK:2 * BLK] + dv_carry[...])
            dk_carry[...] = dkband[0:BLK]
            dv_carry[...] = dvband[0:BLK]
            put(rows, C_GA, dya_v * attn * (sg * (1.0 + g_a * (1.0 - sg))))
            u, du_du_b = _gelu_and_grad(_cols(hm_ref, rows, C_UB, SGU_W))
            v, dv_dv_b = _gelu_and_grad(_cols(hm_ref, rows, C_VB, SGU_W))
            g_b = _cols(hm_ref, rows, C_GB, SGU_W)
            vhat, rstd, vn, mixed = _sgu_fwd(u, v, vng, vnb_ref[...], wt, bsb_ref)
            sgb = _sigmoid(g_b)
            silu_b = g_b * sgb
            dyb_v = _cols(dyb_ref, rows, 0, SGU_W)
            du = dyb_v * mixed * silu_b
            dmixed = dyb_v * u * silu_b
            put(rows, C_GB, dyb_v * u * mixed * (sgb * (1.0 + g_b * (1.0 - sgb))))
            dvn_parts = []
            for g in range(4):
                cols = slice(BLK * g, BLK * (g + 1))
                dmg = dmixed[:, cols]
                dmgb = dmg.astype(BF16)
                dvn_parts.append(_dot_tn(wt[g], dmgb))
                gws_ref[g] += jnp.where(tril, _dot_nt(dmgb, vn[:, cols].astype(BF16)), 0.0)
                gbs_ref[g] += dmg
            dvn = jnp.concatenate(dvn_parts, axis=1)
            gvn_ref[0:1, :] += _colsum(dvn * vhat)
            gvn_ref[1:2, :] += _colsum(dvn)
            dv = _ln_bwd(dvn * vng, vhat, rstd)
            put(rows, C_UB, du * du_du_b)
            put(rows, C_VB, dv * dv_dv_b)
            gbm_ref[...] += _dot(ones8, dhm_ref[rows, :])

        @pl.when(gi == nt - 1)
        def _():
            for g in range(4):
                gbs_ref[g] = jnp.broadcast_to(jnp.sum(gbs_ref[g], axis=1, keepdims=True), (BLK, BLK))

    row = lambda w: pl.BlockSpec((TM, w), lambda g: (nt - 1 - g, 0))
    return pl.pallas_call(
        body, name=name, grid=(nt,),
        in_specs=_h_main_specs(nt, True) + [
            pl.BlockSpec((ngrp, 2 * BLK, 4 * BLK), lambda g: (nt - 1 - g, 0, 0)),
            pl.BlockSpec((ngrp, 8, 4 * BLK), lambda g: (nt - 1 - g, 0, 0)),
            row(Q_W), row(SGU_W),
            pl.BlockSpec((1, SGU_W), _CONST2), pl.BlockSpec((1, SGU_W), _CONST2),
            pl.BlockSpec((4, BLK, BLK), _CONST3), pl.BlockSpec((4, BLK, BLK), _CONST3)],
        out_specs=[row(MAIN_W), pl.BlockSpec((8, MAIN_W), _CONST2), pl.BlockSpec((8, 128), _CONST2),
                   pl.BlockSpec((8, SGU_W), _CONST2), pl.BlockSpec((4, BLK, BLK), _CONST3),
                   pl.BlockSpec((4, BLK, BLK), _CONST3)],
        out_shape=[jax.ShapeDtypeStruct((t, MAIN_W), BF16), jax.ShapeDtypeStruct((8, MAIN_W), F32),
                   jax.ShapeDtypeStruct((8, 128), F32), jax.ShapeDtypeStruct((8, SGU_W), F32),
                   jax.ShapeDtypeStruct((4, BLK, BLK), F32), jax.ShapeDtypeStruct((4, BLK, BLK), F32)],
        scratch_shapes=[pltpu.VMEM((BLK, KV_W), F32), pltpu.VMEM((BLK, KV_W), F32)],
        compiler_params=_ARB,
    )(hm, hm, prob, psink, dya, dyb, vn_g, vn_b, w_s, bsb)


def _dx_inproj(dz, dhm, dhr, w_t, after, name, ln_in=None):
    t = dz.shape[0]

    def body(dz_ref, dhm_ref, dhr_ref, wt_ref, after_ref, *rest):
        dx = (ALPHA * dz_ref[...] + after_ref[0:1, 0:1] + _dot(dhm_ref[...], wt_ref[0:MAIN_W, :])
              + _dot(dhr_ref[...], wt_ref[MAIN_W:IN_COLS, :]))
        if ln_in is None:
            rest[0][...] = dx
            return
        x_ref, g_ref, gx_ref, acc_ref = rest

        @pl.when(pl.program_id(0) == 0)
        def _():
            acc_ref[...] = jnp.zeros_like(acc_ref)

        xhat, rstd = _ln_stats(x_ref[...])
        gx_ref[...] = _ln_bwd(dx * g_ref[...], xhat, rstd)
        acc_ref[0:1, :] += _colsum(dx * xhat)
        acc_ref[1:2, :] += _colsum(dx)

    row = lambda w: pl.BlockSpec((TM_MM, w), lambda i: (i, 0))
    in_specs = [row(D), row(MAIN_W), row(R_W), pl.BlockSpec((IN_COLS, D), _CONST2), pl.BlockSpec((8, 128), _CONST2)]
    if ln_in is None:
        return pl.pallas_call(
            body, name=name, grid=(t // TM_MM,), in_specs=in_specs,
            out_specs=row(D), out_shape=jax.ShapeDtypeStruct((t, D), F32), compiler_params=_ARB,
        )(dz, dhm, dhr, w_t, after)
    return pl.pallas_call(
        body, name=name, grid=(t // TM_MM,), in_specs=in_specs + [row(D), pl.BlockSpec((1, D), _CONST2)],
        out_specs=[row(D), pl.BlockSpec((8, D), _CONST2)],
        out_shape=[jax.ShapeDtypeStruct((t, D), F32), jax.ShapeDtypeStruct((8, D), F32)], compiler_params=_ARB,
    )(dz, dhm, dhr, w_t, after, *ln_in)


def _wgrad(a, b, tm, name, rows=None, under=None):
    t, m = a.shape
    n = b.shape[1]
    tk = min(t, 2048)
    nk = t // tk

    def body(a_ref, b_ref, *rest):
        o_ref, acc_ref = rest[-2:]
        k = pl.program_id(1)

        @pl.when(k == 0)
        def _():
            acc_ref[...] = jnp.zeros_like(acc_ref)

        acc_ref[...] += _dot_tn(a_ref[...].astype(BF16), b_ref[...].astype(BF16))

        @pl.when(k == nk - 1)
        def _():
            o_ref[...] = acc_ref[...].astype(BF16)

    in_specs = [pl.BlockSpec((tk, tm), lambda j, k: (k, j)), pl.BlockSpec((tk, n), lambda j, k: (k, 0))]
    if under is None:
        out_rows, out_spec, operands, aliases = rows or m, pl.BlockSpec((tm, n), lambda j, k: (j, 0)), (a, b), {}
    else:
        out_rows = under.shape[0]
        first = out_rows - m
        assert first % 128 == 0 and tm % 128 == 0
        out_spec = pl.BlockSpec((pl.Element(tm), pl.Element(n)),
                                lambda j, k: (pl.multiple_of(first + j * tm, 128), 0))
        in_specs, operands, aliases = in_specs + [_ANY], (a, b, under), {2: 0}
    return pl.pallas_call(
        body, name=name, grid=(m // tm, nk), in_specs=in_specs, out_specs=out_spec,
        out_shape=jax.ShapeDtypeStruct((out_rows, n), BF16), input_output_aliases=aliases,
        scratch_shapes=[pltpu.VMEM((tm, n), F32)],
        compiler_params=pltpu.CompilerParams(dimension_semantics=("arbitrary", "arbitrary"), vmem_limit_bytes=VMEM_LIMIT),
    )(*operands)


_ANY = pl.BlockSpec(memory_space=pl.ANY)


def _place():
    return lax.axis_index("x"), lax.axis_index("y"), lax.axis_index("c")


def _forward_sibling(lands, name):
    n = len(lands)

    def body(*refs):
        l_refs = refs[n:2 * n]
        send_sems, recv_sems = refs[2 * n:]
        x, y, c = _place()
        chips = [(1 - x, y), (x, 1 - y), (1 - x, 1 - y)]

        def copy(a, j, core):
            rows = l_refs[a].at[4 * chips[j][0] + 2 * chips[j][1] + core]
            return pltpu.make_async_remote_copy(
                src_ref=rows, dst_ref=rows, send_sem=send_sems.at[3 * a + j], recv_sem=recv_sems.at[3 * a + j],
                device_id=(x, y, 1 - c), device_id_type=MESH)

        for a in range(n):
            for j in range(3):
                copy(a, j, c).start()
        for a in range(n):
            for j in range(3):
                copy(a, j, 1 - c).wait_recv()
                copy(a, j, c).wait_send()

    return pl.pallas_call(
        body, name=name, in_specs=[_ANY] * n, out_specs=[_ANY] * n,
        out_shape=[jax.ShapeDtypeStruct(v.shape, v.dtype) for v in lands],
        input_output_aliases={a: a for a in range(n)},
        scratch_shapes=[pltpu.SemaphoreType.DMA((3 * n,)), pltpu.SemaphoreType.DMA((3 * n,))],
    )(*lands)


def _swap_sibling(gs, name):
    n = len(gs)
    first = [0]
    for v in gs:
        first.append(first[-1] + v.shape[0])

    def body(*refs):
        g_refs, r_refs = refs[:n], refs[n:2 * n]
        send_sems, recv_sems = refs[2 * n:]
        x, y, c = _place()
        cps = [pltpu.make_async_remote_copy(
            src_ref=g_refs[a].at[q, 1 - c], dst_ref=r_refs[a].at[q], send_sem=send_sems.at[first[a] + q],
            recv_sem=recv_sems.at[first[a] + q], device_id=(x, y, 1 - c), device_id_type=MESH)
            for a in range(n) for q in range(gs[a].shape[0])]
        for cp in cps:
            cp.start()
        for cp in cps:
            cp.wait()

    return pl.pallas_call(
        body, name=name, in_specs=[_ANY] * n, out_specs=[_ANY] * n,
        out_shape=[jax.ShapeDtypeStruct(v.shape[:1] + v.shape[2:], v.dtype) for v in gs],
        scratch_shapes=[pltpu.SemaphoreType.DMA((first[-1],)), pltpu.SemaphoreType.DMA((first[-1],))],
    )(*gs)


def _row_tile(rows, lanes, cap):
    if rows * lanes * 4 <= (1 << 20):
        return rows
    return max(d for d in range(8, cap + 1, 8) if rows % d == 0 and (d % 16 == 0 or rows % 16 != 0))


def _pair_sums(gs, rs, name):
    n = len(gs)

    def add(g, r, dtype):
        return (g.astype(F32) + r.astype(F32)).astype(dtype)

    def body(c_ref, *refs):
        g_refs, r_refs, o_refs = refs[:n], refs[n:2 * n], refs[2 * n:]
        o_refs[0][...] = add(g_refs[0][0], r_refs[0][...], o_refs[0].dtype)

        @pl.when(pl.program_id(0) == 0)
        def _():
            for a in range(1, n):
                o_refs[a][...] = add(g_refs[a][:, 0], r_refs[a][...], o_refs[a].dtype)

    def whole(shape, mine):
        if mine:
            return pl.BlockSpec(shape, lambda q, c_ref: (0, c_ref[0]) + (0,) * (len(shape) - 2))
        return pl.BlockSpec(shape, lambda q, c_ref: (0,) * len(shape))

    big = gs[0].shape
    return pl.pallas_call(
        body, name=name,
        grid_spec=pltpu.PrefetchScalarGridSpec(
            num_scalar_prefetch=1, grid=(big[0],),
            in_specs=[pl.BlockSpec((1, 1) + big[2:], lambda q, c_ref: (q, c_ref[0], 0, 0))]
            + [whole(g.shape[:1] + (1,) + g.shape[2:], True) for g in gs[1:]]
            + [pl.BlockSpec((1,) + big[2:], lambda q, c_ref: (q, 0, 0))]
            + [whole(r.shape, False) for r in rs[1:]],
            out_specs=[pl.BlockSpec((1,) + big[2:], lambda q, c_ref: (q, 0, 0))]
            + [whole(r.shape, False) for r in rs[1:]]),
        out_shape=[jax.ShapeDtypeStruct(r.shape, g.dtype) for g, r in zip(gs, rs)],
        compiler_params=_ARB,
    )(lax.axis_index("c").astype(jnp.int32).reshape(1), *gs, *rs)


def _adamw(parts, w, m, v, name, own=None):
    nl = len(parts)
    ns, rows, l = parts[0].shape
    tr = _row_tile(rows, l * ns, 304)
    nt = rows // tr
    c1 = 1.0 - ADAM_B1 ** ADAM_STEP
    c2 = 1.0 - ADAM_B2 ** ADAM_STEP

    def body(q_ref, *refs):
        own_refs = refs[:nl] if own is not None else None
        p_refs = refs[-7 - nl:-7]
        w_ref, m_ref, v_ref, g_ref, d_ref, nm_ref, nv_ref = refs[-7:]
        layer = pl.program_id(0)
        g = None
        for j in range(nl):
            gj = None
            for k in range(ns):
                term = p_refs[j][k].astype(F32)
                if own_refs is not None:
                    term = jnp.where(q_ref[0] == k, own_refs[j][0].astype(F32), term)
                gj = term if gj is None else gj + term
            g = gj if g is None else jnp.where(layer == j, gj, g)
        g_ref[...] = g
        nm = ADAM_B1 * m_ref[...] + (1.0 - ADAM_B1) * g
        nv = ADAM_B2 * v_ref[...] + (1.0 - ADAM_B2) * (g * g)
        nm_ref[...] = nm
        nv_ref[...] = nv
        d_ref[...] = -ADAM_LR * ((nm / c1) / (jnp.sqrt(nv / c2) + ADAM_EPS) + ADAM_WD * w_ref[...])

    def tile_of(j):
        return lambda la, i, q: jnp.where(la == j, i, jnp.where(la < j, 0, nt - 1))

    row = pl.BlockSpec((tr, l), lambda la, i, q: (la * nt + i, 0))
    own_specs = [] if own is None else [
        pl.BlockSpec((1, tr, l), lambda la, i, q, j=j: (q[0], tile_of(j)(la, i, q), 0)) for j in range(nl)]
    part_specs = [pl.BlockSpec((ns, tr, l), lambda la, i, q, j=j: (0, tile_of(j)(la, i, q), 0)) for j in range(nl)]
    chip = (2 * lax.axis_index("x") + lax.axis_index("y")).astype(jnp.int32).reshape(1)
    return pl.pallas_call(
        body, name=name,
        grid_spec=pltpu.PrefetchScalarGridSpec(
            num_scalar_prefetch=1, grid=(nl, nt),
            in_specs=own_specs + part_specs + [row, row, row], out_specs=[row] * 4),
        out_shape=[jax.ShapeDtypeStruct((nl * rows, l), F32)] * 4,
        compiler_params=pltpu.CompilerParams(dimension_semantics=("arbitrary", "arbitrary"), vmem_limit_bytes=VMEM_LIMIT),
    )(chip, *([] if own is None else own), *parts, w, m, v)


_HBM = pl.BlockSpec(memory_space=pltpu.HBM)
_SEM = pl.BlockSpec(memory_space=pltpu.SEMAPHORE)
_EFFECT = pltpu.SideEffectType.DATAFLOW_SIDE_EFFECTING


def _plan_all(x, y, c):
    me = 4 * x + 2 * y + c
    peers = [(x, y, 1 - c), (1 - x, y, c), (x, 1 - y, c), (1 - x, 1 - y, c),
             (1 - x, y, 1 - c), (x, 1 - y, 1 - c), (1 - x, 1 - y, 1 - c)]
    return [(None, me, p, 4 * p[0] + 2 * p[1] + p[2]) for p in peers]


def _plan_near(x, y, c):
    me = 4 * x + 2 * y + c
    peers = [(x, y, 1 - c), (1 - x, y, c), (x, 1 - y, c), (1 - x, 1 - y, c)]
    return [(None, me, p, 4 * p[0] + 2 * p[1] + p[2]) for p in peers]


def _plan_sibling(x, y, c):
    return [(2 * q + 1 - c, q, (x, y, 1 - c), q) for q in range(4)]


def _plan_chips(x, y, c):
    me = 2 * x + y
    return [(2 * qx + qy, me, (qx, qy, c), 2 * qx + qy) for qx, qy in ((1 - x, y), (x, 1 - y), (1 - x, 1 - y))]


def _split_copies(plan, src_refs, land_refs, send_sems, recv_sems, arrival):
    n = len(src_refs)
    entries = plan(*_place())
    per = len(entries)
    cps = []
    for a in range(n):
        for k, (src_slot, dst_slot, peer, back_slot) in enumerate(entries):
            src = src_refs[a] if src_slot is None else src_refs[a].at[src_slot]
            cps.append(pltpu.make_async_remote_copy(
                src_ref=src, dst_ref=land_refs[a].at[back_slot if arrival else dst_slot],
                send_sem=send_sems.at[per * a + k], recv_sem=recv_sems.at[per * a + k],
                device_id=peer, device_id_type=MESH))
    return cps


def _split_start(srcs, lands, plan, per, name):
    n = len(srcs)

    def body(*refs):
        for cp in _split_copies(plan, refs[:n], refs[n:2 * n], refs[2 * n], refs[2 * n + 1], False):
            cp.start()
        refs[-1][...] = jnp.zeros_like(refs[-1])

    both = list(srcs) + list(lands)
    outs = pl.pallas_call(
        body, name=name,
        out_shape=(pltpu.SemaphoreType.DMA((per * n,)), pltpu.SemaphoreType.DMA((per * n,)),
                   *[pltpu.HBM(v.shape, v.dtype) for v in both], jax.ShapeDtypeStruct((8, 128), F32)),
        in_specs=[_HBM] * (2 * n),
        out_specs=(_SEM, _SEM, *[_HBM] * (2 * n), pl.BlockSpec(memory_space=pltpu.VMEM)),
        input_output_aliases={i: 2 + i for i in range(2 * n)},
        compiler_params=pltpu.CompilerParams(has_side_effects=_EFFECT),
    )(*[pltpu.with_memory_space_constraint(v, pltpu.HBM) for v in both])
    return outs[0], outs[1], list(outs[2:2 + 2 * n]), outs[-1]


def _split_wait(send_sems, recv_sems, thru, plan, after, name):
    n = len(thru) // 2

    def body(*refs):
        for cp in _split_copies(plan, refs[:n], refs[n:2 * n], refs[2 * n], refs[2 * n + 1], True):
            cp.wait_send()
            cp.wait_recv()

    outs = pl.pallas_call(
        body, name=name, out_shape=tuple(pltpu.HBM(v.shape, v.dtype) for v in thru),
        in_specs=[_HBM] * (2 * n) + [_SEM, _SEM, pl.BlockSpec(memory_space=pl.ANY)],
        out_specs=[_HBM] * (2 * n), input_output_aliases={i: i for i in range(2 * n)},
        compiler_params=pltpu.CompilerParams(has_side_effects=_EFFECT),
    )(*thru, send_sems, recv_sems, after)
    return list(outs[:n]), list(outs[n:])


_SMALL_IN = ("ln_in_g", "ln_in_b")
_SMALL = ("w_s", "b_in", "sinks", "vn_g", "vn_b", "b_s", "b_out", "ln_g", "ln_b")


def _tile_rows(a):
    return -(-a.size // 1024) * 8


def _pack_small(d, names):
    return jnp.concatenate([jnp.pad(d[n].reshape(-1), (0, (-d[n].size) % 1024)).reshape(-1, 128) for n in names])


def _adamw_small(parts, own, w, m, v, like, names, name):
    rows = [_tile_rows(like[n]) for n in names]
    total = sum(rows)
    c1 = 1.0 - ADAM_B1 ** ADAM_STEP
    c2 = 1.0 - ADAM_B2 ** ADAM_STEP

    def body(q_ref, own_ref, p_ref, w_ref, m_ref, v_ref, *o_refs):
        g = None
        for k in range(4):
            term = jnp.where(q_ref[0] == k, own_ref[0], p_ref[k])
            g = term if g is None else g + term
        nm = ADAM_B1 * m_ref[...] + (1.0 - ADAM_B1) * g
        nv = ADAM_B2 * v_ref[...] + (1.0 - ADAM_B2) * (g * g)
        delta = -ADAM_LR * ((nm / c1) / (jnp.sqrt(nv / c2) + ADAM_EPS) + ADAM_WD * w_ref[...])
        for k, val in enumerate((g, delta, nm, nv)):
            off = 0
            for j, r in enumerate(rows):
                o_refs[k * len(rows) + j][...] = val[off:off + r]
                off += r

    whole = pl.BlockSpec((total, 128), lambda i, q: (0, 0))
    chip = (2 * lax.axis_index("x") + lax.axis_index("y")).astype(jnp.int32).reshape(1)
    outs = pl.pallas_call(
        body, name=name,
        grid_spec=pltpu.PrefetchScalarGridSpec(
            num_scalar_prefetch=1, grid=(1,),
            in_specs=[pl.BlockSpec((1, total, 128), lambda i, q: (q[0], 0, 0)),
                      pl.BlockSpec((4, total, 128), lambda i, q: (0, 0, 0)), whole, whole, whole],
            out_specs=[pl.BlockSpec((r, 128), lambda i, q: (0, 0)) for r in rows] * 4),
        out_shape=[jax.ShapeDtypeStruct((r, 128), F32) for r in rows] * 4, compiler_params=_ARB,
    )(chip, own, parts, w, m, v)
    return [{n: outs[k * len(rows) + j].reshape(-1)[:like[n].size].reshape(like[n].shape)
             for j, n in enumerate(names)} for k in range(4)]


def _owner_blocks(g, axis):
    sh = g.shape
    g = g.reshape(sh[:axis] + (4, 2, sh[axis] // N_DEV) + sh[axis + 1:])
    return jnp.moveaxis(g, (axis, axis + 1), (0, 1))


def kernel(x, ln_in_g, ln_in_b, w_in, b_in, sinks, vn_g, vn_b, w_s, b_s, p_a, p_b, w_out, b_out, ln_g, ln_b, loss_target, m_ln_in_g, m_ln_in_b, m_w_in, m_b_in, m_sinks, m_vn_g, m_vn_b, m_w_s, m_b_s, m_p_a, m_p_b, m_w_out, m_b_out, m_ln_g, m_ln_b, v_ln_in_g, v_ln_in_b, v_w_in, v_b_in, v_sinks, v_vn_g, v_vn_b, v_w_s, v_b_s, v_p_a, v_p_b, v_w_out, v_b_out, v_ln_g, v_ln_b):
    nseq, seq, _ = x.shape
    t = nseq * seq
    nblk_seq = seq // BLK
    x2 = x.reshape(t, D)
    tgt = loss_target.reshape(t, D)

    def turned(a):
        return jnp.swapaxes(a, 1, 2)

    w_in_t = turned(w_in)

    def blocks(l):
        return [w_in_t[l].astype(BF16), p_a[l].astype(BF16), p_b[l].astype(BF16), w_out[l].astype(BF16)]

    def full_weights(g):
        w_t_full = g[0].reshape(IN_COLS, D)
        pa_full = jnp.moveaxis(g[1], 0, 1).reshape(Q_W, D)
        pb_full = jnp.moveaxis(g[2], 0, 1).reshape(SGU_W, D)
        wo_full = g[3].reshape(D, D)
        return dict(w_t=w_t_full, pa=pa_full, pb=pb_full, wo=wo_full)

    def landing(bs):
        return [lax.empty((N_DEV,) + v.shape, v.dtype) for v in bs]

    def with_own(landed, sent):
        return [lax.dynamic_update_index_in_dim(g, b, me, 0) for g, b in zip(landed, sent)]

    me = 4 * lax.axis_index("x") + 2 * lax.axis_index("y") + lax.axis_index("c")
    blocks0 = blocks(0)
    a_send, a_recv, a_thru, a_token = _split_start(blocks0[:1], landing(blocks0[:1]), _plan_near, 4,
                                                   "allgather_w_in0_start")
    rest0 = [b + a_token[0, 0].astype(BF16) for b in blocks0[1:]]
    b_send, b_recv, b_thru, b_token = _split_start(rest0, landing(rest0), _plan_all, 7, "allgather_rest0_start")
    xs = [_ln_fwd(x2, ln_in_g + b_token[0, 0], ln_in_b, "ln_in_fwd")]
    sent, landed = _split_wait(a_send, a_recv, a_thru, _plan_near, xs[0], "allgather_w_in0_wait")
    gathered0 = with_own(_forward_sibling(landed, "allgather_w_in0_forward"), sent)
    blocks1, gathered0 = lax.optimization_barrier((blocks(1), gathered0))
    ag_send, ag_recv, ag_thru, ag_token = _split_start(blocks1, landing(blocks1), _plan_all, 7,
                                                       "allgather_weights1_start")
    weights = [None, None]
    bsb = jnp.broadcast_to(b_s[:, :, :, None], (DEPTH, 4, BLK, BLK))
    bias = _band_bias()

    saved = []
    for l in range(DEPTH):
        if l == 1:
            sent, landed = _split_wait(ag_send, ag_recv, ag_thru, _plan_all, xs[1], "allgather_weights1_wait")
            weights[1] = full_weights(with_own(landed, sent))
        w_t = weights[l]["w_t"] if l else gathered0[0].reshape(IN_COLS, D)
        last = l == DEPTH - 1
        b_l = b_in[l].reshape(1, -1) + (ag_token[0, 0] if l == 0 else 0.0)
        hm, hr = _inproj(xs[l], w_t, b_l, f"inproj{l}")
        ya, yb, prob, psink = _mixer_fwd(hm, sinks[l], bias, vn_g[l].reshape(1, -1), vn_b[l].reshape(1, -1),
                                         w_s[l], bsb[l], nblk_seq, f"mixer_fwd{l}")
        if l == 0:
            sent, landed = _split_wait(b_send, b_recv, b_thru, _plan_all, ya, "allgather_rest0_wait")
            weights[0] = full_weights(gathered0 + with_own(landed, sent))
        wl = weights[l]
        outs = _tail_fwd(xs[l], ya, yb, hr, wl["pa"], wl["pb"], wl["wo"], b_out[l].reshape(1, D),
                         ln_g[l].reshape(1, D), ln_b[l].reshape(1, D), f"tail_fwd{l}", last)
        saved.append((hm, hr, ya, yb, prob, psink) + tuple(outs[:4]))
        if not last:
            xs.append(outs[4])

    small = {n: [None] * DEPTH for n in _SMALL}
    names = ("w_in", "p_a", "p_b", "w_out")
    owner_axis = {"w_in": 0, "p_a": 1, "p_b": 1, "w_out": 0}
    token = jnp.zeros((8, 128), F32)
    dx = tgt
    split = [None] * DEPTH
    for l in reversed(range(DEPTH)):
        hm, hr, ya, yb, prob, psink, pa, pb, merged, z = saved[l]
        wl = weights[l]
        dz, dpa, dpb, dhr, dya, dyb, acc, gbr = _tail_bwd(
            dx, z, pa, pb, hr, wl["wo"], wl["pa"], wl["pb"], ln_g[l].reshape(1, D) + token[0, 0],
            ln_b[l].reshape(1, D), f"tail_bwd{l}", l == DEPTH - 1)
        if l == DEPTH - 1:
            sq_err = acc[3:4, 0:128]
        dhm, gbm, gsk, gvn, gws, gbs = _mixer_bwd(
            hm, dya, dyb, prob, psink, vn_g[l].reshape(1, -1), vn_b[l].reshape(1, -1), w_s[l], bsb[l],
            f"mixer_bwd{l}")
        grads = {"w_in": _wgrad(dhr, xs[l], R_W // 2, f"wgrad_in_route{l}",
                                under=_wgrad(dhm, xs[l], MAIN_W // 2, f"wgrad_in_main{l}", rows=IN_COLS)),
                 "p_a": _wgrad(ya, dpa, Q_W, f"wgrad_pa{l}"), "p_b": _wgrad(yb, dpb, SGU_W, f"wgrad_pb{l}"),
                 "w_out": _wgrad(merged, dz, D, f"wgrad_out{l}")}
        small["b_in"][l] = jnp.concatenate([gbm[0], gbr[0]])
        small["sinks"][l] = gsk[:, 0]
        small["vn_g"][l], small["vn_b"][l] = gvn[0], gvn[1]
        small["w_s"][l], small["b_s"][l] = gws, gbs[:, :, 0]
        small["ln_g"][l], small["ln_b"][l], small["b_out"][l] = acc[0], acc[1], acc[2]
        parts = [_owner_blocks(grads[n], owner_axis[n]) for n in names]
        if l == 0:
            packed = _pack_small({n: jnp.stack(v) for n, v in small.items()}, _SMALL)
            parts.append(jnp.broadcast_to(packed[None, None], (1, 2) + packed.shape))
        if l == 0:
            from_sib = _swap_sibling(parts, f"rs_sibling{l}")
        else:
            halves = [p.reshape((N_DEV,) + p.shape[2:]) for p in parts]
            sib = _split_start(halves, [lax.empty((4,) + p.shape[2:], p.dtype) for p in parts], _plan_sibling, 4,
                               f"rs_sibling{l}_start")
            dx = _dx_inproj(dz, dhm, dhr, wl["w_t"], sib[3], f"dx_inproj{l}")
            halves, from_sib = _split_wait(sib[0], sib[1], sib[2], _plan_sibling, dx, f"rs_sibling{l}_wait")
            parts = [h.reshape(p.shape) for h, p in zip(halves, parts)]
        pair = list(_pair_sums(parts, from_sib, f"pair_sums{l}"))
        if l == 0:
            pair[4] = jnp.broadcast_to(pair[4], (4,) + packed.shape)
        lands = [jnp.zeros(p.shape, p.dtype) for p in pair]
        split[l] = _split_start(pair, lands, _plan_chips, 3, f"rs_chips{l}_start")
        token = split[l][3]
        if l == 0:
            grad_x, acc_in = _dx_inproj(dz, dhm, dhr, wl["w_t"], token, f"dx_inproj{l}",
                                        ln_in=(x2, ln_in_g.reshape(1, D)))
    last = [acc_in, jnp.broadcast_to(sq_err, (8, 128))]
    ln_send, ln_recv, ln_thru, ln_token = _split_start(last, landing(last), _plan_all, 7, "allgather_ln_in_start")

    given = {"w_in": (w_in_t, turned(m_w_in), turned(v_w_in)), "p_a": (p_a, m_p_a, v_p_a),
             "p_b": (p_b, m_p_b, v_p_b), "w_out": (w_out, m_w_out, v_w_out)}
    waited = [_split_wait(split[l][0], split[l][1], split[l][2], _plan_chips, ln_token, f"rs_chips{l}_wait")
              for l in range(DEPTH)]
    res = {}
    for a, n in enumerate(names):
        rows, lanes = waited[0][1][a].shape[1:]
        outs = _adamw([waited[l][1][a] for l in range(DEPTH)], *[v.reshape(DEPTH * rows, lanes) for v in given[n]],
                      f"adamw_{n}", own=[waited[l][0][a] for l in range(DEPTH)])
        res[n] = [o.reshape(given[n][0].shape) for o in outs]
    res["w_in"] = [turned(o) for o in res["w_in"]]

    w_small = dict(ln_in_g=ln_in_g, ln_in_b=ln_in_b, b_in=b_in, sinks=sinks, vn_g=vn_g, vn_b=vn_b, w_s=w_s, b_s=b_s,
                   b_out=b_out, ln_g=ln_g, ln_b=ln_b)
    m_small = dict(ln_in_g=m_ln_in_g, ln_in_b=m_ln_in_b, b_in=m_b_in, sinks=m_sinks, vn_g=m_vn_g, vn_b=m_vn_b,
                   w_s=m_w_s, b_s=m_b_s, b_out=m_b_out, ln_g=m_ln_g, ln_b=m_ln_b)
    v_small = dict(ln_in_g=v_ln_in_g, ln_in_b=v_ln_in_b, b_in=v_b_in, sinks=v_sinks, vn_g=v_vn_g, vn_b=v_vn_b,
                   w_s=v_w_s, b_s=v_b_s, b_out=v_b_out, ln_g=v_ln_g, ln_b=v_ln_b)
    outs = _adamw_small(waited[0][1][4], waited[0][0][4], *[_pack_small(d, _SMALL) for d in (w_small, m_small, v_small)],
                        w_small, _SMALL, "adamw_small")
    sent, landed = _split_wait(ln_send, ln_recv, ln_thru, _plan_all, outs[0]["ln_b"], "allgather_ln_in_wait")
    all_in, all_sq = with_own(landed, sent)
    loss = jnp.sum(all_sq[:, 0, 0]) * (0.5 / D)
    outs_in = _adamw([all_in], *[jnp.pad(jnp.stack([d[n] for n in _SMALL_IN]), ((0, 6), (0, 0)))
                                 for d in (w_small, m_small, v_small)], "adamw_ln_in")
    for k in range(4):
        u = dict(outs[k])
        u.update({n: outs_in[k][r] for r, n in enumerate(_SMALL_IN)})
        for n in u:
            res.setdefault(n, [None] * 4)[k] = u[n]

    order = ("ln_in_g", "ln_in_b", "w_in", "b_in", "sinks", "vn_g", "vn_b", "w_s", "b_s", "p_a", "p_b", "w_out",
             "b_out", "ln_g", "ln_b")
    return (loss, grad_x.reshape(x.shape), *[res[n][0] for n in order], *[res[n][1] for n in order],
            *[res[n][2] for n in order], *[res[n][3] for n in order])
```

```python
import jax
import jax.numpy as jnp
from jax import lax
from jax.experimental import pallas as pl
from jax.experimental.pallas import tpu as pltpu

F32 = jnp.float32
BF16 = jnp.bfloat16

D = 1024
BLK = 128
N_KV = 2
Q_W, KV_W, SGU_W = 512, 128, 512
C_Q, C_K, C_V, C_GA, C_UB, C_VB, C_GB = 0, 512, 640, 768, 1280, 1792, 2304
MAIN_W = 2816
R_W = 2048
IN_COLS = MAIN_W + R_W
N_DEV = 8

DEPTH = 2
ALPHA = (2.0 * DEPTH) ** 0.25
LN_EPS = 1e-5
ATTN_SCALE = 0.125
NEG = float(jnp.finfo(jnp.float32).min)

ADAM_LR, ADAM_B1, ADAM_B2, ADAM_EPS, ADAM_WD, ADAM_STEP = 0.001, 0.9, 0.999, 1e-08, 0.01, 10

TM = 512
TM_EW = 1024
TM_MM = 512
TM_T = 512
NB = TM // BLK
MESH = pl.DeviceIdType.MESH
VMEM_LIMIT = 56 * 1024 * 1024

_ARB = pltpu.CompilerParams(dimension_semantics=("arbitrary",), vmem_limit_bytes=VMEM_LIMIT)


def _sigmoid(x):
    return 1.0 / (1.0 + jnp.exp(-x))


_GELU_C = 0.7978845608028654
_GELU_A = 0.044715


def _gelu_parts(x):
    x2 = x * x
    t = jnp.tanh(x * (_GELU_C + (_GELU_C * _GELU_A) * x2))
    hx = 0.5 * x
    return hx, t, x2


def _gelu(x):
    hx, t, _ = _gelu_parts(x)
    return hx + hx * t


def _gelu_and_grad(x):
    hx, t, x2 = _gelu_parts(x)
    grad = 0.5 + 0.5 * t + (hx - hx * (t * t)) * (_GELU_C + (3.0 * _GELU_C * _GELU_A) * x2)
    return hx + hx * t, grad


def _ln_stats(x):
    mu = jnp.mean(x, axis=-1, keepdims=True)
    xc = x - mu
    var = jnp.mean(xc * xc, axis=-1, keepdims=True)
    rstd = lax.rsqrt(var + LN_EPS)
    return xc * rstd, rstd


def _ln_bwd(dy_g, xhat, rstd):
    m1 = jnp.mean(dy_g, axis=-1, keepdims=True)
    m2 = jnp.mean(dy_g * xhat, axis=-1, keepdims=True)
    return rstd * (dy_g - m1 - xhat * m2)


def _colsum(x):
    return jnp.sum(x, axis=0, keepdims=True)


def _dot(a, b):
    return jnp.dot(a, b, preferred_element_type=F32)


def _dot_nt(a, b):
    return lax.dot_general(a, b, (((1,), (1,)), ((), ())), preferred_element_type=F32)


def _dot_tn(a, b):
    return lax.dot_general(a, b, (((0,), (0,)), ((), ())), preferred_element_type=F32)


def _head_place(hk, g):
    j = 4 * hk + g
    return j, j // 2, j % 2


def _head_rows(x, hk):
    d = lax.broadcasted_iota(jnp.int32, x.shape, 0)
    return jnp.where((d >= 64 * hk) & (d < 64 * hk + 64), x, 0.0).astype(BF16)


def _head_lanes(x, hk):
    d = lax.broadcasted_iota(jnp.int32, x.shape, 1)
    return jnp.where((d >= 64 * hk) & (d < 64 * hk + 64), x, 0.0)


def _band_bias():
    kpos = lax.broadcasted_iota(jnp.int32, (2 * BLK, 4 * BLK), 0)
    row = lax.broadcasted_iota(jnp.int32, (2 * BLK, 4 * BLK), 1) & (BLK - 1)
    band = (kpos > row) & (kpos <= row + BLK)
    return jnp.stack([jnp.where(band, 0.0, NEG), jnp.where(band & (kpos >= BLK), 0.0, NEG)]).astype(F32)


def _stack_q(q, hk):
    parts = []
    for g in range(4):
        _, p, pos = _head_place(hk, g)
        qp = q[:, BLK * p:BLK * (p + 1)] * ATTN_SCALE
        if pos != hk:
            qp = pltpu.roll(qp, 64, 1)
        parts.append(qp.astype(BF16))
    return jnp.concatenate(parts, axis=0)


def _attn_probs(q4, kh, hk, sinks_ref, bias):
    s_t = _dot_nt(kh, q4) + bias
    sink_row = jnp.concatenate(
        [jnp.full((1, BLK), sinks_ref[4 * hk + g], F32) for g in range(4)], axis=1)
    m = jnp.maximum(jnp.max(s_t, axis=0, keepdims=True), sink_row)
    p_un = jnp.exp(s_t - m)
    e_sink = jnp.exp(sink_row - m)
    inv = 1.0 / (jnp.sum(p_un, axis=0, keepdims=True) + e_sink)
    return (p_un * inv).astype(BF16), e_sink * inv


def _unstack_heads(x4, hk, pairs):
    for g in range(4):
        _, p, pos = _head_place(hk, g)
        xg = x4[BLK * g:BLK * (g + 1)]
        if pos != hk:
            xg = pltpu.roll(xg, 64, 1)
        pairs[p] = xg if pairs[p] is None else pairs[p] + xg
    return pairs


def _attn_fwd(q, kband, vband, sinks_ref, bias, save):
    pairs = [None] * 4
    vband_t = vband.T
    for hk in range(N_KV):
        prob_t, p_sink = _attn_probs(_stack_q(q, hk), _head_lanes(kband, hk).astype(BF16), hk, sinks_ref, bias)
        save(hk, prob_t, p_sink)
        o_t = _dot(_head_rows(vband_t, hk), prob_t)
        pairs = _unstack_heads(o_t.T, hk, pairs)
    return jnp.concatenate(pairs, axis=1)


def _tril_mask():
    r = lax.broadcasted_iota(jnp.int32, (BLK, BLK), 0)
    c = lax.broadcasted_iota(jnp.int32, (BLK, BLK), 1)
    return c <= r


def _sgu_fwd(u, v, vn_g, vn_b, wt, bsb_ref):
    vhat, rstd = _ln_stats(v)
    vn = vhat * vn_g + vn_b
    mixed = jnp.concatenate(
        [_dot(wt[g], vn[:, BLK * g:BLK * (g + 1)].astype(BF16)) + bsb_ref[g] for g in range(4)], axis=1)
    return vhat, rstd, vn, mixed


def _cols(ref, rows, col, width):
    return ref[rows, col:col + width].astype(F32)


def _band(hm_ref, hprev_ref, s, col):
    r0 = s * BLK
    cur = hm_ref[r0:r0 + BLK, col:col + KV_W]
    if s == 0:
        off = 0 if col == C_K else KV_W
        prev = hprev_ref[:, off:off + KV_W]
    else:
        prev = hm_ref[r0 - BLK:r0, col:col + KV_W]
    return jnp.concatenate([prev, cur], axis=0).astype(F32)


def _h_main_specs(nt, rev):
    def tile(g):
        return nt - 1 - g if rev else g

    return [pl.BlockSpec((TM, MAIN_W), lambda g: (tile(g), 0)),
            pl.BlockSpec((BLK, 2 * KV_W), lambda g: (jnp.maximum(tile(g) * NB - 1, 0), 2))]


_CONST2 = lambda g: (0, 0)
_CONST3 = lambda g: (0, 0, 0)


def _ln_fwd(x, g, b, name):
    t = x.shape[0]

    def body(x_ref, g_ref, b_ref, o_ref):
        xhat, _ = _ln_stats(x_ref[...])
        o_ref[...] = xhat * g_ref[...] + b_ref[...]

    return pl.pallas_call(
        body, name=name, grid=(t // TM_EW,),
        in_specs=[pl.BlockSpec((TM_EW, D), lambda i: (i, 0)), pl.BlockSpec((1, D), _CONST2),
                  pl.BlockSpec((1, D), _CONST2)],
        out_specs=pl.BlockSpec((TM_EW, D), lambda i: (i, 0)),
        out_shape=jax.ShapeDtypeStruct((t, D), F32), compiler_params=_ARB,
    )(x, g.reshape(1, D), b.reshape(1, D))


def _inproj(x, w_t, b, name):
    t = x.shape[0]

    def body(x_ref, wt_ref, b_ref, hm_ref, hr_ref):
        xb = x_ref[...].astype(BF16)
        hm_ref[...] = (_dot_nt(xb, wt_ref[0:MAIN_W, :]) + b_ref[:, 0:MAIN_W]).astype(BF16)
        hr_ref[...] = (_dot_nt(xb, wt_ref[MAIN_W:IN_COLS, :]) + b_ref[:, MAIN_W:IN_COLS]).astype(BF16)

    return pl.pallas_call(
        body, name=name, grid=(t // TM_MM,),
        in_specs=[pl.BlockSpec((TM_MM, D), lambda i: (i, 0)),
                  pl.BlockSpec((IN_COLS, D), _CONST2), pl.BlockSpec((1, IN_COLS), _CONST2)],
        out_specs=[pl.BlockSpec((TM_MM, MAIN_W), lambda i: (i, 0)), pl.BlockSpec((TM_MM, R_W), lambda i: (i, 0))],
        out_shape=[jax.ShapeDtypeStruct((t, MAIN_W), BF16), jax.ShapeDtypeStruct((t, R_W), BF16)],
        compiler_params=_ARB,
    )(x, w_t, b)


def _mixer_fwd(hm, sinks, bias, vn_g, vn_b, w_s, bsb, nblk_seq, name):
    t = hm.shape[0]
    nt = t // TM

    def body(sinks_ref, hm_ref, hprev_ref, bias_ref, vng_ref, vnb_ref, ws_ref, bsb_ref,
             ya_ref, yb_ref, prob_ref, psink_ref):
        i = pl.program_id(0)
        tril = _tril_mask()
        wt = [jnp.where(tril, ws_ref[g], 0.0).astype(BF16) for g in range(4)]
        for s in range(NB):
            r0 = s * BLK
            rows = slice(r0, r0 + BLK)
            bias = bias_ref[jnp.where((i * NB + s) % nblk_seq == 0, 1, 0)]

            def save(hk, prob_t, p_sink, s=s):
                prob_ref[N_KV * s + hk] = prob_t
                psink_ref[N_KV * s + hk] = jnp.broadcast_to(p_sink, (8, 4 * BLK))

            attn = _attn_fwd(_cols(hm_ref, rows, C_Q, Q_W), _band(hm_ref, hprev_ref, s, C_K),
                             _band(hm_ref, hprev_ref, s, C_V), sinks_ref, bias, save)
            g_a = _cols(hm_ref, rows, C_GA, Q_W)
            ya_ref[rows, :] = (attn * (g_a * _sigmoid(g_a))).astype(BF16)
            u = _gelu(_cols(hm_ref, rows, C_UB, SGU_W))
            mixed = _sgu_fwd(u, _gelu(_cols(hm_ref, rows, C_VB, SGU_W)), vng_ref[...], vnb_ref[...], wt, bsb_ref)[-1]
            g_b = _cols(hm_ref, rows, C_GB, SGU_W)
            yb_ref[rows, :] = (u * mixed * (g_b * _sigmoid(g_b))).astype(BF16)

    ngrp = N_KV * NB
    return pl.pallas_call(
        body, name=name, grid=(nt,),
        in_specs=[pl.BlockSpec(memory_space=pltpu.SMEM)] + _h_main_specs(nt, False) + [
            pl.BlockSpec((2, 2 * BLK, 4 * BLK), _CONST3),
            pl.BlockSpec((1, SGU_W), _CONST2), pl.BlockSpec((1, SGU_W), _CONST2),
            pl.BlockSpec((4, BLK, BLK), _CONST3), pl.BlockSpec((4, BLK, BLK), _CONST3)],
        out_specs=[pl.BlockSpec((TM, Q_W), lambda i: (i, 0)), pl.BlockSpec((TM, SGU_W), lambda i: (i, 0)),
                   pl.BlockSpec((ngrp, 2 * BLK, 4 * BLK), lambda i: (i, 0, 0)),
                   pl.BlockSpec((ngrp, 8, 4 * BLK), lambda i: (i, 0, 0))],
        out_shape=[jax.ShapeDtypeStruct((t, Q_W), BF16), jax.ShapeDtypeStruct((t, SGU_W), BF16),
                   jax.ShapeDtypeStruct((nt * ngrp, 2 * BLK, 4 * BLK), BF16),
                   jax.ShapeDtypeStruct((nt * ngrp, 8, 4 * BLK), F32)],
        compiler_params=_ARB,
    )(sinks, hm, hm, bias, vn_g, vn_b, w_s, bsb)


def _tail_fwd(x, ya, yb, hr, pa_w, pb_w, wo, b_out, ln_g, ln_b, name, last):
    t = x.shape[0]

    def body(x_ref, ya_ref, yb_ref, hr_ref, paw_ref, pbw_ref, wo_ref, bo_ref, g_ref, b_ref,
             pa_ref, pb_ref, mg_ref, z_ref, *xn_ref):
        pa = _dot(ya_ref[...], paw_ref[...])
        pb = _dot(yb_ref[...], pbw_ref[...])
        pa_ref[...] = pa.astype(BF16)
        pb_ref[...] = pb.astype(BF16)
        everything = slice(None)
        merged = _sigmoid(_cols(hr_ref, everything, 0, D)) * pa + _sigmoid(_cols(hr_ref, everything, D, D)) * pb
        mb = merged.astype(BF16)
        mg_ref[...] = mb
        z = ALPHA * x_ref[...] + (_dot(mb, wo_ref[...]) + bo_ref[...])
        z_ref[...] = z
        if not last:
            zhat, _ = _ln_stats(z)
            xn_ref[0][...] = zhat * g_ref[...] + b_ref[...]

    row = lambda w: pl.BlockSpec((TM_T, w), lambda i: (i, 0))
    vec = pl.BlockSpec((1, D), _CONST2)
    n_f32 = 1 if last else 2
    return pl.pallas_call(
        body, name=name, grid=(t // TM_T,),
        in_specs=[row(D), row(Q_W), row(SGU_W), row(R_W),
                  pl.BlockSpec((Q_W, D), _CONST2), pl.BlockSpec((SGU_W, D), _CONST2), pl.BlockSpec((D, D), _CONST2),
                  vec, vec, vec],
        out_specs=[row(D)] * (3 + n_f32),
        out_shape=[jax.ShapeDtypeStruct((t, D), BF16)] * 3 + [jax.ShapeDtypeStruct((t, D), F32)] * n_f32,
        compiler_params=_ARB,
    )(x, ya, yb, hr, pa_w, pb_w, wo, b_out, ln_g, ln_b)


def _tail_bwd(dxn, z, pa, pb, hr, wo, pa_w, pb_w, ln_g, ln_b, name, from_loss):
    t = dxn.shape[0]

    def body(dxn_ref, z_ref, pa_ref, pb_ref, hr_ref, wo_ref, paw_ref, pbw_ref, g_ref, b_ref,
             dz_ref, dpa_ref, dpb_ref, dhr_ref, dya_ref, dyb_ref, acc_ref, gbr_ref):
        @pl.when(pl.program_id(0) == 0)
        def _():
            acc_ref[...] = jnp.zeros_like(acc_ref)
            gbr_ref[...] = jnp.zeros_like(gbr_ref)

        zhat, rstd = _ln_stats(z_ref[...])
        if from_loss:
            err = zhat * g_ref[...] + b_ref[...] - dxn_ref[...]
            dxn_v = err * (1.0 / D)
            sq = jnp.sum(jnp.sum(err * err, axis=1, keepdims=True), axis=0, keepdims=True)
            acc_ref[3:4, :] += jnp.broadcast_to(sq, (1, D))
        else:
            dxn_v = dxn_ref[...]
        dz = _ln_bwd(dxn_v * g_ref[...], zhat, rstd)
        dz_ref[...] = dz
        acc_ref[0:1, :] += _colsum(dxn_v * zhat)
        acc_ref[1:2, :] += _colsum(dxn_v)
        acc_ref[2:3, :] += _colsum(dz)
        dmerged = _dot_nt(dz.astype(BF16), wo_ref[...])
        everything = slice(None)
        sa = _sigmoid(_cols(hr_ref, everything, 0, D))
        sb = _sigmoid(_cols(hr_ref, everything, D, D))
        dpa = (dmerged * sa).astype(BF16)
        dpb = (dmerged * sb).astype(BF16)
        dpa_ref[...] = dpa
        dpb_ref[...] = dpb
        dra = dmerged * pa_ref[...].astype(F32) * (sa * (1.0 - sa))
        drb = dmerged * pb_ref[...].astype(F32) * (sb * (1.0 - sb))
        dhr_ref[:, 0:D] = dra.astype(BF16)
        dhr_ref[:, D:2 * D] = drb.astype(BF16)
        gbr_ref[0:1, 0:D] += _colsum(dra)
        gbr_ref[0:1, D:2 * D] += _colsum(drb)
        dya_ref[...] = _dot_nt(dpa, paw_ref[...]).astype(BF16)
        dyb_ref[...] = _dot_nt(dpb, pbw_ref[...]).astype(BF16)

    row = lambda w: pl.BlockSpec((TM_T, w), lambda i: (i, 0))
    vec = pl.BlockSpec((1, D), _CONST2)
    return pl.pallas_call(
        body, name=name, grid=(t // TM_T,),
        in_specs=[row(D), row(D), row(D), row(D), row(R_W),
                  pl.BlockSpec((D, D), _CONST2), pl.BlockSpec((Q_W, D), _CONST2), pl.BlockSpec((SGU_W, D), _CONST2),
                  vec, vec],
        out_specs=[row(D), row(D), row(D), row(R_W), row(Q_W), row(SGU_W), pl.BlockSpec((8, D), _CONST2),
                   pl.BlockSpec((8, R_W), _CONST2)],
        out_shape=[jax.ShapeDtypeStruct((t, D), F32), jax.ShapeDtypeStruct((t, D), BF16),
                   jax.ShapeDtypeStruct((t, D), BF16), jax.ShapeDtypeStruct((t, R_W), BF16),
                   jax.ShapeDtypeStruct((t, Q_W), BF16), jax.ShapeDtypeStruct((t, SGU_W), BF16),
                   jax.ShapeDtypeStruct((8, D), F32), jax.ShapeDtypeStruct((8, R_W), F32)],
        compiler_params=_ARB,
    )(dxn, z, pa, pb, hr, wo, pa_w, pb_w, ln_g, ln_b)


def _mixer_bwd(hm, dya, dyb, prob, psink, vn_g, vn_b, w_s, bsb, name):
    t = hm.shape[0]
    nt = t // TM
    ngrp = N_KV * NB

    def body(hm_ref, hprev_ref, prob_ref, psink_ref, dya_ref, dyb_ref, vng_ref, vnb_ref, ws_ref, bsb_ref,
             dhm_ref, gbm_ref, gsk_ref, gvn_ref, gws_ref, gbs_ref, dk_carry, dv_carry):
        gi = pl.program_id(0)

        @pl.when(gi == 0)
        def _():
            for r in (gbm_ref, gsk_ref, gvn_ref, gws_ref, gbs_ref, dk_carry, dv_carry):
                r[...] = jnp.zeros_like(r)

        tril = _tril_mask()
        wt = [jnp.where(tril, ws_ref[g], 0.0).astype(BF16) for g in range(4)]
        vng = vng_ref[...]
        ones8 = jnp.ones((8, BLK), BF16)

        def put(rows, col, val):
            dhm_ref[rows, col:col + val.shape[1]] = val.astype(BF16)

        for s in reversed(range(NB)):
            r0 = s * BLK
            rows = slice(r0, r0 + BLK)
            q = _cols(hm_ref, rows, C_Q, Q_W)
            kband = _band(hm_ref, hprev_ref, s, C_K)
            vband = _band(hm_ref, hprev_ref, s, C_V)
            g_a = _cols(hm_ref, rows, C_GA, Q_W)
            sg = _sigmoid(g_a)
            dya_v = _cols(dya_ref, rows, 0, Q_W)
            d_o = dya_v * (g_a * sg)
            o_pairs, dq_pairs = [None] * 4, [None] * 4
            dkband = jnp.zeros((2 * BLK, KV_W), F32)
            dvband = jnp.zeros((2 * BLK, KV_W), F32)
            kband_t, vband_t = kband.T, vband.T
            for hk in range(N_KV):
                q4 = _stack_q(q, hk)
                prob_b = prob_ref[N_KV * s + hk]
                p_sink = psink_ref[N_KV * s + hk][0:1, :]
                o_t = _dot(_head_rows(vband_t, hk), prob_b)
                o_pairs = _unstack_heads(o_t.T, hk, o_pairs)
                parts = []
                for g in range(4):
                    _, p, pos = _head_place(hk, g)
                    dp = d_o[:, BLK * p:BLK * (p + 1)]
                    parts.append(pltpu.roll(dp, 64, 1) if pos != hk else dp)
                do4 = _head_lanes(jnp.concatenate(parts, axis=0), hk)
                do4b = do4.astype(BF16)
                delta = _colsum(do4.T * o_t)
                vh = _head_lanes(vband, hk).astype(BF16)
                ds_t = prob_b.astype(F32) * (_dot_nt(vh, do4b) - delta)
                dsb = ds_t.astype(BF16)
                dq4_t = _dot(_head_rows(kband_t, hk), dsb)
                dq_pairs = _unstack_heads(dq4_t.T * ATTN_SCALE, hk, dq_pairs)
                dkband = dkband + _head_lanes(_dot(dsb, q4), hk)
                dvband = dvband + _dot(prob_b, do4b)
                dsk = p_sink * delta
                for g in range(4):
                    j = 4 * hk + g
                    tot = jnp.sum(dsk[:, BLK * g:BLK * (g + 1)], axis=1, keepdims=True)
                    gsk_ref[j:j + 1, :] += jnp.broadcast_to(-tot, (1, 128))
            attn = jnp.concatenate(o_pairs, axis=1)
            put(rows, C_Q, jnp.concatenate(dq_pairs, axis=1))
            put(rows, C_K, dkband[BLK:2 * BLK] + dk_carry[...])
            put(rows, C_V, dvband[BLK:2 * BLK] + dv_carry[...])
            dk_carry[...] = dkband[0:BLK]
            dv_carry[...] = dvband[0:BLK]
            put(rows, C_GA, dya_v * attn * (sg * (1.0 + g_a * (1.0 - sg))))
            u, du_du_b = _gelu_and_grad(_cols(hm_ref, rows, C_UB, SGU_W))
            v, dv_dv_b = _gelu_and_grad(_cols(hm_ref, rows, C_VB, SGU_W))
            g_b = _cols(hm_ref, rows, C_GB, SGU_W)
            vhat, rstd, vn, mixed = _sgu_fwd(u, v, vng, vnb_ref[...], wt, bsb_ref)
            sgb = _sigmoid(g_b)
            silu_b = g_b * sgb
            dyb_v = _cols(dyb_ref, rows, 0, SGU_W)
            du = dyb_v * mixed * silu_b
            dmixed = dyb_v * u * silu_b
            put(rows, C_GB, dyb_v * u * mixed * (sgb * (1.0 + g_b * (1.0 - sgb))))
            dvn_parts = []
            for g in range(4):
                cols = slice(BLK * g, BLK * (g + 1))
                dmg = dmixed[:, cols]
                dmgb = dmg.astype(BF16)
                dvn_parts.append(_dot_tn(wt[g], dmgb))
                gws_ref[g] += jnp.where(tril, _dot_nt(dmgb, vn[:, cols].astype(BF16)), 0.0)
                gbs_ref[g] += dmg
            dvn = jnp.concatenate(dvn_parts, axis=1)
            gvn_ref[0:1, :] += _colsum(dvn * vhat)
            gvn_ref[1:2, :] += _colsum(dvn)
            dv = _ln_bwd(dvn * vng, vhat, rstd)
            put(rows, C_UB, du * du_du_b)
            put(rows, C_VB, dv * dv_dv_b)
            gbm_ref[...] += _dot(ones8, dhm_ref[rows, :])

        @pl.when(gi == nt - 1)
        def _():
            for g in range(4):
                gbs_ref[g] = jnp.broadcast_to(jnp.sum(gbs_ref[g], axis=1, keepdims=True), (BLK, BLK))

    row = lambda w: pl.BlockSpec((TM, w), lambda g: (nt - 1 - g, 0))
    return pl.pallas_call(
        body, name=name, grid=(nt,),
        in_specs=_h_main_specs(nt, True) + [
            pl.BlockSpec((ngrp, 2 * BLK, 4 * BLK), lambda g: (nt - 1 - g, 0, 0)),
            pl.BlockSpec((ngrp, 8, 4 * BLK), lambda g: (nt - 1 - g, 0, 0)),
            row(Q_W), row(SGU_W),
            pl.BlockSpec((1, SGU_W), _CONST2), pl.BlockSpec((1, SGU_W), _CONST2),
            pl.BlockSpec((4, BLK, BLK), _CONST3), pl.BlockSpec((4, BLK, BLK), _CONST3)],
        out_specs=[row(MAIN_W), pl.BlockSpec((8, MAIN_W), _CONST2), pl.BlockSpec((8, 128), _CONST2),
                   pl.BlockSpec((8, SGU_W), _CONST2), pl.BlockSpec((4, BLK, BLK), _CONST3),
                   pl.BlockSpec((4, BLK, BLK), _CONST3)],
        out_shape=[jax.ShapeDtypeStruct((t, MAIN_W), BF16), jax.ShapeDtypeStruct((8, MAIN_W), F32),
                   jax.ShapeDtypeStruct((8, 128), F32), jax.ShapeDtypeStruct((8, SGU_W), F32),
                   jax.ShapeDtypeStruct((4, BLK, BLK), F32), jax.ShapeDtypeStruct((4, BLK, BLK), F32)],
        scratch_shapes=[pltpu.VMEM((BLK, KV_W), F32), pltpu.VMEM((BLK, KV_W), F32)],
        compiler_params=_ARB,
    )(hm, hm, prob, psink, dya, dyb, vn_g, vn_b, w_s, bsb)


def _dx_inproj(dz, dhm, dhr, w_t, after, name, ln_in=None):
    t = dz.shape[0]

    def body(dz_ref, dhm_ref, dhr_ref, wt_ref, after_ref, *rest):
        dx = (ALPHA * dz_ref[...] + after_ref[0:1, 0:1] + _dot(dhm_ref[...], wt_ref[0:MAIN_W, :])
              + _dot(dhr_ref[...], wt_ref[MAIN_W:IN_COLS, :]))
        if ln_in is None:
            rest[0][...] = dx
            return
        x_ref, g_ref, gx_ref, acc_ref = rest

        @pl.when(pl.program_id(0) == 0)
        def _():
            acc_ref[...] = jnp.zeros_like(acc_ref)

        xhat, rstd = _ln_stats(x_ref[...])
        gx_ref[...] = _ln_bwd(dx * g_ref[...], xhat, rstd)
        acc_ref[0:1, :] += _colsum(dx * xhat)
        acc_ref[1:2, :] += _colsum(dx)

    row = lambda w: pl.BlockSpec((TM_MM, w), lambda i: (i, 0))
    in_specs = [row(D), row(MAIN_W), row(R_W), pl.BlockSpec((IN_COLS, D), _CONST2), pl.BlockSpec((8, 128), _CONST2)]
    if ln_in is None:
        return pl.pallas_call(
            body, name=name, grid=(t // TM_MM,), in_specs=in_specs,
            out_specs=row(D), out_shape=jax.ShapeDtypeStruct((t, D), F32), compiler_params=_ARB,
        )(dz, dhm, dhr, w_t, after)
    return pl.pallas_call(
        body, name=name, grid=(t // TM_MM,), in_specs=in_specs + [row(D), pl.BlockSpec((1, D), _CONST2)],
        out_specs=[row(D), pl.BlockSpec((8, D), _CONST2)],
        out_shape=[jax.ShapeDtypeStruct((t, D), F32), jax.ShapeDtypeStruct((8, D), F32)], compiler_params=_ARB,
    )(dz, dhm, dhr, w_t, after, *ln_in)


def _wgrad(a, b, tm, name, rows=None, under=None):
    t, m = a.shape
    n = b.shape[1]
    tk = min(t, 2048)
    nk = t // tk

    def body(a_ref, b_ref, *rest):
        o_ref, acc_ref = rest[-2:]
        k = pl.program_id(1)

        @pl.when(k == 0)
        def _():
            acc_ref[...] = jnp.zeros_like(acc_ref)

        acc_ref[...] += _dot_tn(a_ref[...].astype(BF16), b_ref[...].astype(BF16))

        @pl.when(k == nk - 1)
        def _():
            o_ref[...] = acc_ref[...].astype(BF16)

    in_specs = [pl.BlockSpec((tk, tm), lambda j, k: (k, j)), pl.BlockSpec((tk, n), lambda j, k: (k, 0))]
    if under is None:
        out_rows, out_spec, operands, aliases = rows or m, pl.BlockSpec((tm, n), lambda j, k: (j, 0)), (a, b), {}
    else:
        out_rows = under.shape[0]
        first = out_rows - m
        assert first % 128 == 0 and tm % 128 == 0
        out_spec = pl.BlockSpec((pl.Element(tm), pl.Element(n)),
                                lambda j, k: (pl.multiple_of(first + j * tm, 128), 0))
        in_specs, operands, aliases = in_specs + [_ANY], (a, b, under), {2: 0}
    return pl.pallas_call(
        body, name=name, grid=(m // tm, nk), in_specs=in_specs, out_specs=out_spec,
        out_shape=jax.ShapeDtypeStruct((out_rows, n), BF16), input_output_aliases=aliases,
        scratch_shapes=[pltpu.VMEM((tm, n), F32)],
        compiler_params=pltpu.CompilerParams(dimension_semantics=("arbitrary", "arbitrary"), vmem_limit_bytes=VMEM_LIMIT),
    )(*operands)


_ANY = pl.BlockSpec(memory_space=pl.ANY)


def _place():
    return lax.axis_index("x"), lax.axis_index("y"), lax.axis_index("c")


def _forward_sibling(lands, name):
    n = len(lands)

    def body(*refs):
        l_refs = refs[n:2 * n]
        send_sems, recv_sems = refs[2 * n:]
        x, y, c = _place()
        chips = [(1 - x, y), (x, 1 - y), (1 - x, 1 - y)]

        def copy(a, j, core):
            rows = l_refs[a].at[4 * chips[j][0] + 2 * chips[j][1] + core]
            return pltpu.make_async_remote_copy(
                src_ref=rows, dst_ref=rows, send_sem=send_sems.at[3 * a + j], recv_sem=recv_sems.at[3 * a + j],
                device_id=(x, y, 1 - c), device_id_type=MESH)

        for a in range(n):
            for j in range(3):
                copy(a, j, c).start()
        for a in range(n):
            for j in range(3):
                copy(a, j, 1 - c).wait_recv()
                copy(a, j, c).wait_send()

    return pl.pallas_call(
        body, name=name, in_specs=[_ANY] * n, out_specs=[_ANY] * n,
        out_shape=[jax.ShapeDtypeStruct(v.shape, v.dtype) for v in lands],
        input_output_aliases={a: a for a in range(n)},
        scratch_shapes=[pltpu.SemaphoreType.DMA((3 * n,)), pltpu.SemaphoreType.DMA((3 * n,))],
    )(*lands)


def _swap_sibling(gs, name):
    n = len(gs)
    first = [0]
    for v in gs:
        first.append(first[-1] + v.shape[0])

    def body(*refs):
        g_refs, r_refs = refs[:n], refs[n:2 * n]
        send_sems, recv_sems = refs[2 * n:]
        x, y, c = _place()
        cps = [pltpu.make_async_remote_copy(
            src_ref=g_refs[a].at[q, 1 - c], dst_ref=r_refs[a].at[q], send_sem=send_sems.at[first[a] + q],
            recv_sem=recv_sems.at[first[a] + q], device_id=(x, y, 1 - c), device_id_type=MESH)
            for a in range(n) for q in range(gs[a].shape[0])]
        for cp in cps:
            cp.start()
        for cp in cps:
            cp.wait()

    return pl.pallas_call(
        body, name=name, in_specs=[_ANY] * n, out_specs=[_ANY] * n,
        out_shape=[jax.ShapeDtypeStruct(v.shape[:1] + v.shape[2:], v.dtype) for v in gs],
        scratch_shapes=[pltpu.SemaphoreType.DMA((first[-1],)), pltpu.SemaphoreType.DMA((first[-1],))],
    )(*gs)


def _row_tile(rows, lanes, cap):
    if rows * lanes * 4 <= (1 << 20):
        return rows
    return max(d for d in range(8, cap + 1, 8) if rows % d == 0 and (d % 16 == 0 or rows % 16 != 0))


def _pair_sums(gs, rs, name):
    n = len(gs)

    def add(g, r, dtype):
        return (g.astype(F32) + r.astype(F32)).astype(dtype)

    def body(c_ref, *refs):
        g_refs, r_refs, o_refs = refs[:n], refs[n:2 * n], refs[2 * n:]
        o_refs[0][...] = add(g_refs[0][0], r_refs[0][...], o_refs[0].dtype)

        @pl.when(pl.program_id(0) == 0)
        def _():
            for a in range(1, n):
                o_refs[a][...] = add(g_refs[a][:, 0], r_refs[a][...], o_refs[a].dtype)

    def whole(shape, mine):
        if mine:
            return pl.BlockSpec(shape, lambda q, c_ref: (0, c_ref[0]) + (0,) * (len(shape) - 2))
        return pl.BlockSpec(shape, lambda q, c_ref: (0,) * len(shape))

    big = gs[0].shape
    return pl.pallas_call(
        body, name=name,
        grid_spec=pltpu.PrefetchScalarGridSpec(
            num_scalar_prefetch=1, grid=(big[0],),
            in_specs=[pl.BlockSpec((1, 1) + big[2:], lambda q, c_ref: (q, c_ref[0], 0, 0))]
            + [whole(g.shape[:1] + (1,) + g.shape[2:], True) for g in gs[1:]]
            + [pl.BlockSpec((1,) + big[2:], lambda q, c_ref: (q, 0, 0))]
            + [whole(r.shape, False) for r in rs[1:]],
            out_specs=[pl.BlockSpec((1,) + big[2:], lambda q, c_ref: (q, 0, 0))]
            + [whole(r.shape, False) for r in rs[1:]]),
        out_shape=[jax.ShapeDtypeStruct(r.shape, g.dtype) for g, r in zip(gs, rs)],
        compiler_params=_ARB,
    )(lax.axis_index("c").astype(jnp.int32).reshape(1), *gs, *rs)


def _adamw(parts, w, m, v, name, own=None):
    nl = len(parts)
    ns, rows, l = parts[0].shape
    tr = _row_tile(rows, l * ns, 304)
    nt = rows // tr
    c1 = 1.0 - ADAM_B1 ** ADAM_STEP
    c2 = 1.0 - ADAM_B2 ** ADAM_STEP

    def body(q_ref, *refs):
        own_refs = refs[:nl] if own is not None else None
        p_refs = refs[-7 - nl:-7]
        w_ref, m_ref, v_ref, g_ref, d_ref, nm_ref, nv_ref = refs[-7:]
        layer = pl.program_id(0)
        g = None
        for j in range(nl):
            gj = None
            for k in range(ns):
                term = p_refs[j][k].astype(F32)
                if own_refs is not None:
                    term = jnp.where(q_ref[0] == k, own_refs[j][0].astype(F32), term)
                gj = term if gj is None else gj + term
            g = gj if g is None else jnp.where(layer == j, gj, g)
        g_ref[...] = g
        nm = ADAM_B1 * m_ref[...] + (1.0 - ADAM_B1) * g
        nv = ADAM_B2 * v_ref[...] + (1.0 - ADAM_B2) * (g * g)
        nm_ref[...] = nm
        nv_ref[...] = nv
        d_ref[...] = -ADAM_LR * ((nm / c1) / (jnp.sqrt(nv / c2) + ADAM_EPS) + ADAM_WD * w_ref[...])

    def tile_of(j):
        return lambda la, i, q: jnp.where(la == j, i, jnp.where(la < j, 0, nt - 1))

    row = pl.BlockSpec((tr, l), lambda la, i, q: (la * nt + i, 0))
    own_specs = [] if own is None else [
        pl.BlockSpec((1, tr, l), lambda la, i, q, j=j: (q[0], tile_of(j)(la, i, q), 0)) for j in range(nl)]
    part_specs = [pl.BlockSpec((ns, tr, l), lambda la, i, q, j=j: (0, tile_of(j)(la, i, q), 0)) for j in range(nl)]
    chip = (2 * lax.axis_index("x") + lax.axis_index("y")).astype(jnp.int32).reshape(1)
    return pl.pallas_call(
        body, name=name,
        grid_spec=pltpu.PrefetchScalarGridSpec(
            num_scalar_prefetch=1, grid=(nl, nt),
            in_specs=own_specs + part_specs + [row, row, row], out_specs=[row] * 4),
        out_shape=[jax.ShapeDtypeStruct((nl * rows, l), F32)] * 4,
        compiler_params=pltpu.CompilerParams(dimension_semantics=("arbitrary", "arbitrary"), vmem_limit_bytes=VMEM_LIMIT),
    )(chip, *([] if own is None else own), *parts, w, m, v)


_HBM = pl.BlockSpec(memory_space=pltpu.HBM)
_SEM = pl.BlockSpec(memory_space=pltpu.SEMAPHORE)
_EFFECT = pltpu.SideEffectType.DATAFLOW_SIDE_EFFECTING


def _plan_all(x, y, c):
    me = 4 * x + 2 * y + c
    peers = [(x, y, 1 - c), (1 - x, y, c), (x, 1 - y, c), (1 - x, 1 - y, c),
             (1 - x, y, 1 - c), (x, 1 - y, 1 - c), (1 - x, 1 - y, 1 - c)]
    return [(None, me, p, 4 * p[0] + 2 * p[1] + p[2]) for p in peers]


def _plan_near(x, y, c):
    me = 4 * x + 2 * y + c
    peers = [(x, y, 1 - c), (1 - x, y, c), (x, 1 - y, c), (1 - x, 1 - y, c)]
    return [(None, me, p, 4 * p[0] + 2 * p[1] + p[2]) for p in peers]


def _plan_sibling(x, y, c):
    return [(2 * q + 1 - c, q, (x, y, 1 - c), q) for q in range(4)]


def _plan_chips(x, y, c):
    me = 2 * x + y
    return [(2 * qx + qy, me, (qx, qy, c), 2 * qx + qy) for qx, qy in ((1 - x, y), (x, 1 - y), (1 - x, 1 - y))]


def _split_copies(plan, src_refs, land_refs, send_sems, recv_sems, arrival):
    n = len(src_refs)
    entries = plan(*_place())
    per = len(entries)
    cps = []
    for a in range(n):
        for k, (src_slot, dst_slot, peer, back_slot) in enumerate(entries):
            src = src_refs[a] if src_slot is None else src_refs[a].at[src_slot]
            cps.append(pltpu.make_async_remote_copy(
                src_ref=src, dst_ref=land_refs[a].at[back_slot if arrival else dst_slot],
                send_sem=send_sems.at[per * a + k], recv_sem=recv_sems.at[per * a + k],
                device_id=peer, device_id_type=MESH))
    return cps


def _split_start(srcs, lands, plan, per, name):
    n = len(srcs)

    def body(*refs):
        for cp in _split_copies(plan, refs[:n], refs[n:2 * n], refs[2 * n], refs[2 * n + 1], False):
            cp.start()
        refs[-1][...] = jnp.zeros_like(refs[-1])

    both = list(srcs) + list(lands)
    outs = pl.pallas_call(
        body, name=name,
        out_shape=(pltpu.SemaphoreType.DMA((per * n,)), pltpu.SemaphoreType.DMA((per * n,)),
                   *[pltpu.HBM(v.shape, v.dtype) for v in both], jax.ShapeDtypeStruct((8, 128), F32)),
        in_specs=[_HBM] * (2 * n),
        out_specs=(_SEM, _SEM, *[_HBM] * (2 * n), pl.BlockSpec(memory_space=pltpu.VMEM)),
        input_output_aliases={i: 2 + i for i in range(2 * n)},
        compiler_params=pltpu.CompilerParams(has_side_effects=_EFFECT),
    )(*[pltpu.with_memory_space_constraint(v, pltpu.HBM) for v in both])
    return outs[0], outs[1], list(outs[2:2 + 2 * n]), outs[-1]


def _split_wait(send_sems, recv_sems, thru, plan, after, name):
    n = len(thru) // 2

    def body(*refs):
        for cp in _split_copies(plan, refs[:n], refs[n:2 * n], refs[2 * n], refs[2 * n + 1], True):
            cp.wait_send()
            cp.wait_recv()

    outs = pl.pallas_call(
        body, name=name, out_shape=tuple(pltpu.HBM(v.shape, v.dtype) for v in thru),
        in_specs=[_HBM] * (2 * n) + [_SEM, _SEM, pl.BlockSpec(memory_space=pl.ANY)],
        out_specs=[_HBM] * (2 * n), input_output_aliases={i: i for i in range(2 * n)},
        compiler_params=pltpu.CompilerParams(has_side_effects=_EFFECT),
    )(*thru, send_sems, recv_sems, after)
    return list(outs[:n]), list(outs[n:])


_SMALL_IN = ("ln_in_g", "ln_in_b")
_SMALL_ROWS = ("w_s", "b_s", "sinks")
_SMALL_LANES = ("b_in", "vn_g", "vn_b", "b_out", "ln_g", "ln_b")


def _tile_rows(a):
    return -(-a.size // 1024) * 8


def _pack_small(d, names):
    return jnp.concatenate([jnp.pad(d[n].reshape(-1), (0, (-d[n].size) % 1024)).reshape(-1, 128) for n in names])


def _adamw_small(parts, own, w, m, v, pieces, axis, name):
    c1 = 1.0 - ADAM_B1 ** ADAM_STEP
    c2 = 1.0 - ADAM_B2 ** ADAM_STEP
    shapes = [tuple(p if d == axis else s for d, s in enumerate(w.shape)) for p in pieces]

    def body(q_ref, own_ref, p_ref, w_ref, m_ref, v_ref, *o_refs):
        g = None
        for k in range(4):
            term = jnp.where(q_ref[0] == k, own_ref[0], p_ref[k])
            g = term if g is None else g + term
        nm = ADAM_B1 * m_ref[...] + (1.0 - ADAM_B1) * g
        nv = ADAM_B2 * v_ref[...] + (1.0 - ADAM_B2) * (g * g)
        delta = -ADAM_LR * ((nm / c1) / (jnp.sqrt(nv / c2) + ADAM_EPS) + ADAM_WD * w_ref[...])
        for k, val in enumerate((g, delta, nm, nv)):
            off = 0
            for j, p in enumerate(pieces):
                o_refs[k * len(pieces) + j][...] = val[off:off + p] if axis == 0 else val[:, off:off + p]
                off += p

    whole = pl.BlockSpec(w.shape, lambda i, q: (0, 0))
    chip = (2 * lax.axis_index("x") + lax.axis_index("y")).astype(jnp.int32).reshape(1)
    outs = pl.pallas_call(
        body, name=name,
        grid_spec=pltpu.PrefetchScalarGridSpec(
            num_scalar_prefetch=1, grid=(1,),
            in_specs=[pl.BlockSpec((1,) + w.shape, lambda i, q: (q[0], 0, 0)),
                      pl.BlockSpec((4,) + w.shape, lambda i, q: (0, 0, 0)), whole, whole, whole],
            out_specs=[pl.BlockSpec(s, lambda i, q: (0, 0)) for s in shapes] * 4),
        out_shape=[jax.ShapeDtypeStruct(s, F32) for s in shapes] * 4, compiler_params=_ARB,
    )(chip, own, parts, w, m, v)
    return [outs[k * len(pieces):(k + 1) * len(pieces)] for k in range(4)]


def _owner_blocks(g, axis):
    sh = g.shape
    g = g.reshape(sh[:axis] + (4, 2, sh[axis] // N_DEV) + sh[axis + 1:])
    return jnp.moveaxis(g, (axis, axis + 1), (0, 1))


def kernel(x, ln_in_g, ln_in_b, w_in, b_in, sinks, vn_g, vn_b, w_s, b_s, p_a, p_b, w_out, b_out, ln_g, ln_b, loss_target, m_ln_in_g, m_ln_in_b, m_w_in, m_b_in, m_sinks, m_vn_g, m_vn_b, m_w_s, m_b_s, m_p_a, m_p_b, m_w_out, m_b_out, m_ln_g, m_ln_b, v_ln_in_g, v_ln_in_b, v_w_in, v_b_in, v_sinks, v_vn_g, v_vn_b, v_w_s, v_b_s, v_p_a, v_p_b, v_w_out, v_b_out, v_ln_g, v_ln_b):
    nseq, seq, _ = x.shape
    t = nseq * seq
    nblk_seq = seq // BLK
    x2 = x.reshape(t, D)
    tgt = loss_target.reshape(t, D)

    def turned(a):
        return jnp.swapaxes(a, 1, 2)

    w_in_t = turned(w_in)

    def blocks(l):
        return [w_in_t[l].astype(BF16), p_a[l].astype(BF16), p_b[l].astype(BF16), w_out[l].astype(BF16)]

    def full_weights(g):
        w_t_full = g[0].reshape(IN_COLS, D)
        pa_full = jnp.moveaxis(g[1], 0, 1).reshape(Q_W, D)
        pb_full = jnp.moveaxis(g[2], 0, 1).reshape(SGU_W, D)
        wo_full = g[3].reshape(D, D)
        return dict(w_t=w_t_full, pa=pa_full, pb=pb_full, wo=wo_full)

    def landing(bs):
        return [lax.empty((N_DEV,) + v.shape, v.dtype) for v in bs]

    def with_own(landed, sent):
        return [lax.dynamic_update_index_in_dim(g, b, me, 0) for g, b in zip(landed, sent)]

    me = 4 * lax.axis_index("x") + 2 * lax.axis_index("y") + lax.axis_index("c")
    blocks0 = blocks(0)
    a_send, a_recv, a_thru, a_token = _split_start(blocks0[:1], landing(blocks0[:1]), _plan_near, 4,
                                                   "allgather_w_in0_start")
    rest0 = [b + a_token[0, 0].astype(BF16) for b in blocks0[1:]]
    b_send, b_recv, b_thru, b_token = _split_start(rest0, landing(rest0), _plan_all, 7, "allgather_rest0_start")
    xs = [_ln_fwd(x2, ln_in_g + b_token[0, 0], ln_in_b, "ln_in_fwd")]
    sent, landed = _split_wait(a_send, a_recv, a_thru, _plan_near, xs[0], "allgather_w_in0_wait")
    gathered0 = with_own(_forward_sibling(landed, "allgather_w_in0_forward"), sent)
    blocks1, gathered0 = lax.optimization_barrier((blocks(1), gathered0))
    ag_send, ag_recv, ag_thru, ag_token = _split_start(blocks1, landing(blocks1), _plan_all, 7,
                                                       "allgather_weights1_start")
    weights = [None, None]
    bsb = jnp.broadcast_to(b_s[:, :, :, None], (DEPTH, 4, BLK, BLK))
    bias = _band_bias()

    saved = []
    for l in range(DEPTH):
        if l == 1:
            sent, landed = _split_wait(ag_send, ag_recv, ag_thru, _plan_all, xs[1], "allgather_weights1_wait")
            weights[1] = full_weights(with_own(landed, sent))
        w_t = weights[l]["w_t"] if l else gathered0[0].reshape(IN_COLS, D)
        last = l == DEPTH - 1
        b_l = b_in[l].reshape(1, -1) + (ag_token[0, 0] if l == 0 else 0.0)
        hm, hr = _inproj(xs[l], w_t, b_l, f"inproj{l}")
        ya, yb, prob, psink = _mixer_fwd(hm, sinks[l], bias, vn_g[l].reshape(1, -1), vn_b[l].reshape(1, -1),
                                         w_s[l], bsb[l], nblk_seq, f"mixer_fwd{l}")
        if l == 0:
            sent, landed = _split_wait(b_send, b_recv, b_thru, _plan_all, ya, "allgather_rest0_wait")
            weights[0] = full_weights(gathered0 + with_own(landed, sent))
        wl = weights[l]
        outs = _tail_fwd(xs[l], ya, yb, hr, wl["pa"], wl["pb"], wl["wo"], b_out[l].reshape(1, D),
                         ln_g[l].reshape(1, D), ln_b[l].reshape(1, D), f"tail_fwd{l}", last)
        saved.append((hm, hr, ya, yb, prob, psink) + tuple(outs[:4]))
        if not last:
            xs.append(outs[4])

    small = {n: [None] * DEPTH for n in _SMALL_ROWS + _SMALL_LANES}

    def pack_rows(d):
        return _pack_small(d, _SMALL_ROWS)

    def pack_lanes(d):
        return jnp.concatenate([d[n] for n in _SMALL_LANES], axis=1)
    names = ("w_in", "p_a", "p_b", "w_out")
    owner_axis = {"w_in": 0, "p_a": 1, "p_b": 1, "w_out": 0}
    token = jnp.zeros((8, 128), F32)
    dx = tgt
    split = [None] * DEPTH
    for l in reversed(range(DEPTH)):
        hm, hr, ya, yb, prob, psink, pa, pb, merged, z = saved[l]
        wl = weights[l]
        dz, dpa, dpb, dhr, dya, dyb, acc, gbr = _tail_bwd(
            dx, z, pa, pb, hr, wl["wo"], wl["pa"], wl["pb"], ln_g[l].reshape(1, D) + token[0, 0],
            ln_b[l].reshape(1, D), f"tail_bwd{l}", l == DEPTH - 1)
        if l == DEPTH - 1:
            sq_err = acc[3:4, 0:128]
        dhm, gbm, gsk, gvn, gws, gbs = _mixer_bwd(
            hm, dya, dyb, prob, psink, vn_g[l].reshape(1, -1), vn_b[l].reshape(1, -1), w_s[l], bsb[l],
            f"mixer_bwd{l}")
        grads = {"w_in": _wgrad(dhr, xs[l], R_W // 2, f"wgrad_in_route{l}",
                                under=_wgrad(dhm, xs[l], MAIN_W // 2, f"wgrad_in_main{l}", rows=IN_COLS)),
                 "p_a": _wgrad(ya, dpa, Q_W, f"wgrad_pa{l}"), "p_b": _wgrad(yb, dpb, SGU_W, f"wgrad_pb{l}"),
                 "w_out": _wgrad(merged, dz, D, f"wgrad_out{l}")}
        small["b_in"][l] = jnp.concatenate([gbm[0], gbr[0]])
        small["sinks"][l] = gsk[:, 0]
        small["vn_g"][l], small["vn_b"][l] = gvn[0], gvn[1]
        small["w_s"][l], small["b_s"][l] = gws, gbs[:, :, 0]
        small["ln_g"][l], small["ln_b"][l], small["b_out"][l] = acc[0], acc[1], acc[2]
        parts = [_owner_blocks(grads[n], owner_axis[n]) for n in names]
        if l == 0:
            stacked = {n: jnp.stack(v) for n, v in small.items()}
            for packed in (pack_rows(stacked), pack_lanes(stacked)):
                parts.append(jnp.broadcast_to(packed[None, None], (1, 2) + packed.shape))
        if l == 0:
            from_sib = _swap_sibling(parts, f"rs_sibling{l}")
        else:
            halves = [p.reshape((N_DEV,) + p.shape[2:]) for p in parts]
            sib = _split_start(halves, [lax.empty((4,) + p.shape[2:], p.dtype) for p in parts], _plan_sibling, 4,
                               f"rs_sibling{l}_start")
            dx = _dx_inproj(dz, dhm, dhr, wl["w_t"], sib[3], f"dx_inproj{l}")
            halves, from_sib = _split_wait(sib[0], sib[1], sib[2], _plan_sibling, dx, f"rs_sibling{l}_wait")
            parts = [h.reshape(p.shape) for h, p in zip(halves, parts)]
        pair = list(_pair_sums(parts, from_sib, f"pair_sums{l}"))
        if l == 0:
            pair[4:] = [jnp.broadcast_to(p, (4,) + p.shape[1:]) for p in pair[4:]]
        lands = [jnp.zeros(p.shape, p.dtype) for p in pair]
        split[l] = _split_start(pair, lands, _plan_chips, 3, f"rs_chips{l}_start")
        token = split[l][3]
        if l == 0:
            grad_x, acc_in = _dx_inproj(dz, dhm, dhr, wl["w_t"], token, f"dx_inproj{l}",
                                        ln_in=(x2, ln_in_g.reshape(1, D)))
    last = [acc_in, jnp.broadcast_to(sq_err, (8, 128))]
    ln_send, ln_recv, ln_thru, ln_token = _split_start(last, landing(last), _plan_all, 7, "allgather_ln_in_start")

    given = {"w_in": (w_in_t, turned(m_w_in), turned(v_w_in)), "p_a": (p_a, m_p_a, v_p_a),
             "p_b": (p_b, m_p_b, v_p_b), "w_out": (w_out, m_w_out, v_w_out)}
    waited = [_split_wait(split[l][0], split[l][1], split[l][2], _plan_chips, ln_token, f"rs_chips{l}_wait")
              for l in range(DEPTH)]
    res = {}
    for a, n in enumerate(names):
        rows, lanes = waited[0][1][a].shape[1:]
        outs = _adamw([waited[l][1][a] for l in range(DEPTH)], *[v.reshape(DEPTH * rows, lanes) for v in given[n]],
                      f"adamw_{n}", own=[waited[l][0][a] for l in range(DEPTH)])
        res[n] = [o.reshape(given[n][0].shape) for o in outs]
    res["w_in"] = [turned(o) for o in res["w_in"]]

    w_small = dict(ln_in_g=ln_in_g, ln_in_b=ln_in_b, b_in=b_in, sinks=sinks, vn_g=vn_g, vn_b=vn_b, w_s=w_s, b_s=b_s,
                   b_out=b_out, ln_g=ln_g, ln_b=ln_b)
    m_small = dict(ln_in_g=m_ln_in_g, ln_in_b=m_ln_in_b, b_in=m_b_in, sinks=m_sinks, vn_g=m_vn_g, vn_b=m_vn_b,
                   w_s=m_w_s, b_s=m_b_s, b_out=m_b_out, ln_g=m_ln_g, ln_b=m_ln_b)
    v_small = dict(ln_in_g=v_ln_in_g, ln_in_b=v_ln_in_b, b_in=v_b_in, sinks=v_sinks, vn_g=v_vn_g, vn_b=v_vn_b,
                   w_s=v_w_s, b_s=v_b_s, b_out=v_b_out, ln_g=v_ln_g, ln_b=v_ln_b)
    by_rows = _adamw_small(waited[0][1][4], waited[0][0][4], *[pack_rows(d) for d in (w_small, m_small, v_small)],
                           [_tile_rows(w_small[n]) for n in _SMALL_ROWS], 0, "adamw_small_rows")
    by_lanes = _adamw_small(waited[0][1][5], waited[0][0][5], *[pack_lanes(d) for d in (w_small, m_small, v_small)],
                            [w_small[n].shape[1] for n in _SMALL_LANES], 1, "adamw_small_lanes")
    sent, landed = _split_wait(ln_send, ln_recv, ln_thru, _plan_all, by_lanes[0][0], "allgather_ln_in_wait")
    all_in, all_sq = with_own(landed, sent)
    loss = jnp.sum(all_sq[:, 0, 0]) * (0.5 / D)
    outs_in = _adamw([all_in], *[jnp.pad(jnp.stack([d[n] for n in _SMALL_IN]), ((0, 6), (0, 0)))
                                 for d in (w_small, m_small, v_small)], "adamw_ln_in")
    for k in range(4):
        u = {n: o.reshape(-1)[:w_small[n].size].reshape(w_small[n].shape) for n, o in zip(_SMALL_ROWS, by_rows[k])}
        u.update(zip(_SMALL_LANES, by_lanes[k]))
        u.update({n: outs_in[k][r] for r, n in enumerate(_SMALL_IN)})
        for n in u:
            res.setdefault(n, [None] * 4)[k] = u[n]

    order = ("ln_in_g", "ln_in_b", "w_in", "b_in", "sinks", "vn_g", "vn_b", "w_s", "b_s", "p_a", "p_b", "w_out",
             "b_out", "ln_g", "ln_b")
    return (loss, grad_x.reshape(x.shape), *[res[n][0] for n in order], *[res[n][1] for n in order],
            *[res[n][2] for n in order], *[res[n][3] for n in order])
```

```python
import jax
import jax.numpy as jnp
from jax import lax
from jax.experimental import pallas as pl
from jax.experimental.pallas import tpu as pltpu

F32 = jnp.float32
BF16 = jnp.bfloat16

D = 1024
BLK = 128
N_KV = 2
Q_W, KV_W, SGU_W = 512, 128, 512
C_Q, C_K, C_V, C_GA, C_UB, C_VB, C_GB = 0, 512, 640, 768, 1280, 1792, 2304
MAIN_W = 2816
R_W = 2048
IN_COLS = MAIN_W + R_W
N_DEV = 8

DEPTH = 2
ALPHA = (2.0 * DEPTH) ** 0.25
LN_EPS = 1e-5
ATTN_SCALE = 0.125
NEG = float(jnp.finfo(jnp.float32).min)

ADAM_LR, ADAM_B1, ADAM_B2, ADAM_EPS, ADAM_WD, ADAM_STEP = 0.001, 0.9, 0.999, 1e-08, 0.01, 10

TM = 512
TM_EW = 1024
TM_MM = 512
TM_T = 512
NB = TM // BLK
MESH = pl.DeviceIdType.MESH
VMEM_LIMIT = 56 * 1024 * 1024

_ARB = pltpu.CompilerParams(dimension_semantics=("arbitrary",), vmem_limit_bytes=VMEM_LIMIT)


def _sigmoid(x):
    return 1.0 / (1.0 + jnp.exp(-x))


_GELU_C = 0.7978845608028654
_GELU_A = 0.044715


def _gelu_parts(x):
    x2 = x * x
    t = jnp.tanh(x * (_GELU_C + (_GELU_C * _GELU_A) * x2))
    hx = 0.5 * x
    return hx, t, x2


def _gelu(x):
    hx, t, _ = _gelu_parts(x)
    return hx + hx * t


def _gelu_and_grad(x):
    hx, t, x2 = _gelu_parts(x)
    grad = 0.5 + 0.5 * t + (hx - hx * (t * t)) * (_GELU_C + (3.0 * _GELU_C * _GELU_A) * x2)
    return hx + hx * t, grad


def _ln_stats(x):
    mu = jnp.mean(x, axis=-1, keepdims=True)
    xc = x - mu
    var = jnp.mean(xc * xc, axis=-1, keepdims=True)
    rstd = lax.rsqrt(var + LN_EPS)
    return xc * rstd, rstd


def _ln_bwd(dy_g, xhat, rstd):
    m1 = jnp.mean(dy_g, axis=-1, keepdims=True)
    m2 = jnp.mean(dy_g * xhat, axis=-1, keepdims=True)
    return rstd * (dy_g - m1 - xhat * m2)


def _colsum(x):
    return jnp.sum(x, axis=0, keepdims=True)


def _dot(a, b):
    return jnp.dot(a, b, preferred_element_type=F32)


def _dot_nt(a, b):
    return lax.dot_general(a, b, (((1,), (1,)), ((), ())), preferred_element_type=F32)


def _side_by_side(gathered_ref):
    return jnp.concatenate([gathered_ref[j] for j in range(N_DEV)], axis=1)


def _dot_tn(a, b):
    return lax.dot_general(a, b, (((0,), (0,)), ((), ())), preferred_element_type=F32)


def _head_place(hk, g):
    j = 4 * hk + g
    return j, j // 2, j % 2


def _head_rows(x, hk):
    d = lax.broadcasted_iota(jnp.int32, x.shape, 0)
    return jnp.where((d >= 64 * hk) & (d < 64 * hk + 64), x, 0.0).astype(BF16)


def _head_lanes(x, hk):
    d = lax.broadcasted_iota(jnp.int32, x.shape, 1)
    return jnp.where((d >= 64 * hk) & (d < 64 * hk + 64), x, 0.0)


def _band_bias():
    kpos = lax.broadcasted_iota(jnp.int32, (2 * BLK, 4 * BLK), 0)
    row = lax.broadcasted_iota(jnp.int32, (2 * BLK, 4 * BLK), 1) & (BLK - 1)
    band = (kpos > row) & (kpos <= row + BLK)
    return jnp.stack([jnp.where(band, 0.0, NEG), jnp.where(band & (kpos >= BLK), 0.0, NEG)]).astype(F32)


def _stack_q(q, hk):
    parts = []
    for g in range(4):
        _, p, pos = _head_place(hk, g)
        qp = q[:, BLK * p:BLK * (p + 1)] * ATTN_SCALE
        if pos != hk:
            qp = pltpu.roll(qp, 64, 1)
        parts.append(qp.astype(BF16))
    return jnp.concatenate(parts, axis=0)


def _attn_probs(q4, kh, hk, sinks_ref, bias):
    s_t = _dot_nt(kh, q4) + bias
    sink_row = jnp.concatenate(
        [jnp.full((1, BLK), sinks_ref[4 * hk + g], F32) for g in range(4)], axis=1)
    m = jnp.maximum(jnp.max(s_t, axis=0, keepdims=True), sink_row)
    p_un = jnp.exp(s_t - m)
    e_sink = jnp.exp(sink_row - m)
    inv = 1.0 / (jnp.sum(p_un, axis=0, keepdims=True) + e_sink)
    return (p_un * inv).astype(BF16), e_sink * inv


def _unstack_heads(x4, hk, pairs):
    for g in range(4):
        _, p, pos = _head_place(hk, g)
        xg = x4[BLK * g:BLK * (g + 1)]
        if pos != hk:
            xg = pltpu.roll(xg, 64, 1)
        pairs[p] = xg if pairs[p] is None else pairs[p] + xg
    return pairs


def _attn_fwd(q, kband, vband, sinks_ref, bias, save):
    pairs = [None] * 4
    vband_t = vband.T
    for hk in range(N_KV):
        prob_t, p_sink = _attn_probs(_stack_q(q, hk), _head_lanes(kband, hk).astype(BF16), hk, sinks_ref, bias)
        save(hk, prob_t, p_sink)
        o_t = _dot(_head_rows(vband_t, hk), prob_t)
        pairs = _unstack_heads(o_t.T, hk, pairs)
    return jnp.concatenate(pairs, axis=1)


def _tril_mask():
    r = lax.broadcasted_iota(jnp.int32, (BLK, BLK), 0)
    c = lax.broadcasted_iota(jnp.int32, (BLK, BLK), 1)
    return c <= r


def _sgu_fwd(u, v, vn_g, vn_b, wt, bsb_ref):
    vhat, rstd = _ln_stats(v)
    vn = vhat * vn_g + vn_b
    mixed = jnp.concatenate(
        [_dot(wt[g], vn[:, BLK * g:BLK * (g + 1)].astype(BF16)) + bsb_ref[g] for g in range(4)], axis=1)
    return vhat, rstd, vn, mixed


def _cols(ref, rows, col, width):
    return ref[rows, col:col + width].astype(F32)


def _band(hm_ref, hprev_ref, s, col):
    r0 = s * BLK
    cur = hm_ref[r0:r0 + BLK, col:col + KV_W]
    if s == 0:
        off = 0 if col == C_K else KV_W
        prev = hprev_ref[:, off:off + KV_W]
    else:
        prev = hm_ref[r0 - BLK:r0, col:col + KV_W]
    return jnp.concatenate([prev, cur], axis=0).astype(F32)


def _h_main_specs(nt, rev):
    def tile(g):
        return nt - 1 - g if rev else g

    return [pl.BlockSpec((TM, MAIN_W), lambda g: (tile(g), 0)),
            pl.BlockSpec((BLK, 2 * KV_W), lambda g: (jnp.maximum(tile(g) * NB - 1, 0), 2))]


_CONST2 = lambda g: (0, 0)
_CONST3 = lambda g: (0, 0, 0)


def _ln_fwd(x, g, b, name):
    t = x.shape[0]

    def body(x_ref, g_ref, b_ref, o_ref):
        xhat, _ = _ln_stats(x_ref[...])
        o_ref[...] = xhat * g_ref[...] + b_ref[...]

    return pl.pallas_call(
        body, name=name, grid=(t // TM_EW,),
        in_specs=[pl.BlockSpec((TM_EW, D), lambda i: (i, 0)), pl.BlockSpec((1, D), _CONST2),
                  pl.BlockSpec((1, D), _CONST2)],
        out_specs=pl.BlockSpec((TM_EW, D), lambda i: (i, 0)),
        out_shape=jax.ShapeDtypeStruct((t, D), F32), compiler_params=_ARB,
    )(x, g.reshape(1, D), b.reshape(1, D))


def _inproj(x, w_t, b, name):
    t = x.shape[0]

    def body(x_ref, wt_ref, b_ref, hm_ref, hr_ref):
        xb = x_ref[...].astype(BF16)
        hm_ref[...] = (_dot_nt(xb, wt_ref[0:MAIN_W, :]) + b_ref[:, 0:MAIN_W]).astype(BF16)
        hr_ref[...] = (_dot_nt(xb, wt_ref[MAIN_W:IN_COLS, :]) + b_ref[:, MAIN_W:IN_COLS]).astype(BF16)

    return pl.pallas_call(
        body, name=name, grid=(t // TM_MM,),
        in_specs=[pl.BlockSpec((TM_MM, D), lambda i: (i, 0)),
                  pl.BlockSpec((IN_COLS, D), _CONST2), pl.BlockSpec((1, IN_COLS), _CONST2)],
        out_specs=[pl.BlockSpec((TM_MM, MAIN_W), lambda i: (i, 0)), pl.BlockSpec((TM_MM, R_W), lambda i: (i, 0))],
        out_shape=[jax.ShapeDtypeStruct((t, MAIN_W), BF16), jax.ShapeDtypeStruct((t, R_W), BF16)],
        compiler_params=_ARB,
    )(x, w_t, b)


def _mixer_fwd(hm, sinks, bias, vn_g, vn_b, w_s, bsb, nblk_seq, name):
    t = hm.shape[0]
    nt = t // TM

    def body(sinks_ref, hm_ref, hprev_ref, bias_ref, vng_ref, vnb_ref, ws_ref, bsb_ref,
             ya_ref, yb_ref, prob_ref, psink_ref):
        i = pl.program_id(0)
        tril = _tril_mask()
        wt = [jnp.where(tril, ws_ref[g], 0.0).astype(BF16) for g in range(4)]
        for s in range(NB):
            r0 = s * BLK
            rows = slice(r0, r0 + BLK)
            bias = bias_ref[jnp.where((i * NB + s) % nblk_seq == 0, 1, 0)]

            def save(hk, prob_t, p_sink, s=s):
                prob_ref[N_KV * s + hk] = prob_t
                psink_ref[N_KV * s + hk] = jnp.broadcast_to(p_sink, (8, 4 * BLK))

            attn = _attn_fwd(_cols(hm_ref, rows, C_Q, Q_W), _band(hm_ref, hprev_ref, s, C_K),
                             _band(hm_ref, hprev_ref, s, C_V), sinks_ref, bias, save)
            g_a = _cols(hm_ref, rows, C_GA, Q_W)
            ya_ref[rows, :] = (attn * (g_a * _sigmoid(g_a))).astype(BF16)
            u = _gelu(_cols(hm_ref, rows, C_UB, SGU_W))
            mixed = _sgu_fwd(u, _gelu(_cols(hm_ref, rows, C_VB, SGU_W)), vng_ref[...], vnb_ref[...], wt, bsb_ref)[-1]
            g_b = _cols(hm_ref, rows, C_GB, SGU_W)
            yb_ref[rows, :] = (u * mixed * (g_b * _sigmoid(g_b))).astype(BF16)

    ngrp = N_KV * NB
    return pl.pallas_call(
        body, name=name, grid=(nt,),
        in_specs=[pl.BlockSpec(memory_space=pltpu.SMEM)] + _h_main_specs(nt, False) + [
            pl.BlockSpec((2, 2 * BLK, 4 * BLK), _CONST3),
            pl.BlockSpec((1, SGU_W), _CONST2), pl.BlockSpec((1, SGU_W), _CONST2),
            pl.BlockSpec((4, BLK, BLK), _CONST3), pl.BlockSpec((4, BLK, BLK), _CONST3)],
        out_specs=[pl.BlockSpec((TM, Q_W), lambda i: (i, 0)), pl.BlockSpec((TM, SGU_W), lambda i: (i, 0)),
                   pl.BlockSpec((ngrp, 2 * BLK, 4 * BLK), lambda i: (i, 0, 0)),
                   pl.BlockSpec((ngrp, 8, 4 * BLK), lambda i: (i, 0, 0))],
        out_shape=[jax.ShapeDtypeStruct((t, Q_W), BF16), jax.ShapeDtypeStruct((t, SGU_W), BF16),
                   jax.ShapeDtypeStruct((nt * ngrp, 2 * BLK, 4 * BLK), BF16),
                   jax.ShapeDtypeStruct((nt * ngrp, 8, 4 * BLK), F32)],
        compiler_params=_ARB,
    )(sinks, hm, hm, bias, vn_g, vn_b, w_s, bsb)


def _tail_fwd(x, ya, yb, hr, pa_w, pb_w, wo, b_out, ln_g, ln_b, name, last):
    t = x.shape[0]

    def body(x_ref, ya_ref, yb_ref, hr_ref, paw_ref, pbw_ref, wo_ref, bo_ref, g_ref, b_ref,
             pa_ref, pb_ref, mg_ref, z_ref, *xn_ref):
        pa = _dot(ya_ref[...], _side_by_side(paw_ref))
        pb = _dot(yb_ref[...], _side_by_side(pbw_ref))
        pa_ref[...] = pa.astype(BF16)
        pb_ref[...] = pb.astype(BF16)
        everything = slice(None)
        merged = _sigmoid(_cols(hr_ref, everything, 0, D)) * pa + _sigmoid(_cols(hr_ref, everything, D, D)) * pb
        mb = merged.astype(BF16)
        mg_ref[...] = mb
        z = ALPHA * x_ref[...] + (_dot(mb, wo_ref[...]) + bo_ref[...])
        z_ref[...] = z
        if not last:
            zhat, _ = _ln_stats(z)
            xn_ref[0][...] = zhat * g_ref[...] + b_ref[...]

    row = lambda w: pl.BlockSpec((TM_T, w), lambda i: (i, 0))
    vec = pl.BlockSpec((1, D), _CONST2)
    n_f32 = 1 if last else 2
    return pl.pallas_call(
        body, name=name, grid=(t // TM_T,),
        in_specs=[row(D), row(Q_W), row(SGU_W), row(R_W),
                  pl.BlockSpec((N_DEV, Q_W, 128), _CONST3), pl.BlockSpec((N_DEV, SGU_W, 128), _CONST3),
                  pl.BlockSpec((D, D), _CONST2), vec, vec, vec],
        out_specs=[row(D)] * (3 + n_f32),
        out_shape=[jax.ShapeDtypeStruct((t, D), BF16)] * 3 + [jax.ShapeDtypeStruct((t, D), F32)] * n_f32,
        compiler_params=_ARB,
    )(x, ya, yb, hr, pa_w, pb_w, wo, b_out, ln_g, ln_b)


def _tail_bwd(dxn, z, pa, pb, hr, wo, pa_w, pb_w, ln_g, ln_b, name, from_loss):
    t = dxn.shape[0]

    def body(dxn_ref, z_ref, pa_ref, pb_ref, hr_ref, wo_ref, paw_ref, pbw_ref, g_ref, b_ref,
             dz_ref, dpa_ref, dpb_ref, dhr_ref, dya_ref, dyb_ref, acc_ref, gbr_ref):
        @pl.when(pl.program_id(0) == 0)
        def _():
            acc_ref[...] = jnp.zeros_like(acc_ref)
            gbr_ref[...] = jnp.zeros_like(gbr_ref)

        zhat, rstd = _ln_stats(z_ref[...])
        if from_loss:
            err = zhat * g_ref[...] + b_ref[...] - dxn_ref[...]
            dxn_v = err * (1.0 / D)
            sq = jnp.sum(jnp.sum(err * err, axis=1, keepdims=True), axis=0, keepdims=True)
            acc_ref[3:4, :] += jnp.broadcast_to(sq, (1, D))
        else:
            dxn_v = dxn_ref[...]
        dz = _ln_bwd(dxn_v * g_ref[...], zhat, rstd)
        dz_ref[...] = dz
        acc_ref[0:1, :] += _colsum(dxn_v * zhat)
        acc_ref[1:2, :] += _colsum(dxn_v)
        acc_ref[2:3, :] += _colsum(dz)
        dmerged = _dot_nt(dz.astype(BF16), wo_ref[...])
        everything = slice(None)
        sa = _sigmoid(_cols(hr_ref, everything, 0, D))
        sb = _sigmoid(_cols(hr_ref, everything, D, D))
        dpa = (dmerged * sa).astype(BF16)
        dpb = (dmerged * sb).astype(BF16)
        dpa_ref[...] = dpa
        dpb_ref[...] = dpb
        dra = dmerged * pa_ref[...].astype(F32) * (sa * (1.0 - sa))
        drb = dmerged * pb_ref[...].astype(F32) * (sb * (1.0 - sb))
        dhr_ref[:, 0:D] = dra.astype(BF16)
        dhr_ref[:, D:2 * D] = drb.astype(BF16)
        gbr_ref[0:1, 0:D] += _colsum(dra)
        gbr_ref[0:1, D:2 * D] += _colsum(drb)
        dya_ref[...] = _dot_nt(dpa, _side_by_side(paw_ref)).astype(BF16)
        dyb_ref[...] = _dot_nt(dpb, _side_by_side(pbw_ref)).astype(BF16)

    row = lambda w: pl.BlockSpec((TM_T, w), lambda i: (i, 0))
    vec = pl.BlockSpec((1, D), _CONST2)
    return pl.pallas_call(
        body, name=name, grid=(t // TM_T,),
        in_specs=[row(D), row(D), row(D), row(D), row(R_W),
                  pl.BlockSpec((D, D), _CONST2), pl.BlockSpec((N_DEV, Q_W, 128), _CONST3),
                  pl.BlockSpec((N_DEV, SGU_W, 128), _CONST3), vec, vec],
        out_specs=[row(D), row(D), row(D), row(R_W), row(Q_W), row(SGU_W), pl.BlockSpec((8, D), _CONST2),
                   pl.BlockSpec((8, R_W), _CONST2)],
        out_shape=[jax.ShapeDtypeStruct((t, D), F32), jax.ShapeDtypeStruct((t, D), BF16),
                   jax.ShapeDtypeStruct((t, D), BF16), jax.ShapeDtypeStruct((t, R_W), BF16),
                   jax.ShapeDtypeStruct((t, Q_W), BF16), jax.ShapeDtypeStruct((t, SGU_W), BF16),
                   jax.ShapeDtypeStruct((8, D), F32), jax.ShapeDtypeStruct((8, R_W), F32)],
        compiler_params=_ARB,
    )(dxn, z, pa, pb, hr, wo, pa_w, pb_w, ln_g, ln_b)


def _mixer_bwd(hm, dya, dyb, prob, psink, vn_g, vn_b, w_s, bsb, name):
    t = hm.shape[0]
    nt = t // TM
    ngrp = N_KV * NB

    def body(hm_ref, hprev_ref, prob_ref, psink_ref, dya_ref, dyb_ref, vng_ref, vnb_ref, ws_ref, bsb_ref,
             dhm_ref, gbm_ref, gsk_ref, gvn_ref, gws_ref, gbs_ref, dk_carry, dv_carry):
        gi = pl.program_id(0)

        @pl.when(gi == 0)
        def _():
            for r in (gbm_ref, gsk_ref, gvn_ref, gws_ref, gbs_ref, dk_carry, dv_carry):
                r[...] = jnp.zeros_like(r)

        tril = _tril_mask()
        wt = [jnp.where(tril, ws_ref[g], 0.0).astype(BF16) for g in range(4)]
        vng = vng_ref[...]
        ones8 = jnp.ones((8, BLK), BF16)

        def put(rows, col, val):
            dhm_ref[rows, col:col + val.shape[1]] = val.astype(BF16)

        for s in reversed(range(NB)):
            r0 = s * BLK
            rows = slice(r0, r0 + BLK)
            q = _cols(hm_ref, rows, C_Q, Q_W)
            kband = _band(hm_ref, hprev_ref, s, C_K)
            vband = _band(hm_ref, hprev_ref, s, C_V)
            g_a = _cols(hm_ref, rows, C_GA, Q_W)
            sg = _sigmoid(g_a)
            dya_v = _cols(dya_ref, rows, 0, Q_W)
            d_o = dya_v * (g_a * sg)
            o_pairs, dq_pairs = [None] * 4, [None] * 4
            dkband = jnp.zeros((2 * BLK, KV_W), F32)
            dvband = jnp.zeros((2 * BLK, KV_W), F32)
            kband_t, vband_t = kband.T, vband.T
            for hk in range(N_KV):
                q4 = _stack_q(q, hk)
                prob_b = prob_ref[N_KV * s + hk]
                p_sink = psink_ref[N_KV * s + hk][0:1, :]
                o_t = _dot(_head_rows(vband_t, hk), prob_b)
                o_pairs = _unstack_heads(o_t.T, hk, o_pairs)
                parts = []
                for g in range(4):
                    _, p, pos = _head_place(hk, g)
                    dp = d_o[:, BLK * p:BLK * (p + 1)]
                    parts.append(pltpu.roll(dp, 64, 1) if pos != hk else dp)
                do4 = _head_lanes(jnp.concatenate(parts, axis=0), hk)
                do4b = do4.astype(BF16)
                delta = _colsum(do4.T * o_t)
                vh = _head_lanes(vband, hk).astype(BF16)
                ds_t = prob_b.astype(F32) * (_dot_nt(vh, do4b) - delta)
                dsb = ds_t.astype(BF16)
                dq4_t = _dot(_head_rows(kband_t, hk), dsb)
                dq_pairs = _unstack_heads(dq4_t.T * ATTN_SCALE, hk, dq_pairs)
                dkband = dkband + _head_lanes(_dot(dsb, q4), hk)
                dvband = dvband + _dot(prob_b, do4b)
                dsk = p_sink * delta
                for g in range(4):
                    j = 4 * hk + g
                    tot = jnp.sum(dsk[:, BLK * g:BLK * (g + 1)], axis=1, keepdims=True)
                    gsk_ref[j:j + 1, :] += jnp.broadcast_to(-tot, (1, 128))
            attn = jnp.concatenate(o_pairs, axis=1)
            put(rows, C_Q, jnp.concatenate(dq_pairs, axis=1))
            put(rows, C_K, dkband[BLK:2 * BLK] + dk_carry[...])
            put(rows, C_V, dvband[BLK:2 * BLK] + dv_carry[...])
            dk_carry[...] = dkband[0:BLK]
            dv_carry[...] = dvband[0:BLK]
            put(rows, C_GA, dya_v * attn * (sg * (1.0 + g_a * (1.0 - sg))))
            u, du_du_b = _gelu_and_grad(_cols(hm_ref, rows, C_UB, SGU_W))
            v, dv_dv_b = _gelu_and_grad(_cols(hm_ref, rows, C_VB, SGU_W))
            g_b = _cols(hm_ref, rows, C_GB, SGU_W)
            vhat, rstd, vn, mixed = _sgu_fwd(u, v, vng, vnb_ref[...], wt, bsb_ref)
            sgb = _sigmoid(g_b)
            silu_b = g_b * sgb
            dyb_v = _cols(dyb_ref, rows, 0, SGU_W)
            du = dyb_v * mixed * silu_b
            dmixed = dyb_v * u * silu_b
            put(rows, C_GB, dyb_v * u * mixed * (sgb * (1.0 + g_b * (1.0 - sgb))))
            dvn_parts = []
            for g in range(4):
                cols = slice(BLK * g, BLK * (g + 1))
                dmg = dmixed[:, cols]
                dmgb = dmg.astype(BF16)
                dvn_parts.append(_dot_tn(wt[g], dmgb))
                gws_ref[g] += jnp.where(tril, _dot_nt(dmgb, vn[:, cols].astype(BF16)), 0.0)
                gbs_ref[g] += dmg
            dvn = jnp.concatenate(dvn_parts, axis=1)
            gvn_ref[0:1, :] += _colsum(dvn * vhat)
            gvn_ref[1:2, :] += _colsum(dvn)
            dv = _ln_bwd(dvn * vng, vhat, rstd)
            put(rows, C_UB, du * du_du_b)
            put(rows, C_VB, dv * dv_dv_b)
            gbm_ref[...] += _dot(ones8, dhm_ref[rows, :])

        @pl.when(gi == nt - 1)
        def _():
            for g in range(4):
                gbs_ref[g] = jnp.broadcast_to(jnp.sum(gbs_ref[g], axis=1, keepdims=True), (BLK, BLK))

    row = lambda w: pl.BlockSpec((TM, w), lambda g: (nt - 1 - g, 0))
    return pl.pallas_call(
        body, name=name, grid=(nt,),
        in_specs=_h_main_specs(nt, True) + [
            pl.BlockSpec((ngrp, 2 * BLK, 4 * BLK), lambda g: (nt - 1 - g, 0, 0)),
            pl.BlockSpec((ngrp, 8, 4 * BLK), lambda g: (nt - 1 - g, 0, 0)),
            row(Q_W), row(SGU_W),
            pl.BlockSpec((1, SGU_W), _CONST2), pl.BlockSpec((1, SGU_W), _CONST2),
            pl.BlockSpec((4, BLK, BLK), _CONST3), pl.BlockSpec((4, BLK, BLK), _CONST3)],
        out_specs=[row(MAIN_W), pl.BlockSpec((8, MAIN_W), _CONST2), pl.BlockSpec((8, 128), _CONST2),
                   pl.BlockSpec((8, SGU_W), _CONST2), pl.BlockSpec((4, BLK, BLK), _CONST3),
                   pl.BlockSpec((4, BLK, BLK), _CONST3)],
        out_shape=[jax.ShapeDtypeStruct((t, MAIN_W), BF16), jax.ShapeDtypeStruct((8, MAIN_W), F32),
                   jax.ShapeDtypeStruct((8, 128), F32), jax.ShapeDtypeStruct((8, SGU_W), F32),
                   jax.ShapeDtypeStruct((4, BLK, BLK), F32), jax.ShapeDtypeStruct((4, BLK, BLK), F32)],
        scratch_shapes=[pltpu.VMEM((BLK, KV_W), F32), pltpu.VMEM((BLK, KV_W), F32)],
        compiler_params=_ARB,
    )(hm, hm, prob, psink, dya, dyb, vn_g, vn_b, w_s, bsb)


def _dx_inproj(dz, dhm, dhr, w_t, after, name, ln_in=None):
    t = dz.shape[0]

    def body(dz_ref, dhm_ref, dhr_ref, wt_ref, after_ref, *rest):
        dx = (ALPHA * dz_ref[...] + after_ref[0:1, 0:1] + _dot(dhm_ref[...], wt_ref[0:MAIN_W, :])
              + _dot(dhr_ref[...], wt_ref[MAIN_W:IN_COLS, :]))
        if ln_in is None:
            rest[0][...] = dx
            return
        x_ref, g_ref, gx_ref, acc_ref = rest

        @pl.when(pl.program_id(0) == 0)
        def _():
            acc_ref[...] = jnp.zeros_like(acc_ref)

        xhat, rstd = _ln_stats(x_ref[...])
        gx_ref[...] = _ln_bwd(dx * g_ref[...], xhat, rstd)
        acc_ref[0:1, :] += _colsum(dx * xhat)
        acc_ref[1:2, :] += _colsum(dx)

    row = lambda w: pl.BlockSpec((TM_MM, w), lambda i: (i, 0))
    in_specs = [row(D), row(MAIN_W), row(R_W), pl.BlockSpec((IN_COLS, D), _CONST2), pl.BlockSpec((8, 128), _CONST2)]
    if ln_in is None:
        return pl.pallas_call(
            body, name=name, grid=(t // TM_MM,), in_specs=in_specs,
            out_specs=row(D), out_shape=jax.ShapeDtypeStruct((t, D), F32), compiler_params=_ARB,
        )(dz, dhm, dhr, w_t, after)
    return pl.pallas_call(
        body, name=name, grid=(t // TM_MM,), in_specs=in_specs + [row(D), pl.BlockSpec((1, D), _CONST2)],
        out_specs=[row(D), pl.BlockSpec((8, D), _CONST2)],
        out_shape=[jax.ShapeDtypeStruct((t, D), F32), jax.ShapeDtypeStruct((8, D), F32)], compiler_params=_ARB,
    )(dz, dhm, dhr, w_t, after, *ln_in)


def _wgrad(a, b, tm, name, rows=None, under=None):
    t, m = a.shape
    n = b.shape[1]
    tk = min(t, 2048)
    nk = t // tk

    def body(a_ref, b_ref, *rest):
        o_ref, acc_ref = rest[-2:]
        k = pl.program_id(1)

        @pl.when(k == 0)
        def _():
            acc_ref[...] = jnp.zeros_like(acc_ref)

        acc_ref[...] += _dot_tn(a_ref[...].astype(BF16), b_ref[...].astype(BF16))

        @pl.when(k == nk - 1)
        def _():
            o_ref[...] = acc_ref[...].astype(BF16)

    in_specs = [pl.BlockSpec((tk, tm), lambda j, k: (k, j)), pl.BlockSpec((tk, n), lambda j, k: (k, 0))]
    if under is None:
        out_rows, out_spec, operands, aliases = rows or m, pl.BlockSpec((tm, n), lambda j, k: (j, 0)), (a, b), {}
    else:
        out_rows = under.shape[0]
        first = out_rows - m
        assert first % 128 == 0 and tm % 128 == 0
        out_spec = pl.BlockSpec((pl.Element(tm), pl.Element(n)),
                                lambda j, k: (pl.multiple_of(first + j * tm, 128), 0))
        in_specs, operands, aliases = in_specs + [_ANY], (a, b, under), {2: 0}
    return pl.pallas_call(
        body, name=name, grid=(m // tm, nk), in_specs=in_specs, out_specs=out_spec,
        out_shape=jax.ShapeDtypeStruct((out_rows, n), BF16), input_output_aliases=aliases,
        scratch_shapes=[pltpu.VMEM((tm, n), F32)],
        compiler_params=pltpu.CompilerParams(dimension_semantics=("arbitrary", "arbitrary"), vmem_limit_bytes=VMEM_LIMIT),
    )(*operands)


_ANY = pl.BlockSpec(memory_space=pl.ANY)


def _place():
    return lax.axis_index("x"), lax.axis_index("y"), lax.axis_index("c")


def _forward_sibling(lands, name):
    n = len(lands)

    def body(*refs):
        l_refs = refs[n:2 * n]
        send_sems, recv_sems = refs[2 * n:]
        x, y, c = _place()
        chips = [(1 - x, y), (x, 1 - y), (1 - x, 1 - y)]

        def copy(a, j, core):
            rows = l_refs[a].at[4 * chips[j][0] + 2 * chips[j][1] + core]
            return pltpu.make_async_remote_copy(
                src_ref=rows, dst_ref=rows, send_sem=send_sems.at[3 * a + j], recv_sem=recv_sems.at[3 * a + j],
                device_id=(x, y, 1 - c), device_id_type=MESH)

        for a in range(n):
            for j in range(3):
                copy(a, j, c).start()
        for a in range(n):
            for j in range(3):
                copy(a, j, 1 - c).wait_recv()
                copy(a, j, c).wait_send()

    return pl.pallas_call(
        body, name=name, in_specs=[_ANY] * n, out_specs=[_ANY] * n,
        out_shape=[jax.ShapeDtypeStruct(v.shape, v.dtype) for v in lands],
        input_output_aliases={a: a for a in range(n)},
        scratch_shapes=[pltpu.SemaphoreType.DMA((3 * n,)), pltpu.SemaphoreType.DMA((3 * n,))],
    )(*lands)


def _swap_sibling(gs, name):
    n = len(gs)
    first = [0]
    for v in gs:
        first.append(first[-1] + v.shape[0])

    def body(*refs):
        g_refs, r_refs = refs[:n], refs[n:2 * n]
        send_sems, recv_sems = refs[2 * n:]
        x, y, c = _place()
        cps = [pltpu.make_async_remote_copy(
            src_ref=g_refs[a].at[q, 1 - c], dst_ref=r_refs[a].at[q], send_sem=send_sems.at[first[a] + q],
            recv_sem=recv_sems.at[first[a] + q], device_id=(x, y, 1 - c), device_id_type=MESH)
            for a in range(n) for q in range(gs[a].shape[0])]
        for cp in cps:
            cp.start()
        for cp in cps:
            cp.wait()

    return pl.pallas_call(
        body, name=name, in_specs=[_ANY] * n, out_specs=[_ANY] * n,
        out_shape=[jax.ShapeDtypeStruct(v.shape[:1] + v.shape[2:], v.dtype) for v in gs],
        scratch_shapes=[pltpu.SemaphoreType.DMA((first[-1],)), pltpu.SemaphoreType.DMA((first[-1],))],
    )(*gs)


def _row_tile(rows, lanes, cap):
    if rows * lanes * 4 <= (1 << 20):
        return rows
    return max(d for d in range(8, cap + 1, 8) if rows % d == 0 and (d % 16 == 0 or rows % 16 != 0))


def _pair_sums(gs, rs, name):
    n = len(gs)

    def add(g, r, dtype):
        return (g.astype(F32) + r.astype(F32)).astype(dtype)

    def body(c_ref, *refs):
        g_refs, r_refs, o_refs = refs[:n], refs[n:2 * n], refs[2 * n:]
        o_refs[0][...] = add(g_refs[0][0], r_refs[0][...], o_refs[0].dtype)

        @pl.when(pl.program_id(0) == 0)
        def _():
            for a in range(1, n):
                o_refs[a][...] = add(g_refs[a][:, 0], r_refs[a][...], o_refs[a].dtype)

    def whole(shape, mine):
        if mine:
            return pl.BlockSpec(shape, lambda q, c_ref: (0, c_ref[0]) + (0,) * (len(shape) - 2))
        return pl.BlockSpec(shape, lambda q, c_ref: (0,) * len(shape))

    big = gs[0].shape
    return pl.pallas_call(
        body, name=name,
        grid_spec=pltpu.PrefetchScalarGridSpec(
            num_scalar_prefetch=1, grid=(big[0],),
            in_specs=[pl.BlockSpec((1, 1) + big[2:], lambda q, c_ref: (q, c_ref[0], 0, 0))]
            + [whole(g.shape[:1] + (1,) + g.shape[2:], True) for g in gs[1:]]
            + [pl.BlockSpec((1,) + big[2:], lambda q, c_ref: (q, 0, 0))]
            + [whole(r.shape, False) for r in rs[1:]],
            out_specs=[pl.BlockSpec((1,) + big[2:], lambda q, c_ref: (q, 0, 0))]
            + [whole(r.shape, False) for r in rs[1:]]),
        out_shape=[jax.ShapeDtypeStruct(r.shape, g.dtype) for g, r in zip(gs, rs)],
        compiler_params=_ARB,
    )(lax.axis_index("c").astype(jnp.int32).reshape(1), *gs, *rs)


def _adamw(parts, w, m, v, name, own=None):
    nl = len(parts)
    ns, rows, l = parts[0].shape
    tr = _row_tile(rows, l * ns, 304)
    nt = rows // tr
    c1 = 1.0 - ADAM_B1 ** ADAM_STEP
    c2 = 1.0 - ADAM_B2 ** ADAM_STEP

    def body(q_ref, *refs):
        own_refs = refs[:nl] if own is not None else None
        p_refs = refs[-7 - nl:-7]
        w_ref, m_ref, v_ref, g_ref, d_ref, nm_ref, nv_ref = refs[-7:]
        layer = pl.program_id(0)
        g = None
        for j in range(nl):
            gj = None
            for k in range(ns):
                term = p_refs[j][k].astype(F32)
                if own_refs is not None:
                    term = jnp.where(q_ref[0] == k, own_refs[j][0].astype(F32), term)
                gj = term if gj is None else gj + term
            g = gj if g is None else jnp.where(layer == j, gj, g)
        g_ref[...] = g
        nm = ADAM_B1 * m_ref[...] + (1.0 - ADAM_B1) * g
        nv = ADAM_B2 * v_ref[...] + (1.0 - ADAM_B2) * (g * g)
        nm_ref[...] = nm
        nv_ref[...] = nv
        d_ref[...] = -ADAM_LR * ((nm / c1) / (jnp.sqrt(nv / c2) + ADAM_EPS) + ADAM_WD * w_ref[...])

    def tile_of(j):
        return lambda la, i, q: jnp.where(la == j, i, jnp.where(la < j, 0, nt - 1))

    row = pl.BlockSpec((tr, l), lambda la, i, q: (la * nt + i, 0))
    own_specs = [] if own is None else [
        pl.BlockSpec((1, tr, l), lambda la, i, q, j=j: (q[0], tile_of(j)(la, i, q), 0)) for j in range(nl)]
    part_specs = [pl.BlockSpec((ns, tr, l), lambda la, i, q, j=j: (0, tile_of(j)(la, i, q), 0)) for j in range(nl)]
    chip = (2 * lax.axis_index("x") + lax.axis_index("y")).astype(jnp.int32).reshape(1)
    return pl.pallas_call(
        body, name=name,
        grid_spec=pltpu.PrefetchScalarGridSpec(
            num_scalar_prefetch=1, grid=(nl, nt),
            in_specs=own_specs + part_specs + [row, row, row], out_specs=[row] * 4),
        out_shape=[jax.ShapeDtypeStruct((nl * rows, l), F32)] * 4,
        compiler_params=pltpu.CompilerParams(dimension_semantics=("arbitrary", "arbitrary"), vmem_limit_bytes=VMEM_LIMIT),
    )(chip, *([] if own is None else own), *parts, w, m, v)


_HBM = pl.BlockSpec(memory_space=pltpu.HBM)
_SEM = pl.BlockSpec(memory_space=pltpu.SEMAPHORE)
_EFFECT = pltpu.SideEffectType.DATAFLOW_SIDE_EFFECTING


def _plan_all(x, y, c):
    me = 4 * x + 2 * y + c
    peers = [(x, y, 1 - c), (1 - x, y, c), (x, 1 - y, c), (1 - x, 1 - y, c),
             (1 - x, y, 1 - c), (x, 1 - y, 1 - c), (1 - x, 1 - y, 1 - c)]
    return [(None, me, p, 4 * p[0] + 2 * p[1] + p[2]) for p in peers]


def _plan_near(x, y, c):
    me = 4 * x + 2 * y + c
    peers = [(x, y, 1 - c), (1 - x, y, c), (x, 1 - y, c), (1 - x, 1 - y, c)]
    return [(None, me, p, 4 * p[0] + 2 * p[1] + p[2]) for p in peers]


def _plan_sibling(x, y, c):
    return [(2 * q + 1 - c, q, (x, y, 1 - c), q) for q in range(4)]


def _plan_chips(x, y, c):
    me = 2 * x + y
    return [(2 * qx + qy, me, (qx, qy, c), 2 * qx + qy) for qx, qy in ((1 - x, y), (x, 1 - y), (1 - x, 1 - y))]


def _split_copies(plan, src_refs, land_refs, send_sems, recv_sems, arrival):
    n = len(src_refs)
    entries = plan(*_place())
    per = len(entries)
    cps = []
    for a in range(n):
        for k, (src_slot, dst_slot, peer, back_slot) in enumerate(entries):
            src = src_refs[a] if src_slot is None else src_refs[a].at[src_slot]
            cps.append(pltpu.make_async_remote_copy(
                src_ref=src, dst_ref=land_refs[a].at[back_slot if arrival else dst_slot],
                send_sem=send_sems.at[per * a + k], recv_sem=recv_sems.at[per * a + k],
                device_id=peer, device_id_type=MESH))
    return cps


def _split_start(srcs, lands, plan, per, name):
    n = len(srcs)

    def body(*refs):
        for cp in _split_copies(plan, refs[:n], refs[n:2 * n], refs[2 * n], refs[2 * n + 1], False):
            cp.start()
        refs[-1][...] = jnp.zeros_like(refs[-1])

    both = list(srcs) + list(lands)
    outs = pl.pallas_call(
        body, name=name,
        out_shape=(pltpu.SemaphoreType.DMA((per * n,)), pltpu.SemaphoreType.DMA((per * n,)),
                   *[pltpu.HBM(v.shape, v.dtype) for v in both], jax.ShapeDtypeStruct((8, 128), F32)),
        in_specs=[_HBM] * (2 * n),
        out_specs=(_SEM, _SEM, *[_HBM] * (2 * n), pl.BlockSpec(memory_space=pltpu.VMEM)),
        input_output_aliases={i: 2 + i for i in range(2 * n)},
        compiler_params=pltpu.CompilerParams(has_side_effects=_EFFECT),
    )(*[pltpu.with_memory_space_constraint(v, pltpu.HBM) for v in both])
    return outs[0], outs[1], list(outs[2:2 + 2 * n]), outs[-1]


def _split_wait(send_sems, recv_sems, thru, plan, after, name):
    n = len(thru) // 2

    def body(*refs):
        for cp in _split_copies(plan, refs[:n], refs[n:2 * n], refs[2 * n], refs[2 * n + 1], True):
            cp.wait_send()
            cp.wait_recv()

    outs = pl.pallas_call(
        body, name=name, out_shape=tuple(pltpu.HBM(v.shape, v.dtype) for v in thru),
        in_specs=[_HBM] * (2 * n) + [_SEM, _SEM, pl.BlockSpec(memory_space=pl.ANY)],
        out_specs=[_HBM] * (2 * n), input_output_aliases={i: i for i in range(2 * n)},
        compiler_params=pltpu.CompilerParams(has_side_effects=_EFFECT),
    )(*thru, send_sems, recv_sems, after)
    return list(outs[:n]), list(outs[n:])


_SMALL_IN = ("ln_in_g", "ln_in_b")
_SMALL_ROWS = ("w_s", "b_s", "sinks")
_SMALL_LANES = ("b_in", "vn_g", "vn_b", "b_out", "ln_g", "ln_b")


def _tile_rows(a):
    return -(-a.size // 1024) * 8


def _pack_small(d, names):
    return jnp.concatenate([jnp.pad(d[n].reshape(-1), (0, (-d[n].size) % 1024)).reshape(-1, 128) for n in names])


def _adamw_small(parts, own, w, m, v, pieces, axis, name):
    c1 = 1.0 - ADAM_B1 ** ADAM_STEP
    c2 = 1.0 - ADAM_B2 ** ADAM_STEP
    shapes = [tuple(p if d == axis else s for d, s in enumerate(w.shape)) for p in pieces]

    def body(q_ref, own_ref, p_ref, w_ref, m_ref, v_ref, *o_refs):
        g = None
        for k in range(4):
            term = jnp.where(q_ref[0] == k, own_ref[0], p_ref[k])
            g = term if g is None else g + term
        nm = ADAM_B1 * m_ref[...] + (1.0 - ADAM_B1) * g
        nv = ADAM_B2 * v_ref[...] + (1.0 - ADAM_B2) * (g * g)
        delta = -ADAM_LR * ((nm / c1) / (jnp.sqrt(nv / c2) + ADAM_EPS) + ADAM_WD * w_ref[...])
        for k, val in enumerate((g, delta, nm, nv)):
            off = 0
            for j, p in enumerate(pieces):
                o_refs[k * len(pieces) + j][...] = val[off:off + p] if axis == 0 else val[:, off:off + p]
                off += p

    whole = pl.BlockSpec(w.shape, lambda i, q: (0, 0))
    chip = (2 * lax.axis_index("x") + lax.axis_index("y")).astype(jnp.int32).reshape(1)
    outs = pl.pallas_call(
        body, name=name,
        grid_spec=pltpu.PrefetchScalarGridSpec(
            num_scalar_prefetch=1, grid=(1,),
            in_specs=[pl.BlockSpec((1,) + w.shape, lambda i, q: (q[0], 0, 0)),
                      pl.BlockSpec((4,) + w.shape, lambda i, q: (0, 0, 0)), whole, whole, whole],
            out_specs=[pl.BlockSpec(s, lambda i, q: (0, 0)) for s in shapes] * 4),
        out_shape=[jax.ShapeDtypeStruct(s, F32) for s in shapes] * 4, compiler_params=_ARB,
    )(chip, own, parts, w, m, v)
    return [outs[k * len(pieces):(k + 1) * len(pieces)] for k in range(4)]


def _owner_blocks(g, axis):
    sh = g.shape
    g = g.reshape(sh[:axis] + (4, 2, sh[axis] // N_DEV) + sh[axis + 1:])
    return jnp.moveaxis(g, (axis, axis + 1), (0, 1))


def kernel(x, ln_in_g, ln_in_b, w_in, b_in, sinks, vn_g, vn_b, w_s, b_s, p_a, p_b, w_out, b_out, ln_g, ln_b, loss_target, m_ln_in_g, m_ln_in_b, m_w_in, m_b_in, m_sinks, m_vn_g, m_vn_b, m_w_s, m_b_s, m_p_a, m_p_b, m_w_out, m_b_out, m_ln_g, m_ln_b, v_ln_in_g, v_ln_in_b, v_w_in, v_b_in, v_sinks, v_vn_g, v_vn_b, v_w_s, v_b_s, v_p_a, v_p_b, v_w_out, v_b_out, v_ln_g, v_ln_b):
    nseq, seq, _ = x.shape
    t = nseq * seq
    nblk_seq = seq // BLK
    x2 = x.reshape(t, D)
    tgt = loss_target.reshape(t, D)

    def turned(a):
        return jnp.swapaxes(a, 1, 2)

    w_in_t = turned(w_in)

    def blocks(l):
        return [w_in_t[l].astype(BF16), p_a[l].astype(BF16), p_b[l].astype(BF16), w_out[l].astype(BF16)]

    def full_weights(g):
        w_t_full = g[0].reshape(IN_COLS, D)
        wo_full = g[3].reshape(D, D)
        return dict(w_t=w_t_full, pa=g[1], pb=g[2], wo=wo_full)

    def landing(bs):
        return [lax.empty((N_DEV,) + v.shape, v.dtype) for v in bs]

    def with_own(landed, sent):
        return [lax.dynamic_update_index_in_dim(g, b, me, 0) for g, b in zip(landed, sent)]

    me = 4 * lax.axis_index("x") + 2 * lax.axis_index("y") + lax.axis_index("c")
    blocks0 = blocks(0)
    a_send, a_recv, a_thru, a_token = _split_start(blocks0[:1], landing(blocks0[:1]), _plan_near, 4,
                                                   "allgather_w_in0_start")
    rest0 = [b + a_token[0, 0].astype(BF16) for b in blocks0[1:]]
    b_send, b_recv, b_thru, b_token = _split_start(rest0, landing(rest0), _plan_all, 7, "allgather_rest0_start")
    xs = [_ln_fwd(x2, ln_in_g + b_token[0, 0], ln_in_b, "ln_in_fwd")]
    sent, landed = _split_wait(a_send, a_recv, a_thru, _plan_near, xs[0], "allgather_w_in0_wait")
    gathered0 = with_own(_forward_sibling(landed, "allgather_w_in0_forward"), sent)
    blocks1, gathered0 = lax.optimization_barrier((blocks(1), gathered0))
    ag_send, ag_recv, ag_thru, ag_token = _split_start(blocks1, landing(blocks1), _plan_all, 7,
                                                       "allgather_weights1_start")
    weights = [None, None]
    bsb = jnp.broadcast_to(b_s[:, :, :, None], (DEPTH, 4, BLK, BLK))
    bias = _band_bias()

    saved = []
    for l in range(DEPTH):
        if l == 1:
            sent, landed = _split_wait(ag_send, ag_recv, ag_thru, _plan_all, xs[1], "allgather_weights1_wait")
            weights[1] = full_weights(with_own(landed, sent))
        w_t = weights[l]["w_t"] if l else gathered0[0].reshape(IN_COLS, D)
        last = l == DEPTH - 1
        b_l = b_in[l].reshape(1, -1) + (ag_token[0, 0] if l == 0 else 0.0)
        hm, hr = _inproj(xs[l], w_t, b_l, f"inproj{l}")
        ya, yb, prob, psink = _mixer_fwd(hm, sinks[l], bias, vn_g[l].reshape(1, -1), vn_b[l].reshape(1, -1),
                                         w_s[l], bsb[l], nblk_seq, f"mixer_fwd{l}")
        if l == 0:
            sent, landed = _split_wait(b_send, b_recv, b_thru, _plan_all, ya, "allgather_rest0_wait")
            weights[0] = full_weights(gathered0 + with_own(landed, sent))
        wl = weights[l]
        outs = _tail_fwd(xs[l], ya, yb, hr, wl["pa"], wl["pb"], wl["wo"], b_out[l].reshape(1, D),
                         ln_g[l].reshape(1, D), ln_b[l].reshape(1, D), f"tail_fwd{l}", last)
        saved.append((hm, hr, ya, yb, prob, psink) + tuple(outs[:4]))
        if not last:
            xs.append(outs[4])

    small = {n: [None] * DEPTH for n in _SMALL_ROWS + _SMALL_LANES}

    def pack_rows(d):
        return _pack_small(d, _SMALL_ROWS)

    def pack_lanes(d):
        return jnp.concatenate([d[n] for n in _SMALL_LANES], axis=1)
    names = ("w_in", "p_a", "p_b", "w_out")
    owner_axis = {"w_in": 0, "p_a": 1, "p_b": 1, "w_out": 0}
    token = jnp.zeros((8, 128), F32)
    dx = tgt
    split = [None] * DEPTH
    for l in reversed(range(DEPTH)):
        hm, hr, ya, yb, prob, psink, pa, pb, merged, z = saved[l]
        wl = weights[l]
        dz, dpa, dpb, dhr, dya, dyb, acc, gbr = _tail_bwd(
            dx, z, pa, pb, hr, wl["wo"], wl["pa"], wl["pb"], ln_g[l].reshape(1, D) + token[0, 0],
            ln_b[l].reshape(1, D), f"tail_bwd{l}", l == DEPTH - 1)
        if l == DEPTH - 1:
            sq_err = acc[3:4, 0:128]
        dhm, gbm, gsk, gvn, gws, gbs = _mixer_bwd(
            hm, dya, dyb, prob, psink, vn_g[l].reshape(1, -1), vn_b[l].reshape(1, -1), w_s[l], bsb[l],
            f"mixer_bwd{l}")
        grads = {"w_in": _wgrad(dhr, xs[l], R_W // 2, f"wgrad_in_route{l}",
                                under=_wgrad(dhm, xs[l], MAIN_W // 2, f"wgrad_in_main{l}", rows=IN_COLS)),
                 "p_a": _wgrad(ya, dpa, Q_W, f"wgrad_pa{l}"), "p_b": _wgrad(yb, dpb, SGU_W, f"wgrad_pb{l}"),
                 "w_out": _wgrad(merged, dz, D, f"wgrad_out{l}")}
        small["b_in"][l] = jnp.concatenate([gbm[0], gbr[0]])
        small["sinks"][l] = gsk[:, 0]
        small["vn_g"][l], small["vn_b"][l] = gvn[0], gvn[1]
        small["w_s"][l], small["b_s"][l] = gws, gbs[:, :, 0]
        small["ln_g"][l], small["ln_b"][l], small["b_out"][l] = acc[0], acc[1], acc[2]
        parts = [_owner_blocks(grads[n], owner_axis[n]) for n in names]
        if l == 0:
            stacked = {n: jnp.stack(v) for n, v in small.items()}
            for packed in (pack_rows(stacked), pack_lanes(stacked)):
                parts.append(jnp.broadcast_to(packed[None, None], (1, 2) + packed.shape))
        if l == 0:
            from_sib = _swap_sibling(parts, f"rs_sibling{l}")
        else:
            halves = [p.reshape((N_DEV,) + p.shape[2:]) for p in parts]
            sib = _split_start(halves, [lax.empty((4,) + p.shape[2:], p.dtype) for p in parts], _plan_sibling, 4,
                               f"rs_sibling{l}_start")
            dx = _dx_inproj(dz, dhm, dhr, wl["w_t"], sib[3], f"dx_inproj{l}")
            halves, from_sib = _split_wait(sib[0], sib[1], sib[2], _plan_sibling, dx, f"rs_sibling{l}_wait")
            parts = [h.reshape(p.shape) for h, p in zip(halves, parts)]
        pair = list(_pair_sums(parts, from_sib, f"pair_sums{l}"))
        if l == 0:
            pair[4:] = [jnp.broadcast_to(p, (4,) + p.shape[1:]) for p in pair[4:]]
        lands = [jnp.zeros(p.shape, p.dtype) for p in pair]
        split[l] = _split_start(pair, lands, _plan_chips, 3, f"rs_chips{l}_start")
        token = split[l][3]
        if l == 0:
            grad_x, acc_in = _dx_inproj(dz, dhm, dhr, wl["w_t"], token, f"dx_inproj{l}",
                                        ln_in=(x2, ln_in_g.reshape(1, D)))
    last = [acc_in, jnp.broadcast_to(sq_err, (8, 128))]
    ln_send, ln_recv, ln_thru, ln_token = _split_start(last, landing(last), _plan_all, 7, "allgather_ln_in_start")

    given = {"w_in": (w_in_t, turned(m_w_in), turned(v_w_in)), "p_a": (p_a, m_p_a, v_p_a),
             "p_b": (p_b, m_p_b, v_p_b), "w_out": (w_out, m_w_out, v_w_out)}
    waited = [_split_wait(split[l][0], split[l][1], split[l][2], _plan_chips, ln_token, f"rs_chips{l}_wait")
              for l in range(DEPTH)]
    res = {}
    for a, n in enumerate(names):
        rows, lanes = waited[0][1][a].shape[1:]
        outs = _adamw([waited[l][1][a] for l in range(DEPTH)], *[v.reshape(DEPTH * rows, lanes) for v in given[n]],
                      f"adamw_{n}", own=[waited[l][0][a] for l in range(DEPTH)])
        res[n] = [o.reshape(given[n][0].shape) for o in outs]
    res["w_in"] = [turned(o) for o in res["w_in"]]

    w_small = dict(ln_in_g=ln_in_g, ln_in_b=ln_in_b, b_in=b_in, sinks=sinks, vn_g=vn_g, vn_b=vn_b, w_s=w_s, b_s=b_s,
                   b_out=b_out, ln_g=ln_g, ln_b=ln_b)
    m_small = dict(ln_in_g=m_ln_in_g, ln_in_b=m_ln_in_b, b_in=m_b_in, sinks=m_sinks, vn_g=m_vn_g, vn_b=m_vn_b,
                   w_s=m_w_s, b_s=m_b_s, b_out=m_b_out, ln_g=m_ln_g, ln_b=m_ln_b)
    v_small = dict(ln_in_g=v_ln_in_g, ln_in_b=v_ln_in_b, b_in=v_b_in, sinks=v_sinks, vn_g=v_vn_g, vn_b=v_vn_b,
                   w_s=v_w_s, b_s=v_b_s, b_out=v_b_out, ln_g=v_ln_g, ln_b=v_ln_b)
    by_rows = _adamw_small(waited[0][1][4], waited[0][0][4], *[pack_rows(d) for d in (w_small, m_small, v_small)],
                           [_tile_rows(w_small[n]) for n in _SMALL_ROWS], 0, "adamw_small_rows")
    by_lanes = _adamw_small(waited[0][1][5], waited[0][0][5], *[pack_lanes(d) for d in (w_small, m_small, v_small)],
                            [w_small[n].shape[1] for n in _SMALL_LANES], 1, "adamw_small_lanes")
    sent, landed = _split_wait(ln_send, ln_recv, ln_thru, _plan_all, by_lanes[0][0], "allgather_ln_in_wait")
    all_in, all_sq = with_own(landed, sent)
    loss = jnp.sum(all_sq[:, 0, 0]) * (0.5 / D)
    outs_in = _adamw([all_in], *[jnp.pad(jnp.stack([d[n] for n in _SMALL_IN]), ((0, 6), (0, 0)))
                                 for d in (w_small, m_small, v_small)], "adamw_ln_in")
    for k in range(4):
        u = {n: o.reshape(-1)[:w_small[n].size].reshape(w_small[n].shape) for n, o in zip(_SMALL_ROWS, by_rows[k])}
        u.update(zip(_SMALL_LANES, by_lanes[k]))
        u.update({n: outs_in[k][r] for r, n in enumerate(_SMALL_IN)})
        for n in u:
            res.setdefault(n, [None] * 4)[k] = u[n]

    order = ("ln_in_g", "ln_in_b", "w_in", "b_in", "sinks", "vn_g", "vn_b", "w_s", "b_s", "p_a", "p_b", "w_out",
             "b_out", "ln_g", "ln_b")
    return (loss, grad_x.reshape(x.shape), *[res[n][0] for n in order], *[res[n][1] for n in order],
            *[res[n][2] for n in order], *[res[n][3] for n in order])
```

```python
import jax
import jax.numpy as jnp
from jax import lax
from jax.experimental import pallas as pl
from jax.experimental.pallas import tpu as pltpu

F32 = jnp.float32
BF16 = jnp.bfloat16

D = 1024
BLK = 128
N_KV = 2
Q_W, KV_W, SGU_W = 512, 128, 512
C_Q, C_K, C_V, C_GA, C_UB, C_VB, C_GB = 0, 512, 640, 768, 1280, 1792, 2304
MAIN_W = 2816
R_W = 2048
IN_COLS = MAIN_W + R_W
N_DEV = 8

DEPTH = 2
ALPHA = (2.0 * DEPTH) ** 0.25
LN_EPS = 1e-5
ATTN_SCALE = 0.125
NEG = float(jnp.finfo(jnp.float32).min)

ADAM_LR, ADAM_B1, ADAM_B2, ADAM_EPS, ADAM_WD, ADAM_STEP = 0.001, 0.9, 0.999, 1e-08, 0.01, 10

TM = 512
TM_EW = 1024
TM_MM = 512
TM_T = 512
NB = TM // BLK
MESH = pl.DeviceIdType.MESH
VMEM_LIMIT = 56 * 1024 * 1024

_ARB = pltpu.CompilerParams(dimension_semantics=("arbitrary",), vmem_limit_bytes=VMEM_LIMIT)


def _sigmoid(x):
    return 1.0 / (1.0 + jnp.exp(-x))


_GELU_C = 0.7978845608028654
_GELU_A = 0.044715


def _gelu_parts(x):
    x2 = x * x
    t = jnp.tanh(x * (_GELU_C + (_GELU_C * _GELU_A) * x2))
    hx = 0.5 * x
    return hx, t, x2


def _gelu(x):
    hx, t, _ = _gelu_parts(x)
    return hx + hx * t


def _gelu_and_grad(x):
    hx, t, x2 = _gelu_parts(x)
    grad = 0.5 + 0.5 * t + (hx - hx * (t * t)) * (_GELU_C + (3.0 * _GELU_C * _GELU_A) * x2)
    return hx + hx * t, grad


def _ln_stats(x):
    mu = jnp.mean(x, axis=-1, keepdims=True)
    xc = x - mu
    var = jnp.mean(xc * xc, axis=-1, keepdims=True)
    rstd = lax.rsqrt(var + LN_EPS)
    return xc * rstd, rstd


def _ln_bwd(dy_g, xhat, rstd):
    m1 = jnp.mean(dy_g, axis=-1, keepdims=True)
    m2 = jnp.mean(dy_g * xhat, axis=-1, keepdims=True)
    return rstd * (dy_g - m1 - xhat * m2)


def _colsum(x):
    return jnp.sum(x, axis=0, keepdims=True)


def _dot(a, b):
    return jnp.dot(a, b, preferred_element_type=F32)


def _dot_nt(a, b):
    return lax.dot_general(a, b, (((1,), (1,)), ((), ())), preferred_element_type=F32)


def _side_by_side(gathered_ref):
    return jnp.concatenate([gathered_ref[j] for j in range(N_DEV)], axis=1)


def _dot_tn(a, b):
    return lax.dot_general(a, b, (((0,), (0,)), ((), ())), preferred_element_type=F32)


def _head_place(hk, g):
    j = 4 * hk + g
    return j, j // 2, j % 2


def _head_rows(x, hk):
    d = lax.broadcasted_iota(jnp.int32, x.shape, 0)
    return jnp.where((d >= 64 * hk) & (d < 64 * hk + 64), x, 0.0).astype(BF16)


def _head_lanes(x, hk):
    d = lax.broadcasted_iota(jnp.int32, x.shape, 1)
    return jnp.where((d >= 64 * hk) & (d < 64 * hk + 64), x, 0.0)


def _band_bias():
    kpos = lax.broadcasted_iota(jnp.int32, (2 * BLK, 4 * BLK), 0)
    row = lax.broadcasted_iota(jnp.int32, (2 * BLK, 4 * BLK), 1) & (BLK - 1)
    band = (kpos > row) & (kpos <= row + BLK)
    return jnp.stack([jnp.where(band, 0.0, NEG), jnp.where(band & (kpos >= BLK), 0.0, NEG)]).astype(F32)


def _stack_q(q, hk):
    parts = []
    for g in range(4):
        _, p, pos = _head_place(hk, g)
        qp = q[:, BLK * p:BLK * (p + 1)] * ATTN_SCALE
        if pos != hk:
            qp = pltpu.roll(qp, 64, 1)
        parts.append(qp.astype(BF16))
    return jnp.concatenate(parts, axis=0)


def _attn_probs(q4, kh, hk, sinks_ref, bias):
    s_t = _dot_nt(kh, q4) + bias
    sink_row = jnp.concatenate(
        [jnp.full((1, BLK), sinks_ref[4 * hk + g], F32) for g in range(4)], axis=1)
    m = jnp.maximum(jnp.max(s_t, axis=0, keepdims=True), sink_row)
    p_un = jnp.exp(s_t - m)
    e_sink = jnp.exp(sink_row - m)
    inv = 1.0 / (jnp.sum(p_un, axis=0, keepdims=True) + e_sink)
    return (p_un * inv).astype(BF16), e_sink * inv


def _unstack_heads(x4, hk, pairs):
    for g in range(4):
        _, p, pos = _head_place(hk, g)
        xg = x4[BLK * g:BLK * (g + 1)]
        if pos != hk:
            xg = pltpu.roll(xg, 64, 1)
        pairs[p] = xg if pairs[p] is None else pairs[p] + xg
    return pairs


def _attn_fwd(q, kband, vband, sinks_ref, bias, save):
    pairs = [None] * 4
    vband_t = vband.T
    for hk in range(N_KV):
        prob_t, p_sink = _attn_probs(_stack_q(q, hk), _head_lanes(kband, hk).astype(BF16), hk, sinks_ref, bias)
        save(hk, prob_t, p_sink)
        o_t = _dot(_head_rows(vband_t, hk), prob_t)
        pairs = _unstack_heads(o_t.T, hk, pairs)
    return jnp.concatenate(pairs, axis=1)


def _tril_mask():
    r = lax.broadcasted_iota(jnp.int32, (BLK, BLK), 0)
    c = lax.broadcasted_iota(jnp.int32, (BLK, BLK), 1)
    return c <= r


def _sgu_fwd(u, v, vn_g, vn_b, wt, bsb_ref):
    vhat, rstd = _ln_stats(v)
    vn = vhat * vn_g + vn_b
    mixed = jnp.concatenate(
        [_dot(wt[g], vn[:, BLK * g:BLK * (g + 1)].astype(BF16)) + bsb_ref[g] for g in range(4)], axis=1)
    return vhat, rstd, vn, mixed


def _cols(ref, rows, col, width):
    return ref[rows, col:col + width].astype(F32)


def _band(hm_ref, hprev_ref, s, col):
    r0 = s * BLK
    cur = hm_ref[r0:r0 + BLK, col:col + KV_W]
    if s == 0:
        off = 0 if col == C_K else KV_W
        prev = hprev_ref[:, off:off + KV_W]
    else:
        prev = hm_ref[r0 - BLK:r0, col:col + KV_W]
    return jnp.concatenate([prev, cur], axis=0).astype(F32)


def _h_main_specs(nt, rev):
    def tile(g):
        return nt - 1 - g if rev else g

    return [pl.BlockSpec((TM, MAIN_W), lambda g: (tile(g), 0)),
            pl.BlockSpec((BLK, 2 * KV_W), lambda g: (jnp.maximum(tile(g) * NB - 1, 0), 2))]


_CONST2 = lambda g: (0, 0)
_CONST3 = lambda g: (0, 0, 0)


def _ln_fwd(x, g, b, name):
    t = x.shape[0]

    def body(x_ref, g_ref, b_ref, o_ref):
        xhat, _ = _ln_stats(x_ref[...])
        o_ref[...] = xhat * g_ref[...] + b_ref[...]

    return pl.pallas_call(
        body, name=name, grid=(t // TM_EW,),
        in_specs=[pl.BlockSpec((TM_EW, D), lambda i: (i, 0)), pl.BlockSpec((1, D), _CONST2),
                  pl.BlockSpec((1, D), _CONST2)],
        out_specs=pl.BlockSpec((TM_EW, D), lambda i: (i, 0)),
        out_shape=jax.ShapeDtypeStruct((t, D), F32), compiler_params=_ARB,
    )(x, g.reshape(1, D), b.reshape(1, D))


def _inproj(x, w_t, b, name):
    t = x.shape[0]

    def body(x_ref, wt_ref, b_ref, hm_ref, hr_ref):
        xb = x_ref[...].astype(BF16)
        hm_ref[...] = (_dot_nt(xb, wt_ref[0:MAIN_W, :]) + b_ref[:, 0:MAIN_W]).astype(BF16)
        hr_ref[...] = (_dot_nt(xb, wt_ref[MAIN_W:IN_COLS, :]) + b_ref[:, MAIN_W:IN_COLS]).astype(BF16)

    return pl.pallas_call(
        body, name=name, grid=(t // TM_MM,),
        in_specs=[pl.BlockSpec((TM_MM, D), lambda i: (i, 0)),
                  pl.BlockSpec((IN_COLS, D), _CONST2), pl.BlockSpec((1, IN_COLS), _CONST2)],
        out_specs=[pl.BlockSpec((TM_MM, MAIN_W), lambda i: (i, 0)), pl.BlockSpec((TM_MM, R_W), lambda i: (i, 0))],
        out_shape=[jax.ShapeDtypeStruct((t, MAIN_W), BF16), jax.ShapeDtypeStruct((t, R_W), BF16)],
        compiler_params=_ARB,
    )(x, w_t, b)


def _mixer_fwd(hm, sinks, bias, vn_g, vn_b, w_s, bsb, nblk_seq, name):
    t = hm.shape[0]
    nt = t // TM

    def body(sinks_ref, hm_ref, hprev_ref, bias_ref, vng_ref, vnb_ref, ws_ref, bsb_ref,
             ya_ref, yb_ref, prob_ref, psink_ref):
        i = pl.program_id(0)
        tril = _tril_mask()
        wt = [jnp.where(tril, ws_ref[g], 0.0).astype(BF16) for g in range(4)]
        for s in range(NB):
            r0 = s * BLK
            rows = slice(r0, r0 + BLK)
            bias = bias_ref[jnp.where((i * NB + s) % nblk_seq == 0, 1, 0)]

            def save(hk, prob_t, p_sink, s=s):
                prob_ref[N_KV * s + hk] = prob_t
                psink_ref[N_KV * s + hk] = jnp.broadcast_to(p_sink, (8, 4 * BLK))

            attn = _attn_fwd(_cols(hm_ref, rows, C_Q, Q_W), _band(hm_ref, hprev_ref, s, C_K),
                             _band(hm_ref, hprev_ref, s, C_V), sinks_ref, bias, save)
            g_a = _cols(hm_ref, rows, C_GA, Q_W)
            ya_ref[rows, :] = (attn * (g_a * _sigmoid(g_a))).astype(BF16)
            u = _gelu(_cols(hm_ref, rows, C_UB, SGU_W))
            mixed = _sgu_fwd(u, _gelu(_cols(hm_ref, rows, C_VB, SGU_W)), vng_ref[...], vnb_ref[...], wt, bsb_ref)[-1]
            g_b = _cols(hm_ref, rows, C_GB, SGU_W)
            yb_ref[rows, :] = (u * mixed * (g_b * _sigmoid(g_b))).astype(BF16)

    ngrp = N_KV * NB
    return pl.pallas_call(
        body, name=name, grid=(nt,),
        in_specs=[pl.BlockSpec(memory_space=pltpu.SMEM)] + _h_main_specs(nt, False) + [
            pl.BlockSpec((2, 2 * BLK, 4 * BLK), _CONST3),
            pl.BlockSpec((1, SGU_W), _CONST2), pl.BlockSpec((1, SGU_W), _CONST2),
            pl.BlockSpec((4, BLK, BLK), _CONST3), pl.BlockSpec((4, BLK, BLK), _CONST3)],
        out_specs=[pl.BlockSpec((TM, Q_W), lambda i: (i, 0)), pl.BlockSpec((TM, SGU_W), lambda i: (i, 0)),
                   pl.BlockSpec((ngrp, 2 * BLK, 4 * BLK), lambda i: (i, 0, 0)),
                   pl.BlockSpec((ngrp, 8, 4 * BLK), lambda i: (i, 0, 0))],
        out_shape=[jax.ShapeDtypeStruct((t, Q_W), BF16), jax.ShapeDtypeStruct((t, SGU_W), BF16),
                   jax.ShapeDtypeStruct((nt * ngrp, 2 * BLK, 4 * BLK), BF16),
                   jax.ShapeDtypeStruct((nt * ngrp, 8, 4 * BLK), F32)],
        compiler_params=_ARB,
    )(sinks, hm, hm, bias, vn_g, vn_b, w_s, bsb)


def _tail_fwd(x, ya, yb, hr, pa_w, pb_w, wo, b_out, ln_g, ln_b, name, last):
    t = x.shape[0]

    def body(x_ref, ya_ref, yb_ref, hr_ref, paw_ref, pbw_ref, wo_ref, bo_ref, g_ref, b_ref,
             pa_ref, pb_ref, mg_ref, z_ref, *xn_ref):
        pa = _dot(ya_ref[...], _side_by_side(paw_ref))
        pb = _dot(yb_ref[...], _side_by_side(pbw_ref))
        pa_ref[...] = pa.astype(BF16)
        pb_ref[...] = pb.astype(BF16)
        everything = slice(None)
        merged = _sigmoid(_cols(hr_ref, everything, 0, D)) * pa + _sigmoid(_cols(hr_ref, everything, D, D)) * pb
        mb = merged.astype(BF16)
        mg_ref[...] = mb
        z = ALPHA * x_ref[...] + (_dot(mb, wo_ref[...]) + bo_ref[...])
        z_ref[...] = z
        if not last:
            zhat, _ = _ln_stats(z)
            xn_ref[0][...] = zhat * g_ref[...] + b_ref[...]

    row = lambda w: pl.BlockSpec((TM_T, w), lambda i: (i, 0))
    vec = pl.BlockSpec((1, D), _CONST2)
    n_f32 = 1 if last else 2
    return pl.pallas_call(
        body, name=name, grid=(t // TM_T,),
        in_specs=[row(D), row(Q_W), row(SGU_W), row(R_W),
                  pl.BlockSpec((N_DEV, Q_W, 128), _CONST3), pl.BlockSpec((N_DEV, SGU_W, 128), _CONST3),
                  pl.BlockSpec((D, D), _CONST2), vec, vec, vec],
        out_specs=[row(D)] * (3 + n_f32),
        out_shape=[jax.ShapeDtypeStruct((t, D), BF16)] * 3 + [jax.ShapeDtypeStruct((t, D), F32)] * n_f32,
        compiler_params=_ARB,
    )(x, ya, yb, hr, pa_w, pb_w, wo, b_out, ln_g, ln_b)


def _tail_bwd(dxn, z, pa, pb, hr, wo, pa_w, pb_w, ln_g, ln_b, name, from_loss):
    t = dxn.shape[0]

    def body(dxn_ref, z_ref, pa_ref, pb_ref, hr_ref, wo_ref, paw_ref, pbw_ref, g_ref, b_ref,
             dz_ref, dpa_ref, dpb_ref, dhr_ref, dya_ref, dyb_ref, acc_ref, gbr_ref):
        @pl.when(pl.program_id(0) == 0)
        def _():
            acc_ref[...] = jnp.zeros_like(acc_ref)
            gbr_ref[...] = jnp.zeros_like(gbr_ref)

        zhat, rstd = _ln_stats(z_ref[...])
        if from_loss:
            err = zhat * g_ref[...] + b_ref[...] - dxn_ref[...]
            dxn_v = err * (1.0 / D)
            sq = jnp.sum(jnp.sum(err * err, axis=1, keepdims=True), axis=0, keepdims=True)
            acc_ref[3:4, :] += jnp.broadcast_to(sq, (1, D))
        else:
            dxn_v = dxn_ref[...]
        dz = _ln_bwd(dxn_v * g_ref[...], zhat, rstd)
        dz_ref[...] = dz
        acc_ref[0:1, :] += _colsum(dxn_v * zhat)
        acc_ref[1:2, :] += _colsum(dxn_v)
        acc_ref[2:3, :] += _colsum(dz)
        dmerged = _dot_nt(dz.astype(BF16), wo_ref[...])
        everything = slice(None)
        sa = _sigmoid(_cols(hr_ref, everything, 0, D))
        sb = _sigmoid(_cols(hr_ref, everything, D, D))
        dpa = (dmerged * sa).astype(BF16)
        dpb = (dmerged * sb).astype(BF16)
        dpa_ref[...] = dpa
        dpb_ref[...] = dpb
        dra = dmerged * pa_ref[...].astype(F32) * (sa * (1.0 - sa))
        drb = dmerged * pb_ref[...].astype(F32) * (sb * (1.0 - sb))
        dhr_ref[:, 0:D] = dra.astype(BF16)
        dhr_ref[:, D:2 * D] = drb.astype(BF16)
        gbr_ref[0:1, 0:D] += _colsum(dra)
        gbr_ref[0:1, D:2 * D] += _colsum(drb)
        dya_ref[...] = _dot_nt(dpa, _side_by_side(paw_ref)).astype(BF16)
        dyb_ref[...] = _dot_nt(dpb, _side_by_side(pbw_ref)).astype(BF16)

    row = lambda w: pl.BlockSpec((TM_T, w), lambda i: (i, 0))
    vec = pl.BlockSpec((1, D), _CONST2)
    return pl.pallas_call(
        body, name=name, grid=(t // TM_T,),
        in_specs=[row(D), row(D), row(D), row(D), row(R_W),
                  pl.BlockSpec((D, D), _CONST2), pl.BlockSpec((N_DEV, Q_W, 128), _CONST3),
                  pl.BlockSpec((N_DEV, SGU_W, 128), _CONST3), vec, vec],
        out_specs=[row(D), row(D), row(D), row(R_W), row(Q_W), row(SGU_W), pl.BlockSpec((8, D), _CONST2),
                   pl.BlockSpec((8, R_W), _CONST2)],
        out_shape=[jax.ShapeDtypeStruct((t, D), F32), jax.ShapeDtypeStruct((t, D), BF16),
                   jax.ShapeDtypeStruct((t, D), BF16), jax.ShapeDtypeStruct((t, R_W), BF16),
                   jax.ShapeDtypeStruct((t, Q_W), BF16), jax.ShapeDtypeStruct((t, SGU_W), BF16),
                   jax.ShapeDtypeStruct((8, D), F32), jax.ShapeDtypeStruct((8, R_W), F32)],
        compiler_params=_ARB,
    )(dxn, z, pa, pb, hr, wo, pa_w, pb_w, ln_g, ln_b)


def _mixer_bwd(hm, dya, dyb, prob, psink, vn_g, vn_b, w_s, bsb, name):
    t = hm.shape[0]
    nt = t // TM
    ngrp = N_KV * NB

    def body(hm_ref, hprev_ref, prob_ref, psink_ref, dya_ref, dyb_ref, vng_ref, vnb_ref, ws_ref, bsb_ref,
             dhm_ref, gbm_ref, gsk_ref, gvn_ref, gws_ref, gbs_ref, dk_carry, dv_carry):
        gi = pl.program_id(0)

        @pl.when(gi == 0)
        def _():
            for r in (gbm_ref, gsk_ref, gvn_ref, gws_ref, gbs_ref, dk_carry, dv_carry):
                r[...] = jnp.zeros_like(r)

        tril = _tril_mask()
        wt = [jnp.where(tril, ws_ref[g], 0.0).astype(BF16) for g in range(4)]
        vng = vng_ref[...]
        ones8 = jnp.ones((8, BLK), BF16)

        def put(rows, col, val):
            dhm_ref[rows, col:col + val.shape[1]] = val.astype(BF16)

        for s in reversed(range(NB)):
            r0 = s * BLK
            rows = slice(r0, r0 + BLK)
            q = _cols(hm_ref, rows, C_Q, Q_W)
            kband = _band(hm_ref, hprev_ref, s, C_K)
            vband = _band(hm_ref, hprev_ref, s, C_V)
            g_a = _cols(hm_ref, rows, C_GA, Q_W)
            sg = _sigmoid(g_a)
            dya_v = _cols(dya_ref, rows, 0, Q_W)
            d_o = dya_v * (g_a * sg)
            o_pairs, dq_pairs = [None] * 4, [None] * 4
            dkband = jnp.zeros((2 * BLK, KV_W), F32)
            dvband = jnp.zeros((2 * BLK, KV_W), F32)
            kband_t, vband_t = kband.T, vband.T
            for hk in range(N_KV):
                q4 = _stack_q(q, hk)
                prob_b = prob_ref[N_KV * s + hk]
                p_sink = psink_ref[N_KV * s + hk][0:1, :]
                o_t = _dot(_head_rows(vband_t, hk), prob_b)
                o_pairs = _unstack_heads(o_t.T, hk, o_pairs)
                parts = []
                for g in range(4):
                    _, p, pos = _head_place(hk, g)
                    dp = d_o[:, BLK * p:BLK * (p + 1)]
                    parts.append(pltpu.roll(dp, 64, 1) if pos != hk else dp)
                do4 = _head_lanes(jnp.concatenate(parts, axis=0), hk)
                do4b = do4.astype(BF16)
                delta = _colsum(do4.T * o_t)
                vh = _head_lanes(vband, hk).astype(BF16)
                ds_t = prob_b.astype(F32) * (_dot_nt(vh, do4b) - delta)
                dsb = ds_t.astype(BF16)
                dq4_t = _dot(_head_rows(kband_t, hk), dsb)
                dq_pairs = _unstack_heads(dq4_t.T * ATTN_SCALE, hk, dq_pairs)
                dkband = dkband + _head_lanes(_dot(dsb, q4), hk)
                dvband = dvband + _dot(prob_b, do4b)
                dsk = p_sink * delta
                for g in range(4):
                    j = 4 * hk + g
                    tot = jnp.sum(dsk[:, BLK * g:BLK * (g + 1)], axis=1, keepdims=True)
                    gsk_ref[j:j + 1, :] += jnp.broadcast_to(-tot, (1, 128))
            attn = jnp.concatenate(o_pairs, axis=1)
            put(rows, C_Q, jnp.concatenate(dq_pairs, axis=1))
            put(rows, C_K, dkband[BLK:2 * BLK] + dk_carry[...])
            put(rows, C_V, dvband[BLK:2 * BLK] + dv_carry[...])
            dk_carry[...] = dkband[0:BLK]
            dv_carry[...] = dvband[0:BLK]
            put(rows, C_GA, dya_v * attn * (sg * (1.0 + g_a * (1.0 - sg))))
            u, du_du_b = _gelu_and_grad(_cols(hm_ref, rows, C_UB, SGU_W))
            v, dv_dv_b = _gelu_and_grad(_cols(hm_ref, rows, C_VB, SGU_W))
            g_b = _cols(hm_ref, rows, C_GB, SGU_W)
            vhat, rstd, vn, mixed = _sgu_fwd(u, v, vng, vnb_ref[...], wt, bsb_ref)
            sgb = _sigmoid(g_b)
            silu_b = g_b * sgb
            dyb_v = _cols(dyb_ref, rows, 0, SGU_W)
            du = dyb_v * mixed * silu_b
            dmixed = dyb_v * u * silu_b
            put(rows, C_GB, dyb_v * u * mixed * (sgb * (1.0 + g_b * (1.0 - sgb))))
            dvn_parts = []
            for g in range(4):
                cols = slice(BLK * g, BLK * (g + 1))
                dmg = dmixed[:, cols]
                dmgb = dmg.astype(BF16)
                dvn_parts.append(_dot_tn(wt[g], dmgb))
                gws_ref[g] += jnp.where(tril, _dot_nt(dmgb, vn[:, cols].astype(BF16)), 0.0)
                gbs_ref[g] += dmg
            dvn = jnp.concatenate(dvn_parts, axis=1)
            gvn_ref[0:1, :] += _colsum(dvn * vhat)
            gvn_ref[1:2, :] += _colsum(dvn)
            dv = _ln_bwd(dvn * vng, vhat, rstd)
            put(rows, C_UB, du * du_du_b)
            put(rows, C_VB, dv * dv_dv_b)
            gbm_ref[...] += _dot(ones8, dhm_ref[rows, :])

        @pl.when(gi == nt - 1)
        def _():
            for g in range(4):
                gbs_ref[g] = jnp.broadcast_to(jnp.sum(gbs_ref[g], axis=1, keepdims=True), (BLK, BLK))

    row = lambda w: pl.BlockSpec((TM, w), lambda g: (nt - 1 - g, 0))
    return pl.pallas_call(
        body, name=name, grid=(nt,),
        in_specs=_h_main_specs(nt, True) + [
            pl.BlockSpec((ngrp, 2 * BLK, 4 * BLK), lambda g: (nt - 1 - g, 0, 0)),
            pl.BlockSpec((ngrp, 8, 4 * BLK), lambda g: (nt - 1 - g, 0, 0)),
            row(Q_W), row(SGU_W),
            pl.BlockSpec((1, SGU_W), _CONST2), pl.BlockSpec((1, SGU_W), _CONST2),
            pl.BlockSpec((4, BLK, BLK), _CONST3), pl.BlockSpec((4, BLK, BLK), _CONST3)],
        out_specs=[row(MAIN_W), pl.BlockSpec((8, MAIN_W), _CONST2), pl.BlockSpec((8, 128), _CONST2),
                   pl.BlockSpec((8, SGU_W), _CONST2), pl.BlockSpec((4, BLK, BLK), _CONST3),
                   pl.BlockSpec((4, BLK, BLK), _CONST3)],
        out_shape=[jax.ShapeDtypeStruct((t, MAIN_W), BF16), jax.ShapeDtypeStruct((8, MAIN_W), F32),
                   jax.ShapeDtypeStruct((8, 128), F32), jax.ShapeDtypeStruct((8, SGU_W), F32),
                   jax.ShapeDtypeStruct((4, BLK, BLK), F32), jax.ShapeDtypeStruct((4, BLK, BLK), F32)],
        scratch_shapes=[pltpu.VMEM((BLK, KV_W), F32), pltpu.VMEM((BLK, KV_W), F32)],
        compiler_params=_ARB,
    )(hm, hm, prob, psink, dya, dyb, vn_g, vn_b, w_s, bsb)


def _dx_inproj(dz, dhm, dhr, w_t, after, name, ln_in=None):
    t = dz.shape[0]

    def body(dz_ref, dhm_ref, dhr_ref, wt_ref, after_ref, *rest):
        dx = (ALPHA * dz_ref[...] + after_ref[0:1, 0:1] + _dot(dhm_ref[...], wt_ref[0:MAIN_W, :])
              + _dot(dhr_ref[...], wt_ref[MAIN_W:IN_COLS, :]))
        if ln_in is None:
            rest[0][...] = dx
            return
        x_ref, g_ref, gx_ref, acc_ref = rest

        @pl.when(pl.program_id(0) == 0)
        def _():
            acc_ref[...] = jnp.zeros_like(acc_ref)

        xhat, rstd = _ln_stats(x_ref[...])
        gx_ref[...] = _ln_bwd(dx * g_ref[...], xhat, rstd)
        acc_ref[0:1, :] += _colsum(dx * xhat)
        acc_ref[1:2, :] += _colsum(dx)

    row = lambda w: pl.BlockSpec((TM_MM, w), lambda i: (i, 0))
    in_specs = [row(D), row(MAIN_W), row(R_W), pl.BlockSpec((IN_COLS, D), _CONST2), pl.BlockSpec((8, 128), _CONST2)]
    if ln_in is None:
        return pl.pallas_call(
            body, name=name, grid=(t // TM_MM,), in_specs=in_specs,
            out_specs=row(D), out_shape=jax.ShapeDtypeStruct((t, D), F32), compiler_params=_ARB,
        )(dz, dhm, dhr, w_t, after)
    return pl.pallas_call(
        body, name=name, grid=(t // TM_MM,), in_specs=in_specs + [row(D), pl.BlockSpec((1, D), _CONST2)],
        out_specs=[row(D), pl.BlockSpec((8, D), _CONST2)],
        out_shape=[jax.ShapeDtypeStruct((t, D), F32), jax.ShapeDtypeStruct((8, D), F32)], compiler_params=_ARB,
    )(dz, dhm, dhr, w_t, after, *ln_in)


def _wgrad(a, b, tm, name, rows=None, under=None, by_owner=False):
    t, m = a.shape
    n = b.shape[1]
    tk = min(t, 2048)
    nk = t // tk

    def body(a_ref, b_ref, *rest):
        o_ref, acc_ref = rest[-2:]
        k = pl.program_id(1)

        @pl.when(k == 0)
        def _():
            acc_ref[...] = jnp.zeros_like(acc_ref)

        acc_ref[...] += _dot_tn(a_ref[...].astype(BF16), b_ref[...].astype(BF16))

        @pl.when(k == nk - 1)
        def _():
            if by_owner:
                for j in range(N_DEV):
                    o_ref[j] = acc_ref[:, j * (n // N_DEV):(j + 1) * (n // N_DEV)].astype(BF16)
            else:
                o_ref[...] = acc_ref[...].astype(BF16)

    in_specs = [pl.BlockSpec((tk, tm), lambda j, k: (k, j)), pl.BlockSpec((tk, n), lambda j, k: (k, 0))]
    if by_owner:
        return pl.pallas_call(
            body, name=name, grid=(m // tm, nk), in_specs=in_specs,
            out_specs=pl.BlockSpec((N_DEV, tm, n // N_DEV), lambda j, k: (0, j, 0)),
            out_shape=jax.ShapeDtypeStruct((N_DEV, m, n // N_DEV), BF16), scratch_shapes=[pltpu.VMEM((tm, n), F32)],
            compiler_params=pltpu.CompilerParams(dimension_semantics=("arbitrary", "arbitrary"),
                                                 vmem_limit_bytes=VMEM_LIMIT),
        )(a, b)
    if under is None:
        out_rows, out_spec, operands, aliases = rows or m, pl.BlockSpec((tm, n), lambda j, k: (j, 0)), (a, b), {}
    else:
        out_rows = under.shape[0]
        first = out_rows - m
        assert first % 128 == 0 and tm % 128 == 0
        out_spec = pl.BlockSpec((pl.Element(tm), pl.Element(n)),
                                lambda j, k: (pl.multiple_of(first + j * tm, 128), 0))
        in_specs, operands, aliases = in_specs + [_ANY], (a, b, under), {2: 0}
    return pl.pallas_call(
        body, name=name, grid=(m // tm, nk), in_specs=in_specs, out_specs=out_spec,
        out_shape=jax.ShapeDtypeStruct((out_rows, n), BF16), input_output_aliases=aliases,
        scratch_shapes=[pltpu.VMEM((tm, n), F32)],
        compiler_params=pltpu.CompilerParams(dimension_semantics=("arbitrary", "arbitrary"), vmem_limit_bytes=VMEM_LIMIT),
    )(*operands)


_ANY = pl.BlockSpec(memory_space=pl.ANY)


def _place():
    return lax.axis_index("x"), lax.axis_index("y"), lax.axis_index("c")


def _forward_sibling(lands, name):
    n = len(lands)

    def body(*refs):
        l_refs = refs[n:2 * n]
        send_sems, recv_sems = refs[2 * n:]
        x, y, c = _place()
        chips = [(1 - x, y), (x, 1 - y), (1 - x, 1 - y)]

        def copy(a, j, core):
            rows = l_refs[a].at[4 * chips[j][0] + 2 * chips[j][1] + core]
            return pltpu.make_async_remote_copy(
                src_ref=rows, dst_ref=rows, send_sem=send_sems.at[3 * a + j], recv_sem=recv_sems.at[3 * a + j],
                device_id=(x, y, 1 - c), device_id_type=MESH)

        for a in range(n):
            for j in range(3):
                copy(a, j, c).start()
        for a in range(n):
            for j in range(3):
                copy(a, j, 1 - c).wait_recv()
                copy(a, j, c).wait_send()

    return pl.pallas_call(
        body, name=name, in_specs=[_ANY] * n, out_specs=[_ANY] * n,
        out_shape=[jax.ShapeDtypeStruct(v.shape, v.dtype) for v in lands],
        input_output_aliases={a: a for a in range(n)},
        scratch_shapes=[pltpu.SemaphoreType.DMA((3 * n,)), pltpu.SemaphoreType.DMA((3 * n,))],
    )(*lands)


def _swap_sibling(gs, name):
    n = len(gs)
    first = [0]
    for v in gs:
        first.append(first[-1] + v.shape[0])

    def body(*refs):
        g_refs, r_refs = refs[:n], refs[n:2 * n]
        send_sems, recv_sems = refs[2 * n:]
        x, y, c = _place()
        cps = [pltpu.make_async_remote_copy(
            src_ref=g_refs[a].at[q, 1 - c], dst_ref=r_refs[a].at[q], send_sem=send_sems.at[first[a] + q],
            recv_sem=recv_sems.at[first[a] + q], device_id=(x, y, 1 - c), device_id_type=MESH)
            for a in range(n) for q in range(gs[a].shape[0])]
        for cp in cps:
            cp.start()
        for cp in cps:
            cp.wait()

    return pl.pallas_call(
        body, name=name, in_specs=[_ANY] * n, out_specs=[_ANY] * n,
        out_shape=[jax.ShapeDtypeStruct(v.shape[:1] + v.shape[2:], v.dtype) for v in gs],
        scratch_shapes=[pltpu.SemaphoreType.DMA((first[-1],)), pltpu.SemaphoreType.DMA((first[-1],))],
    )(*gs)


def _row_tile(rows, lanes, cap):
    if rows * lanes * 4 <= (1 << 20):
        return rows
    return max(d for d in range(8, cap + 1, 8) if rows % d == 0 and (d % 16 == 0 or rows % 16 != 0))


def _pair_sums(gs, rs, name):
    n = len(gs)

    def add(g, r, dtype):
        return (g.astype(F32) + r.astype(F32)).astype(dtype)

    def body(c_ref, *refs):
        g_refs, r_refs, o_refs = refs[:n], refs[n:2 * n], refs[2 * n:]
        o_refs[0][...] = add(g_refs[0][0], r_refs[0][...], o_refs[0].dtype)

        @pl.when(pl.program_id(0) == 0)
        def _():
            for a in range(1, n):
                o_refs[a][...] = add(g_refs[a][:, 0], r_refs[a][...], o_refs[a].dtype)

    def whole(shape, mine):
        if mine:
            return pl.BlockSpec(shape, lambda q, c_ref: (0, c_ref[0]) + (0,) * (len(shape) - 2))
        return pl.BlockSpec(shape, lambda q, c_ref: (0,) * len(shape))

    big = gs[0].shape
    return pl.pallas_call(
        body, name=name,
        grid_spec=pltpu.PrefetchScalarGridSpec(
            num_scalar_prefetch=1, grid=(big[0],),
            in_specs=[pl.BlockSpec((1, 1) + big[2:], lambda q, c_ref: (q, c_ref[0], 0, 0))]
            + [whole(g.shape[:1] + (1,) + g.shape[2:], True) for g in gs[1:]]
            + [pl.BlockSpec((1,) + big[2:], lambda q, c_ref: (q, 0, 0))]
            + [whole(r.shape, False) for r in rs[1:]],
            out_specs=[pl.BlockSpec((1,) + big[2:], lambda q, c_ref: (q, 0, 0))]
            + [whole(r.shape, False) for r in rs[1:]]),
        out_shape=[jax.ShapeDtypeStruct(r.shape, g.dtype) for g, r in zip(gs, rs)],
        compiler_params=_ARB,
    )(lax.axis_index("c").astype(jnp.int32).reshape(1), *gs, *rs)


def _adamw(parts, w, m, v, name, own=None):
    nl = len(parts)
    ns, rows, l = parts[0].shape
    tr = _row_tile(rows, l * ns, 304)
    nt = rows // tr
    c1 = 1.0 - ADAM_B1 ** ADAM_STEP
    c2 = 1.0 - ADAM_B2 ** ADAM_STEP

    def body(q_ref, *refs):
        own_refs = refs[:nl] if own is not None else None
        p_refs = refs[-7 - nl:-7]
        w_ref, m_ref, v_ref, g_ref, d_ref, nm_ref, nv_ref = refs[-7:]
        layer = pl.program_id(0)
        g = None
        for j in range(nl):
            gj = None
            for k in range(ns):
                term = p_refs[j][k].astype(F32)
                if own_refs is not None:
                    term = jnp.where(q_ref[0] == k, own_refs[j][0].astype(F32), term)
                gj = term if gj is None else gj + term
            g = gj if g is None else jnp.where(layer == j, gj, g)
        g_ref[...] = g
        nm = ADAM_B1 * m_ref[...] + (1.0 - ADAM_B1) * g
        nv = ADAM_B2 * v_ref[...] + (1.0 - ADAM_B2) * (g * g)
        nm_ref[...] = nm
        nv_ref[...] = nv
        d_ref[...] = -ADAM_LR * ((nm / c1) / (jnp.sqrt(nv / c2) + ADAM_EPS) + ADAM_WD * w_ref[...])

    def tile_of(j):
        return lambda la, i, q: jnp.where(la == j, i, jnp.where(la < j, 0, nt - 1))

    row = pl.BlockSpec((tr, l), lambda la, i, q: (la * nt + i, 0))
    own_specs = [] if own is None else [
        pl.BlockSpec((1, tr, l), lambda la, i, q, j=j: (q[0], tile_of(j)(la, i, q), 0)) for j in range(nl)]
    part_specs = [pl.BlockSpec((ns, tr, l), lambda la, i, q, j=j: (0, tile_of(j)(la, i, q), 0)) for j in range(nl)]
    chip = (2 * lax.axis_index("x") + lax.axis_index("y")).astype(jnp.int32).reshape(1)
    return pl.pallas_call(
        body, name=name,
        grid_spec=pltpu.PrefetchScalarGridSpec(
            num_scalar_prefetch=1, grid=(nl, nt),
            in_specs=own_specs + part_specs + [row, row, row], out_specs=[row] * 4),
        out_shape=[jax.ShapeDtypeStruct((nl * rows, l), F32)] * 4,
        compiler_params=pltpu.CompilerParams(dimension_semantics=("arbitrary", "arbitrary"), vmem_limit_bytes=VMEM_LIMIT),
    )(chip, *([] if own is None else own), *parts, w, m, v)


_HBM = pl.BlockSpec(memory_space=pltpu.HBM)
_SEM = pl.BlockSpec(memory_space=pltpu.SEMAPHORE)
_EFFECT = pltpu.SideEffectType.DATAFLOW_SIDE_EFFECTING


def _plan_all(x, y, c):
    me = 4 * x + 2 * y + c
    peers = [(x, y, 1 - c), (1 - x, y, c), (x, 1 - y, c), (1 - x, 1 - y, c),
             (1 - x, y, 1 - c), (x, 1 - y, 1 - c), (1 - x, 1 - y, 1 - c)]
    return [(None, me, p, 4 * p[0] + 2 * p[1] + p[2]) for p in peers]


def _plan_near(x, y, c):
    me = 4 * x + 2 * y + c
    peers = [(x, y, 1 - c), (1 - x, y, c), (x, 1 - y, c), (1 - x, 1 - y, c)]
    return [(None, me, p, 4 * p[0] + 2 * p[1] + p[2]) for p in peers]


def _plan_sibling(x, y, c):
    return [(2 * q + 1 - c, q, (x, y, 1 - c), q) for q in range(4)]


def _plan_chips(x, y, c):
    me = 2 * x + y
    return [(2 * qx + qy, me, (qx, qy, c), 2 * qx + qy) for qx, qy in ((1 - x, y), (x, 1 - y), (1 - x, 1 - y))]


def _split_copies(plan, src_refs, land_refs, send_sems, recv_sems, arrival):
    n = len(src_refs)
    entries = plan(*_place())
    per = len(entries)
    cps = []
    for a in range(n):
        for k, (src_slot, dst_slot, peer, back_slot) in enumerate(entries):
            src = src_refs[a] if src_slot is None else src_refs[a].at[src_slot]
            cps.append(pltpu.make_async_remote_copy(
                src_ref=src, dst_ref=land_refs[a].at[back_slot if arrival else dst_slot],
                send_sem=send_sems.at[per * a + k], recv_sem=recv_sems.at[per * a + k],
                device_id=peer, device_id_type=MESH))
    return cps


def _split_start(srcs, lands, plan, per, name):
    n = len(srcs)

    def body(*refs):
        for cp in _split_copies(plan, refs[:n], refs[n:2 * n], refs[2 * n], refs[2 * n + 1], False):
            cp.start()
        refs[-1][...] = jnp.zeros_like(refs[-1])

    both = list(srcs) + list(lands)
    outs = pl.pallas_call(
        body, name=name,
        out_shape=(pltpu.SemaphoreType.DMA((per * n,)), pltpu.SemaphoreType.DMA((per * n,)),
                   *[pltpu.HBM(v.shape, v.dtype) for v in both], jax.ShapeDtypeStruct((8, 128), F32)),
        in_specs=[_HBM] * (2 * n),
        out_specs=(_SEM, _SEM, *[_HBM] * (2 * n), pl.BlockSpec(memory_space=pltpu.VMEM)),
        input_output_aliases={i: 2 + i for i in range(2 * n)},
        compiler_params=pltpu.CompilerParams(has_side_effects=_EFFECT),
    )(*[pltpu.with_memory_space_constraint(v, pltpu.HBM) for v in both])
    return outs[0], outs[1], list(outs[2:2 + 2 * n]), outs[-1]


def _split_wait(send_sems, recv_sems, thru, plan, after, name):
    n = len(thru) // 2

    def body(*refs):
        for cp in _split_copies(plan, refs[:n], refs[n:2 * n], refs[2 * n], refs[2 * n + 1], True):
            cp.wait_send()
            cp.wait_recv()

    outs = pl.pallas_call(
        body, name=name, out_shape=tuple(pltpu.HBM(v.shape, v.dtype) for v in thru),
        in_specs=[_HBM] * (2 * n) + [_SEM, _SEM, pl.BlockSpec(memory_space=pl.ANY)],
        out_specs=[_HBM] * (2 * n), input_output_aliases={i: i for i in range(2 * n)},
        compiler_params=pltpu.CompilerParams(has_side_effects=_EFFECT),
    )(*thru, send_sems, recv_sems, after)
    return list(outs[:n]), list(outs[n:])


_SMALL_IN = ("ln_in_g", "ln_in_b")
_SMALL_ROWS = ("w_s", "b_s", "sinks")
_SMALL_LANES = ("b_in", "vn_g", "vn_b", "b_out", "ln_g", "ln_b")


def _tile_rows(a):
    return -(-a.size // 1024) * 8


def _pack_small(d, names):
    return jnp.concatenate([jnp.pad(d[n].reshape(-1), (0, (-d[n].size) % 1024)).reshape(-1, 128) for n in names])


def _adamw_small(parts, own, w, m, v, pieces, axis, name):
    c1 = 1.0 - ADAM_B1 ** ADAM_STEP
    c2 = 1.0 - ADAM_B2 ** ADAM_STEP
    shapes = [tuple(p if d == axis else s for d, s in enumerate(w.shape)) for p in pieces]

    def body(q_ref, own_ref, p_ref, w_ref, m_ref, v_ref, *o_refs):
        g = None
        for k in range(4):
            term = jnp.where(q_ref[0] == k, own_ref[0], p_ref[k])
            g = term if g is None else g + term
        nm = ADAM_B1 * m_ref[...] + (1.0 - ADAM_B1) * g
        nv = ADAM_B2 * v_ref[...] + (1.0 - ADAM_B2) * (g * g)
        delta = -ADAM_LR * ((nm / c1) / (jnp.sqrt(nv / c2) + ADAM_EPS) + ADAM_WD * w_ref[...])
        for k, val in enumerate((g, delta, nm, nv)):
            off = 0
            for j, p in enumerate(pieces):
                o_refs[k * len(pieces) + j][...] = val[off:off + p] if axis == 0 else val[:, off:off + p]
                off += p

    whole = pl.BlockSpec(w.shape, lambda i, q: (0, 0))
    chip = (2 * lax.axis_index("x") + lax.axis_index("y")).astype(jnp.int32).reshape(1)
    outs = pl.pallas_call(
        body, name=name,
        grid_spec=pltpu.PrefetchScalarGridSpec(
            num_scalar_prefetch=1, grid=(1,),
            in_specs=[pl.BlockSpec((1,) + w.shape, lambda i, q: (q[0], 0, 0)),
                      pl.BlockSpec((4,) + w.shape, lambda i, q: (0, 0, 0)), whole, whole, whole],
            out_specs=[pl.BlockSpec(s, lambda i, q: (0, 0)) for s in shapes] * 4),
        out_shape=[jax.ShapeDtypeStruct(s, F32) for s in shapes] * 4, compiler_params=_ARB,
    )(chip, own, parts, w, m, v)
    return [outs[k * len(pieces):(k + 1) * len(pieces)] for k in range(4)]


def kernel(x, ln_in_g, ln_in_b, w_in, b_in, sinks, vn_g, vn_b, w_s, b_s, p_a, p_b, w_out, b_out, ln_g, ln_b, loss_target, m_ln_in_g, m_ln_in_b, m_w_in, m_b_in, m_sinks, m_vn_g, m_vn_b, m_w_s, m_b_s, m_p_a, m_p_b, m_w_out, m_b_out, m_ln_g, m_ln_b, v_ln_in_g, v_ln_in_b, v_w_in, v_b_in, v_sinks, v_vn_g, v_vn_b, v_w_s, v_b_s, v_p_a, v_p_b, v_w_out, v_b_out, v_ln_g, v_ln_b):
    nseq, seq, _ = x.shape
    t = nseq * seq
    nblk_seq = seq // BLK
    x2 = x.reshape(t, D)
    tgt = loss_target.reshape(t, D)

    def turned(a):
        return jnp.swapaxes(a, 1, 2)

    w_in_t = turned(w_in)

    def blocks(l):
        return [w_in_t[l].astype(BF16), p_a[l].astype(BF16), p_b[l].astype(BF16), w_out[l].astype(BF16)]

    def full_weights(g):
        w_t_full = g[0].reshape(IN_COLS, D)
        wo_full = g[3].reshape(D, D)
        return dict(w_t=w_t_full, pa=g[1], pb=g[2], wo=wo_full)

    def landing(bs):
        return [lax.empty((N_DEV,) + v.shape, v.dtype) for v in bs]

    def with_own(landed, sent):
        return [lax.dynamic_update_index_in_dim(g, b, me, 0) for g, b in zip(landed, sent)]

    me = 4 * lax.axis_index("x") + 2 * lax.axis_index("y") + lax.axis_index("c")
    blocks0 = blocks(0)
    a_send, a_recv, a_thru, a_token = _split_start(blocks0[:1], landing(blocks0[:1]), _plan_near, 4,
                                                   "allgather_w_in0_start")
    rest0 = [b + a_token[0, 0].astype(BF16) for b in blocks0[1:]]
    b_send, b_recv, b_thru, b_token = _split_start(rest0, landing(rest0), _plan_all, 7, "allgather_rest0_start")
    xs = [_ln_fwd(x2, ln_in_g + b_token[0, 0], ln_in_b, "ln_in_fwd")]
    sent, landed = _split_wait(a_send, a_recv, a_thru, _plan_near, xs[0], "allgather_w_in0_wait")
    gathered0 = with_own(_forward_sibling(landed, "allgather_w_in0_forward"), sent)
    blocks1, gathered0 = lax.optimization_barrier((blocks(1), gathered0))
    ag_send, ag_recv, ag_thru, ag_token = _split_start(blocks1, landing(blocks1), _plan_all, 7,
                                                       "allgather_weights1_start")
    weights = [None, None]
    bsb = jnp.broadcast_to(b_s[:, :, :, None], (DEPTH, 4, BLK, BLK))
    bias = _band_bias()

    saved = []
    for l in range(DEPTH):
        if l == 1:
            sent, landed = _split_wait(ag_send, ag_recv, ag_thru, _plan_all, xs[1], "allgather_weights1_wait")
            weights[1] = full_weights(with_own(landed, sent))
        w_t = weights[l]["w_t"] if l else gathered0[0].reshape(IN_COLS, D)
        last = l == DEPTH - 1
        b_l = b_in[l].reshape(1, -1) + (ag_token[0, 0] if l == 0 else 0.0)
        hm, hr = _inproj(xs[l], w_t, b_l, f"inproj{l}")
        ya, yb, prob, psink = _mixer_fwd(hm, sinks[l], bias, vn_g[l].reshape(1, -1), vn_b[l].reshape(1, -1),
                                         w_s[l], bsb[l], nblk_seq, f"mixer_fwd{l}")
        if l == 0:
            sent, landed = _split_wait(b_send, b_recv, b_thru, _plan_all, ya, "allgather_rest0_wait")
            weights[0] = full_weights(gathered0 + with_own(landed, sent))
        wl = weights[l]
        outs = _tail_fwd(xs[l], ya, yb, hr, wl["pa"], wl["pb"], wl["wo"], b_out[l].reshape(1, D),
                         ln_g[l].reshape(1, D), ln_b[l].reshape(1, D), f"tail_fwd{l}", last)
        saved.append((hm, hr, ya, yb, prob, psink) + tuple(outs[:4]))
        if not last:
            xs.append(outs[4])

    small = {n: [None] * DEPTH for n in _SMALL_ROWS + _SMALL_LANES}

    def pack_rows(d):
        return _pack_small(d, _SMALL_ROWS)

    def pack_lanes(d):
        return jnp.concatenate([d[n] for n in _SMALL_LANES], axis=1)
    names = ("w_in", "p_a", "p_b", "w_out")
    token = jnp.zeros((8, 128), F32)
    dx = tgt
    split = [None] * DEPTH
    for l in reversed(range(DEPTH)):
        hm, hr, ya, yb, prob, psink, pa, pb, merged, z = saved[l]
        wl = weights[l]
        dz, dpa, dpb, dhr, dya, dyb, acc, gbr = _tail_bwd(
            dx, z, pa, pb, hr, wl["wo"], wl["pa"], wl["pb"], ln_g[l].reshape(1, D) + token[0, 0],
            ln_b[l].reshape(1, D), f"tail_bwd{l}", l == DEPTH - 1)
        if l == DEPTH - 1:
            sq_err = acc[3:4, 0:128]
        dhm, gbm, gsk, gvn, gws, gbs = _mixer_bwd(
            hm, dya, dyb, prob, psink, vn_g[l].reshape(1, -1), vn_b[l].reshape(1, -1), w_s[l], bsb[l],
            f"mixer_bwd{l}")
        grads = {"w_in": _wgrad(dhr, xs[l], R_W // 2, f"wgrad_in_route{l}",
                                under=_wgrad(dhm, xs[l], MAIN_W // 2, f"wgrad_in_main{l}", rows=IN_COLS)),
                 "p_a": _wgrad(ya, dpa, Q_W, f"wgrad_pa{l}", by_owner=True),
                 "p_b": _wgrad(yb, dpb, SGU_W, f"wgrad_pb{l}", by_owner=True),
                 "w_out": _wgrad(merged, dz, D, f"wgrad_out{l}")}
        small["b_in"][l] = jnp.concatenate([gbm[0], gbr[0]])
        small["sinks"][l] = gsk[:, 0]
        small["vn_g"][l], small["vn_b"][l] = gvn[0], gvn[1]
        small["w_s"][l], small["b_s"][l] = gws, gbs[:, :, 0]
        small["ln_g"][l], small["ln_b"][l], small["b_out"][l] = acc[0], acc[1], acc[2]
        parts = [grads[n].reshape((4, 2, -1, grads[n].shape[-1])) for n in names]
        if l == 0:
            stacked = {n: jnp.stack(v) for n, v in small.items()}
            for packed in (pack_rows(stacked), pack_lanes(stacked)):
                parts.append(jnp.broadcast_to(packed[None, None], (1, 2) + packed.shape))
        if l == 0:
            from_sib = _swap_sibling(parts, f"rs_sibling{l}")
        else:
            halves = [p.reshape((N_DEV,) + p.shape[2:]) for p in parts]
            sib = _split_start(halves, [lax.empty((4,) + p.shape[2:], p.dtype) for p in parts], _plan_sibling, 4,
                               f"rs_sibling{l}_start")
            dx = _dx_inproj(dz, dhm, dhr, wl["w_t"], sib[3], f"dx_inproj{l}")
            halves, from_sib = _split_wait(sib[0], sib[1], sib[2], _plan_sibling, dx, f"rs_sibling{l}_wait")
            parts = [h.reshape(p.shape) for h, p in zip(halves, parts)]
        pair = list(_pair_sums(parts, from_sib, f"pair_sums{l}"))
        if l == 0:
            pair[4:] = [jnp.broadcast_to(p, (4,) + p.shape[1:]) for p in pair[4:]]
        lands = [jnp.zeros(p.shape, p.dtype) for p in pair]
        split[l] = _split_start(pair, lands, _plan_chips, 3, f"rs_chips{l}_start")
        token = split[l][3]
        if l == 0:
            grad_x, acc_in = _dx_inproj(dz, dhm, dhr, wl["w_t"], token, f"dx_inproj{l}",
                                        ln_in=(x2, ln_in_g.reshape(1, D)))
    last = [acc_in, jnp.broadcast_to(sq_err, (8, 128))]
    ln_send, ln_recv, ln_thru, ln_token = _split_start(last, landing(last), _plan_all, 7, "allgather_ln_in_start")

    given = {"w_in": (w_in_t, turned(m_w_in), turned(v_w_in)), "p_a": (p_a, m_p_a, v_p_a),
             "p_b": (p_b, m_p_b, v_p_b), "w_out": (w_out, m_w_out, v_w_out)}
    waited = [_split_wait(split[l][0], split[l][1], split[l][2], _plan_chips, ln_token, f"rs_chips{l}_wait")
              for l in range(DEPTH)]
    res = {}
    for a, n in enumerate(names):
        rows, lanes = waited[0][1][a].shape[1:]
        outs = _adamw([waited[l][1][a] for l in range(DEPTH)], *[v.reshape(DEPTH * rows, lanes) for v in given[n]],
                      f"adamw_{n}", own=[waited[l][0][a] for l in range(DEPTH)])
        res[n] = [o.reshape(given[n][0].shape) for o in outs]
    res["w_in"] = [turned(o) for o in res["w_in"]]

    w_small = dict(ln_in_g=ln_in_g, ln_in_b=ln_in_b, b_in=b_in, sinks=sinks, vn_g=vn_g, vn_b=vn_b, w_s=w_s, b_s=b_s,
                   b_out=b_out, ln_g=ln_g, ln_b=ln_b)
    m_small = dict(ln_in_g=m_ln_in_g, ln_in_b=m_ln_in_b, b_in=m_b_in, sinks=m_sinks, vn_g=m_vn_g, vn_b=m_vn_b,
                   w_s=m_w_s, b_s=m_b_s, b_out=m_b_out, ln_g=m_ln_g, ln_b=m_ln_b)
    v_small = dict(ln_in_g=v_ln_in_g, ln_in_b=v_ln_in_b, b_in=v_b_in, sinks=v_sinks, vn_g=v_vn_g, vn_b=v_vn_b,
                   w_s=v_w_s, b_s=v_b_s, b_out=v_b_out, ln_g=v_ln_g, ln_b=v_ln_b)
    by_rows = _adamw_small(waited[0][1][4], waited[0][0][4], *[pack_rows(d) for d in (w_small, m_small, v_small)],
                           [_tile_rows(w_small[n]) for n in _SMALL_ROWS], 0, "adamw_small_rows")
    by_lanes = _adamw_small(waited[0][1][5], waited[0][0][5], *[pack_lanes(d) for d in (w_small, m_small, v_small)],
                            [w_small[n].shape[1] for n in _SMALL_LANES], 1, "adamw_small_lanes")
    sent, landed = _split_wait(ln_send, ln_recv, ln_thru, _plan_all, by_lanes[0][0], "allgather_ln_in_wait")
    all_in, all_sq = with_own(landed, sent)
    loss = jnp.sum(all_sq[:, 0, 0]) * (0.5 / D)
    outs_in = _adamw([all_in], *[jnp.pad(jnp.stack([d[n] for n in _SMALL_IN]), ((0, 6), (0, 0)))
                                 for d in (w_small, m_small, v_small)], "adamw_ln_in")
    for k in range(4):
        u = {n: o.reshape(-1)[:w_small[n].size].reshape(w_small[n].shape) for n, o in zip(_SMALL_ROWS, by_rows[k])}
        u.update(zip(_SMALL_LANES, by_lanes[k]))
        u.update({n: outs_in[k][r] for r, n in enumerate(_SMALL_IN)})
        for n in u:
            res.setdefault(n, [None] * 4)[k] = u[n]

    order = ("ln_in_g", "ln_in_b", "w_in", "b_in", "sinks", "vn_g", "vn_b", "w_s", "b_s", "p_a", "p_b", "w_out",
             "b_out", "ln_g", "ln_b")
    return (loss, grad_x.reshape(x.shape), *[res[n][0] for n in order], *[res[n][1] for n in order],
            *[res[n][2] for n in order], *[res[n][3] for n in order])
```

```python
import jax
import jax.numpy as jnp
from jax import lax
from jax.experimental import pallas as pl
from jax.experimental.pallas import tpu as pltpu

F32 = jnp.float32
BF16 = jnp.bfloat16

D = 1024
BLK = 128
N_KV = 2
Q_W, KV_W, SGU_W = 512, 128, 512
C_Q, C_K, C_V, C_GA, C_UB, C_VB, C_GB = 0, 512, 640, 768, 1280, 1792, 2304
MAIN_W = 2816
R_W = 2048
IN_COLS = MAIN_W + R_W
N_DEV = 8

DEPTH = 2
ALPHA = (2.0 * DEPTH) ** 0.25
LN_EPS = 1e-5
ATTN_SCALE = 0.125
NEG = float(jnp.finfo(jnp.float32).min)

ADAM_LR, ADAM_B1, ADAM_B2, ADAM_EPS, ADAM_WD, ADAM_STEP = 0.001, 0.9, 0.999, 1e-08, 0.01, 10

TM = 512
TM_EW = 1024
TM_MM = 512
TM_T = 512
NB = TM // BLK
MESH = pl.DeviceIdType.MESH
VMEM_LIMIT = 56 * 1024 * 1024

_ARB = pltpu.CompilerParams(dimension_semantics=("arbitrary",), vmem_limit_bytes=VMEM_LIMIT)


def _sigmoid(x):
    return 0.5 + 0.5 * jnp.tanh(0.5 * x)


_GELU_C = 0.7978845608028654
_GELU_A = 0.044715


def _gelu_parts(x):
    x2 = x * x
    t = jnp.tanh(x * (_GELU_C + (_GELU_C * _GELU_A) * x2))
    hx = 0.5 * x
    return hx, t, x2


def _gelu(x):
    hx, t, _ = _gelu_parts(x)
    return hx + hx * t


def _gelu_and_grad(x):
    hx, t, x2 = _gelu_parts(x)
    grad = 0.5 + 0.5 * t + (hx - hx * (t * t)) * (_GELU_C + (3.0 * _GELU_C * _GELU_A) * x2)
    return hx + hx * t, grad


def _ln_stats(x):
    mu = jnp.mean(x, axis=-1, keepdims=True)
    xc = x - mu
    var = jnp.mean(xc * xc, axis=-1, keepdims=True)
    rstd = lax.rsqrt(var + LN_EPS)
    return xc * rstd, rstd


def _ln_bwd(dy_g, xhat, rstd):
    m1 = jnp.mean(dy_g, axis=-1, keepdims=True)
    m2 = jnp.mean(dy_g * xhat, axis=-1, keepdims=True)
    return rstd * (dy_g - m1 - xhat * m2)


def _colsum(x):
    return jnp.sum(x, axis=0, keepdims=True)


def _dot(a, b):
    return jnp.dot(a, b, preferred_element_type=F32)


def _dot_nt(a, b):
    return lax.dot_general(a, b, (((1,), (1,)), ((), ())), preferred_element_type=F32)


def _side_by_side(gathered_ref):
    return jnp.concatenate([gathered_ref[j] for j in range(N_DEV)], axis=1)


def _dot_tn(a, b):
    return lax.dot_general(a, b, (((0,), (0,)), ((), ())), preferred_element_type=F32)


def _head_place(hk, g):
    j = 4 * hk + g
    return j, j // 2, j % 2


def _head_rows(x, hk):
    d = lax.broadcasted_iota(jnp.int32, x.shape, 0)
    return jnp.where((d >= 64 * hk) & (d < 64 * hk + 64), x, 0.0).astype(BF16)


def _head_lanes(x, hk):
    d = lax.broadcasted_iota(jnp.int32, x.shape, 1)
    return jnp.where((d >= 64 * hk) & (d < 64 * hk + 64), x, 0.0)


def _band_bias():
    kpos = lax.broadcasted_iota(jnp.int32, (2 * BLK, 4 * BLK), 0)
    row = lax.broadcasted_iota(jnp.int32, (2 * BLK, 4 * BLK), 1) & (BLK - 1)
    band = (kpos > row) & (kpos <= row + BLK)
    return jnp.stack([jnp.where(band, 0.0, NEG), jnp.where(band & (kpos >= BLK), 0.0, NEG)]).astype(F32)


def _stack_q(q, hk):
    parts = []
    for g in range(4):
        _, p, pos = _head_place(hk, g)
        qp = q[:, BLK * p:BLK * (p + 1)] * ATTN_SCALE
        if pos != hk:
            qp = pltpu.roll(qp, 64, 1)
        parts.append(qp.astype(BF16))
    return jnp.concatenate(parts, axis=0)


def _attn_probs(q4, kh, hk, sinks_ref, bias):
    s_t = _dot_nt(kh, q4) + bias
    sink_row = jnp.concatenate(
        [jnp.full((1, BLK), sinks_ref[4 * hk + g], F32) for g in range(4)], axis=1)
    m = jnp.maximum(jnp.max(s_t, axis=0, keepdims=True), sink_row)
    p_un = jnp.exp(s_t - m)
    e_sink = jnp.exp(sink_row - m)
    inv = 1.0 / (jnp.sum(p_un, axis=0, keepdims=True) + e_sink)
    return (p_un * inv).astype(BF16), e_sink * inv


def _unstack_heads(x4, hk, pairs):
    for g in range(4):
        _, p, pos = _head_place(hk, g)
        xg = x4[BLK * g:BLK * (g + 1)]
        if pos != hk:
            xg = pltpu.roll(xg, 64, 1)
        pairs[p] = xg if pairs[p] is None else pairs[p] + xg
    return pairs


def _attn_fwd(q, kband, vband, sinks_ref, bias, save):
    pairs = [None] * 4
    vband_t = vband.T
    for hk in range(N_KV):
        prob_t, p_sink = _attn_probs(_stack_q(q, hk), _head_lanes(kband, hk).astype(BF16), hk, sinks_ref, bias)
        save(hk, prob_t, p_sink)
        o_t = _dot(_head_rows(vband_t, hk), prob_t)
        pairs = _unstack_heads(o_t.T, hk, pairs)
    return jnp.concatenate(pairs, axis=1)


def _tril_mask():
    r = lax.broadcasted_iota(jnp.int32, (BLK, BLK), 0)
    c = lax.broadcasted_iota(jnp.int32, (BLK, BLK), 1)
    return c <= r


def _sgu_fwd(u, v, vn_g, vn_b, wt, bsb_ref):
    vhat, rstd = _ln_stats(v)
    vn = vhat * vn_g + vn_b
    mixed = jnp.concatenate(
        [_dot(wt[g], vn[:, BLK * g:BLK * (g + 1)].astype(BF16)) + bsb_ref[g] for g in range(4)], axis=1)
    return vhat, rstd, vn, mixed


def _cols(ref, rows, col, width):
    return ref[rows, col:col + width].astype(F32)


def _band(hm_ref, hprev_ref, s, col):
    r0 = s * BLK
    cur = hm_ref[r0:r0 + BLK, col:col + KV_W]
    if s == 0:
        off = 0 if col == C_K else KV_W
        prev = hprev_ref[:, off:off + KV_W]
    else:
        prev = hm_ref[r0 - BLK:r0, col:col + KV_W]
    return jnp.concatenate([prev, cur], axis=0).astype(F32)


def _h_main_specs(nt, rev):
    def tile(g):
        return nt - 1 - g if rev else g

    return [pl.BlockSpec((TM, MAIN_W), lambda g: (tile(g), 0)),
            pl.BlockSpec((BLK, 2 * KV_W), lambda g: (jnp.maximum(tile(g) * NB - 1, 0), 2))]


_CONST2 = lambda g: (0, 0)
_CONST3 = lambda g: (0, 0, 0)


def _ln_fwd(x, g, b, name):
    t = x.shape[0]

    def body(x_ref, g_ref, b_ref, o_ref):
        xhat, _ = _ln_stats(x_ref[...])
        o_ref[...] = xhat * g_ref[...] + b_ref[...]

    return pl.pallas_call(
        body, name=name, grid=(t // TM_EW,),
        in_specs=[pl.BlockSpec((TM_EW, D), lambda i: (i, 0)), pl.BlockSpec((1, D), _CONST2),
                  pl.BlockSpec((1, D), _CONST2)],
        out_specs=pl.BlockSpec((TM_EW, D), lambda i: (i, 0)),
        out_shape=jax.ShapeDtypeStruct((t, D), F32), compiler_params=_ARB,
    )(x, g.reshape(1, D), b.reshape(1, D))


def _inproj(x, w_t, b, name):
    t = x.shape[0]

    def body(x_ref, wt_ref, b_ref, hm_ref, hr_ref):
        xb = x_ref[...].astype(BF16)
        hm_ref[...] = (_dot_nt(xb, wt_ref[0:MAIN_W, :]) + b_ref[:, 0:MAIN_W]).astype(BF16)
        hr_ref[...] = (_dot_nt(xb, wt_ref[MAIN_W:IN_COLS, :]) + b_ref[:, MAIN_W:IN_COLS]).astype(BF16)

    return pl.pallas_call(
        body, name=name, grid=(t // TM_MM,),
        in_specs=[pl.BlockSpec((TM_MM, D), lambda i: (i, 0)),
                  pl.BlockSpec((IN_COLS, D), _CONST2), pl.BlockSpec((1, IN_COLS), _CONST2)],
        out_specs=[pl.BlockSpec((TM_MM, MAIN_W), lambda i: (i, 0)), pl.BlockSpec((TM_MM, R_W), lambda i: (i, 0))],
        out_shape=[jax.ShapeDtypeStruct((t, MAIN_W), BF16), jax.ShapeDtypeStruct((t, R_W), BF16)],
        compiler_params=_ARB,
    )(x, w_t, b)


def _mixer_fwd(hm, sinks, bias, vn_g, vn_b, w_s, bsb, nblk_seq, name):
    t = hm.shape[0]
    nt = t // TM

    def body(sinks_ref, hm_ref, hprev_ref, bias_ref, vng_ref, vnb_ref, ws_ref, bsb_ref,
             ya_ref, yb_ref, prob_ref, psink_ref):
        i = pl.program_id(0)
        tril = _tril_mask()
        wt = [jnp.where(tril, ws_ref[g], 0.0).astype(BF16) for g in range(4)]
        for s in range(NB):
            r0 = s * BLK
            rows = slice(r0, r0 + BLK)
            bias = bias_ref[jnp.where((i * NB + s) % nblk_seq == 0, 1, 0)]

            def save(hk, prob_t, p_sink, s=s):
                prob_ref[N_KV * s + hk] = prob_t
                psink_ref[N_KV * s + hk] = jnp.broadcast_to(p_sink, (8, 4 * BLK))

            attn = _attn_fwd(_cols(hm_ref, rows, C_Q, Q_W), _band(hm_ref, hprev_ref, s, C_K),
                             _band(hm_ref, hprev_ref, s, C_V), sinks_ref, bias, save)
            g_a = _cols(hm_ref, rows, C_GA, Q_W)
            ya_ref[rows, :] = (attn * (g_a * _sigmoid(g_a))).astype(BF16)
            u = _gelu(_cols(hm_ref, rows, C_UB, SGU_W))
            mixed = _sgu_fwd(u, _gelu(_cols(hm_ref, rows, C_VB, SGU_W)), vng_ref[...], vnb_ref[...], wt, bsb_ref)[-1]
            g_b = _cols(hm_ref, rows, C_GB, SGU_W)
            yb_ref[rows, :] = (u * mixed * (g_b * _sigmoid(g_b))).astype(BF16)

    ngrp = N_KV * NB
    return pl.pallas_call(
        body, name=name, grid=(nt,),
        in_specs=[pl.BlockSpec(memory_space=pltpu.SMEM)] + _h_main_specs(nt, False) + [
            pl.BlockSpec((2, 2 * BLK, 4 * BLK), _CONST3),
            pl.BlockSpec((1, SGU_W), _CONST2), pl.BlockSpec((1, SGU_W), _CONST2),
            pl.BlockSpec((4, BLK, BLK), _CONST3), pl.BlockSpec((4, BLK, BLK), _CONST3)],
        out_specs=[pl.BlockSpec((TM, Q_W), lambda i: (i, 0)), pl.BlockSpec((TM, SGU_W), lambda i: (i, 0)),
                   pl.BlockSpec((ngrp, 2 * BLK, 4 * BLK), lambda i: (i, 0, 0)),
                   pl.BlockSpec((ngrp, 8, 4 * BLK), lambda i: (i, 0, 0))],
        out_shape=[jax.ShapeDtypeStruct((t, Q_W), BF16), jax.ShapeDtypeStruct((t, SGU_W), BF16),
                   jax.ShapeDtypeStruct((nt * ngrp, 2 * BLK, 4 * BLK), BF16),
                   jax.ShapeDtypeStruct((nt * ngrp, 8, 4 * BLK), F32)],
        compiler_params=_ARB,
    )(sinks, hm, hm, bias, vn_g, vn_b, w_s, bsb)


def _tail_fwd(x, ya, yb, hr, pa_w, pb_w, wo, b_out, ln_g, ln_b, name, last):
    t = x.shape[0]

    def body(x_ref, ya_ref, yb_ref, hr_ref, paw_ref, pbw_ref, wo_ref, bo_ref, g_ref, b_ref,
             pa_ref, pb_ref, mg_ref, z_ref, *xn_ref):
        pa = _dot(ya_ref[...], _side_by_side(paw_ref))
        pb = _dot(yb_ref[...], _side_by_side(pbw_ref))
        pa_ref[...] = pa.astype(BF16)
        pb_ref[...] = pb.astype(BF16)
        everything = slice(None)
        merged = _sigmoid(_cols(hr_ref, everything, 0, D)) * pa + _sigmoid(_cols(hr_ref, everything, D, D)) * pb
        mb = merged.astype(BF16)
        mg_ref[...] = mb
        z = ALPHA * x_ref[...] + (_dot(mb, wo_ref[...]) + bo_ref[...])
        z_ref[...] = z
        if not last:
            zhat, _ = _ln_stats(z)
            xn_ref[0][...] = zhat * g_ref[...] + b_ref[...]

    row = lambda w: pl.BlockSpec((TM_T, w), lambda i: (i, 0))
    vec = pl.BlockSpec((1, D), _CONST2)
    n_f32 = 1 if last else 2
    return pl.pallas_call(
        body, name=name, grid=(t // TM_T,),
        in_specs=[row(D), row(Q_W), row(SGU_W), row(R_W),
                  pl.BlockSpec((N_DEV, Q_W, 128), _CONST3), pl.BlockSpec((N_DEV, SGU_W, 128), _CONST3),
                  pl.BlockSpec((D, D), _CONST2), vec, vec, vec],
        out_specs=[row(D)] * (3 + n_f32),
        out_shape=[jax.ShapeDtypeStruct((t, D), BF16)] * 3 + [jax.ShapeDtypeStruct((t, D), F32)] * n_f32,
        compiler_params=_ARB,
    )(x, ya, yb, hr, pa_w, pb_w, wo, b_out, ln_g, ln_b)


def _tail_bwd(dxn, z, pa, pb, hr, wo, pa_w, pb_w, ln_g, ln_b, name, from_loss):
    t = dxn.shape[0]

    def body(dxn_ref, z_ref, pa_ref, pb_ref, hr_ref, wo_ref, paw_ref, pbw_ref, g_ref, b_ref,
             dz_ref, dpa_ref, dpb_ref, dhr_ref, dya_ref, dyb_ref, acc_ref, gbr_ref):
        @pl.when(pl.program_id(0) == 0)
        def _():
            acc_ref[...] = jnp.zeros_like(acc_ref)
            gbr_ref[...] = jnp.zeros_like(gbr_ref)

        zhat, rstd = _ln_stats(z_ref[...])
        if from_loss:
            err = zhat * g_ref[...] + b_ref[...] - dxn_ref[...]
            dxn_v = err * (1.0 / D)
            sq = jnp.sum(jnp.sum(err * err, axis=1, keepdims=True), axis=0, keepdims=True)
            acc_ref[3:4, :] += jnp.broadcast_to(sq, (1, D))
        else:
            dxn_v = dxn_ref[...]
        dz = _ln_bwd(dxn_v * g_ref[...], zhat, rstd)
        dz_ref[...] = dz
        acc_ref[0:1, :] += _colsum(dxn_v * zhat)
        acc_ref[1:2, :] += _colsum(dxn_v)
        acc_ref[2:3, :] += _colsum(dz)
        dmerged = _dot_nt(dz.astype(BF16), wo_ref[...])
        everything = slice(None)
        sa = _sigmoid(_cols(hr_ref, everything, 0, D))
        sb = _sigmoid(_cols(hr_ref, everything, D, D))
        dpa = (dmerged * sa).astype(BF16)
        dpb = (dmerged * sb).astype(BF16)
        dpa_ref[...] = dpa
        dpb_ref[...] = dpb
        dra = dmerged * pa_ref[...].astype(F32) * (sa * (1.0 - sa))
        drb = dmerged * pb_ref[...].astype(F32) * (sb * (1.0 - sb))
        dhr_ref[:, 0:D] = dra.astype(BF16)
        dhr_ref[:, D:2 * D] = drb.astype(BF16)
        gbr_ref[0:1, 0:D] += _colsum(dra)
        gbr_ref[0:1, D:2 * D] += _colsum(drb)
        dya_ref[...] = _dot_nt(dpa, _side_by_side(paw_ref)).astype(BF16)
        dyb_ref[...] = _dot_nt(dpb, _side_by_side(pbw_ref)).astype(BF16)

    row = lambda w: pl.BlockSpec((TM_T, w), lambda i: (i, 0))
    vec = pl.BlockSpec((1, D), _CONST2)
    return pl.pallas_call(
        body, name=name, grid=(t // TM_T,),
        in_specs=[row(D), row(D), row(D), row(D), row(R_W),
                  pl.BlockSpec((D, D), _CONST2), pl.BlockSpec((N_DEV, Q_W, 128), _CONST3),
                  pl.BlockSpec((N_DEV, SGU_W, 128), _CONST3), vec, vec],
        out_specs=[row(D), row(D), row(D), row(R_W), row(Q_W), row(SGU_W), pl.BlockSpec((8, D), _CONST2),
                   pl.BlockSpec((8, R_W), _CONST2)],
        out_shape=[jax.ShapeDtypeStruct((t, D), F32), jax.ShapeDtypeStruct((t, D), BF16),
                   jax.ShapeDtypeStruct((t, D), BF16), jax.ShapeDtypeStruct((t, R_W), BF16),
                   jax.ShapeDtypeStruct((t, Q_W), BF16), jax.ShapeDtypeStruct((t, SGU_W), BF16),
                   jax.ShapeDtypeStruct((8, D), F32), jax.ShapeDtypeStruct((8, R_W), F32)],
        compiler_params=_ARB,
    )(dxn, z, pa, pb, hr, wo, pa_w, pb_w, ln_g, ln_b)


def _mixer_bwd(hm, dya, dyb, prob, psink, vn_g, vn_b, w_s, bsb, name):
    t = hm.shape[0]
    nt = t // TM
    ngrp = N_KV * NB

    def body(hm_ref, hprev_ref, prob_ref, psink_ref, dya_ref, dyb_ref, vng_ref, vnb_ref, ws_ref, bsb_ref,
             dhm_ref, gbm_ref, gsk_ref, gvn_ref, gws_ref, gbs_ref, dk_carry, dv_carry):
        gi = pl.program_id(0)

        @pl.when(gi == 0)
        def _():
            for r in (gbm_ref, gsk_ref, gvn_ref, gws_ref, gbs_ref, dk_carry, dv_carry):
                r[...] = jnp.zeros_like(r)

        tril = _tril_mask()
        wt = [jnp.where(tril, ws_ref[g], 0.0).astype(BF16) for g in range(4)]
        vng = vng_ref[...]
        ones8 = jnp.ones((8, BLK), BF16)

        def put(rows, col, val):
            dhm_ref[rows, col:col + val.shape[1]] = val.astype(BF16)

        for s in reversed(range(NB)):
            r0 = s * BLK
            rows = slice(r0, r0 + BLK)
            q = _cols(hm_ref, rows, C_Q, Q_W)
            kband = _band(hm_ref, hprev_ref, s, C_K)
            vband = _band(hm_ref, hprev_ref, s, C_V)
            g_a = _cols(hm_ref, rows, C_GA, Q_W)
            sg = _sigmoid(g_a)
            dya_v = _cols(dya_ref, rows, 0, Q_W)
            d_o = dya_v * (g_a * sg)
            o_pairs, dq_pairs = [None] * 4, [None] * 4
            dkband = jnp.zeros((2 * BLK, KV_W), F32)
            dvband = jnp.zeros((2 * BLK, KV_W), F32)
            kband_t, vband_t = kband.T, vband.T
            for hk in range(N_KV):
                q4 = _stack_q(q, hk)
                prob_b = prob_ref[N_KV * s + hk]
                p_sink = psink_ref[N_KV * s + hk][0:1, :]
                o_t = _dot(_head_rows(vband_t, hk), prob_b)
                o_pairs = _unstack_heads(o_t.T, hk, o_pairs)
                parts = []
                for g in range(4):
                    _, p, pos = _head_place(hk, g)
                    dp = d_o[:, BLK * p:BLK * (p + 1)]
                    parts.append(pltpu.roll(dp, 64, 1) if pos != hk else dp)
                do4 = _head_lanes(jnp.concatenate(parts, axis=0), hk)
                do4b = do4.astype(BF16)
                delta = _colsum(do4.T * o_t)
                vh = _head_lanes(vband, hk).astype(BF16)
                ds_t = prob_b.astype(F32) * (_dot_nt(vh, do4b) - delta)
                dsb = ds_t.astype(BF16)
                dq4_t = _dot(_head_rows(kband_t, hk), dsb)
                dq_pairs = _unstack_heads(dq4_t.T * ATTN_SCALE, hk, dq_pairs)
                dkband = dkband + _head_lanes(_dot(dsb, q4), hk)
                dvband = dvband + _dot(prob_b, do4b)
                dsk = p_sink * delta
                for g in range(4):
                    j = 4 * hk + g
                    tot = jnp.sum(dsk[:, BLK * g:BLK * (g + 1)], axis=1, keepdims=True)
                    gsk_ref[j:j + 1, :] += jnp.broadcast_to(-tot, (1, 128))
            attn = jnp.concatenate(o_pairs, axis=1)
            put(rows, C_Q, jnp.concatenate(dq_pairs, axis=1))
            put(rows, C_K, dkband[BLK:2 * BLK] + dk_carry[...])
            put(rows, C_V, dvband[BLK:2 * BLK] + dv_carry[...])
            dk_carry[...] = dkband[0:BLK]
            dv_carry[...] = dvband[0:BLK]
            put(rows, C_GA, dya_v * attn * (sg * (1.0 + g_a * (1.0 - sg))))
            u, du_du_b = _gelu_and_grad(_cols(hm_ref, rows, C_UB, SGU_W))
            v, dv_dv_b = _gelu_and_grad(_cols(hm_ref, rows, C_VB, SGU_W))
            g_b = _cols(hm_ref, rows, C_GB, SGU_W)
            vhat, rstd, vn, mixed = _sgu_fwd(u, v, vng, vnb_ref[...], wt, bsb_ref)
            sgb = _sigmoid(g_b)
            silu_b = g_b * sgb
            dyb_v = _cols(dyb_ref, rows, 0, SGU_W)
            du = dyb_v * mixed * silu_b
            dmixed = dyb_v * u * silu_b
            put(rows, C_GB, dyb_v * u * mixed * (sgb * (1.0 + g_b * (1.0 - sgb))))
            dvn_parts = []
            for g in range(4):
                cols = slice(BLK * g, BLK * (g + 1))
                dmg = dmixed[:, cols]
                dmgb = dmg.astype(BF16)
                dvn_parts.append(_dot_tn(wt[g], dmgb))
                gws_ref[g] += jnp.where(tril, _dot_nt(dmgb, vn[:, cols].astype(BF16)), 0.0)
                gbs_ref[g] += dmg
            dvn = jnp.concatenate(dvn_parts, axis=1)
            gvn_ref[0:1, :] += _colsum(dvn * vhat)
            gvn_ref[1:2, :] += _colsum(dvn)
            dv = _ln_bwd(dvn * vng, vhat, rstd)
            put(rows, C_UB, du * du_du_b)
            put(rows, C_VB, dv * dv_dv_b)
            gbm_ref[...] += _dot(ones8, dhm_ref[rows, :])

        @pl.when(gi == nt - 1)
        def _():
            for g in range(4):
                gbs_ref[g] = jnp.broadcast_to(jnp.sum(gbs_ref[g], axis=1, keepdims=True), (BLK, BLK))

    row = lambda w: pl.BlockSpec((TM, w), lambda g: (nt - 1 - g, 0))
    return pl.pallas_call(
        body, name=name, grid=(nt,),
        in_specs=_h_main_specs(nt, True) + [
            pl.BlockSpec((ngrp, 2 * BLK, 4 * BLK), lambda g: (nt - 1 - g, 0, 0)),
            pl.BlockSpec((ngrp, 8, 4 * BLK), lambda g: (nt - 1 - g, 0, 0)),
            row(Q_W), row(SGU_W),
            pl.BlockSpec((1, SGU_W), _CONST2), pl.BlockSpec((1, SGU_W), _CONST2),
            pl.BlockSpec((4, BLK, BLK), _CONST3), pl.BlockSpec((4, BLK, BLK), _CONST3)],
        out_specs=[row(MAIN_W), pl.BlockSpec((8, MAIN_W), _CONST2), pl.BlockSpec((8, 128), _CONST2),
                   pl.BlockSpec((8, SGU_W), _CONST2), pl.BlockSpec((4, BLK, BLK), _CONST3),
                   pl.BlockSpec((4, BLK, BLK), _CONST3)],
        out_shape=[jax.ShapeDtypeStruct((t, MAIN_W), BF16), jax.ShapeDtypeStruct((8, MAIN_W), F32),
                   jax.ShapeDtypeStruct((8, 128), F32), jax.ShapeDtypeStruct((8, SGU_W), F32),
                   jax.ShapeDtypeStruct((4, BLK, BLK), F32), jax.ShapeDtypeStruct((4, BLK, BLK), F32)],
        scratch_shapes=[pltpu.VMEM((BLK, KV_W), F32), pltpu.VMEM((BLK, KV_W), F32)],
        compiler_params=_ARB,
    )(hm, hm, prob, psink, dya, dyb, vn_g, vn_b, w_s, bsb)


def _dx_inproj(dz, dhm, dhr, w_t, after, name, ln_in=None):
    t = dz.shape[0]

    def body(dz_ref, dhm_ref, dhr_ref, wt_ref, after_ref, *rest):
        dx = (ALPHA * dz_ref[...] + after_ref[0:1, 0:1] + _dot(dhm_ref[...], wt_ref[0:MAIN_W, :])
              + _dot(dhr_ref[...], wt_ref[MAIN_W:IN_COLS, :]))
        if ln_in is None:
            rest[0][...] = dx
            return
        x_ref, g_ref, gx_ref, acc_ref = rest

        @pl.when(pl.program_id(0) == 0)
        def _():
            acc_ref[...] = jnp.zeros_like(acc_ref)

        xhat, rstd = _ln_stats(x_ref[...])
        gx_ref[...] = _ln_bwd(dx * g_ref[...], xhat, rstd)
        acc_ref[0:1, :] += _colsum(dx * xhat)
        acc_ref[1:2, :] += _colsum(dx)

    row = lambda w: pl.BlockSpec((TM_MM, w), lambda i: (i, 0))
    in_specs = [row(D), row(MAIN_W), row(R_W), pl.BlockSpec((IN_COLS, D), _CONST2), pl.BlockSpec((8, 128), _CONST2)]
    if ln_in is None:
        return pl.pallas_call(
            body, name=name, grid=(t // TM_MM,), in_specs=in_specs,
            out_specs=row(D), out_shape=jax.ShapeDtypeStruct((t, D), F32), compiler_params=_ARB,
        )(dz, dhm, dhr, w_t, after)
    return pl.pallas_call(
        body, name=name, grid=(t // TM_MM,), in_specs=in_specs + [row(D), pl.BlockSpec((1, D), _CONST2)],
        out_specs=[row(D), pl.BlockSpec((8, D), _CONST2)],
        out_shape=[jax.ShapeDtypeStruct((t, D), F32), jax.ShapeDtypeStruct((8, D), F32)], compiler_params=_ARB,
    )(dz, dhm, dhr, w_t, after, *ln_in)


def _wgrad(a, b, tm, name, rows=None, under=None, by_owner=False):
    t, m = a.shape
    n = b.shape[1]
    tk = min(t, 2048)
    nk = t // tk

    def body(a_ref, b_ref, *rest):
        o_ref, acc_ref = rest[-2:]
        k = pl.program_id(1)

        @pl.when(k == 0)
        def _():
            acc_ref[...] = jnp.zeros_like(acc_ref)

        acc_ref[...] += _dot_tn(a_ref[...].astype(BF16), b_ref[...].astype(BF16))

        @pl.when(k == nk - 1)
        def _():
            if by_owner:
                for j in range(N_DEV):
                    o_ref[j] = acc_ref[:, j * (n // N_DEV):(j + 1) * (n // N_DEV)].astype(BF16)
            else:
                o_ref[...] = acc_ref[...].astype(BF16)

    in_specs = [pl.BlockSpec((tk, tm), lambda j, k: (k, j)), pl.BlockSpec((tk, n), lambda j, k: (k, 0))]
    if by_owner:
        return pl.pallas_call(
            body, name=name, grid=(m // tm, nk), in_specs=in_specs,
            out_specs=pl.BlockSpec((N_DEV, tm, n // N_DEV), lambda j, k: (0, j, 0)),
            out_shape=jax.ShapeDtypeStruct((N_DEV, m, n // N_DEV), BF16), scratch_shapes=[pltpu.VMEM((tm, n), F32)],
            compiler_params=pltpu.CompilerParams(dimension_semantics=("arbitrary", "arbitrary"),
                                                 vmem_limit_bytes=VMEM_LIMIT),
        )(a, b)
    if under is None:
        out_rows, out_spec, operands, aliases = rows or m, pl.BlockSpec((tm, n), lambda j, k: (j, 0)), (a, b), {}
    else:
        out_rows = under.shape[0]
        first = out_rows - m
        assert first % 128 == 0 and tm % 128 == 0
        out_spec = pl.BlockSpec((pl.Element(tm), pl.Element(n)),
                                lambda j, k: (pl.multiple_of(first + j * tm, 128), 0))
        in_specs, operands, aliases = in_specs + [_ANY], (a, b, under), {2: 0}
    return pl.pallas_call(
        body, name=name, grid=(m // tm, nk), in_specs=in_specs, out_specs=out_spec,
        out_shape=jax.ShapeDtypeStruct((out_rows, n), BF16), input_output_aliases=aliases,
        scratch_shapes=[pltpu.VMEM((tm, n), F32)],
        compiler_params=pltpu.CompilerParams(dimension_semantics=("arbitrary", "arbitrary"), vmem_limit_bytes=VMEM_LIMIT),
    )(*operands)


_ANY = pl.BlockSpec(memory_space=pl.ANY)


def _place():
    return lax.axis_index("x"), lax.axis_index("y"), lax.axis_index("c")


def _forward_sibling(lands, name):
    n = len(lands)

    def body(*refs):
        l_refs = refs[n:2 * n]
        send_sems, recv_sems = refs[2 * n:]
        x, y, c = _place()
        chips = [(1 - x, y), (x, 1 - y), (1 - x, 1 - y)]

        def copy(a, j, core):
            rows = l_refs[a].at[4 * chips[j][0] + 2 * chips[j][1] + core]
            return pltpu.make_async_remote_copy(
                src_ref=rows, dst_ref=rows, send_sem=send_sems.at[3 * a + j], recv_sem=recv_sems.at[3 * a + j],
                device_id=(x, y, 1 - c), device_id_type=MESH)

        for a in range(n):
            for j in range(3):
                copy(a, j, c).start()
        for a in range(n):
            for j in range(3):
                copy(a, j, 1 - c).wait_recv()
                copy(a, j, c).wait_send()

    return pl.pallas_call(
        body, name=name, in_specs=[_ANY] * n, out_specs=[_ANY] * n,
        out_shape=[jax.ShapeDtypeStruct(v.shape, v.dtype) for v in lands],
        input_output_aliases={a: a for a in range(n)},
        scratch_shapes=[pltpu.SemaphoreType.DMA((3 * n,)), pltpu.SemaphoreType.DMA((3 * n,))],
    )(*lands)


def _swap_sibling(gs, name):
    n = len(gs)
    first = [0]
    for v in gs:
        first.append(first[-1] + v.shape[0])

    def body(*refs):
        g_refs, r_refs = refs[:n], refs[n:2 * n]
        send_sems, recv_sems = refs[2 * n:]
        x, y, c = _place()
        cps = [pltpu.make_async_remote_copy(
            src_ref=g_refs[a].at[q, 1 - c], dst_ref=r_refs[a].at[q], send_sem=send_sems.at[first[a] + q],
            recv_sem=recv_sems.at[first[a] + q], device_id=(x, y, 1 - c), device_id_type=MESH)
            for a in range(n) for q in range(gs[a].shape[0])]
        for cp in cps:
            cp.start()
        for cp in cps:
            cp.wait()

    return pl.pallas_call(
        body, name=name, in_specs=[_ANY] * n, out_specs=[_ANY] * n,
        out_shape=[jax.ShapeDtypeStruct(v.shape[:1] + v.shape[2:], v.dtype) for v in gs],
        scratch_shapes=[pltpu.SemaphoreType.DMA((first[-1],)), pltpu.SemaphoreType.DMA((first[-1],))],
    )(*gs)


def _row_tile(rows, lanes, cap):
    if rows * lanes * 4 <= (1 << 20):
        return rows
    return max(d for d in range(8, cap + 1, 8) if rows % d == 0 and (d % 16 == 0 or rows % 16 != 0))


def _pair_sums(gs, rs, name):
    n = len(gs)

    def add(g, r, dtype):
        return (g.astype(F32) + r.astype(F32)).astype(dtype)

    def body(c_ref, *refs):
        g_refs, r_refs, o_refs = refs[:n], refs[n:2 * n], refs[2 * n:]
        o_refs[0][...] = add(g_refs[0][0], r_refs[0][...], o_refs[0].dtype)

        @pl.when(pl.program_id(0) == 0)
        def _():
            for a in range(1, n):
                o_refs[a][...] = add(g_refs[a][:, 0], r_refs[a][...], o_refs[a].dtype)

    def whole(shape, mine):
        if mine:
            return pl.BlockSpec(shape, lambda q, c_ref: (0, c_ref[0]) + (0,) * (len(shape) - 2))
        return pl.BlockSpec(shape, lambda q, c_ref: (0,) * len(shape))

    big = gs[0].shape
    return pl.pallas_call(
        body, name=name,
        grid_spec=pltpu.PrefetchScalarGridSpec(
            num_scalar_prefetch=1, grid=(big[0],),
            in_specs=[pl.BlockSpec((1, 1) + big[2:], lambda q, c_ref: (q, c_ref[0], 0, 0))]
            + [whole(g.shape[:1] + (1,) + g.shape[2:], True) for g in gs[1:]]
            + [pl.BlockSpec((1,) + big[2:], lambda q, c_ref: (q, 0, 0))]
            + [whole(r.shape, False) for r in rs[1:]],
            out_specs=[pl.BlockSpec((1,) + big[2:], lambda q, c_ref: (q, 0, 0))]
            + [whole(r.shape, False) for r in rs[1:]]),
        out_shape=[jax.ShapeDtypeStruct(r.shape, g.dtype) for g, r in zip(gs, rs)],
        compiler_params=_ARB,
    )(lax.axis_index("c").astype(jnp.int32).reshape(1), *gs, *rs)


def _adamw(parts, w, m, v, name, own=None):
    nl = len(parts)
    ns, rows, l = parts[0].shape
    tr = _row_tile(rows, l * ns, 304)
    nt = rows // tr
    c1 = 1.0 - ADAM_B1 ** ADAM_STEP
    c2 = 1.0 - ADAM_B2 ** ADAM_STEP

    def body(q_ref, *refs):
        own_refs = refs[:nl] if own is not None else None
        p_refs = refs[-7 - nl:-7]
        w_ref, m_ref, v_ref, g_ref, d_ref, nm_ref, nv_ref = refs[-7:]
        layer = pl.program_id(0)
        g = None
        for j in range(nl):
            gj = None
            for k in range(ns):
                term = p_refs[j][k].astype(F32)
                if own_refs is not None:
                    term = jnp.where(q_ref[0] == k, own_refs[j][0].astype(F32), term)
                gj = term if gj is None else gj + term
            g = gj if g is None else jnp.where(layer == j, gj, g)
        g_ref[...] = g
        nm = ADAM_B1 * m_ref[...] + (1.0 - ADAM_B1) * g
        nv = ADAM_B2 * v_ref[...] + (1.0 - ADAM_B2) * (g * g)
        nm_ref[...] = nm
        nv_ref[...] = nv
        d_ref[...] = -ADAM_LR * ((nm / c1) / (jnp.sqrt(nv / c2) + ADAM_EPS) + ADAM_WD * w_ref[...])

    def tile_of(j):
        return lambda la, i, q: jnp.where(la == j, i, jnp.where(la < j, 0, nt - 1))

    row = pl.BlockSpec((tr, l), lambda la, i, q: (la * nt + i, 0))
    own_specs = [] if own is None else [
        pl.BlockSpec((1, tr, l), lambda la, i, q, j=j: (q[0], tile_of(j)(la, i, q), 0)) for j in range(nl)]
    part_specs = [pl.BlockSpec((ns, tr, l), lambda la, i, q, j=j: (0, tile_of(j)(la, i, q), 0)) for j in range(nl)]
    chip = (2 * lax.axis_index("x") + lax.axis_index("y")).astype(jnp.int32).reshape(1)
    return pl.pallas_call(
        body, name=name,
        grid_spec=pltpu.PrefetchScalarGridSpec(
            num_scalar_prefetch=1, grid=(nl, nt),
            in_specs=own_specs + part_specs + [row, row, row], out_specs=[row] * 4),
        out_shape=[jax.ShapeDtypeStruct((nl * rows, l), F32)] * 4,
        compiler_params=pltpu.CompilerParams(dimension_semantics=("arbitrary", "arbitrary"), vmem_limit_bytes=VMEM_LIMIT),
    )(chip, *([] if own is None else own), *parts, w, m, v)


_HBM = pl.BlockSpec(memory_space=pltpu.HBM)
_SEM = pl.BlockSpec(memory_space=pltpu.SEMAPHORE)
_EFFECT = pltpu.SideEffectType.DATAFLOW_SIDE_EFFECTING


def _plan_all(x, y, c):
    me = 4 * x + 2 * y + c
    peers = [(x, y, 1 - c), (1 - x, y, c), (x, 1 - y, c), (1 - x, 1 - y, c),
             (1 - x, y, 1 - c), (x, 1 - y, 1 - c), (1 - x, 1 - y, 1 - c)]
    return [(None, me, p, 4 * p[0] + 2 * p[1] + p[2]) for p in peers]


def _plan_near(x, y, c):
    me = 4 * x + 2 * y + c
    peers = [(x, y, 1 - c), (1 - x, y, c), (x, 1 - y, c), (1 - x, 1 - y, c)]
    return [(None, me, p, 4 * p[0] + 2 * p[1] + p[2]) for p in peers]


def _plan_sibling(x, y, c):
    return [(2 * q + 1 - c, q, (x, y, 1 - c), q) for q in range(4)]


def _plan_chips(x, y, c):
    me = 2 * x + y
    return [(2 * qx + qy, me, (qx, qy, c), 2 * qx + qy) for qx, qy in ((1 - x, y), (x, 1 - y), (1 - x, 1 - y))]


def _split_copies(plan, src_refs, land_refs, send_sems, recv_sems, arrival):
    n = len(src_refs)
    entries = plan(*_place())
    per = len(entries)
    cps = []
    for a in range(n):
        for k, (src_slot, dst_slot, peer, back_slot) in enumerate(entries):
            src = src_refs[a] if src_slot is None else src_refs[a].at[src_slot]
            cps.append(pltpu.make_async_remote_copy(
                src_ref=src, dst_ref=land_refs[a].at[back_slot if arrival else dst_slot],
                send_sem=send_sems.at[per * a + k], recv_sem=recv_sems.at[per * a + k],
                device_id=peer, device_id_type=MESH))
    return cps


def _split_start(srcs, lands, plan, per, name):
    n = len(srcs)

    def body(*refs):
        for cp in _split_copies(plan, refs[:n], refs[n:2 * n], refs[2 * n], refs[2 * n + 1], False):
            cp.start()
        refs[-1][...] = jnp.zeros_like(refs[-1])

    both = list(srcs) + list(lands)
    outs = pl.pallas_call(
        body, name=name,
        out_shape=(pltpu.SemaphoreType.DMA((per * n,)), pltpu.SemaphoreType.DMA((per * n,)),
                   *[pltpu.HBM(v.shape, v.dtype) for v in both], jax.ShapeDtypeStruct((8, 128), F32)),
        in_specs=[_HBM] * (2 * n),
        out_specs=(_SEM, _SEM, *[_HBM] * (2 * n), pl.BlockSpec(memory_space=pltpu.VMEM)),
        input_output_aliases={i: 2 + i for i in range(2 * n)},
        compiler_params=pltpu.CompilerParams(has_side_effects=_EFFECT),
    )(*[pltpu.with_memory_space_constraint(v, pltpu.HBM) for v in both])
    return outs[0], outs[1], list(outs[2:2 + 2 * n]), outs[-1]


def _split_wait(send_sems, recv_sems, thru, plan, after, name):
    n = len(thru) // 2

    def body(*refs):
        for cp in _split_copies(plan, refs[:n], refs[n:2 * n], refs[2 * n], refs[2 * n + 1], True):
            cp.wait_send()
            cp.wait_recv()

    outs = pl.pallas_call(
        body, name=name, out_shape=tuple(pltpu.HBM(v.shape, v.dtype) for v in thru),
        in_specs=[_HBM] * (2 * n) + [_SEM, _SEM, pl.BlockSpec(memory_space=pl.ANY)],
        out_specs=[_HBM] * (2 * n), input_output_aliases={i: i for i in range(2 * n)},
        compiler_params=pltpu.CompilerParams(has_side_effects=_EFFECT),
    )(*thru, send_sems, recv_sems, after)
    return list(outs[:n]), list(outs[n:])


_SMALL_IN = ("ln_in_g", "ln_in_b")
_SMALL_ROWS = ("w_s", "b_s", "sinks")
_SMALL_LANES = ("b_in", "vn_g", "vn_b", "b_out", "ln_g", "ln_b")


def _tile_rows(a):
    return -(-a.size // 1024) * 8


def _pack_small(d, names):
    return jnp.concatenate([jnp.pad(d[n].reshape(-1), (0, (-d[n].size) % 1024)).reshape(-1, 128) for n in names])


def _adamw_small(parts, own, w, m, v, pieces, axis, name):
    c1 = 1.0 - ADAM_B1 ** ADAM_STEP
    c2 = 1.0 - ADAM_B2 ** ADAM_STEP
    shapes = [tuple(p if d == axis else s for d, s in enumerate(w.shape)) for p in pieces]

    def body(q_ref, own_ref, p_ref, w_ref, m_ref, v_ref, *o_refs):
        g = None
        for k in range(4):
            term = jnp.where(q_ref[0] == k, own_ref[0], p_ref[k])
            g = term if g is None else g + term
        nm = ADAM_B1 * m_ref[...] + (1.0 - ADAM_B1) * g
        nv = ADAM_B2 * v_ref[...] + (1.0 - ADAM_B2) * (g * g)
        delta = -ADAM_LR * ((nm / c1) / (jnp.sqrt(nv / c2) + ADAM_EPS) + ADAM_WD * w_ref[...])
        for k, val in enumerate((g, delta, nm, nv)):
            off = 0
            for j, p in enumerate(pieces):
                o_refs[k * len(pieces) + j][...] = val[off:off + p] if axis == 0 else val[:, off:off + p]
                off += p

    whole = pl.BlockSpec(w.shape, lambda i, q: (0, 0))
    chip = (2 * lax.axis_index("x") + lax.axis_index("y")).astype(jnp.int32).reshape(1)
    outs = pl.pallas_call(
        body, name=name,
        grid_spec=pltpu.PrefetchScalarGridSpec(
            num_scalar_prefetch=1, grid=(1,),
            in_specs=[pl.BlockSpec((1,) + w.shape, lambda i, q: (q[0], 0, 0)),
                      pl.BlockSpec((4,) + w.shape, lambda i, q: (0, 0, 0)), whole, whole, whole],
            out_specs=[pl.BlockSpec(s, lambda i, q: (0, 0)) for s in shapes] * 4),
        out_shape=[jax.ShapeDtypeStruct(s, F32) for s in shapes] * 4, compiler_params=_ARB,
    )(chip, own, parts, w, m, v)
    return [outs[k * len(pieces):(k + 1) * len(pieces)] for k in range(4)]


def kernel(x, ln_in_g, ln_in_b, w_in, b_in, sinks, vn_g, vn_b, w_s, b_s, p_a, p_b, w_out, b_out, ln_g, ln_b, loss_target, m_ln_in_g, m_ln_in_b, m_w_in, m_b_in, m_sinks, m_vn_g, m_vn_b, m_w_s, m_b_s, m_p_a, m_p_b, m_w_out, m_b_out, m_ln_g, m_ln_b, v_ln_in_g, v_ln_in_b, v_w_in, v_b_in, v_sinks, v_vn_g, v_vn_b, v_w_s, v_b_s, v_p_a, v_p_b, v_w_out, v_b_out, v_ln_g, v_ln_b):
    nseq, seq, _ = x.shape
    t = nseq * seq
    nblk_seq = seq // BLK
    x2 = x.reshape(t, D)
    tgt = loss_target.reshape(t, D)

    def turned(a):
        return jnp.swapaxes(a, 1, 2)

    w_in_t = turned(w_in)

    def blocks(l):
        return [w_in_t[l].astype(BF16), p_a[l].astype(BF16), p_b[l].astype(BF16), w_out[l].astype(BF16)]

    def full_weights(g):
        w_t_full = g[0].reshape(IN_COLS, D)
        wo_full = g[3].reshape(D, D)
        return dict(w_t=w_t_full, pa=g[1], pb=g[2], wo=wo_full)

    def landing(bs):
        return [lax.empty((N_DEV,) + v.shape, v.dtype) for v in bs]

    def with_own(landed, sent):
        return [lax.dynamic_update_index_in_dim(g, b, me, 0) for g, b in zip(landed, sent)]

    me = 4 * lax.axis_index("x") + 2 * lax.axis_index("y") + lax.axis_index("c")
    blocks0 = blocks(0)
    a_send, a_recv, a_thru, a_token = _split_start(blocks0[:1], landing(blocks0[:1]), _plan_near, 4,
                                                   "allgather_w_in0_start")
    rest0 = [b + a_token[0, 0].astype(BF16) for b in blocks0[1:]]
    b_send, b_recv, b_thru, b_token = _split_start(rest0, landing(rest0), _plan_all, 7, "allgather_rest0_start")
    xs = [_ln_fwd(x2, ln_in_g + b_token[0, 0], ln_in_b, "ln_in_fwd")]
    sent, landed = _split_wait(a_send, a_recv, a_thru, _plan_near, xs[0], "allgather_w_in0_wait")
    gathered0 = with_own(_forward_sibling(landed, "allgather_w_in0_forward"), sent)
    blocks1, gathered0 = lax.optimization_barrier((blocks(1), gathered0))
    ag_send, ag_recv, ag_thru, ag_token = _split_start(blocks1, landing(blocks1), _plan_all, 7,
                                                       "allgather_weights1_start")
    weights = [None, None]
    bsb = jnp.broadcast_to(b_s[:, :, :, None], (DEPTH, 4, BLK, BLK))
    bias = _band_bias()

    saved = []
    for l in range(DEPTH):
        if l == 1:
            sent, landed = _split_wait(ag_send, ag_recv, ag_thru, _plan_all, xs[1], "allgather_weights1_wait")
            weights[1] = full_weights(with_own(landed, sent))
        w_t = weights[l]["w_t"] if l else gathered0[0].reshape(IN_COLS, D)
        last = l == DEPTH - 1
        b_l = b_in[l].reshape(1, -1) + (ag_token[0, 0] if l == 0 else 0.0)
        hm, hr = _inproj(xs[l], w_t, b_l, f"inproj{l}")
        ya, yb, prob, psink = _mixer_fwd(hm, sinks[l], bias, vn_g[l].reshape(1, -1), vn_b[l].reshape(1, -1),
                                         w_s[l], bsb[l], nblk_seq, f"mixer_fwd{l}")
        if l == 0:
            sent, landed = _split_wait(b_send, b_recv, b_thru, _plan_all, ya, "allgather_rest0_wait")
            weights[0] = full_weights(gathered0 + with_own(landed, sent))
        wl = weights[l]
        outs = _tail_fwd(xs[l], ya, yb, hr, wl["pa"], wl["pb"], wl["wo"], b_out[l].reshape(1, D),
                         ln_g[l].reshape(1, D), ln_b[l].reshape(1, D), f"tail_fwd{l}", last)
        saved.append((hm, hr, ya, yb, prob, psink) + tuple(outs[:4]))
        if not last:
            xs.append(outs[4])

    small = {n: [None] * DEPTH for n in _SMALL_ROWS + _SMALL_LANES}

    def pack_rows(d):
        return _pack_small(d, _SMALL_ROWS)

    def pack_lanes(d):
        return jnp.concatenate([d[n] for n in _SMALL_LANES], axis=1)
    names = ("w_in", "p_a", "p_b", "w_out")
    token = jnp.zeros((8, 128), F32)
    dx = tgt
    split = [None] * DEPTH
    for l in reversed(range(DEPTH)):
        hm, hr, ya, yb, prob, psink, pa, pb, merged, z = saved[l]
        wl = weights[l]
        dz, dpa, dpb, dhr, dya, dyb, acc, gbr = _tail_bwd(
            dx, z, pa, pb, hr, wl["wo"], wl["pa"], wl["pb"], ln_g[l].reshape(1, D) + token[0, 0],
            ln_b[l].reshape(1, D), f"tail_bwd{l}", l == DEPTH - 1)
        if l == DEPTH - 1:
            sq_err = acc[3:4, 0:128]
        dhm, gbm, gsk, gvn, gws, gbs = _mixer_bwd(
            hm, dya, dyb, prob, psink, vn_g[l].reshape(1, -1), vn_b[l].reshape(1, -1), w_s[l], bsb[l],
            f"mixer_bwd{l}")
        grads = {"w_in": _wgrad(dhr, xs[l], R_W // 2, f"wgrad_in_route{l}",
                                under=_wgrad(dhm, xs[l], MAIN_W // 2, f"wgrad_in_main{l}", rows=IN_COLS)),
                 "p_a": _wgrad(ya, dpa, Q_W, f"wgrad_pa{l}", by_owner=True),
                 "p_b": _wgrad(yb, dpb, SGU_W, f"wgrad_pb{l}", by_owner=True),
                 "w_out": _wgrad(merged, dz, D, f"wgrad_out{l}")}
        small["b_in"][l] = jnp.concatenate([gbm[0], gbr[0]])
        small["sinks"][l] = gsk[:, 0]
        small["vn_g"][l], small["vn_b"][l] = gvn[0], gvn[1]
        small["w_s"][l], small["b_s"][l] = gws, gbs[:, :, 0]
        small["ln_g"][l], small["ln_b"][l], small["b_out"][l] = acc[0], acc[1], acc[2]
        parts = [grads[n].reshape((4, 2, -1, grads[n].shape[-1])) for n in names]
        if l == 0:
            stacked = {n: jnp.stack(v) for n, v in small.items()}
            for packed in (pack_rows(stacked), pack_lanes(stacked)):
                parts.append(jnp.broadcast_to(packed[None, None], (1, 2) + packed.shape))
        if l == 0:
            from_sib = _swap_sibling(parts, f"rs_sibling{l}")
        else:
            halves = [p.reshape((N_DEV,) + p.shape[2:]) for p in parts]
            sib = _split_start(halves, [lax.empty((4,) + p.shape[2:], p.dtype) for p in parts], _plan_sibling, 4,
                               f"rs_sibling{l}_start")
            dx = _dx_inproj(dz, dhm, dhr, wl["w_t"], sib[3], f"dx_inproj{l}")
            halves, from_sib = _split_wait(sib[0], sib[1], sib[2], _plan_sibling, dx, f"rs_sibling{l}_wait")
            parts = [h.reshape(p.shape) for h, p in zip(halves, parts)]
        pair = list(_pair_sums(parts, from_sib, f"pair_sums{l}"))
        if l == 0:
            pair[4:] = [jnp.broadcast_to(p, (4,) + p.shape[1:]) for p in pair[4:]]
        lands = [jnp.zeros(p.shape, p.dtype) for p in pair]
        split[l] = _split_start(pair, lands, _plan_chips, 3, f"rs_chips{l}_start")
        token = split[l][3]
        if l == 0:
            grad_x, acc_in = _dx_inproj(dz, dhm, dhr, wl["w_t"], token, f"dx_inproj{l}",
                                        ln_in=(x2, ln_in_g.reshape(1, D)))
    last = [acc_in, jnp.broadcast_to(sq_err, (8, 128))]
    ln_send, ln_recv, ln_thru, ln_token = _split_start(last, landing(last), _plan_all, 7, "allgather_ln_in_start")

    given = {"w_in": (w_in_t, turned(m_w_in), turned(v_w_in)), "p_a": (p_a, m_p_a, v_p_a),
             "p_b": (p_b, m_p_b, v_p_b), "w_out": (w_out, m_w_out, v_w_out)}
    waited = [_split_wait(split[l][0], split[l][1], split[l][2], _plan_chips, ln_token, f"rs_chips{l}_wait")
              for l in range(DEPTH)]
    res = {}
    for a, n in enumerate(names):
        rows, lanes = waited[0][1][a].shape[1:]
        outs = _adamw([waited[l][1][a] for l in range(DEPTH)], *[v.reshape(DEPTH * rows, lanes) for v in given[n]],
                      f"adamw_{n}", own=[waited[l][0][a] for l in range(DEPTH)])
        res[n] = [o.reshape(given[n][0].shape) for o in outs]
    res["w_in"] = [turned(o) for o in res["w_in"]]

    w_small = dict(ln_in_g=ln_in_g, ln_in_b=ln_in_b, b_in=b_in, sinks=sinks, vn_g=vn_g, vn_b=vn_b, w_s=w_s, b_s=b_s,
                   b_out=b_out, ln_g=ln_g, ln_b=ln_b)
    m_small = dict(ln_in_g=m_ln_in_g, ln_in_b=m_ln_in_b, b_in=m_b_in, sinks=m_sinks, vn_g=m_vn_g, vn_b=m_vn_b,
                   w_s=m_w_s, b_s=m_b_s, b_out=m_b_out, ln_g=m_ln_g, ln_b=m_ln_b)
    v_small = dict(ln_in_g=v_ln_in_g, ln_in_b=v_ln_in_b, b_in=v_b_in, sinks=v_sinks, vn_g=v_vn_g, vn_b=v_vn_b,
                   w_s=v_w_s, b_s=v_b_s, b_out=v_b_out, ln_g=v_ln_g, ln_b=v_ln_b)
    by_rows = _adamw_small(waited[0][1][4], waited[0][0][4], *[pack_rows(d) for d in (w_small, m_small, v_small)],
                           [_tile_rows(w_small[n]) for n in _SMALL_ROWS], 0, "adamw_small_rows")
    by_lanes = _adamw_small(waited[0][1][5], waited[0][0][5], *[pack_lanes(d) for d in (w_small, m_small, v_small)],
                            [w_small[n].shape[1] for n in _SMALL_LANES], 1, "adamw_small_lanes")
    sent, landed = _split_wait(ln_send, ln_recv, ln_thru, _plan_all, by_lanes[0][0], "allgather_ln_in_wait")
    all_in, all_sq = with_own(landed, sent)
    loss = jnp.sum(all_sq[:, 0, 0]) * (0.5 / D)
    outs_in = _adamw([all_in], *[jnp.pad(jnp.stack([d[n] for n in _SMALL_IN]), ((0, 6), (0, 0)))
                                 for d in (w_small, m_small, v_small)], "adamw_ln_in")
    for k in range(4):
        u = {n: o.reshape(-1)[:w_small[n].size].reshape(w_small[n].shape) for n, o in zip(_SMALL_ROWS, by_rows[k])}
        u.update(zip(_SMALL_LANES, by_lanes[k]))
        u.update({n: outs_in[k][r] for r, n in enumerate(_SMALL_IN)})
        for n in u:
            res.setdefault(n, [None] * 4)[k] = u[n]

    order = ("ln_in_g", "ln_in_b", "w_in", "b_in", "sinks", "vn_g", "vn_b", "w_s", "b_s", "p_a", "p_b", "w_out",
             "b_out", "ln_g", "ln_b")
    return (loss, grad_x.reshape(x.shape), *[res[n][0] for n in order], *[res[n][1] for n in order],
            *[res[n][2] for n in order], *[res[n][3] for n in order])
```

```python
import jax
import jax.numpy as jnp
from jax import lax
from jax.experimental import pallas as pl
from jax.experimental.pallas import tpu as pltpu

F32 = jnp.float32
BF16 = jnp.bfloat16

D = 1024
BLK = 128
N_KV = 2
Q_W, KV_W, SGU_W = 512, 128, 512
C_Q, C_K, C_V, C_GA, C_UB, C_VB, C_GB = 0, 512, 640, 768, 1280, 1792, 2304
MAIN_W = 2816
R_W = 2048
IN_COLS = MAIN_W + R_W
N_DEV = 8

DEPTH = 2
ALPHA = (2.0 * DEPTH) ** 0.25
LN_EPS = 1e-5
ATTN_SCALE = 0.125
NEG = float(jnp.finfo(jnp.float32).min)

ADAM_LR, ADAM_B1, ADAM_B2, ADAM_EPS, ADAM_WD, ADAM_STEP = 0.001, 0.9, 0.999, 1e-08, 0.01, 10

TM = 512
TM_EW = 1024
TM_MM = 512
TM_T = 512
NB = TM // BLK
MESH = pl.DeviceIdType.MESH
VMEM_LIMIT = 56 * 1024 * 1024

_ARB = pltpu.CompilerParams(dimension_semantics=("arbitrary",), vmem_limit_bytes=VMEM_LIMIT)


def _sigmoid(x):
    return 0.5 + 0.5 * jnp.tanh(0.5 * x)


_GELU_C = 0.7978845608028654
_GELU_A = 0.044715


def _gelu_parts(x):
    x2 = x * x
    t = jnp.tanh(x * (_GELU_C + (_GELU_C * _GELU_A) * x2))
    hx = 0.5 * x
    return hx, t, x2


def _gelu(x):
    hx, t, _ = _gelu_parts(x)
    return hx + hx * t


def _gelu_and_grad(x):
    hx, t, x2 = _gelu_parts(x)
    grad = 0.5 + 0.5 * t + (hx - hx * (t * t)) * (_GELU_C + (3.0 * _GELU_C * _GELU_A) * x2)
    return hx + hx * t, grad


def _ln_stats(x):
    mu = jnp.mean(x, axis=-1, keepdims=True)
    xc = x - mu
    var = jnp.mean(xc * xc, axis=-1, keepdims=True)
    rstd = lax.rsqrt(var + LN_EPS)
    return xc * rstd, rstd


def _ln_bwd(dy_g, xhat, rstd):
    m1 = jnp.mean(dy_g, axis=-1, keepdims=True)
    m2 = jnp.mean(dy_g * xhat, axis=-1, keepdims=True)
    return rstd * (dy_g - m1 - xhat * m2)


def _colsum(x):
    return jnp.sum(x, axis=0, keepdims=True)


def _dot(a, b):
    return jnp.dot(a, b, preferred_element_type=F32)


def _dot_nt(a, b):
    return lax.dot_general(a, b, (((1,), (1,)), ((), ())), preferred_element_type=F32)


def _side_by_side(gathered_ref):
    return jnp.concatenate([gathered_ref[j] for j in range(N_DEV)], axis=1)


def _dot_tn(a, b):
    return lax.dot_general(a, b, (((0,), (0,)), ((), ())), preferred_element_type=F32)


def _head_place(hk, g):
    j = 4 * hk + g
    return j, j // 2, j % 2


def _head_rows(x, hk):
    d = lax.broadcasted_iota(jnp.int32, x.shape, 0)
    return jnp.where((d >= 64 * hk) & (d < 64 * hk + 64), x, 0.0).astype(BF16)


def _head_lanes(x, hk):
    d = lax.broadcasted_iota(jnp.int32, x.shape, 1)
    return jnp.where((d >= 64 * hk) & (d < 64 * hk + 64), x, 0.0)


def _band_bias():
    kpos = lax.broadcasted_iota(jnp.int32, (2 * BLK, 4 * BLK), 0)
    row = lax.broadcasted_iota(jnp.int32, (2 * BLK, 4 * BLK), 1) & (BLK - 1)
    band = (kpos > row) & (kpos <= row + BLK)
    return jnp.stack([jnp.where(band, 0.0, NEG), jnp.where(band & (kpos >= BLK), 0.0, NEG)]).astype(F32)


def _stack_q(q, hk):
    parts = []
    for g in range(4):
        _, p, pos = _head_place(hk, g)
        qp = q[:, BLK * p:BLK * (p + 1)] * ATTN_SCALE
        if pos != hk:
            qp = pltpu.roll(qp, 64, 1)
        parts.append(qp.astype(BF16))
    return jnp.concatenate(parts, axis=0)


def _attn_probs(q4, kh, hk, sinks_ref, bias):
    s_t = _dot_nt(kh, q4) + bias
    sink_row = jnp.concatenate(
        [jnp.full((1, BLK), sinks_ref[4 * hk + g], F32) for g in range(4)], axis=1)
    m = jnp.maximum(jnp.max(s_t, axis=0, keepdims=True), sink_row)
    p_un = jnp.exp(s_t - m)
    e_sink = jnp.exp(sink_row - m)
    inv = 1.0 / (jnp.sum(p_un, axis=0, keepdims=True) + e_sink)
    return (p_un * inv).astype(BF16), e_sink * inv


def _unstack_heads(x4, hk, pairs):
    for g in range(4):
        _, p, pos = _head_place(hk, g)
        xg = x4[BLK * g:BLK * (g + 1)]
        if pos != hk:
            xg = pltpu.roll(xg, 64, 1)
        pairs[p] = xg if pairs[p] is None else pairs[p] + xg
    return pairs


def _attn_fwd(q, kband, vband, sinks_ref, bias, save):
    pairs = [None] * 4
    vband_t = vband.T
    for hk in range(N_KV):
        prob_t, p_sink = _attn_probs(_stack_q(q, hk), _head_lanes(kband, hk).astype(BF16), hk, sinks_ref, bias)
        save(hk, prob_t, p_sink)
        o_t = _dot(_head_rows(vband_t, hk), prob_t)
        pairs = _unstack_heads(o_t.T, hk, pairs)
    return jnp.concatenate(pairs, axis=1)


def _tril_mask():
    r = lax.broadcasted_iota(jnp.int32, (BLK, BLK), 0)
    c = lax.broadcasted_iota(jnp.int32, (BLK, BLK), 1)
    return c <= r


def _sgu_fwd(u, v, vn_g, vn_b, wt, bsb_ref):
    vhat, rstd = _ln_stats(v)
    vn = vhat * vn_g + vn_b
    mixed = jnp.concatenate(
        [_dot(wt[g], vn[:, BLK * g:BLK * (g + 1)].astype(BF16)) + bsb_ref[g] for g in range(4)], axis=1)
    return vhat, rstd, vn, mixed


def _cols(ref, rows, col, width):
    return ref[rows, col:col + width].astype(F32)


def _band(hm_ref, hprev_ref, s, col):
    r0 = s * BLK
    cur = hm_ref[r0:r0 + BLK, col:col + KV_W]
    if s == 0:
        off = 0 if col == C_K else KV_W
        prev = hprev_ref[:, off:off + KV_W]
    else:
        prev = hm_ref[r0 - BLK:r0, col:col + KV_W]
    return jnp.concatenate([prev, cur], axis=0).astype(F32)


def _h_main_specs(nt, rev):
    def tile(g):
        return nt - 1 - g if rev else g

    return [pl.BlockSpec((TM, MAIN_W), lambda g: (tile(g), 0)),
            pl.BlockSpec((BLK, 2 * KV_W), lambda g: (jnp.maximum(tile(g) * NB - 1, 0), 2))]


_CONST2 = lambda g: (0, 0)
_CONST3 = lambda g: (0, 0, 0)


def _ln_fwd(x, g, b, name):
    t = x.shape[0]

    def body(x_ref, g_ref, b_ref, o_ref):
        xhat, _ = _ln_stats(x_ref[...])
        o_ref[...] = xhat * g_ref[...] + b_ref[...]

    return pl.pallas_call(
        body, name=name, grid=(t // TM_EW,),
        in_specs=[pl.BlockSpec((TM_EW, D), lambda i: (i, 0)), pl.BlockSpec((1, D), _CONST2),
                  pl.BlockSpec((1, D), _CONST2)],
        out_specs=pl.BlockSpec((TM_EW, D), lambda i: (i, 0)),
        out_shape=jax.ShapeDtypeStruct((t, D), F32), compiler_params=_ARB,
    )(x, g.reshape(1, D), b.reshape(1, D))


def _inproj(x, w_t, b, name):
    t = x.shape[0]

    def body(x_ref, wt_ref, b_ref, hm_ref, hr_ref):
        xb = x_ref[...].astype(BF16)
        hm_ref[...] = (_dot_nt(xb, wt_ref[0:MAIN_W, :]) + b_ref[:, 0:MAIN_W]).astype(BF16)
        hr_ref[...] = (_dot_nt(xb, wt_ref[MAIN_W:IN_COLS, :]) + b_ref[:, MAIN_W:IN_COLS]).astype(BF16)

    return pl.pallas_call(
        body, name=name, grid=(t // TM_MM,),
        in_specs=[pl.BlockSpec((TM_MM, D), lambda i: (i, 0)),
                  pl.BlockSpec((IN_COLS, D), _CONST2), pl.BlockSpec((1, IN_COLS), _CONST2)],
        out_specs=[pl.BlockSpec((TM_MM, MAIN_W), lambda i: (i, 0)), pl.BlockSpec((TM_MM, R_W), lambda i: (i, 0))],
        out_shape=[jax.ShapeDtypeStruct((t, MAIN_W), BF16), jax.ShapeDtypeStruct((t, R_W), BF16)],
        compiler_params=_ARB,
    )(x, w_t, b)


def _mixer_fwd(hm, sinks, bias, vn_g, vn_b, w_s, bsb, nblk_seq, name):
    t = hm.shape[0]
    nt = t // TM

    def body(sinks_ref, hm_ref, hprev_ref, bias_ref, vng_ref, vnb_ref, ws_ref, bsb_ref,
             ya_ref, yb_ref, prob_ref, psink_ref):
        i = pl.program_id(0)
        tril = _tril_mask()
        wt = [jnp.where(tril, ws_ref[g], 0.0).astype(BF16) for g in range(4)]
        for s in range(NB):
            r0 = s * BLK
            rows = slice(r0, r0 + BLK)
            bias = bias_ref[jnp.where((i * NB + s) % nblk_seq == 0, 1, 0)]

            def save(hk, prob_t, p_sink, s=s):
                prob_ref[N_KV * s + hk] = prob_t
                psink_ref[N_KV * s + hk] = jnp.broadcast_to(p_sink, (8, 4 * BLK))

            attn = _attn_fwd(_cols(hm_ref, rows, C_Q, Q_W), _band(hm_ref, hprev_ref, s, C_K),
                             _band(hm_ref, hprev_ref, s, C_V), sinks_ref, bias, save)
            g_a = _cols(hm_ref, rows, C_GA, Q_W)
            ya_ref[rows, :] = (attn * (g_a * _sigmoid(g_a))).astype(BF16)
            u = _gelu(_cols(hm_ref, rows, C_UB, SGU_W))
            mixed = _sgu_fwd(u, _gelu(_cols(hm_ref, rows, C_VB, SGU_W)), vng_ref[...], vnb_ref[...], wt, bsb_ref)[-1]
            g_b = _cols(hm_ref, rows, C_GB, SGU_W)
            yb_ref[rows, :] = (u * mixed * (g_b * _sigmoid(g_b))).astype(BF16)

    ngrp = N_KV * NB
    return pl.pallas_call(
        body, name=name, grid=(nt,),
        in_specs=[pl.BlockSpec(memory_space=pltpu.SMEM)] + _h_main_specs(nt, False) + [
            pl.BlockSpec((2, 2 * BLK, 4 * BLK), _CONST3),
            pl.BlockSpec((1, SGU_W), _CONST2), pl.BlockSpec((1, SGU_W), _CONST2),
            pl.BlockSpec((4, BLK, BLK), _CONST3), pl.BlockSpec((4, BLK, BLK), _CONST3)],
        out_specs=[pl.BlockSpec((TM, Q_W), lambda i: (i, 0)), pl.BlockSpec((TM, SGU_W), lambda i: (i, 0)),
                   pl.BlockSpec((ngrp, 2 * BLK, 4 * BLK), lambda i: (i, 0, 0)),
                   pl.BlockSpec((ngrp, 8, 4 * BLK), lambda i: (i, 0, 0))],
        out_shape=[jax.ShapeDtypeStruct((t, Q_W), BF16), jax.ShapeDtypeStruct((t, SGU_W), BF16),
                   jax.ShapeDtypeStruct((nt * ngrp, 2 * BLK, 4 * BLK), BF16),
                   jax.ShapeDtypeStruct((nt * ngrp, 8, 4 * BLK), F32)],
        compiler_params=_ARB,
    )(sinks, hm, hm, bias, vn_g, vn_b, w_s, bsb)


def _tail_fwd(x, ya, yb, hr, pa_w, pb_w, wo, b_out, ln_g, ln_b, name, last):
    t = x.shape[0]

    def body(x_ref, ya_ref, yb_ref, hr_ref, paw_ref, pbw_ref, wo_ref, bo_ref, g_ref, b_ref,
             pa_ref, pb_ref, mg_ref, z_ref, *xn_ref):
        pa = _dot(ya_ref[...], _side_by_side(paw_ref))
        pb = _dot(yb_ref[...], _side_by_side(pbw_ref))
        pa_ref[...] = pa.astype(BF16)
        pb_ref[...] = pb.astype(BF16)
        everything = slice(None)
        merged = _sigmoid(_cols(hr_ref, everything, 0, D)) * pa + _sigmoid(_cols(hr_ref, everything, D, D)) * pb
        mb = merged.astype(BF16)
        mg_ref[...] = mb
        z = ALPHA * x_ref[...] + (_dot(mb, wo_ref[...]) + bo_ref[...])
        z_ref[...] = z
        if not last:
            zhat, _ = _ln_stats(z)
            xn_ref[0][...] = zhat * g_ref[...] + b_ref[...]

    row = lambda w: pl.BlockSpec((TM_T, w), lambda i: (i, 0))
    vec = pl.BlockSpec((1, D), _CONST2)
    n_f32 = 1 if last else 2
    return pl.pallas_call(
        body, name=name, grid=(t // TM_T,),
        in_specs=[row(D), row(Q_W), row(SGU_W), row(R_W),
                  pl.BlockSpec((N_DEV, Q_W, 128), _CONST3), pl.BlockSpec((N_DEV, SGU_W, 128), _CONST3),
                  pl.BlockSpec((D, D), _CONST2), vec, vec, vec],
        out_specs=[row(D)] * (3 + n_f32),
        out_shape=[jax.ShapeDtypeStruct((t, D), BF16)] * 3 + [jax.ShapeDtypeStruct((t, D), F32)] * n_f32,
        compiler_params=_ARB,
    )(x, ya, yb, hr, pa_w, pb_w, wo, b_out, ln_g, ln_b)


def _tail_bwd(dxn, z, pa, pb, hr, wo, pa_w, pb_w, ln_g, ln_b, name, from_loss):
    t = dxn.shape[0]

    def body(dxn_ref, z_ref, pa_ref, pb_ref, hr_ref, wo_ref, paw_ref, pbw_ref, g_ref, b_ref,
             dz_ref, dpa_ref, dpb_ref, dhr_ref, dya_ref, dyb_ref, acc_ref, gbr_ref):
        @pl.when(pl.program_id(0) == 0)
        def _():
            acc_ref[...] = jnp.zeros_like(acc_ref)
            gbr_ref[...] = jnp.zeros_like(gbr_ref)

        zhat, rstd = _ln_stats(z_ref[...])
        if from_loss:
            err = zhat * g_ref[...] + b_ref[...] - dxn_ref[...]
            dxn_v = err * (1.0 / D)
            sq = jnp.sum(jnp.sum(err * err, axis=1, keepdims=True), axis=0, keepdims=True)
            acc_ref[3:4, :] += jnp.broadcast_to(sq, (1, D))
        else:
            dxn_v = dxn_ref[...]
        dz = _ln_bwd(dxn_v * g_ref[...], zhat, rstd)
        dz_ref[...] = dz
        acc_ref[0:1, :] += _colsum(dxn_v * zhat)
        acc_ref[1:2, :] += _colsum(dxn_v)
        acc_ref[2:3, :] += _colsum(dz)
        dmerged = _dot_nt(dz.astype(BF16), wo_ref[...])
        everything = slice(None)
        sa = _sigmoid(_cols(hr_ref, everything, 0, D))
        sb = _sigmoid(_cols(hr_ref, everything, D, D))
        dpa = (dmerged * sa).astype(BF16)
        dpb = (dmerged * sb).astype(BF16)
        dpa_ref[...] = dpa
        dpb_ref[...] = dpb
        dra = dmerged * pa_ref[...].astype(F32) * (sa * (1.0 - sa))
        drb = dmerged * pb_ref[...].astype(F32) * (sb * (1.0 - sb))
        dhr_ref[:, 0:D] = dra.astype(BF16)
        dhr_ref[:, D:2 * D] = drb.astype(BF16)
        gbr_ref[0:1, 0:D] += _colsum(dra)
        gbr_ref[0:1, D:2 * D] += _colsum(drb)
        dya_ref[...] = _dot_nt(dpa, _side_by_side(paw_ref)).astype(BF16)
        dyb_ref[...] = _dot_nt(dpb, _side_by_side(pbw_ref)).astype(BF16)

    row = lambda w: pl.BlockSpec((TM_T, w), lambda i: (i, 0))
    vec = pl.BlockSpec((1, D), _CONST2)
    return pl.pallas_call(
        body, name=name, grid=(t // TM_T,),
        in_specs=[row(D), row(D), row(D), row(D), row(R_W),
                  pl.BlockSpec((D, D), _CONST2), pl.BlockSpec((N_DEV, Q_W, 128), _CONST3),
                  pl.BlockSpec((N_DEV, SGU_W, 128), _CONST3), vec, vec],
        out_specs=[row(D), row(D), row(D), row(R_W), row(Q_W), row(SGU_W), pl.BlockSpec((8, D), _CONST2),
                   pl.BlockSpec((8, R_W), _CONST2)],
        out_shape=[jax.ShapeDtypeStruct((t, D), F32), jax.ShapeDtypeStruct((t, D), BF16),
                   jax.ShapeDtypeStruct((t, D), BF16), jax.ShapeDtypeStruct((t, R_W), BF16),
                   jax.ShapeDtypeStruct((t, Q_W), BF16), jax.ShapeDtypeStruct((t, SGU_W), BF16),
                   jax.ShapeDtypeStruct((8, D), F32), jax.ShapeDtypeStruct((8, R_W), F32)],
        compiler_params=_ARB,
    )(dxn, z, pa, pb, hr, wo, pa_w, pb_w, ln_g, ln_b)


def _mixer_bwd(hm, dya, dyb, prob, psink, vn_g, vn_b, w_s, bsb, name):
    t = hm.shape[0]
    nt = t // TM
    ngrp = N_KV * NB

    def body(hm_ref, hprev_ref, prob_ref, psink_ref, dya_ref, dyb_ref, vng_ref, vnb_ref, ws_ref, bsb_ref,
             dhm_ref, gbm_ref, gsk_ref, gvn_ref, gws_ref, gbs_ref, dk_carry, dv_carry):
        gi = pl.program_id(0)

        @pl.when(gi == 0)
        def _():
            for r in (gbm_ref, gsk_ref, gvn_ref, gws_ref, gbs_ref, dk_carry, dv_carry):
                r[...] = jnp.zeros_like(r)

        tril = _tril_mask()
        wt = [jnp.where(tril, ws_ref[g], 0.0).astype(BF16) for g in range(4)]
        vng = vng_ref[...]
        ones8 = jnp.ones((8, BLK), BF16)

        def put(rows, col, val):
            dhm_ref[rows, col:col + val.shape[1]] = val.astype(BF16)

        for s in reversed(range(NB)):
            r0 = s * BLK
            rows = slice(r0, r0 + BLK)
            q = _cols(hm_ref, rows, C_Q, Q_W)
            kband = _band(hm_ref, hprev_ref, s, C_K)
            vband = _band(hm_ref, hprev_ref, s, C_V)
            g_a = _cols(hm_ref, rows, C_GA, Q_W)
            sg = _sigmoid(g_a)
            dya_v = _cols(dya_ref, rows, 0, Q_W)
            d_o = dya_v * (g_a * sg)
            o_pairs, dq_pairs = [None] * 4, [None] * 4
            dkband = jnp.zeros((2 * BLK, KV_W), F32)
            dvband = jnp.zeros((2 * BLK, KV_W), F32)
            kband_t, vband_t = kband.T, vband.T
            for hk in range(N_KV):
                q4 = _stack_q(q, hk)
                prob_b = prob_ref[N_KV * s + hk]
                p_sink = psink_ref[N_KV * s + hk][0:1, :]
                o_t = _dot(_head_rows(vband_t, hk), prob_b)
                o_pairs = _unstack_heads(o_t.T, hk, o_pairs)
                parts = []
                for g in range(4):
                    _, p, pos = _head_place(hk, g)
                    dp = d_o[:, BLK * p:BLK * (p + 1)]
                    parts.append(pltpu.roll(dp, 64, 1) if pos != hk else dp)
                do4 = _head_lanes(jnp.concatenate(parts, axis=0), hk)
                do4b = do4.astype(BF16)
                delta = _colsum(do4.T * o_t)
                vh = _head_lanes(vband, hk).astype(BF16)
                ds_t = prob_b.astype(F32) * (_dot_nt(vh, do4b) - delta)
                dsb = ds_t.astype(BF16)
                dq4_t = _dot(_head_rows(kband_t, hk), dsb)
                dq_pairs = _unstack_heads(dq4_t.T * ATTN_SCALE, hk, dq_pairs)
                dkband = dkband + _head_lanes(_dot(dsb, q4), hk)
                dvband = dvband + _dot(prob_b, do4b)
                dsk = p_sink * delta
                for g in range(4):
                    j = 4 * hk + g
                    tot = jnp.sum(dsk[:, BLK * g:BLK * (g + 1)], axis=1, keepdims=True)
                    gsk_ref[j:j + 1, :] += jnp.broadcast_to(-tot, (1, 128))
            attn = jnp.concatenate(o_pairs, axis=1)
            put(rows, C_Q, jnp.concatenate(dq_pairs, axis=1))
            put(rows, C_K, dkband[BLK:2 * BLK] + dk_carry[...])
            put(rows, C_V, dvband[BLK:2 * BLK] + dv_carry[...])
            dk_carry[...] = dkband[0:BLK]
            dv_carry[...] = dvband[0:BLK]
            put(rows, C_GA, dya_v * attn * (sg * (1.0 + g_a * (1.0 - sg))))
            u, du_du_b = _gelu_and_grad(_cols(hm_ref, rows, C_UB, SGU_W))
            v, dv_dv_b = _gelu_and_grad(_cols(hm_ref, rows, C_VB, SGU_W))
            g_b = _cols(hm_ref, rows, C_GB, SGU_W)
            vhat, rstd, vn, mixed = _sgu_fwd(u, v, vng, vnb_ref[...], wt, bsb_ref)
            sgb = _sigmoid(g_b)
            silu_b = g_b * sgb
            dyb_v = _cols(dyb_ref, rows, 0, SGU_W)
            du = dyb_v * mixed * silu_b
            dmixed = dyb_v * u * silu_b
            put(rows, C_GB, dyb_v * u * mixed * (sgb * (1.0 + g_b * (1.0 - sgb))))
            dvn_parts = []
            for g in range(4):
                cols = slice(BLK * g, BLK * (g + 1))
                dmg = dmixed[:, cols]
                dmgb = dmg.astype(BF16)
                dvn_parts.append(_dot_tn(wt[g], dmgb))
                gws_ref[g] += jnp.where(tril, _dot_nt(dmgb, vn[:, cols].astype(BF16)), 0.0)
                gbs_ref[g] += dmg
            dvn = jnp.concatenate(dvn_parts, axis=1)
            gvn_ref[0:1, :] += _colsum(dvn * vhat)
            gvn_ref[1:2, :] += _colsum(dvn)
            dv = _ln_bwd(dvn * vng, vhat, rstd)
            put(rows, C_UB, du * du_du_b)
            put(rows, C_VB, dv * dv_dv_b)
            gbm_ref[...] += _dot(ones8, dhm_ref[rows, :])

        @pl.when(gi == nt - 1)
        def _():
            for g in range(4):
                gbs_ref[g] = jnp.broadcast_to(jnp.sum(gbs_ref[g], axis=1, keepdims=True), (BLK, BLK))

    row = lambda w: pl.BlockSpec((TM, w), lambda g: (nt - 1 - g, 0))
    return pl.pallas_call(
        body, name=name, grid=(nt,),
        in_specs=_h_main_specs(nt, True) + [
            pl.BlockSpec((ngrp, 2 * BLK, 4 * BLK), lambda g: (nt - 1 - g, 0, 0)),
            pl.BlockSpec((ngrp, 8, 4 * BLK), lambda g: (nt - 1 - g, 0, 0)),
            row(Q_W), row(SGU_W),
            pl.BlockSpec((1, SGU_W), _CONST2), pl.BlockSpec((1, SGU_W), _CONST2),
            pl.BlockSpec((4, BLK, BLK), _CONST3), pl.BlockSpec((4, BLK, BLK), _CONST3)],
        out_specs=[row(MAIN_W), pl.BlockSpec((8, MAIN_W), _CONST2), pl.BlockSpec((8, 128), _CONST2),
                   pl.BlockSpec((8, SGU_W), _CONST2), pl.BlockSpec((4, BLK, BLK), _CONST3),
                   pl.BlockSpec((4, BLK, BLK), _CONST3)],
        out_shape=[jax.ShapeDtypeStruct((t, MAIN_W), BF16), jax.ShapeDtypeStruct((8, MAIN_W), F32),
                   jax.ShapeDtypeStruct((8, 128), F32), jax.ShapeDtypeStruct((8, SGU_W), F32),
                   jax.ShapeDtypeStruct((4, BLK, BLK), F32), jax.ShapeDtypeStruct((4, BLK, BLK), F32)],
        scratch_shapes=[pltpu.VMEM((BLK, KV_W), F32), pltpu.VMEM((BLK, KV_W), F32)],
        compiler_params=_ARB,
    )(hm, hm, prob, psink, dya, dyb, vn_g, vn_b, w_s, bsb)


def _dx_inproj(dz, dhm, dhr, w_t, after, name, ln_in=None):
    t = dz.shape[0]

    def body(dz_ref, dhm_ref, dhr_ref, wt_ref, after_ref, *rest):
        dx = (ALPHA * dz_ref[...] + after_ref[0:1, 0:1] + _dot(dhm_ref[...], wt_ref[0:MAIN_W, :])
              + _dot(dhr_ref[...], wt_ref[MAIN_W:IN_COLS, :]))
        if ln_in is None:
            rest[0][...] = dx
            return
        x_ref, g_ref, gx_ref, acc_ref = rest

        @pl.when(pl.program_id(0) == 0)
        def _():
            acc_ref[...] = jnp.zeros_like(acc_ref)

        xhat, rstd = _ln_stats(x_ref[...])
        gx_ref[...] = _ln_bwd(dx * g_ref[...], xhat, rstd)
        acc_ref[0:1, :] += _colsum(dx * xhat)
        acc_ref[1:2, :] += _colsum(dx)

    row = lambda w: pl.BlockSpec((TM_MM, w), lambda i: (i, 0))
    in_specs = [row(D), row(MAIN_W), row(R_W), pl.BlockSpec((IN_COLS, D), _CONST2), pl.BlockSpec((8, 128), _CONST2)]
    if ln_in is None:
        return pl.pallas_call(
            body, name=name, grid=(t // TM_MM,), in_specs=in_specs,
            out_specs=row(D), out_shape=jax.ShapeDtypeStruct((t, D), F32), compiler_params=_ARB,
        )(dz, dhm, dhr, w_t, after)
    return pl.pallas_call(
        body, name=name, grid=(t // TM_MM,), in_specs=in_specs + [row(D), pl.BlockSpec((1, D), _CONST2)],
        out_specs=[row(D), pl.BlockSpec((8, D), _CONST2)],
        out_shape=[jax.ShapeDtypeStruct((t, D), F32), jax.ShapeDtypeStruct((8, D), F32)], compiler_params=_ARB,
    )(dz, dhm, dhr, w_t, after, *ln_in)


def _wgrad(a, b, tm, name, rows=None, under=None, by_owner=False):
    t, m = a.shape
    n = b.shape[1]
    tk = min(t, 2048)
    nk = t // tk

    def body(a_ref, b_ref, *rest):
        o_ref, acc_ref = rest[-2:]
        k = pl.program_id(1)

        @pl.when(k == 0)
        def _():
            acc_ref[...] = jnp.zeros_like(acc_ref)

        acc_ref[...] += _dot_tn(a_ref[...].astype(BF16), b_ref[...].astype(BF16))

        @pl.when(k == nk - 1)
        def _():
            if by_owner:
                for j in range(N_DEV):
                    o_ref[j] = acc_ref[:, j * (n // N_DEV):(j + 1) * (n // N_DEV)].astype(BF16)
            else:
                o_ref[...] = acc_ref[...].astype(BF16)

    in_specs = [pl.BlockSpec((tk, tm), lambda j, k: (k, j)), pl.BlockSpec((tk, n), lambda j, k: (k, 0))]
    if by_owner:
        return pl.pallas_call(
            body, name=name, grid=(m // tm, nk), in_specs=in_specs,
            out_specs=pl.BlockSpec((N_DEV, tm, n // N_DEV), lambda j, k: (0, j, 0)),
            out_shape=jax.ShapeDtypeStruct((N_DEV, m, n // N_DEV), BF16), scratch_shapes=[pltpu.VMEM((tm, n), F32)],
            compiler_params=pltpu.CompilerParams(dimension_semantics=("arbitrary", "arbitrary"),
                                                 vmem_limit_bytes=VMEM_LIMIT),
        )(a, b)
    if under is None:
        out_rows, out_spec, operands, aliases = rows or m, pl.BlockSpec((tm, n), lambda j, k: (j, 0)), (a, b), {}
    else:
        out_rows = under.shape[0]
        first = out_rows - m
        assert first % 128 == 0 and tm % 128 == 0
        out_spec = pl.BlockSpec((pl.Element(tm), pl.Element(n)),
                                lambda j, k: (pl.multiple_of(first + j * tm, 128), 0))
        in_specs, operands, aliases = in_specs + [_ANY], (a, b, under), {2: 0}
    return pl.pallas_call(
        body, name=name, grid=(m // tm, nk), in_specs=in_specs, out_specs=out_spec,
        out_shape=jax.ShapeDtypeStruct((out_rows, n), BF16), input_output_aliases=aliases,
        scratch_shapes=[pltpu.VMEM((tm, n), F32)],
        compiler_params=pltpu.CompilerParams(dimension_semantics=("arbitrary", "arbitrary"), vmem_limit_bytes=VMEM_LIMIT),
    )(*operands)


_ANY = pl.BlockSpec(memory_space=pl.ANY)


def _place():
    return lax.axis_index("x"), lax.axis_index("y"), lax.axis_index("c")


def _forward_sibling(lands, name):
    n = len(lands)

    def body(*refs):
        l_refs = refs[n:2 * n]
        send_sems, recv_sems = refs[2 * n:]
        x, y, c = _place()
        chips = [(1 - x, y), (x, 1 - y), (1 - x, 1 - y)]

        def copy(a, j, core):
            rows = l_refs[a].at[4 * chips[j][0] + 2 * chips[j][1] + core]
            return pltpu.make_async_remote_copy(
                src_ref=rows, dst_ref=rows, send_sem=send_sems.at[3 * a + j], recv_sem=recv_sems.at[3 * a + j],
                device_id=(x, y, 1 - c), device_id_type=MESH)

        for a in range(n):
            for j in range(3):
                copy(a, j, c).start()
        for a in range(n):
            for j in range(3):
                copy(a, j, 1 - c).wait_recv()
                copy(a, j, c).wait_send()

    return pl.pallas_call(
        body, name=name, in_specs=[_ANY] * n, out_specs=[_ANY] * n,
        out_shape=[jax.ShapeDtypeStruct(v.shape, v.dtype) for v in lands],
        input_output_aliases={a: a for a in range(n)},
        scratch_shapes=[pltpu.SemaphoreType.DMA((3 * n,)), pltpu.SemaphoreType.DMA((3 * n,))],
    )(*lands)


def _swap_sibling(gs, name):
    n = len(gs)
    first = [0]
    for v in gs:
        first.append(first[-1] + v.shape[0])

    def body(*refs):
        g_refs, r_refs = refs[:n], refs[n:2 * n]
        send_sems, recv_sems = refs[2 * n:]
        x, y, c = _place()
        cps = [pltpu.make_async_remote_copy(
            src_ref=g_refs[a].at[q, 1 - c], dst_ref=r_refs[a].at[q], send_sem=send_sems.at[first[a] + q],
            recv_sem=recv_sems.at[first[a] + q], device_id=(x, y, 1 - c), device_id_type=MESH)
            for a in range(n) for q in range(gs[a].shape[0])]
        for cp in cps:
            cp.start()
        for cp in cps:
            cp.wait()

    return pl.pallas_call(
        body, name=name, in_specs=[_ANY] * n, out_specs=[_ANY] * n,
        out_shape=[jax.ShapeDtypeStruct(v.shape[:1] + v.shape[2:], v.dtype) for v in gs],
        scratch_shapes=[pltpu.SemaphoreType.DMA((first[-1],)), pltpu.SemaphoreType.DMA((first[-1],))],
    )(*gs)


def _row_tile(rows, lanes, cap):
    if rows * lanes * 4 <= (1 << 20):
        return rows
    return max(d for d in range(8, cap + 1, 8) if rows % d == 0 and (d % 16 == 0 or rows % 16 != 0))


def _pair_sums(gs, rs, name):
    n = len(gs)

    def add(g, r, dtype):
        return (g.astype(F32) + r.astype(F32)).astype(dtype)

    def body(c_ref, *refs):
        g_refs, r_refs, o_refs = refs[:n], refs[n:2 * n], refs[2 * n:]
        o_refs[0][...] = add(g_refs[0][0], r_refs[0][...], o_refs[0].dtype)

        @pl.when(pl.program_id(0) == 0)
        def _():
            for a in range(1, n):
                o_refs[a][...] = add(g_refs[a][:, 0], r_refs[a][...], o_refs[a].dtype)

    def whole(shape, mine):
        if mine:
            return pl.BlockSpec(shape, lambda q, c_ref: (0, c_ref[0]) + (0,) * (len(shape) - 2))
        return pl.BlockSpec(shape, lambda q, c_ref: (0,) * len(shape))

    big = gs[0].shape
    return pl.pallas_call(
        body, name=name,
        grid_spec=pltpu.PrefetchScalarGridSpec(
            num_scalar_prefetch=1, grid=(big[0],),
            in_specs=[pl.BlockSpec((1, 1) + big[2:], lambda q, c_ref: (q, c_ref[0], 0, 0))]
            + [whole(g.shape[:1] + (1,) + g.shape[2:], True) for g in gs[1:]]
            + [pl.BlockSpec((1,) + big[2:], lambda q, c_ref: (q, 0, 0))]
            + [whole(r.shape, False) for r in rs[1:]],
            out_specs=[pl.BlockSpec((1,) + big[2:], lambda q, c_ref: (q, 0, 0))]
            + [whole(r.shape, False) for r in rs[1:]]),
        out_shape=[jax.ShapeDtypeStruct(r.shape, g.dtype) for g, r in zip(gs, rs)],
        compiler_params=_ARB,
    )(lax.axis_index("c").astype(jnp.int32).reshape(1), *gs, *rs)


def _adamw(parts, w, m, v, name, own=None):
    nl = len(parts)
    ns, rows, l = parts[0].shape
    tr = _row_tile(rows, l * ns, 304)
    nt = rows // tr
    c1 = 1.0 - ADAM_B1 ** ADAM_STEP
    c2 = 1.0 - ADAM_B2 ** ADAM_STEP

    def body(q_ref, *refs):
        own_refs = refs[:nl] if own is not None else None
        p_refs = refs[-7 - nl:-7]
        w_ref, m_ref, v_ref, g_ref, d_ref, nm_ref, nv_ref = refs[-7:]
        layer = pl.program_id(0)
        g = None
        for j in range(nl):
            gj = None
            for k in range(ns):
                term = p_refs[j][k].astype(F32)
                if own_refs is not None:
                    term = jnp.where(q_ref[0] == k, own_refs[j][0].astype(F32), term)
                gj = term if gj is None else gj + term
            g = gj if g is None else jnp.where(layer == j, gj, g)
        g_ref[...] = g
        nm = ADAM_B1 * m_ref[...] + (1.0 - ADAM_B1) * g
        nv = ADAM_B2 * v_ref[...] + (1.0 - ADAM_B2) * (g * g)
        nm_ref[...] = nm
        nv_ref[...] = nv
        d_ref[...] = -ADAM_LR * ((nm / c1) / (jnp.sqrt(nv / c2) + ADAM_EPS) + ADAM_WD * w_ref[...])

    def tile_of(j):
        return lambda la, i, q: jnp.where(la == j, i, jnp.where(la < j, 0, nt - 1))

    row = pl.BlockSpec((tr, l), lambda la, i, q: (la * nt + i, 0))
    own_specs = [] if own is None else [
        pl.BlockSpec((1, tr, l), lambda la, i, q, j=j: (q[0], tile_of(j)(la, i, q), 0)) for j in range(nl)]
    part_specs = [pl.BlockSpec((ns, tr, l), lambda la, i, q, j=j: (0, tile_of(j)(la, i, q), 0)) for j in range(nl)]
    chip = (2 * lax.axis_index("x") + lax.axis_index("y")).astype(jnp.int32).reshape(1)
    return pl.pallas_call(
        body, name=name,
        grid_spec=pltpu.PrefetchScalarGridSpec(
            num_scalar_prefetch=1, grid=(nl, nt),
            in_specs=own_specs + part_specs + [row, row, row], out_specs=[row] * 4),
        out_shape=[jax.ShapeDtypeStruct((nl * rows, l), F32)] * 4,
        compiler_params=pltpu.CompilerParams(dimension_semantics=("arbitrary", "arbitrary"), vmem_limit_bytes=VMEM_LIMIT),
    )(chip, *([] if own is None else own), *parts, w, m, v)


_HBM = pl.BlockSpec(memory_space=pltpu.HBM)
_SEM = pl.BlockSpec(memory_space=pltpu.SEMAPHORE)
_EFFECT = pltpu.SideEffectType.DATAFLOW_SIDE_EFFECTING


def _plan_all(x, y, c):
    me = 4 * x + 2 * y + c
    peers = [(x, y, 1 - c), (1 - x, y, c), (x, 1 - y, c), (1 - x, 1 - y, c),
             (1 - x, y, 1 - c), (x, 1 - y, 1 - c), (1 - x, 1 - y, 1 - c)]
    return [(None, me, p, 4 * p[0] + 2 * p[1] + p[2]) for p in peers]


def _plan_near(x, y, c):
    me = 4 * x + 2 * y + c
    peers = [(x, y, 1 - c), (1 - x, y, c), (x, 1 - y, c), (1 - x, 1 - y, c)]
    return [(None, me, p, 4 * p[0] + 2 * p[1] + p[2]) for p in peers]


def _plan_sibling(x, y, c):
    return [(2 * q + 1 - c, q, (x, y, 1 - c), q) for q in range(4)]


def _plan_chips(x, y, c):
    me = 2 * x + y
    return [(2 * qx + qy, me, (qx, qy, c), 2 * qx + qy) for qx, qy in ((1 - x, y), (x, 1 - y), (1 - x, 1 - y))]


def _split_copies(plan, src_refs, land_refs, send_sems, recv_sems, arrival):
    n = len(src_refs)
    entries = plan(*_place())
    per = len(entries)
    cps = []
    for a in range(n):
        for k, (src_slot, dst_slot, peer, back_slot) in enumerate(entries):
            src = src_refs[a] if src_slot is None else src_refs[a].at[src_slot]
            cps.append(pltpu.make_async_remote_copy(
                src_ref=src, dst_ref=land_refs[a].at[back_slot if arrival else dst_slot],
                send_sem=send_sems.at[per * a + k], recv_sem=recv_sems.at[per * a + k],
                device_id=peer, device_id_type=MESH))
    return cps


def _split_start(srcs, lands, plan, per, name):
    n = len(srcs)

    def body(*refs):
        for cp in _split_copies(plan, refs[:n], refs[n:2 * n], refs[2 * n], refs[2 * n + 1], False):
            cp.start()
        refs[-1][...] = jnp.zeros_like(refs[-1])

    both = list(srcs) + list(lands)
    outs = pl.pallas_call(
        body, name=name,
        out_shape=(pltpu.SemaphoreType.DMA((per * n,)), pltpu.SemaphoreType.DMA((per * n,)),
                   *[pltpu.HBM(v.shape, v.dtype) for v in both], jax.ShapeDtypeStruct((8, 128), F32)),
        in_specs=[_HBM] * (2 * n),
        out_specs=(_SEM, _SEM, *[_HBM] * (2 * n), pl.BlockSpec(memory_space=pltpu.VMEM)),
        input_output_aliases={i: 2 + i for i in range(2 * n)},
        compiler_params=pltpu.CompilerParams(has_side_effects=_EFFECT),
    )(*[pltpu.with_memory_space_constraint(v, pltpu.HBM) for v in both])
    return outs[0], outs[1], list(outs[2:2 + 2 * n]), outs[-1]


def _split_wait(send_sems, recv_sems, thru, plan, after, name):
    n = len(thru) // 2

    def body(*refs):
        for cp in _split_copies(plan, refs[:n], refs[n:2 * n], refs[2 * n], refs[2 * n + 1], True):
            cp.wait_send()
            cp.wait_recv()

    outs = pl.pallas_call(
        body, name=name, out_shape=tuple(pltpu.HBM(v.shape, v.dtype) for v in thru),
        in_specs=[_HBM] * (2 * n) + [_SEM, _SEM, pl.BlockSpec(memory_space=pl.ANY)],
        out_specs=[_HBM] * (2 * n), input_output_aliases={i: i for i in range(2 * n)},
        compiler_params=pltpu.CompilerParams(has_side_effects=_EFFECT),
    )(*thru, send_sems, recv_sems, after)
    return list(outs[:n]), list(outs[n:])


_SMALL_IN = ("ln_in_g", "ln_in_b")
_SMALL_ROWS = ("w_s", "b_s", "sinks")
_SMALL_LANES = ("b_in", "vn_g", "vn_b", "b_out", "ln_g", "ln_b")


def _tile_rows(a):
    return -(-a.size // 1024) * 8


def _pack_small(d, names):
    return jnp.concatenate([jnp.pad(d[n].reshape(-1), (0, (-d[n].size) % 1024)).reshape(-1, 128) for n in names])


def _adamw_small(parts, own, w, m, v, pieces, axis, name):
    c1 = 1.0 - ADAM_B1 ** ADAM_STEP
    c2 = 1.0 - ADAM_B2 ** ADAM_STEP
    shapes = [tuple(p if d == axis else s for d, s in enumerate(w.shape)) for p in pieces]

    def body(q_ref, own_ref, p_ref, w_ref, m_ref, v_ref, *o_refs):
        g = None
        for k in range(4):
            term = jnp.where(q_ref[0] == k, own_ref[0], p_ref[k])
            g = term if g is None else g + term
        nm = ADAM_B1 * m_ref[...] + (1.0 - ADAM_B1) * g
        nv = ADAM_B2 * v_ref[...] + (1.0 - ADAM_B2) * (g * g)
        delta = -ADAM_LR * ((nm / c1) / (jnp.sqrt(nv / c2) + ADAM_EPS) + ADAM_WD * w_ref[...])
        for k, val in enumerate((g, delta, nm, nv)):
            off = 0
            for j, p in enumerate(pieces):
                o_refs[k * len(pieces) + j][...] = val[off:off + p] if axis == 0 else val[:, off:off + p]
                off += p

    whole = pl.BlockSpec(w.shape, lambda i, q: (0, 0))
    chip = (2 * lax.axis_index("x") + lax.axis_index("y")).astype(jnp.int32).reshape(1)
    outs = pl.pallas_call(
        body, name=name,
        grid_spec=pltpu.PrefetchScalarGridSpec(
            num_scalar_prefetch=1, grid=(1,),
            in_specs=[pl.BlockSpec((1,) + w.shape, lambda i, q: (q[0], 0, 0)),
                      pl.BlockSpec((4,) + w.shape, lambda i, q: (0, 0, 0)), whole, whole, whole],
            out_specs=[pl.BlockSpec(s, lambda i, q: (0, 0)) for s in shapes] * 4),
        out_shape=[jax.ShapeDtypeStruct(s, F32) for s in shapes] * 4, compiler_params=_ARB,
    )(chip, own, parts, w, m, v)
    return [outs[k * len(pieces):(k + 1) * len(pieces)] for k in range(4)]


def kernel(x, ln_in_g, ln_in_b, w_in, b_in, sinks, vn_g, vn_b, w_s, b_s, p_a, p_b, w_out, b_out, ln_g, ln_b, loss_target, m_ln_in_g, m_ln_in_b, m_w_in, m_b_in, m_sinks, m_vn_g, m_vn_b, m_w_s, m_b_s, m_p_a, m_p_b, m_w_out, m_b_out, m_ln_g, m_ln_b, v_ln_in_g, v_ln_in_b, v_w_in, v_b_in, v_sinks, v_vn_g, v_vn_b, v_w_s, v_b_s, v_p_a, v_p_b, v_w_out, v_b_out, v_ln_g, v_ln_b):
    nseq, seq, _ = x.shape
    t = nseq * seq
    nblk_seq = seq // BLK
    x2 = x.reshape(t, D)
    tgt = loss_target.reshape(t, D)

    def turned(a):
        return jnp.swapaxes(a, 1, 2)

    w_in_t = turned(w_in)

    def blocks(l):
        return [w_in_t[l].astype(BF16), p_a[l].astype(BF16), p_b[l].astype(BF16), w_out[l].astype(BF16)]

    def full_weights(g):
        w_t_full = g[0].reshape(IN_COLS, D)
        wo_full = g[3].reshape(D, D)
        return dict(w_t=w_t_full, pa=g[1], pb=g[2], wo=wo_full)

    def landing(bs):
        return [lax.empty((N_DEV,) + v.shape, v.dtype) for v in bs]

    def with_own(landed, sent):
        return [lax.dynamic_update_index_in_dim(g, b, me, 0) for g, b in zip(landed, sent)]

    me = 4 * lax.axis_index("x") + 2 * lax.axis_index("y") + lax.axis_index("c")
    blocks0 = blocks(0)
    a_send, a_recv, a_thru, a_token = _split_start(blocks0[:1], landing(blocks0[:1]), _plan_near, 4,
                                                   "allgather_w_in0_start")
    rest0 = [b + a_token[0, 0].astype(BF16) for b in blocks0[1:]]
    b_send, b_recv, b_thru, b_token = _split_start(rest0, landing(rest0), _plan_all, 7, "allgather_rest0_start")
    xs = [_ln_fwd(x2, ln_in_g + b_token[0, 0], ln_in_b, "ln_in_fwd")]
    sent, landed = _split_wait(a_send, a_recv, a_thru, _plan_near, xs[0], "allgather_w_in0_wait")
    gathered0 = with_own(_forward_sibling(landed, "allgather_w_in0_forward"), sent)
    blocks1, gathered0 = lax.optimization_barrier((blocks(1), gathered0))
    ag_send, ag_recv, ag_thru, ag_token = _split_start(blocks1, landing(blocks1), _plan_all, 7,
                                                       "allgather_weights1_start")
    weights = [None, None]
    bsb = jnp.broadcast_to(b_s[:, :, :, None], (DEPTH, 4, BLK, BLK))
    bias = _band_bias()

    saved = []
    for l in range(DEPTH):
        if l == 1:
            sent, landed = _split_wait(ag_send, ag_recv, ag_thru, _plan_all, xs[1], "allgather_weights1_wait")
            weights[1] = full_weights(with_own(landed, sent))
        w_t = weights[l]["w_t"] if l else gathered0[0].reshape(IN_COLS, D)
        last = l == DEPTH - 1
        b_l = b_in[l].reshape(1, -1) + (ag_token[0, 0] if l == 0 else 0.0)
        hm, hr = _inproj(xs[l], w_t, b_l, f"inproj{l}")
        ya, yb, prob, psink = _mixer_fwd(hm, sinks[l], bias, vn_g[l].reshape(1, -1), vn_b[l].reshape(1, -1),
                                         w_s[l], bsb[l], nblk_seq, f"mixer_fwd{l}")
        if l == 0:
            sent, landed = _split_wait(b_send, b_recv, b_thru, _plan_all, ya, "allgather_rest0_wait")
            weights[0] = full_weights(gathered0 + with_own(landed, sent))
        wl = weights[l]
        outs = _tail_fwd(xs[l], ya, yb, hr, wl["pa"], wl["pb"], wl["wo"], b_out[l].reshape(1, D),
                         ln_g[l].reshape(1, D), ln_b[l].reshape(1, D), f"tail_fwd{l}", last)
        saved.append((hm, hr, ya, yb, prob, psink) + tuple(outs[:4]))
        if not last:
            xs.append(outs[4])

    small = {n: [None] * DEPTH for n in _SMALL_ROWS + _SMALL_LANES}

    def pack_rows(d):
        return _pack_small(d, _SMALL_ROWS)

    def pack_lanes(d):
        return jnp.concatenate([d[n] for n in _SMALL_LANES], axis=1)
    names = ("w_in", "p_a", "p_b", "w_out")
    token = jnp.zeros((8, 128), F32)
    dx = tgt
    split = [None] * DEPTH
    for l in reversed(range(DEPTH)):
        hm, hr, ya, yb, prob, psink, pa, pb, merged, z = saved[l]
        wl = weights[l]
        dz, dpa, dpb, dhr, dya, dyb, acc, gbr = _tail_bwd(
            dx, z, pa, pb, hr, wl["wo"], wl["pa"], wl["pb"], ln_g[l].reshape(1, D) + token[0, 0],
            ln_b[l].reshape(1, D), f"tail_bwd{l}", l == DEPTH - 1)
        if l == DEPTH - 1:
            sq_err = acc[3:4, 0:128]
        dhm, gbm, gsk, gvn, gws, gbs = _mixer_bwd(
            hm, dya, dyb, prob, psink, vn_g[l].reshape(1, -1), vn_b[l].reshape(1, -1), w_s[l], bsb[l],
            f"mixer_bwd{l}")
        grads = {"w_in": _wgrad(dhr, xs[l], R_W // 2, f"wgrad_in_route{l}",
                                under=_wgrad(dhm, xs[l], MAIN_W // 2, f"wgrad_in_main{l}", rows=IN_COLS)),
                 "p_a": _wgrad(ya, dpa, Q_W, f"wgrad_pa{l}", by_owner=True),
                 "p_b": _wgrad(yb, dpb, SGU_W, f"wgrad_pb{l}", by_owner=True),
                 "w_out": _wgrad(merged, dz, D, f"wgrad_out{l}")}
        small["b_in"][l] = jnp.concatenate([gbm[0], gbr[0]])
        small["sinks"][l] = gsk[:, 0]
        small["vn_g"][l], small["vn_b"][l] = gvn[0], gvn[1]
        small["w_s"][l], small["b_s"][l] = gws, gbs[:, :, 0]
        small["ln_g"][l], small["ln_b"][l], small["b_out"][l] = acc[0], acc[1], acc[2]
        parts = [grads[n].reshape((4, 2, -1, grads[n].shape[-1])) for n in names]
        if l == 0:
            stacked = {n: jnp.stack(v) for n, v in small.items()}
            for packed in (pack_rows(stacked), pack_lanes(stacked)):
                parts.append(jnp.broadcast_to(packed[None, None], (1, 2) + packed.shape))
        if l == 0:
            from_sib = _swap_sibling(parts, f"rs_sibling{l}")
        else:
            halves = [p.reshape((N_DEV,) + p.shape[2:]) for p in parts]
            sib = _split_start(halves, [lax.empty((4,) + p.shape[2:], p.dtype) for p in parts], _plan_sibling, 4,
                               f"rs_sibling{l}_start")
            dx = _dx_inproj(dz, dhm, dhr, wl["w_t"], sib[3], f"dx_inproj{l}")
            halves, from_sib = _split_wait(sib[0], sib[1], sib[2], _plan_sibling, dx, f"rs_sibling{l}_wait")
            parts = [h.reshape(p.shape) for h, p in zip(halves, parts)]
        pair = list(_pair_sums(parts, from_sib, f"pair_sums{l}"))
        if l == 0:
            pair[4:] = [jnp.broadcast_to(p, (4,) + p.shape[1:]) for p in pair[4:]]
        lands = [lax.empty(p.shape, p.dtype) for p in pair]
        split[l] = _split_start(pair, lands, _plan_chips, 3, f"rs_chips{l}_start")
        token = split[l][3]
        if l == 0:
            grad_x, acc_in = _dx_inproj(dz, dhm, dhr, wl["w_t"], token, f"dx_inproj{l}",
                                        ln_in=(x2, ln_in_g.reshape(1, D)))
    last = [acc_in, jnp.broadcast_to(sq_err, (8, 128))]
    ln_send, ln_recv, ln_thru, ln_token = _split_start(last, landing(last), _plan_all, 7, "allgather_ln_in_start")

    given = {"w_in": (w_in_t, turned(m_w_in), turned(v_w_in)), "p_a": (p_a, m_p_a, v_p_a),
             "p_b": (p_b, m_p_b, v_p_b), "w_out": (w_out, m_w_out, v_w_out)}
    waited = [_split_wait(split[l][0], split[l][1], split[l][2], _plan_chips, ln_token, f"rs_chips{l}_wait")
              for l in range(DEPTH)]
    res = {}
    for a, n in enumerate(names):
        rows, lanes = waited[0][1][a].shape[1:]
        outs = _adamw([waited[l][1][a] for l in range(DEPTH)], *[v.reshape(DEPTH * rows, lanes) for v in given[n]],
                      f"adamw_{n}", own=[waited[l][0][a] for l in range(DEPTH)])
        res[n] = [o.reshape(given[n][0].shape) for o in outs]
    res["w_in"] = [turned(o) for o in res["w_in"]]

    w_small = dict(ln_in_g=ln_in_g, ln_in_b=ln_in_b, b_in=b_in, sinks=sinks, vn_g=vn_g, vn_b=vn_b, w_s=w_s, b_s=b_s,
                   b_out=b_out, ln_g=ln_g, ln_b=ln_b)
    m_small = dict(ln_in_g=m_ln_in_g, ln_in_b=m_ln_in_b, b_in=m_b_in, sinks=m_sinks, vn_g=m_vn_g, vn_b=m_vn_b,
                   w_s=m_w_s, b_s=m_b_s, b_out=m_b_out, ln_g=m_ln_g, ln_b=m_ln_b)
    v_small = dict(ln_in_g=v_ln_in_g, ln_in_b=v_ln_in_b, b_in=v_b_in, sinks=v_sinks, vn_g=v_vn_g, vn_b=v_vn_b,
                   w_s=v_w_s, b_s=v_b_s, b_out=v_b_out, ln_g=v_ln_g, ln_b=v_ln_b)
    by_rows = _adamw_small(waited[0][1][4], waited[0][0][4], *[pack_rows(d) for d in (w_small, m_small, v_small)],
                           [_tile_rows(w_small[n]) for n in _SMALL_ROWS], 0, "adamw_small_rows")
    by_lanes = _adamw_small(waited[0][1][5], waited[0][0][5], *[pack_lanes(d) for d in (w_small, m_small, v_small)],
                            [w_small[n].shape[1] for n in _SMALL_LANES], 1, "adamw_small_lanes")
    sent, landed = _split_wait(ln_send, ln_recv, ln_thru, _plan_all, by_lanes[0][0], "allgather_ln_in_wait")
    all_in, all_sq = with_own(landed, sent)
    loss = jnp.sum(all_sq[:, 0, 0]) * (0.5 / D)
    outs_in = _adamw([all_in], *[jnp.pad(jnp.stack([d[n] for n in _SMALL_IN]), ((0, 6), (0, 0)))
                                 for d in (w_small, m_small, v_small)], "adamw_ln_in")
    for k in range(4):
        u = {n: o.reshape(-1)[:w_small[n].size].reshape(w_small[n].shape) for n, o in zip(_SMALL_ROWS, by_rows[k])}
        u.update(zip(_SMALL_LANES, by_lanes[k]))
        u.update({n: outs_in[k][r] for r, n in enumerate(_SMALL_IN)})
        for n in u:
            res.setdefault(n, [None] * 4)[k] = u[n]

    order = ("ln_in_g", "ln_in_b", "w_in", "b_in", "sinks", "vn_g", "vn_b", "w_s", "b_s", "p_a", "p_b", "w_out",
             "b_out", "ln_g", "ln_b")
    return (loss, grad_x.reshape(x.shape), *[res[n][0] for n in order], *[res[n][1] for n in order],
            *[res[n][2] for n in order], *[res[n][3] for n in order])
```

```python
import jax
import jax.numpy as jnp
from jax import lax
from jax.experimental import pallas as pl
from jax.experimental.pallas import tpu as pltpu

F32 = jnp.float32
BF16 = jnp.bfloat16

D = 1024
BLK = 128
N_KV = 2
Q_W, KV_W, SGU_W = 512, 128, 512
C_Q, C_K, C_V, C_GA, C_UB, C_VB, C_GB = 0, 512, 640, 768, 1280, 1792, 2304
MAIN_W = 2816
R_W = 2048
IN_COLS = MAIN_W + R_W
N_DEV = 8

DEPTH = 2
ALPHA = (2.0 * DEPTH) ** 0.25
LN_EPS = 1e-5
ATTN_SCALE = 0.125
NEG = float(jnp.finfo(jnp.float32).min)

ADAM_LR, ADAM_B1, ADAM_B2, ADAM_EPS, ADAM_WD, ADAM_STEP = 0.001, 0.9, 0.999, 1e-08, 0.01, 10

TM = 512
TM_EW = 1024
NB = TM // BLK
MESH = pl.DeviceIdType.MESH
VMEM_LIMIT = 56 * 1024 * 1024

_ARB = pltpu.CompilerParams(dimension_semantics=("arbitrary",), vmem_limit_bytes=VMEM_LIMIT)


def _sigmoid(x):
    return 0.5 + 0.5 * jnp.tanh(0.5 * x)


_GELU_C = 0.7978845608028654
_GELU_A = 0.044715


def _gelu_parts(x):
    x2 = x * x
    t = jnp.tanh(x * (_GELU_C + (_GELU_C * _GELU_A) * x2))
    hx = 0.5 * x
    return hx, t, x2


def _gelu(x):
    hx, t, _ = _gelu_parts(x)
    return hx + hx * t


def _gelu_and_grad(x):
    hx, t, x2 = _gelu_parts(x)
    grad = 0.5 + 0.5 * t + (hx - hx * (t * t)) * (_GELU_C + (3.0 * _GELU_C * _GELU_A) * x2)
    return hx + hx * t, grad


def _ln_stats(x):
    mu = jnp.mean(x, axis=-1, keepdims=True)
    xc = x - mu
    var = jnp.mean(xc * xc, axis=-1, keepdims=True)
    rstd = lax.rsqrt(var + LN_EPS)
    return xc * rstd, rstd


def _ln_bwd(dy_g, xhat, rstd):
    m1 = jnp.mean(dy_g, axis=-1, keepdims=True)
    m2 = jnp.mean(dy_g * xhat, axis=-1, keepdims=True)
    return rstd * (dy_g - m1 - xhat * m2)


def _colsum(x):
    return jnp.sum(x, axis=0, keepdims=True)


def _dot(a, b):
    return jnp.dot(a, b, preferred_element_type=F32)


def _dot_nt(a, b):
    return lax.dot_general(a, b, (((1,), (1,)), ((), ())), preferred_element_type=F32)


def _side_by_side(gathered_ref):
    return jnp.concatenate([gathered_ref[j] for j in range(N_DEV)], axis=1)


def _dot_tn(a, b):
    return lax.dot_general(a, b, (((0,), (0,)), ((), ())), preferred_element_type=F32)


def _head_place(hk, g):
    j = 4 * hk + g
    return j, j // 2, j % 2


def _head_rows(x, hk):
    d = lax.broadcasted_iota(jnp.int32, x.shape, 0)
    return jnp.where((d >= 64 * hk) & (d < 64 * hk + 64), x, 0.0).astype(BF16)


def _head_lanes(x, hk):
    d = lax.broadcasted_iota(jnp.int32, x.shape, 1)
    return jnp.where((d >= 64 * hk) & (d < 64 * hk + 64), x, 0.0)


def _band_bias():
    kpos = lax.broadcasted_iota(jnp.int32, (2 * BLK, 4 * BLK), 0)
    row = lax.broadcasted_iota(jnp.int32, (2 * BLK, 4 * BLK), 1) & (BLK - 1)
    band = (kpos > row) & (kpos <= row + BLK)
    return jnp.stack([jnp.where(band, 0.0, NEG), jnp.where(band & (kpos >= BLK), 0.0, NEG)]).astype(F32)


def _stack_q(q, hk):
    parts = []
    for g in range(4):
        _, p, pos = _head_place(hk, g)
        qp = q[:, BLK * p:BLK * (p + 1)] * ATTN_SCALE
        if pos != hk:
            qp = pltpu.roll(qp, 64, 1)
        parts.append(qp.astype(BF16))
    return jnp.concatenate(parts, axis=0)


def _attn_probs(q4, kh, hk, sinks_ref, bias):
    s_t = _dot_nt(kh, q4) + bias
    sink_row = jnp.concatenate(
        [jnp.full((1, BLK), sinks_ref[4 * hk + g], F32) for g in range(4)], axis=1)
    m = jnp.maximum(jnp.max(s_t, axis=0, keepdims=True), sink_row)
    p_un = jnp.exp(s_t - m)
    e_sink = jnp.exp(sink_row - m)
    inv = 1.0 / (jnp.sum(p_un, axis=0, keepdims=True) + e_sink)
    return (p_un * inv).astype(BF16), e_sink * inv


def _unstack_heads(x4, hk, pairs):
    for g in range(4):
        _, p, pos = _head_place(hk, g)
        xg = x4[BLK * g:BLK * (g + 1)]
        if pos != hk:
            xg = pltpu.roll(xg, 64, 1)
        pairs[p] = xg if pairs[p] is None else pairs[p] + xg
    return pairs


def _attn_fwd(q, kband, vband, sinks_ref, bias, save):
    pairs = [None] * 4
    vband_t = vband.T
    for hk in range(N_KV):
        prob_t, p_sink = _attn_probs(_stack_q(q, hk), _head_lanes(kband, hk).astype(BF16), hk, sinks_ref, bias)
        save(hk, prob_t, p_sink)
        o_t = _dot(_head_rows(vband_t, hk), prob_t)
        pairs = _unstack_heads(o_t.T, hk, pairs)
    return jnp.concatenate(pairs, axis=1)


def _tril_mask():
    r = lax.broadcasted_iota(jnp.int32, (BLK, BLK), 0)
    c = lax.broadcasted_iota(jnp.int32, (BLK, BLK), 1)
    return c <= r


def _sgu_fwd(u, v, vn_g, vn_b, wt, bsb_ref):
    vhat, rstd = _ln_stats(v)
    vn = vhat * vn_g + vn_b
    mixed = jnp.concatenate(
        [_dot(wt[g], vn[:, BLK * g:BLK * (g + 1)].astype(BF16)) + bsb_ref[g] for g in range(4)], axis=1)
    return vhat, rstd, vn, mixed


def _cols(ref, rows, col, width):
    return ref[rows, col:col + width].astype(F32)


def _band(hm_ref, hprev_ref, s, col):
    r0 = s * BLK
    cur = hm_ref[r0:r0 + BLK, col:col + KV_W]
    if s == 0:
        off = 0 if col == C_K else KV_W
        prev = hprev_ref[:, off:off + KV_W]
    else:
        prev = hm_ref[r0 - BLK:r0, col:col + KV_W]
    return jnp.concatenate([prev, cur], axis=0).astype(F32)


def _h_main_specs(nt, rev):
    def tile(g):
        return nt - 1 - g if rev else g

    return [pl.BlockSpec((TM, MAIN_W), lambda g: (tile(g), 0)),
            pl.BlockSpec((BLK, 2 * KV_W), lambda g: (jnp.maximum(tile(g) * NB - 1, 0), 2))]


_CONST2 = lambda g: (0, 0)
_CONST3 = lambda g: (0, 0, 0)


def _ln_fwd(x, g, b, name):
    t = x.shape[0]

    def body(x_ref, g_ref, b_ref, o_ref):
        xhat, _ = _ln_stats(x_ref[...])
        o_ref[...] = xhat * g_ref[...] + b_ref[...]

    return pl.pallas_call(
        body, name=name, grid=(t // TM_EW,),
        in_specs=[pl.BlockSpec((TM_EW, D), lambda i: (i, 0)), pl.BlockSpec((1, D), _CONST2),
                  pl.BlockSpec((1, D), _CONST2)],
        out_specs=pl.BlockSpec((TM_EW, D), lambda i: (i, 0)),
        out_shape=jax.ShapeDtypeStruct((t, D), F32), compiler_params=_ARB,
    )(x, g.reshape(1, D), b.reshape(1, D))


def _inproj(x, w_t, b, name):
    t = x.shape[0]

    def body(x_ref, wt_ref, b_ref, hm_ref, hr_ref):
        xb = x_ref[...].astype(BF16)
        hm_ref[...] = (_dot_nt(xb, wt_ref[0:MAIN_W, :]) + b_ref[:, 0:MAIN_W]).astype(BF16)
        hr_ref[...] = (_dot_nt(xb, wt_ref[MAIN_W:IN_COLS, :]) + b_ref[:, MAIN_W:IN_COLS]).astype(BF16)

    return pl.pallas_call(
        body, name=name, grid=(t // TM,),
        in_specs=[pl.BlockSpec((TM, D), lambda i: (i, 0)),
                  pl.BlockSpec((IN_COLS, D), _CONST2), pl.BlockSpec((1, IN_COLS), _CONST2)],
        out_specs=[pl.BlockSpec((TM, MAIN_W), lambda i: (i, 0)), pl.BlockSpec((TM, R_W), lambda i: (i, 0))],
        out_shape=[jax.ShapeDtypeStruct((t, MAIN_W), BF16), jax.ShapeDtypeStruct((t, R_W), BF16)],
        compiler_params=_ARB,
    )(x, w_t, b)


def _mixer_fwd(hm, sinks, bias, vn_g, vn_b, w_s, bsb, nblk_seq, name):
    t = hm.shape[0]
    nt = t // TM

    def body(sinks_ref, hm_ref, hprev_ref, bias_ref, vng_ref, vnb_ref, ws_ref, bsb_ref,
             ya_ref, yb_ref, prob_ref, psink_ref):
        i = pl.program_id(0)
        tril = _tril_mask()
        wt = [jnp.where(tril, ws_ref[g], 0.0).astype(BF16) for g in range(4)]
        for s in range(NB):
            r0 = s * BLK
            rows = slice(r0, r0 + BLK)
            bias = bias_ref[jnp.where((i * NB + s) % nblk_seq == 0, 1, 0)]

            def save(hk, prob_t, p_sink, s=s):
                prob_ref[N_KV * s + hk] = prob_t
                psink_ref[N_KV * s + hk] = jnp.broadcast_to(p_sink, (8, 4 * BLK))

            attn = _attn_fwd(_cols(hm_ref, rows, C_Q, Q_W), _band(hm_ref, hprev_ref, s, C_K),
                             _band(hm_ref, hprev_ref, s, C_V), sinks_ref, bias, save)
            g_a = _cols(hm_ref, rows, C_GA, Q_W)
            ya_ref[rows, :] = (attn * (g_a * _sigmoid(g_a))).astype(BF16)
            u = _gelu(_cols(hm_ref, rows, C_UB, SGU_W))
            mixed = _sgu_fwd(u, _gelu(_cols(hm_ref, rows, C_VB, SGU_W)), vng_ref[...], vnb_ref[...], wt, bsb_ref)[-1]
            g_b = _cols(hm_ref, rows, C_GB, SGU_W)
            yb_ref[rows, :] = (u * mixed * (g_b * _sigmoid(g_b))).astype(BF16)

    ngrp = N_KV * NB
    return pl.pallas_call(
        body, name=name, grid=(nt,),
        in_specs=[pl.BlockSpec(memory_space=pltpu.SMEM)] + _h_main_specs(nt, False) + [
            pl.BlockSpec((2, 2 * BLK, 4 * BLK), _CONST3),
            pl.BlockSpec((1, SGU_W), _CONST2), pl.BlockSpec((1, SGU_W), _CONST2),
            pl.BlockSpec((4, BLK, BLK), _CONST3), pl.BlockSpec((4, BLK, BLK), _CONST3)],
        out_specs=[pl.BlockSpec((TM, Q_W), lambda i: (i, 0)), pl.BlockSpec((TM, SGU_W), lambda i: (i, 0)),
                   pl.BlockSpec((ngrp, 2 * BLK, 4 * BLK), lambda i: (i, 0, 0)),
                   pl.BlockSpec((ngrp, 8, 4 * BLK), lambda i: (i, 0, 0))],
        out_shape=[jax.ShapeDtypeStruct((t, Q_W), BF16), jax.ShapeDtypeStruct((t, SGU_W), BF16),
                   jax.ShapeDtypeStruct((nt * ngrp, 2 * BLK, 4 * BLK), BF16),
                   jax.ShapeDtypeStruct((nt * ngrp, 8, 4 * BLK), F32)],
        compiler_params=_ARB,
    )(sinks, hm, hm, bias, vn_g, vn_b, w_s, bsb)


def _tail_fwd(x, ya, yb, hr, pa_w, pb_w, wo, b_out, ln_g, ln_b, name, last):
    t = x.shape[0]

    def body(x_ref, ya_ref, yb_ref, hr_ref, paw_ref, pbw_ref, wo_ref, bo_ref, g_ref, b_ref,
             pa_ref, pb_ref, mg_ref, z_ref, *xn_ref):
        pa = _dot(ya_ref[...], _side_by_side(paw_ref))
        pb = _dot(yb_ref[...], _side_by_side(pbw_ref))
        pa_ref[...] = pa.astype(BF16)
        pb_ref[...] = pb.astype(BF16)
        everything = slice(None)
        merged = _sigmoid(_cols(hr_ref, everything, 0, D)) * pa + _sigmoid(_cols(hr_ref, everything, D, D)) * pb
        mb = merged.astype(BF16)
        mg_ref[...] = mb
        z = ALPHA * x_ref[...] + (_dot(mb, wo_ref[...]) + bo_ref[...])
        z_ref[...] = z
        if not last:
            zhat, _ = _ln_stats(z)
            xn_ref[0][...] = zhat * g_ref[...] + b_ref[...]

    row = lambda w: pl.BlockSpec((TM, w), lambda i: (i, 0))
    vec = pl.BlockSpec((1, D), _CONST2)
    n_f32 = 1 if last else 2
    return pl.pallas_call(
        body, name=name, grid=(t // TM,),
        in_specs=[row(D), row(Q_W), row(SGU_W), row(R_W),
                  pl.BlockSpec((N_DEV, Q_W, 128), _CONST3), pl.BlockSpec((N_DEV, SGU_W, 128), _CONST3),
                  pl.BlockSpec((D, D), _CONST2), vec, vec, vec],
        out_specs=[row(D)] * (3 + n_f32),
        out_shape=[jax.ShapeDtypeStruct((t, D), BF16)] * 3 + [jax.ShapeDtypeStruct((t, D), F32)] * n_f32,
        compiler_params=_ARB,
    )(x, ya, yb, hr, pa_w, pb_w, wo, b_out, ln_g, ln_b)


def _tail_bwd(dxn, z, pa, pb, hr, wo, pa_w, pb_w, ln_g, ln_b, name, from_loss):
    t = dxn.shape[0]

    def body(dxn_ref, z_ref, pa_ref, pb_ref, hr_ref, wo_ref, paw_ref, pbw_ref, g_ref, b_ref,
             dz_ref, dpa_ref, dpb_ref, dhr_ref, dya_ref, dyb_ref, acc_ref, gbr_ref):
        @pl.when(pl.program_id(0) == 0)
        def _():
            acc_ref[...] = jnp.zeros_like(acc_ref)
            gbr_ref[...] = jnp.zeros_like(gbr_ref)

        zhat, rstd = _ln_stats(z_ref[...])
        if from_loss:
            err = zhat * g_ref[...] + b_ref[...] - dxn_ref[...]
            dxn_v = err * (1.0 / D)
            sq = jnp.sum(jnp.sum(err * err, axis=1, keepdims=True), axis=0, keepdims=True)
            acc_ref[3:4, :] += jnp.broadcast_to(sq, (1, D))
        else:
            dxn_v = dxn_ref[...]
        dz = _ln_bwd(dxn_v * g_ref[...], zhat, rstd)
        dz_ref[...] = dz
        acc_ref[0:1, :] += _colsum(dxn_v * zhat)
        acc_ref[1:2, :] += _colsum(dxn_v)
        acc_ref[2:3, :] += _colsum(dz)
        dmerged = _dot_nt(dz.astype(BF16), wo_ref[...])
        everything = slice(None)
        sa = _sigmoid(_cols(hr_ref, everything, 0, D))
        sb = _sigmoid(_cols(hr_ref, everything, D, D))
        dpa = (dmerged * sa).astype(BF16)
        dpb = (dmerged * sb).astype(BF16)
        dpa_ref[...] = dpa
        dpb_ref[...] = dpb
        dra = dmerged * pa_ref[...].astype(F32) * (sa * (1.0 - sa))
        drb = dmerged * pb_ref[...].astype(F32) * (sb * (1.0 - sb))
        dhr_ref[:, 0:D] = dra.astype(BF16)
        dhr_ref[:, D:2 * D] = drb.astype(BF16)
        gbr_ref[0:1, 0:D] += _colsum(dra)
        gbr_ref[0:1, D:2 * D] += _colsum(drb)
        dya_ref[...] = _dot_nt(dpa, _side_by_side(paw_ref)).astype(BF16)
        dyb_ref[...] = _dot_nt(dpb, _side_by_side(pbw_ref)).astype(BF16)

    row = lambda w: pl.BlockSpec((TM, w), lambda i: (i, 0))
    vec = pl.BlockSpec((1, D), _CONST2)
    return pl.pallas_call(
        body, name=name, grid=(t // TM,),
        in_specs=[row(D), row(D), row(D), row(D), row(R_W),
                  pl.BlockSpec((D, D), _CONST2), pl.BlockSpec((N_DEV, Q_W, 128), _CONST3),
                  pl.BlockSpec((N_DEV, SGU_W, 128), _CONST3), vec, vec],
        out_specs=[row(D), row(D), row(D), row(R_W), row(Q_W), row(SGU_W), pl.BlockSpec((8, D), _CONST2),
                   pl.BlockSpec((8, R_W), _CONST2)],
        out_shape=[jax.ShapeDtypeStruct((t, D), F32), jax.ShapeDtypeStruct((t, D), BF16),
                   jax.ShapeDtypeStruct((t, D), BF16), jax.ShapeDtypeStruct((t, R_W), BF16),
                   jax.ShapeDtypeStruct((t, Q_W), BF16), jax.ShapeDtypeStruct((t, SGU_W), BF16),
                   jax.ShapeDtypeStruct((8, D), F32), jax.ShapeDtypeStruct((8, R_W), F32)],
        compiler_params=_ARB,
    )(dxn, z, pa, pb, hr, wo, pa_w, pb_w, ln_g, ln_b)


def _mixer_bwd(hm, dya, dyb, prob, psink, vn_g, vn_b, w_s, bsb, name):
    t = hm.shape[0]
    nt = t // TM
    ngrp = N_KV * NB

    def body(hm_ref, hprev_ref, prob_ref, psink_ref, dya_ref, dyb_ref, vng_ref, vnb_ref, ws_ref, bsb_ref,
             dhm_ref, gbm_ref, gsk_ref, gvn_ref, gws_ref, gbs_ref, dk_carry, dv_carry):
        gi = pl.program_id(0)

        @pl.when(gi == 0)
        def _():
            for r in (gbm_ref, gsk_ref, gvn_ref, gws_ref, gbs_ref, dk_carry, dv_carry):
                r[...] = jnp.zeros_like(r)

        tril = _tril_mask()
        wt = [jnp.where(tril, ws_ref[g], 0.0).astype(BF16) for g in range(4)]
        vng = vng_ref[...]
        ones8 = jnp.ones((8, BLK), BF16)

        def put(rows, col, val):
            dhm_ref[rows, col:col + val.shape[1]] = val.astype(BF16)

        for s in reversed(range(NB)):
            r0 = s * BLK
            rows = slice(r0, r0 + BLK)
            q = _cols(hm_ref, rows, C_Q, Q_W)
            kband = _band(hm_ref, hprev_ref, s, C_K)
            vband = _band(hm_ref, hprev_ref, s, C_V)
            g_a = _cols(hm_ref, rows, C_GA, Q_W)
            sg = _sigmoid(g_a)
            dya_v = _cols(dya_ref, rows, 0, Q_W)
            d_o = dya_v * (g_a * sg)
            o_pairs, dq_pairs = [None] * 4, [None] * 4
            dkband = jnp.zeros((2 * BLK, KV_W), F32)
            dvband = jnp.zeros((2 * BLK, KV_W), F32)
            kband_t, vband_t = kband.T, vband.T
            for hk in range(N_KV):
                q4 = _stack_q(q, hk)
                prob_b = prob_ref[N_KV * s + hk]
                p_sink = psink_ref[N_KV * s + hk][0:1, :]
                o_t = _dot(_head_rows(vband_t, hk), prob_b)
                o_pairs = _unstack_heads(o_t.T, hk, o_pairs)
                parts = []
                for g in range(4):
                    _, p, pos = _head_place(hk, g)
                    dp = d_o[:, BLK * p:BLK * (p + 1)]
                    parts.append(pltpu.roll(dp, 64, 1) if pos != hk else dp)
                do4 = _head_lanes(jnp.concatenate(parts, axis=0), hk)
                do4b = do4.astype(BF16)
                delta = _colsum(do4.T * o_t)
                vh = _head_lanes(vband, hk).astype(BF16)
                ds_t = prob_b.astype(F32) * (_dot_nt(vh, do4b) - delta)
                dsb = ds_t.astype(BF16)
                dq4_t = _dot(_head_rows(kband_t, hk), dsb)
                dq_pairs = _unstack_heads(dq4_t.T * ATTN_SCALE, hk, dq_pairs)
                dkband = dkband + _head_lanes(_dot(dsb, q4), hk)
                dvband = dvband + _dot(prob_b, do4b)
                dsk = p_sink * delta
                for g in range(4):
                    j = 4 * hk + g
                    tot = jnp.sum(dsk[:, BLK * g:BLK * (g + 1)], axis=1, keepdims=True)
                    gsk_ref[j:j + 1, :] += jnp.broadcast_to(-tot, (1, 128))
            attn = jnp.concatenate(o_pairs, axis=1)
            put(rows, C_Q, jnp.concatenate(dq_pairs, axis=1))
            put(rows, C_K, dkband[BLK:2 * BLK] + dk_carry[...])
            put(rows, C_V, dvband[BLK:2 * BLK] + dv_carry[...])
            dk_carry[...] = dkband[0:BLK]
            dv_carry[...] = dvband[0:BLK]
            put(rows, C_GA, dya_v * attn * (sg * (1.0 + g_a * (1.0 - sg))))
            u, du_du_b = _gelu_and_grad(_cols(hm_ref, rows, C_UB, SGU_W))
            v, dv_dv_b = _gelu_and_grad(_cols(hm_ref, rows, C_VB, SGU_W))
            g_b = _cols(hm_ref, rows, C_GB, SGU_W)
            vhat, rstd, vn, mixed = _sgu_fwd(u, v, vng, vnb_ref[...], wt, bsb_ref)
            sgb = _sigmoid(g_b)
            silu_b = g_b * sgb
            dyb_v = _cols(dyb_ref, rows, 0, SGU_W)
            du = dyb_v * mixed * silu_b
            dmixed = dyb_v * u * silu_b
            put(rows, C_GB, dyb_v * u * mixed * (sgb * (1.0 + g_b * (1.0 - sgb))))
            dvn_parts = []
            for g in range(4):
                cols = slice(BLK * g, BLK * (g + 1))
                dmg = dmixed[:, cols]
                dmgb = dmg.astype(BF16)
                dvn_parts.append(_dot_tn(wt[g], dmgb))
                gws_ref[g] += jnp.where(tril, _dot_nt(dmgb, vn[:, cols].astype(BF16)), 0.0)
                gbs_ref[g] += dmg
            dvn = jnp.concatenate(dvn_parts, axis=1)
            gvn_ref[0:1, :] += _colsum(dvn * vhat)
            gvn_ref[1:2, :] += _colsum(dvn)
            dv = _ln_bwd(dvn * vng, vhat, rstd)
            put(rows, C_UB, du * du_du_b)
            put(rows, C_VB, dv * dv_dv_b)
            gbm_ref[...] += _dot(ones8, dhm_ref[rows, :])

        @pl.when(gi == nt - 1)
        def _():
            for g in range(4):
                gbs_ref[g] = jnp.broadcast_to(jnp.sum(gbs_ref[g], axis=1, keepdims=True), (BLK, BLK))

    row = lambda w: pl.BlockSpec((TM, w), lambda g: (nt - 1 - g, 0))
    return pl.pallas_call(
        body, name=name, grid=(nt,),
        in_specs=_h_main_specs(nt, True) + [
            pl.BlockSpec((ngrp, 2 * BLK, 4 * BLK), lambda g: (nt - 1 - g, 0, 0)),
            pl.BlockSpec((ngrp, 8, 4 * BLK), lambda g: (nt - 1 - g, 0, 0)),
            row(Q_W), row(SGU_W),
            pl.BlockSpec((1, SGU_W), _CONST2), pl.BlockSpec((1, SGU_W), _CONST2),
            pl.BlockSpec((4, BLK, BLK), _CONST3), pl.BlockSpec((4, BLK, BLK), _CONST3)],
        out_specs=[row(MAIN_W), pl.BlockSpec((8, MAIN_W), _CONST2), pl.BlockSpec((8, 128), _CONST2),
                   pl.BlockSpec((8, SGU_W), _CONST2), pl.BlockSpec((4, BLK, BLK), _CONST3),
                   pl.BlockSpec((4, BLK, BLK), _CONST3)],
        out_shape=[jax.ShapeDtypeStruct((t, MAIN_W), BF16), jax.ShapeDtypeStruct((8, MAIN_W), F32),
                   jax.ShapeDtypeStruct((8, 128), F32), jax.ShapeDtypeStruct((8, SGU_W), F32),
                   jax.ShapeDtypeStruct((4, BLK, BLK), F32), jax.ShapeDtypeStruct((4, BLK, BLK), F32)],
        scratch_shapes=[pltpu.VMEM((BLK, KV_W), F32), pltpu.VMEM((BLK, KV_W), F32)],
        compiler_params=_ARB,
    )(hm, hm, prob, psink, dya, dyb, vn_g, vn_b, w_s, bsb)


def _dx_inproj(dz, dhm, dhr, w_t, after, name, ln_in=None):
    t = dz.shape[0]

    def body(dz_ref, dhm_ref, dhr_ref, wt_ref, after_ref, *rest):
        dx = (ALPHA * dz_ref[...] + after_ref[0:1, 0:1] + _dot(dhm_ref[...], wt_ref[0:MAIN_W, :])
              + _dot(dhr_ref[...], wt_ref[MAIN_W:IN_COLS, :]))
        if ln_in is None:
            rest[0][...] = dx
            return
        x_ref, g_ref, gx_ref, acc_ref = rest

        @pl.when(pl.program_id(0) == 0)
        def _():
            acc_ref[...] = jnp.zeros_like(acc_ref)

        xhat, rstd = _ln_stats(x_ref[...])
        gx_ref[...] = _ln_bwd(dx * g_ref[...], xhat, rstd)
        acc_ref[0:1, :] += _colsum(dx * xhat)
        acc_ref[1:2, :] += _colsum(dx)

    row = lambda w: pl.BlockSpec((TM, w), lambda i: (i, 0))
    in_specs = [row(D), row(MAIN_W), row(R_W), pl.BlockSpec((IN_COLS, D), _CONST2), pl.BlockSpec((8, 128), _CONST2)]
    if ln_in is None:
        return pl.pallas_call(
            body, name=name, grid=(t // TM,), in_specs=in_specs,
            out_specs=row(D), out_shape=jax.ShapeDtypeStruct((t, D), F32), compiler_params=_ARB,
        )(dz, dhm, dhr, w_t, after)
    return pl.pallas_call(
        body, name=name, grid=(t // TM,), in_specs=in_specs + [row(D), pl.BlockSpec((1, D), _CONST2)],
        out_specs=[row(D), pl.BlockSpec((8, D), _CONST2)],
        out_shape=[jax.ShapeDtypeStruct((t, D), F32), jax.ShapeDtypeStruct((8, D), F32)], compiler_params=_ARB,
    )(dz, dhm, dhr, w_t, after, *ln_in)


def _wgrad(a, b, tm, name, rows=None, under=None, by_owner=False):
    t, m = a.shape
    n = b.shape[1]
    tk = min(t, 2048)
    nk = t // tk

    def body(a_ref, b_ref, *rest):
        o_ref, acc_ref = rest[-2:]
        k = pl.program_id(1)

        @pl.when(k == 0)
        def _():
            acc_ref[...] = jnp.zeros_like(acc_ref)

        acc_ref[...] += _dot_tn(a_ref[...].astype(BF16), b_ref[...].astype(BF16))

        @pl.when(k == nk - 1)
        def _():
            if by_owner:
                for j in range(N_DEV):
                    o_ref[j] = acc_ref[:, j * (n // N_DEV):(j + 1) * (n // N_DEV)].astype(BF16)
            else:
                o_ref[...] = acc_ref[...].astype(BF16)

    in_specs = [pl.BlockSpec((tk, tm), lambda j, k: (k, j)), pl.BlockSpec((tk, n), lambda j, k: (k, 0))]
    if by_owner:
        return pl.pallas_call(
            body, name=name, grid=(m // tm, nk), in_specs=in_specs,
            out_specs=pl.BlockSpec((N_DEV, tm, n // N_DEV), lambda j, k: (0, j, 0)),
            out_shape=jax.ShapeDtypeStruct((N_DEV, m, n // N_DEV), BF16), scratch_shapes=[pltpu.VMEM((tm, n), F32)],
            compiler_params=pltpu.CompilerParams(dimension_semantics=("arbitrary", "arbitrary"),
                                                 vmem_limit_bytes=VMEM_LIMIT),
        )(a, b)
    if under is None:
        out_rows, out_spec, operands, aliases = rows or m, pl.BlockSpec((tm, n), lambda j, k: (j, 0)), (a, b), {}
    else:
        out_rows = under.shape[0]
        first = out_rows - m
        assert first % 128 == 0 and tm % 128 == 0
        out_spec = pl.BlockSpec((pl.Element(tm), pl.Element(n)),
                                lambda j, k: (pl.multiple_of(first + j * tm, 128), 0))
        in_specs, operands, aliases = in_specs + [_ANY], (a, b, under), {2: 0}
    return pl.pallas_call(
        body, name=name, grid=(m // tm, nk), in_specs=in_specs, out_specs=out_spec,
        out_shape=jax.ShapeDtypeStruct((out_rows, n), BF16), input_output_aliases=aliases,
        scratch_shapes=[pltpu.VMEM((tm, n), F32)],
        compiler_params=pltpu.CompilerParams(dimension_semantics=("arbitrary", "arbitrary"), vmem_limit_bytes=VMEM_LIMIT),
    )(*operands)


_ANY = pl.BlockSpec(memory_space=pl.ANY)


def _place():
    return lax.axis_index("x"), lax.axis_index("y"), lax.axis_index("c")


def _forward_sibling(lands, name):
    n = len(lands)

    def body(*refs):
        l_refs = refs[n:2 * n]
        send_sems, recv_sems = refs[2 * n:]
        x, y, c = _place()
        chips = [(1 - x, y), (x, 1 - y), (1 - x, 1 - y)]

        def copy(a, j, core):
            rows = l_refs[a].at[4 * chips[j][0] + 2 * chips[j][1] + core]
            return pltpu.make_async_remote_copy(
                src_ref=rows, dst_ref=rows, send_sem=send_sems.at[3 * a + j], recv_sem=recv_sems.at[3 * a + j],
                device_id=(x, y, 1 - c), device_id_type=MESH)

        for a in range(n):
            for j in range(3):
                copy(a, j, c).start()
        for a in range(n):
            for j in range(3):
                copy(a, j, 1 - c).wait_recv()
                copy(a, j, c).wait_send()

    return pl.pallas_call(
        body, name=name, in_specs=[_ANY] * n, out_specs=[_ANY] * n,
        out_shape=[jax.ShapeDtypeStruct(v.shape, v.dtype) for v in lands],
        input_output_aliases={a: a for a in range(n)},
        scratch_shapes=[pltpu.SemaphoreType.DMA((3 * n,)), pltpu.SemaphoreType.DMA((3 * n,))],
    )(*lands)


def _swap_sibling(gs, name):
    n = len(gs)
    first = [0]
    for v in gs:
        first.append(first[-1] + v.shape[0])

    def body(*refs):
        g_refs, r_refs = refs[:n], refs[n:2 * n]
        send_sems, recv_sems = refs[2 * n:]
        x, y, c = _place()
        cps = [pltpu.make_async_remote_copy(
            src_ref=g_refs[a].at[q, 1 - c], dst_ref=r_refs[a].at[q], send_sem=send_sems.at[first[a] + q],
            recv_sem=recv_sems.at[first[a] + q], device_id=(x, y, 1 - c), device_id_type=MESH)
            for a in range(n) for q in range(gs[a].shape[0])]
        for cp in cps:
            cp.start()
        for cp in cps:
            cp.wait()

    return pl.pallas_call(
        body, name=name, in_specs=[_ANY] * n, out_specs=[_ANY] * n,
        out_shape=[jax.ShapeDtypeStruct(v.shape[:1] + v.shape[2:], v.dtype) for v in gs],
        scratch_shapes=[pltpu.SemaphoreType.DMA((first[-1],)), pltpu.SemaphoreType.DMA((first[-1],))],
    )(*gs)


def _row_tile(rows, lanes, cap):
    if rows * lanes * 4 <= (1 << 20):
        return rows
    return max(d for d in range(8, cap + 1, 8) if rows % d == 0 and (d % 16 == 0 or rows % 16 != 0))


def _pair_sums(gs, rs, name):
    n = len(gs)

    def add(g, r, dtype):
        return (g.astype(F32) + r.astype(F32)).astype(dtype)

    def body(c_ref, *refs):
        g_refs, r_refs, o_refs = refs[:n], refs[n:2 * n], refs[2 * n:]
        o_refs[0][...] = add(g_refs[0][0], r_refs[0][...], o_refs[0].dtype)

        @pl.when(pl.program_id(0) == 0)
        def _():
            for a in range(1, n):
                o_refs[a][...] = add(g_refs[a][:, 0], r_refs[a][...], o_refs[a].dtype)

    def whole(shape, mine):
        if mine:
            return pl.BlockSpec(shape, lambda q, c_ref: (0, c_ref[0]) + (0,) * (len(shape) - 2))
        return pl.BlockSpec(shape, lambda q, c_ref: (0,) * len(shape))

    big = gs[0].shape
    return pl.pallas_call(
        body, name=name,
        grid_spec=pltpu.PrefetchScalarGridSpec(
            num_scalar_prefetch=1, grid=(big[0],),
            in_specs=[pl.BlockSpec((1, 1) + big[2:], lambda q, c_ref: (q, c_ref[0], 0, 0))]
            + [whole(g.shape[:1] + (1,) + g.shape[2:], True) for g in gs[1:]]
            + [pl.BlockSpec((1,) + big[2:], lambda q, c_ref: (q, 0, 0))]
            + [whole(r.shape, False) for r in rs[1:]],
            out_specs=[pl.BlockSpec((1,) + big[2:], lambda q, c_ref: (q, 0, 0))]
            + [whole(r.shape, False) for r in rs[1:]]),
        out_shape=[jax.ShapeDtypeStruct(r.shape, g.dtype) for g, r in zip(gs, rs)],
        compiler_params=_ARB,
    )(lax.axis_index("c").astype(jnp.int32).reshape(1), *gs, *rs)


def _adamw(parts, w, m, v, name, own=None):
    nl = len(parts)
    ns, rows, l = parts[0].shape
    tr = _row_tile(rows, l * ns, 304)
    nt = rows // tr
    c1 = 1.0 - ADAM_B1 ** ADAM_STEP
    c2 = 1.0 - ADAM_B2 ** ADAM_STEP

    def body(q_ref, *refs):
        own_refs = refs[:nl] if own is not None else None
        p_refs = refs[-7 - nl:-7]
        w_ref, m_ref, v_ref, g_ref, d_ref, nm_ref, nv_ref = refs[-7:]
        layer = pl.program_id(0)
        g = None
        for j in range(nl):
            gj = None
            for k in range(ns):
                term = p_refs[j][k].astype(F32)
                if own_refs is not None:
                    term = jnp.where(q_ref[0] == k, own_refs[j][0].astype(F32), term)
                gj = term if gj is None else gj + term
            g = gj if g is None else jnp.where(layer == j, gj, g)
        g_ref[...] = g
        nm = ADAM_B1 * m_ref[...] + (1.0 - ADAM_B1) * g
        nv = ADAM_B2 * v_ref[...] + (1.0 - ADAM_B2) * (g * g)
        nm_ref[...] = nm
        nv_ref[...] = nv
        d_ref[...] = -ADAM_LR * ((nm / c1) / (jnp.sqrt(nv / c2) + ADAM_EPS) + ADAM_WD * w_ref[...])

    def tile_of(j):
        return lambda la, i, q: jnp.where(la == j, i, jnp.where(la < j, 0, nt - 1))

    row = pl.BlockSpec((tr, l), lambda la, i, q: (la * nt + i, 0))
    own_specs = [] if own is None else [
        pl.BlockSpec((1, tr, l), lambda la, i, q, j=j: (q[0], tile_of(j)(la, i, q), 0)) for j in range(nl)]
    part_specs = [pl.BlockSpec((ns, tr, l), lambda la, i, q, j=j: (0, tile_of(j)(la, i, q), 0)) for j in range(nl)]
    chip = (2 * lax.axis_index("x") + lax.axis_index("y")).astype(jnp.int32).reshape(1)
    return pl.pallas_call(
        body, name=name,
        grid_spec=pltpu.PrefetchScalarGridSpec(
            num_scalar_prefetch=1, grid=(nl, nt),
            in_specs=own_specs + part_specs + [row, row, row], out_specs=[row] * 4),
        out_shape=[jax.ShapeDtypeStruct((nl * rows, l), F32)] * 4,
        compiler_params=pltpu.CompilerParams(dimension_semantics=("arbitrary", "arbitrary"), vmem_limit_bytes=VMEM_LIMIT),
    )(chip, *([] if own is None else own), *parts, w, m, v)


_HBM = pl.BlockSpec(memory_space=pltpu.HBM)
_SEM = pl.BlockSpec(memory_space=pltpu.SEMAPHORE)
_EFFECT = pltpu.SideEffectType.DATAFLOW_SIDE_EFFECTING


def _plan_all(x, y, c):
    me = 4 * x + 2 * y + c
    peers = [(x, y, c), (x, y, 1 - c), (1 - x, y, c), (x, 1 - y, c), (1 - x, 1 - y, c),
             (1 - x, y, 1 - c), (x, 1 - y, 1 - c), (1 - x, 1 - y, 1 - c)]
    return [(None, me, p, 4 * p[0] + 2 * p[1] + p[2]) for p in peers]


def _plan_near(x, y, c):
    me = 4 * x + 2 * y + c
    peers = [(x, y, 1 - c), (1 - x, y, c), (x, 1 - y, c), (1 - x, 1 - y, c)]
    return [(None, me, p, 4 * p[0] + 2 * p[1] + p[2]) for p in peers]


def _plan_sibling(x, y, c):
    return [(2 * q + 1 - c, q, (x, y, 1 - c), q) for q in range(4)]


def _plan_chips(x, y, c):
    me = 2 * x + y
    return [(2 * qx + qy, me, (qx, qy, c), 2 * qx + qy) for qx, qy in ((1 - x, y), (x, 1 - y), (1 - x, 1 - y))]


def _split_copies(plan, src_refs, land_refs, send_sems, recv_sems, arrival):
    n = len(src_refs)
    entries = plan(*_place())
    per = len(entries)
    cps = []
    for a in range(n):
        for k, (src_slot, dst_slot, peer, back_slot) in enumerate(entries):
            src = src_refs[a] if src_slot is None else src_refs[a].at[src_slot]
            cps.append(pltpu.make_async_remote_copy(
                src_ref=src, dst_ref=land_refs[a].at[back_slot if arrival else dst_slot],
                send_sem=send_sems.at[per * a + k], recv_sem=recv_sems.at[per * a + k],
                device_id=peer, device_id_type=MESH))
    return cps


def _split_start(srcs, lands, plan, per, name):
    n = len(srcs)

    def body(*refs):
        for cp in _split_copies(plan, refs[:n], refs[n:2 * n], refs[2 * n], refs[2 * n + 1], False):
            cp.start()
        refs[-1][...] = jnp.zeros_like(refs[-1])

    both = list(srcs) + list(lands)
    outs = pl.pallas_call(
        body, name=name,
        out_shape=(pltpu.SemaphoreType.DMA((per * n,)), pltpu.SemaphoreType.DMA((per * n,)),
                   *[pltpu.HBM(v.shape, v.dtype) for v in both], jax.ShapeDtypeStruct((8, 128), F32)),
        in_specs=[_HBM] * (2 * n),
        out_specs=(_SEM, _SEM, *[_HBM] * (2 * n), pl.BlockSpec(memory_space=pltpu.VMEM)),
        input_output_aliases={i: 2 + i for i in range(2 * n)},
        compiler_params=pltpu.CompilerParams(has_side_effects=_EFFECT),
    )(*[pltpu.with_memory_space_constraint(v, pltpu.HBM) for v in both])
    return outs[0], outs[1], list(outs[2:2 + 2 * n]), outs[-1]


def _split_wait(send_sems, recv_sems, thru, plan, after, name):
    n = len(thru) // 2

    def body(*refs):
        for cp in _split_copies(plan, refs[:n], refs[n:2 * n], refs[2 * n], refs[2 * n + 1], True):
            cp.wait_send()
            cp.wait_recv()

    outs = pl.pallas_call(
        body, name=name, out_shape=tuple(pltpu.HBM(v.shape, v.dtype) for v in thru),
        in_specs=[_HBM] * (2 * n) + [_SEM, _SEM, pl.BlockSpec(memory_space=pl.ANY)],
        out_specs=[_HBM] * (2 * n), input_output_aliases={i: i for i in range(2 * n)},
        compiler_params=pltpu.CompilerParams(has_side_effects=_EFFECT),
    )(*thru, send_sems, recv_sems, after)
    return list(outs[:n]), list(outs[n:])


_SMALL_IN = ("ln_in_g", "ln_in_b")
_SMALL_ROWS = ("w_s", "b_s", "sinks")
_SMALL_LANES = ("b_in", "vn_g", "vn_b", "b_out", "ln_g", "ln_b")


def _tile_rows(a):
    return -(-a.size // 1024) * 8


def _pack_small(d, names):
    return jnp.concatenate([jnp.pad(d[n].reshape(-1), (0, (-d[n].size) % 1024)).reshape(-1, 128) for n in names])


def _adamw_small(parts, own, w, m, v, pieces, axis, name):
    c1 = 1.0 - ADAM_B1 ** ADAM_STEP
    c2 = 1.0 - ADAM_B2 ** ADAM_STEP
    shapes = [tuple(p if d == axis else s for d, s in enumerate(w.shape)) for p in pieces]

    def body(q_ref, own_ref, p_ref, w_ref, m_ref, v_ref, *o_refs):
        g = None
        for k in range(4):
            term = jnp.where(q_ref[0] == k, own_ref[0], p_ref[k])
            g = term if g is None else g + term
        nm = ADAM_B1 * m_ref[...] + (1.0 - ADAM_B1) * g
        nv = ADAM_B2 * v_ref[...] + (1.0 - ADAM_B2) * (g * g)
        delta = -ADAM_LR * ((nm / c1) / (jnp.sqrt(nv / c2) + ADAM_EPS) + ADAM_WD * w_ref[...])
        for k, val in enumerate((g, delta, nm, nv)):
            off = 0
            for j, p in enumerate(pieces):
                o_refs[k * len(pieces) + j][...] = val[off:off + p] if axis == 0 else val[:, off:off + p]
                off += p

    whole = pl.BlockSpec(w.shape, lambda i, q: (0, 0))
    chip = (2 * lax.axis_index("x") + lax.axis_index("y")).astype(jnp.int32).reshape(1)
    outs = pl.pallas_call(
        body, name=name,
        grid_spec=pltpu.PrefetchScalarGridSpec(
            num_scalar_prefetch=1, grid=(1,),
            in_specs=[pl.BlockSpec((1,) + w.shape, lambda i, q: (q[0], 0, 0)),
                      pl.BlockSpec((4,) + w.shape, lambda i, q: (0, 0, 0)), whole, whole, whole],
            out_specs=[pl.BlockSpec(s, lambda i, q: (0, 0)) for s in shapes] * 4),
        out_shape=[jax.ShapeDtypeStruct(s, F32) for s in shapes] * 4, compiler_params=_ARB,
    )(chip, own, parts, w, m, v)
    return [outs[k * len(pieces):(k + 1) * len(pieces)] for k in range(4)]


def kernel(x, ln_in_g, ln_in_b, w_in, b_in, sinks, vn_g, vn_b, w_s, b_s, p_a, p_b, w_out, b_out, ln_g, ln_b, loss_target, m_ln_in_g, m_ln_in_b, m_w_in, m_b_in, m_sinks, m_vn_g, m_vn_b, m_w_s, m_b_s, m_p_a, m_p_b, m_w_out, m_b_out, m_ln_g, m_ln_b, v_ln_in_g, v_ln_in_b, v_w_in, v_b_in, v_sinks, v_vn_g, v_vn_b, v_w_s, v_b_s, v_p_a, v_p_b, v_w_out, v_b_out, v_ln_g, v_ln_b):
    nseq, seq, _ = x.shape
    t = nseq * seq
    nblk_seq = seq // BLK
    x2 = x.reshape(t, D)
    tgt = loss_target.reshape(t, D)

    def turned(a):
        return jnp.swapaxes(a, 1, 2)

    w_in_t = turned(w_in)

    def blocks(l):
        return [w_in_t[l].astype(BF16), p_a[l].astype(BF16), p_b[l].astype(BF16), w_out[l].astype(BF16)]

    def full_weights(g):
        w_t_full = g[0].reshape(IN_COLS, D)
        wo_full = g[3].reshape(D, D)
        return dict(w_t=w_t_full, pa=g[1], pb=g[2], wo=wo_full)

    def landing(bs):
        return [lax.empty((N_DEV,) + v.shape, v.dtype) for v in bs]

    def with_own(landed, sent):
        return [lax.dynamic_update_index_in_dim(g, b, me, 0) for g, b in zip(landed, sent)]

    me = 4 * lax.axis_index("x") + 2 * lax.axis_index("y") + lax.axis_index("c")
    blocks0 = blocks(0)
    a_send, a_recv, a_thru, a_token = _split_start(blocks0[:1], landing(blocks0[:1]), _plan_near, 4,
                                                   "allgather_w_in0_start")
    rest0 = [b + a_token[0, 0].astype(BF16) for b in blocks0[1:]]
    b_send, b_recv, b_thru, b_token = _split_start(rest0, landing(rest0), _plan_all, 8, "allgather_rest0_start")
    xs = [_ln_fwd(x2, ln_in_g + b_token[0, 0], ln_in_b, "ln_in_fwd")]
    sent, landed = _split_wait(a_send, a_recv, a_thru, _plan_near, xs[0], "allgather_w_in0_wait")
    gathered0 = with_own(_forward_sibling(landed, "allgather_w_in0_forward"), sent)
    blocks1, gathered0 = lax.optimization_barrier((blocks(1), gathered0))
    ag_send, ag_recv, ag_thru, ag_token = _split_start(blocks1, landing(blocks1), _plan_all, 8,
                                                       "allgather_weights1_start")
    weights = [None, None]
    bsb = jnp.broadcast_to(b_s[:, :, :, None], (DEPTH, 4, BLK, BLK))
    bias = _band_bias()

    saved = []
    for l in range(DEPTH):
        if l == 1:
            _, landed = _split_wait(ag_send, ag_recv, ag_thru, _plan_all, xs[1], "allgather_weights1_wait")
            weights[1] = full_weights(landed)
        w_t = weights[l]["w_t"] if l else gathered0[0].reshape(IN_COLS, D)
        last = l == DEPTH - 1
        b_l = b_in[l].reshape(1, -1) + (ag_token[0, 0] if l == 0 else 0.0)
        hm, hr = _inproj(xs[l], w_t, b_l, f"inproj{l}")
        ya, yb, prob, psink = _mixer_fwd(hm, sinks[l], bias, vn_g[l].reshape(1, -1), vn_b[l].reshape(1, -1),
                                         w_s[l], bsb[l], nblk_seq, f"mixer_fwd{l}")
        if l == 0:
            _, landed = _split_wait(b_send, b_recv, b_thru, _plan_all, ya, "allgather_rest0_wait")
            weights[0] = full_weights(gathered0 + landed)
        wl = weights[l]
        outs = _tail_fwd(xs[l], ya, yb, hr, wl["pa"], wl["pb"], wl["wo"], b_out[l].reshape(1, D),
                         ln_g[l].reshape(1, D), ln_b[l].reshape(1, D), f"tail_fwd{l}", last)
        saved.append((hm, hr, ya, yb, prob, psink) + tuple(outs[:4]))
        if not last:
            xs.append(outs[4])

    small = {n: [None] * DEPTH for n in _SMALL_ROWS + _SMALL_LANES}

    def pack_rows(d):
        return _pack_small(d, _SMALL_ROWS)

    def pack_lanes(d):
        return jnp.concatenate([d[n] for n in _SMALL_LANES], axis=1)
    names = ("w_in", "p_a", "p_b", "w_out")
    token = jnp.zeros((8, 128), F32)
    dx = tgt
    split = [None] * DEPTH
    for l in reversed(range(DEPTH)):
        hm, hr, ya, yb, prob, psink, pa, pb, merged, z = saved[l]
        wl = weights[l]
        dz, dpa, dpb, dhr, dya, dyb, acc, gbr = _tail_bwd(
            dx, z, pa, pb, hr, wl["wo"], wl["pa"], wl["pb"], ln_g[l].reshape(1, D) + token[0, 0],
            ln_b[l].reshape(1, D), f"tail_bwd{l}", l == DEPTH - 1)
        if l == DEPTH - 1:
            sq_err = acc[3:4, 0:128]
        dhm, gbm, gsk, gvn, gws, gbs = _mixer_bwd(
            hm, dya, dyb, prob, psink, vn_g[l].reshape(1, -1), vn_b[l].reshape(1, -1), w_s[l], bsb[l],
            f"mixer_bwd{l}")
        grads = {"w_in": _wgrad(dhr, xs[l], R_W // 2, f"wgrad_in_route{l}",
                                under=_wgrad(dhm, xs[l], MAIN_W // 2, f"wgrad_in_main{l}", rows=IN_COLS)),
                 "p_a": _wgrad(ya, dpa, Q_W, f"wgrad_pa{l}", by_owner=True),
                 "p_b": _wgrad(yb, dpb, SGU_W, f"wgrad_pb{l}", by_owner=True),
                 "w_out": _wgrad(merged, dz, D, f"wgrad_out{l}")}
        small["b_in"][l] = jnp.concatenate([gbm[0], gbr[0]])
        small["sinks"][l] = gsk[:, 0]
        small["vn_g"][l], small["vn_b"][l] = gvn[0], gvn[1]
        small["w_s"][l], small["b_s"][l] = gws, gbs[:, :, 0]
        small["ln_g"][l], small["ln_b"][l], small["b_out"][l] = acc[0], acc[1], acc[2]
        parts = [grads[n].reshape((4, 2, -1, grads[n].shape[-1])) for n in names]
        if l == 0:
            stacked = {n: jnp.stack(v) for n, v in small.items()}
            for packed in (pack_rows(stacked), pack_lanes(stacked)):
                parts.append(jnp.broadcast_to(packed[None, None], (1, 2) + packed.shape))
        if l == 0:
            from_sib = _swap_sibling(parts, f"rs_sibling{l}")
        else:
            halves = [p.reshape((N_DEV,) + p.shape[2:]) for p in parts]
            sib = _split_start(halves, [lax.empty((4,) + p.shape[2:], p.dtype) for p in parts], _plan_sibling, 4,
                               f"rs_sibling{l}_start")
            dx = _dx_inproj(dz, dhm, dhr, wl["w_t"], sib[3], f"dx_inproj{l}")
            halves, from_sib = _split_wait(sib[0], sib[1], sib[2], _plan_sibling, dx, f"rs_sibling{l}_wait")
            parts = [h.reshape(p.shape) for h, p in zip(halves, parts)]
        pair = list(_pair_sums(parts, from_sib, f"pair_sums{l}"))
        if l == 0:
            pair[4:] = [jnp.broadcast_to(p, (4,) + p.shape[1:]) for p in pair[4:]]
        lands = [lax.empty(p.shape, p.dtype) for p in pair]
        split[l] = _split_start(pair, lands, _plan_chips, 3, f"rs_chips{l}_start")
        token = split[l][3]
        if l == 0:
            grad_x, acc_in = _dx_inproj(dz, dhm, dhr, wl["w_t"], token, f"dx_inproj{l}",
                                        ln_in=(x2, ln_in_g.reshape(1, D)))
    last = [acc_in, jnp.broadcast_to(sq_err, (8, 128))]
    ln_send, ln_recv, ln_thru, ln_token = _split_start(last, landing(last), _plan_all, 8, "allgather_ln_in_start")

    given = {"w_in": (w_in_t, turned(m_w_in), turned(v_w_in)), "p_a": (p_a, m_p_a, v_p_a),
             "p_b": (p_b, m_p_b, v_p_b), "w_out": (w_out, m_w_out, v_w_out)}
    waited = [_split_wait(split[l][0], split[l][1], split[l][2], _plan_chips, ln_token, f"rs_chips{l}_wait")
              for l in range(DEPTH)]
    res = {}
    for a, n in enumerate(names):
        rows, lanes = waited[0][1][a].shape[1:]
        outs = _adamw([waited[l][1][a] for l in range(DEPTH)], *[v.reshape(DEPTH * rows, lanes) for v in given[n]],
                      f"adamw_{n}", own=[waited[l][0][a] for l in range(DEPTH)])
        res[n] = [o.reshape(given[n][0].shape) for o in outs]
    res["w_in"] = [turned(o) for o in res["w_in"]]

    w_small = dict(ln_in_g=ln_in_g, ln_in_b=ln_in_b, b_in=b_in, sinks=sinks, vn_g=vn_g, vn_b=vn_b, w_s=w_s, b_s=b_s,
                   b_out=b_out, ln_g=ln_g, ln_b=ln_b)
    m_small = dict(ln_in_g=m_ln_in_g, ln_in_b=m_ln_in_b, b_in=m_b_in, sinks=m_sinks, vn_g=m_vn_g, vn_b=m_vn_b,
                   w_s=m_w_s, b_s=m_b_s, b_out=m_b_out, ln_g=m_ln_g, ln_b=m_ln_b)
    v_small = dict(ln_in_g=v_ln_in_g, ln_in_b=v_ln_in_b, b_in=v_b_in, sinks=v_sinks, vn_g=v_vn_g, vn_b=v_vn_b,
                   w_s=v_w_s, b_s=v_b_s, b_out=v_b_out, ln_g=v_ln_g, ln_b=v_ln_b)
    by_rows = _adamw_small(waited[0][1][4], waited[0][0][4], *[pack_rows(d) for d in (w_small, m_small, v_small)],
                           [_tile_rows(w_small[n]) for n in _SMALL_ROWS], 0, "adamw_small_rows")
    by_lanes = _adamw_small(waited[0][1][5], waited[0][0][5], *[pack_lanes(d) for d in (w_small, m_small, v_small)],
                            [w_small[n].shape[1] for n in _SMALL_LANES], 1, "adamw_small_lanes")
    _, (all_in, all_sq) = _split_wait(ln_send, ln_recv, ln_thru, _plan_all, by_lanes[0][0], "allgather_ln_in_wait")
    loss = jnp.sum(all_sq[:, 0, 0]) * (0.5 / D)
    outs_in = _adamw([all_in], *[jnp.pad(jnp.stack([d[n] for n in _SMALL_IN]), ((0, 6), (0, 0)))
                                 for d in (w_small, m_small, v_small)], "adamw_ln_in")
    for k in range(4):
        u = {n: o.reshape(-1)[:w_small[n].size].reshape(w_small[n].shape) for n, o in zip(_SMALL_ROWS, by_rows[k])}
        u.update(zip(_SMALL_LANES, by_lanes[k]))
        u.update({n: outs_in[k][r] for r, n in enumerate(_SMALL_IN)})
        for n in u:
            res.setdefault(n, [None] * 4)[k] = u[n]

    order = ("ln_in_g", "ln_in_b", "w_in", "b_in", "sinks", "vn_g", "vn_b", "w_s", "b_s", "p_a", "p_b", "w_out",
             "b_out", "ln_g", "ln_b")
    return (loss, grad_x.reshape(x.shape), *[res[n][0] for n in order], *[res[n][1] for n in order],
            *[res[n][2] for n in order], *[res[n][3] for n in order])
```

```python
import jax
import jax.numpy as jnp
from jax import lax
from jax.experimental import pallas as pl
from jax.experimental.pallas import tpu as pltpu

F32 = jnp.float32
BF16 = jnp.bfloat16

D = 1024
BLK = 128
N_KV = 2
Q_W, KV_W, SGU_W = 512, 128, 512
C_Q, C_K, C_V, C_GA, C_UB, C_VB, C_GB = 0, 512, 640, 768, 1280, 1792, 2304
MAIN_W = 2816
R_W = 2048
IN_COLS = MAIN_W + R_W
N_DEV = 8

DEPTH = 2
ALPHA = (2.0 * DEPTH) ** 0.25
LN_EPS = 1e-5
ATTN_SCALE = 0.125
NEG = float(jnp.finfo(jnp.float32).min)

ADAM_LR, ADAM_B1, ADAM_B2, ADAM_EPS, ADAM_WD, ADAM_STEP = 0.001, 0.9, 0.999, 1e-08, 0.01, 10

TM = 512
TM_EW = 1024
NB = TM // BLK
MESH = pl.DeviceIdType.MESH
VMEM_LIMIT = 56 * 1024 * 1024

_ARB = pltpu.CompilerParams(dimension_semantics=("arbitrary",), vmem_limit_bytes=VMEM_LIMIT)


def _sigmoid(x):
    return 0.5 + 0.5 * jnp.tanh(0.5 * x)


_GELU_C = 0.7978845608028654
_GELU_A = 0.044715


def _gelu_parts(x):
    x2 = x * x
    t = jnp.tanh(x * (_GELU_C + (_GELU_C * _GELU_A) * x2))
    hx = 0.5 * x
    return hx, t, x2


def _gelu(x):
    hx, t, _ = _gelu_parts(x)
    return hx + hx * t


def _gelu_and_grad(x):
    hx, t, x2 = _gelu_parts(x)
    grad = 0.5 + 0.5 * t + (hx - hx * (t * t)) * (_GELU_C + (3.0 * _GELU_C * _GELU_A) * x2)
    return hx + hx * t, grad


def _ln_stats(x):
    mu = jnp.mean(x, axis=-1, keepdims=True)
    xc = x - mu
    var = jnp.mean(xc * xc, axis=-1, keepdims=True)
    rstd = lax.rsqrt(var + LN_EPS)
    return xc * rstd, rstd


def _ln_bwd(dy_g, xhat, rstd):
    m1 = jnp.mean(dy_g, axis=-1, keepdims=True)
    m2 = jnp.mean(dy_g * xhat, axis=-1, keepdims=True)
    return rstd * (dy_g - m1 - xhat * m2)


def _colsum(x):
    return jnp.sum(x, axis=0, keepdims=True)


def _dot(a, b):
    return jnp.dot(a, b, preferred_element_type=F32)


def _dot_nt(a, b):
    return lax.dot_general(a, b, (((1,), (1,)), ((), ())), preferred_element_type=F32)


def _side_by_side(gathered_ref):
    return jnp.concatenate([gathered_ref[j] for j in range(N_DEV)], axis=1)


def _dot_tn(a, b):
    return lax.dot_general(a, b, (((0,), (0,)), ((), ())), preferred_element_type=F32)


def _head_place(hk, g):
    j = 4 * hk + g
    return j, j // 2, j % 2


def _head_rows(x, hk):
    d = lax.broadcasted_iota(jnp.int32, x.shape, 0)
    return jnp.where((d >= 64 * hk) & (d < 64 * hk + 64), x, 0.0).astype(BF16)


def _head_lanes(x, hk):
    d = lax.broadcasted_iota(jnp.int32, x.shape, 1)
    return jnp.where((d >= 64 * hk) & (d < 64 * hk + 64), x, 0.0)


def _band_bias():
    kpos = lax.broadcasted_iota(jnp.int32, (2 * BLK, 4 * BLK), 0)
    row = lax.broadcasted_iota(jnp.int32, (2 * BLK, 4 * BLK), 1) & (BLK - 1)
    band = (kpos > row) & (kpos <= row + BLK)
    return jnp.stack([jnp.where(band, 0.0, NEG), jnp.where(band & (kpos >= BLK), 0.0, NEG)]).astype(F32)


def _stack_q(q, hk):
    parts = []
    for g in range(4):
        _, p, pos = _head_place(hk, g)
        qp = q[:, BLK * p:BLK * (p + 1)] * ATTN_SCALE
        if pos != hk:
            qp = pltpu.roll(qp, 64, 1)
        parts.append(qp.astype(BF16))
    return jnp.concatenate(parts, axis=0)


def _attn_probs(q4, kh, hk, sinks_ref, bias):
    s_t = _dot_nt(kh, q4) + bias
    sink_row = jnp.concatenate(
        [jnp.full((1, BLK), sinks_ref[4 * hk + g], F32) for g in range(4)], axis=1)
    m = jnp.maximum(jnp.max(s_t, axis=0, keepdims=True), sink_row)
    p_un = jnp.exp(s_t - m)
    e_sink = jnp.exp(sink_row - m)
    inv = 1.0 / (jnp.sum(p_un, axis=0, keepdims=True) + e_sink)
    return (p_un * inv).astype(BF16), e_sink * inv


def _unstack_heads(x4, hk, pairs):
    for g in range(4):
        _, p, pos = _head_place(hk, g)
        xg = x4[BLK * g:BLK * (g + 1)]
        if pos != hk:
            xg = pltpu.roll(xg, 64, 1)
        pairs[p] = xg if pairs[p] is None else pairs[p] + xg
    return pairs


def _attn_fwd(q, kband, vband, sinks_ref, bias, save):
    pairs = [None] * 4
    vband_t = vband.T
    for hk in range(N_KV):
        prob_t, p_sink = _attn_probs(_stack_q(q, hk), _head_lanes(kband, hk).astype(BF16), hk, sinks_ref, bias)
        save(hk, prob_t, p_sink)
        o_t = _dot(_head_rows(vband_t, hk), prob_t)
        pairs = _unstack_heads(o_t.T, hk, pairs)
    return jnp.concatenate(pairs, axis=1)


def _tril_mask():
    r = lax.broadcasted_iota(jnp.int32, (BLK, BLK), 0)
    c = lax.broadcasted_iota(jnp.int32, (BLK, BLK), 1)
    return c <= r


def _sgu_fwd(u, v, vn_g, vn_b, wt, bsb_ref):
    vhat, rstd = _ln_stats(v)
    vn = vhat * vn_g + vn_b
    mixed = jnp.concatenate(
        [_dot(wt[g], vn[:, BLK * g:BLK * (g + 1)].astype(BF16)) + bsb_ref[g] for g in range(4)], axis=1)
    return vhat, rstd, vn, mixed


def _cols(ref, rows, col, width):
    return ref[rows, col:col + width].astype(F32)


def _band(hm_ref, hprev_ref, s, col):
    r0 = s * BLK
    cur = hm_ref[r0:r0 + BLK, col:col + KV_W]
    if s == 0:
        off = 0 if col == C_K else KV_W
        prev = hprev_ref[:, off:off + KV_W]
    else:
        prev = hm_ref[r0 - BLK:r0, col:col + KV_W]
    return jnp.concatenate([prev, cur], axis=0).astype(F32)


def _h_main_specs(nt, rev):
    def tile(g):
        return nt - 1 - g if rev else g

    return [pl.BlockSpec((TM, MAIN_W), lambda g: (tile(g), 0)),
            pl.BlockSpec((BLK, 2 * KV_W), lambda g: (jnp.maximum(tile(g) * NB - 1, 0), 2))]


_CONST2 = lambda g: (0, 0)
_CONST3 = lambda g: (0, 0, 0)


def _ln_fwd(x, g, b, name):
    t = x.shape[0]

    def body(x_ref, g_ref, b_ref, o_ref):
        xhat, _ = _ln_stats(x_ref[...])
        o_ref[...] = xhat * g_ref[...] + b_ref[...]

    return pl.pallas_call(
        body, name=name, grid=(t // TM_EW,),
        in_specs=[pl.BlockSpec((TM_EW, D), lambda i: (i, 0)), pl.BlockSpec((1, D), _CONST2),
                  pl.BlockSpec((1, D), _CONST2)],
        out_specs=pl.BlockSpec((TM_EW, D), lambda i: (i, 0)),
        out_shape=jax.ShapeDtypeStruct((t, D), F32), compiler_params=_ARB,
    )(x, g.reshape(1, D), b.reshape(1, D))


def _inproj(x, w_t, b, name):
    t = x.shape[0]

    def body(x_ref, wt_ref, b_ref, hm_ref, hr_ref):
        xb = x_ref[...].astype(BF16)
        hm_ref[...] = (_dot_nt(xb, wt_ref[0:MAIN_W, :]) + b_ref[:, 0:MAIN_W]).astype(BF16)
        hr_ref[...] = (_dot_nt(xb, wt_ref[MAIN_W:IN_COLS, :]) + b_ref[:, MAIN_W:IN_COLS]).astype(BF16)

    return pl.pallas_call(
        body, name=name, grid=(t // TM,),
        in_specs=[pl.BlockSpec((TM, D), lambda i: (i, 0)),
                  pl.BlockSpec((IN_COLS, D), _CONST2), pl.BlockSpec((1, IN_COLS), _CONST2)],
        out_specs=[pl.BlockSpec((TM, MAIN_W), lambda i: (i, 0)), pl.BlockSpec((TM, R_W), lambda i: (i, 0))],
        out_shape=[jax.ShapeDtypeStruct((t, MAIN_W), BF16), jax.ShapeDtypeStruct((t, R_W), BF16)],
        compiler_params=_ARB,
    )(x, w_t, b)


def _mixer_fwd(hm, sinks, bias, vn_g, vn_b, w_s, bsb, nblk_seq, name):
    t = hm.shape[0]
    nt = t // TM

    def body(sinks_ref, hm_ref, hprev_ref, bias_ref, vng_ref, vnb_ref, ws_ref, bsb_ref,
             ya_ref, yb_ref, prob_ref, psink_ref):
        i = pl.program_id(0)
        tril = _tril_mask()
        wt = [jnp.where(tril, ws_ref[g], 0.0).astype(BF16) for g in range(4)]
        for s in range(NB):
            r0 = s * BLK
            rows = slice(r0, r0 + BLK)
            bias = bias_ref[jnp.where((i * NB + s) % nblk_seq == 0, 1, 0)]

            def save(hk, prob_t, p_sink, s=s):
                prob_ref[N_KV * s + hk] = prob_t
                psink_ref[N_KV * s + hk] = jnp.broadcast_to(p_sink, (8, 4 * BLK))

            attn = _attn_fwd(_cols(hm_ref, rows, C_Q, Q_W), _band(hm_ref, hprev_ref, s, C_K),
                             _band(hm_ref, hprev_ref, s, C_V), sinks_ref, bias, save)
            g_a = _cols(hm_ref, rows, C_GA, Q_W)
            ya_ref[rows, :] = (attn * (g_a * _sigmoid(g_a))).astype(BF16)
            u = _gelu(_cols(hm_ref, rows, C_UB, SGU_W))
            mixed = _sgu_fwd(u, _gelu(_cols(hm_ref, rows, C_VB, SGU_W)), vng_ref[...], vnb_ref[...], wt, bsb_ref)[-1]
            g_b = _cols(hm_ref, rows, C_GB, SGU_W)
            yb_ref[rows, :] = (u * mixed * (g_b * _sigmoid(g_b))).astype(BF16)

    ngrp = N_KV * NB
    return pl.pallas_call(
        body, name=name, grid=(nt,),
        in_specs=[pl.BlockSpec(memory_space=pltpu.SMEM)] + _h_main_specs(nt, False) + [
            pl.BlockSpec((2, 2 * BLK, 4 * BLK), _CONST3),
            pl.BlockSpec((1, SGU_W), _CONST2), pl.BlockSpec((1, SGU_W), _CONST2),
            pl.BlockSpec((4, BLK, BLK), _CONST3), pl.BlockSpec((4, BLK, BLK), _CONST3)],
        out_specs=[pl.BlockSpec((TM, Q_W), lambda i: (i, 0)), pl.BlockSpec((TM, SGU_W), lambda i: (i, 0)),
                   pl.BlockSpec((ngrp, 2 * BLK, 4 * BLK), lambda i: (i, 0, 0)),
                   pl.BlockSpec((ngrp, 8, 4 * BLK), lambda i: (i, 0, 0))],
        out_shape=[jax.ShapeDtypeStruct((t, Q_W), BF16), jax.ShapeDtypeStruct((t, SGU_W), BF16),
                   jax.ShapeDtypeStruct((nt * ngrp, 2 * BLK, 4 * BLK), BF16),
                   jax.ShapeDtypeStruct((nt * ngrp, 8, 4 * BLK), F32)],
        compiler_params=_ARB,
    )(sinks, hm, hm, bias, vn_g, vn_b, w_s, bsb)


def _tail_fwd(x, ya, yb, hr, pa_w, pb_w, wo, b_out, ln_g, ln_b, name, last):
    t = x.shape[0]

    def body(x_ref, ya_ref, yb_ref, hr_ref, paw_ref, pbw_ref, wo_ref, bo_ref, g_ref, b_ref,
             pa_ref, pb_ref, mg_ref, z_ref, *xn_ref):
        pa = _dot(ya_ref[...], _side_by_side(paw_ref))
        pb = _dot(yb_ref[...], _side_by_side(pbw_ref))
        pa_ref[...] = pa.astype(BF16)
        pb_ref[...] = pb.astype(BF16)
        everything = slice(None)
        merged = _sigmoid(_cols(hr_ref, everything, 0, D)) * pa + _sigmoid(_cols(hr_ref, everything, D, D)) * pb
        mb = merged.astype(BF16)
        mg_ref[...] = mb
        z = ALPHA * x_ref[...] + (_dot(mb, wo_ref[...]) + bo_ref[...])
        z_ref[...] = z
        if not last:
            zhat, _ = _ln_stats(z)
            xn_ref[0][...] = zhat * g_ref[...] + b_ref[...]

    row = lambda w: pl.BlockSpec((TM, w), lambda i: (i, 0))
    vec = pl.BlockSpec((1, D), _CONST2)
    n_f32 = 1 if last else 2
    return pl.pallas_call(
        body, name=name, grid=(t // TM,),
        in_specs=[row(D), row(Q_W), row(SGU_W), row(R_W),
                  pl.BlockSpec((N_DEV, Q_W, 128), _CONST3), pl.BlockSpec((N_DEV, SGU_W, 128), _CONST3),
                  pl.BlockSpec((D, D), _CONST2), vec, vec, vec],
        out_specs=[row(D)] * (3 + n_f32),
        out_shape=[jax.ShapeDtypeStruct((t, D), BF16)] * 3 + [jax.ShapeDtypeStruct((t, D), F32)] * n_f32,
        compiler_params=_ARB,
    )(x, ya, yb, hr, pa_w, pb_w, wo, b_out, ln_g, ln_b)


def _tail_bwd(dxn, z, pa, pb, hr, wo, pa_w, pb_w, ln_g, ln_b, name, from_loss):
    t = dxn.shape[0]

    def body(dxn_ref, z_ref, pa_ref, pb_ref, hr_ref, wo_ref, paw_ref, pbw_ref, g_ref, b_ref,
             dz_ref, dpa_ref, dpb_ref, dhr_ref, dya_ref, dyb_ref, acc_ref, gbr_ref):
        @pl.when(pl.program_id(0) == 0)
        def _():
            acc_ref[...] = jnp.zeros_like(acc_ref)
            gbr_ref[...] = jnp.zeros_like(gbr_ref)

        zhat, rstd = _ln_stats(z_ref[...])
        if from_loss:
            err = zhat * g_ref[...] + b_ref[...] - dxn_ref[...]
            dxn_v = err * (1.0 / D)
            sq = jnp.sum(jnp.sum(err * err, axis=1, keepdims=True), axis=0, keepdims=True)
            acc_ref[3:4, :] += jnp.broadcast_to(sq, (1, D))
        else:
            dxn_v = dxn_ref[...]
        dz = _ln_bwd(dxn_v * g_ref[...], zhat, rstd)
        dz_ref[...] = dz
        acc_ref[0:1, :] += _colsum(dxn_v * zhat)
        acc_ref[1:2, :] += _colsum(dxn_v)
        acc_ref[2:3, :] += _colsum(dz)
        dmerged = _dot_nt(dz.astype(BF16), wo_ref[...])
        everything = slice(None)
        sa = _sigmoid(_cols(hr_ref, everything, 0, D))
        sb = _sigmoid(_cols(hr_ref, everything, D, D))
        dpa = (dmerged * sa).astype(BF16)
        dpb = (dmerged * sb).astype(BF16)
        dpa_ref[...] = dpa
        dpb_ref[...] = dpb
        dra = dmerged * pa_ref[...].astype(F32) * (sa * (1.0 - sa))
        drb = dmerged * pb_ref[...].astype(F32) * (sb * (1.0 - sb))
        dhr_ref[:, 0:D] = dra.astype(BF16)
        dhr_ref[:, D:2 * D] = drb.astype(BF16)
        gbr_ref[0:1, 0:D] += _colsum(dra)
        gbr_ref[0:1, D:2 * D] += _colsum(drb)
        dya_ref[...] = _dot_nt(dpa, _side_by_side(paw_ref)).astype(BF16)
        dyb_ref[...] = _dot_nt(dpb, _side_by_side(pbw_ref)).astype(BF16)

    row = lambda w: pl.BlockSpec((TM, w), lambda i: (i, 0))
    vec = pl.BlockSpec((1, D), _CONST2)
    return pl.pallas_call(
        body, name=name, grid=(t // TM,),
        in_specs=[row(D), row(D), row(D), row(D), row(R_W),
                  pl.BlockSpec((D, D), _CONST2), pl.BlockSpec((N_DEV, Q_W, 128), _CONST3),
                  pl.BlockSpec((N_DEV, SGU_W, 128), _CONST3), vec, vec],
        out_specs=[row(D), row(D), row(D), row(R_W), row(Q_W), row(SGU_W), pl.BlockSpec((8, D), _CONST2),
                   pl.BlockSpec((8, R_W), _CONST2)],
        out_shape=[jax.ShapeDtypeStruct((t, D), F32), jax.ShapeDtypeStruct((t, D), BF16),
                   jax.ShapeDtypeStruct((t, D), BF16), jax.ShapeDtypeStruct((t, R_W), BF16),
                   jax.ShapeDtypeStruct((t, Q_W), BF16), jax.ShapeDtypeStruct((t, SGU_W), BF16),
                   jax.ShapeDtypeStruct((8, D), F32), jax.ShapeDtypeStruct((8, R_W), F32)],
        compiler_params=_ARB,
    )(dxn, z, pa, pb, hr, wo, pa_w, pb_w, ln_g, ln_b)


def _mixer_bwd(hm, dya, dyb, prob, psink, vn_g, vn_b, w_s, bsb, name):
    t = hm.shape[0]
    nt = t // TM
    ngrp = N_KV * NB

    def body(hm_ref, hprev_ref, prob_ref, psink_ref, dya_ref, dyb_ref, vng_ref, vnb_ref, ws_ref, bsb_ref,
             dhm_ref, gbm_ref, gsk_ref, gvn_ref, gws_ref, gbs_ref, dk_carry, dv_carry):
        gi = pl.program_id(0)

        @pl.when(gi == 0)
        def _():
            for r in (gbm_ref, gsk_ref, gvn_ref, gws_ref, gbs_ref, dk_carry, dv_carry):
                r[...] = jnp.zeros_like(r)

        tril = _tril_mask()
        wt = [jnp.where(tril, ws_ref[g], 0.0).astype(BF16) for g in range(4)]
        vng = vng_ref[...]
        ones8 = jnp.ones((8, BLK), BF16)

        def put(rows, col, val):
            dhm_ref[rows, col:col + val.shape[1]] = val.astype(BF16)

        for s in reversed(range(NB)):
            r0 = s * BLK
            rows = slice(r0, r0 + BLK)
            q = _cols(hm_ref, rows, C_Q, Q_W)
            kband = _band(hm_ref, hprev_ref, s, C_K)
            vband = _band(hm_ref, hprev_ref, s, C_V)
            g_a = _cols(hm_ref, rows, C_GA, Q_W)
            sg = _sigmoid(g_a)
            dya_v = _cols(dya_ref, rows, 0, Q_W)
            d_o = dya_v * (g_a * sg)
            o_pairs, dq_pairs = [None] * 4, [None] * 4
            dkband = jnp.zeros((2 * BLK, KV_W), F32)
            dvband = jnp.zeros((2 * BLK, KV_W), F32)
            kband_t, vband_t = kband.T, vband.T
            for hk in range(N_KV):
                q4 = _stack_q(q, hk)
                prob_b = prob_ref[N_KV * s + hk]
                p_sink = psink_ref[N_KV * s + hk][0:1, :]
                o_t = _dot(_head_rows(vband_t, hk), prob_b)
                o_pairs = _unstack_heads(o_t.T, hk, o_pairs)
                parts = []
                for g in range(4):
                    _, p, pos = _head_place(hk, g)
                    dp = d_o[:, BLK * p:BLK * (p + 1)]
                    parts.append(pltpu.roll(dp, 64, 1) if pos != hk else dp)
                do4 = _head_lanes(jnp.concatenate(parts, axis=0), hk)
                do4b = do4.astype(BF16)
                delta = _colsum(do4.T * o_t)
                vh = _head_lanes(vband, hk).astype(BF16)
                ds_t = prob_b.astype(F32) * (_dot_nt(vh, do4b) - delta)
                dsb = ds_t.astype(BF16)
                dq4_t = _dot(_head_rows(kband_t, hk), dsb)
                dq_pairs = _unstack_heads(dq4_t.T * ATTN_SCALE, hk, dq_pairs)
                dkband = dkband + _head_lanes(_dot(dsb, q4), hk)
                dvband = dvband + _dot(prob_b, do4b)
                dsk = p_sink * delta
                for g in range(4):
                    j = 4 * hk + g
                    tot = jnp.sum(dsk[:, BLK * g:BLK * (g + 1)], axis=1, keepdims=True)
                    gsk_ref[j:j + 1, :] += jnp.broadcast_to(-tot, (1, 128))
            attn = jnp.concatenate(o_pairs, axis=1)
            put(rows, C_Q, jnp.concatenate(dq_pairs, axis=1))
            put(rows, C_K, dkband[BLK:2 * BLK] + dk_carry[...])
            put(rows, C_V, dvband[BLK:2 * BLK] + dv_carry[...])
            dk_carry[...] = dkband[0:BLK]
            dv_carry[...] = dvband[0:BLK]
            put(rows, C_GA, dya_v * attn * (sg * (1.0 + g_a * (1.0 - sg))))
            u, du_du_b = _gelu_and_grad(_cols(hm_ref, rows, C_UB, SGU_W))
            v, dv_dv_b = _gelu_and_grad(_cols(hm_ref, rows, C_VB, SGU_W))
            g_b = _cols(hm_ref, rows, C_GB, SGU_W)
            vhat, rstd, vn, mixed = _sgu_fwd(u, v, vng, vnb_ref[...], wt, bsb_ref)
            sgb = _sigmoid(g_b)
            silu_b = g_b * sgb
            dyb_v = _cols(dyb_ref, rows, 0, SGU_W)
            du = dyb_v * mixed * silu_b
            dmixed = dyb_v * u * silu_b
            put(rows, C_GB, dyb_v * u * mixed * (sgb * (1.0 + g_b * (1.0 - sgb))))
            dvn_parts = []
            for g in range(4):
                cols = slice(BLK * g, BLK * (g + 1))
                dmg = dmixed[:, cols]
                dmgb = dmg.astype(BF16)
                dvn_parts.append(_dot_tn(wt[g], dmgb))
                gws_ref[g] += jnp.where(tril, _dot_nt(dmgb, vn[:, cols].astype(BF16)), 0.0)
                gbs_ref[g] += dmg
            dvn = jnp.concatenate(dvn_parts, axis=1)
            gvn_ref[0:1, :] += _colsum(dvn * vhat)
            gvn_ref[1:2, :] += _colsum(dvn)
            dv = _ln_bwd(dvn * vng, vhat, rstd)
            put(rows, C_UB, du * du_du_b)
            put(rows, C_VB, dv * dv_dv_b)
            gbm_ref[...] += _dot(ones8, dhm_ref[rows, :])

        @pl.when(gi == nt - 1)
        def _():
            for g in range(4):
                gbs_ref[g] = jnp.broadcast_to(jnp.sum(gbs_ref[g], axis=1, keepdims=True), (BLK, BLK))

    row = lambda w: pl.BlockSpec((TM, w), lambda g: (nt - 1 - g, 0))
    return pl.pallas_call(
        body, name=name, grid=(nt,),
        in_specs=_h_main_specs(nt, True) + [
            pl.BlockSpec((ngrp, 2 * BLK, 4 * BLK), lambda g: (nt - 1 - g, 0, 0)),
            pl.BlockSpec((ngrp, 8, 4 * BLK), lambda g: (nt - 1 - g, 0, 0)),
            row(Q_W), row(SGU_W),
            pl.BlockSpec((1, SGU_W), _CONST2), pl.BlockSpec((1, SGU_W), _CONST2),
            pl.BlockSpec((4, BLK, BLK), _CONST3), pl.BlockSpec((4, BLK, BLK), _CONST3)],
        out_specs=[row(MAIN_W), pl.BlockSpec((8, MAIN_W), _CONST2), pl.BlockSpec((8, 128), _CONST2),
                   pl.BlockSpec((8, SGU_W), _CONST2), pl.BlockSpec((4, BLK, BLK), _CONST3),
                   pl.BlockSpec((4, BLK, BLK), _CONST3)],
        out_shape=[jax.ShapeDtypeStruct((t, MAIN_W), BF16), jax.ShapeDtypeStruct((8, MAIN_W), F32),
                   jax.ShapeDtypeStruct((8, 128), F32), jax.ShapeDtypeStruct((8, SGU_W), F32),
                   jax.ShapeDtypeStruct((4, BLK, BLK), F32), jax.ShapeDtypeStruct((4, BLK, BLK), F32)],
        scratch_shapes=[pltpu.VMEM((BLK, KV_W), F32), pltpu.VMEM((BLK, KV_W), F32)],
        compiler_params=_ARB,
    )(hm, hm, prob, psink, dya, dyb, vn_g, vn_b, w_s, bsb)


def _dx_inproj(dz, dhm, dhr, w_t, after, name, ln_in=None):
    t = dz.shape[0]

    def body(dz_ref, dhm_ref, dhr_ref, wt_ref, after_ref, *rest):
        dx = (ALPHA * dz_ref[...] + after_ref[0:1, 0:1] + _dot(dhm_ref[...], wt_ref[0:MAIN_W, :])
              + _dot(dhr_ref[...], wt_ref[MAIN_W:IN_COLS, :]))
        if ln_in is None:
            rest[0][...] = dx
            return
        x_ref, g_ref, gx_ref, acc_ref = rest

        @pl.when(pl.program_id(0) == 0)
        def _():
            acc_ref[...] = jnp.zeros_like(acc_ref)

        xhat, rstd = _ln_stats(x_ref[...])
        gx_ref[...] = _ln_bwd(dx * g_ref[...], xhat, rstd)
        acc_ref[0:1, :] += _colsum(dx * xhat)
        acc_ref[1:2, :] += _colsum(dx)

    row = lambda w: pl.BlockSpec((TM, w), lambda i: (i, 0))
    in_specs = [row(D), row(MAIN_W), row(R_W), pl.BlockSpec((IN_COLS, D), _CONST2), pl.BlockSpec((8, 128), _CONST2)]
    if ln_in is None:
        return pl.pallas_call(
            body, name=name, grid=(t // TM,), in_specs=in_specs,
            out_specs=row(D), out_shape=jax.ShapeDtypeStruct((t, D), F32), compiler_params=_ARB,
        )(dz, dhm, dhr, w_t, after)
    return pl.pallas_call(
        body, name=name, grid=(t // TM,), in_specs=in_specs + [row(D), pl.BlockSpec((1, D), _CONST2)],
        out_specs=[row(D), pl.BlockSpec((8, D), _CONST2)],
        out_shape=[jax.ShapeDtypeStruct((t, D), F32), jax.ShapeDtypeStruct((8, D), F32)], compiler_params=_ARB,
    )(dz, dhm, dhr, w_t, after, *ln_in)


def _wgrad(a, b, tm, name, rows=None, under=None, by_owner=False):
    t, m = a.shape
    n = b.shape[1]
    tk = min(t, 2048)
    nk = t // tk

    def body(a_ref, b_ref, *rest):
        o_ref, acc_ref = rest[-2:]
        k = pl.program_id(1)

        @pl.when(k == 0)
        def _():
            acc_ref[...] = jnp.zeros_like(acc_ref)

        acc_ref[...] += _dot_tn(a_ref[...].astype(BF16), b_ref[...].astype(BF16))

        @pl.when(k == nk - 1)
        def _():
            if by_owner:
                for j in range(N_DEV):
                    o_ref[j] = acc_ref[:, j * (n // N_DEV):(j + 1) * (n // N_DEV)].astype(BF16)
            else:
                o_ref[...] = acc_ref[...].astype(BF16)

    in_specs = [pl.BlockSpec((tk, tm), lambda j, k: (k, j)), pl.BlockSpec((tk, n), lambda j, k: (k, 0))]
    if by_owner:
        return pl.pallas_call(
            body, name=name, grid=(m // tm, nk), in_specs=in_specs,
            out_specs=pl.BlockSpec((N_DEV, tm, n // N_DEV), lambda j, k: (0, j, 0)),
            out_shape=jax.ShapeDtypeStruct((N_DEV, m, n // N_DEV), BF16), scratch_shapes=[pltpu.VMEM((tm, n), F32)],
            compiler_params=pltpu.CompilerParams(dimension_semantics=("arbitrary", "arbitrary"),
                                                 vmem_limit_bytes=VMEM_LIMIT),
        )(a, b)
    if under is None:
        out_rows, out_spec, operands, aliases = rows or m, pl.BlockSpec((tm, n), lambda j, k: (j, 0)), (a, b), {}
    else:
        out_rows = under.shape[0]
        first = out_rows - m
        assert first % 128 == 0 and tm % 128 == 0
        out_spec = pl.BlockSpec((pl.Element(tm), pl.Element(n)),
                                lambda j, k: (pl.multiple_of(first + j * tm, 128), 0))
        in_specs, operands, aliases = in_specs + [_ANY], (a, b, under), {2: 0}
    return pl.pallas_call(
        body, name=name, grid=(m // tm, nk), in_specs=in_specs, out_specs=out_spec,
        out_shape=jax.ShapeDtypeStruct((out_rows, n), BF16), input_output_aliases=aliases,
        scratch_shapes=[pltpu.VMEM((tm, n), F32)],
        compiler_params=pltpu.CompilerParams(dimension_semantics=("arbitrary", "arbitrary"), vmem_limit_bytes=VMEM_LIMIT),
    )(*operands)


_ANY = pl.BlockSpec(memory_space=pl.ANY)


def _place():
    return lax.axis_index("x"), lax.axis_index("y"), lax.axis_index("c")


def _forward_sibling(lands, name):
    n = len(lands)

    def body(*refs):
        l_refs = refs[n:2 * n]
        send_sems, recv_sems = refs[2 * n:]
        x, y, c = _place()
        chips = [(1 - x, y), (x, 1 - y), (1 - x, 1 - y)]

        def copy(a, j, core):
            rows = l_refs[a].at[4 * chips[j][0] + 2 * chips[j][1] + core]
            return pltpu.make_async_remote_copy(
                src_ref=rows, dst_ref=rows, send_sem=send_sems.at[3 * a + j], recv_sem=recv_sems.at[3 * a + j],
                device_id=(x, y, 1 - c), device_id_type=MESH)

        for a in range(n):
            for j in range(3):
                copy(a, j, c).start()
        for a in range(n):
            for j in range(3):
                copy(a, j, 1 - c).wait_recv()
                copy(a, j, c).wait_send()

    return pl.pallas_call(
        body, name=name, in_specs=[_ANY] * n, out_specs=[_ANY] * n,
        out_shape=[jax.ShapeDtypeStruct(v.shape, v.dtype) for v in lands],
        input_output_aliases={a: a for a in range(n)},
        scratch_shapes=[pltpu.SemaphoreType.DMA((3 * n,)), pltpu.SemaphoreType.DMA((3 * n,))],
    )(*lands)


def _swap_sibling(gs, name):
    n = len(gs)
    first = [0]
    for v in gs:
        first.append(first[-1] + v.shape[0])

    def body(*refs):
        g_refs, r_refs = refs[:n], refs[n:2 * n]
        send_sems, recv_sems = refs[2 * n:]
        x, y, c = _place()
        cps = [pltpu.make_async_remote_copy(
            src_ref=g_refs[a].at[q, 1 - c], dst_ref=r_refs[a].at[q], send_sem=send_sems.at[first[a] + q],
            recv_sem=recv_sems.at[first[a] + q], device_id=(x, y, 1 - c), device_id_type=MESH)
            for a in range(n) for q in range(gs[a].shape[0])]
        for cp in cps:
            cp.start()
        for cp in cps:
            cp.wait()

    return pl.pallas_call(
        body, name=name, in_specs=[_ANY] * n, out_specs=[_ANY] * n,
        out_shape=[jax.ShapeDtypeStruct(v.shape[:1] + v.shape[2:], v.dtype) for v in gs],
        scratch_shapes=[pltpu.SemaphoreType.DMA((first[-1],)), pltpu.SemaphoreType.DMA((first[-1],))],
    )(*gs)


def _row_tile(rows, lanes, cap):
    if rows * lanes * 4 <= (1 << 20):
        return rows
    return max(d for d in range(8, cap + 1, 8) if rows % d == 0 and (d % 16 == 0 or rows % 16 != 0))


def _pair_sums(gs, rs, name):
    n = len(gs)

    def add(g, r, dtype):
        return (g.astype(F32) + r.astype(F32)).astype(dtype)

    def body(c_ref, *refs):
        g_refs, r_refs, o_refs = refs[:n], refs[n:2 * n], refs[2 * n:]
        o_refs[0][...] = add(g_refs[0][0], r_refs[0][...], o_refs[0].dtype)

        @pl.when(pl.program_id(0) == 0)
        def _():
            for a in range(1, n):
                o_refs[a][...] = add(g_refs[a][:, 0], r_refs[a][...], o_refs[a].dtype)

    def whole(shape, mine):
        if mine:
            return pl.BlockSpec(shape, lambda q, c_ref: (0, c_ref[0]) + (0,) * (len(shape) - 2))
        return pl.BlockSpec(shape, lambda q, c_ref: (0,) * len(shape))

    big = gs[0].shape
    return pl.pallas_call(
        body, name=name,
        grid_spec=pltpu.PrefetchScalarGridSpec(
            num_scalar_prefetch=1, grid=(big[0],),
            in_specs=[pl.BlockSpec((1, 1) + big[2:], lambda q, c_ref: (q, c_ref[0], 0, 0))]
            + [whole(g.shape[:1] + (1,) + g.shape[2:], True) for g in gs[1:]]
            + [pl.BlockSpec((1,) + big[2:], lambda q, c_ref: (q, 0, 0))]
            + [whole(r.shape, False) for r in rs[1:]],
            out_specs=[pl.BlockSpec((1,) + big[2:], lambda q, c_ref: (q, 0, 0))]
            + [whole(r.shape, False) for r in rs[1:]]),
        out_shape=[jax.ShapeDtypeStruct(r.shape, g.dtype) for g, r in zip(gs, rs)],
        compiler_params=_ARB,
    )(lax.axis_index("c").astype(jnp.int32).reshape(1), *gs, *rs)


def _adamw(parts, w, m, v, name, own=None):
    nl = len(parts)
    ns, rows, l = parts[0].shape
    tr = _row_tile(rows, l * ns, 304)
    nt = rows // tr
    c1 = 1.0 - ADAM_B1 ** ADAM_STEP
    c2 = 1.0 - ADAM_B2 ** ADAM_STEP

    def body(q_ref, *refs):
        own_refs = refs[:nl] if own is not None else None
        p_refs = refs[-7 - nl:-7]
        w_ref, m_ref, v_ref, g_ref, d_ref, nm_ref, nv_ref = refs[-7:]
        layer = pl.program_id(0)
        g = None
        for j in range(nl):
            gj = None
            for k in range(ns):
                term = p_refs[j][k].astype(F32)
                if own_refs is not None:
                    term = jnp.where(q_ref[0] == k, own_refs[j][0].astype(F32), term)
                gj = term if gj is None else gj + term
            g = gj if g is None else jnp.where(layer == j, gj, g)
        g_ref[...] = g
        nm = ADAM_B1 * m_ref[...] + (1.0 - ADAM_B1) * g
        nv = ADAM_B2 * v_ref[...] + (1.0 - ADAM_B2) * (g * g)
        nm_ref[...] = nm
        nv_ref[...] = nv
        d_ref[...] = -ADAM_LR * ((nm / c1) / (jnp.sqrt(nv / c2) + ADAM_EPS) + ADAM_WD * w_ref[...])

    def tile_of(j):
        return lambda la, i, q: jnp.where(la == j, i, jnp.where(la < j, 0, nt - 1))

    row = pl.BlockSpec((tr, l), lambda la, i, q: (la * nt + i, 0))
    own_specs = [] if own is None else [
        pl.BlockSpec((1, tr, l), lambda la, i, q, j=j: (q[0], tile_of(j)(la, i, q), 0)) for j in range(nl)]
    part_specs = [pl.BlockSpec((ns, tr, l), lambda la, i, q, j=j: (0, tile_of(j)(la, i, q), 0)) for j in range(nl)]
    chip = (2 * lax.axis_index("x") + lax.axis_index("y")).astype(jnp.int32).reshape(1)
    return pl.pallas_call(
        body, name=name,
        grid_spec=pltpu.PrefetchScalarGridSpec(
            num_scalar_prefetch=1, grid=(nl, nt),
            in_specs=own_specs + part_specs + [row, row, row], out_specs=[row] * 4),
        out_shape=[jax.ShapeDtypeStruct((nl * rows, l), F32)] * 4,
        compiler_params=pltpu.CompilerParams(dimension_semantics=("arbitrary", "arbitrary"), vmem_limit_bytes=VMEM_LIMIT),
    )(chip, *([] if own is None else own), *parts, w, m, v)


_HBM = pl.BlockSpec(memory_space=pltpu.HBM)
_SEM = pl.BlockSpec(memory_space=pltpu.SEMAPHORE)
_EFFECT = pltpu.SideEffectType.DATAFLOW_SIDE_EFFECTING


def _plan_all(x, y, c):
    me = 4 * x + 2 * y + c
    peers = [(x, y, c), (x, y, 1 - c), (1 - x, y, c), (x, 1 - y, c), (1 - x, 1 - y, c),
             (1 - x, y, 1 - c), (x, 1 - y, 1 - c), (1 - x, 1 - y, 1 - c)]
    return [(None, me, p, 4 * p[0] + 2 * p[1] + p[2]) for p in peers]


def _plan_near(x, y, c):
    me = 4 * x + 2 * y + c
    peers = [(x, y, 1 - c), (1 - x, y, c), (x, 1 - y, c), (1 - x, 1 - y, c)]
    return [(None, me, p, 4 * p[0] + 2 * p[1] + p[2]) for p in peers]


def _plan_sibling(x, y, c):
    return [(2 * q + 1 - c, q, (x, y, 1 - c), q) for q in range(4)]


def _plan_chips(x, y, c):
    me = 2 * x + y
    return [(2 * qx + qy, me, (qx, qy, c), 2 * qx + qy) for qx, qy in ((1 - x, y), (x, 1 - y), (1 - x, 1 - y))]


def _split_copies(plan, src_refs, land_refs, send_sems, recv_sems, arrival):
    n = len(src_refs)
    entries = plan(*_place())
    per = len(entries)
    cps = []
    for a in range(n):
        for k, (src_slot, dst_slot, peer, back_slot) in enumerate(entries):
            src = src_refs[a] if src_slot is None else src_refs[a].at[src_slot if src_refs[a].shape[0] > 1 else 0]
            cps.append(pltpu.make_async_remote_copy(
                src_ref=src, dst_ref=land_refs[a].at[back_slot if arrival else dst_slot],
                send_sem=send_sems.at[per * a + k], recv_sem=recv_sems.at[per * a + k],
                device_id=peer, device_id_type=MESH))
    return cps


def _split_start(srcs, lands, plan, per, name):
    n = len(srcs)

    def body(*refs):
        for cp in _split_copies(plan, refs[:n], refs[n:2 * n], refs[2 * n], refs[2 * n + 1], False):
            cp.start()
        refs[-1][...] = jnp.zeros_like(refs[-1])

    both = list(srcs) + list(lands)
    outs = pl.pallas_call(
        body, name=name,
        out_shape=(pltpu.SemaphoreType.DMA((per * n,)), pltpu.SemaphoreType.DMA((per * n,)),
                   *[pltpu.HBM(v.shape, v.dtype) for v in both], jax.ShapeDtypeStruct((8, 128), F32)),
        in_specs=[_HBM] * (2 * n),
        out_specs=(_SEM, _SEM, *[_HBM] * (2 * n), pl.BlockSpec(memory_space=pltpu.VMEM)),
        input_output_aliases={i: 2 + i for i in range(2 * n)},
        compiler_params=pltpu.CompilerParams(has_side_effects=_EFFECT),
    )(*[pltpu.with_memory_space_constraint(v, pltpu.HBM) for v in both])
    return outs[0], outs[1], list(outs[2:2 + 2 * n]), outs[-1]


def _split_wait(send_sems, recv_sems, thru, plan, after, name):
    n = len(thru) // 2

    def body(*refs):
        for cp in _split_copies(plan, refs[:n], refs[n:2 * n], refs[2 * n], refs[2 * n + 1], True):
            cp.wait_send()
            cp.wait_recv()

    outs = pl.pallas_call(
        body, name=name, out_shape=tuple(pltpu.HBM(v.shape, v.dtype) for v in thru),
        in_specs=[_HBM] * (2 * n) + [_SEM, _SEM, pl.BlockSpec(memory_space=pl.ANY)],
        out_specs=[_HBM] * (2 * n), input_output_aliases={i: i for i in range(2 * n)},
        compiler_params=pltpu.CompilerParams(has_side_effects=_EFFECT),
    )(*thru, send_sems, recv_sems, after)
    return list(outs[:n]), list(outs[n:])


_SMALL_IN = ("ln_in_g", "ln_in_b")
_SMALL_ROWS = ("w_s", "b_s", "sinks")
_SMALL_LANES = ("b_in", "vn_g", "vn_b", "b_out", "ln_g", "ln_b")


def _tile_rows(a):
    return -(-a.size // 1024) * 8


def _pack_small(d, names):
    return jnp.concatenate([jnp.pad(d[n].reshape(-1), (0, (-d[n].size) % 1024)).reshape(-1, 128) for n in names])


def _adamw_small(parts, own, w, m, v, pieces, axis, name):
    c1 = 1.0 - ADAM_B1 ** ADAM_STEP
    c2 = 1.0 - ADAM_B2 ** ADAM_STEP
    shapes = [tuple(p if d == axis else s for d, s in enumerate(w.shape)) for p in pieces]

    def body(q_ref, own_ref, p_ref, w_ref, m_ref, v_ref, *o_refs):
        g = None
        for k in range(4):
            term = jnp.where(q_ref[0] == k, own_ref[0], p_ref[k])
            g = term if g is None else g + term
        nm = ADAM_B1 * m_ref[...] + (1.0 - ADAM_B1) * g
        nv = ADAM_B2 * v_ref[...] + (1.0 - ADAM_B2) * (g * g)
        delta = -ADAM_LR * ((nm / c1) / (jnp.sqrt(nv / c2) + ADAM_EPS) + ADAM_WD * w_ref[...])
        for k, val in enumerate((g, delta, nm, nv)):
            off = 0
            for j, p in enumerate(pieces):
                o_refs[k * len(pieces) + j][...] = val[off:off + p] if axis == 0 else val[:, off:off + p]
                off += p

    whole = pl.BlockSpec(w.shape, lambda i, q: (0, 0))
    chip = (2 * lax.axis_index("x") + lax.axis_index("y")).astype(jnp.int32).reshape(1)
    outs = pl.pallas_call(
        body, name=name,
        grid_spec=pltpu.PrefetchScalarGridSpec(
            num_scalar_prefetch=1, grid=(1,),
            in_specs=[pl.BlockSpec((1,) + w.shape, lambda i, q: (q[0] if own.shape[0] > 1 else 0, 0, 0)),
                      pl.BlockSpec((4,) + w.shape, lambda i, q: (0, 0, 0)), whole, whole, whole],
            out_specs=[pl.BlockSpec(s, lambda i, q: (0, 0)) for s in shapes] * 4),
        out_shape=[jax.ShapeDtypeStruct(s, F32) for s in shapes] * 4, compiler_params=_ARB,
    )(chip, own, parts, w, m, v)
    return [outs[k * len(pieces):(k + 1) * len(pieces)] for k in range(4)]


def kernel(x, ln_in_g, ln_in_b, w_in, b_in, sinks, vn_g, vn_b, w_s, b_s, p_a, p_b, w_out, b_out, ln_g, ln_b, loss_target, m_ln_in_g, m_ln_in_b, m_w_in, m_b_in, m_sinks, m_vn_g, m_vn_b, m_w_s, m_b_s, m_p_a, m_p_b, m_w_out, m_b_out, m_ln_g, m_ln_b, v_ln_in_g, v_ln_in_b, v_w_in, v_b_in, v_sinks, v_vn_g, v_vn_b, v_w_s, v_b_s, v_p_a, v_p_b, v_w_out, v_b_out, v_ln_g, v_ln_b):
    nseq, seq, _ = x.shape
    t = nseq * seq
    nblk_seq = seq // BLK
    x2 = x.reshape(t, D)
    tgt = loss_target.reshape(t, D)

    def turned(a):
        return jnp.swapaxes(a, 1, 2)

    w_in_t = turned(w_in)

    def blocks(l):
        return [w_in_t[l].astype(BF16), p_a[l].astype(BF16), p_b[l].astype(BF16), w_out[l].astype(BF16)]

    def full_weights(g):
        w_t_full = g[0].reshape(IN_COLS, D)
        wo_full = g[3].reshape(D, D)
        return dict(w_t=w_t_full, pa=g[1], pb=g[2], wo=wo_full)

    def landing(bs):
        return [lax.empty((N_DEV,) + v.shape, v.dtype) for v in bs]

    def with_own(landed, sent):
        return [lax.dynamic_update_index_in_dim(g, b, me, 0) for g, b in zip(landed, sent)]

    me = 4 * lax.axis_index("x") + 2 * lax.axis_index("y") + lax.axis_index("c")
    blocks0 = blocks(0)
    a_send, a_recv, a_thru, a_token = _split_start(blocks0[:1], landing(blocks0[:1]), _plan_near, 4,
                                                   "allgather_w_in0_start")
    rest0 = [b + a_token[0, 0].astype(BF16) for b in blocks0[1:]]
    b_send, b_recv, b_thru, b_token = _split_start(rest0, landing(rest0), _plan_all, 8, "allgather_rest0_start")
    xs = [_ln_fwd(x2, ln_in_g + b_token[0, 0], ln_in_b, "ln_in_fwd")]
    sent, landed = _split_wait(a_send, a_recv, a_thru, _plan_near, xs[0], "allgather_w_in0_wait")
    gathered0 = with_own(_forward_sibling(landed, "allgather_w_in0_forward"), sent)
    blocks1, gathered0 = lax.optimization_barrier((blocks(1), gathered0))
    ag_send, ag_recv, ag_thru, ag_token = _split_start(blocks1, landing(blocks1), _plan_all, 8,
                                                       "allgather_weights1_start")
    weights = [None, None]
    bsb = jnp.broadcast_to(b_s[:, :, :, None], (DEPTH, 4, BLK, BLK))
    bias = _band_bias()

    saved = []
    for l in range(DEPTH):
        if l == 1:
            _, landed = _split_wait(ag_send, ag_recv, ag_thru, _plan_all, xs[1], "allgather_weights1_wait")
            weights[1] = full_weights(landed)
        w_t = weights[l]["w_t"] if l else gathered0[0].reshape(IN_COLS, D)
        last = l == DEPTH - 1
        b_l = b_in[l].reshape(1, -1) + (ag_token[0, 0] if l == 0 else 0.0)
        hm, hr = _inproj(xs[l], w_t, b_l, f"inproj{l}")
        ya, yb, prob, psink = _mixer_fwd(hm, sinks[l], bias, vn_g[l].reshape(1, -1), vn_b[l].reshape(1, -1),
                                         w_s[l], bsb[l], nblk_seq, f"mixer_fwd{l}")
        if l == 0:
            _, landed = _split_wait(b_send, b_recv, b_thru, _plan_all, ya, "allgather_rest0_wait")
            weights[0] = full_weights(gathered0 + landed)
        wl = weights[l]
        outs = _tail_fwd(xs[l], ya, yb, hr, wl["pa"], wl["pb"], wl["wo"], b_out[l].reshape(1, D),
                         ln_g[l].reshape(1, D), ln_b[l].reshape(1, D), f"tail_fwd{l}", last)
        saved.append((hm, hr, ya, yb, prob, psink) + tuple(outs[:4]))
        if not last:
            xs.append(outs[4])

    small = {n: [None] * DEPTH for n in _SMALL_ROWS + _SMALL_LANES}

    def pack_rows(d):
        return _pack_small(d, _SMALL_ROWS)

    def pack_lanes(d):
        return jnp.concatenate([d[n] for n in _SMALL_LANES], axis=1)
    names = ("w_in", "p_a", "p_b", "w_out")
    token = jnp.zeros((8, 128), F32)
    dx = tgt
    split = [None] * DEPTH
    for l in reversed(range(DEPTH)):
        hm, hr, ya, yb, prob, psink, pa, pb, merged, z = saved[l]
        wl = weights[l]
        dz, dpa, dpb, dhr, dya, dyb, acc, gbr = _tail_bwd(
            dx, z, pa, pb, hr, wl["wo"], wl["pa"], wl["pb"], ln_g[l].reshape(1, D) + token[0, 0],
            ln_b[l].reshape(1, D), f"tail_bwd{l}", l == DEPTH - 1)
        if l == DEPTH - 1:
            sq_err = acc[3:4, 0:128]
        dhm, gbm, gsk, gvn, gws, gbs = _mixer_bwd(
            hm, dya, dyb, prob, psink, vn_g[l].reshape(1, -1), vn_b[l].reshape(1, -1), w_s[l], bsb[l],
            f"mixer_bwd{l}")
        grads = {"w_in": _wgrad(dhr, xs[l], R_W // 2, f"wgrad_in_route{l}",
                                under=_wgrad(dhm, xs[l], MAIN_W // 2, f"wgrad_in_main{l}", rows=IN_COLS)),
                 "p_a": _wgrad(ya, dpa, Q_W, f"wgrad_pa{l}", by_owner=True),
                 "p_b": _wgrad(yb, dpb, SGU_W, f"wgrad_pb{l}", by_owner=True),
                 "w_out": _wgrad(merged, dz, D, f"wgrad_out{l}")}
        small["b_in"][l] = jnp.concatenate([gbm[0], gbr[0]])
        small["sinks"][l] = gsk[:, 0]
        small["vn_g"][l], small["vn_b"][l] = gvn[0], gvn[1]
        small["w_s"][l], small["b_s"][l] = gws, gbs[:, :, 0]
        small["ln_g"][l], small["ln_b"][l], small["b_out"][l] = acc[0], acc[1], acc[2]
        parts = [grads[n].reshape((4, 2, -1, grads[n].shape[-1])) for n in names]
        if l == 0:
            stacked = {n: jnp.stack(v) for n, v in small.items()}
            for packed in (pack_rows(stacked), pack_lanes(stacked)):
                parts.append(jnp.broadcast_to(packed[None, None], (1, 2) + packed.shape))
        if l == 0:
            from_sib = _swap_sibling(parts, f"rs_sibling{l}")
        else:
            halves = [p.reshape((N_DEV,) + p.shape[2:]) for p in parts]
            sib = _split_start(halves, [lax.empty((4,) + p.shape[2:], p.dtype) for p in parts], _plan_sibling, 4,
                               f"rs_sibling{l}_start")
            dx = _dx_inproj(dz, dhm, dhr, wl["w_t"], sib[3], f"dx_inproj{l}")
            halves, from_sib = _split_wait(sib[0], sib[1], sib[2], _plan_sibling, dx, f"rs_sibling{l}_wait")
            parts = [h.reshape(p.shape) for h, p in zip(halves, parts)]
        pair = list(_pair_sums(parts, from_sib, f"pair_sums{l}"))
        lands = [lax.empty((4,) + p.shape[1:], p.dtype) for p in pair]
        split[l] = _split_start(pair, lands, _plan_chips, 3, f"rs_chips{l}_start")
        token = split[l][3]
        if l == 0:
            grad_x, acc_in = _dx_inproj(dz, dhm, dhr, wl["w_t"], token, f"dx_inproj{l}",
                                        ln_in=(x2, ln_in_g.reshape(1, D)))
    last = [acc_in, jnp.broadcast_to(sq_err, (8, 128))]
    ln_send, ln_recv, ln_thru, ln_token = _split_start(last, landing(last), _plan_all, 8, "allgather_ln_in_start")

    given = {"w_in": (w_in_t, turned(m_w_in), turned(v_w_in)), "p_a": (p_a, m_p_a, v_p_a),
             "p_b": (p_b, m_p_b, v_p_b), "w_out": (w_out, m_w_out, v_w_out)}
    waited = [_split_wait(split[l][0], split[l][1], split[l][2], _plan_chips, ln_token, f"rs_chips{l}_wait")
              for l in range(DEPTH)]
    res = {}
    for a, n in enumerate(names):
        rows, lanes = waited[0][1][a].shape[1:]
        outs = _adamw([waited[l][1][a] for l in range(DEPTH)], *[v.reshape(DEPTH * rows, lanes) for v in given[n]],
                      f"adamw_{n}", own=[waited[l][0][a] for l in range(DEPTH)])
        res[n] = [o.reshape(given[n][0].shape) for o in outs]
    res["w_in"] = [turned(o) for o in res["w_in"]]

    w_small = dict(ln_in_g=ln_in_g, ln_in_b=ln_in_b, b_in=b_in, sinks=sinks, vn_g=vn_g, vn_b=vn_b, w_s=w_s, b_s=b_s,
                   b_out=b_out, ln_g=ln_g, ln_b=ln_b)
    m_small = dict(ln_in_g=m_ln_in_g, ln_in_b=m_ln_in_b, b_in=m_b_in, sinks=m_sinks, vn_g=m_vn_g, vn_b=m_vn_b,
                   w_s=m_w_s, b_s=m_b_s, b_out=m_b_out, ln_g=m_ln_g, ln_b=m_ln_b)
    v_small = dict(ln_in_g=v_ln_in_g, ln_in_b=v_ln_in_b, b_in=v_b_in, sinks=v_sinks, vn_g=v_vn_g, vn_b=v_vn_b,
                   w_s=v_w_s, b_s=v_b_s, b_out=v_b_out, ln_g=v_ln_g, ln_b=v_ln_b)
    by_rows = _adamw_small(waited[0][1][4], waited[0][0][4], *[pack_rows(d) for d in (w_small, m_small, v_small)],
                           [_tile_rows(w_small[n]) for n in _SMALL_ROWS], 0, "adamw_small_rows")
    by_lanes = _adamw_small(waited[0][1][5], waited[0][0][5], *[pack_lanes(d) for d in (w_small, m_small, v_small)],
                            [w_small[n].shape[1] for n in _SMALL_LANES], 1, "adamw_small_lanes")
    _, (all_in, all_sq) = _split_wait(ln_send, ln_recv, ln_thru, _plan_all, by_lanes[0][0], "allgather_ln_in_wait")
    loss = jnp.sum(all_sq[:, 0, 0]) * (0.5 / D)
    outs_in = _adamw([all_in], *[jnp.pad(jnp.stack([d[n] for n in _SMALL_IN]), ((0, 6), (0, 0)))
                                 for d in (w_small, m_small, v_small)], "adamw_ln_in")
    for k in range(4):
        u = {n: o.reshape(-1)[:w_small[n].size].reshape(w_small[n].shape) for n, o in zip(_SMALL_ROWS, by_rows[k])}
        u.update(zip(_SMALL_LANES, by_lanes[k]))
        u.update({n: outs_in[k][r] for r, n in enumerate(_SMALL_IN)})
        for n in u:
            res.setdefault(n, [None] * 4)[k] = u[n]

    order = ("ln_in_g", "ln_in_b", "w_in", "b_in", "sinks", "vn_g", "vn_b", "w_s", "b_s", "p_a", "p_b", "w_out",
             "b_out", "ln_g", "ln_b")
    return (loss, grad_x.reshape(x.shape), *[res[n][0] for n in order], *[res[n][1] for n in order],
            *[res[n][2] for n in order], *[res[n][3] for n in order])
```

```python
import jax
import jax.numpy as jnp
from jax import lax
from jax.experimental import pallas as pl
from jax.experimental.pallas import tpu as pltpu

F32 = jnp.float32
BF16 = jnp.bfloat16

D = 1024
BLK = 128
N_KV = 2
Q_W, KV_W, SGU_W = 512, 128, 512
C_Q, C_K, C_V, C_GA, C_UB, C_VB, C_GB = 0, 512, 640, 768, 1280, 1792, 2304
MAIN_W = 2816
R_W = 2048
IN_COLS = MAIN_W + R_W
N_DEV = 8

DEPTH = 2
ALPHA = (2.0 * DEPTH) ** 0.25
LN_EPS = 1e-5
ATTN_SCALE = 0.125
NEG = float(jnp.finfo(jnp.float32).min)

ADAM_LR, ADAM_B1, ADAM_B2, ADAM_EPS, ADAM_WD, ADAM_STEP = 0.001, 0.9, 0.999, 1e-08, 0.01, 10

TM = 512
TM_EW = 1024
NB = TM // BLK
MESH = pl.DeviceIdType.MESH
VMEM_LIMIT = 56 * 1024 * 1024

_ARB = pltpu.CompilerParams(dimension_semantics=("arbitrary",), vmem_limit_bytes=VMEM_LIMIT)


def _sigmoid(x):
    return 0.5 + 0.5 * jnp.tanh(0.5 * x)


_GELU_C = 0.7978845608028654
_GELU_A = 0.044715


def _gelu_parts(x):
    x2 = x * x
    t = jnp.tanh(x * (_GELU_C + (_GELU_C * _GELU_A) * x2))
    hx = 0.5 * x
    return hx, t, x2


def _gelu(x):
    hx, t, _ = _gelu_parts(x)
    return hx + hx * t


def _gelu_and_grad(x):
    hx, t, x2 = _gelu_parts(x)
    grad = 0.5 + 0.5 * t + (hx - hx * (t * t)) * (_GELU_C + (3.0 * _GELU_C * _GELU_A) * x2)
    return hx + hx * t, grad


def _ln_stats(x):
    mu = jnp.mean(x, axis=-1, keepdims=True)
    xc = x - mu
    var = jnp.mean(xc * xc, axis=-1, keepdims=True)
    rstd = lax.rsqrt(var + LN_EPS)
    return xc * rstd, rstd


def _ln_bwd(dy_g, xhat, rstd):
    m1 = jnp.mean(dy_g, axis=-1, keepdims=True)
    m2 = jnp.mean(dy_g * xhat, axis=-1, keepdims=True)
    return rstd * (dy_g - m1 - xhat * m2)


def _colsum(x):
    return jnp.sum(x, axis=0, keepdims=True)


def _dot(a, b):
    return jnp.dot(a, b, preferred_element_type=F32)


def _dot_nt(a, b):
    return lax.dot_general(a, b, (((1,), (1,)), ((), ())), preferred_element_type=F32)


def _side_by_side(gathered_ref):
    return jnp.concatenate([gathered_ref[j] for j in range(N_DEV)], axis=1)


def _dot_tn(a, b):
    return lax.dot_general(a, b, (((0,), (0,)), ((), ())), preferred_element_type=F32)


def _head_place(hk, g):
    j = 4 * hk + g
    return j, j // 2, j % 2


def _head_rows(x, hk):
    d = lax.broadcasted_iota(jnp.int32, x.shape, 0)
    return jnp.where((d >= 64 * hk) & (d < 64 * hk + 64), x, 0.0).astype(BF16)


def _head_lanes(x, hk):
    d = lax.broadcasted_iota(jnp.int32, x.shape, 1)
    return jnp.where((d >= 64 * hk) & (d < 64 * hk + 64), x, 0.0)


def _band_bias():
    kpos = lax.broadcasted_iota(jnp.int32, (2 * BLK, 4 * BLK), 0)
    row = lax.broadcasted_iota(jnp.int32, (2 * BLK, 4 * BLK), 1) & (BLK - 1)
    band = (kpos > row) & (kpos <= row + BLK)
    return jnp.stack([jnp.where(band, 0.0, NEG), jnp.where(band & (kpos >= BLK), 0.0, NEG)]).astype(F32)


def _stack_q(q, hk):
    parts = []
    for g in range(4):
        _, p, pos = _head_place(hk, g)
        qp = q[:, BLK * p:BLK * (p + 1)] * ATTN_SCALE
        if pos != hk:
            qp = pltpu.roll(qp, 64, 1)
        parts.append(qp.astype(BF16))
    return jnp.concatenate(parts, axis=0)


def _attn_probs(q4, kh, hk, sinks_ref, bias):
    s_t = _dot_nt(kh, q4) + bias
    sink_row = jnp.concatenate(
        [jnp.full((1, BLK), sinks_ref[4 * hk + g], F32) for g in range(4)], axis=1)
    m = jnp.maximum(jnp.max(s_t, axis=0, keepdims=True), sink_row)
    p_un = jnp.exp(s_t - m)
    e_sink = jnp.exp(sink_row - m)
    inv = 1.0 / (jnp.sum(p_un, axis=0, keepdims=True) + e_sink)
    return (p_un * inv).astype(BF16), e_sink * inv


def _unstack_heads(x4, hk, pairs):
    for g in range(4):
        _, p, pos = _head_place(hk, g)
        xg = x4[BLK * g:BLK * (g + 1)]
        if pos != hk:
            xg = pltpu.roll(xg, 64, 1)
        pairs[p] = xg if pairs[p] is None else pairs[p] + xg
    return pairs


def _attn_fwd(q, kband, vband, sinks_ref, bias, save):
    pairs = [None] * 4
    vband_t = vband.T
    for hk in range(N_KV):
        prob_t, p_sink = _attn_probs(_stack_q(q, hk), _head_lanes(kband, hk).astype(BF16), hk, sinks_ref, bias)
        save(hk, prob_t, p_sink)
        o_t = _dot(_head_rows(vband_t, hk), prob_t)
        pairs = _unstack_heads(o_t.T, hk, pairs)
    return jnp.concatenate(pairs, axis=1)


def _tril_mask():
    r = lax.broadcasted_iota(jnp.int32, (BLK, BLK), 0)
    c = lax.broadcasted_iota(jnp.int32, (BLK, BLK), 1)
    return c <= r


def _sgu_fwd(u, v, vn_g, vn_b, wt, bsb_ref):
    vhat, rstd = _ln_stats(v)
    vn = vhat * vn_g + vn_b
    mixed = jnp.concatenate(
        [_dot(wt[g], vn[:, BLK * g:BLK * (g + 1)].astype(BF16)) + bsb_ref[g] for g in range(4)], axis=1)
    return vhat, rstd, vn, mixed


def _cols(ref, rows, col, width):
    return ref[rows, col:col + width].astype(F32)


def _band(hm_ref, hprev_ref, s, col):
    r0 = s * BLK
    cur = hm_ref[r0:r0 + BLK, col:col + KV_W]
    if s == 0:
        off = 0 if col == C_K else KV_W
        prev = hprev_ref[:, off:off + KV_W]
    else:
        prev = hm_ref[r0 - BLK:r0, col:col + KV_W]
    return jnp.concatenate([prev, cur], axis=0).astype(F32)


def _h_main_specs(nt, rev):
    def tile(g):
        return nt - 1 - g if rev else g

    return [pl.BlockSpec((TM, MAIN_W), lambda g: (tile(g), 0)),
            pl.BlockSpec((BLK, 2 * KV_W), lambda g: (jnp.maximum(tile(g) * NB - 1, 0), 2))]


_CONST2 = lambda g: (0, 0)
_CONST3 = lambda g: (0, 0, 0)


def _ln_fwd(x, g, b, name):
    t = x.shape[0]

    def body(x_ref, g_ref, b_ref, o_ref):
        xhat, _ = _ln_stats(x_ref[...])
        o_ref[...] = xhat * g_ref[...] + b_ref[...]

    return pl.pallas_call(
        body, name=name, grid=(t // TM_EW,),
        in_specs=[pl.BlockSpec((TM_EW, D), lambda i: (i, 0)), pl.BlockSpec((1, D), _CONST2),
                  pl.BlockSpec((1, D), _CONST2)],
        out_specs=pl.BlockSpec((TM_EW, D), lambda i: (i, 0)),
        out_shape=jax.ShapeDtypeStruct((t, D), F32), compiler_params=_ARB,
    )(x, g.reshape(1, D), b.reshape(1, D))


def _inproj(x, w_t, b, name):
    t = x.shape[0]

    def body(x_ref, wt_ref, b_ref, hm_ref, hr_ref):
        xb = x_ref[...].astype(BF16)
        hm_ref[...] = (_dot_nt(xb, wt_ref[0:MAIN_W, :]) + b_ref[:, 0:MAIN_W]).astype(BF16)
        hr_ref[...] = (_dot_nt(xb, wt_ref[MAIN_W:IN_COLS, :]) + b_ref[:, MAIN_W:IN_COLS]).astype(BF16)

    return pl.pallas_call(
        body, name=name, grid=(t // TM,),
        in_specs=[pl.BlockSpec((TM, D), lambda i: (i, 0)),
                  pl.BlockSpec((IN_COLS, D), _CONST2), pl.BlockSpec((1, IN_COLS), _CONST2)],
        out_specs=[pl.BlockSpec((TM, MAIN_W), lambda i: (i, 0)), pl.BlockSpec((TM, R_W), lambda i: (i, 0))],
        out_shape=[jax.ShapeDtypeStruct((t, MAIN_W), BF16), jax.ShapeDtypeStruct((t, R_W), BF16)],
        compiler_params=_ARB,
    )(x, w_t, b)


def _mixer_fwd(hm, sinks, bias, vn_g, vn_b, w_s, bsb, nblk_seq, name):
    t = hm.shape[0]
    nt = t // TM

    def body(sinks_ref, hm_ref, hprev_ref, bias_ref, vng_ref, vnb_ref, ws_ref, bsb_ref,
             ya_ref, yb_ref, prob_ref, psink_ref):
        i = pl.program_id(0)
        tril = _tril_mask()
        wt = [jnp.where(tril, ws_ref[g], 0.0).astype(BF16) for g in range(4)]
        for s in range(NB):
            r0 = s * BLK
            rows = slice(r0, r0 + BLK)
            bias = bias_ref[jnp.where((i * NB + s) % nblk_seq == 0, 1, 0)]

            def save(hk, prob_t, p_sink, s=s):
                prob_ref[N_KV * s + hk] = prob_t
                psink_ref[N_KV * s + hk] = jnp.broadcast_to(p_sink, (8, 4 * BLK))

            attn = _attn_fwd(_cols(hm_ref, rows, C_Q, Q_W), _band(hm_ref, hprev_ref, s, C_K),
                             _band(hm_ref, hprev_ref, s, C_V), sinks_ref, bias, save)
            g_a = _cols(hm_ref, rows, C_GA, Q_W)
            ya_ref[rows, :] = (attn * (g_a * _sigmoid(g_a))).astype(BF16)
            u = _gelu(_cols(hm_ref, rows, C_UB, SGU_W))
            mixed = _sgu_fwd(u, _gelu(_cols(hm_ref, rows, C_VB, SGU_W)), vng_ref[...], vnb_ref[...], wt, bsb_ref)[-1]
            g_b = _cols(hm_ref, rows, C_GB, SGU_W)
            yb_ref[rows, :] = (u * mixed * (g_b * _sigmoid(g_b))).astype(BF16)

    ngrp = N_KV * NB
    return pl.pallas_call(
        body, name=name, grid=(nt,),
        in_specs=[pl.BlockSpec(memory_space=pltpu.SMEM)] + _h_main_specs(nt, False) + [
            pl.BlockSpec((2, 2 * BLK, 4 * BLK), _CONST3),
            pl.BlockSpec((1, SGU_W), _CONST2), pl.BlockSpec((1, SGU_W), _CONST2),
            pl.BlockSpec((4, BLK, BLK), _CONST3), pl.BlockSpec((4, BLK, BLK), _CONST3)],
        out_specs=[pl.BlockSpec((TM, Q_W), lambda i: (i, 0)), pl.BlockSpec((TM, SGU_W), lambda i: (i, 0)),
                   pl.BlockSpec((ngrp, 2 * BLK, 4 * BLK), lambda i: (i, 0, 0)),
                   pl.BlockSpec((ngrp, 8, 4 * BLK), lambda i: (i, 0, 0))],
        out_shape=[jax.ShapeDtypeStruct((t, Q_W), BF16), jax.ShapeDtypeStruct((t, SGU_W), BF16),
                   jax.ShapeDtypeStruct((nt * ngrp, 2 * BLK, 4 * BLK), BF16),
                   jax.ShapeDtypeStruct((nt * ngrp, 8, 4 * BLK), F32)],
        compiler_params=_ARB,
    )(sinks, hm, hm, bias, vn_g, vn_b, w_s, bsb)


def _tail_fwd(x, ya, yb, hr, pa_w, pb_w, wo, b_out, ln_g, ln_b, name, last):
    t = x.shape[0]

    def body(x_ref, ya_ref, yb_ref, hr_ref, paw_ref, pbw_ref, wo_ref, bo_ref, g_ref, b_ref,
             pa_ref, pb_ref, mg_ref, z_ref, *xn_ref):
        pa = _dot(ya_ref[...], _side_by_side(paw_ref))
        pb = _dot(yb_ref[...], _side_by_side(pbw_ref))
        pa_ref[...] = pa.astype(BF16)
        pb_ref[...] = pb.astype(BF16)
        everything = slice(None)
        merged = _sigmoid(_cols(hr_ref, everything, 0, D)) * pa + _sigmoid(_cols(hr_ref, everything, D, D)) * pb
        mb = merged.astype(BF16)
        mg_ref[...] = mb
        z = ALPHA * x_ref[...] + (_dot(mb, wo_ref[...]) + bo_ref[...])
        z_ref[...] = z
        if not last:
            zhat, _ = _ln_stats(z)
            xn_ref[0][...] = zhat * g_ref[...] + b_ref[...]

    row = lambda w: pl.BlockSpec((TM, w), lambda i: (i, 0))
    vec = pl.BlockSpec((1, D), _CONST2)
    n_f32 = 1 if last else 2
    return pl.pallas_call(
        body, name=name, grid=(t // TM,),
        in_specs=[row(D), row(Q_W), row(SGU_W), row(R_W),
                  pl.BlockSpec((N_DEV, Q_W, 128), _CONST3), pl.BlockSpec((N_DEV, SGU_W, 128), _CONST3),
                  pl.BlockSpec((D, D), _CONST2), vec, vec, vec],
        out_specs=[row(D)] * (3 + n_f32),
        out_shape=[jax.ShapeDtypeStruct((t, D), BF16)] * 3 + [jax.ShapeDtypeStruct((t, D), F32)] * n_f32,
        compiler_params=_ARB,
    )(x, ya, yb, hr, pa_w, pb_w, wo, b_out, ln_g, ln_b)


def _tail_bwd(dxn, z, pa, pb, hr, wo, pa_w, pb_w, ln_g, ln_b, name, from_loss):
    t = dxn.shape[0]

    def body(dxn_ref, z_ref, pa_ref, pb_ref, hr_ref, wo_ref, paw_ref, pbw_ref, g_ref, b_ref,
             dz_ref, dpa_ref, dpb_ref, dhr_ref, dya_ref, dyb_ref, acc_ref, gbr_ref):
        @pl.when(pl.program_id(0) == 0)
        def _():
            acc_ref[...] = jnp.zeros_like(acc_ref)
            gbr_ref[...] = jnp.zeros_like(gbr_ref)

        zhat, rstd = _ln_stats(z_ref[...])
        if from_loss:
            err = zhat * g_ref[...] + b_ref[...] - dxn_ref[...]
            dxn_v = err * (1.0 / D)
            sq = jnp.sum(jnp.sum(err * err, axis=1, keepdims=True), axis=0, keepdims=True)
            acc_ref[3:4, :] += jnp.broadcast_to(sq, (1, D))
        else:
            dxn_v = dxn_ref[...]
        dz = _ln_bwd(dxn_v * g_ref[...], zhat, rstd)
        dz_ref[...] = dz
        acc_ref[0:1, :] += _colsum(dxn_v * zhat)
        acc_ref[1:2, :] += _colsum(dxn_v)
        acc_ref[2:3, :] += _colsum(dz)
        dmerged = _dot_nt(dz.astype(BF16), wo_ref[...])
        everything = slice(None)
        sa = _sigmoid(_cols(hr_ref, everything, 0, D))
        sb = _sigmoid(_cols(hr_ref, everything, D, D))
        dpa = (dmerged * sa).astype(BF16)
        dpb = (dmerged * sb).astype(BF16)
        dpa_ref[...] = dpa
        dpb_ref[...] = dpb
        dra = dmerged * pa_ref[...].astype(F32) * (sa * (1.0 - sa))
        drb = dmerged * pb_ref[...].astype(F32) * (sb * (1.0 - sb))
        dhr_ref[:, 0:D] = dra.astype(BF16)
        dhr_ref[:, D:2 * D] = drb.astype(BF16)
        gbr_ref[0:1, 0:D] += _colsum(dra)
        gbr_ref[0:1, D:2 * D] += _colsum(drb)
        dya_ref[...] = _dot_nt(dpa, _side_by_side(paw_ref)).astype(BF16)
        dyb_ref[...] = _dot_nt(dpb, _side_by_side(pbw_ref)).astype(BF16)

    row = lambda w: pl.BlockSpec((TM, w), lambda i: (i, 0))
    vec = pl.BlockSpec((1, D), _CONST2)
    return pl.pallas_call(
        body, name=name, grid=(t // TM,),
        in_specs=[row(D), row(D), row(D), row(D), row(R_W),
                  pl.BlockSpec((D, D), _CONST2), pl.BlockSpec((N_DEV, Q_W, 128), _CONST3),
                  pl.BlockSpec((N_DEV, SGU_W, 128), _CONST3), vec, vec],
        out_specs=[row(D), row(D), row(D), row(R_W), row(Q_W), row(SGU_W), pl.BlockSpec((8, D), _CONST2),
                   pl.BlockSpec((8, R_W), _CONST2)],
        out_shape=[jax.ShapeDtypeStruct((t, D), F32), jax.ShapeDtypeStruct((t, D), BF16),
                   jax.ShapeDtypeStruct((t, D), BF16), jax.ShapeDtypeStruct((t, R_W), BF16),
                   jax.ShapeDtypeStruct((t, Q_W), BF16), jax.ShapeDtypeStruct((t, SGU_W), BF16),
                   jax.ShapeDtypeStruct((8, D), F32), jax.ShapeDtypeStruct((8, R_W), F32)],
        compiler_params=_ARB,
    )(dxn, z, pa, pb, hr, wo, pa_w, pb_w, ln_g, ln_b)


def _mixer_bwd(hm, dya, dyb, prob, psink, vn_g, vn_b, w_s, bsb, name):
    t = hm.shape[0]
    nt = t // TM
    ngrp = N_KV * NB

    def body(hm_ref, hprev_ref, prob_ref, psink_ref, dya_ref, dyb_ref, vng_ref, vnb_ref, ws_ref, bsb_ref,
             dhm_ref, gbm_ref, gsk_ref, gvn_ref, gws_ref, gbs_ref, dk_carry, dv_carry):
        gi = pl.program_id(0)

        @pl.when(gi == 0)
        def _():
            for r in (gbm_ref, gsk_ref, gvn_ref, gws_ref, gbs_ref, dk_carry, dv_carry):
                r[...] = jnp.zeros_like(r)

        tril = _tril_mask()
        wt = [jnp.where(tril, ws_ref[g], 0.0).astype(BF16) for g in range(4)]
        vng = vng_ref[...]
        ones8 = jnp.ones((8, BLK), BF16)

        def put(rows, col, val):
            dhm_ref[rows, col:col + val.shape[1]] = val.astype(BF16)

        for s in reversed(range(NB)):
            r0 = s * BLK
            rows = slice(r0, r0 + BLK)
            q = _cols(hm_ref, rows, C_Q, Q_W)
            kband = _band(hm_ref, hprev_ref, s, C_K)
            vband = _band(hm_ref, hprev_ref, s, C_V)
            g_a = _cols(hm_ref, rows, C_GA, Q_W)
            sg = _sigmoid(g_a)
            dya_v = _cols(dya_ref, rows, 0, Q_W)
            d_o = dya_v * (g_a * sg)
            o_pairs, dq_pairs = [None] * 4, [None] * 4
            dkband = jnp.zeros((2 * BLK, KV_W), F32)
            dvband = jnp.zeros((2 * BLK, KV_W), F32)
            kband_t, vband_t = kband.T, vband.T
            for hk in range(N_KV):
                q4 = _stack_q(q, hk)
                prob_b = prob_ref[N_KV * s + hk]
                p_sink = psink_ref[N_KV * s + hk][0:1, :]
                o_t = _dot(_head_rows(vband_t, hk), prob_b)
                o_pairs = _unstack_heads(o_t.T, hk, o_pairs)
                parts = []
                for g in range(4):
                    _, p, pos = _head_place(hk, g)
                    dp = d_o[:, BLK * p:BLK * (p + 1)]
                    parts.append(pltpu.roll(dp, 64, 1) if pos != hk else dp)
                do4 = _head_lanes(jnp.concatenate(parts, axis=0), hk)
                do4b = do4.astype(BF16)
                delta = _colsum(do4.T * o_t)
                vh = _head_lanes(vband, hk).astype(BF16)
                ds_t = prob_b.astype(F32) * (_dot_nt(vh, do4b) - delta)
                dsb = ds_t.astype(BF16)
                dq4_t = _dot(_head_rows(kband_t, hk), dsb)
                dq_pairs = _unstack_heads(dq4_t.T * ATTN_SCALE, hk, dq_pairs)
                dkband = dkband + _head_lanes(_dot(dsb, q4), hk)
                dvband = dvband + _dot(prob_b, do4b)
                dsk = p_sink * delta
                for g in range(4):
                    j = 4 * hk + g
                    tot = jnp.sum(dsk[:, BLK * g:BLK * (g + 1)], axis=1, keepdims=True)
                    gsk_ref[j:j + 1, :] += jnp.broadcast_to(-tot, (1, 128))
            attn = jnp.concatenate(o_pairs, axis=1)
            put(rows, C_Q, jnp.concatenate(dq_pairs, axis=1))
            put(rows, C_K, dkband[BLK:2 * BLK] + dk_carry[...])
            put(rows, C_V, dvband[BLK:2 * BLK] + dv_carry[...])
            dk_carry[...] = dkband[0:BLK]
            dv_carry[...] = dvband[0:BLK]
            put(rows, C_GA, dya_v * attn * (sg * (1.0 + g_a * (1.0 - sg))))
            u, du_du_b = _gelu_and_grad(_cols(hm_ref, rows, C_UB, SGU_W))
            v, dv_dv_b = _gelu_and_grad(_cols(hm_ref, rows, C_VB, SGU_W))
            g_b = _cols(hm_ref, rows, C_GB, SGU_W)
            vhat, rstd, vn, mixed = _sgu_fwd(u, v, vng, vnb_ref[...], wt, bsb_ref)
            sgb = _sigmoid(g_b)
            silu_b = g_b * sgb
            dyb_v = _cols(dyb_ref, rows, 0, SGU_W)
            du = dyb_v * mixed * silu_b
            dmixed = dyb_v * u * silu_b
            put(rows, C_GB, dyb_v * u * mixed * (sgb * (1.0 + g_b * (1.0 - sgb))))
            dvn_parts = []
            for g in range(4):
                cols = slice(BLK * g, BLK * (g + 1))
                dmg = dmixed[:, cols]
                dmgb = dmg.astype(BF16)
                dvn_parts.append(_dot_tn(wt[g], dmgb))
                gws_ref[g] += jnp.where(tril, _dot_nt(dmgb, vn[:, cols].astype(BF16)), 0.0)
                gbs_ref[g] += dmg
            dvn = jnp.concatenate(dvn_parts, axis=1)
            gvn_ref[0:1, :] += _colsum(dvn * vhat)
            gvn_ref[1:2, :] += _colsum(dvn)
            dv = _ln_bwd(dvn * vng, vhat, rstd)
            put(rows, C_UB, du * du_du_b)
            put(rows, C_VB, dv * dv_dv_b)
            gbm_ref[...] += _dot(ones8, dhm_ref[rows, :])

        @pl.when(gi == nt - 1)
        def _():
            for g in range(4):
                gbs_ref[g] = jnp.broadcast_to(jnp.sum(gbs_ref[g], axis=1, keepdims=True), (BLK, BLK))

    row = lambda w: pl.BlockSpec((TM, w), lambda g: (nt - 1 - g, 0))
    return pl.pallas_call(
        body, name=name, grid=(nt,),
        in_specs=_h_main_specs(nt, True) + [
            pl.BlockSpec((ngrp, 2 * BLK, 4 * BLK), lambda g: (nt - 1 - g, 0, 0)),
            pl.BlockSpec((ngrp, 8, 4 * BLK), lambda g: (nt - 1 - g, 0, 0)),
            row(Q_W), row(SGU_W),
            pl.BlockSpec((1, SGU_W), _CONST2), pl.BlockSpec((1, SGU_W), _CONST2),
            pl.BlockSpec((4, BLK, BLK), _CONST3), pl.BlockSpec((4, BLK, BLK), _CONST3)],
        out_specs=[row(MAIN_W), pl.BlockSpec((8, MAIN_W), _CONST2), pl.BlockSpec((8, 128), _CONST2),
                   pl.BlockSpec((8, SGU_W), _CONST2), pl.BlockSpec((4, BLK, BLK), _CONST3),
                   pl.BlockSpec((4, BLK, BLK), _CONST3)],
        out_shape=[jax.ShapeDtypeStruct((t, MAIN_W), BF16), jax.ShapeDtypeStruct((8, MAIN_W), F32),
                   jax.ShapeDtypeStruct((8, 128), F32), jax.ShapeDtypeStruct((8, SGU_W), F32),
                   jax.ShapeDtypeStruct((4, BLK, BLK), F32), jax.ShapeDtypeStruct((4, BLK, BLK), F32)],
        scratch_shapes=[pltpu.VMEM((BLK, KV_W), F32), pltpu.VMEM((BLK, KV_W), F32)],
        compiler_params=_ARB,
    )(hm, hm, prob, psink, dya, dyb, vn_g, vn_b, w_s, bsb)


def _dx_inproj(dz, dhm, dhr, w_t, after, name, ln_in=None):
    t = dz.shape[0]

    def body(dz_ref, dhm_ref, dhr_ref, wt_ref, after_ref, *rest):
        dx = (ALPHA * dz_ref[...] + after_ref[0:1, 0:1] + _dot(dhm_ref[...], wt_ref[0:MAIN_W, :])
              + _dot(dhr_ref[...], wt_ref[MAIN_W:IN_COLS, :]))
        if ln_in is None:
            rest[0][...] = dx
            return
        x_ref, g_ref, gx_ref, acc_ref = rest

        @pl.when(pl.program_id(0) == 0)
        def _():
            acc_ref[...] = jnp.zeros_like(acc_ref)

        xhat, rstd = _ln_stats(x_ref[...])
        gx_ref[...] = _ln_bwd(dx * g_ref[...], xhat, rstd)
        acc_ref[0:1, :] += _colsum(dx * xhat)
        acc_ref[1:2, :] += _colsum(dx)

    row = lambda w: pl.BlockSpec((TM, w), lambda i: (i, 0))
    in_specs = [row(D), row(MAIN_W), row(R_W), pl.BlockSpec((IN_COLS, D), _CONST2), pl.BlockSpec((8, 128), _CONST2)]
    if ln_in is None:
        return pl.pallas_call(
            body, name=name, grid=(t // TM,), in_specs=in_specs,
            out_specs=row(D), out_shape=jax.ShapeDtypeStruct((t, D), F32), compiler_params=_ARB,
        )(dz, dhm, dhr, w_t, after)
    return pl.pallas_call(
        body, name=name, grid=(t // TM,), in_specs=in_specs + [row(D), pl.BlockSpec((1, D), _CONST2)],
        out_specs=[row(D), pl.BlockSpec((8, D), _CONST2)],
        out_shape=[jax.ShapeDtypeStruct((t, D), F32), jax.ShapeDtypeStruct((8, D), F32)], compiler_params=_ARB,
    )(dz, dhm, dhr, w_t, after, *ln_in)


def _wgrad(a, b, tm, name, rows=None, under=None, by_owner=False):
    t, m = a.shape
    n = b.shape[1]
    tk = min(t, 2048)
    nk = t // tk

    def body(a_ref, b_ref, *rest):
        o_ref, acc_ref = rest[-2:]
        k = pl.program_id(1)

        @pl.when(k == 0)
        def _():
            acc_ref[...] = jnp.zeros_like(acc_ref)

        acc_ref[...] += _dot_tn(a_ref[...].astype(BF16), b_ref[...].astype(BF16))

        @pl.when(k == nk - 1)
        def _():
            if by_owner:
                for j in range(N_DEV):
                    o_ref[j] = acc_ref[:, j * (n // N_DEV):(j + 1) * (n // N_DEV)].astype(BF16)
            else:
                o_ref[...] = acc_ref[...].astype(BF16)

    in_specs = [pl.BlockSpec((tk, tm), lambda j, k: (k, j)), pl.BlockSpec((tk, n), lambda j, k: (k, 0))]
    if by_owner:
        return pl.pallas_call(
            body, name=name, grid=(m // tm, nk), in_specs=in_specs,
            out_specs=pl.BlockSpec((N_DEV, tm, n // N_DEV), lambda j, k: (0, j, 0)),
            out_shape=jax.ShapeDtypeStruct((N_DEV, m, n // N_DEV), BF16), scratch_shapes=[pltpu.VMEM((tm, n), F32)],
            compiler_params=pltpu.CompilerParams(dimension_semantics=("arbitrary", "arbitrary"),
                                                 vmem_limit_bytes=VMEM_LIMIT),
        )(a, b)
    if under is None:
        out_rows, out_spec, operands, aliases = rows or m, pl.BlockSpec((tm, n), lambda j, k: (j, 0)), (a, b), {}
    else:
        out_rows = under.shape[0]
        first = out_rows - m
        assert first % 128 == 0 and tm % 128 == 0
        out_spec = pl.BlockSpec((pl.Element(tm), pl.Element(n)),
                                lambda j, k: (pl.multiple_of(first + j * tm, 128), 0))
        in_specs, operands, aliases = in_specs + [_ANY], (a, b, under), {2: 0}
    return pl.pallas_call(
        body, name=name, grid=(m // tm, nk), in_specs=in_specs, out_specs=out_spec,
        out_shape=jax.ShapeDtypeStruct((out_rows, n), BF16), input_output_aliases=aliases,
        scratch_shapes=[pltpu.VMEM((tm, n), F32)],
        compiler_params=pltpu.CompilerParams(dimension_semantics=("arbitrary", "arbitrary"), vmem_limit_bytes=VMEM_LIMIT),
    )(*operands)


_ANY = pl.BlockSpec(memory_space=pl.ANY)


def _place():
    return lax.axis_index("x"), lax.axis_index("y"), lax.axis_index("c")


def _forward_sibling(lands, name):
    n = len(lands)

    def body(*refs):
        l_refs = refs[n:2 * n]
        send_sems, recv_sems = refs[2 * n:]
        x, y, c = _place()
        chips = [(1 - x, y), (x, 1 - y), (1 - x, 1 - y)]

        def copy(a, j, core):
            rows = l_refs[a].at[4 * chips[j][0] + 2 * chips[j][1] + core]
            return pltpu.make_async_remote_copy(
                src_ref=rows, dst_ref=rows, send_sem=send_sems.at[3 * a + j], recv_sem=recv_sems.at[3 * a + j],
                device_id=(x, y, 1 - c), device_id_type=MESH)

        for a in range(n):
            for j in range(3):
                copy(a, j, c).start()
        for a in range(n):
            for j in range(3):
                copy(a, j, 1 - c).wait_recv()
                copy(a, j, c).wait_send()

    return pl.pallas_call(
        body, name=name, in_specs=[_ANY] * n, out_specs=[_ANY] * n,
        out_shape=[jax.ShapeDtypeStruct(v.shape, v.dtype) for v in lands],
        input_output_aliases={a: a for a in range(n)},
        scratch_shapes=[pltpu.SemaphoreType.DMA((3 * n,)), pltpu.SemaphoreType.DMA((3 * n,))],
    )(*lands)


def _swap_sibling(gs, name):
    n = len(gs)
    first = [0]
    for v in gs:
        first.append(first[-1] + v.shape[0])

    def body(*refs):
        g_refs, r_refs = refs[:n], refs[n:2 * n]
        send_sems, recv_sems = refs[2 * n:]
        x, y, c = _place()
        cps = [pltpu.make_async_remote_copy(
            src_ref=g_refs[a].at[q, 1 - c], dst_ref=r_refs[a].at[q], send_sem=send_sems.at[first[a] + q],
            recv_sem=recv_sems.at[first[a] + q], device_id=(x, y, 1 - c), device_id_type=MESH)
            for a in range(n) for q in range(gs[a].shape[0])]
        for cp in cps:
            cp.start()
        for cp in cps:
            cp.wait()

    return pl.pallas_call(
        body, name=name, in_specs=[_ANY] * n, out_specs=[_ANY] * n,
        out_shape=[jax.ShapeDtypeStruct(v.shape[:1] + v.shape[2:], v.dtype) for v in gs],
        scratch_shapes=[pltpu.SemaphoreType.DMA((first[-1],)), pltpu.SemaphoreType.DMA((first[-1],))],
    )(*gs)


def _row_tile(rows, lanes, cap):
    if rows * lanes * 4 <= (1 << 20):
        return rows
    return max(d for d in range(8, cap + 1, 8) if rows % d == 0 and (d % 16 == 0 or rows % 16 != 0))


def _pair_sums(gs, rs, name):
    n = len(gs)

    def add(g, r, dtype):
        return (g.astype(F32) + r.astype(F32)).astype(dtype)

    def body(c_ref, *refs):
        g_refs, r_refs, o_refs = refs[:n], refs[n:2 * n], refs[2 * n:]
        o_refs[0][...] = add(g_refs[0][0], r_refs[0][...], o_refs[0].dtype)

        @pl.when(pl.program_id(0) == 0)
        def _():
            for a in range(1, n):
                o_refs[a][...] = add(g_refs[a][:, 0], r_refs[a][...], o_refs[a].dtype)

    def whole(shape, mine):
        if mine:
            return pl.BlockSpec(shape, lambda q, c_ref: (0, c_ref[0]) + (0,) * (len(shape) - 2))
        return pl.BlockSpec(shape, lambda q, c_ref: (0,) * len(shape))

    big = gs[0].shape
    return pl.pallas_call(
        body, name=name,
        grid_spec=pltpu.PrefetchScalarGridSpec(
            num_scalar_prefetch=1, grid=(big[0],),
            in_specs=[pl.BlockSpec((1, 1) + big[2:], lambda q, c_ref: (q, c_ref[0], 0, 0))]
            + [whole(g.shape[:1] + (1,) + g.shape[2:], True) for g in gs[1:]]
            + [pl.BlockSpec((1,) + big[2:], lambda q, c_ref: (q, 0, 0))]
            + [whole(r.shape, False) for r in rs[1:]],
            out_specs=[pl.BlockSpec((1,) + big[2:], lambda q, c_ref: (q, 0, 0))]
            + [whole(r.shape, False) for r in rs[1:]]),
        out_shape=[jax.ShapeDtypeStruct(r.shape, g.dtype) for g, r in zip(gs, rs)],
        compiler_params=_ARB,
    )(lax.axis_index("c").astype(jnp.int32).reshape(1), *gs, *rs)


def _adamw(parts, w, m, v, name, own=None):
    nl = len(parts)
    ns, rows, l = parts[0].shape
    tr = _row_tile(rows, l * ns, 304)
    nt = rows // tr
    c1 = 1.0 - ADAM_B1 ** ADAM_STEP
    c2 = 1.0 - ADAM_B2 ** ADAM_STEP

    def body(q_ref, *refs):
        own_refs = refs[:nl] if own is not None else None
        p_refs = refs[-7 - nl:-7]
        w_ref, m_ref, v_ref, g_ref, d_ref, nm_ref, nv_ref = refs[-7:]
        layer = pl.program_id(0)
        g = None
        for j in range(nl):
            gj = None
            for k in range(ns):
                term = p_refs[j][k].astype(F32)
                if own_refs is not None:
                    term = jnp.where(q_ref[0] == k, own_refs[j][0].astype(F32), term)
                gj = term if gj is None else gj + term
            g = gj if g is None else jnp.where(layer == j, gj, g)
        g_ref[...] = g
        nm = ADAM_B1 * m_ref[...] + (1.0 - ADAM_B1) * g
        nv = ADAM_B2 * v_ref[...] + (1.0 - ADAM_B2) * (g * g)
        nm_ref[...] = nm
        nv_ref[...] = nv
        d_ref[...] = -ADAM_LR * ((nm / c1) / (jnp.sqrt(nv / c2) + ADAM_EPS) + ADAM_WD * w_ref[...])

    def tile_of(j):
        return lambda la, i, q: jnp.where(la == j, i, jnp.where(la < j, 0, nt - 1))

    row = pl.BlockSpec((tr, l), lambda la, i, q: (la * nt + i, 0))
    own_specs = [] if own is None else [
        pl.BlockSpec((1, tr, l), lambda la, i, q, j=j: (q[0], tile_of(j)(la, i, q), 0)) for j in range(nl)]
    part_specs = [pl.BlockSpec((ns, tr, l), lambda la, i, q, j=j: (0, tile_of(j)(la, i, q), 0)) for j in range(nl)]
    chip = (2 * lax.axis_index("x") + lax.axis_index("y")).astype(jnp.int32).reshape(1)
    return pl.pallas_call(
        body, name=name,
        grid_spec=pltpu.PrefetchScalarGridSpec(
            num_scalar_prefetch=1, grid=(nl, nt),
            in_specs=own_specs + part_specs + [row, row, row], out_specs=[row] * 4),
        out_shape=[jax.ShapeDtypeStruct((nl * rows, l), F32)] * 4,
        compiler_params=pltpu.CompilerParams(dimension_semantics=("arbitrary", "arbitrary"), vmem_limit_bytes=VMEM_LIMIT),
    )(chip, *([] if own is None else own), *parts, w, m, v)


_HBM = pl.BlockSpec(memory_space=pltpu.HBM)
_SEM = pl.BlockSpec(memory_space=pltpu.SEMAPHORE)
_EFFECT = pltpu.SideEffectType.DATAFLOW_SIDE_EFFECTING


def _plan_all(x, y, c):
    me = 4 * x + 2 * y + c
    peers = [(x, y, c), (x, y, 1 - c), (1 - x, y, c), (x, 1 - y, c), (1 - x, 1 - y, c),
             (1 - x, y, 1 - c), (x, 1 - y, 1 - c), (1 - x, 1 - y, 1 - c)]
    return [(None, me, p, 4 * p[0] + 2 * p[1] + p[2]) for p in peers]


def _plan_near(x, y, c):
    me = 4 * x + 2 * y + c
    peers = [(x, y, c), (x, y, 1 - c), (1 - x, y, c), (x, 1 - y, c), (1 - x, 1 - y, c)]
    return [(None, me, p, 4 * p[0] + 2 * p[1] + p[2]) for p in peers]


def _plan_sibling(x, y, c):
    return [(2 * q + 1 - c, q, (x, y, 1 - c), q) for q in range(4)]


def _plan_chips(x, y, c):
    me = 2 * x + y
    return [(2 * qx + qy, me, (qx, qy, c), 2 * qx + qy) for qx, qy in ((1 - x, y), (x, 1 - y), (1 - x, 1 - y))]


def _split_copies(plan, src_refs, land_refs, send_sems, recv_sems, arrival):
    n = len(src_refs)
    entries = plan(*_place())
    per = len(entries)
    cps = []
    for a in range(n):
        for k, (src_slot, dst_slot, peer, back_slot) in enumerate(entries):
            src = src_refs[a] if src_slot is None else src_refs[a].at[src_slot if src_refs[a].shape[0] > 1 else 0]
            cps.append(pltpu.make_async_remote_copy(
                src_ref=src, dst_ref=land_refs[a].at[back_slot if arrival else dst_slot],
                send_sem=send_sems.at[per * a + k], recv_sem=recv_sems.at[per * a + k],
                device_id=peer, device_id_type=MESH))
    return cps


def _split_start(srcs, lands, plan, per, name):
    n = len(srcs)

    def body(*refs):
        for cp in _split_copies(plan, refs[:n], refs[n:2 * n], refs[2 * n], refs[2 * n + 1], False):
            cp.start()
        refs[-1][...] = jnp.zeros_like(refs[-1])

    both = list(srcs) + list(lands)
    outs = pl.pallas_call(
        body, name=name,
        out_shape=(pltpu.SemaphoreType.DMA((per * n,)), pltpu.SemaphoreType.DMA((per * n,)),
                   *[pltpu.HBM(v.shape, v.dtype) for v in both], jax.ShapeDtypeStruct((8, 128), F32)),
        in_specs=[_HBM] * (2 * n),
        out_specs=(_SEM, _SEM, *[_HBM] * (2 * n), pl.BlockSpec(memory_space=pltpu.VMEM)),
        input_output_aliases={i: 2 + i for i in range(2 * n)},
        compiler_params=pltpu.CompilerParams(has_side_effects=_EFFECT),
    )(*[pltpu.with_memory_space_constraint(v, pltpu.HBM) for v in both])
    return outs[0], outs[1], list(outs[2:2 + 2 * n]), outs[-1]


def _split_wait(send_sems, recv_sems, thru, plan, after, name):
    n = len(thru) // 2

    def body(*refs):
        for cp in _split_copies(plan, refs[:n], refs[n:2 * n], refs[2 * n], refs[2 * n + 1], True):
            cp.wait_send()
            cp.wait_recv()

    outs = pl.pallas_call(
        body, name=name, out_shape=tuple(pltpu.HBM(v.shape, v.dtype) for v in thru),
        in_specs=[_HBM] * (2 * n) + [_SEM, _SEM, pl.BlockSpec(memory_space=pl.ANY)],
        out_specs=[_HBM] * (2 * n), input_output_aliases={i: i for i in range(2 * n)},
        compiler_params=pltpu.CompilerParams(has_side_effects=_EFFECT),
    )(*thru, send_sems, recv_sems, after)
    return list(outs[:n]), list(outs[n:])


_SMALL_IN = ("ln_in_g", "ln_in_b")
_SMALL_ROWS = ("w_s", "b_s", "sinks")
_SMALL_LANES = ("b_in", "vn_g", "vn_b", "b_out", "ln_g", "ln_b")


def _tile_rows(a):
    return -(-a.size // 1024) * 8


def _pack_small(d, names):
    return jnp.concatenate([jnp.pad(d[n].reshape(-1), (0, (-d[n].size) % 1024)).reshape(-1, 128) for n in names])


def _adamw_small(parts, own, w, m, v, pieces, axis, name):
    c1 = 1.0 - ADAM_B1 ** ADAM_STEP
    c2 = 1.0 - ADAM_B2 ** ADAM_STEP
    shapes = [tuple(p if d == axis else s for d, s in enumerate(w.shape)) for p in pieces]

    def body(q_ref, own_ref, p_ref, w_ref, m_ref, v_ref, *o_refs):
        g = None
        for k in range(4):
            term = jnp.where(q_ref[0] == k, own_ref[0], p_ref[k])
            g = term if g is None else g + term
        nm = ADAM_B1 * m_ref[...] + (1.0 - ADAM_B1) * g
        nv = ADAM_B2 * v_ref[...] + (1.0 - ADAM_B2) * (g * g)
        delta = -ADAM_LR * ((nm / c1) / (jnp.sqrt(nv / c2) + ADAM_EPS) + ADAM_WD * w_ref[...])
        for k, val in enumerate((g, delta, nm, nv)):
            off = 0
            for j, p in enumerate(pieces):
                o_refs[k * len(pieces) + j][...] = val[off:off + p] if axis == 0 else val[:, off:off + p]
                off += p

    whole = pl.BlockSpec(w.shape, lambda i, q: (0, 0))
    chip = (2 * lax.axis_index("x") + lax.axis_index("y")).astype(jnp.int32).reshape(1)
    outs = pl.pallas_call(
        body, name=name,
        grid_spec=pltpu.PrefetchScalarGridSpec(
            num_scalar_prefetch=1, grid=(1,),
            in_specs=[pl.BlockSpec((1,) + w.shape, lambda i, q: (q[0] if own.shape[0] > 1 else 0, 0, 0)),
                      pl.BlockSpec((4,) + w.shape, lambda i, q: (0, 0, 0)), whole, whole, whole],
            out_specs=[pl.BlockSpec(s, lambda i, q: (0, 0)) for s in shapes] * 4),
        out_shape=[jax.ShapeDtypeStruct(s, F32) for s in shapes] * 4, compiler_params=_ARB,
    )(chip, own, parts, w, m, v)
    return [outs[k * len(pieces):(k + 1) * len(pieces)] for k in range(4)]


def kernel(x, ln_in_g, ln_in_b, w_in, b_in, sinks, vn_g, vn_b, w_s, b_s, p_a, p_b, w_out, b_out, ln_g, ln_b, loss_target, m_ln_in_g, m_ln_in_b, m_w_in, m_b_in, m_sinks, m_vn_g, m_vn_b, m_w_s, m_b_s, m_p_a, m_p_b, m_w_out, m_b_out, m_ln_g, m_ln_b, v_ln_in_g, v_ln_in_b, v_w_in, v_b_in, v_sinks, v_vn_g, v_vn_b, v_w_s, v_b_s, v_p_a, v_p_b, v_w_out, v_b_out, v_ln_g, v_ln_b):
    nseq, seq, _ = x.shape
    t = nseq * seq
    nblk_seq = seq // BLK
    x2 = x.reshape(t, D)
    tgt = loss_target.reshape(t, D)

    def turned(a):
        return jnp.swapaxes(a, 1, 2)

    w_in_t = turned(w_in)

    def blocks(l):
        return [w_in_t[l].astype(BF16), p_a[l].astype(BF16), p_b[l].astype(BF16), w_out[l].astype(BF16)]

    def full_weights(g):
        w_t_full = g[0].reshape(IN_COLS, D)
        wo_full = g[3].reshape(D, D)
        return dict(w_t=w_t_full, pa=g[1], pb=g[2], wo=wo_full)

    def landing(bs):
        return [lax.empty((N_DEV,) + v.shape, v.dtype) for v in bs]

    blocks0 = blocks(0)
    a_send, a_recv, a_thru, a_token = _split_start(blocks0[:1], landing(blocks0[:1]), _plan_near, 5,
                                                   "allgather_w_in0_start")
    rest0 = [b + a_token[0, 0].astype(BF16) for b in blocks0[1:]]
    b_send, b_recv, b_thru, b_token = _split_start(rest0, landing(rest0), _plan_all, 8, "allgather_rest0_start")
    xs = [_ln_fwd(x2, ln_in_g + b_token[0, 0], ln_in_b, "ln_in_fwd")]
    _, landed = _split_wait(a_send, a_recv, a_thru, _plan_near, xs[0], "allgather_w_in0_wait")
    gathered0 = list(_forward_sibling(landed, "allgather_w_in0_forward"))
    blocks1, gathered0 = lax.optimization_barrier((blocks(1), gathered0))
    ag_send, ag_recv, ag_thru, ag_token = _split_start(blocks1, landing(blocks1), _plan_all, 8,
                                                       "allgather_weights1_start")
    weights = [None, None]
    bsb = jnp.broadcast_to(b_s[:, :, :, None], (DEPTH, 4, BLK, BLK))
    bias = _band_bias()

    saved = []
    for l in range(DEPTH):
        if l == 1:
            _, landed = _split_wait(ag_send, ag_recv, ag_thru, _plan_all, xs[1], "allgather_weights1_wait")
            weights[1] = full_weights(landed)
        w_t = weights[l]["w_t"] if l else gathered0[0].reshape(IN_COLS, D)
        last = l == DEPTH - 1
        b_l = b_in[l].reshape(1, -1) + (ag_token[0, 0] if l == 0 else 0.0)
        hm, hr = _inproj(xs[l], w_t, b_l, f"inproj{l}")
        ya, yb, prob, psink = _mixer_fwd(hm, sinks[l], bias, vn_g[l].reshape(1, -1), vn_b[l].reshape(1, -1),
                                         w_s[l], bsb[l], nblk_seq, f"mixer_fwd{l}")
        if l == 0:
            _, landed = _split_wait(b_send, b_recv, b_thru, _plan_all, ya, "allgather_rest0_wait")
            weights[0] = full_weights(gathered0 + landed)
        wl = weights[l]
        outs = _tail_fwd(xs[l], ya, yb, hr, wl["pa"], wl["pb"], wl["wo"], b_out[l].reshape(1, D),
                         ln_g[l].reshape(1, D), ln_b[l].reshape(1, D), f"tail_fwd{l}", last)
        saved.append((hm, hr, ya, yb, prob, psink) + tuple(outs[:4]))
        if not last:
            xs.append(outs[4])

    small = {n: [None] * DEPTH for n in _SMALL_ROWS + _SMALL_LANES}

    def pack_rows(d):
        return _pack_small(d, _SMALL_ROWS)

    def pack_lanes(d):
        return jnp.concatenate([d[n] for n in _SMALL_LANES], axis=1)
    names = ("w_in", "p_a", "p_b", "w_out")
    token = jnp.zeros((8, 128), F32)
    dx = tgt
    split = [None] * DEPTH
    for l in reversed(range(DEPTH)):
        hm, hr, ya, yb, prob, psink, pa, pb, merged, z = saved[l]
        wl = weights[l]
        dz, dpa, dpb, dhr, dya, dyb, acc, gbr = _tail_bwd(
            dx, z, pa, pb, hr, wl["wo"], wl["pa"], wl["pb"], ln_g[l].reshape(1, D) + token[0, 0],
            ln_b[l].reshape(1, D), f"tail_bwd{l}", l == DEPTH - 1)
        if l == DEPTH - 1:
            sq_err = acc[3:4, 0:128]
        dhm, gbm, gsk, gvn, gws, gbs = _mixer_bwd(
            hm, dya, dyb, prob, psink, vn_g[l].reshape(1, -1), vn_b[l].reshape(1, -1), w_s[l], bsb[l],
            f"mixer_bwd{l}")
        grads = {"w_in": _wgrad(dhr, xs[l], R_W // 2, f"wgrad_in_route{l}",
                                under=_wgrad(dhm, xs[l], MAIN_W // 2, f"wgrad_in_main{l}", rows=IN_COLS)),
                 "p_a": _wgrad(ya, dpa, Q_W, f"wgrad_pa{l}", by_owner=True),
                 "p_b": _wgrad(yb, dpb, SGU_W, f"wgrad_pb{l}", by_owner=True),
                 "w_out": _wgrad(merged, dz, D, f"wgrad_out{l}")}
        small["b_in"][l] = jnp.concatenate([gbm[0], gbr[0]])
        small["sinks"][l] = gsk[:, 0]
        small["vn_g"][l], small["vn_b"][l] = gvn[0], gvn[1]
        small["w_s"][l], small["b_s"][l] = gws, gbs[:, :, 0]
        small["ln_g"][l], small["ln_b"][l], small["b_out"][l] = acc[0], acc[1], acc[2]
        parts = [grads[n].reshape((4, 2, -1, grads[n].shape[-1])) for n in names]
        if l == 0:
            stacked = {n: jnp.stack(v) for n, v in small.items()}
            for packed in (pack_rows(stacked), pack_lanes(stacked)):
                parts.append(jnp.broadcast_to(packed[None, None], (1, 2) + packed.shape))
        if l == 0:
            from_sib = _swap_sibling(parts, f"rs_sibling{l}")
        else:
            halves = [p.reshape((N_DEV,) + p.shape[2:]) for p in parts]
            sib = _split_start(halves, [lax.empty((4,) + p.shape[2:], p.dtype) for p in parts], _plan_sibling, 4,
                               f"rs_sibling{l}_start")
            dx = _dx_inproj(dz, dhm, dhr, wl["w_t"], sib[3], f"dx_inproj{l}")
            halves, from_sib = _split_wait(sib[0], sib[1], sib[2], _plan_sibling, dx, f"rs_sibling{l}_wait")
            parts = [h.reshape(p.shape) for h, p in zip(halves, parts)]
        pair = list(_pair_sums(parts, from_sib, f"pair_sums{l}"))
        lands = [lax.empty((4,) + p.shape[1:], p.dtype) for p in pair]
        split[l] = _split_start(pair, lands, _plan_chips, 3, f"rs_chips{l}_start")
        token = split[l][3]
        if l == 0:
            grad_x, acc_in = _dx_inproj(dz, dhm, dhr, wl["w_t"], token, f"dx_inproj{l}",
                                        ln_in=(x2, ln_in_g.reshape(1, D)))
    last = [acc_in, jnp.broadcast_to(sq_err, (8, 128))]
    ln_send, ln_recv, ln_thru, ln_token = _split_start(last, landing(last), _plan_all, 8, "allgather_ln_in_start")

    given = {"w_in": (w_in_t, turned(m_w_in), turned(v_w_in)), "p_a": (p_a, m_p_a, v_p_a),
             "p_b": (p_b, m_p_b, v_p_b), "w_out": (w_out, m_w_out, v_w_out)}
    waited = [_split_wait(split[l][0], split[l][1], split[l][2], _plan_chips, ln_token, f"rs_chips{l}_wait")
              for l in range(DEPTH)]
    res = {}
    for a, n in enumerate(names):
        rows, lanes = waited[0][1][a].shape[1:]
        outs = _adamw([waited[l][1][a] for l in range(DEPTH)], *[v.reshape(DEPTH * rows, lanes) for v in given[n]],
                      f"adamw_{n}", own=[waited[l][0][a] for l in range(DEPTH)])
        res[n] = [o.reshape(given[n][0].shape) for o in outs]
    res["w_in"] = [turned(o) for o in res["w_in"]]

    w_small = dict(ln_in_g=ln_in_g, ln_in_b=ln_in_b, b_in=b_in, sinks=sinks, vn_g=vn_g, vn_b=vn_b, w_s=w_s, b_s=b_s,
                   b_out=b_out, ln_g=ln_g, ln_b=ln_b)
    m_small = dict(ln_in_g=m_ln_in_g, ln_in_b=m_ln_in_b, b_in=m_b_in, sinks=m_sinks, vn_g=m_vn_g, vn_b=m_vn_b,
                   w_s=m_w_s, b_s=m_b_s, b_out=m_b_out, ln_g=m_ln_g, ln_b=m_ln_b)
    v_small = dict(ln_in_g=v_ln_in_g, ln_in_b=v_ln_in_b, b_in=v_b_in, sinks=v_sinks, vn_g=v_vn_g, vn_b=v_vn_b,
                   w_s=v_w_s, b_s=v_b_s, b_out=v_b_out, ln_g=v_ln_g, ln_b=v_ln_b)
    by_rows = _adamw_small(waited[0][1][4], waited[0][0][4], *[pack_rows(d) for d in (w_small, m_small, v_small)],
                           [_tile_rows(w_small[n]) for n in _SMALL_ROWS], 0, "adamw_small_rows")
    by_lanes = _adamw_small(waited[0][1][5], waited[0][0][5], *[pack_lanes(d) for d in (w_small, m_small, v_small)],
                            [w_small[n].shape[1] for n in _SMALL_LANES], 1, "adamw_small_lanes")
    _, (all_in, all_sq) = _split_wait(ln_send, ln_recv, ln_thru, _plan_all, by_lanes[0][0], "allgather_ln_in_wait")
    loss = jnp.sum(all_sq[:, 0, 0]) * (0.5 / D)
    outs_in = _adamw([all_in], *[jnp.pad(jnp.stack([d[n] for n in _SMALL_IN]), ((0, 6), (0, 0)))
                                 for d in (w_small, m_small, v_small)], "adamw_ln_in")
    for k in range(4):
        u = {n: o.reshape(-1)[:w_small[n].size].reshape(w_small[n].shape) for n, o in zip(_SMALL_ROWS, by_rows[k])}
        u.update(zip(_SMALL_LANES, by_lanes[k]))
        u.update({n: outs_in[k][r] for r, n in enumerate(_SMALL_IN)})
        for n in u:
            res.setdefault(n, [None] * 4)[k] = u[n]

    order = ("ln_in_g", "ln_in_b", "w_in", "b_in", "sinks", "vn_g", "vn_b", "w_s", "b_s", "p_a", "p_b", "w_out",
             "b_out", "ln_g", "ln_b")
    return (loss, grad_x.reshape(x.shape), *[res[n][0] for n in order], *[res[n][1] for n in order],
            *[res[n][2] for n in order], *[res[n][3] for n in order])
```
